```python
import math
import jax
import jax.numpy as jnp
from jax import lax
import numpy as np

D_MODEL = 2048
BATCH = 1
SEQ = 8192
DEPTH = 1
DEC_BATCH = 32
DEC_SEQ = 1
PAST_LEN = 8192
PAGE_SIZE = 128

A_HEADS = 8
A_DK = 128
A_DV = 128
A_WIDTH = A_HEADS * A_DK
A_CHUNK = 64
B_HEADS = 8
B_KV_HEADS = 2
B_GROUP = B_HEADS // B_KV_HEADS
B_HEAD_DIM = 128
B_WIDTH = B_HEADS * B_HEAD_DIM
IDX_HEADS = 16
IDX_DIM = 64
TOPK_MAX = 256
Q_BLOCK = 128
REL_BUCKETS = 32
REL_MAX_DIST = 128
MEM_LEN = 256
X_HEADS = 4
X_HEAD_DIM = 128
N_GROUPS = 4
EXP_PER_GROUP = 8
N_EXPERTS = N_GROUPS * EXP_PER_GROUP
EXPERT_TOPK = 2
EXPERT_FF = 512
MOE_BLOCK = 128
EPS = 1e-6
IN_SIZES = (A_WIDTH, A_WIDTH, A_HEADS * A_DV, A_HEADS * A_DV,
            B_WIDTH, B_KV_HEADS * B_HEAD_DIM, B_KV_HEADS * B_HEAD_DIM,
            IDX_HEADS * IDX_DIM, IDX_HEADS, IDX_DIM, D_MODEL, D_MODEL)
IN_COLS = sum(IN_SIZES)

kernel_name = 'hgrn2_dsa_hmoe_hybrid_step'


def rmsnorm(x, g):
    xf = x.astype(jnp.float32)
    y = xf * lax.rsqrt(jnp.mean(xf * xf, axis=-1, keepdims=True) + EPS)
    return (y * g.astype(jnp.float32)).astype(x.dtype)


def split_in(z):
    offs, acc = [], 0
    for s in IN_SIZES[:-1]:
        acc += s
        offs.append(acc)
    return jnp.split(z, offs, axis=-1)


def hgrn_lower_bounds(lb_logits):
    p = jax.nn.softmax(lb_logits.astype(jnp.float32), axis=0)
    return jnp.cumsum(p, axis=0)[:DEPTH]


def hgrn2_recurrence(q, k, v, log_f, s0):
    b, t, h, dk = q.shape
    dv = v.shape[-1]
    c = min(A_CHUNK, t)
    n = -(-t // c)
    pad = n * c - t

    def prep(a):
        a = jnp.pad(a, ((0, 0), (0, pad), (0, 0), (0, 0)))
        return a.reshape(b, n, c, h, a.shape[-1]).transpose(1, 0, 3, 2, 4)

    qc, kc, vc, gc = prep(q), prep(k), prep(v), prep(log_f)
    causal = jnp.tril(jnp.ones((c, c), bool))

    def step(s, inp):
        qb, kb, vb, gb = inp
        g = jnp.cumsum(gb, axis=2)
        diff = g[:, :, :, None, :] - g[:, :, None, :, :]
        decay = jnp.exp(jnp.where(causal[:, :, None], diff, -jnp.inf))
        att = jnp.einsum('bhtk,bhsk,bhtsk->bhts', qb, kb, decay)
        o = jnp.einsum('bhts,bhsv->bhtv', att, vb) + jnp.einsum('bhtk,bhkv->bhtv', qb * jnp.exp(g), s)
        g_last = g[:, :, -1:, :]
        s_new = jnp.exp(g_last[:, :, 0, :])[..., None] * s + jnp.einsum('bhsk,bhsv->bhkv', kb * jnp.exp(g_last - g), vb)
        return s_new, o

    s_t, oc = lax.scan(step, s0, (qc, kc, vc, gc))
    o = oc.transpose(1, 0, 3, 2, 4).reshape(b, n * c, h, dv)[:, :t]
    return o, s_t


def hgrn2_branch(aq, af, ai, ag, lb, norm_g, s0):
    b, t, _ = aq.shape
    q = jax.nn.silu(aq.astype(jnp.float32)).reshape(b, t, A_HEADS, A_DK)
    f = lb + (1.0 - lb) * jax.nn.sigmoid(af.astype(jnp.float32))
    k = (1.0 - f).reshape(b, t, A_HEADS, A_DK)
    log_f = jnp.log(f).reshape(b, t, A_HEADS, A_DK)
    v = ai.astype(jnp.float32).reshape(b, t, A_HEADS, A_DV)
    o, s_t = hgrn2_recurrence(q, k, v, log_f, s0.astype(jnp.float32))
    o = rmsnorm(o, norm_g) * jax.nn.silu(ag.astype(jnp.float32)).reshape(b, t, A_HEADS, A_DV)
    return o.reshape(b, t, A_HEADS * A_DV).astype(aq.dtype), s_t


def t5_bucket(dist):
    max_exact = REL_BUCKETS // 2
    dist_f = jnp.maximum(dist, 1).astype(jnp.float32)
    large = max_exact + (jnp.log(dist_f / max_exact) / math.log(REL_MAX_DIST / max_exact)
                         * (REL_BUCKETS - max_exact)).astype(jnp.int32)
    large = jnp.minimum(large, REL_BUCKETS - 1)
    return jnp.where(dist < max_exact, dist, large)


def dsa_attention(q, q_idx, w_idx, k_idx_all, q_pos, gather_kv, rel_bias):
    b, t = q.shape[:2]
    n_keys = k_idx_all.shape[1]
    topk = min(TOPK_MAX, n_keys // 4)
    qb = Q_BLOCK if t % Q_BLOCK == 0 else t
    nb = t // qb
    key_pos = jnp.arange(n_keys, dtype=jnp.int32)
    kif = k_idx_all.astype(jnp.float32)
    bias_tab = rel_bias.astype(jnp.float32)

    def blk(a):
        return a.reshape((b, nb, qb) + a.shape[2:]).swapaxes(0, 1)

    def one_block(args):
        qq, qi, wi, pos = args
        s = jax.nn.relu(jnp.einsum('bqhd,bld->bqhl', qi.astype(jnp.float32), kif) * IDX_DIM ** -0.5)
        score = jnp.einsum('bqhl,bqh->bql', s, wi.astype(jnp.float32) * IDX_HEADS ** -0.5)
        admissible = key_pos[None, :] <= pos[:, None]
        score = jnp.where(admissible[None], score, -jnp.inf)
        _, idx = lax.top_k(score, topk)
        k_sel, v_sel = gather_kv(idx)
        dist = pos[None, :, None] - idx
        valid = dist >= 0
        bias = bias_tab[t5_bucket(jnp.maximum(dist, 0))]
        bias = bias.reshape(b, qb, topk, B_KV_HEADS, B_GROUP).transpose(0, 1, 3, 4, 2)
        qg = qq.reshape(b, qb, B_KV_HEADS, B_GROUP, B_HEAD_DIM).astype(jnp.float32)
        logits = jnp.einsum('bqngd,bqknd->bqngk', qg, k_sel.astype(jnp.float32)) * B_HEAD_DIM ** -0.5 + bias
        logits = jnp.where(valid[:, :, None, None, :], logits, -jnp.inf)
        p = jax.nn.softmax(logits, axis=-1)
        o = jnp.einsum('bqngk,bqknd->bqngd', p, v_sel.astype(jnp.float32))
        return o.reshape(b, qb, B_HEADS, B_HEAD_DIM).astype(q.dtype)

    out = lax.map(one_block, (blk(q), blk(q_idx), blk(w_idx), q_pos.reshape(nb, qb)))
    return out.swapaxes(0, 1).reshape(b, t, B_HEADS, B_HEAD_DIM)


def cross_attend(h, w_q, mem_k, mem_v, w_o):
    b, t, _ = h.shape
    q = (h @ w_q).reshape(b, t, X_HEADS, X_HEAD_DIM)
    logits = jnp.einsum('bthd,bmhd->bhtm', q.astype(jnp.float32), mem_k.astype(jnp.float32)) * X_HEAD_DIM ** -0.5
    p = jax.nn.softmax(logits, axis=-1)
    o = jnp.einsum('bhtm,bmhd->bthd', p, mem_v.astype(jnp.float32)).astype(h.dtype)
    return o.reshape(b, t, X_HEADS * X_HEAD_DIM) @ w_o


def hier_moe(h, w_rg, b_rg, w_re, b_re, w_g, w_u, w_d):
    n, d = h.shape
    hf = h.astype(jnp.float32)
    g_logits = hf @ w_rg.astype(jnp.float32) + b_rg.astype(jnp.float32)
    g_prob = jax.nn.softmax(g_logits, axis=-1)
    grp = jnp.argmax(g_logits, axis=-1).astype(jnp.int32)
    p_grp = jnp.take_along_axis(g_prob, grp[:, None], axis=-1)
    e_logits = (hf @ w_re.astype(jnp.float32) + b_re.astype(jnp.float32)).reshape(n, N_GROUPS, EXP_PER_GROUP)
    e_logits = jnp.take_along_axis(e_logits, grp[:, None, None], axis=1)[:, 0]
    top_v, top_i = lax.top_k(e_logits, EXPERT_TOPK)
    gates = p_grp * jax.nn.softmax(top_v, axis=-1)
    eid = (grp[:, None] * EXP_PER_GROUP + top_i).reshape(-1)
    a = n * EXPERT_TOPK
    blk = min(MOE_BLOCK, max(8, a // N_EXPERTS))
    n_blocks = -(-(a + N_EXPERTS * (blk - 1)) // blk)
    rows = n_blocks * blk
    tok = jnp.arange(a, dtype=jnp.int32) // EXPERT_TOPK
    order = jnp.argsort(eid)
    e_sorted = eid[order]
    counts = jnp.bincount(eid, length=N_EXPERTS)
    starts = jnp.cumsum(counts) - counts
    padded = (counts + blk - 1) // blk * blk
    pad_end = jnp.cumsum(padded)
    pad_start = pad_end - padded
    dest_sorted = pad_start[e_sorted] + jnp.arange(a, dtype=jnp.int32) - starts[e_sorted]
    dest = jnp.zeros((a,), jnp.int32).at[order].set(dest_sorted.astype(jnp.int32))
    slot_tok = jnp.full((rows,), n, jnp.int32).at[dest].set(tok)
    h_pad = jnp.concatenate([h, jnp.zeros((1, d), h.dtype)], axis=0)
    xb = h_pad[slot_tok].reshape(n_blocks, blk, d)
    block_e = jnp.minimum(jnp.searchsorted(pad_end, jnp.arange(n_blocks) * blk, side='right'), N_EXPERTS - 1)

    def expert_block(args):
        xe, e = args
        return (jax.nn.silu(xe @ w_g[e]) * (xe @ w_u[e])) @ w_d[e]

    yb = lax.map(expert_block, (xb, block_e)).reshape(rows, d)
    return (yb[dest].reshape(n, EXPERT_TOPK, d) * gates[..., None].astype(h.dtype)).sum(axis=1)


def setup_inputs(seed: int = 0) -> dict:
    key = jax.random.key(seed)
    ks = iter(jax.random.split(key, 48))

    def nrm(shape, scale):
        return jax.random.normal(next(ks), shape, jnp.float32) * scale

    n_pages = PAST_LEN // PAGE_SIZE
    n_used = DEC_BATCH * n_pages
    n_pool = n_used + max(1, n_used // 4)
    page_table = jax.random.permutation(next(ks), n_pool)[:n_used].reshape(DEC_BATCH, n_pages).astype(jnp.int32)
    d_in = D_MODEL ** -0.5
    return {
        'x_prompt': nrm((BATCH, SEQ, D_MODEL), 1.0),
        'x_sample': nrm((DEC_BATCH, DEC_SEQ, D_MODEL), 1.0),
        'mem_prompt': nrm((BATCH, MEM_LEN, D_MODEL), 1.0),
        'cache_k': nrm((DEPTH, n_pool, PAGE_SIZE, B_KV_HEADS, B_HEAD_DIM), 1.0),
        'cache_v': nrm((DEPTH, n_pool, PAGE_SIZE, B_KV_HEADS, B_HEAD_DIM), 1.0),
        'cache_kidx': nrm((DEPTH, n_pool, PAGE_SIZE, IDX_DIM), 1.0),
        'page_table': page_table,
        'state_hgrn': nrm((DEPTH, DEC_BATCH, A_HEADS, A_DK, A_DV), 0.5),
        'cache_mem_k': nrm((DEPTH, DEC_BATCH, MEM_LEN, X_HEADS, X_HEAD_DIM), 1.0),
        'cache_mem_v': nrm((DEPTH, DEC_BATCH, MEM_LEN, X_HEADS, X_HEAD_DIM), 1.0),
        'norm_mix': 1.0 + nrm((DEPTH, D_MODEL), 0.02),
        'w_in': nrm((DEPTH, D_MODEL, IN_COLS), d_in),
        'hgrn_lb_logits': nrm((DEPTH + 1, A_WIDTH), 0.1),
        'hgrn_norm': 1.0 + nrm((DEPTH, A_DV), 0.02),
        'w_branch_a': nrm((DEPTH, A_HEADS * A_DV, D_MODEL), (A_HEADS * A_DV) ** -0.5),
        'w_branch_b': nrm((DEPTH, B_WIDTH, D_MODEL), B_WIDTH ** -0.5),
        'w_out': nrm((DEPTH, D_MODEL, D_MODEL), d_in),
        'norm_cross': 1.0 + nrm((DEPTH, D_MODEL), 0.02),
        'w_xq': nrm((DEPTH, D_MODEL, X_HEADS * X_HEAD_DIM), d_in),
        'w_xk': nrm((DEPTH, D_MODEL, X_HEADS * X_HEAD_DIM), d_in),
        'w_xv': nrm((DEPTH, D_MODEL, X_HEADS * X_HEAD_DIM), d_in),
        'w_xo': nrm((DEPTH, X_HEADS * X_HEAD_DIM, D_MODEL), (X_HEADS * X_HEAD_DIM) ** -0.5),
        'norm_ffn': 1.0 + nrm((DEPTH, D_MODEL), 0.02),
        'w_router_group': nrm((DEPTH, D_MODEL, N_GROUPS), d_in),
        'b_router_group': nrm((DEPTH, N_GROUPS), 0.01),
        'w_router_expert': nrm((DEPTH, D_MODEL, N_EXPERTS), d_in),
        'b_router_expert': nrm((DEPTH, N_EXPERTS), 0.01),
        'w_exp_gate': nrm((DEPTH, N_EXPERTS, D_MODEL, EXPERT_FF), d_in),
        'w_exp_up': nrm((DEPTH, N_EXPERTS, D_MODEL, EXPERT_FF), d_in),
        'w_exp_down': nrm((DEPTH, N_EXPERTS, EXPERT_FF, D_MODEL), EXPERT_FF ** -0.5),
        'rel_bias': nrm((REL_BUCKETS, B_HEADS), 0.2),
        'norm_final': 1.0 + nrm((D_MODEL,), 0.02),
    }


def reference(x_prompt, x_sample, mem_prompt, cache_k, cache_v, cache_kidx, page_table, state_hgrn,
              cache_mem_k, cache_mem_v, norm_mix, w_in, hgrn_lb_logits, hgrn_norm, w_branch_a, w_branch_b,
              w_out, norm_cross, w_xq, w_xk, w_xv, w_xo, norm_ffn, w_router_group, b_router_group,
              w_router_expert, b_router_expert, w_exp_gate, w_exp_up, w_exp_down, rel_bias, norm_final):
    lbs = hgrn_lower_bounds(hgrn_lb_logits)

    def layer(x, l, pos0, s0, keys_fn, mem_k, mem_v):
        b, t, _ = x.shape
        h = rmsnorm(x, norm_mix[l])
        aq, af, ai, ag, bq, bk, bv, iq, iw, ik, ga, gb = split_in(h @ w_in[l])
        ya, s_new = hgrn2_branch(aq, af, ai, ag, lbs[l], hgrn_norm[l], s0)
        k_new = bk.reshape(b, t, B_KV_HEADS, B_HEAD_DIM)
        v_new = bv.reshape(b, t, B_KV_HEADS, B_HEAD_DIM)
        kidx_all, gather_kv = keys_fn(l, k_new, v_new, ik)
        q_pos = pos0 + jnp.arange(t, dtype=jnp.int32)
        yb = dsa_attention(bq.reshape(b, t, B_HEADS, B_HEAD_DIM), iq.reshape(b, t, IDX_HEADS, IDX_DIM), iw,
                           kidx_all, q_pos, gather_kv, rel_bias)
        merged = (jax.nn.sigmoid(ga) * (ya @ w_branch_a[l])
                  + jax.nn.sigmoid(gb) * (yb.reshape(b, t, B_WIDTH) @ w_branch_b[l]))
        x = x + merged @ w_out[l]
        x = x + cross_attend(rmsnorm(x, norm_cross[l]), w_xq[l], mem_k, mem_v, w_xo[l])
        hf = rmsnorm(x, norm_ffn[l]).reshape(b * t, D_MODEL)
        x = x + hier_moe(hf, w_router_group[l], b_router_group[l], w_router_expert[l], b_router_expert[l],
                         w_exp_gate[l], w_exp_up[l], w_exp_down[l]).reshape(b, t, D_MODEL)
        return x, k_new, v_new, ik, s_new

    def prompt_keys(l, k_new, v_new, kidx_new):
        bi = jnp.arange(k_new.shape[0])[:, None, None]

        def gather(idx):
            return k_new[bi, idx], v_new[bi, idx]
        return kidx_new, gather

    def sample_keys(l, k_new, v_new, kidx_new):
        db, t = k_new.shape[:2]
        kidx_past = cache_kidx[l, page_table].reshape(db, PAST_LEN, IDX_DIM)
        kidx_all = jnp.concatenate([kidx_past, kidx_new.astype(kidx_past.dtype)], axis=1)
        bi = jnp.arange(db)[:, None, None]

        def gather(idx):
            pidx = jnp.minimum(idx, PAST_LEN - 1)
            phys = page_table[bi, pidx // PAGE_SIZE]
            off = pidx % PAGE_SIZE
            nidx = jnp.clip(idx - PAST_LEN, 0, t - 1)
            is_past = (idx < PAST_LEN)[..., None, None]
            k_sel = jnp.where(is_past, cache_k[l, phys, off], k_new[bi, nidx].astype(cache_k.dtype))
            v_sel = jnp.where(is_past, cache_v[l, phys, off], v_new[bi, nidx].astype(cache_v.dtype))
            return k_sel, v_sel
        return kidx_all, gather

    bp, mlen = mem_prompt.shape[0], mem_prompt.shape[1]
    xp = x_prompt
    kp_l, vp_l, ip_l, sp_l, mk_l, mv_l = [], [], [], [], [], []
    for l in range(DEPTH):
        mk = (mem_prompt @ w_xk[l]).reshape(bp, mlen, X_HEADS, X_HEAD_DIM)
        mv = (mem_prompt @ w_xv[l]).reshape(bp, mlen, X_HEADS, X_HEAD_DIM)
        s0 = jnp.zeros((bp, A_HEADS, A_DK, A_DV), jnp.float32)
        xp, kp, vp, ip, sp = layer(xp, l, 0, s0, prompt_keys, mk, mv)
        kp_l.append(kp)
        vp_l.append(vp)
        ip_l.append(ip)
        sp_l.append(sp.astype(x_prompt.dtype))
        mk_l.append(mk)
        mv_l.append(mv)
    y_prompt = rmsnorm(xp, norm_final)

    xs = x_sample
    ks_l, vs_l, is_l, ss_l = [], [], [], []
    for l in range(DEPTH):
        xs, ks_, vs_, is_, ss_ = layer(xs, l, PAST_LEN, state_hgrn[l], sample_keys, cache_mem_k[l], cache_mem_v[l])
        ks_l.append(ks_)
        vs_l.append(vs_)
        is_l.append(is_)
        ss_l.append(ss_.astype(state_hgrn.dtype))
    y_sample = rmsnorm(xs, norm_final)

    return (y_prompt, y_sample,
            jnp.stack(kp_l), jnp.stack(vp_l), jnp.stack(ip_l), jnp.stack(sp_l), jnp.stack(mk_l), jnp.stack(mv_l),
            jnp.stack(ks_l), jnp.stack(vs_l), jnp.stack(is_l), jnp.stack(ss_l))
```

```python
import functools
import math

import jax
import jax.numpy as jnp
import numpy as np
from jax import lax
from jax.experimental import pallas as pl
from jax.experimental.pallas import tpu as pltpu

F32 = jnp.float32
BF16 = jnp.bfloat16
EPS = 1e-6

D_MODEL = 2048
A_HEADS, A_DK, A_DV = 8, 128, 128
B_HEADS, B_KV_HEADS, B_HEAD_DIM = 8, 2, 128
B_GROUP = B_HEADS // B_KV_HEADS
IDX_HEADS, IDX_DIM = 16, 64
TOPK_MAX = 256
PAGE_SIZE = 128
REL_BUCKETS, REL_MAX_DIST = 32, 128
X_HEADS, X_HEAD_DIM = 4, 128
N_GROUPS, EXP_PER_GROUP = 4, 8
N_EXPERTS = N_GROUPS * EXP_PER_GROUP
EXPERT_TOPK = 2
EXPERT_FF = 512
MOE_BLOCK = 128

LANES = 128
VMEM_LIMIT = 56 * 1024 * 1024

NEG = -1e30

C_AQ, C_AF, C_AI, C_AG, C_BQ, C_IQ, C_GA, C_GB, C_BK, C_BV = (
    0, 1024, 2048, 3072, 4096, 5120, 6144, 8192, 10240, 10496)
NZ = 10752


def _cparams(*sem):
    return pltpu.CompilerParams(dimension_semantics=sem, vmem_limit_bytes=VMEM_LIMIT)


def _silu(x):
    return x * jax.nn.sigmoid(x)


def _nt(a, b):
    return lax.dot_general(a, b, (((1,), (1,)), ((), ())), preferred_element_type=F32)


def _nt_f32(a, b):
    return lax.dot_general(a, b, (((1,), (1,)), ((), ())), preferred_element_type=F32,
                           precision=lax.Precision.HIGHEST)


def _mm(a, w):
    if w.dtype == F32:
        return jnp.dot(a.astype(F32), w, preferred_element_type=F32, precision=lax.Precision.HIGHEST)
    return jnp.dot(a.astype(BF16), w, preferred_element_type=F32)


def _norm_matmul_kernel(x_ref, g_ref, w_ref, o_ref, h_ref):
    @pl.when(pl.program_id(1) == 0)
    def _():
        x = x_ref[...]
        ms = jnp.mean(x * x, axis=-1, keepdims=True)
        h_ref[...] = (x * lax.rsqrt(ms + EPS) * g_ref[...]).astype(h_ref.dtype)

    o_ref[...] = _mm(h_ref[...], w_ref[...])


def norm_matmul(x, g, w, *, tm, tn):
    m, k = x.shape
    n = w.shape[1]
    return pl.pallas_call(
        _norm_matmul_kernel,
        grid=(m // tm, pl.cdiv(n, tn)),
        in_specs=[pl.BlockSpec((tm, k), lambda i, j: (i, 0)),
                  pl.BlockSpec((1, k), lambda i, j: (0, 0)),
                  pl.BlockSpec((k, tn), lambda i, j: (0, j))],
        out_specs=pl.BlockSpec((tm, tn), lambda i, j: (i, j)),
        out_shape=jax.ShapeDtypeStruct((m, n), F32),
        scratch_shapes=[pltpu.VMEM((tm, k), w.dtype)],
        compiler_params=_cparams("parallel", "arbitrary"),
        name="norm_matmul",
    )(x, g.reshape(1, k), w)


def _matmul_res_kernel(x_ref, w_ref, r_ref, o_ref):
    o_ref[...] = r_ref[...] + _mm(x_ref[...], w_ref[...])


def _matmul_kernel(x_ref, w_ref, o_ref):
    o_ref[...] = _mm(x_ref[...], w_ref[...])


def matmul(x, w, res=None, *, tm, tn):
    m, k = x.shape
    n = w.shape[1]
    in_specs = [pl.BlockSpec((tm, k), lambda i, j: (i, 0)),
                pl.BlockSpec((k, tn), lambda i, j: (0, j))]
    args = [x, w]
    kern = _matmul_kernel
    if res is not None:
        in_specs.append(pl.BlockSpec((tm, tn), lambda i, j: (i, j)))
        args.append(res)
        kern = _matmul_res_kernel
    return pl.pallas_call(
        kern,
        grid=(m // tm, n // tn),
        in_specs=in_specs,
        out_specs=pl.BlockSpec((tm, tn), lambda i, j: (i, j)),
        out_shape=jax.ShapeDtypeStruct((m, n), F32),
        compiler_params=_cparams("parallel", "arbitrary"),
        name="matmul",
    )(*args)


HG_TB = 128
HG_C = 16


def _hgrn_prompt_kernel(aq_ref, af_ref, ai_ref, ag_ref, lbl_ref, ng_ref, ya_ref, st_out_ref,
                        st_ref, q_s, k_s, g_s, v_s):
    t = pl.program_id(0)

    @pl.when(t == 0)
    def _():
        st_ref[...] = jnp.zeros_like(st_ref)

    lbl = lbl_ref[...]
    mx = jnp.max(lbl, axis=0, keepdims=True)
    ex = jnp.exp(lbl - mx)
    lb = ex[0:1, :] / jnp.sum(ex, axis=0, keepdims=True)

    f = lb + (1.0 - lb) * jax.nn.sigmoid(af_ref[...])
    logf = jnp.log(f)
    row = lax.broadcasted_iota(jnp.int32, (HG_TB, HG_TB), 0)
    col = lax.broadcasted_iota(jnp.int32, (HG_TB, HG_TB), 1)
    tri = jnp.where((row // HG_C == col // HG_C) & (col <= row), 1.0, 0.0).astype(F32)
    g_s[...] = jnp.dot(tri, logf, preferred_element_type=F32, precision=lax.Precision.HIGHEST)
    q_s[...] = _silu(aq_ref[...])
    k_s[...] = 1.0 - f
    v_s[...] = ai_ref[...]

    sub = lax.broadcasted_iota(jnp.int32, (HG_C, A_DK), 0)
    ng = ng_ref[...]

    def chunk(c, carry):
        r0 = pl.multiple_of(c * HG_C, HG_C)
        rows = pl.ds(r0, HG_C)
        for h in range(A_HEADS):
            cols = slice(h * A_DK, (h + 1) * A_DK)
            g = g_s[rows, cols]
            qh = q_s[rows, cols]
            kh = k_s[rows, cols]
            vh = v_s[rows, cols]
            o = jnp.zeros((HG_C, A_DV), F32)
            for tt in range(HG_C):
                d = g[tt:tt + 1, :] - g
                e = jnp.exp(jnp.where(sub <= tt, d, -jnp.inf))
                p = e * (qh[tt:tt + 1, :] * kh)
                a_col = jnp.sum(p, axis=1, keepdims=True)
                o_row = jnp.sum(a_col * vh, axis=0, keepdims=True)
                o = jnp.where(sub == tt, o_row, o)
            st = st_ref[h]
            qg = (qh * jnp.exp(g)).astype(BF16)
            o = o + _nt(qg, st.astype(BF16))
            g_last = g[HG_C - 1:HG_C, :]
            kt = (kh * jnp.exp(g_last - g)).astype(BF16)
            upd = lax.dot_general(vh.astype(BF16), kt, (((0,), (0,)), ((), ())),
                                  preferred_element_type=F32)
            st_ref[h] = st * jnp.exp(g_last) + upd
            on = o * lax.rsqrt(jnp.mean(o * o, axis=-1, keepdims=True) + EPS) * ng
            ya_ref[rows, cols] = (on * _silu(ag_ref[rows, cols])).astype(ya_ref.dtype)
        return carry

    lax.fori_loop(0, HG_TB // HG_C, chunk, 0)

    @pl.when(t == pl.num_programs(0) - 1)
    def _():
        st_out_ref[...] = st_ref[...]


def hgrn_prompt(z, lb_logits, norm_g):
    m = z.shape[0]
    w = A_HEADS * A_DK

    def zspec(cb):
        return pl.BlockSpec((HG_TB, w), lambda t, cb=cb: (t, cb))

    return pl.pallas_call(
        _hgrn_prompt_kernel,
        grid=(m // HG_TB,),
        in_specs=[zspec(C_AQ // w), zspec(C_AF // w), zspec(C_AI // w), zspec(C_AG // w),
                  pl.BlockSpec(lb_logits.shape, lambda t: (0, 0)),
                  pl.BlockSpec((1, A_DV), lambda t: (0, 0))],
        out_specs=[pl.BlockSpec((HG_TB, w), lambda t: (t, 0)),
                   pl.BlockSpec((A_HEADS, A_DV, A_DK), lambda t: (0, 0, 0))],
        out_shape=[jax.ShapeDtypeStruct((m, w), BF16),
                   jax.ShapeDtypeStruct((A_HEADS, A_DV, A_DK), F32)],
        scratch_shapes=[pltpu.VMEM((A_HEADS, A_DV, A_DK), F32)] + [pltpu.VMEM((HG_TB, w), F32)] * 4,
        compiler_params=_cparams("arbitrary"),
        name="hgrn_prompt",
    )(z, z, z, z, lb_logits, norm_g.reshape(1, A_DV))


def _hgrn_step_kernel(z_ref, lbl_ref, ng_ref, s_ref, ya_ref, s_out_ref):
    lbl = lbl_ref[...]
    mx = jnp.max(lbl, axis=0, keepdims=True)
    ex = jnp.exp(lbl - mx)
    lb = ex[0:1, :] / jnp.sum(ex, axis=0, keepdims=True)
    z = z_ref[0]
    w = A_HEADS * A_DK
    q = _silu(z[:, 0:w])
    f = lb + (1.0 - lb) * jax.nn.sigmoid(z[:, w:2 * w])
    kk = 1.0 - f
    v = z[:, 2 * w:3 * w]
    ag = z[:, 3 * w:4 * w]
    rows = []
    for h in range(A_HEADS):
        cols = slice(h * A_DK, (h + 1) * A_DK)
        rows += [f[:, cols], kk[:, cols], q[:, cols]]
    rows.append(jnp.zeros((LANES - 3 * A_HEADS, A_DK), F32))
    xt = jnp.concatenate(rows, axis=0).T
    ng = ng_ref[...]
    outs = []
    for h in range(A_HEADS):
        cols = slice(h * A_DV, (h + 1) * A_DV)
        fcol = xt[:, 3 * h:3 * h + 1]
        kcol = xt[:, 3 * h + 1:3 * h + 2]
        qcol = xt[:, 3 * h + 2:3 * h + 3]
        s_new = fcol * s_ref[0, h] + kcol * v[:, cols]
        s_out_ref[0, h] = s_new
        o = jnp.sum(qcol * s_new, axis=0, keepdims=True)
        on = o * lax.rsqrt(jnp.mean(o * o, axis=-1, keepdims=True) + EPS) * ng
        outs.append(on * _silu(ag[:, cols]))
    ya_ref[0] = jnp.concatenate(outs, axis=1).astype(ya_ref.dtype)


def hgrn_step(z4, lb_logits, norm_g, state):
    b = z4.shape[0]
    w = A_HEADS * A_DK
    return pl.pallas_call(
        _hgrn_step_kernel,
        grid=(b,),
        in_specs=[pl.BlockSpec((1, 1, 4 * w), lambda i: (i, 0, 0)),
                  pl.BlockSpec(lb_logits.shape, lambda i: (0, 0)),
                  pl.BlockSpec((1, A_DV), lambda i: (0, 0)),
                  pl.BlockSpec((1, A_HEADS, A_DK, A_DV), lambda i: (i, 0, 0, 0))],
        out_specs=[pl.BlockSpec((1, 1, w), lambda i: (i, 0, 0)),
                   pl.BlockSpec((1, A_HEADS, A_DK, A_DV), lambda i: (i, 0, 0, 0))],
        out_shape=[jax.ShapeDtypeStruct((b, 1, w), F32),
                   jax.ShapeDtypeStruct(state.shape, F32)],
        compiler_params=_cparams("arbitrary"),
        name="hgrn_step",
    )(z4.reshape(b, 1, 4 * w), lb_logits, norm_g.reshape(1, A_DV), state)


BISECT_MAX_ITERS = 48


def _bisect_threshold(count_ge, lo, hi, cnt_lo, topk):
    kf = float(topk)

    def cond(c):
        it, _, _, _, busy = c
        return jnp.logical_and(it < BISECT_MAX_ITERS, busy > 0.0)

    def body(c):
        it, lo, hi, cnt, _ = c
        mid = 0.5 * lo + 0.5 * hi
        cm = count_ge(mid)
        ge = cm >= kf
        lo = jnp.where(ge, mid, lo)
        cnt = jnp.where(ge, cm, cnt)
        hi = jnp.where(ge, hi, mid)
        busy = jnp.max(jnp.where(cnt > kf, 1.0, 0.0))
        return it + 1, lo, hi, cnt, busy

    busy0 = jnp.max(jnp.where(cnt_lo > kf, 1.0, 0.0))
    _, lo, _, _, _ = lax.while_loop(cond, body, (jnp.int32(0), lo, hi, cnt_lo, busy0))
    return lo


DSA_QB = 128
DSA_W = 512


def _dsa_prompt_kernel(iq_ref, bq_ref, iw_ref, kidx_ref, k_ref, v_ref, bias_ref, o_ref,
                       score_s, qih_s, qs_s, wb_s, m_s, l_s, acc_s, *, topk):
    i = pl.program_id(0)
    nsub = DSA_W // LANES
    nch = (i * DSA_QB + DSA_QB + DSA_W - 1) // DSA_W
    qpos = i * DSA_QB + lax.broadcasted_iota(jnp.int32, (DSA_QB, 1), 0)

    iw = iw_ref[...]
    wscale = IDX_DIM ** -0.5 * IDX_HEADS ** -0.5
    for h in range(IDX_HEADS):
        qih_s[h] = iq_ref[:, h * IDX_DIM:(h + 1) * IDX_DIM].astype(BF16)
        wb_s[h] = jnp.broadcast_to(iw[:, h:h + 1] * wscale, (DSA_QB, LANES))
    for h in range(B_HEADS):
        qs_s[h] = (bq_ref[:, h * B_HEAD_DIM:(h + 1) * B_HEAD_DIM] * B_HEAD_DIM ** -0.5).astype(BF16)

    def p1(c, carry):
        c0 = pl.multiple_of(c * DSA_W, DSA_W)
        kc = kidx_ref[pl.ds(c0, DSA_W), :]
        sc = [jnp.zeros((DSA_QB, LANES), F32) for _ in range(nsub)]
        for h in range(IDX_HEADS):
            s = jnp.maximum(_nt(qih_s[h], kc), 0.0)
            wb = wb_s[h]
            for j in range(nsub):
                sc[j] = sc[j] + s[:, j * LANES:(j + 1) * LANES] * wb
        for j in range(nsub):
            kpos = c0 + j * LANES + lax.broadcasted_iota(jnp.int32, (1, LANES), 1)
            score_s[c * nsub + j] = jnp.where(kpos <= qpos, sc[j], -jnp.inf)
        return carry

    lax.fori_loop(0, nch, p1, 0)

    def stats(c, carry):
        mn, mx = carry
        for j in range(nsub):
            s = score_s[c * nsub + j]
            mx = jnp.maximum(mx, s)
            mn = jnp.minimum(mn, jnp.where(s > -jnp.inf, s, jnp.inf))
        return mn, mx

    mn, mx = lax.fori_loop(0, nch, stats, (jnp.full((DSA_QB, LANES), jnp.inf, F32),
                                           jnp.full((DSA_QB, LANES), -jnp.inf, F32)))
    lo0 = jnp.min(mn, axis=1, keepdims=True)
    hi0 = jnp.max(mx, axis=1, keepdims=True)

    def count_ge(thr):
        thr_b = jnp.broadcast_to(thr, (DSA_QB, LANES))

        def body(c, acc):
            for j in range(nsub):
                s = score_s[c * nsub + j]
                acc = acc + jnp.where(s >= thr_b, 1.0, 0.0)
            return acc

        acc = lax.fori_loop(0, nch, body, jnp.zeros((DSA_QB, LANES), F32))
        return jnp.sum(acc, axis=1, keepdims=True)

    thr = _bisect_threshold(count_ge, lo0, hi0, (qpos + 1).astype(F32), topk)
    thr_b = jnp.broadcast_to(thr, (DSA_QB, LANES))

    m_s[...] = jnp.full(m_s.shape, NEG, F32)
    l_s[...] = jnp.zeros(l_s.shape, F32)
    acc_s[...] = jnp.zeros(acc_s.shape, F32)

    def p3(c, carry):
        c0 = pl.multiple_of(c * DSA_W, DSA_W)
        sel = [score_s[c * nsub + j] >= thr_b for j in range(nsub)]
        bidx = [jnp.clip(i - (c * nsub + j), 0, 2) for j in range(nsub)]
        for n in range(B_KV_HEADS):
            kn = k_ref[pl.ds(c0, DSA_W), n * B_HEAD_DIM:(n + 1) * B_HEAD_DIM]
            vn = v_ref[pl.ds(c0, DSA_W), n * B_HEAD_DIM:(n + 1) * B_HEAD_DIM]
            for gq in range(B_GROUP):
                h = n * B_GROUP + gq
                lg = _nt(qs_s[h], kn)
                lgs = [lg[:, j * LANES:(j + 1) * LANES] + bias_ref[h, bidx[j]] for j in range(nsub)]
                m_old = m_s[h]
                m_cur = m_old
                for j in range(nsub):
                    m_cur = jnp.maximum(m_cur, jnp.where(sel[j], lgs[j], NEG))
                m_new = jnp.max(m_cur, axis=1, keepdims=True)
                m_new = jnp.broadcast_to(m_new, (DSA_QB, LANES))
                ps = [jnp.where(sel[j], jnp.exp(lgs[j] - m_new), 0.0) for j in range(nsub)]
                psum = ps[0]
                for j in range(1, nsub):
                    psum = psum + ps[j]
                alpha = jnp.exp(m_old - m_new)
                l_s[h] = alpha * l_s[h] + jnp.broadcast_to(jnp.sum(psum, axis=1, keepdims=True), (DSA_QB, LANES))
                p = jnp.concatenate(ps, axis=1).astype(BF16)
                acc_s[h] = alpha * acc_s[h] + jnp.dot(p, vn, preferred_element_type=F32)
                m_s[h] = m_new
        return carry

    lax.fori_loop(0, nch, p3, 0)

    for h in range(B_HEADS):
        o_ref[:, h * B_HEAD_DIM:(h + 1) * B_HEAD_DIM] = (acc_s[h] / l_s[h]).astype(o_ref.dtype)


def dsa_prompt(z, zsmall, kidx_bf, k_bf, v_bf, bias_tiles):
    m = z.shape[0]
    topk = min(TOPK_MAX, m // 4)
    wq = B_HEADS * B_HEAD_DIM
    kern = functools.partial(_dsa_prompt_kernel, topk=topk)
    return pl.pallas_call(
        kern,
        grid=(m // DSA_QB,),
        in_specs=[pl.BlockSpec((DSA_QB, IDX_HEADS * IDX_DIM), lambda i: (i, C_IQ // (IDX_HEADS * IDX_DIM))),
                  pl.BlockSpec((DSA_QB, wq), lambda i: (i, C_BQ // wq)),
                  pl.BlockSpec((DSA_QB, LANES), lambda i: (i, 0)),
                  pl.BlockSpec(kidx_bf.shape, lambda i: (0, 0)),
                  pl.BlockSpec(k_bf.shape, lambda i: (0, 0)),
                  pl.BlockSpec(v_bf.shape, lambda i: (0, 0)),
                  pl.BlockSpec(bias_tiles.shape, lambda i: (0, 0, 0, 0))],
        out_specs=pl.BlockSpec((DSA_QB, wq), lambda i: (i, 0)),
        out_shape=jax.ShapeDtypeStruct((m, wq), BF16),
        scratch_shapes=[pltpu.VMEM((m // LANES, DSA_QB, LANES), F32),
                        pltpu.VMEM((IDX_HEADS, DSA_QB, IDX_DIM), BF16),
                        pltpu.VMEM((B_HEADS, DSA_QB, B_HEAD_DIM), BF16),
                        pltpu.VMEM((IDX_HEADS, DSA_QB, LANES), F32),
                        pltpu.VMEM((B_HEADS, DSA_QB, LANES), F32),
                        pltpu.VMEM((B_HEADS, DSA_QB, LANES), F32),
                        pltpu.VMEM((B_HEADS, DSA_QB, B_HEAD_DIM), F32)],
        compiler_params=_cparams("arbitrary"),
        name="dsa_prompt",
    )(z, z, zsmall, kidx_bf, k_bf, v_bf, bias_tiles)


def _page_copies(table_ref, b, n_pages, src_hbm, dst, sem):
    def copy(p):
        return pltpu.make_async_copy(src_hbm.at[table_ref[b, p]],
                                     dst.at[pl.ds(p * PAGE_SIZE, PAGE_SIZE)], sem)
    return copy


def _dsa_scores_kernel(pt_ref, iq_ref, iw_ref, iknew_ref, kidx_hbm, o_ref, buf, sem, *, n_pages):
    b = pl.program_id(0)
    nb = pl.num_programs(0)
    past = n_pages * PAGE_SIZE

    def start(bb, slot):
        cp = _page_copies(pt_ref, bb, n_pages, kidx_hbm, buf.at[slot], sem.at[slot])
        lax.fori_loop(0, n_pages, lambda p, c: (cp(p).start(), c)[1], 0)

    def wait(bb, slot):
        cp = _page_copies(pt_ref, bb, n_pages, kidx_hbm, buf.at[slot], sem.at[slot])
        lax.fori_loop(0, n_pages, lambda p, c: (cp(p).wait(), c)[1], 0)

    slot = b % 2

    @pl.when(b == 0)
    def _():
        start(0, 0)

    @pl.when(b + 1 < nb)
    def _():
        start(b + 1, 1 - slot)

    wait(b, slot)

    qi = iq_ref[0]
    wcol = iw_ref[0] * (IDX_DIM ** -0.5 * IDX_HEADS ** -0.5)
    s = jnp.maximum(_nt_f32(qi, buf[slot]), 0.0)
    o_ref[0, :, 0:past] = jnp.sum(s * wcol, axis=0, keepdims=True)
    sn = jnp.maximum(_nt_f32(qi, iknew_ref[0]), 0.0)
    sn = jnp.sum(sn * wcol, axis=0, keepdims=True)
    lane = lax.broadcasted_iota(jnp.int32, (1, LANES), 1)
    o_ref[0, :, past:past + LANES] = jnp.where(lane == 0, sn, -jnp.inf)


def dsa_scores(page_table, iq, iw, iknew_pad, cache_kidx):
    b, n_pages = page_table.shape
    past = n_pages * PAGE_SIZE
    kern = functools.partial(_dsa_scores_kernel, n_pages=n_pages)
    gs = pltpu.PrefetchScalarGridSpec(
        num_scalar_prefetch=1,
        grid=(b,),
        in_specs=[pl.BlockSpec((1, IDX_HEADS, IDX_DIM), lambda i, pt: (i, 0, 0)),
                  pl.BlockSpec((1, IDX_HEADS, 1), lambda i, pt: (i, 0, 0)),
                  pl.BlockSpec((1, LANES, IDX_DIM), lambda i, pt: (i, 0, 0)),
                  pl.BlockSpec(memory_space=pl.ANY)],
        out_specs=pl.BlockSpec((1, 1, past + LANES), lambda i, pt: (i, 0, 0)),
        scratch_shapes=[pltpu.VMEM((2, past, IDX_DIM), F32), pltpu.SemaphoreType.DMA((2,))],
    )
    return pl.pallas_call(
        kern, grid_spec=gs,
        out_shape=jax.ShapeDtypeStruct((b, 1, past + LANES), F32),
        compiler_params=_cparams("arbitrary"),
        name="dsa_scores",
    )(page_table, iq, iw, iknew_pad, cache_kidx)


def _dsa_threshold_kernel(s_ref, thr_ref, *, topk):
    s = s_ref[...]
    nb = s.shape[0]
    finite = s > -jnp.inf
    lo0 = jnp.min(jnp.where(finite, s, jnp.inf), axis=1, keepdims=True)
    hi0 = jnp.max(s, axis=1, keepdims=True)
    cnt0 = jnp.sum(jnp.where(finite, 1.0, 0.0), axis=1, keepdims=True)

    def count_ge(thr):
        return jnp.sum(jnp.where(s_ref[...] >= thr, 1.0, 0.0), axis=1, keepdims=True)

    thr = _bisect_threshold(count_ge, lo0, hi0, cnt0, topk)
    thr_ref[...] = jnp.broadcast_to(thr, (nb, LANES))


def dsa_threshold(scores, topk):
    b, l = scores.shape
    return pl.pallas_call(
        functools.partial(_dsa_threshold_kernel, topk=topk),
        grid=(1,),
        in_specs=[pl.BlockSpec((b, l), lambda i: (0, 0))],
        out_specs=pl.BlockSpec((b, LANES), lambda i: (0, 0)),
        out_shape=jax.ShapeDtypeStruct((b, LANES), F32),
        compiler_params=_cparams("arbitrary"),
        name="dsa_threshold",
    )(scores)


def _dsa_decode_kernel(pt_ref, q_ref, s_ref, thr_ref, knew_ref, vnew_ref, bias_ref, k_hbm, v_hbm, o_ref,
                       kbuf, vbuf, sem, *, n_pages):
    b = pl.program_id(0)
    nb = pl.num_programs(0)
    past = n_pages * PAGE_SIZE

    def copies(bb, slot):
        ck = _page_copies(pt_ref, bb, n_pages, k_hbm, kbuf.at[slot], sem.at[0, slot])
        cv = _page_copies(pt_ref, bb, n_pages, v_hbm, vbuf.at[slot], sem.at[1, slot])
        return ck, cv

    def start(bb, slot):
        ck, cv = copies(bb, slot)
        lax.fori_loop(0, n_pages, lambda p, c: (ck(p).start(), cv(p).start(), c)[2], 0)

    def wait(bb, slot):
        ck, cv = copies(bb, slot)
        lax.fori_loop(0, n_pages, lambda p, c: (ck(p).wait(), cv(p).wait(), c)[2], 0)

    slot = b % 2

    @pl.when(b == 0)
    def _():
        kbuf[:, past:past + LANES, :] = jnp.zeros((2, LANES, kbuf.shape[2]), F32)
        vbuf[:, past:past + LANES, :] = jnp.zeros((2, LANES, vbuf.shape[2]), F32)
        start(0, 0)

    @pl.when(b + 1 < nb)
    def _():
        start(b + 1, 1 - slot)

    kbuf[slot, past:past + 8, :] = knew_ref[0]
    vbuf[slot, past:past + 8, :] = vnew_ref[0]
    wait(b, slot)

    sel = s_ref[0] >= thr_ref[0][:, 0:1]
    outs = []
    for n in range(B_KV_HEADS):
        cols = slice(n * B_HEAD_DIM, (n + 1) * B_HEAD_DIM)
        kn = kbuf[slot, :, cols]
        vn = vbuf[slot, :, cols]
        qn = q_ref[0, n]
        lg = _nt_f32(qn, kn) * B_HEAD_DIM ** -0.5 + bias_ref[n]
        m = jnp.max(jnp.where(sel, lg, NEG), axis=1, keepdims=True)
        p = jnp.where(sel, jnp.exp(lg - m), 0.0)
        l = jnp.sum(p, axis=1, keepdims=True)
        o = jnp.dot(p, vn, preferred_element_type=F32, precision=lax.Precision.HIGHEST)
        outs.append(o / l)
    o_ref[0] = jnp.concatenate(outs, axis=0).astype(o_ref.dtype)


def dsa_decode(page_table, q8, scores, thr, knew8, vnew8, bias_rows, cache_k2, cache_v2):
    b, n_pages = page_table.shape
    past = n_pages * PAGE_SIZE
    l = past + LANES
    wkv = B_KV_HEADS * B_HEAD_DIM
    kern = functools.partial(_dsa_decode_kernel, n_pages=n_pages)
    gs = pltpu.PrefetchScalarGridSpec(
        num_scalar_prefetch=1,
        grid=(b,),
        in_specs=[pl.BlockSpec((1, B_KV_HEADS, 8, B_HEAD_DIM), lambda i, pt: (i, 0, 0, 0)),
                  pl.BlockSpec((1, 1, l), lambda i, pt: (i, 0, 0)),
                  pl.BlockSpec((1, 1, LANES), lambda i, pt: (i, 0, 0)),
                  pl.BlockSpec((1, 8, wkv), lambda i, pt: (i, 0, 0)),
                  pl.BlockSpec((1, 8, wkv), lambda i, pt: (i, 0, 0)),
                  pl.BlockSpec((B_KV_HEADS, 8, l), lambda i, pt: (0, 0, 0)),
                  pl.BlockSpec(memory_space=pl.ANY),
                  pl.BlockSpec(memory_space=pl.ANY)],
        out_specs=pl.BlockSpec((1, 2 * 8, B_HEAD_DIM), lambda i, pt: (i, 0, 0)),
        scratch_shapes=[pltpu.VMEM((2, l, wkv), F32), pltpu.VMEM((2, l, wkv), F32),
                        pltpu.SemaphoreType.DMA((2, 2))],
    )
    return pl.pallas_call(
        kern, grid_spec=gs,
        out_shape=jax.ShapeDtypeStruct((b, 2 * 8, B_HEAD_DIM), F32),
        compiler_params=_cparams("arbitrary"),
        name="dsa_decode",
    )(page_table, q8, scores, thr, knew8, vnew8, bias_rows, cache_k2, cache_v2)


def _merge_kernel(ya_ref, yb_ref, ga_ref, gb_ref, wa_ref, wb_ref, o_ref):
    a = _mm(ya_ref[...], wa_ref[...])
    bb = _mm(yb_ref[...], wb_ref[...])
    o_ref[...] = (jax.nn.sigmoid(ga_ref[...]) * a + jax.nn.sigmoid(gb_ref[...]) * bb).astype(o_ref.dtype)


def merge(ya, yb, z, ga_col, gb_col, wa, wb, *, tm, tn):
    m, k = ya.shape
    n = wa.shape[1]
    return pl.pallas_call(
        _merge_kernel,
        grid=(m // tm, n // tn),
        in_specs=[pl.BlockSpec((tm, k), lambda i, j: (i, 0)),
                  pl.BlockSpec((tm, k), lambda i, j: (i, 0)),
                  pl.BlockSpec((tm, tn), lambda i, j: (i, ga_col // tn + j)),
                  pl.BlockSpec((tm, tn), lambda i, j: (i, gb_col // tn + j)),
                  pl.BlockSpec((k, tn), lambda i, j: (0, j)),
                  pl.BlockSpec((k, tn), lambda i, j: (0, j))],
        out_specs=pl.BlockSpec((tm, tn), lambda i, j: (i, j)),
        out_shape=jax.ShapeDtypeStruct((m, n), wa.dtype),
        compiler_params=_cparams("parallel", "arbitrary"),
        name="merge",
    )(ya, yb, z, z, wa, wb)


def _cross_prompt_kernel(x_ref, g_ref, wq_ref, mk_ref, mv_ref, wo_ref, o_ref):
    x = x_ref[...]
    ms = jnp.mean(x * x, axis=-1, keepdims=True)
    h = (x * lax.rsqrt(ms + EPS) * g_ref[...]).astype(BF16)
    q = jnp.dot(h, wq_ref[...], preferred_element_type=F32)
    outs = []
    for hh in range(X_HEADS):
        cols = slice(hh * X_HEAD_DIM, (hh + 1) * X_HEAD_DIM)
        lg = _nt(q[:, cols].astype(BF16), mk_ref[:, cols]) * X_HEAD_DIM ** -0.5
        mx = jnp.max(lg, axis=1, keepdims=True)
        p = jnp.exp(lg - mx)
        l = jnp.sum(p, axis=1, keepdims=True)
        o = jnp.dot(p.astype(BF16), mv_ref[:, cols], preferred_element_type=F32)
        outs.append((o / l).astype(BF16))
    att = jnp.concatenate(outs, axis=1)
    o_ref[...] = x + jnp.dot(att, wo_ref[...], preferred_element_type=F32)


def cross_prompt(x, g, wq, mk, mv, wo, *, tm):
    m, d = x.shape
    full = lambda a: pl.BlockSpec(a.shape, lambda i: (0,) * a.ndim)
    g2 = g.reshape(1, d)
    return pl.pallas_call(
        _cross_prompt_kernel,
        grid=(m // tm,),
        in_specs=[pl.BlockSpec((tm, d), lambda i: (i, 0)), full(g2), full(wq), full(mk), full(mv), full(wo)],
        out_specs=pl.BlockSpec((tm, d), lambda i: (i, 0)),
        out_shape=jax.ShapeDtypeStruct((m, d), F32),
        compiler_params=_cparams("parallel"),
        name="cross_prompt",
    )(x, g2, wq, mk, mv, wo)


def _cross_step_kernel(q_ref, mk_ref, mv_ref, o_ref):
    q = q_ref[0]
    outs = []
    for hh in range(X_HEADS):
        cols = slice(hh * X_HEAD_DIM, (hh + 1) * X_HEAD_DIM)
        kh = mk_ref[0, :, cols]
        vh = mv_ref[0, :, cols]
        lg = jnp.sum(kh * q[:, cols], axis=1, keepdims=True) * X_HEAD_DIM ** -0.5
        mx = jnp.max(lg, axis=0, keepdims=True)
        p = jnp.exp(lg - mx)
        l = jnp.sum(p, axis=0, keepdims=True)
        outs.append(jnp.sum(p * vh, axis=0, keepdims=True) / l)
    o_ref[0] = jnp.concatenate(outs, axis=1).astype(o_ref.dtype)


def cross_step(q, mk, mv):
    b, w = q.shape
    mem = mk.shape[1]
    return pl.pallas_call(
        _cross_step_kernel,
        grid=(b,),
        in_specs=[pl.BlockSpec((1, 1, w), lambda i: (i, 0, 0)),
                  pl.BlockSpec((1, mem, w), lambda i: (i, 0, 0)),
                  pl.BlockSpec((1, mem, w), lambda i: (i, 0, 0))],
        out_specs=pl.BlockSpec((1, 1, w), lambda i: (i, 0, 0)),
        out_shape=jax.ShapeDtypeStruct((b, 1, w), F32),
        compiler_params=_cparams("arbitrary"),
        name="cross_step",
    )(q.reshape(b, 1, w), mk, mv)


def _router_kernel(x_ref, g_ref, w_ref, b_ref, hf_ref, route_ref):
    x = x_ref[...]
    ms = jnp.mean(x * x, axis=-1, keepdims=True)
    hf = x * lax.rsqrt(ms + EPS) * g_ref[...]
    hf_ref[...] = hf
    lg = jnp.dot(hf, w_ref[...], preferred_element_type=F32, precision=lax.Precision.HIGHEST) + b_ref[...]
    tm = lg.shape[0]
    lane = lax.broadcasted_iota(jnp.int32, (tm, LANES), 1)
    big = jnp.int32(LANES)
    is_g = lane < N_GROUPS
    gmax = jnp.max(jnp.where(is_g, lg, -jnp.inf), axis=1, keepdims=True)
    grp = jnp.min(jnp.where(is_g & (lg == gmax), lane, big), axis=1, keepdims=True)
    p_grp = 1.0 / jnp.sum(jnp.where(is_g, jnp.exp(lg - gmax), 0.0), axis=1, keepdims=True)
    e_lo = N_GROUPS + grp * EXP_PER_GROUP
    in_g = (lane >= e_lo) & (lane < e_lo + EXP_PER_GROUP)
    v1 = jnp.max(jnp.where(in_g, lg, -jnp.inf), axis=1, keepdims=True)
    i1 = jnp.min(jnp.where(in_g & (lg == v1), lane, big), axis=1, keepdims=True)
    rest = in_g & (lane != i1)
    v2 = jnp.max(jnp.where(rest, lg, -jnp.inf), axis=1, keepdims=True)
    i2 = jnp.min(jnp.where(rest & (lg == v2), lane, big), axis=1, keepdims=True)
    e2 = jnp.exp(v2 - v1)
    g1 = p_grp / (1.0 + e2)
    g2 = p_grp * e2 / (1.0 + e2)
    r = jnp.where(lane == 0, (i1 - N_GROUPS).astype(F32),
                  jnp.where(lane == 1, (i2 - N_GROUPS).astype(F32),
                            jnp.where(lane == 2, g1, jnp.where(lane == 3, g2, 0.0))))
    route_ref[...] = r


def router(x, g, w_pad, b_pad, *, tm):
    m, d = x.shape
    return pl.pallas_call(
        _router_kernel,
        grid=(pl.cdiv(m, tm),),
        in_specs=[pl.BlockSpec((tm, d), lambda i: (i, 0)),
                  pl.BlockSpec((1, d), lambda i: (0, 0)),
                  pl.BlockSpec((d, LANES), lambda i: (0, 0)),
                  pl.BlockSpec((1, LANES), lambda i: (0, 0))],
        out_specs=[pl.BlockSpec((tm, d), lambda i: (i, 0)),
                   pl.BlockSpec((tm, LANES), lambda i: (i, 0))],
        out_shape=[jax.ShapeDtypeStruct((m, d), F32), jax.ShapeDtypeStruct((m, LANES), F32)],
        compiler_params=_cparams("parallel"),
        name="router",
    )(x, g.reshape(1, d), w_pad, b_pad)


def _moe_kernel(be_ref, tok_ref, hf_hbm, wg_ref, wu_ref, wd_ref, o_ref, xbuf, wg_s, wu_s, wd_s, sem):
    b = pl.program_id(0)
    nb = pl.num_programs(0)
    blk = xbuf.shape[1]

    def row_copy(bb, slot, r):
        tok = tok_ref[bb * blk + r]
        return pltpu.make_async_copy(hf_hbm.at[pl.ds(tok, 1)], xbuf.at[slot, pl.ds(r, 1)], sem.at[slot])

    def start(bb, slot):
        lax.fori_loop(0, blk, lambda r, c: (row_copy(bb, slot, r).start(), c)[1], 0)

    def wait(bb, slot):
        lax.fori_loop(0, blk, lambda r, c: (row_copy(bb, slot, r).wait(), c)[1], 0)

    slot = b % 2

    @pl.when(b == 0)
    def _():
        start(0, 0)

    @pl.when(b + 1 < nb)
    def _():
        start(b + 1, 1 - slot)

    changed = jnp.logical_or(b == 0, be_ref[b] != be_ref[jnp.maximum(b - 1, 0)])

    @pl.when(changed)
    def _():
        wg_s[...] = wg_ref[0].astype(BF16)
        wu_s[...] = wu_ref[0].astype(BF16)
        wd_s[...] = wd_ref[0].astype(BF16)

    wait(b, slot)
    x = xbuf[slot].astype(BF16)
    gg = jnp.dot(x, wg_s[...], preferred_element_type=F32)
    uu = jnp.dot(x, wu_s[...], preferred_element_type=F32)
    a = (_silu(gg) * uu).astype(BF16)
    o_ref[...] = jnp.dot(a, wd_s[...], preferred_element_type=F32)


def moe_experts(block_e, slot_tok, hf, w_g, w_u, w_d, *, blk):
    n_blocks = block_e.shape[0]
    d = hf.shape[1]
    ff = w_g.shape[2]
    gs = pltpu.PrefetchScalarGridSpec(
        num_scalar_prefetch=2,
        grid=(n_blocks,),
        in_specs=[pl.BlockSpec(memory_space=pl.ANY),
                  pl.BlockSpec((1, d, ff), lambda i, be, tk: (be[i], 0, 0)),
                  pl.BlockSpec((1, d, ff), lambda i, be, tk: (be[i], 0, 0)),
                  pl.BlockSpec((1, ff, d), lambda i, be, tk: (be[i], 0, 0))],
        out_specs=pl.BlockSpec((blk, d), lambda i, be, tk: (i, 0)),
        scratch_shapes=[pltpu.VMEM((2, blk, d), F32),
                        pltpu.VMEM((d, ff), BF16), pltpu.VMEM((d, ff), BF16), pltpu.VMEM((ff, d), BF16),
                        pltpu.SemaphoreType.DMA((2,))],
    )
    return pl.pallas_call(
        _moe_kernel, grid_spec=gs,
        out_shape=jax.ShapeDtypeStruct((n_blocks * blk, d), F32),
        compiler_params=_cparams("arbitrary"),
        name="moe_experts",
    )(block_e, slot_tok, hf, w_g, w_u, w_d)


def _combine_kernel(dest_ref, x_ref, route_ref, gf_ref, yb_hbm, o_ref, ybuf, sem):
    i = pl.program_id(0)
    nt = pl.num_programs(0)
    tm = x_ref.shape[0]

    def row_copy(ii, slot, r):
        base = (ii * tm + r) * EXPERT_TOPK
        c0 = pltpu.make_async_copy(yb_hbm.at[pl.ds(dest_ref[base], 1)], ybuf.at[slot, 0, pl.ds(r, 1)], sem.at[slot])
        c1 = pltpu.make_async_copy(yb_hbm.at[pl.ds(dest_ref[base + 1], 1)], ybuf.at[slot, 1, pl.ds(r, 1)], sem.at[slot])
        return c0, c1

    def start(ii, slot):
        def body(r, c):
            c0, c1 = row_copy(ii, slot, r)
            c0.start()
            c1.start()
            return c
        lax.fori_loop(0, tm, body, 0)

    def wait(ii, slot):
        def body(r, c):
            c0, c1 = row_copy(ii, slot, r)
            c0.wait()
            c1.wait()
            return c
        lax.fori_loop(0, tm, body, 0)

    slot = i % 2

    @pl.when(i == 0)
    def _():
        start(0, 0)

    @pl.when(i + 1 < nt)
    def _():
        start(i + 1, 1 - slot)

    wait(i, slot)
    route = route_ref[...]
    x = x_ref[...] + route[:, 2:3] * ybuf[slot, 0] + route[:, 3:4] * ybuf[slot, 1]
    ms = jnp.mean(x * x, axis=-1, keepdims=True)
    o_ref[...] = x * lax.rsqrt(ms + EPS) * gf_ref[...]


def combine(dest_pad, x, route, gf, yb, *, tm):
    m, d = x.shape
    gs = pltpu.PrefetchScalarGridSpec(
        num_scalar_prefetch=1,
        grid=(pl.cdiv(m, tm),),
        in_specs=[pl.BlockSpec((tm, d), lambda i, ds_: (i, 0)),
                  pl.BlockSpec((tm, LANES), lambda i, ds_: (i, 0)),
                  pl.BlockSpec((1, d), lambda i, ds_: (0, 0)),
                  pl.BlockSpec(memory_space=pl.ANY)],
        out_specs=pl.BlockSpec((tm, d), lambda i, ds_: (i, 0)),
        scratch_shapes=[pltpu.VMEM((2, EXPERT_TOPK, tm, d), F32), pltpu.SemaphoreType.DMA((2,))],
    )
    return pl.pallas_call(
        _combine_kernel, grid_spec=gs,
        out_shape=jax.ShapeDtypeStruct((m, d), F32),
        compiler_params=_cparams("arbitrary"),
        name="combine",
    )(dest_pad, x, route, gf.reshape(1, d), yb)


def _t5_bucket(dist):
    dist = jnp.asarray(dist, jnp.int32)
    max_exact = REL_BUCKETS // 2
    dist_f = jnp.maximum(dist, 1).astype(F32)
    large = max_exact + (jnp.log(dist_f / max_exact) / math.log(REL_MAX_DIST / max_exact)
                         * (REL_BUCKETS - max_exact)).astype(jnp.int32)
    large = jnp.minimum(large, REL_BUCKETS - 1)
    return jnp.where(dist < max_exact, dist, large)


def _bias_tables(rel_bias, past):
    r = np.arange(LANES)
    diff = r[:, None] - r[None, :]
    buckets = jnp.stack([_t5_bucket(np.maximum(diff, 0)),
                         _t5_bucket(np.maximum(diff + LANES, 0)),
                         _t5_bucket(np.full((LANES, LANES), 2 * LANES))])
    tiles = jnp.transpose(rel_bias.astype(F32)[buckets], (3, 0, 1, 2))
    dist = np.maximum(past - np.arange(past + LANES), 0)
    rows = rel_bias.astype(F32)[_t5_bucket(dist)].T
    rows = rows.reshape(B_KV_HEADS, B_GROUP, past + LANES)
    rows = jnp.concatenate([rows, jnp.zeros_like(rows)], axis=1)
    return tiles, rows


def _dispatch(eid, n_tokens, blk):
    a = eid.shape[0]
    n_blocks = -(-(a + N_EXPERTS * (blk - 1)) // blk)
    rows = n_blocks * blk
    tok = jnp.arange(a, dtype=jnp.int32) // EXPERT_TOPK
    order = jnp.argsort(eid)
    e_sorted = eid[order]
    counts = jnp.bincount(eid, length=N_EXPERTS)
    starts = jnp.cumsum(counts) - counts
    padded = (counts + blk - 1) // blk * blk
    pad_end = jnp.cumsum(padded)
    pad_start = pad_end - padded
    dest_sorted = pad_start[e_sorted] + jnp.arange(a, dtype=jnp.int32) - starts[e_sorted]
    dest = jnp.zeros((a,), jnp.int32).at[order].set(dest_sorted.astype(jnp.int32))
    slot_tok = jnp.zeros((rows,), jnp.int32).at[dest].set(tok)
    block_e = jnp.minimum(jnp.searchsorted(pad_end, jnp.arange(n_blocks) * blk, side='right'), N_EXPERTS - 1)
    return dest, slot_tok, block_e.astype(jnp.int32)


def kernel(x_prompt, x_sample, mem_prompt, cache_k, cache_v, cache_kidx, page_table, state_hgrn, cache_mem_k,
           cache_mem_v, norm_mix, w_in, hgrn_lb_logits, hgrn_norm, w_branch_a, w_branch_b, w_out, norm_cross, w_xq,
           w_xk, w_xv, w_xo, norm_ffn, w_router_group, b_router_group, w_router_expert, b_router_expert, w_exp_gate,
           w_exp_up, w_exp_down, rel_bias, norm_final):
    l = 0
    bp, t, d = x_prompt.shape
    db = x_sample.shape[0]
    past = page_table.shape[1] * PAGE_SIZE
    xp = x_prompt.reshape(bp * t, d)
    xs = x_sample.reshape(db, d)

    wi = w_in[l]
    o = np.cumsum((0,) + (1024, 1024, 1024, 1024, 1024, 256, 256, 1024, 16, 64, 2048, 2048))
    seg = lambda i: wi[:, o[i]:o[i + 1]]
    w_cat = jnp.concatenate([seg(0), seg(1), seg(2), seg(3), seg(4), seg(7), seg(10), seg(11), seg(5), seg(6)],
                            axis=1).astype(BF16)
    w_small = jnp.pad(jnp.concatenate([seg(8), seg(9)], axis=1), ((0, 0), (0, LANES - 80))).astype(BF16)
    wa, wb, wo = w_branch_a[l].astype(BF16), w_branch_b[l].astype(BF16), w_out[l].astype(BF16)
    wxq, wxk, wxv, wxo = (w_xq[l].astype(BF16), w_xk[l].astype(BF16), w_xv[l].astype(BF16), w_xo[l].astype(BF16))
    w_route = jnp.pad(jnp.concatenate([w_router_group[l], w_router_expert[l]], axis=1),
                      ((0, 0), (0, LANES - N_GROUPS - N_EXPERTS)))
    b_route = jnp.pad(jnp.concatenate([b_router_group[l], b_router_expert[l]]),
                      (0, LANES - N_GROUPS - N_EXPERTS)).reshape(1, LANES)
    bias_tiles, bias_rows = _bias_tables(rel_bias, past)

    zp = norm_matmul(xp, norm_mix[l], w_cat, tm=1024, tn=512)
    zps = norm_matmul(xp, norm_mix[l], w_small, tm=1024, tn=LANES)
    kp = zp[:, C_BK:C_BK + 256]
    vp = zp[:, C_BV:C_BV + 256]
    ikp = zps[:, IDX_HEADS:IDX_HEADS + IDX_DIM]
    ya_p, st_p = hgrn_prompt(zp, hgrn_lb_logits, hgrn_norm[l])
    yb_p = dsa_prompt(zp, zps, ikp.astype(BF16), kp.astype(BF16), vp.astype(BF16), bias_tiles)
    mg_p = merge(ya_p, yb_p, zp, C_GA, C_GB, wa, wb, tm=512, tn=512)
    x1p = matmul(mg_p, wo, xp, tm=512, tn=512)
    memp = mem_prompt.reshape(-1, d)
    mk = matmul(memp, wxk, tm=memp.shape[0], tn=512)
    mv = matmul(memp, wxv, tm=memp.shape[0], tn=512)
    x2p = cross_prompt(x1p, norm_cross[l], wxq, mk.astype(BF16), mv.astype(BF16), wxo, tm=512)

    zs = norm_matmul(xs, norm_mix[l], wi, tm=db, tn=512)
    aq_s, af_s, ai_s, ag_s, bq_s, ks, vs, iq_s, iw_s, iks, ga_s, gb_s = jnp.split(zs, o[1:-1].tolist(), axis=1)
    ya_s, st_s = hgrn_step(zs[:, :4 * A_HEADS * A_DK], hgrn_lb_logits, hgrn_norm[l], state_hgrn[l])
    iknew_pad = jnp.pad(iks[:, None, :], ((0, 0), (0, LANES - 1), (0, 0)))
    scores = dsa_scores(page_table, iq_s.reshape(db, IDX_HEADS, IDX_DIM), iw_s.reshape(db, IDX_HEADS, 1), iknew_pad,
                        cache_kidx[l]).reshape(db, past + LANES)
    topk_s = min(TOPK_MAX, (past + 1) // 4)
    thr = dsa_threshold(scores, topk_s)
    q8 = jnp.pad(bq_s.reshape(db, B_KV_HEADS, B_GROUP, B_HEAD_DIM), ((0, 0), (0, 0), (0, 8 - B_GROUP), (0, 0)))
    knew8 = jnp.pad(ks[:, None, :], ((0, 0), (0, 7), (0, 0)))
    vnew8 = jnp.pad(vs[:, None, :], ((0, 0), (0, 7), (0, 0)))
    n_pool = cache_k.shape[1]
    ob = dsa_decode(page_table, q8, scores.reshape(db, 1, -1), thr.reshape(db, 1, LANES), knew8, vnew8, bias_rows,
                    cache_k[l].reshape(n_pool, PAGE_SIZE, -1), cache_v[l].reshape(n_pool, PAGE_SIZE, -1))
    yb_s = ob.reshape(db, B_KV_HEADS, 8, B_HEAD_DIM)[:, :, :B_GROUP].reshape(db, B_HEADS * B_HEAD_DIM)
    gates_s = jnp.concatenate([ga_s, gb_s], axis=1)
    mg_s = merge(ya_s.reshape(db, -1), yb_s, gates_s, 0, d, w_branch_a[l], w_branch_b[l], tm=db, tn=512)
    x1s = matmul(mg_s, w_out[l], xs, tm=db, tn=512)
    qx_s = norm_matmul(x1s, norm_cross[l], w_xq[l], tm=db, tn=512)
    mem = cache_mem_k.shape[2]
    att_s = cross_step(qx_s, cache_mem_k[l].reshape(db, mem, -1), cache_mem_v[l].reshape(db, mem, -1))
    x2s = matmul(att_s.reshape(db, -1), w_xo[l], x1s, tm=db, tn=512)

    x2 = jnp.concatenate([x2p, x2s], axis=0)
    n = x2.shape[0]
    hf, route = router(x2, norm_ffn[l], w_route, b_route, tm=256)
    eid = route[:, :EXPERT_TOPK].astype(jnp.int32).reshape(-1)
    a = n * EXPERT_TOPK
    blk = min(MOE_BLOCK, max(8, a // N_EXPERTS))
    dest, slot_tok, block_e = _dispatch(eid, n, blk)
    yb = moe_experts(block_e, slot_tok, hf, w_exp_gate[l], w_exp_up[l], w_exp_down[l], blk=blk)
    tmc = 128
    n_pad = -(-n // tmc) * tmc
    dest_pad = jnp.pad(dest, (0, (n_pad - n) * EXPERT_TOPK))
    y = combine(dest_pad, x2, route, norm_final, yb, tm=tmc)

    y_prompt = y[:bp * t].reshape(bp, t, d)
    y_sample = y[bp * t:].reshape(db, 1, d)
    return (y_prompt, y_sample,
            kp.reshape(1, bp, t, B_KV_HEADS, B_HEAD_DIM), vp.reshape(1, bp, t, B_KV_HEADS, B_HEAD_DIM),
            ikp.reshape(1, bp, t, IDX_DIM),
            jnp.swapaxes(st_p, 1, 2).reshape(1, bp, A_HEADS, A_DK, A_DV),
            mk.reshape(1, bp, -1, X_HEADS, X_HEAD_DIM), mv.reshape(1, bp, -1, X_HEADS, X_HEAD_DIM),
            ks.reshape(1, db, 1, B_KV_HEADS, B_HEAD_DIM), vs.reshape(1, db, 1, B_KV_HEADS, B_HEAD_DIM),
            iks.reshape(1, db, 1, IDX_DIM),
            st_s.reshape(1, db, A_HEADS, A_DK, A_DV))
```

```python
import functools
import math

import jax
import jax.numpy as jnp
import numpy as np
from jax import lax
from jax.experimental import pallas as pl
from jax.experimental.pallas import tpu as pltpu

F32 = jnp.float32
BF16 = jnp.bfloat16
EPS = 1e-6

D_MODEL = 2048
A_HEADS, A_DK, A_DV = 8, 128, 128
B_HEADS, B_KV_HEADS, B_HEAD_DIM = 8, 2, 128
B_GROUP = B_HEADS // B_KV_HEADS
IDX_HEADS, IDX_DIM = 16, 64
TOPK_MAX = 256
PAGE_SIZE = 128
REL_BUCKETS, REL_MAX_DIST = 32, 128
X_HEADS, X_HEAD_DIM = 4, 128
N_GROUPS, EXP_PER_GROUP = 4, 8
N_EXPERTS = N_GROUPS * EXP_PER_GROUP
EXPERT_TOPK = 2
EXPERT_FF = 512
MOE_BLOCK = 128

LANES = 128
VMEM_LIMIT = 56 * 1024 * 1024

NEG = -1e30

C_AQ, C_AF, C_AI, C_AG, C_BQ, C_IQ, C_GA, C_GB, C_BK, C_BV = (
    0, 1024, 2048, 3072, 4096, 5120, 6144, 8192, 10240, 10496)
NZ = 10752


def _cparams(*sem):
    return pltpu.CompilerParams(dimension_semantics=sem, vmem_limit_bytes=VMEM_LIMIT)


def _silu(x):
    return x * jax.nn.sigmoid(x)


def _nt(a, b):
    return lax.dot_general(a, b, (((1,), (1,)), ((), ())), preferred_element_type=F32)


def _nt_f32(a, b):
    return lax.dot_general(a, b, (((1,), (1,)), ((), ())), preferred_element_type=F32,
                           precision=lax.Precision.HIGHEST)


def _mm(a, w):
    if w.dtype == F32:
        return jnp.dot(a.astype(F32), w, preferred_element_type=F32, precision=lax.Precision.HIGHEST)
    return jnp.dot(a.astype(BF16), w, preferred_element_type=F32)


def _norm_matmul_kernel(x_ref, g_ref, w_ref, o_ref, h_ref):
    @pl.when(pl.program_id(1) == 0)
    def _():
        x = x_ref[...]
        ms = jnp.mean(x * x, axis=-1, keepdims=True)
        h_ref[...] = (x * lax.rsqrt(ms + EPS) * g_ref[...]).astype(h_ref.dtype)

    o_ref[...] = _mm(h_ref[...], w_ref[...])


def norm_matmul(x, g, w, *, tm, tn):
    m, k = x.shape
    n = w.shape[1]
    return pl.pallas_call(
        _norm_matmul_kernel,
        grid=(m // tm, pl.cdiv(n, tn)),
        in_specs=[pl.BlockSpec((tm, k), lambda i, j: (i, 0)),
                  pl.BlockSpec((1, k), lambda i, j: (0, 0)),
                  pl.BlockSpec((k, tn), lambda i, j: (0, j))],
        out_specs=pl.BlockSpec((tm, tn), lambda i, j: (i, j)),
        out_shape=jax.ShapeDtypeStruct((m, n), F32),
        scratch_shapes=[pltpu.VMEM((tm, k), w.dtype)],
        compiler_params=_cparams("parallel", "arbitrary"),
        name="norm_matmul",
    )(x, g.reshape(1, k), w)


def _matmul_res_kernel(x_ref, w_ref, r_ref, o_ref):
    o_ref[...] = r_ref[...] + _mm(x_ref[...], w_ref[...])


def _matmul_kernel(x_ref, w_ref, o_ref):
    o_ref[...] = _mm(x_ref[...], w_ref[...])


def matmul(x, w, res=None, *, tm, tn):
    m, k = x.shape
    n = w.shape[1]
    in_specs = [pl.BlockSpec((tm, k), lambda i, j: (i, 0)),
                pl.BlockSpec((k, tn), lambda i, j: (0, j))]
    args = [x, w]
    kern = _matmul_kernel
    if res is not None:
        in_specs.append(pl.BlockSpec((tm, tn), lambda i, j: (i, j)))
        args.append(res)
        kern = _matmul_res_kernel
    return pl.pallas_call(
        kern,
        grid=(m // tm, n // tn),
        in_specs=in_specs,
        out_specs=pl.BlockSpec((tm, tn), lambda i, j: (i, j)),
        out_shape=jax.ShapeDtypeStruct((m, n), F32),
        compiler_params=_cparams("parallel", "arbitrary"),
        name="matmul",
    )(*args)


HG_TB = 128
HG_C = 16


def _hgrn_prompt_kernel(aq_ref, af_ref, ai_ref, ag_ref, lbl_ref, ng_ref, ya_ref, st_out_ref,
                        st_ref, q_s, k_s, g_s, v_s):
    t = pl.program_id(0)

    @pl.when(t == 0)
    def _():
        st_ref[...] = jnp.zeros_like(st_ref)

    lbl = lbl_ref[...]
    mx = jnp.max(lbl, axis=0, keepdims=True)
    ex = jnp.exp(lbl - mx)
    lb = ex[0:1, :] / jnp.sum(ex, axis=0, keepdims=True)

    f = lb + (1.0 - lb) * jax.nn.sigmoid(af_ref[...])
    logf = jnp.log(f)
    row = lax.broadcasted_iota(jnp.int32, (HG_TB, HG_TB), 0)
    col = lax.broadcasted_iota(jnp.int32, (HG_TB, HG_TB), 1)
    tri = jnp.where((row // HG_C == col // HG_C) & (col <= row), 1.0, 0.0).astype(F32)
    g_s[...] = jnp.dot(tri, logf, preferred_element_type=F32, precision=lax.Precision.HIGHEST)
    q_s[...] = _silu(aq_ref[...])
    k_s[...] = 1.0 - f
    v_s[...] = ai_ref[...]

    sub = lax.broadcasted_iota(jnp.int32, (HG_C, A_DK), 0)
    ng = ng_ref[...]

    def chunk(c, carry):
        r0 = pl.multiple_of(c * HG_C, HG_C)
        rows = pl.ds(r0, HG_C)
        for h in range(A_HEADS):
            cols = slice(h * A_DK, (h + 1) * A_DK)
            g = g_s[rows, cols]
            qh = q_s[rows, cols]
            kh = k_s[rows, cols]
            vh = v_s[rows, cols]
            o = jnp.zeros((HG_C, A_DV), F32)
            for tt in range(HG_C):
                d = g[tt:tt + 1, :] - g
                e = jnp.exp(jnp.where(sub <= tt, d, -jnp.inf))
                p = e * (qh[tt:tt + 1, :] * kh)
                a_col = jnp.sum(p, axis=1, keepdims=True)
                o_row = jnp.sum(a_col * vh, axis=0, keepdims=True)
                o = jnp.where(sub == tt, o_row, o)
            st = st_ref[h]
            qg = (qh * jnp.exp(g)).astype(BF16)
            o = o + _nt(qg, st.astype(BF16))
            g_last = g[HG_C - 1:HG_C, :]
            kt = (kh * jnp.exp(g_last - g)).astype(BF16)
            upd = lax.dot_general(vh.astype(BF16), kt, (((0,), (0,)), ((), ())),
                                  preferred_element_type=F32)
            st_ref[h] = st * jnp.exp(g_last) + upd
            on = o * lax.rsqrt(jnp.mean(o * o, axis=-1, keepdims=True) + EPS) * ng
            ya_ref[rows, cols] = (on * _silu(ag_ref[rows, cols])).astype(ya_ref.dtype)
        return carry

    lax.fori_loop(0, HG_TB // HG_C, chunk, 0)

    @pl.when(t == pl.num_programs(0) - 1)
    def _():
        st_out_ref[...] = st_ref[...]


def hgrn_prompt(z, lb_logits, norm_g):
    m = z.shape[0]
    w = A_HEADS * A_DK

    def zspec(cb):
        return pl.BlockSpec((HG_TB, w), lambda t, cb=cb: (t, cb))

    return pl.pallas_call(
        _hgrn_prompt_kernel,
        grid=(m // HG_TB,),
        in_specs=[zspec(C_AQ // w), zspec(C_AF // w), zspec(C_AI // w), zspec(C_AG // w),
                  pl.BlockSpec(lb_logits.shape, lambda t: (0, 0)),
                  pl.BlockSpec((1, A_DV), lambda t: (0, 0))],
        out_specs=[pl.BlockSpec((HG_TB, w), lambda t: (t, 0)),
                   pl.BlockSpec((A_HEADS, A_DV, A_DK), lambda t: (0, 0, 0))],
        out_shape=[jax.ShapeDtypeStruct((m, w), BF16),
                   jax.ShapeDtypeStruct((A_HEADS, A_DV, A_DK), F32)],
        scratch_shapes=[pltpu.VMEM((A_HEADS, A_DV, A_DK), F32)] + [pltpu.VMEM((HG_TB, w), F32)] * 4,
        compiler_params=_cparams("arbitrary"),
        name="hgrn_prompt",
    )(z, z, z, z, lb_logits, norm_g.reshape(1, A_DV))


def _hgrn_step_kernel(z_ref, lbl_ref, ng_ref, s_ref, ya_ref, s_out_ref):
    lbl = lbl_ref[...]
    mx = jnp.max(lbl, axis=0, keepdims=True)
    ex = jnp.exp(lbl - mx)
    lb = ex[0:1, :] / jnp.sum(ex, axis=0, keepdims=True)
    z = z_ref[0]
    w = A_HEADS * A_DK
    q = _silu(z[:, 0:w])
    f = lb + (1.0 - lb) * jax.nn.sigmoid(z[:, w:2 * w])
    kk = 1.0 - f
    v = z[:, 2 * w:3 * w]
    ag = z[:, 3 * w:4 * w]
    rows = []
    for h in range(A_HEADS):
        cols = slice(h * A_DK, (h + 1) * A_DK)
        rows += [f[:, cols], kk[:, cols], q[:, cols]]
    rows.append(jnp.zeros((LANES - 3 * A_HEADS, A_DK), F32))
    xt = jnp.concatenate(rows, axis=0).T
    ng = ng_ref[...]
    r16 = lambda a: a.astype(BF16).astype(F32)
    outs = []
    for h in range(A_HEADS):
        cols = slice(h * A_DV, (h + 1) * A_DV)
        fcol = xt[:, 3 * h:3 * h + 1]
        kcol = xt[:, 3 * h + 1:3 * h + 2]
        qcol = xt[:, 3 * h + 2:3 * h + 3]
        s_old = s_ref[0, h]
        s_out_ref[0, h] = fcol * s_old + kcol * v[:, cols]
        o = (jnp.sum(r16(qcol * fcol) * r16(s_old), axis=0, keepdims=True)
             + jnp.sum(qcol * kcol, axis=0, keepdims=True) * v[:, cols])
        on = o * lax.rsqrt(jnp.mean(o * o, axis=-1, keepdims=True) + EPS) * ng
        outs.append(on * _silu(ag[:, cols]))
    ya_ref[0] = jnp.concatenate(outs, axis=1).astype(ya_ref.dtype)


def hgrn_step(z4, lb_logits, norm_g, state):
    b = z4.shape[0]
    w = A_HEADS * A_DK
    return pl.pallas_call(
        _hgrn_step_kernel,
        grid=(b,),
        in_specs=[pl.BlockSpec((1, 1, 4 * w), lambda i: (i, 0, 0)),
                  pl.BlockSpec(lb_logits.shape, lambda i: (0, 0)),
                  pl.BlockSpec((1, A_DV), lambda i: (0, 0)),
                  pl.BlockSpec((1, A_HEADS, A_DK, A_DV), lambda i: (i, 0, 0, 0))],
        out_specs=[pl.BlockSpec((1, 1, w), lambda i: (i, 0, 0)),
                   pl.BlockSpec((1, A_HEADS, A_DK, A_DV), lambda i: (i, 0, 0, 0))],
        out_shape=[jax.ShapeDtypeStruct((b, 1, w), F32),
                   jax.ShapeDtypeStruct(state.shape, F32)],
        compiler_params=_cparams("arbitrary"),
        name="hgrn_step",
    )(z4.reshape(b, 1, 4 * w), lb_logits, norm_g.reshape(1, A_DV), state)


BISECT_MAX_ITERS = 48


def _bisect_threshold(count_ge, lo, hi, cnt_lo, topk):
    kf = float(topk)

    def cond(c):
        it, _, _, _, busy = c
        return jnp.logical_and(it < BISECT_MAX_ITERS, busy > 0.0)

    def body(c):
        it, lo, hi, cnt, _ = c
        mid = 0.5 * lo + 0.5 * hi
        cm = count_ge(mid)
        ge = cm >= kf
        lo = jnp.where(ge, mid, lo)
        cnt = jnp.where(ge, cm, cnt)
        hi = jnp.where(ge, hi, mid)
        busy = jnp.max(jnp.where(cnt > kf, 1.0, 0.0))
        return it + 1, lo, hi, cnt, busy

    busy0 = jnp.max(jnp.where(cnt_lo > kf, 1.0, 0.0))
    _, lo, _, _, _ = lax.while_loop(cond, body, (jnp.int32(0), lo, hi, cnt_lo, busy0))
    return lo


DSA_QB = 128
DSA_W = 512


def _dsa_prompt_kernel(iq_ref, bq_ref, iw_ref, kidx_ref, k_ref, v_ref, bias_ref, o_ref,
                       score_s, qih_s, qs_s, wb_s, m_s, l_s, acc_s, *, topk):
    i = pl.program_id(0)
    nsub = DSA_W // LANES
    nch = (i * DSA_QB + DSA_QB + DSA_W - 1) // DSA_W
    qpos = i * DSA_QB + lax.broadcasted_iota(jnp.int32, (DSA_QB, 1), 0)

    iw = iw_ref[...]
    wscale = IDX_DIM ** -0.5 * IDX_HEADS ** -0.5
    for h in range(IDX_HEADS):
        qih_s[h] = iq_ref[:, h * IDX_DIM:(h + 1) * IDX_DIM].astype(BF16)
        wb_s[h] = jnp.broadcast_to(iw[:, h:h + 1] * wscale, (DSA_QB, LANES))
    for h in range(B_HEADS):
        qs_s[h] = bq_ref[:, h * B_HEAD_DIM:(h + 1) * B_HEAD_DIM].astype(BF16)

    def p1(c, carry):
        c0 = pl.multiple_of(c * DSA_W, DSA_W)
        kc = kidx_ref[pl.ds(c0, DSA_W), :]
        sc = [jnp.zeros((DSA_QB, LANES), F32) for _ in range(nsub)]
        for h in range(IDX_HEADS):
            s = jnp.maximum(_nt(qih_s[h], kc), 0.0)
            wb = wb_s[h]
            for j in range(nsub):
                sc[j] = sc[j] + s[:, j * LANES:(j + 1) * LANES] * wb
        for j in range(nsub):
            kpos = c0 + j * LANES + lax.broadcasted_iota(jnp.int32, (1, LANES), 1)
            score_s[c * nsub + j] = jnp.where(kpos <= qpos, sc[j], -jnp.inf)
        return carry

    lax.fori_loop(0, nch, p1, 0)

    def stats(c, carry):
        mn, mx = carry
        for j in range(nsub):
            s = score_s[c * nsub + j]
            mx = jnp.maximum(mx, s)
            mn = jnp.minimum(mn, jnp.where(s > -jnp.inf, s, jnp.inf))
        return mn, mx

    mn, mx = lax.fori_loop(0, nch, stats, (jnp.full((DSA_QB, LANES), jnp.inf, F32),
                                           jnp.full((DSA_QB, LANES), -jnp.inf, F32)))
    lo0 = jnp.min(mn, axis=1, keepdims=True)
    hi0 = jnp.max(mx, axis=1, keepdims=True)

    def count_ge(thr):
        thr_b = jnp.broadcast_to(thr, (DSA_QB, LANES))

        def body(c, acc):
            for j in range(nsub):
                s = score_s[c * nsub + j]
                acc = acc + jnp.where(s >= thr_b, 1.0, 0.0)
            return acc

        acc = lax.fori_loop(0, nch, body, jnp.zeros((DSA_QB, LANES), F32))
        return jnp.sum(acc, axis=1, keepdims=True)

    thr = _bisect_threshold(count_ge, lo0, hi0, (qpos + 1).astype(F32), topk)
    thr_b = jnp.broadcast_to(thr, (DSA_QB, LANES))

    m_s[...] = jnp.full(m_s.shape, NEG, F32)
    l_s[...] = jnp.zeros(l_s.shape, F32)
    acc_s[...] = jnp.zeros(acc_s.shape, F32)

    def p3(c, carry):
        c0 = pl.multiple_of(c * DSA_W, DSA_W)
        sel = [score_s[c * nsub + j] >= thr_b for j in range(nsub)]
        bidx = [jnp.clip(i - (c * nsub + j), 0, 2) for j in range(nsub)]
        for n in range(B_KV_HEADS):
            kn = k_ref[pl.ds(c0, DSA_W), n * B_HEAD_DIM:(n + 1) * B_HEAD_DIM]
            vn = v_ref[pl.ds(c0, DSA_W), n * B_HEAD_DIM:(n + 1) * B_HEAD_DIM]
            for gq in range(B_GROUP):
                h = n * B_GROUP + gq
                lg = _nt(qs_s[h], kn) * B_HEAD_DIM ** -0.5
                lgs = [lg[:, j * LANES:(j + 1) * LANES] + bias_ref[h, bidx[j]] for j in range(nsub)]
                m_old = m_s[h]
                m_cur = m_old
                for j in range(nsub):
                    m_cur = jnp.maximum(m_cur, jnp.where(sel[j], lgs[j], NEG))
                m_new = jnp.max(m_cur, axis=1, keepdims=True)
                m_new = jnp.broadcast_to(m_new, (DSA_QB, LANES))
                ps = [jnp.where(sel[j], jnp.exp(lgs[j] - m_new), 0.0) for j in range(nsub)]
                psum = ps[0]
                for j in range(1, nsub):
                    psum = psum + ps[j]
                alpha = jnp.exp(m_old - m_new)
                l_s[h] = alpha * l_s[h] + jnp.broadcast_to(jnp.sum(psum, axis=1, keepdims=True), (DSA_QB, LANES))
                p = jnp.concatenate(ps, axis=1).astype(BF16)
                acc_s[h] = alpha * acc_s[h] + jnp.dot(p, vn, preferred_element_type=F32)
                m_s[h] = m_new
        return carry

    lax.fori_loop(0, nch, p3, 0)

    for h in range(B_HEADS):
        o_ref[:, h * B_HEAD_DIM:(h + 1) * B_HEAD_DIM] = (acc_s[h] / l_s[h]).astype(o_ref.dtype)


def dsa_prompt(z, zsmall, kidx_bf, k_bf, v_bf, bias_tiles):
    m = z.shape[0]
    topk = min(TOPK_MAX, m // 4)
    wq = B_HEADS * B_HEAD_DIM
    kern = functools.partial(_dsa_prompt_kernel, topk=topk)
    return pl.pallas_call(
        kern,
        grid=(m // DSA_QB,),
        in_specs=[pl.BlockSpec((DSA_QB, IDX_HEADS * IDX_DIM), lambda i: (i, C_IQ // (IDX_HEADS * IDX_DIM))),
                  pl.BlockSpec((DSA_QB, wq), lambda i: (i, C_BQ // wq)),
                  pl.BlockSpec((DSA_QB, LANES), lambda i: (i, 0)),
                  pl.BlockSpec(kidx_bf.shape, lambda i: (0, 0)),
                  pl.BlockSpec(k_bf.shape, lambda i: (0, 0)),
                  pl.BlockSpec(v_bf.shape, lambda i: (0, 0)),
                  pl.BlockSpec(bias_tiles.shape, lambda i: (0, 0, 0, 0))],
        out_specs=pl.BlockSpec((DSA_QB, wq), lambda i: (i, 0)),
        out_shape=jax.ShapeDtypeStruct((m, wq), BF16),
        scratch_shapes=[pltpu.VMEM((m // LANES, DSA_QB, LANES), F32),
                        pltpu.VMEM((IDX_HEADS, DSA_QB, IDX_DIM), BF16),
                        pltpu.VMEM((B_HEADS, DSA_QB, B_HEAD_DIM), BF16),
                        pltpu.VMEM((IDX_HEADS, DSA_QB, LANES), F32),
                        pltpu.VMEM((B_HEADS, DSA_QB, LANES), F32),
                        pltpu.VMEM((B_HEADS, DSA_QB, LANES), F32),
                        pltpu.VMEM((B_HEADS, DSA_QB, B_HEAD_DIM), F32)],
        compiler_params=_cparams("arbitrary"),
        name="dsa_prompt",
    )(z, z, zsmall, kidx_bf, k_bf, v_bf, bias_tiles)


def _page_copies(table_ref, b, n_pages, src_hbm, dst, sem):
    def copy(p):
        return pltpu.make_async_copy(src_hbm.at[table_ref[b, p]],
                                     dst.at[pl.ds(p * PAGE_SIZE, PAGE_SIZE)], sem)
    return copy


def _dsa_scores_kernel(pt_ref, iq_ref, iw_ref, iknew_ref, kidx_hbm, o_ref, buf, sem, *, n_pages):
    b = pl.program_id(0)
    nb = pl.num_programs(0)
    past = n_pages * PAGE_SIZE

    def start(bb, slot):
        cp = _page_copies(pt_ref, bb, n_pages, kidx_hbm, buf.at[slot], sem.at[slot])
        lax.fori_loop(0, n_pages, lambda p, c: (cp(p).start(), c)[1], 0)

    def wait(bb, slot):
        cp = _page_copies(pt_ref, bb, n_pages, kidx_hbm, buf.at[slot], sem.at[slot])
        lax.fori_loop(0, n_pages, lambda p, c: (cp(p).wait(), c)[1], 0)

    slot = b % 2

    @pl.when(b == 0)
    def _():
        start(0, 0)

    @pl.when(b + 1 < nb)
    def _():
        start(b + 1, 1 - slot)

    wait(b, slot)

    r16 = lambda a: a.astype(BF16).astype(F32)
    qi = iq_ref[0].astype(BF16)
    wcol = r16(iw_ref[0]) * (IDX_DIM ** -0.5 * IDX_HEADS ** -0.5)
    s = r16(jnp.maximum(_nt(qi, buf[slot].astype(BF16)), 0.0))
    o_ref[0, :, 0:past] = jnp.sum(s * wcol, axis=0, keepdims=True)
    sn = r16(jnp.maximum(_nt(qi, iknew_ref[0].astype(BF16)), 0.0))
    sn = jnp.sum(sn * wcol, axis=0, keepdims=True)
    lane = lax.broadcasted_iota(jnp.int32, (1, LANES), 1)
    o_ref[0, :, past:past + LANES] = jnp.where(lane == 0, sn, -jnp.inf)


def dsa_scores(page_table, iq, iw, iknew_pad, cache_kidx):
    b, n_pages = page_table.shape
    past = n_pages * PAGE_SIZE
    kern = functools.partial(_dsa_scores_kernel, n_pages=n_pages)
    gs = pltpu.PrefetchScalarGridSpec(
        num_scalar_prefetch=1,
        grid=(b,),
        in_specs=[pl.BlockSpec((1, IDX_HEADS, IDX_DIM), lambda i, pt: (i, 0, 0)),
                  pl.BlockSpec((1, IDX_HEADS, 1), lambda i, pt: (i, 0, 0)),
                  pl.BlockSpec((1, LANES, IDX_DIM), lambda i, pt: (i, 0, 0)),
                  pl.BlockSpec(memory_space=pl.ANY)],
        out_specs=pl.BlockSpec((1, 1, past + LANES), lambda i, pt: (i, 0, 0)),
        scratch_shapes=[pltpu.VMEM((2, past, IDX_DIM), F32), pltpu.SemaphoreType.DMA((2,))],
    )
    return pl.pallas_call(
        kern, grid_spec=gs,
        out_shape=jax.ShapeDtypeStruct((b, 1, past + LANES), F32),
        compiler_params=_cparams("arbitrary"),
        name="dsa_scores",
    )(page_table, iq, iw, iknew_pad, cache_kidx)


def _dsa_threshold_kernel(s_ref, thr_ref, *, topk):
    s = s_ref[...]
    nb = s.shape[0]
    finite = s > -jnp.inf
    lo0 = jnp.min(jnp.where(finite, s, jnp.inf), axis=1, keepdims=True)
    hi0 = jnp.max(s, axis=1, keepdims=True)
    cnt0 = jnp.sum(jnp.where(finite, 1.0, 0.0), axis=1, keepdims=True)

    def count_ge(thr):
        return jnp.sum(jnp.where(s_ref[...] >= thr, 1.0, 0.0), axis=1, keepdims=True)

    thr = _bisect_threshold(count_ge, lo0, hi0, cnt0, topk)
    thr_ref[...] = jnp.broadcast_to(thr, (nb, LANES))


def dsa_threshold(scores, topk):
    b, l = scores.shape
    return pl.pallas_call(
        functools.partial(_dsa_threshold_kernel, topk=topk),
        grid=(1,),
        in_specs=[pl.BlockSpec((b, l), lambda i: (0, 0))],
        out_specs=pl.BlockSpec((b, LANES), lambda i: (0, 0)),
        out_shape=jax.ShapeDtypeStruct((b, LANES), F32),
        compiler_params=_cparams("arbitrary"),
        name="dsa_threshold",
    )(scores)


def _dsa_decode_kernel(pt_ref, q_ref, s_ref, thr_ref, knew_ref, vnew_ref, bias_ref, k_hbm, v_hbm, o_ref,
                       kbuf, vbuf, sem, *, n_pages):
    b = pl.program_id(0)
    nb = pl.num_programs(0)
    past = n_pages * PAGE_SIZE

    def copies(bb, slot):
        ck = _page_copies(pt_ref, bb, n_pages, k_hbm, kbuf.at[slot], sem.at[0, slot])
        cv = _page_copies(pt_ref, bb, n_pages, v_hbm, vbuf.at[slot], sem.at[1, slot])
        return ck, cv

    def start(bb, slot):
        ck, cv = copies(bb, slot)
        lax.fori_loop(0, n_pages, lambda p, c: (ck(p).start(), cv(p).start(), c)[2], 0)

    def wait(bb, slot):
        ck, cv = copies(bb, slot)
        lax.fori_loop(0, n_pages, lambda p, c: (ck(p).wait(), cv(p).wait(), c)[2], 0)

    slot = b % 2

    @pl.when(b == 0)
    def _():
        kbuf[:, past:past + LANES, :] = jnp.zeros((2, LANES, kbuf.shape[2]), F32)
        vbuf[:, past:past + LANES, :] = jnp.zeros((2, LANES, vbuf.shape[2]), F32)
        start(0, 0)

    @pl.when(b + 1 < nb)
    def _():
        start(b + 1, 1 - slot)

    kbuf[slot, past:past + 8, :] = knew_ref[0]
    vbuf[slot, past:past + 8, :] = vnew_ref[0]
    wait(b, slot)

    sel = s_ref[0] >= thr_ref[0][:, 0:1]
    outs = []
    for n in range(B_KV_HEADS):
        cols = slice(n * B_HEAD_DIM, (n + 1) * B_HEAD_DIM)
        kn = kbuf[slot, :, cols].astype(BF16)
        vn = vbuf[slot, :, cols].astype(BF16)
        qn = q_ref[0, n].astype(BF16)
        lg = _nt(qn, kn) * B_HEAD_DIM ** -0.5 + bias_ref[n]
        m = jnp.max(jnp.where(sel, lg, NEG), axis=1, keepdims=True)
        p = jnp.where(sel, jnp.exp(lg - m), 0.0)
        p = p / jnp.sum(p, axis=1, keepdims=True)
        outs.append(jnp.dot(p.astype(BF16), vn, preferred_element_type=F32))
    o_ref[0] = jnp.concatenate(outs, axis=0).astype(o_ref.dtype)


def dsa_decode(page_table, q8, scores, thr, knew8, vnew8, bias_rows, cache_k2, cache_v2):
    b, n_pages = page_table.shape
    past = n_pages * PAGE_SIZE
    l = past + LANES
    wkv = B_KV_HEADS * B_HEAD_DIM
    kern = functools.partial(_dsa_decode_kernel, n_pages=n_pages)
    gs = pltpu.PrefetchScalarGridSpec(
        num_scalar_prefetch=1,
        grid=(b,),
        in_specs=[pl.BlockSpec((1, B_KV_HEADS, 8, B_HEAD_DIM), lambda i, pt: (i, 0, 0, 0)),
                  pl.BlockSpec((1, 1, l), lambda i, pt: (i, 0, 0)),
                  pl.BlockSpec((1, 1, LANES), lambda i, pt: (i, 0, 0)),
                  pl.BlockSpec((1, 8, wkv), lambda i, pt: (i, 0, 0)),
                  pl.BlockSpec((1, 8, wkv), lambda i, pt: (i, 0, 0)),
                  pl.BlockSpec((B_KV_HEADS, 8, l), lambda i, pt: (0, 0, 0)),
                  pl.BlockSpec(memory_space=pl.ANY),
                  pl.BlockSpec(memory_space=pl.ANY)],
        out_specs=pl.BlockSpec((1, 2 * 8, B_HEAD_DIM), lambda i, pt: (i, 0, 0)),
        scratch_shapes=[pltpu.VMEM((2, l, wkv), F32), pltpu.VMEM((2, l, wkv), F32),
                        pltpu.SemaphoreType.DMA((2, 2))],
    )
    return pl.pallas_call(
        kern, grid_spec=gs,
        out_shape=jax.ShapeDtypeStruct((b, 2 * 8, B_HEAD_DIM), F32),
        compiler_params=_cparams("arbitrary"),
        name="dsa_decode",
    )(page_table, q8, scores, thr, knew8, vnew8, bias_rows, cache_k2, cache_v2)


def _merge_kernel(ya_ref, yb_ref, ga_ref, gb_ref, wa_ref, wb_ref, o_ref):
    a = _mm(ya_ref[...], wa_ref[...])
    bb = _mm(yb_ref[...], wb_ref[...])
    o_ref[...] = (jax.nn.sigmoid(ga_ref[...]) * a + jax.nn.sigmoid(gb_ref[...]) * bb).astype(o_ref.dtype)


def merge(ya, yb, z, ga_col, gb_col, wa, wb, *, tm, tn):
    m, k = ya.shape
    n = wa.shape[1]
    return pl.pallas_call(
        _merge_kernel,
        grid=(m // tm, n // tn),
        in_specs=[pl.BlockSpec((tm, k), lambda i, j: (i, 0)),
                  pl.BlockSpec((tm, k), lambda i, j: (i, 0)),
                  pl.BlockSpec((tm, tn), lambda i, j: (i, ga_col // tn + j)),
                  pl.BlockSpec((tm, tn), lambda i, j: (i, gb_col // tn + j)),
                  pl.BlockSpec((k, tn), lambda i, j: (0, j)),
                  pl.BlockSpec((k, tn), lambda i, j: (0, j))],
        out_specs=pl.BlockSpec((tm, tn), lambda i, j: (i, j)),
        out_shape=jax.ShapeDtypeStruct((m, n), wa.dtype),
        compiler_params=_cparams("parallel", "arbitrary"),
        name="merge",
    )(ya, yb, z, z, wa, wb)


def _cross_prompt_kernel(x_ref, g_ref, wq_ref, mk_ref, mv_ref, wo_ref, o_ref):
    x = x_ref[...]
    ms = jnp.mean(x * x, axis=-1, keepdims=True)
    h = (x * lax.rsqrt(ms + EPS) * g_ref[...]).astype(BF16)
    q = jnp.dot(h, wq_ref[...], preferred_element_type=F32)
    outs = []
    for hh in range(X_HEADS):
        cols = slice(hh * X_HEAD_DIM, (hh + 1) * X_HEAD_DIM)
        lg = _nt(q[:, cols].astype(BF16), mk_ref[:, cols]) * X_HEAD_DIM ** -0.5
        mx = jnp.max(lg, axis=1, keepdims=True)
        p = jnp.exp(lg - mx)
        p = p / jnp.sum(p, axis=1, keepdims=True)
        outs.append(jnp.dot(p.astype(BF16), mv_ref[:, cols], preferred_element_type=F32).astype(BF16))
    att = jnp.concatenate(outs, axis=1)
    o_ref[...] = x + jnp.dot(att, wo_ref[...], preferred_element_type=F32)


def cross_prompt(x, g, wq, mk, mv, wo, *, tm):
    m, d = x.shape
    full = lambda a: pl.BlockSpec(a.shape, lambda i: (0,) * a.ndim)
    g2 = g.reshape(1, d)
    return pl.pallas_call(
        _cross_prompt_kernel,
        grid=(m // tm,),
        in_specs=[pl.BlockSpec((tm, d), lambda i: (i, 0)), full(g2), full(wq), full(mk), full(mv), full(wo)],
        out_specs=pl.BlockSpec((tm, d), lambda i: (i, 0)),
        out_shape=jax.ShapeDtypeStruct((m, d), F32),
        compiler_params=_cparams("parallel"),
        name="cross_prompt",
    )(x, g2, wq, mk, mv, wo)


def _cross_step_kernel(q_ref, mk_ref, mv_ref, o_ref):
    r16 = lambda a: a.astype(BF16).astype(F32)
    q = r16(q_ref[0])
    outs = []
    for hh in range(X_HEADS):
        cols = slice(hh * X_HEAD_DIM, (hh + 1) * X_HEAD_DIM)
        kh = r16(mk_ref[0, :, cols])
        vh = r16(mv_ref[0, :, cols])
        lg = jnp.sum(kh * q[:, cols], axis=1, keepdims=True) * X_HEAD_DIM ** -0.5
        mx = jnp.max(lg, axis=0, keepdims=True)
        p = jnp.exp(lg - mx)
        p = r16(p / jnp.sum(p, axis=0, keepdims=True))
        outs.append(jnp.sum(p * vh, axis=0, keepdims=True))
    o_ref[0] = jnp.concatenate(outs, axis=1).astype(o_ref.dtype)


def cross_step(q, mk, mv):
    b, w = q.shape
    mem = mk.shape[1]
    return pl.pallas_call(
        _cross_step_kernel,
        grid=(b,),
        in_specs=[pl.BlockSpec((1, 1, w), lambda i: (i, 0, 0)),
                  pl.BlockSpec((1, mem, w), lambda i: (i, 0, 0)),
                  pl.BlockSpec((1, mem, w), lambda i: (i, 0, 0))],
        out_specs=pl.BlockSpec((1, 1, w), lambda i: (i, 0, 0)),
        out_shape=jax.ShapeDtypeStruct((b, 1, w), F32),
        compiler_params=_cparams("arbitrary"),
        name="cross_step",
    )(q.reshape(b, 1, w), mk, mv)


def _router_kernel(x_ref, g_ref, w_ref, b_ref, hf_ref, route_ref):
    x = x_ref[...]
    ms = jnp.mean(x * x, axis=-1, keepdims=True)
    hf = x * lax.rsqrt(ms + EPS) * g_ref[...]
    hf_ref[...] = hf
    lg = _mm(hf, w_ref[...]) + b_ref[...]
    tm = lg.shape[0]
    lane = lax.broadcasted_iota(jnp.int32, (tm, LANES), 1)
    big = jnp.int32(LANES)
    is_g = lane < N_GROUPS
    gmax = jnp.max(jnp.where(is_g, lg, -jnp.inf), axis=1, keepdims=True)
    grp = jnp.min(jnp.where(is_g & (lg == gmax), lane, big), axis=1, keepdims=True)
    p_grp = 1.0 / jnp.sum(jnp.where(is_g, jnp.exp(lg - gmax), 0.0), axis=1, keepdims=True)
    e_lo = N_GROUPS + grp * EXP_PER_GROUP
    in_g = (lane >= e_lo) & (lane < e_lo + EXP_PER_GROUP)
    v1 = jnp.max(jnp.where(in_g, lg, -jnp.inf), axis=1, keepdims=True)
    i1 = jnp.min(jnp.where(in_g & (lg == v1), lane, big), axis=1, keepdims=True)
    rest = in_g & (lane != i1)
    v2 = jnp.max(jnp.where(rest, lg, -jnp.inf), axis=1, keepdims=True)
    i2 = jnp.min(jnp.where(rest & (lg == v2), lane, big), axis=1, keepdims=True)
    e2 = jnp.exp(v2 - v1)
    g1 = p_grp / (1.0 + e2)
    g2 = p_grp * e2 / (1.0 + e2)
    r = jnp.where(lane == 0, (i1 - N_GROUPS).astype(F32),
                  jnp.where(lane == 1, (i2 - N_GROUPS).astype(F32),
                            jnp.where(lane == 2, g1, jnp.where(lane == 3, g2, 0.0))))
    route_ref[...] = r


def router(x, g, w_pad, b_pad, *, tm):
    m, d = x.shape
    return pl.pallas_call(
        _router_kernel,
        grid=(pl.cdiv(m, tm),),
        in_specs=[pl.BlockSpec((tm, d), lambda i: (i, 0)),
                  pl.BlockSpec((1, d), lambda i: (0, 0)),
                  pl.BlockSpec((d, LANES), lambda i: (0, 0)),
                  pl.BlockSpec((1, LANES), lambda i: (0, 0))],
        out_specs=[pl.BlockSpec((tm, d), lambda i: (i, 0)),
                   pl.BlockSpec((tm, LANES), lambda i: (i, 0))],
        out_shape=[jax.ShapeDtypeStruct((m, d), F32), jax.ShapeDtypeStruct((m, LANES), F32)],
        compiler_params=_cparams("parallel"),
        name="router",
    )(x, g.reshape(1, d), w_pad, b_pad)


def _moe_kernel(be_ref, tok_ref, hf_hbm, wg_ref, wu_ref, wd_ref, o_ref, xbuf, wg_s, wu_s, wd_s, sem):
    b = pl.program_id(0)
    nb = pl.num_programs(0)
    blk = xbuf.shape[1]

    def row_copy(bb, slot, r):
        tok = tok_ref[bb * blk + r]
        return pltpu.make_async_copy(hf_hbm.at[pl.ds(tok, 1)], xbuf.at[slot, pl.ds(r, 1)], sem.at[slot])

    def start(bb, slot):
        lax.fori_loop(0, blk, lambda r, c: (row_copy(bb, slot, r).start(), c)[1], 0)

    def wait(bb, slot):
        lax.fori_loop(0, blk, lambda r, c: (row_copy(bb, slot, r).wait(), c)[1], 0)

    slot = b % 2

    @pl.when(b == 0)
    def _():
        start(0, 0)

    @pl.when(b + 1 < nb)
    def _():
        start(b + 1, 1 - slot)

    changed = jnp.logical_or(b == 0, be_ref[b] != be_ref[jnp.maximum(b - 1, 0)])

    @pl.when(changed)
    def _():
        wg_s[...] = wg_ref[0].astype(BF16)
        wu_s[...] = wu_ref[0].astype(BF16)
        wd_s[...] = wd_ref[0].astype(BF16)

    wait(b, slot)
    x = xbuf[slot].astype(BF16)
    gg = jnp.dot(x, wg_s[...], preferred_element_type=F32)
    uu = jnp.dot(x, wu_s[...], preferred_element_type=F32)
    a = (_silu(gg) * uu).astype(BF16)
    o_ref[...] = jnp.dot(a, wd_s[...], preferred_element_type=F32)


def moe_experts(block_e, slot_tok, hf, w_g, w_u, w_d, *, blk):
    n_blocks = block_e.shape[0]
    d = hf.shape[1]
    ff = w_g.shape[2]
    gs = pltpu.PrefetchScalarGridSpec(
        num_scalar_prefetch=2,
        grid=(n_blocks,),
        in_specs=[pl.BlockSpec(memory_space=pl.ANY),
                  pl.BlockSpec((1, d, ff), lambda i, be, tk: (be[i], 0, 0)),
                  pl.BlockSpec((1, d, ff), lambda i, be, tk: (be[i], 0, 0)),
                  pl.BlockSpec((1, ff, d), lambda i, be, tk: (be[i], 0, 0))],
        out_specs=pl.BlockSpec((blk, d), lambda i, be, tk: (i, 0)),
        scratch_shapes=[pltpu.VMEM((2, blk, d), F32),
                        pltpu.VMEM((d, ff), BF16), pltpu.VMEM((d, ff), BF16), pltpu.VMEM((ff, d), BF16),
                        pltpu.SemaphoreType.DMA((2,))],
    )
    return pl.pallas_call(
        _moe_kernel, grid_spec=gs,
        out_shape=jax.ShapeDtypeStruct((n_blocks * blk, d), F32),
        compiler_params=_cparams("arbitrary"),
        name="moe_experts",
    )(block_e, slot_tok, hf, w_g, w_u, w_d)


def _combine_kernel(dest_ref, x_ref, route_ref, gf_ref, yb_hbm, o_ref, ybuf, sem):
    i = pl.program_id(0)
    nt = pl.num_programs(0)
    tm = x_ref.shape[0]

    def row_copy(ii, slot, r):
        base = (ii * tm + r) * EXPERT_TOPK
        c0 = pltpu.make_async_copy(yb_hbm.at[pl.ds(dest_ref[base], 1)], ybuf.at[slot, 0, pl.ds(r, 1)], sem.at[slot])
        c1 = pltpu.make_async_copy(yb_hbm.at[pl.ds(dest_ref[base + 1], 1)], ybuf.at[slot, 1, pl.ds(r, 1)], sem.at[slot])
        return c0, c1

    def start(ii, slot):
        def body(r, c):
            c0, c1 = row_copy(ii, slot, r)
            c0.start()
            c1.start()
            return c
        lax.fori_loop(0, tm, body, 0)

    def wait(ii, slot):
        def body(r, c):
            c0, c1 = row_copy(ii, slot, r)
            c0.wait()
            c1.wait()
            return c
        lax.fori_loop(0, tm, body, 0)

    slot = i % 2

    @pl.when(i == 0)
    def _():
        start(0, 0)

    @pl.when(i + 1 < nt)
    def _():
        start(i + 1, 1 - slot)

    wait(i, slot)
    route = route_ref[...]
    x = x_ref[...] + route[:, 2:3] * ybuf[slot, 0] + route[:, 3:4] * ybuf[slot, 1]
    ms = jnp.mean(x * x, axis=-1, keepdims=True)
    o_ref[...] = x * lax.rsqrt(ms + EPS) * gf_ref[...]


def combine(dest_pad, x, route, gf, yb, *, tm):
    m, d = x.shape
    gs = pltpu.PrefetchScalarGridSpec(
        num_scalar_prefetch=1,
        grid=(pl.cdiv(m, tm),),
        in_specs=[pl.BlockSpec((tm, d), lambda i, ds_: (i, 0)),
                  pl.BlockSpec((tm, LANES), lambda i, ds_: (i, 0)),
                  pl.BlockSpec((1, d), lambda i, ds_: (0, 0)),
                  pl.BlockSpec(memory_space=pl.ANY)],
        out_specs=pl.BlockSpec((tm, d), lambda i, ds_: (i, 0)),
        scratch_shapes=[pltpu.VMEM((2, EXPERT_TOPK, tm, d), F32), pltpu.SemaphoreType.DMA((2,))],
    )
    return pl.pallas_call(
        _combine_kernel, grid_spec=gs,
        out_shape=jax.ShapeDtypeStruct((m, d), F32),
        compiler_params=_cparams("arbitrary"),
        name="combine",
    )(dest_pad, x, route, gf.reshape(1, d), yb)


def _t5_bucket(dist):
    dist = jnp.asarray(dist, jnp.int32)
    max_exact = REL_BUCKETS // 2
    dist_f = jnp.maximum(dist, 1).astype(F32)
    large = max_exact + (jnp.log(dist_f / max_exact) / math.log(REL_MAX_DIST / max_exact)
                         * (REL_BUCKETS - max_exact)).astype(jnp.int32)
    large = jnp.minimum(large, REL_BUCKETS - 1)
    return jnp.where(dist < max_exact, dist, large)


def _bias_tables(rel_bias, past):
    r = np.arange(LANES)
    diff = r[:, None] - r[None, :]
    buckets = jnp.stack([_t5_bucket(np.maximum(diff, 0)),
                         _t5_bucket(np.maximum(diff + LANES, 0)),
                         _t5_bucket(np.full((LANES, LANES), 2 * LANES))])
    tiles = jnp.transpose(rel_bias.astype(F32)[buckets], (3, 0, 1, 2))
    dist = np.maximum(past - np.arange(past + LANES), 0)
    rows = rel_bias.astype(F32)[_t5_bucket(dist)].T
    rows = rows.reshape(B_KV_HEADS, B_GROUP, past + LANES)
    rows = jnp.concatenate([rows, jnp.zeros_like(rows)], axis=1)
    return tiles, rows


def _dispatch(eid, n_tokens, blk):
    a = eid.shape[0]
    n_blocks = -(-(a + N_EXPERTS * (blk - 1)) // blk)
    rows = n_blocks * blk
    tok = jnp.arange(a, dtype=jnp.int32) // EXPERT_TOPK
    order = jnp.argsort(eid)
    e_sorted = eid[order]
    counts = jnp.bincount(eid, length=N_EXPERTS)
    starts = jnp.cumsum(counts) - counts
    padded = (counts + blk - 1) // blk * blk
    pad_end = jnp.cumsum(padded)
    pad_start = pad_end - padded
    dest_sorted = pad_start[e_sorted] + jnp.arange(a, dtype=jnp.int32) - starts[e_sorted]
    dest = jnp.zeros((a,), jnp.int32).at[order].set(dest_sorted.astype(jnp.int32))
    slot_tok = jnp.zeros((rows,), jnp.int32).at[dest].set(tok)
    block_e = jnp.minimum(jnp.searchsorted(pad_end, jnp.arange(n_blocks) * blk, side='right'), N_EXPERTS - 1)
    return dest, slot_tok, block_e.astype(jnp.int32)


def kernel(x_prompt, x_sample, mem_prompt, cache_k, cache_v, cache_kidx, page_table, state_hgrn, cache_mem_k,
           cache_mem_v, norm_mix, w_in, hgrn_lb_logits, hgrn_norm, w_branch_a, w_branch_b, w_out, norm_cross, w_xq,
           w_xk, w_xv, w_xo, norm_ffn, w_router_group, b_router_group, w_router_expert, b_router_expert, w_exp_gate,
           w_exp_up, w_exp_down, rel_bias, norm_final):
    assert w_in.shape[0] == 1, "single-layer step"
    l = 0
    drop0 = lambda a: a.reshape(a.shape[1:])
    bp, t, d = x_prompt.shape
    db = x_sample.shape[0]
    past = page_table.shape[1] * PAGE_SIZE
    xp = x_prompt.reshape(bp * t, d)
    xs = x_sample.reshape(db, d)

    wi = drop0(w_in)
    o = np.cumsum((0,) + (1024, 1024, 1024, 1024, 1024, 256, 256, 1024, 16, 64, 2048, 2048))
    seg = lambda i: wi[:, o[i]:o[i + 1]]
    w_cat = jnp.concatenate([seg(0), seg(1), seg(2), seg(3), seg(4), seg(7), seg(10), seg(11), seg(5), seg(6)],
                            axis=1).astype(BF16)
    w_small = jnp.pad(jnp.concatenate([seg(8), seg(9)], axis=1), ((0, 0), (0, LANES - 80))).astype(BF16)
    wa, wb, wo = w_branch_a[l].astype(BF16), w_branch_b[l].astype(BF16), w_out[l].astype(BF16)
    wxq, wxk, wxv, wxo = (w_xq[l].astype(BF16), w_xk[l].astype(BF16), w_xv[l].astype(BF16), w_xo[l].astype(BF16))
    w_route = jnp.pad(jnp.concatenate([w_router_group[l], w_router_expert[l]], axis=1),
                      ((0, 0), (0, LANES - N_GROUPS - N_EXPERTS))).astype(BF16)
    b_route = jnp.pad(jnp.concatenate([b_router_group[l], b_router_expert[l]]),
                      (0, LANES - N_GROUPS - N_EXPERTS)).reshape(1, LANES)
    bias_tiles, bias_rows = _bias_tables(rel_bias, past)

    zp = norm_matmul(xp, norm_mix[l], w_cat, tm=1024, tn=512)
    zps = norm_matmul(xp, norm_mix[l], w_small, tm=1024, tn=LANES)
    kp = zp[:, C_BK:C_BK + 256]
    vp = zp[:, C_BV:C_BV + 256]
    ikp = zps[:, IDX_HEADS:IDX_HEADS + IDX_DIM]
    ya_p, st_p = hgrn_prompt(zp, hgrn_lb_logits, hgrn_norm[l])
    yb_p = dsa_prompt(zp, zps, ikp.astype(BF16), kp.astype(BF16), vp.astype(BF16), bias_tiles)
    mg_p = merge(ya_p, yb_p, zp, C_GA, C_GB, wa, wb, tm=512, tn=512)
    x1p = matmul(mg_p, wo, xp, tm=512, tn=512)
    memp = mem_prompt.reshape(-1, d)
    mk = matmul(memp, wxk, tm=memp.shape[0], tn=512)
    mv = matmul(memp, wxv, tm=memp.shape[0], tn=512)
    x2p = cross_prompt(x1p, norm_cross[l], wxq, mk.astype(BF16), mv.astype(BF16), wxo, tm=512)

    zs = norm_matmul(xs, norm_mix[l], w_cat, tm=db, tn=512)
    zss = norm_matmul(xs, norm_mix[l], w_small, tm=db, tn=LANES)
    ks = zs[:, C_BK:C_BK + 256]
    vs = zs[:, C_BV:C_BV + 256]
    iks = zss[:, IDX_HEADS:IDX_HEADS + IDX_DIM]
    ya_s, st_s = hgrn_step(zs[:, :4 * A_HEADS * A_DK], hgrn_lb_logits, hgrn_norm[l], drop0(state_hgrn))
    iq_s = zs[:, C_IQ:C_IQ + IDX_HEADS * IDX_DIM].reshape(db, IDX_HEADS, IDX_DIM)
    iw_s = zss[:, :IDX_HEADS].reshape(db, IDX_HEADS, 1)
    iknew_pad = jnp.pad(iks[:, None, :], ((0, 0), (0, LANES - 1), (0, 0)))
    scores = dsa_scores(page_table, iq_s, iw_s, iknew_pad, drop0(cache_kidx)).reshape(db, past + LANES)
    topk_s = min(TOPK_MAX, (past + 1) // 4)
    thr = dsa_threshold(scores, topk_s)
    q8 = jnp.pad(zs[:, C_BQ:C_BQ + B_HEADS * B_HEAD_DIM].reshape(db, B_KV_HEADS, B_GROUP, B_HEAD_DIM),
                 ((0, 0), (0, 0), (0, 8 - B_GROUP), (0, 0)))
    knew8 = jnp.pad(ks[:, None, :], ((0, 0), (0, 7), (0, 0)))
    vnew8 = jnp.pad(vs[:, None, :], ((0, 0), (0, 7), (0, 0)))
    n_pool = cache_k.shape[1]
    ob = dsa_decode(page_table, q8, scores.reshape(db, 1, -1), thr.reshape(db, 1, LANES), knew8, vnew8, bias_rows,
                    cache_k.reshape(n_pool, PAGE_SIZE, -1), cache_v.reshape(n_pool, PAGE_SIZE, -1))
    yb_s = ob.reshape(db, B_KV_HEADS, 8, B_HEAD_DIM)[:, :, :B_GROUP].reshape(db, B_HEADS * B_HEAD_DIM)
    mg_s = merge(ya_s.reshape(db, -1), yb_s, zs, C_GA, C_GB, wa, wb, tm=db, tn=512)
    x1s = matmul(mg_s, wo, xs, tm=db, tn=512)
    qx_s = norm_matmul(x1s, norm_cross[l], wxq, tm=db, tn=512)
    mem = cache_mem_k.shape[2]
    att_s = cross_step(qx_s, cache_mem_k.reshape(db, mem, -1), cache_mem_v.reshape(db, mem, -1))
    x2s = matmul(att_s.reshape(db, -1), wxo, x1s, tm=db, tn=512)

    x2 = jnp.concatenate([x2p, x2s], axis=0)
    n = x2.shape[0]
    hf, route = router(x2, norm_ffn[l], w_route, b_route, tm=256)
    eid = route[:, :EXPERT_TOPK].astype(jnp.int32).reshape(-1)
    a = n * EXPERT_TOPK
    blk = min(MOE_BLOCK, max(8, a // N_EXPERTS))
    dest, slot_tok, block_e = _dispatch(eid, n, blk)
    yb = moe_experts(block_e, slot_tok, hf, drop0(w_exp_gate), drop0(w_exp_up), drop0(w_exp_down), blk=blk)
    tmc = 128
    n_pad = -(-n // tmc) * tmc
    dest_pad = jnp.pad(dest, (0, (n_pad - n) * EXPERT_TOPK))
    y = combine(dest_pad, x2, route, norm_final, yb, tm=tmc)

    y_prompt = y[:bp * t].reshape(bp, t, d)
    y_sample = y[bp * t:].reshape(db, 1, d)
    return (y_prompt, y_sample,
            kp.reshape(1, bp, t, B_KV_HEADS, B_HEAD_DIM), vp.reshape(1, bp, t, B_KV_HEADS, B_HEAD_DIM),
            ikp.reshape(1, bp, t, IDX_DIM),
            jnp.swapaxes(st_p, 1, 2).reshape(1, bp, A_HEADS, A_DK, A_DV),
            mk.reshape(1, bp, -1, X_HEADS, X_HEAD_DIM), mv.reshape(1, bp, -1, X_HEADS, X_HEAD_DIM),
            ks.reshape(1, db, 1, B_KV_HEADS, B_HEAD_DIM), vs.reshape(1, db, 1, B_KV_HEADS, B_HEAD_DIM),
            iks.reshape(1, db, 1, IDX_DIM),
            st_s.reshape(1, db, A_HEADS, A_DK, A_DV))
```

```python
import functools
import math

import jax
import jax.numpy as jnp
import numpy as np
from jax import lax
from jax.experimental import pallas as pl
from jax.experimental.pallas import tpu as pltpu

F32 = jnp.float32
BF16 = jnp.bfloat16
EPS = 1e-6

D_MODEL = 2048
A_HEADS, A_DK, A_DV = 8, 128, 128
B_HEADS, B_KV_HEADS, B_HEAD_DIM = 8, 2, 128
B_GROUP = B_HEADS // B_KV_HEADS
IDX_HEADS, IDX_DIM = 16, 64
TOPK_MAX = 256
PAGE_SIZE = 128
REL_BUCKETS, REL_MAX_DIST = 32, 128
X_HEADS, X_HEAD_DIM = 4, 128
N_GROUPS, EXP_PER_GROUP = 4, 8
N_EXPERTS = N_GROUPS * EXP_PER_GROUP
EXPERT_TOPK = 2
EXPERT_FF = 512
MOE_BLOCK = 128

LANES = 128
VMEM_LIMIT = 56 * 1024 * 1024

NEG = -1e30

C_AQ, C_AF, C_AI, C_AG, C_BQ, C_IQ, C_GA, C_GB, C_BK, C_BV = (
    0, 1024, 2048, 3072, 4096, 5120, 6144, 8192, 10240, 10496)
NZ = 10752


def _cparams(*sem):
    return pltpu.CompilerParams(dimension_semantics=sem, vmem_limit_bytes=VMEM_LIMIT)


def _silu(x):
    return x * jax.nn.sigmoid(x)


def _nt(a, b):
    return lax.dot_general(a, b, (((1,), (1,)), ((), ())), preferred_element_type=F32)


def _nt_f32(a, b):
    return lax.dot_general(a, b, (((1,), (1,)), ((), ())), preferred_element_type=F32,
                           precision=lax.Precision.HIGHEST)


def _mm(a, w):
    if w.dtype == F32:
        return jnp.dot(a.astype(F32), w, preferred_element_type=F32, precision=lax.Precision.HIGHEST)
    return jnp.dot(a.astype(BF16), w, preferred_element_type=F32)


def _norm_matmul_kernel(x_ref, g_ref, w_ref, o_ref, h_ref):
    @pl.when(pl.program_id(1) == 0)
    def _():
        x = x_ref[...]
        ms = jnp.mean(x * x, axis=-1, keepdims=True)
        h_ref[...] = (x * lax.rsqrt(ms + EPS) * g_ref[...]).astype(h_ref.dtype)

    o_ref[...] = _mm(h_ref[...], w_ref[...])


def norm_matmul(x, g, w, *, tm, tn):
    m, k = x.shape
    n = w.shape[1]
    return pl.pallas_call(
        _norm_matmul_kernel,
        grid=(m // tm, pl.cdiv(n, tn)),
        in_specs=[pl.BlockSpec((tm, k), lambda i, j: (i, 0)),
                  pl.BlockSpec((1, k), lambda i, j: (0, 0)),
                  pl.BlockSpec((k, tn), lambda i, j: (0, j))],
        out_specs=pl.BlockSpec((tm, tn), lambda i, j: (i, j)),
        out_shape=jax.ShapeDtypeStruct((m, n), F32),
        scratch_shapes=[pltpu.VMEM((tm, k), w.dtype)],
        compiler_params=_cparams("parallel", "arbitrary"),
        name="norm_matmul",
    )(x, g.reshape(1, k), w)


def _matmul_res_kernel(x_ref, w_ref, r_ref, o_ref):
    o_ref[...] = r_ref[...] + _mm(x_ref[...], w_ref[...])


def _matmul_kernel(x_ref, w_ref, o_ref):
    o_ref[...] = _mm(x_ref[...], w_ref[...])


def matmul(x, w, res=None, *, tm, tn):
    m, k = x.shape
    n = w.shape[1]
    in_specs = [pl.BlockSpec((tm, k), lambda i, j: (i, 0)),
                pl.BlockSpec((k, tn), lambda i, j: (0, j))]
    args = [x, w]
    kern = _matmul_kernel
    if res is not None:
        in_specs.append(pl.BlockSpec((tm, tn), lambda i, j: (i, j)))
        args.append(res)
        kern = _matmul_res_kernel
    return pl.pallas_call(
        kern,
        grid=(m // tm, n // tn),
        in_specs=in_specs,
        out_specs=pl.BlockSpec((tm, tn), lambda i, j: (i, j)),
        out_shape=jax.ShapeDtypeStruct((m, n), F32),
        compiler_params=_cparams("parallel", "arbitrary"),
        name="matmul",
    )(*args)


HG_TB = 128
HG_C = 16


def _hgrn_prompt_kernel(aq_ref, af_ref, ai_ref, ag_ref, lbl_ref, ng_ref, ya_ref, st_out_ref,
                        st_ref, q_s, k_s, g_s, v_s):
    t = pl.program_id(0)

    @pl.when(t == 0)
    def _():
        st_ref[...] = jnp.zeros_like(st_ref)

    lbl = lbl_ref[...]
    mx = jnp.max(lbl, axis=0, keepdims=True)
    ex = jnp.exp(lbl - mx)
    lb = ex[0:1, :] / jnp.sum(ex, axis=0, keepdims=True)

    f = lb + (1.0 - lb) * jax.nn.sigmoid(af_ref[...])
    logf = jnp.log(f)
    row = lax.broadcasted_iota(jnp.int32, (HG_TB, HG_TB), 0)
    col = lax.broadcasted_iota(jnp.int32, (HG_TB, HG_TB), 1)
    tri = jnp.where((row // HG_C == col // HG_C) & (col <= row), 1.0, 0.0).astype(F32)
    g_s[...] = jnp.dot(tri, logf, preferred_element_type=F32, precision=lax.Precision.HIGHEST)
    q_s[...] = _silu(aq_ref[...])
    k_s[...] = 1.0 - f
    v_s[...] = ai_ref[...]

    sub = lax.broadcasted_iota(jnp.int32, (HG_C, A_DK), 0)
    ng = ng_ref[...]

    def chunk(c, carry):
        r0 = pl.multiple_of(c * HG_C, HG_C)
        rows = pl.ds(r0, HG_C)
        for h in range(A_HEADS):
            cols = slice(h * A_DK, (h + 1) * A_DK)
            g = g_s[rows, cols]
            qh = q_s[rows, cols]
            kh = k_s[rows, cols]
            vh = v_s[rows, cols]
            o = jnp.zeros((HG_C, A_DV), F32)
            for tt in range(HG_C):
                d = g[tt:tt + 1, :] - g
                e = jnp.exp(jnp.where(sub <= tt, d, -jnp.inf))
                p = e * (qh[tt:tt + 1, :] * kh)
                a_col = jnp.sum(p, axis=1, keepdims=True)
                o_row = jnp.sum(a_col * vh, axis=0, keepdims=True)
                o = jnp.where(sub == tt, o_row, o)
            st = st_ref[h]
            qg = (qh * jnp.exp(g)).astype(BF16)
            o = o + _nt(qg, st.astype(BF16))
            g_last = g[HG_C - 1:HG_C, :]
            kt = (kh * jnp.exp(g_last - g)).astype(BF16)
            upd = lax.dot_general(vh.astype(BF16), kt, (((0,), (0,)), ((), ())),
                                  preferred_element_type=F32)
            st_ref[h] = st * jnp.exp(g_last) + upd
            on = o * lax.rsqrt(jnp.mean(o * o, axis=-1, keepdims=True) + EPS) * ng
            ya_ref[rows, cols] = (on * _silu(ag_ref[rows, cols])).astype(ya_ref.dtype)
        return carry

    lax.fori_loop(0, HG_TB // HG_C, chunk, 0)

    @pl.when(t == pl.num_programs(0) - 1)
    def _():
        st_out_ref[...] = st_ref[...]


def hgrn_prompt(z, lb_logits, norm_g):
    m = z.shape[0]
    w = A_HEADS * A_DK

    def zspec(cb):
        return pl.BlockSpec((HG_TB, w), lambda t, cb=cb: (t, cb))

    return pl.pallas_call(
        _hgrn_prompt_kernel,
        grid=(m // HG_TB,),
        in_specs=[zspec(C_AQ // w), zspec(C_AF // w), zspec(C_AI // w), zspec(C_AG // w),
                  pl.BlockSpec(lb_logits.shape, lambda t: (0, 0)),
                  pl.BlockSpec((1, A_DV), lambda t: (0, 0))],
        out_specs=[pl.BlockSpec((HG_TB, w), lambda t: (t, 0)),
                   pl.BlockSpec((A_HEADS, A_DV, A_DK), lambda t: (0, 0, 0))],
        out_shape=[jax.ShapeDtypeStruct((m, w), BF16),
                   jax.ShapeDtypeStruct((A_HEADS, A_DV, A_DK), F32)],
        scratch_shapes=[pltpu.VMEM((A_HEADS, A_DV, A_DK), F32)] + [pltpu.VMEM((HG_TB, w), F32)] * 4,
        compiler_params=_cparams("arbitrary"),
        name="hgrn_prompt",
    )(z, z, z, z, lb_logits, norm_g.reshape(1, A_DV))


def _hgrn_step_kernel(z_ref, lbl_ref, ng_ref, s_ref, ya_ref, s_out_ref):
    lbl = lbl_ref[...]
    mx = jnp.max(lbl, axis=0, keepdims=True)
    ex = jnp.exp(lbl - mx)
    lb = ex[0:1, :] / jnp.sum(ex, axis=0, keepdims=True)
    z = z_ref[0]
    w = A_HEADS * A_DK
    q = _silu(z[:, 0:w])
    f = lb + (1.0 - lb) * jax.nn.sigmoid(z[:, w:2 * w])
    kk = 1.0 - f
    v = z[:, 2 * w:3 * w]
    ag = z[:, 3 * w:4 * w]
    rows = []
    for h in range(A_HEADS):
        cols = slice(h * A_DK, (h + 1) * A_DK)
        rows += [f[:, cols], kk[:, cols], q[:, cols]]
    rows.append(jnp.zeros((LANES - 3 * A_HEADS, A_DK), F32))
    xt = jnp.concatenate(rows, axis=0).T
    ng = ng_ref[...]
    r16 = lambda a: a.astype(BF16).astype(F32)
    outs = []
    for h in range(A_HEADS):
        cols = slice(h * A_DV, (h + 1) * A_DV)
        fcol = xt[:, 3 * h:3 * h + 1]
        kcol = xt[:, 3 * h + 1:3 * h + 2]
        qcol = xt[:, 3 * h + 2:3 * h + 3]
        s_old = s_ref[0, h]
        s_out_ref[0, h] = fcol * s_old + kcol * v[:, cols]
        o = (jnp.sum(r16(qcol * fcol) * r16(s_old), axis=0, keepdims=True)
             + jnp.sum(qcol * kcol, axis=0, keepdims=True) * v[:, cols])
        on = o * lax.rsqrt(jnp.mean(o * o, axis=-1, keepdims=True) + EPS) * ng
        outs.append(on * _silu(ag[:, cols]))
    ya_ref[0] = jnp.concatenate(outs, axis=1).astype(ya_ref.dtype)


def hgrn_step(z4, lb_logits, norm_g, state):
    b = z4.shape[0]
    w = A_HEADS * A_DK
    return pl.pallas_call(
        _hgrn_step_kernel,
        grid=(b,),
        in_specs=[pl.BlockSpec((1, 1, 4 * w), lambda i: (i, 0, 0)),
                  pl.BlockSpec(lb_logits.shape, lambda i: (0, 0)),
                  pl.BlockSpec((1, A_DV), lambda i: (0, 0)),
                  pl.BlockSpec((1, A_HEADS, A_DK, A_DV), lambda i: (i, 0, 0, 0))],
        out_specs=[pl.BlockSpec((1, 1, w), lambda i: (i, 0, 0)),
                   pl.BlockSpec((1, A_HEADS, A_DK, A_DV), lambda i: (i, 0, 0, 0))],
        out_shape=[jax.ShapeDtypeStruct((b, 1, w), F32),
                   jax.ShapeDtypeStruct(state.shape, F32)],
        compiler_params=_cparams("arbitrary"),
        name="hgrn_step",
    )(z4.reshape(b, 1, 4 * w), lb_logits, norm_g.reshape(1, A_DV), state)


BISECT_MAX_ITERS = 48


def _bisect_threshold(count_ge, lo, hi, cnt_lo, topk):
    kf = float(topk)

    def cond(c):
        return jnp.logical_and(c[0] < BISECT_MAX_ITERS, c[-1] > 0.0)

    def body(c):
        it, lo, hi, cl, ch, _ = c
        mid = 0.5 * lo + 0.5 * hi
        t_int = hi - (hi - lo) * ((kf - ch) / jnp.maximum(cl - ch, 1.0))
        ok = jnp.logical_and(it % 2 == 0, jnp.logical_and(t_int > lo, t_int < hi))
        t = jnp.where(ok, t_int, mid)
        cm = count_ge(t)
        ge = cm >= kf
        lo = jnp.where(ge, t, lo)
        cl = jnp.where(ge, cm, cl)
        hi = jnp.where(ge, hi, t)
        ch = jnp.where(ge, ch, cm)
        busy = jnp.max(jnp.where(cl > kf, 1.0, 0.0))
        return it + 1, lo, hi, cl, ch, busy

    busy0 = jnp.max(jnp.where(cnt_lo > kf, 1.0, 0.0))
    out = lax.while_loop(cond, body, (jnp.int32(0), lo, hi, cnt_lo, jnp.ones_like(cnt_lo), busy0))
    return out[1]


DSA_QB = 128
DSA_W = 512
DSA_W3 = 1024


def _dsa_prompt_kernel(iq_ref, bq_ref, iw_ref, kidx_ref, k_ref, v_ref, bias_ref, o_ref,
                       score_s, qih_s, qs_s, wb_s, m_s, l_s, acc_s, *, topk):
    i = pl.program_id(0)
    nsub = DSA_W // LANES
    nsub3 = DSA_W3 // LANES
    nch3 = (i * DSA_QB + DSA_QB + DSA_W3 - 1) // DSA_W3
    nch = nch3 * (DSA_W3 // DSA_W)
    qpos = i * DSA_QB + lax.broadcasted_iota(jnp.int32, (DSA_QB, 1), 0)

    iw = iw_ref[...]
    wscale = IDX_DIM ** -0.5 * IDX_HEADS ** -0.5
    for h in range(IDX_HEADS):
        qih_s[h] = iq_ref[:, h * IDX_DIM:(h + 1) * IDX_DIM].astype(BF16)
        wb_s[h] = jnp.broadcast_to(iw[:, h:h + 1] * wscale, (DSA_QB, LANES))
    for h in range(B_HEADS):
        qs_s[h // B_GROUP, (h % B_GROUP) * DSA_QB:(h % B_GROUP + 1) * DSA_QB, :] = (
            bq_ref[:, h * B_HEAD_DIM:(h + 1) * B_HEAD_DIM] * B_HEAD_DIM ** -0.5).astype(BF16)

    def p1(c, carry):
        c0 = pl.multiple_of(c * DSA_W, DSA_W)
        kc = kidx_ref[pl.ds(c0, DSA_W), :]
        sc = [jnp.zeros((DSA_QB, LANES), F32) for _ in range(nsub)]
        for h in range(IDX_HEADS):
            s = jnp.maximum(_nt(qih_s[h], kc), 0.0)
            wb = wb_s[h]
            for j in range(nsub):
                sc[j] = sc[j] + s[:, j * LANES:(j + 1) * LANES] * wb
        for j in range(nsub):
            kpos = c0 + j * LANES + lax.broadcasted_iota(jnp.int32, (1, LANES), 1)
            score_s[c * nsub + j] = jnp.where(kpos <= qpos, sc[j], -jnp.inf)
        return carry

    lax.fori_loop(0, nch, p1, 0)

    def stats(c, carry):
        mn, mx = carry
        for j in range(nsub):
            s = score_s[c * nsub + j]
            mx = jnp.maximum(mx, s)
            mn = jnp.minimum(mn, jnp.where(s > -jnp.inf, s, jnp.inf))
        return mn, mx

    mn, mx = lax.fori_loop(0, nch, stats, (jnp.full((DSA_QB, LANES), jnp.inf, F32),
                                           jnp.full((DSA_QB, LANES), -jnp.inf, F32)))
    lo0 = jnp.min(mn, axis=1, keepdims=True)
    hi0 = jnp.max(mx, axis=1, keepdims=True)

    def count_ge(thr):
        thr_b = jnp.broadcast_to(thr, (DSA_QB, LANES))

        def body(c, acc):
            for j in range(nsub):
                s = score_s[c * nsub + j]
                acc = acc + jnp.where(s >= thr_b, 1.0, 0.0)
            return acc

        acc = lax.fori_loop(0, nch, body, jnp.zeros((DSA_QB, LANES), F32))
        return jnp.sum(acc, axis=1, keepdims=True)

    thr = _bisect_threshold(count_ge, lo0, hi0, (qpos + 1).astype(F32), topk)
    thr_b = jnp.broadcast_to(thr, (DSA_QB, LANES))

    m_s[...] = jnp.full(m_s.shape, NEG, F32)
    l_s[...] = jnp.zeros(l_s.shape, F32)
    acc_s[...] = jnp.zeros(acc_s.shape, F32)

    def p3(c, with_bias):
        c0 = pl.multiple_of(c * DSA_W3, DSA_W3)
        madd = jnp.concatenate([jnp.where(score_s[c * nsub3 + j] >= thr_b, 0.0, NEG) for j in range(nsub3)], axis=1)
        kc = k_ref[pl.ds(c0, DSA_W3), :]
        vc = v_ref[pl.ds(c0, DSA_W3), :]
        rel = [i - (c * nsub3 + j) for j in range(nsub3)]

        def scores(n):
            return _nt(qs_s[n], kc[:, n * B_HEAD_DIM:(n + 1) * B_HEAD_DIM])

        def softmax(n, lg):
            lg = lg.reshape(B_GROUP, DSA_QB, DSA_W3) + madd[None]
            if with_bias:
                lg = lg + jnp.stack([jnp.concatenate(
                    [jnp.where(rel[j] == 0, bias_ref[n * B_GROUP + gq, 0],
                               jnp.where(rel[j] == 1, bias_ref[n * B_GROUP + gq, 1], 0.0)) for j in range(nsub3)],
                    axis=1) for gq in range(B_GROUP)])
            m_old = m_s[n]
            m_new = jnp.maximum(m_old, jnp.max(lg, axis=-1, keepdims=True))
            p = jnp.exp(lg - m_new)
            alpha = jnp.exp(m_old - m_new)
            l_s[n] = alpha * l_s[n] + jnp.sum(p, axis=-1, keepdims=True)
            m_s[n] = m_new
            pv = jnp.dot(p.reshape(B_GROUP * DSA_QB, DSA_W3).astype(BF16), vc[:, n * B_HEAD_DIM:(n + 1) * B_HEAD_DIM],
                         preferred_element_type=F32)
            return alpha, pv.reshape(B_GROUP, DSA_QB, B_HEAD_DIM)

        lgs = [scores(n) for n in range(B_KV_HEADS)]
        outs = [softmax(n, lgs[n]) for n in range(B_KV_HEADS)]
        for n in range(B_KV_HEADS):
            acc_s[n] = outs[n][0] * acc_s[n] + outs[n][1]

    n_far = jnp.maximum(i - 1, 0) // nsub3
    lax.fori_loop(0, n_far, lambda c, carry: (p3(c, False), carry)[1], 0)
    lax.fori_loop(n_far, nch3, lambda c, carry: (p3(c, True), carry)[1], 0)

    for h in range(B_HEADS):
        n, gq = h // B_GROUP, h % B_GROUP
        o_ref[:, h * B_HEAD_DIM:(h + 1) * B_HEAD_DIM] = (acc_s[n, gq] / l_s[n, gq]).astype(o_ref.dtype)


def dsa_prompt(z, zsmall, kidx_bf, k_bf, v_bf, bias_tiles):
    m = z.shape[0]
    topk = min(TOPK_MAX, m // 4)
    wq = B_HEADS * B_HEAD_DIM
    kern = functools.partial(_dsa_prompt_kernel, topk=topk)
    return pl.pallas_call(
        kern,
        grid=(m // DSA_QB,),
        in_specs=[pl.BlockSpec((DSA_QB, IDX_HEADS * IDX_DIM), lambda i: (i, C_IQ // (IDX_HEADS * IDX_DIM))),
                  pl.BlockSpec((DSA_QB, wq), lambda i: (i, C_BQ // wq)),
                  pl.BlockSpec((DSA_QB, LANES), lambda i: (i, 0)),
                  pl.BlockSpec(kidx_bf.shape, lambda i: (0, 0)),
                  pl.BlockSpec(k_bf.shape, lambda i: (0, 0)),
                  pl.BlockSpec(v_bf.shape, lambda i: (0, 0)),
                  pl.BlockSpec(bias_tiles.shape, lambda i: (0, 0, 0, 0))],
        out_specs=pl.BlockSpec((DSA_QB, wq), lambda i: (i, 0)),
        out_shape=jax.ShapeDtypeStruct((m, wq), BF16),
        scratch_shapes=[pltpu.VMEM((m // LANES, DSA_QB, LANES), F32),
                        pltpu.VMEM((IDX_HEADS, DSA_QB, IDX_DIM), BF16),
                        pltpu.VMEM((B_KV_HEADS, B_GROUP * DSA_QB, B_HEAD_DIM), BF16),
                        pltpu.VMEM((IDX_HEADS, DSA_QB, LANES), F32),
                        pltpu.VMEM((B_KV_HEADS, B_GROUP, DSA_QB, 1), F32),
                        pltpu.VMEM((B_KV_HEADS, B_GROUP, DSA_QB, 1), F32),
                        pltpu.VMEM((B_KV_HEADS, B_GROUP, DSA_QB, B_HEAD_DIM), F32)],
        compiler_params=_cparams("arbitrary"),
        name="dsa_prompt",
    )(z, z, zsmall, kidx_bf, k_bf, v_bf, bias_tiles)


def _page_copies(table_ref, b, n_pages, src_hbm, dst, sem, rows_per_page=PAGE_SIZE):
    def copy(p):
        return pltpu.make_async_copy(src_hbm.at[table_ref[b, p]],
                                     dst.at[pl.ds(p * rows_per_page, rows_per_page)], sem)
    return copy


def _dsa_scores_kernel(pt_ref, iq_ref, iw_ref, iknew_ref, kidx_hbm, o_ref, buf, sem, *, n_pages):
    b = pl.program_id(0)
    nb = pl.num_programs(0)
    past = n_pages * PAGE_SIZE

    def start(bb, slot):
        cp = _page_copies(pt_ref, bb, n_pages, kidx_hbm, buf.at[slot], sem.at[slot])
        lax.fori_loop(0, n_pages, lambda p, c: (cp(p).start(), c)[1], 0)

    def wait(bb, slot):
        cp = _page_copies(pt_ref, bb, n_pages, kidx_hbm, buf.at[slot], sem.at[slot])
        lax.fori_loop(0, n_pages, lambda p, c: (cp(p).wait(), c)[1], 0)

    slot = b % 2

    @pl.when(b == 0)
    def _():
        start(0, 0)

    @pl.when(b + 1 < nb)
    def _():
        start(b + 1, 1 - slot)

    wait(b, slot)

    r16 = lambda a: a.astype(BF16).astype(F32)
    qi = iq_ref[0].astype(BF16)
    wcol = r16(iw_ref[0]) * (IDX_DIM ** -0.5 * IDX_HEADS ** -0.5)
    s = r16(jnp.maximum(_nt(qi, buf[slot].astype(BF16)), 0.0))
    o_ref[0, :, 0:past] = jnp.sum(s * wcol, axis=0, keepdims=True)
    sn = r16(jnp.maximum(_nt(qi, iknew_ref[0].astype(BF16)), 0.0))
    sn = jnp.sum(sn * wcol, axis=0, keepdims=True)
    lane = lax.broadcasted_iota(jnp.int32, (1, LANES), 1)
    o_ref[0, :, past:past + LANES] = jnp.where(lane == 0, sn, -jnp.inf)


def dsa_scores(page_table, iq, iw, iknew_pad, cache_kidx):
    b, n_pages = page_table.shape
    past = n_pages * PAGE_SIZE
    kern = functools.partial(_dsa_scores_kernel, n_pages=n_pages)
    gs = pltpu.PrefetchScalarGridSpec(
        num_scalar_prefetch=1,
        grid=(b,),
        in_specs=[pl.BlockSpec((1, IDX_HEADS, IDX_DIM), lambda i, pt: (i, 0, 0)),
                  pl.BlockSpec((1, IDX_HEADS, 1), lambda i, pt: (i, 0, 0)),
                  pl.BlockSpec((1, LANES, IDX_DIM), lambda i, pt: (i, 0, 0)),
                  pl.BlockSpec(memory_space=pl.ANY)],
        out_specs=pl.BlockSpec((1, 1, past + LANES), lambda i, pt: (i, 0, 0)),
        scratch_shapes=[pltpu.VMEM((2, past, IDX_DIM), F32), pltpu.SemaphoreType.DMA((2,))],
    )
    return pl.pallas_call(
        kern, grid_spec=gs,
        out_shape=jax.ShapeDtypeStruct((b, 1, past + LANES), F32),
        compiler_params=_cparams("arbitrary"),
        name="dsa_scores",
    )(page_table, iq, iw, iknew_pad, cache_kidx)


def _dsa_threshold_kernel(s_ref, thr_ref, *, topk):
    s = s_ref[...]
    nb = s.shape[0]
    finite = s > -jnp.inf
    lo0 = jnp.min(jnp.where(finite, s, jnp.inf), axis=1, keepdims=True)
    hi0 = jnp.max(s, axis=1, keepdims=True)
    cnt0 = jnp.sum(jnp.where(finite, 1.0, 0.0), axis=1, keepdims=True)

    def count_ge(thr):
        return jnp.sum(jnp.where(s_ref[...] >= thr, 1.0, 0.0), axis=1, keepdims=True)

    thr = _bisect_threshold(count_ge, lo0, hi0, cnt0, topk)
    thr_ref[...] = jnp.broadcast_to(thr, (nb, LANES))


def dsa_threshold(scores, topk):
    b, l = scores.shape
    return pl.pallas_call(
        functools.partial(_dsa_threshold_kernel, topk=topk),
        grid=(1,),
        in_specs=[pl.BlockSpec((b, l), lambda i: (0, 0))],
        out_specs=pl.BlockSpec((b, LANES), lambda i: (0, 0)),
        out_shape=jax.ShapeDtypeStruct((b, LANES), F32),
        compiler_params=_cparams("arbitrary"),
        name="dsa_threshold",
    )(scores)


def _dsa_decode_kernel(pt_ref, q_ref, s_ref, thr_ref, knew_ref, vnew_ref, bias_ref, k_hbm, v_hbm, o_ref,
                       kbuf, vbuf, sem, *, n_pages):
    b = pl.program_id(0)
    nb = pl.num_programs(0)
    past = n_pages * PAGE_SIZE

    rpp = PAGE_SIZE * B_KV_HEADS

    def copies(bb, slot):
        ck = _page_copies(pt_ref, bb, n_pages, k_hbm, kbuf.at[slot], sem.at[0, slot], rpp)
        cv = _page_copies(pt_ref, bb, n_pages, v_hbm, vbuf.at[slot], sem.at[1, slot], rpp)
        return ck, cv

    def start(bb, slot):
        ck, cv = copies(bb, slot)
        lax.fori_loop(0, n_pages, lambda p, c: (ck(p).start(), cv(p).start(), c)[2], 0)

    def wait(bb, slot):
        ck, cv = copies(bb, slot)
        lax.fori_loop(0, n_pages, lambda p, c: (ck(p).wait(), cv(p).wait(), c)[2], 0)

    slot = b % 2

    @pl.when(b == 0)
    def _():
        kbuf[:, n_pages * rpp:, :] = jnp.zeros((2, rpp, B_HEAD_DIM), F32)
        vbuf[:, n_pages * rpp:, :] = jnp.zeros((2, rpp, B_HEAD_DIM), F32)
        start(0, 0)

    @pl.when(b + 1 < nb)
    def _():
        start(b + 1, 1 - slot)

    kbuf[slot, n_pages * rpp:n_pages * rpp + 8, :] = knew_ref[0]
    vbuf[slot, n_pages * rpp:n_pages * rpp + 8, :] = vnew_ref[0]
    wait(b, slot)

    sel = s_ref[0] >= thr_ref[0][:, 0:1]
    n_keys = past + PAGE_SIZE
    outs = []
    for n in range(B_KV_HEADS):
        kn = kbuf[slot, pl.ds(n, n_keys, stride=B_KV_HEADS), :].astype(BF16)
        vn = vbuf[slot, pl.ds(n, n_keys, stride=B_KV_HEADS), :].astype(BF16)
        qn = q_ref[0, n].astype(BF16)
        lg = _nt(qn, kn) * B_HEAD_DIM ** -0.5 + bias_ref[n]
        m = jnp.max(jnp.where(sel, lg, NEG), axis=1, keepdims=True)
        p = jnp.where(sel, jnp.exp(lg - m), 0.0)
        p = p / jnp.sum(p, axis=1, keepdims=True)
        outs.append(jnp.dot(p.astype(BF16), vn, preferred_element_type=F32))
    o_ref[0] = jnp.concatenate(outs, axis=0).astype(o_ref.dtype)


def dsa_decode(page_table, q8, scores, thr, knew8, vnew8, bias_rows, cache_k2, cache_v2):
    b, n_pages = page_table.shape
    past = n_pages * PAGE_SIZE
    l = past + LANES
    wkv = B_KV_HEADS * B_HEAD_DIM
    kern = functools.partial(_dsa_decode_kernel, n_pages=n_pages)
    gs = pltpu.PrefetchScalarGridSpec(
        num_scalar_prefetch=1,
        grid=(b,),
        in_specs=[pl.BlockSpec((1, B_KV_HEADS, 8, B_HEAD_DIM), lambda i, pt: (i, 0, 0, 0)),
                  pl.BlockSpec((1, 1, l), lambda i, pt: (i, 0, 0)),
                  pl.BlockSpec((1, 1, LANES), lambda i, pt: (i, 0, 0)),
                  pl.BlockSpec((1, 8, B_HEAD_DIM), lambda i, pt: (i, 0, 0)),
                  pl.BlockSpec((1, 8, B_HEAD_DIM), lambda i, pt: (i, 0, 0)),
                  pl.BlockSpec((B_KV_HEADS, 8, l), lambda i, pt: (0, 0, 0)),
                  pl.BlockSpec(memory_space=pl.ANY),
                  pl.BlockSpec(memory_space=pl.ANY)],
        out_specs=pl.BlockSpec((1, 2 * 8, B_HEAD_DIM), lambda i, pt: (i, 0, 0)),
        scratch_shapes=[pltpu.VMEM((2, l * B_KV_HEADS, B_HEAD_DIM), F32),
                        pltpu.VMEM((2, l * B_KV_HEADS, B_HEAD_DIM), F32),
                        pltpu.SemaphoreType.DMA((2, 2))],
    )
    return pl.pallas_call(
        kern, grid_spec=gs,
        out_shape=jax.ShapeDtypeStruct((b, 2 * 8, B_HEAD_DIM), F32),
        compiler_params=_cparams("arbitrary"),
        name="dsa_decode",
    )(page_table, q8, scores, thr, knew8, vnew8, bias_rows, cache_k2, cache_v2)


def _merge_kernel(ya_ref, yb_ref, ga_ref, gb_ref, wa_ref, wb_ref, o_ref):
    a = _mm(ya_ref[...], wa_ref[...])
    bb = _mm(yb_ref[...], wb_ref[...])
    o_ref[...] = (jax.nn.sigmoid(ga_ref[...]) * a + jax.nn.sigmoid(gb_ref[...]) * bb).astype(o_ref.dtype)


def merge(ya, yb, z, ga_col, gb_col, wa, wb, *, tm, tn):
    m, k = ya.shape
    n = wa.shape[1]
    return pl.pallas_call(
        _merge_kernel,
        grid=(m // tm, n // tn),
        in_specs=[pl.BlockSpec((tm, k), lambda i, j: (i, 0)),
                  pl.BlockSpec((tm, k), lambda i, j: (i, 0)),
                  pl.BlockSpec((tm, tn), lambda i, j: (i, ga_col // tn + j)),
                  pl.BlockSpec((tm, tn), lambda i, j: (i, gb_col // tn + j)),
                  pl.BlockSpec((k, tn), lambda i, j: (0, j)),
                  pl.BlockSpec((k, tn), lambda i, j: (0, j))],
        out_specs=pl.BlockSpec((tm, tn), lambda i, j: (i, j)),
        out_shape=jax.ShapeDtypeStruct((m, n), wa.dtype),
        compiler_params=_cparams("parallel", "arbitrary"),
        name="merge",
    )(ya, yb, z, z, wa, wb)


def _cross_prompt_kernel(x_ref, g_ref, wq_ref, mk_ref, mv_ref, wo_ref, o_ref):
    x = x_ref[...]
    ms = jnp.mean(x * x, axis=-1, keepdims=True)
    h = (x * lax.rsqrt(ms + EPS) * g_ref[...]).astype(BF16)
    q = jnp.dot(h, wq_ref[...], preferred_element_type=F32)
    outs = []
    for hh in range(X_HEADS):
        cols = slice(hh * X_HEAD_DIM, (hh + 1) * X_HEAD_DIM)
        lg = _nt(q[:, cols].astype(BF16), mk_ref[:, cols]) * X_HEAD_DIM ** -0.5
        mx = jnp.max(lg, axis=1, keepdims=True)
        p = jnp.exp(lg - mx)
        p = p / jnp.sum(p, axis=1, keepdims=True)
        outs.append(jnp.dot(p.astype(BF16), mv_ref[:, cols], preferred_element_type=F32).astype(BF16))
    att = jnp.concatenate(outs, axis=1)
    o_ref[...] = x + jnp.dot(att, wo_ref[...], preferred_element_type=F32)


def cross_prompt(x, g, wq, mk, mv, wo, *, tm):
    m, d = x.shape
    full = lambda a: pl.BlockSpec(a.shape, lambda i: (0,) * a.ndim)
    g2 = g.reshape(1, d)
    return pl.pallas_call(
        _cross_prompt_kernel,
        grid=(m // tm,),
        in_specs=[pl.BlockSpec((tm, d), lambda i: (i, 0)), full(g2), full(wq), full(mk), full(mv), full(wo)],
        out_specs=pl.BlockSpec((tm, d), lambda i: (i, 0)),
        out_shape=jax.ShapeDtypeStruct((m, d), F32),
        compiler_params=_cparams("parallel"),
        name="cross_prompt",
    )(x, g2, wq, mk, mv, wo)


def _cross_step_kernel(q_ref, mk_ref, mv_ref, o_ref):
    r16 = lambda a: a.astype(BF16).astype(F32)
    q = r16(q_ref[0])
    outs = []
    for hh in range(X_HEADS):
        cols = slice(hh * X_HEAD_DIM, (hh + 1) * X_HEAD_DIM)
        kh = r16(mk_ref[0, :, cols])
        vh = r16(mv_ref[0, :, cols])
        lg = jnp.sum(kh * q[:, cols], axis=1, keepdims=True) * X_HEAD_DIM ** -0.5
        mx = jnp.max(lg, axis=0, keepdims=True)
        p = jnp.exp(lg - mx)
        p = r16(p / jnp.sum(p, axis=0, keepdims=True))
        outs.append(jnp.sum(p * vh, axis=0, keepdims=True))
    o_ref[0] = jnp.concatenate(outs, axis=1).astype(o_ref.dtype)


def cross_step(q, mk, mv):
    b, w = q.shape
    mem = mk.shape[1]
    return pl.pallas_call(
        _cross_step_kernel,
        grid=(b,),
        in_specs=[pl.BlockSpec((1, 1, w), lambda i: (i, 0, 0)),
                  pl.BlockSpec((1, mem, w), lambda i: (i, 0, 0)),
                  pl.BlockSpec((1, mem, w), lambda i: (i, 0, 0))],
        out_specs=pl.BlockSpec((1, 1, w), lambda i: (i, 0, 0)),
        out_shape=jax.ShapeDtypeStruct((b, 1, w), F32),
        compiler_params=_cparams("arbitrary"),
        name="cross_step",
    )(q.reshape(b, 1, w), mk, mv)


def _router_kernel(x_ref, g_ref, w_ref, b_ref, hf_ref, route_ref):
    x = x_ref[...]
    ms = jnp.mean(x * x, axis=-1, keepdims=True)
    hf = x * lax.rsqrt(ms + EPS) * g_ref[...]
    hf_ref[...] = hf
    lg = _mm(hf, w_ref[...]) + b_ref[...]
    tm = lg.shape[0]
    lane = lax.broadcasted_iota(jnp.int32, (tm, LANES), 1)
    big = jnp.int32(LANES)
    is_g = lane < N_GROUPS
    gmax = jnp.max(jnp.where(is_g, lg, -jnp.inf), axis=1, keepdims=True)
    grp = jnp.min(jnp.where(is_g & (lg == gmax), lane, big), axis=1, keepdims=True)
    p_grp = 1.0 / jnp.sum(jnp.where(is_g, jnp.exp(lg - gmax), 0.0), axis=1, keepdims=True)
    e_lo = N_GROUPS + grp * EXP_PER_GROUP
    in_g = (lane >= e_lo) & (lane < e_lo + EXP_PER_GROUP)
    v1 = jnp.max(jnp.where(in_g, lg, -jnp.inf), axis=1, keepdims=True)
    i1 = jnp.min(jnp.where(in_g & (lg == v1), lane, big), axis=1, keepdims=True)
    rest = in_g & (lane != i1)
    v2 = jnp.max(jnp.where(rest, lg, -jnp.inf), axis=1, keepdims=True)
    i2 = jnp.min(jnp.where(rest & (lg == v2), lane, big), axis=1, keepdims=True)
    e2 = jnp.exp(v2 - v1)
    g1 = p_grp / (1.0 + e2)
    g2 = p_grp * e2 / (1.0 + e2)
    r = jnp.where(lane == 0, (i1 - N_GROUPS).astype(F32),
                  jnp.where(lane == 1, (i2 - N_GROUPS).astype(F32),
                            jnp.where(lane == 2, g1, jnp.where(lane == 3, g2, 0.0))))
    route_ref[...] = r


def router(x, g, w_pad, b_pad, *, tm):
    m, d = x.shape
    return pl.pallas_call(
        _router_kernel,
        grid=(pl.cdiv(m, tm),),
        in_specs=[pl.BlockSpec((tm, d), lambda i: (i, 0)),
                  pl.BlockSpec((1, d), lambda i: (0, 0)),
                  pl.BlockSpec((d, LANES), lambda i: (0, 0)),
                  pl.BlockSpec((1, LANES), lambda i: (0, 0))],
        out_specs=[pl.BlockSpec((tm, d), lambda i: (i, 0)),
                   pl.BlockSpec((tm, LANES), lambda i: (i, 0))],
        out_shape=[jax.ShapeDtypeStruct((m, d), F32), jax.ShapeDtypeStruct((m, LANES), F32)],
        compiler_params=_cparams("parallel"),
        name="router",
    )(x, g.reshape(1, d), w_pad, b_pad)


def _moe_kernel(be_ref, tok_ref, hf_hbm, wg_ref, wu_ref, wd_ref, o_ref, xbuf, wg_s, wu_s, wd_s, sem):
    b = pl.program_id(0)
    nb = pl.num_programs(0)
    blk = xbuf.shape[1]

    def row_copy(bb, slot, r):
        tok = tok_ref[bb * blk + r]
        return pltpu.make_async_copy(hf_hbm.at[pl.ds(tok, 1)], xbuf.at[slot, pl.ds(r, 1)], sem.at[slot])

    def start(bb, slot):
        lax.fori_loop(0, blk, lambda r, c: (row_copy(bb, slot, r).start(), c)[1], 0)

    def wait(bb, slot):
        lax.fori_loop(0, blk, lambda r, c: (row_copy(bb, slot, r).wait(), c)[1], 0)

    slot = b % 2

    @pl.when(b == 0)
    def _():
        start(0, 0)

    @pl.when(b + 1 < nb)
    def _():
        start(b + 1, 1 - slot)

    changed = jnp.logical_or(b == 0, be_ref[b] != be_ref[jnp.maximum(b - 1, 0)])

    @pl.when(changed)
    def _():
        wg_s[...] = wg_ref[0].astype(BF16)
        wu_s[...] = wu_ref[0].astype(BF16)
        wd_s[...] = wd_ref[0].astype(BF16)

    wait(b, slot)
    x = xbuf[slot].astype(BF16)
    gg = jnp.dot(x, wg_s[...], preferred_element_type=F32)
    uu = jnp.dot(x, wu_s[...], preferred_element_type=F32)
    a = (_silu(gg) * uu).astype(BF16)
    o_ref[...] = jnp.dot(a, wd_s[...], preferred_element_type=F32)


def moe_experts(block_e, slot_tok, hf, w_g, w_u, w_d, *, blk):
    n_blocks = block_e.shape[0]
    d = hf.shape[1]
    ff = w_g.shape[2]
    gs = pltpu.PrefetchScalarGridSpec(
        num_scalar_prefetch=2,
        grid=(n_blocks,),
        in_specs=[pl.BlockSpec(memory_space=pl.ANY),
                  pl.BlockSpec((1, d, ff), lambda i, be, tk: (be[i], 0, 0)),
                  pl.BlockSpec((1, d, ff), lambda i, be, tk: (be[i], 0, 0)),
                  pl.BlockSpec((1, ff, d), lambda i, be, tk: (be[i], 0, 0))],
        out_specs=pl.BlockSpec((blk, d), lambda i, be, tk: (i, 0)),
        scratch_shapes=[pltpu.VMEM((2, blk, d), F32),
                        pltpu.VMEM((d, ff), BF16), pltpu.VMEM((d, ff), BF16), pltpu.VMEM((ff, d), BF16),
                        pltpu.SemaphoreType.DMA((2,))],
    )
    return pl.pallas_call(
        _moe_kernel, grid_spec=gs,
        out_shape=jax.ShapeDtypeStruct((n_blocks * blk, d), F32),
        compiler_params=_cparams("arbitrary"),
        name="moe_experts",
    )(block_e, slot_tok, hf, w_g, w_u, w_d)


def _combine_kernel(dest_ref, x_ref, route_ref, gf_ref, yb_hbm, o_ref, ybuf, sem):
    i = pl.program_id(0)
    nt = pl.num_programs(0)
    tm = x_ref.shape[0]

    def row_copy(ii, slot, r):
        base = (ii * tm + r) * EXPERT_TOPK
        c0 = pltpu.make_async_copy(yb_hbm.at[pl.ds(dest_ref[base], 1)], ybuf.at[slot, 0, pl.ds(r, 1)], sem.at[slot])
        c1 = pltpu.make_async_copy(yb_hbm.at[pl.ds(dest_ref[base + 1], 1)], ybuf.at[slot, 1, pl.ds(r, 1)], sem.at[slot])
        return c0, c1

    def start(ii, slot):
        def body(r, c):
            c0, c1 = row_copy(ii, slot, r)
            c0.start()
            c1.start()
            return c
        lax.fori_loop(0, tm, body, 0)

    def wait(ii, slot):
        def body(r, c):
            c0, c1 = row_copy(ii, slot, r)
            c0.wait()
            c1.wait()
            return c
        lax.fori_loop(0, tm, body, 0)

    slot = i % 2

    @pl.when(i == 0)
    def _():
        start(0, 0)

    @pl.when(i + 1 < nt)
    def _():
        start(i + 1, 1 - slot)

    wait(i, slot)
    route = route_ref[...]
    x = x_ref[...] + route[:, 2:3] * ybuf[slot, 0] + route[:, 3:4] * ybuf[slot, 1]
    ms = jnp.mean(x * x, axis=-1, keepdims=True)
    o_ref[...] = x * lax.rsqrt(ms + EPS) * gf_ref[...]


def combine(dest_pad, x, route, gf, yb, *, tm):
    m, d = x.shape
    gs = pltpu.PrefetchScalarGridSpec(
        num_scalar_prefetch=1,
        grid=(pl.cdiv(m, tm),),
        in_specs=[pl.BlockSpec((tm, d), lambda i, ds_: (i, 0)),
                  pl.BlockSpec((tm, LANES), lambda i, ds_: (i, 0)),
                  pl.BlockSpec((1, d), lambda i, ds_: (0, 0)),
                  pl.BlockSpec(memory_space=pl.ANY)],
        out_specs=pl.BlockSpec((tm, d), lambda i, ds_: (i, 0)),
        scratch_shapes=[pltpu.VMEM((2, EXPERT_TOPK, tm, d), F32), pltpu.SemaphoreType.DMA((2,))],
    )
    return pl.pallas_call(
        _combine_kernel, grid_spec=gs,
        out_shape=jax.ShapeDtypeStruct((m, d), F32),
        compiler_params=_cparams("arbitrary"),
        name="combine",
    )(dest_pad, x, route, gf.reshape(1, d), yb)


def _t5_bucket(dist):
    dist = jnp.asarray(dist, jnp.int32)
    max_exact = REL_BUCKETS // 2
    dist_f = jnp.maximum(dist, 1).astype(F32)
    large = max_exact + (jnp.log(dist_f / max_exact) / math.log(REL_MAX_DIST / max_exact)
                         * (REL_BUCKETS - max_exact)).astype(jnp.int32)
    large = jnp.minimum(large, REL_BUCKETS - 1)
    return jnp.where(dist < max_exact, dist, large)


def _bias_tables(rel_bias, past):
    r = np.arange(LANES)
    diff = r[:, None] - r[None, :]
    buckets = jnp.stack([_t5_bucket(np.maximum(diff, 0)),
                         _t5_bucket(np.maximum(diff + LANES, 0)),
                         _t5_bucket(np.full((LANES, LANES), 2 * LANES))])
    def lookup(bkt):
        oh = (bkt.reshape(-1, 1) == jnp.arange(REL_BUCKETS)[None, :]).astype(F32)
        out = jnp.dot(oh, rel_bias.astype(F32), precision=lax.Precision.HIGHEST)
        return out.T.reshape((rel_bias.shape[1],) + bkt.shape)

    tiles = lookup(buckets)
    tiles = tiles - tiles[:, 2:3]
    dist = np.maximum(past - np.arange(past + LANES), 0)
    rows = lookup(_t5_bucket(dist))
    rows = rows.reshape(B_KV_HEADS, B_GROUP, past + LANES)
    rows = jnp.concatenate([rows, jnp.zeros_like(rows)], axis=1)
    return tiles, rows


def _dispatch(eid, n_tokens, blk):
    a = eid.shape[0]
    n_blocks = -(-(a + N_EXPERTS * (blk - 1)) // blk)
    rows = n_blocks * blk
    tok = jnp.arange(a, dtype=jnp.int32) // EXPERT_TOPK
    order = jnp.argsort(eid)
    e_sorted = eid[order]
    counts = jnp.bincount(eid, length=N_EXPERTS)
    starts = jnp.cumsum(counts) - counts
    padded = (counts + blk - 1) // blk * blk
    pad_end = jnp.cumsum(padded)
    pad_start = pad_end - padded
    dest_sorted = pad_start[e_sorted] + jnp.arange(a, dtype=jnp.int32) - starts[e_sorted]
    dest = jnp.zeros((a,), jnp.int32).at[order].set(dest_sorted.astype(jnp.int32))
    slot_tok = jnp.zeros((rows,), jnp.int32).at[dest].set(tok)
    block_e = jnp.minimum(jnp.sum(pad_end[None, :] <= (jnp.arange(n_blocks) * blk)[:, None], axis=1), N_EXPERTS - 1)
    return dest, slot_tok, block_e.astype(jnp.int32)


def kernel(x_prompt, x_sample, mem_prompt, cache_k, cache_v, cache_kidx, page_table, state_hgrn, cache_mem_k,
           cache_mem_v, norm_mix, w_in, hgrn_lb_logits, hgrn_norm, w_branch_a, w_branch_b, w_out, norm_cross, w_xq,
           w_xk, w_xv, w_xo, norm_ffn, w_router_group, b_router_group, w_router_expert, b_router_expert, w_exp_gate,
           w_exp_up, w_exp_down, rel_bias, norm_final):
    assert w_in.shape[0] == 1, "single-layer step"
    l = 0
    drop0 = lambda a: a.reshape(a.shape[1:])
    bp, t, d = x_prompt.shape
    db = x_sample.shape[0]
    past = page_table.shape[1] * PAGE_SIZE
    xp = x_prompt.reshape(bp * t, d)
    xs = x_sample.reshape(db, d)

    wi = drop0(w_in)
    o = np.cumsum((0,) + (1024, 1024, 1024, 1024, 1024, 256, 256, 1024, 16, 64, 2048, 2048))
    seg = lambda i: wi[:, o[i]:o[i + 1]]
    w_cat = jnp.concatenate([seg(0), seg(1), seg(2), seg(3), seg(4), seg(7), seg(10), seg(11), seg(5), seg(6)],
                            axis=1).astype(BF16)
    w_small = jnp.pad(jnp.concatenate([seg(8), seg(9)], axis=1), ((0, 0), (0, LANES - 80))).astype(BF16)
    wa, wb, wo = w_branch_a[l].astype(BF16), w_branch_b[l].astype(BF16), w_out[l].astype(BF16)
    wxq, wxk, wxv, wxo = (w_xq[l].astype(BF16), w_xk[l].astype(BF16), w_xv[l].astype(BF16), w_xo[l].astype(BF16))
    w_route = jnp.pad(jnp.concatenate([w_router_group[l], w_router_expert[l]], axis=1),
                      ((0, 0), (0, LANES - N_GROUPS - N_EXPERTS))).astype(BF16)
    b_route = jnp.pad(jnp.concatenate([b_router_group[l], b_router_expert[l]]),
                      (0, LANES - N_GROUPS - N_EXPERTS)).reshape(1, LANES)
    bias_tiles, bias_rows = _bias_tables(rel_bias, past)

    zp = norm_matmul(xp, norm_mix[l], w_cat, tm=1024, tn=512)
    zps = norm_matmul(xp, norm_mix[l], w_small, tm=1024, tn=LANES)
    kp = zp[:, C_BK:C_BK + 256]
    vp = zp[:, C_BV:C_BV + 256]
    ikp = zps[:, IDX_HEADS:IDX_HEADS + IDX_DIM]
    ya_p, st_p = hgrn_prompt(zp, hgrn_lb_logits, hgrn_norm[l])
    yb_p = dsa_prompt(zp, zps, ikp.astype(BF16), kp.astype(BF16), vp.astype(BF16), bias_tiles)
    mg_p = merge(ya_p, yb_p, zp, C_GA, C_GB, wa, wb, tm=512, tn=512)
    x1p = matmul(mg_p, wo, xp, tm=512, tn=512)
    memp = mem_prompt.reshape(-1, d)
    mk = matmul(memp, wxk, tm=memp.shape[0], tn=512)
    mv = matmul(memp, wxv, tm=memp.shape[0], tn=512)
    x2p = cross_prompt(x1p, norm_cross[l], wxq, mk.astype(BF16), mv.astype(BF16), wxo, tm=512)

    zs = norm_matmul(xs, norm_mix[l], w_cat, tm=db, tn=512)
    zss = norm_matmul(xs, norm_mix[l], w_small, tm=db, tn=LANES)
    ks = zs[:, C_BK:C_BK + 256]
    vs = zs[:, C_BV:C_BV + 256]
    iks = zss[:, IDX_HEADS:IDX_HEADS + IDX_DIM]
    ya_s, st_s = hgrn_step(zs[:, :4 * A_HEADS * A_DK], hgrn_lb_logits, hgrn_norm[l], drop0(state_hgrn))
    iq_s = zs[:, C_IQ:C_IQ + IDX_HEADS * IDX_DIM].reshape(db, IDX_HEADS, IDX_DIM)
    iw_s = zss[:, :IDX_HEADS].reshape(db, IDX_HEADS, 1)
    iknew_pad = jnp.pad(iks[:, None, :], ((0, 0), (0, LANES - 1), (0, 0)))
    scores = dsa_scores(page_table, iq_s, iw_s, iknew_pad, drop0(cache_kidx)).reshape(db, past + LANES)
    topk_s = min(TOPK_MAX, (past + 1) // 4)
    thr = dsa_threshold(scores, topk_s)
    q8 = jnp.pad(zs[:, C_BQ:C_BQ + B_HEADS * B_HEAD_DIM].reshape(db, B_KV_HEADS, B_GROUP, B_HEAD_DIM),
                 ((0, 0), (0, 0), (0, 8 - B_GROUP), (0, 0)))
    knew8 = jnp.pad(ks.reshape(db, B_KV_HEADS, B_HEAD_DIM), ((0, 0), (0, 8 - B_KV_HEADS), (0, 0)))
    vnew8 = jnp.pad(vs.reshape(db, B_KV_HEADS, B_HEAD_DIM), ((0, 0), (0, 8 - B_KV_HEADS), (0, 0)))
    n_pool = cache_k.shape[1]
    ob = dsa_decode(page_table, q8, scores.reshape(db, 1, -1), thr.reshape(db, 1, LANES), knew8, vnew8, bias_rows,
                    cache_k.reshape(n_pool, PAGE_SIZE * B_KV_HEADS, B_HEAD_DIM),
                    cache_v.reshape(n_pool, PAGE_SIZE * B_KV_HEADS, B_HEAD_DIM))
    yb_s = ob.reshape(db, B_KV_HEADS, 8, B_HEAD_DIM)[:, :, :B_GROUP].reshape(db, B_HEADS * B_HEAD_DIM)
    mg_s = merge(ya_s.reshape(db, -1), yb_s, zs, C_GA, C_GB, wa, wb, tm=db, tn=512)
    x1s = matmul(mg_s, wo, xs, tm=db, tn=512)
    qx_s = norm_matmul(x1s, norm_cross[l], wxq, tm=db, tn=512)
    mem = cache_mem_k.shape[2]
    att_s = cross_step(qx_s, cache_mem_k.reshape(db, mem, -1), cache_mem_v.reshape(db, mem, -1))
    x2s = matmul(att_s.reshape(db, -1), wxo, x1s, tm=db, tn=512)

    x2 = jnp.concatenate([x2p, x2s], axis=0)
    n = x2.shape[0]
    hf, route = router(x2, norm_ffn[l], w_route, b_route, tm=256)
    eid = route[:, :EXPERT_TOPK].astype(jnp.int32).reshape(-1)
    a = n * EXPERT_TOPK
    blk = min(MOE_BLOCK, max(8, a // N_EXPERTS))
    dest, slot_tok, block_e = _dispatch(eid, n, blk)
    yb = moe_experts(block_e, slot_tok, hf, drop0(w_exp_gate), drop0(w_exp_up), drop0(w_exp_down), blk=blk)
    tmc = 128
    n_pad = -(-n // tmc) * tmc
    dest_pad = jnp.pad(dest, (0, (n_pad - n) * EXPERT_TOPK))
    y = combine(dest_pad, x2, route, norm_final, yb, tm=tmc)

    y_prompt = y[:bp * t].reshape(bp, t, d)
    y_sample = y[bp * t:].reshape(db, 1, d)
    return (y_prompt, y_sample,
            kp.reshape(1, bp, t, B_KV_HEADS, B_HEAD_DIM), vp.reshape(1, bp, t, B_KV_HEADS, B_HEAD_DIM),
            ikp.reshape(1, bp, t, IDX_DIM),
            jnp.swapaxes(st_p, 1, 2).reshape(1, bp, A_HEADS, A_DK, A_DV),
            mk.reshape(1, bp, -1, X_HEADS, X_HEAD_DIM), mv.reshape(1, bp, -1, X_HEADS, X_HEAD_DIM),
            ks.reshape(1, db, 1, B_KV_HEADS, B_HEAD_DIM), vs.reshape(1, db, 1, B_KV_HEADS, B_HEAD_DIM),
            iks.reshape(1, db, 1, IDX_DIM),
            st_s.reshape(1, db, A_HEADS, A_DK, A_DV))
```

```python
import functools
import math

import jax
import jax.numpy as jnp
import numpy as np
from jax import lax
from jax.experimental import pallas as pl
from jax.experimental.pallas import tpu as pltpu

F32 = jnp.float32
BF16 = jnp.bfloat16
EPS = 1e-6

D_MODEL = 2048
A_HEADS, A_DK, A_DV = 8, 128, 128
B_HEADS, B_KV_HEADS, B_HEAD_DIM = 8, 2, 128
B_GROUP = B_HEADS // B_KV_HEADS
IDX_HEADS, IDX_DIM = 16, 64
TOPK_MAX = 256
PAGE_SIZE = 128
REL_BUCKETS, REL_MAX_DIST = 32, 128
X_HEADS, X_HEAD_DIM = 4, 128
N_GROUPS, EXP_PER_GROUP = 4, 8
N_EXPERTS = N_GROUPS * EXP_PER_GROUP
EXPERT_TOPK = 2
EXPERT_FF = 512
MOE_BLOCK = 128

LANES = 128
VMEM_LIMIT = 56 * 1024 * 1024

NEG = -1e30

C_AQ, C_AF, C_AI, C_AG, C_BQ, C_IQ, C_GA, C_GB, C_BK, C_BV = (
    0, 1024, 2048, 3072, 4096, 5120, 6144, 8192, 10240, 10496)
NZ = 10752


def _cparams(*sem):
    return pltpu.CompilerParams(dimension_semantics=sem, vmem_limit_bytes=VMEM_LIMIT)


def _silu(x):
    return x * jax.nn.sigmoid(x)


def _nt(a, b):
    return lax.dot_general(a, b, (((1,), (1,)), ((), ())), preferred_element_type=F32)


def _nt_f32(a, b):
    return lax.dot_general(a, b, (((1,), (1,)), ((), ())), preferred_element_type=F32,
                           precision=lax.Precision.HIGHEST)


def _mm(a, w):
    if w.dtype == F32:
        return jnp.dot(a.astype(F32), w, preferred_element_type=F32, precision=lax.Precision.HIGHEST)
    return jnp.dot(a.astype(BF16), w, preferred_element_type=F32)


def _norm_matmul_kernel(x_ref, g_ref, w_ref, o_ref, h_ref):
    @pl.when(pl.program_id(1) == 0)
    def _():
        x = x_ref[...]
        ms = jnp.mean(x * x, axis=-1, keepdims=True)
        h_ref[...] = (x * lax.rsqrt(ms + EPS) * g_ref[...]).astype(h_ref.dtype)

    o_ref[...] = _mm(h_ref[...], w_ref[...])


def norm_matmul(x, g, w, *, tm, tn):
    m, k = x.shape
    n = w.shape[1]
    return pl.pallas_call(
        _norm_matmul_kernel,
        grid=(m // tm, pl.cdiv(n, tn)),
        in_specs=[pl.BlockSpec((tm, k), lambda i, j: (i, 0)),
                  pl.BlockSpec((1, k), lambda i, j: (0, 0)),
                  pl.BlockSpec((k, tn), lambda i, j: (0, j))],
        out_specs=pl.BlockSpec((tm, tn), lambda i, j: (i, j)),
        out_shape=jax.ShapeDtypeStruct((m, n), F32),
        scratch_shapes=[pltpu.VMEM((tm, k), w.dtype)],
        compiler_params=_cparams("parallel", "arbitrary"),
        name="norm_matmul",
    )(x, g.reshape(1, k), w)


def _matmul_res_kernel(x_ref, w_ref, r_ref, o_ref):
    o_ref[...] = r_ref[...] + _mm(x_ref[...], w_ref[...])


def _matmul_kernel(x_ref, w_ref, o_ref):
    o_ref[...] = _mm(x_ref[...], w_ref[...])


def matmul(x, w, res=None, *, tm, tn):
    m, k = x.shape
    n = w.shape[1]
    in_specs = [pl.BlockSpec((tm, k), lambda i, j: (i, 0)),
                pl.BlockSpec((k, tn), lambda i, j: (0, j))]
    args = [x, w]
    kern = _matmul_kernel
    if res is not None:
        in_specs.append(pl.BlockSpec((tm, tn), lambda i, j: (i, j)))
        args.append(res)
        kern = _matmul_res_kernel
    return pl.pallas_call(
        kern,
        grid=(m // tm, n // tn),
        in_specs=in_specs,
        out_specs=pl.BlockSpec((tm, tn), lambda i, j: (i, j)),
        out_shape=jax.ShapeDtypeStruct((m, n), F32),
        compiler_params=_cparams("parallel", "arbitrary"),
        name="matmul",
    )(*args)


HG_TB = 128
HG_C = 16


def _hgrn_prompt_kernel(aq_ref, af_ref, ai_ref, ag_ref, lbl_ref, ng_ref, ya_ref, st_out_ref,
                        st_ref, q_s, k_s, g_s, v_s):
    t = pl.program_id(0)

    @pl.when(t == 0)
    def _():
        st_ref[...] = jnp.zeros_like(st_ref)

    lbl = lbl_ref[...]
    mx = jnp.max(lbl, axis=0, keepdims=True)
    ex = jnp.exp(lbl - mx)
    lb = ex[0:1, :] / jnp.sum(ex, axis=0, keepdims=True)

    f = lb + (1.0 - lb) * jax.nn.sigmoid(af_ref[...])
    logf = jnp.log(f)
    row = lax.broadcasted_iota(jnp.int32, (HG_TB, HG_TB), 0)
    col = lax.broadcasted_iota(jnp.int32, (HG_TB, HG_TB), 1)
    tri = jnp.where((row // HG_C == col // HG_C) & (col <= row), 1.0, 0.0).astype(F32)
    g_s[...] = jnp.dot(tri, logf, preferred_element_type=F32, precision=lax.Precision.HIGHEST)
    q_s[...] = _silu(aq_ref[...])
    k_s[...] = 1.0 - f
    v_s[...] = ai_ref[...]

    sub = lax.broadcasted_iota(jnp.int32, (HG_C, A_DK), 0)
    ng = ng_ref[...]

    def chunk(c, carry):
        r0 = pl.multiple_of(c * HG_C, HG_C)
        rows = pl.ds(r0, HG_C)
        for h in range(A_HEADS):
            cols = slice(h * A_DK, (h + 1) * A_DK)
            g = g_s[rows, cols]
            qh = q_s[rows, cols]
            kh = k_s[rows, cols]
            vh = v_s[rows, cols]
            o = jnp.zeros((HG_C, A_DV), F32)
            for tt in range(HG_C):
                d = g[tt:tt + 1, :] - g
                e = jnp.exp(jnp.where(sub <= tt, d, -jnp.inf))
                p = e * (qh[tt:tt + 1, :] * kh)
                a_col = jnp.sum(p, axis=1, keepdims=True)
                o_row = jnp.sum(a_col * vh, axis=0, keepdims=True)
                o = jnp.where(sub == tt, o_row, o)
            st = st_ref[h]
            qg = (qh * jnp.exp(g)).astype(BF16)
            o = o + _nt(qg, st.astype(BF16))
            g_last = g[HG_C - 1:HG_C, :]
            kt = (kh * jnp.exp(g_last - g)).astype(BF16)
            upd = lax.dot_general(vh.astype(BF16), kt, (((0,), (0,)), ((), ())),
                                  preferred_element_type=F32)
            st_ref[h] = st * jnp.exp(g_last) + upd
            on = o * lax.rsqrt(jnp.mean(o * o, axis=-1, keepdims=True) + EPS) * ng
            ya_ref[rows, cols] = (on * _silu(ag_ref[rows, cols])).astype(ya_ref.dtype)
        return carry

    lax.fori_loop(0, HG_TB // HG_C, chunk, 0)

    @pl.when(t == pl.num_programs(0) - 1)
    def _():
        st_out_ref[...] = st_ref[...]


def hgrn_prompt(z, lb_logits, norm_g):
    m = z.shape[0]
    w = A_HEADS * A_DK

    def zspec(cb):
        return pl.BlockSpec((HG_TB, w), lambda t, cb=cb: (t, cb))

    return pl.pallas_call(
        _hgrn_prompt_kernel,
        grid=(m // HG_TB,),
        in_specs=[zspec(C_AQ // w), zspec(C_AF // w), zspec(C_AI // w), zspec(C_AG // w),
                  pl.BlockSpec(lb_logits.shape, lambda t: (0, 0)),
                  pl.BlockSpec((1, A_DV), lambda t: (0, 0))],
        out_specs=[pl.BlockSpec((HG_TB, w), lambda t: (t, 0)),
                   pl.BlockSpec((A_HEADS, A_DV, A_DK), lambda t: (0, 0, 0))],
        out_shape=[jax.ShapeDtypeStruct((m, w), BF16),
                   jax.ShapeDtypeStruct((A_HEADS, A_DV, A_DK), F32)],
        scratch_shapes=[pltpu.VMEM((A_HEADS, A_DV, A_DK), F32)] + [pltpu.VMEM((HG_TB, w), F32)] * 4,
        compiler_params=_cparams("arbitrary"),
        name="hgrn_prompt",
    )(z, z, z, z, lb_logits, norm_g.reshape(1, A_DV))


def _hgrn_step_kernel(z_ref, lbl_ref, ng_ref, s_ref, ya_ref, s_out_ref):
    lbl = lbl_ref[...]
    mx = jnp.max(lbl, axis=0, keepdims=True)
    ex = jnp.exp(lbl - mx)
    lb = ex[0:1, :] / jnp.sum(ex, axis=0, keepdims=True)
    z = z_ref[0]
    w = A_HEADS * A_DK
    q = _silu(z[:, 0:w])
    f = lb + (1.0 - lb) * jax.nn.sigmoid(z[:, w:2 * w])
    kk = 1.0 - f
    v = z[:, 2 * w:3 * w]
    ag = z[:, 3 * w:4 * w]
    rows = []
    for h in range(A_HEADS):
        cols = slice(h * A_DK, (h + 1) * A_DK)
        rows += [f[:, cols], kk[:, cols], q[:, cols]]
    rows.append(jnp.zeros((LANES - 3 * A_HEADS, A_DK), F32))
    xt = jnp.concatenate(rows, axis=0).T
    ng = ng_ref[...]
    r16 = lambda a: a.astype(BF16).astype(F32)
    outs = []
    for h in range(A_HEADS):
        cols = slice(h * A_DV, (h + 1) * A_DV)
        fcol = xt[:, 3 * h:3 * h + 1]
        kcol = xt[:, 3 * h + 1:3 * h + 2]
        qcol = xt[:, 3 * h + 2:3 * h + 3]
        s_old = s_ref[0, h]
        s_out_ref[0, h] = fcol * s_old + kcol * v[:, cols]
        o = (jnp.sum(r16(qcol * fcol) * r16(s_old), axis=0, keepdims=True)
             + jnp.sum(qcol * kcol, axis=0, keepdims=True) * v[:, cols])
        on = o * lax.rsqrt(jnp.mean(o * o, axis=-1, keepdims=True) + EPS) * ng
        outs.append(on * _silu(ag[:, cols]))
    ya_ref[0] = jnp.concatenate(outs, axis=1).astype(ya_ref.dtype)


def hgrn_step(z4, lb_logits, norm_g, state):
    b = z4.shape[0]
    w = A_HEADS * A_DK
    return pl.pallas_call(
        _hgrn_step_kernel,
        grid=(b,),
        in_specs=[pl.BlockSpec((1, 1, 4 * w), lambda i: (i, 0, 0)),
                  pl.BlockSpec(lb_logits.shape, lambda i: (0, 0)),
                  pl.BlockSpec((1, A_DV), lambda i: (0, 0)),
                  pl.BlockSpec((1, A_HEADS, A_DK, A_DV), lambda i: (i, 0, 0, 0))],
        out_specs=[pl.BlockSpec((1, 1, w), lambda i: (i, 0, 0)),
                   pl.BlockSpec((1, A_HEADS, A_DK, A_DV), lambda i: (i, 0, 0, 0))],
        out_shape=[jax.ShapeDtypeStruct((b, 1, w), F32),
                   jax.ShapeDtypeStruct(state.shape, F32)],
        compiler_params=_cparams("arbitrary"),
        name="hgrn_step",
    )(z4.reshape(b, 1, 4 * w), lb_logits, norm_g.reshape(1, A_DV), state)


BISECT_MAX_ITERS = 48


def _bisect_threshold(count_ge, lo, hi, cnt_lo, topk):
    kf = float(topk)

    def cond(c):
        return jnp.logical_and(c[0] < BISECT_MAX_ITERS, c[-1] > 0.0)

    def body(c):
        it, lo, hi, cl, ch, _ = c
        mid = 0.5 * lo + 0.5 * hi
        t_int = hi - (hi - lo) * ((kf - ch) / jnp.maximum(cl - ch, 1.0))
        ok = jnp.logical_and(it % 2 == 0, jnp.logical_and(t_int > lo, t_int < hi))
        t = jnp.where(ok, t_int, mid)
        cm = count_ge(t)
        ge = cm >= kf
        lo = jnp.where(ge, t, lo)
        cl = jnp.where(ge, cm, cl)
        hi = jnp.where(ge, hi, t)
        ch = jnp.where(ge, ch, cm)
        busy = jnp.max(jnp.where(cl > kf, 1.0, 0.0))
        return it + 1, lo, hi, cl, ch, busy

    busy0 = jnp.max(jnp.where(cnt_lo > kf, 1.0, 0.0))
    out = lax.while_loop(cond, body, (jnp.int32(0), lo, hi, cnt_lo, jnp.ones_like(cnt_lo), busy0))
    return out[1]


DSA_QB = 128
DSA_W = 512
DSA_W3 = 1024


def _dsa_prompt_kernel(iq_ref, bq_ref, iw_ref, kidx_ref, k_ref, v_ref, bias_ref, o_ref,
                       score_s, qih_s, qs_s, wb_s, m_s, l_s, acc_s, *, topk):
    i = pl.program_id(0)
    nsub = DSA_W // LANES
    nsub3 = DSA_W3 // LANES
    nch3 = (i * DSA_QB + DSA_QB + DSA_W3 - 1) // DSA_W3
    nch = nch3 * (DSA_W3 // DSA_W)
    qpos = i * DSA_QB + lax.broadcasted_iota(jnp.int32, (DSA_QB, 1), 0)

    iw = iw_ref[...]
    wscale = IDX_DIM ** -0.5 * IDX_HEADS ** -0.5
    for h in range(IDX_HEADS):
        qih_s[h] = iq_ref[:, h * IDX_DIM:(h + 1) * IDX_DIM].astype(BF16)
        wb_s[h] = jnp.broadcast_to(iw[:, h:h + 1] * wscale, (DSA_QB, LANES))
    for h in range(B_HEADS):
        qs_s[h // B_GROUP, (h % B_GROUP) * DSA_QB:(h % B_GROUP + 1) * DSA_QB, :] = (
            bq_ref[:, h * B_HEAD_DIM:(h + 1) * B_HEAD_DIM] * B_HEAD_DIM ** -0.5).astype(BF16)

    def p1(c, carry):
        c0 = pl.multiple_of(c * DSA_W, DSA_W)
        kc = kidx_ref[pl.ds(c0, DSA_W), :]
        sc = [jnp.zeros((DSA_QB, LANES), F32) for _ in range(nsub)]
        for h in range(IDX_HEADS):
            s = jnp.maximum(_nt(qih_s[h], kc), 0.0)
            wb = wb_s[h]
            for j in range(nsub):
                sc[j] = sc[j] + s[:, j * LANES:(j + 1) * LANES] * wb
        for j in range(nsub):
            kpos = c0 + j * LANES + lax.broadcasted_iota(jnp.int32, (1, LANES), 1)
            score_s[c * nsub + j] = jnp.where(kpos <= qpos, sc[j], -jnp.inf)
        return carry

    lax.fori_loop(0, nch, p1, 0)

    def stats(c, carry):
        mn, mx = carry
        for j in range(nsub):
            s = score_s[c * nsub + j]
            mx = jnp.maximum(mx, s)
            mn = jnp.minimum(mn, jnp.where(s > -jnp.inf, s, jnp.inf))
        return mn, mx

    mn, mx = lax.fori_loop(0, nch, stats, (jnp.full((DSA_QB, LANES), jnp.inf, F32),
                                           jnp.full((DSA_QB, LANES), -jnp.inf, F32)))
    lo0 = jnp.min(mn, axis=1, keepdims=True)
    hi0 = jnp.max(mx, axis=1, keepdims=True)

    def count_ge(thr):
        thr_b = jnp.broadcast_to(thr, (DSA_QB, LANES))

        def body(c, acc):
            for j in range(nsub):
                s = score_s[c * nsub + j]
                acc = acc + jnp.where(s >= thr_b, 1.0, 0.0)
            return acc

        acc = lax.fori_loop(0, nch, body, jnp.zeros((DSA_QB, LANES), F32))
        return jnp.sum(acc, axis=1, keepdims=True)

    thr = _bisect_threshold(count_ge, lo0, hi0, (qpos + 1).astype(F32), topk)
    thr_b = jnp.broadcast_to(thr, (DSA_QB, LANES))

    m_s[...] = jnp.full(m_s.shape, NEG, F32)
    l_s[...] = jnp.zeros(l_s.shape, F32)
    acc_s[...] = jnp.zeros(acc_s.shape, F32)

    def p3(c, with_bias):
        c0 = pl.multiple_of(c * DSA_W3, DSA_W3)
        madd = jnp.concatenate([jnp.where(score_s[c * nsub3 + j] >= thr_b, 0.0, NEG) for j in range(nsub3)], axis=1)
        kc = k_ref[pl.ds(c0, DSA_W3), :]
        vc = v_ref[pl.ds(c0, DSA_W3), :]
        rel = [i - (c * nsub3 + j) for j in range(nsub3)]

        def scores(n):
            return _nt(qs_s[n], kc[:, n * B_HEAD_DIM:(n + 1) * B_HEAD_DIM])

        def softmax(n, lg):
            lg = lg.reshape(B_GROUP, DSA_QB, DSA_W3) + madd[None]
            if with_bias:
                lg = lg + jnp.stack([jnp.concatenate(
                    [jnp.where(rel[j] == 0, bias_ref[n * B_GROUP + gq, 0],
                               jnp.where(rel[j] == 1, bias_ref[n * B_GROUP + gq, 1], 0.0)) for j in range(nsub3)],
                    axis=1) for gq in range(B_GROUP)])
            m_old = m_s[n]
            m_new = jnp.maximum(m_old, jnp.max(lg, axis=-1, keepdims=True))
            p = jnp.exp(lg - m_new)
            alpha = jnp.exp(m_old - m_new)
            l_s[n] = alpha * l_s[n] + jnp.sum(p, axis=-1, keepdims=True)
            m_s[n] = m_new
            pv = jnp.dot(p.reshape(B_GROUP * DSA_QB, DSA_W3).astype(BF16), vc[:, n * B_HEAD_DIM:(n + 1) * B_HEAD_DIM],
                         preferred_element_type=F32)
            return alpha, pv.reshape(B_GROUP, DSA_QB, B_HEAD_DIM)

        lgs = [scores(n) for n in range(B_KV_HEADS)]
        outs = [softmax(n, lgs[n]) for n in range(B_KV_HEADS)]
        for n in range(B_KV_HEADS):
            acc_s[n] = outs[n][0] * acc_s[n] + outs[n][1]

    n_far = jnp.maximum(i - 1, 0) // nsub3
    lax.fori_loop(0, n_far, lambda c, carry: (p3(c, False), carry)[1], 0)
    lax.fori_loop(n_far, nch3, lambda c, carry: (p3(c, True), carry)[1], 0)

    for h in range(B_HEADS):
        n, gq = h // B_GROUP, h % B_GROUP
        o_ref[:, h * B_HEAD_DIM:(h + 1) * B_HEAD_DIM] = (acc_s[n, gq] / l_s[n, gq]).astype(o_ref.dtype)


def dsa_prompt(z, zsmall, kidx_bf, k_bf, v_bf, bias_tiles):
    m = z.shape[0]
    topk = min(TOPK_MAX, m // 4)
    wq = B_HEADS * B_HEAD_DIM
    kern = functools.partial(_dsa_prompt_kernel, topk=topk)
    return pl.pallas_call(
        kern,
        grid=(m // DSA_QB,),
        in_specs=[pl.BlockSpec((DSA_QB, IDX_HEADS * IDX_DIM), lambda i: (i, C_IQ // (IDX_HEADS * IDX_DIM))),
                  pl.BlockSpec((DSA_QB, wq), lambda i: (i, C_BQ // wq)),
                  pl.BlockSpec((DSA_QB, LANES), lambda i: (i, 0)),
                  pl.BlockSpec(kidx_bf.shape, lambda i: (0, 0)),
                  pl.BlockSpec(k_bf.shape, lambda i: (0, 0)),
                  pl.BlockSpec(v_bf.shape, lambda i: (0, 0)),
                  pl.BlockSpec(bias_tiles.shape, lambda i: (0, 0, 0, 0))],
        out_specs=pl.BlockSpec((DSA_QB, wq), lambda i: (i, 0)),
        out_shape=jax.ShapeDtypeStruct((m, wq), BF16),
        scratch_shapes=[pltpu.VMEM((m // LANES, DSA_QB, LANES), F32),
                        pltpu.VMEM((IDX_HEADS, DSA_QB, IDX_DIM), BF16),
                        pltpu.VMEM((B_KV_HEADS, B_GROUP * DSA_QB, B_HEAD_DIM), BF16),
                        pltpu.VMEM((IDX_HEADS, DSA_QB, LANES), F32),
                        pltpu.VMEM((B_KV_HEADS, B_GROUP, DSA_QB, 1), F32),
                        pltpu.VMEM((B_KV_HEADS, B_GROUP, DSA_QB, 1), F32),
                        pltpu.VMEM((B_KV_HEADS, B_GROUP, DSA_QB, B_HEAD_DIM), F32)],
        compiler_params=_cparams("arbitrary"),
        name="dsa_prompt",
    )(z, z, zsmall, kidx_bf, k_bf, v_bf, bias_tiles)


def _page_copies(table_ref, b, n_pages, src_hbm, dst, sem, rows_per_page=PAGE_SIZE):
    def copy(p):
        return pltpu.make_async_copy(src_hbm.at[table_ref[b, p]],
                                     dst.at[pl.ds(p * rows_per_page, rows_per_page)], sem)
    return copy


def _dsa_scores_kernel(pt_ref, iq_ref, iw_ref, iknew_ref, kidx_hbm, o_ref, buf, sem, *, n_pages):
    b = pl.program_id(0)
    nb = pl.num_programs(0)
    past = n_pages * PAGE_SIZE

    def start(bb, slot):
        cp = _page_copies(pt_ref, bb, n_pages, kidx_hbm, buf.at[slot], sem.at[slot])
        lax.fori_loop(0, n_pages, lambda p, c: (cp(p).start(), c)[1], 0)

    def wait(bb, slot):
        cp = _page_copies(pt_ref, bb, n_pages, kidx_hbm, buf.at[slot], sem.at[slot])
        lax.fori_loop(0, n_pages, lambda p, c: (cp(p).wait(), c)[1], 0)

    slot = b % 2

    @pl.when(b == 0)
    def _():
        start(0, 0)

    @pl.when(b + 1 < nb)
    def _():
        start(b + 1, 1 - slot)

    wait(b, slot)

    r16 = lambda a: a.astype(BF16).astype(F32)
    qi = iq_ref[0].astype(BF16)
    wcol = r16(iw_ref[0]) * (IDX_DIM ** -0.5 * IDX_HEADS ** -0.5)
    s = r16(jnp.maximum(_nt(qi, buf[slot].astype(BF16)), 0.0))
    o_ref[0, :, 0:past] = jnp.sum(s * wcol, axis=0, keepdims=True)
    sn = r16(jnp.maximum(_nt(qi, iknew_ref[0].astype(BF16)), 0.0))
    sn = jnp.sum(sn * wcol, axis=0, keepdims=True)
    lane = lax.broadcasted_iota(jnp.int32, (1, LANES), 1)
    o_ref[0, :, past:past + LANES] = jnp.where(lane == 0, sn, -jnp.inf)


def dsa_scores(page_table, iq, iw, iknew_pad, cache_kidx):
    b, n_pages = page_table.shape
    past = n_pages * PAGE_SIZE
    kern = functools.partial(_dsa_scores_kernel, n_pages=n_pages)
    gs = pltpu.PrefetchScalarGridSpec(
        num_scalar_prefetch=1,
        grid=(b,),
        in_specs=[pl.BlockSpec((1, IDX_HEADS, IDX_DIM), lambda i, pt: (i, 0, 0)),
                  pl.BlockSpec((1, IDX_HEADS, 1), lambda i, pt: (i, 0, 0)),
                  pl.BlockSpec((1, LANES, IDX_DIM), lambda i, pt: (i, 0, 0)),
                  pl.BlockSpec(memory_space=pl.ANY)],
        out_specs=pl.BlockSpec((1, 1, past + LANES), lambda i, pt: (i, 0, 0)),
        scratch_shapes=[pltpu.VMEM((2, past, IDX_DIM), F32), pltpu.SemaphoreType.DMA((2,))],
    )
    return pl.pallas_call(
        kern, grid_spec=gs,
        out_shape=jax.ShapeDtypeStruct((b, 1, past + LANES), F32),
        compiler_params=_cparams("arbitrary"),
        name="dsa_scores",
    )(page_table, iq, iw, iknew_pad, cache_kidx)


def _dsa_threshold_kernel(s_ref, thr_ref, *, topk):
    s = s_ref[...]
    nb = s.shape[0]
    finite = s > -jnp.inf
    lo0 = jnp.min(jnp.where(finite, s, jnp.inf), axis=1, keepdims=True)
    hi0 = jnp.max(s, axis=1, keepdims=True)
    cnt0 = jnp.sum(jnp.where(finite, 1.0, 0.0), axis=1, keepdims=True)

    def count_ge(thr):
        return jnp.sum(jnp.where(s_ref[...] >= thr, 1.0, 0.0), axis=1, keepdims=True)

    thr = _bisect_threshold(count_ge, lo0, hi0, cnt0, topk)
    thr_ref[...] = jnp.broadcast_to(thr, (nb, LANES))


def dsa_threshold(scores, topk):
    b, l = scores.shape
    return pl.pallas_call(
        functools.partial(_dsa_threshold_kernel, topk=topk),
        grid=(1,),
        in_specs=[pl.BlockSpec((b, l), lambda i: (0, 0))],
        out_specs=pl.BlockSpec((b, LANES), lambda i: (0, 0)),
        out_shape=jax.ShapeDtypeStruct((b, LANES), F32),
        compiler_params=_cparams("arbitrary"),
        name="dsa_threshold",
    )(scores)


def _dsa_decode_kernel(pt_ref, q_ref, s_ref, thr_ref, knew_ref, vnew_ref, bias_ref, k_hbm, v_hbm, o_ref,
                       kbuf, vbuf, sem, *, n_pages):
    b = pl.program_id(0)
    nb = pl.num_programs(0)
    past = n_pages * PAGE_SIZE

    rpp = PAGE_SIZE * B_KV_HEADS

    def copies(bb, slot):
        ck = _page_copies(pt_ref, bb, n_pages, k_hbm, kbuf.at[slot], sem.at[0, slot], rpp)
        cv = _page_copies(pt_ref, bb, n_pages, v_hbm, vbuf.at[slot], sem.at[1, slot], rpp)
        return ck, cv

    def start(bb, slot):
        ck, cv = copies(bb, slot)
        lax.fori_loop(0, n_pages, lambda p, c: (ck(p).start(), cv(p).start(), c)[2], 0)

    def wait(bb, slot):
        ck, cv = copies(bb, slot)
        lax.fori_loop(0, n_pages, lambda p, c: (ck(p).wait(), cv(p).wait(), c)[2], 0)

    slot = b % 2

    @pl.when(b == 0)
    def _():
        kbuf[:, n_pages * rpp:, :] = jnp.zeros((2, rpp, B_HEAD_DIM), F32)
        vbuf[:, n_pages * rpp:, :] = jnp.zeros((2, rpp, B_HEAD_DIM), F32)
        start(0, 0)

    @pl.when(b + 1 < nb)
    def _():
        start(b + 1, 1 - slot)

    kbuf[slot, n_pages * rpp:n_pages * rpp + 8, :] = knew_ref[0]
    vbuf[slot, n_pages * rpp:n_pages * rpp + 8, :] = vnew_ref[0]
    wait(b, slot)

    sel = s_ref[0] >= thr_ref[0][:, 0:1]
    n_keys = past + PAGE_SIZE
    outs = []
    for n in range(B_KV_HEADS):
        kn = kbuf[slot, pl.ds(n, n_keys, stride=B_KV_HEADS), :].astype(BF16)
        vn = vbuf[slot, pl.ds(n, n_keys, stride=B_KV_HEADS), :].astype(BF16)
        qn = q_ref[0, n].astype(BF16)
        lg = _nt(qn, kn) * B_HEAD_DIM ** -0.5 + bias_ref[n]
        m = jnp.max(jnp.where(sel, lg, NEG), axis=1, keepdims=True)
        p = jnp.where(sel, jnp.exp(lg - m), 0.0)
        p = p / jnp.sum(p, axis=1, keepdims=True)
        outs.append(jnp.dot(p.astype(BF16), vn, preferred_element_type=F32))
    o_ref[0] = jnp.concatenate(outs, axis=0).astype(o_ref.dtype)


def dsa_decode(page_table, q8, scores, thr, knew8, vnew8, bias_rows, cache_k2, cache_v2):
    b, n_pages = page_table.shape
    past = n_pages * PAGE_SIZE
    l = past + LANES
    wkv = B_KV_HEADS * B_HEAD_DIM
    kern = functools.partial(_dsa_decode_kernel, n_pages=n_pages)
    gs = pltpu.PrefetchScalarGridSpec(
        num_scalar_prefetch=1,
        grid=(b,),
        in_specs=[pl.BlockSpec((1, B_KV_HEADS, 8, B_HEAD_DIM), lambda i, pt: (i, 0, 0, 0)),
                  pl.BlockSpec((1, 1, l), lambda i, pt: (i, 0, 0)),
                  pl.BlockSpec((1, 1, LANES), lambda i, pt: (i, 0, 0)),
                  pl.BlockSpec((1, 8, B_HEAD_DIM), lambda i, pt: (i, 0, 0)),
                  pl.BlockSpec((1, 8, B_HEAD_DIM), lambda i, pt: (i, 0, 0)),
                  pl.BlockSpec((B_KV_HEADS, 8, l), lambda i, pt: (0, 0, 0)),
                  pl.BlockSpec(memory_space=pl.ANY),
                  pl.BlockSpec(memory_space=pl.ANY)],
        out_specs=pl.BlockSpec((1, 2 * 8, B_HEAD_DIM), lambda i, pt: (i, 0, 0)),
        scratch_shapes=[pltpu.VMEM((2, l * B_KV_HEADS, B_HEAD_DIM), F32),
                        pltpu.VMEM((2, l * B_KV_HEADS, B_HEAD_DIM), F32),
                        pltpu.SemaphoreType.DMA((2, 2))],
    )
    return pl.pallas_call(
        kern, grid_spec=gs,
        out_shape=jax.ShapeDtypeStruct((b, 2 * 8, B_HEAD_DIM), F32),
        compiler_params=_cparams("arbitrary"),
        name="dsa_decode",
    )(page_table, q8, scores, thr, knew8, vnew8, bias_rows, cache_k2, cache_v2)


def _merge_kernel(ya_ref, yb_ref, ga_ref, gb_ref, wa_ref, wb_ref, o_ref):
    a = _mm(ya_ref[...], wa_ref[...])
    bb = _mm(yb_ref[...], wb_ref[...])
    o_ref[...] = (jax.nn.sigmoid(ga_ref[...]) * a + jax.nn.sigmoid(gb_ref[...]) * bb).astype(o_ref.dtype)


def merge(ya, yb, z, ga_col, gb_col, wa, wb, *, tm, tn):
    m, k = ya.shape
    n = wa.shape[1]
    return pl.pallas_call(
        _merge_kernel,
        grid=(m // tm, n // tn),
        in_specs=[pl.BlockSpec((tm, k), lambda i, j: (i, 0)),
                  pl.BlockSpec((tm, k), lambda i, j: (i, 0)),
                  pl.BlockSpec((tm, tn), lambda i, j: (i, ga_col // tn + j)),
                  pl.BlockSpec((tm, tn), lambda i, j: (i, gb_col // tn + j)),
                  pl.BlockSpec((k, tn), lambda i, j: (0, j)),
                  pl.BlockSpec((k, tn), lambda i, j: (0, j))],
        out_specs=pl.BlockSpec((tm, tn), lambda i, j: (i, j)),
        out_shape=jax.ShapeDtypeStruct((m, n), wa.dtype),
        compiler_params=_cparams("parallel", "arbitrary"),
        name="merge",
    )(ya, yb, z, z, wa, wb)


def _cross_prompt_kernel(x_ref, g_ref, wq_ref, mk_ref, mv_ref, wo_ref, o_ref):
    x = x_ref[...]
    ms = jnp.mean(x * x, axis=-1, keepdims=True)
    h = (x * lax.rsqrt(ms + EPS) * g_ref[...]).astype(BF16)
    q = jnp.dot(h, wq_ref[...], preferred_element_type=F32)
    outs = []
    for hh in range(X_HEADS):
        cols = slice(hh * X_HEAD_DIM, (hh + 1) * X_HEAD_DIM)
        lg = _nt(q[:, cols].astype(BF16), mk_ref[:, cols]) * X_HEAD_DIM ** -0.5
        mx = jnp.max(lg, axis=1, keepdims=True)
        p = jnp.exp(lg - mx)
        p = p / jnp.sum(p, axis=1, keepdims=True)
        outs.append(jnp.dot(p.astype(BF16), mv_ref[:, cols], preferred_element_type=F32).astype(BF16))
    att = jnp.concatenate(outs, axis=1)
    o_ref[...] = x + jnp.dot(att, wo_ref[...], preferred_element_type=F32)


def cross_prompt(x, g, wq, mk, mv, wo, *, tm):
    m, d = x.shape
    full = lambda a: pl.BlockSpec(a.shape, lambda i: (0,) * a.ndim)
    g2 = g.reshape(1, d)
    return pl.pallas_call(
        _cross_prompt_kernel,
        grid=(m // tm,),
        in_specs=[pl.BlockSpec((tm, d), lambda i: (i, 0)), full(g2), full(wq), full(mk), full(mv), full(wo)],
        out_specs=pl.BlockSpec((tm, d), lambda i: (i, 0)),
        out_shape=jax.ShapeDtypeStruct((m, d), F32),
        compiler_params=_cparams("parallel"),
        name="cross_prompt",
    )(x, g2, wq, mk, mv, wo)


def _cross_step_kernel(q_ref, mk_ref, mv_ref, o_ref):
    r16 = lambda a: a.astype(BF16).astype(F32)
    q = r16(q_ref[0])
    outs = []
    for hh in range(X_HEADS):
        cols = slice(hh * X_HEAD_DIM, (hh + 1) * X_HEAD_DIM)
        kh = r16(mk_ref[0, :, cols])
        vh = r16(mv_ref[0, :, cols])
        lg = jnp.sum(kh * q[:, cols], axis=1, keepdims=True) * X_HEAD_DIM ** -0.5
        mx = jnp.max(lg, axis=0, keepdims=True)
        p = jnp.exp(lg - mx)
        p = r16(p / jnp.sum(p, axis=0, keepdims=True))
        outs.append(jnp.sum(p * vh, axis=0, keepdims=True))
    o_ref[0] = jnp.concatenate(outs, axis=1).astype(o_ref.dtype)


def cross_step(q, mk, mv):
    b, w = q.shape
    mem = mk.shape[1]
    return pl.pallas_call(
        _cross_step_kernel,
        grid=(b,),
        in_specs=[pl.BlockSpec((1, 1, w), lambda i: (i, 0, 0)),
                  pl.BlockSpec((1, mem, w), lambda i: (i, 0, 0)),
                  pl.BlockSpec((1, mem, w), lambda i: (i, 0, 0))],
        out_specs=pl.BlockSpec((1, 1, w), lambda i: (i, 0, 0)),
        out_shape=jax.ShapeDtypeStruct((b, 1, w), F32),
        compiler_params=_cparams("arbitrary"),
        name="cross_step",
    )(q.reshape(b, 1, w), mk, mv)


def _router_kernel(x_ref, g_ref, w_ref, b_ref, hf_ref, route_ref):
    x = x_ref[...]
    ms = jnp.mean(x * x, axis=-1, keepdims=True)
    hf = x * lax.rsqrt(ms + EPS) * g_ref[...]
    hf_ref[...] = hf
    lg = _mm(hf, w_ref[...]) + b_ref[...]
    tm = lg.shape[0]
    lane = lax.broadcasted_iota(jnp.int32, (tm, LANES), 1)
    big = jnp.int32(LANES)
    is_g = lane < N_GROUPS
    gmax = jnp.max(jnp.where(is_g, lg, -jnp.inf), axis=1, keepdims=True)
    grp = jnp.min(jnp.where(is_g & (lg == gmax), lane, big), axis=1, keepdims=True)
    p_grp = 1.0 / jnp.sum(jnp.where(is_g, jnp.exp(lg - gmax), 0.0), axis=1, keepdims=True)
    e_lo = N_GROUPS + grp * EXP_PER_GROUP
    in_g = (lane >= e_lo) & (lane < e_lo + EXP_PER_GROUP)
    v1 = jnp.max(jnp.where(in_g, lg, -jnp.inf), axis=1, keepdims=True)
    i1 = jnp.min(jnp.where(in_g & (lg == v1), lane, big), axis=1, keepdims=True)
    rest = in_g & (lane != i1)
    v2 = jnp.max(jnp.where(rest, lg, -jnp.inf), axis=1, keepdims=True)
    i2 = jnp.min(jnp.where(rest & (lg == v2), lane, big), axis=1, keepdims=True)
    e2 = jnp.exp(v2 - v1)
    g1 = p_grp / (1.0 + e2)
    g2 = p_grp * e2 / (1.0 + e2)
    r = jnp.where(lane == 0, (i1 - N_GROUPS).astype(F32),
                  jnp.where(lane == 1, (i2 - N_GROUPS).astype(F32),
                            jnp.where(lane == 2, g1, jnp.where(lane == 3, g2, 0.0))))
    route_ref[...] = r


def router(x, g, w_pad, b_pad, *, tm):
    m, d = x.shape
    return pl.pallas_call(
        _router_kernel,
        grid=(pl.cdiv(m, tm),),
        in_specs=[pl.BlockSpec((tm, d), lambda i: (i, 0)),
                  pl.BlockSpec((1, d), lambda i: (0, 0)),
                  pl.BlockSpec((d, LANES), lambda i: (0, 0)),
                  pl.BlockSpec((1, LANES), lambda i: (0, 0))],
        out_specs=[pl.BlockSpec((tm, d), lambda i: (i, 0)),
                   pl.BlockSpec((tm, LANES), lambda i: (i, 0))],
        out_shape=[jax.ShapeDtypeStruct((m, d), F32), jax.ShapeDtypeStruct((m, LANES), F32)],
        compiler_params=_cparams("parallel"),
        name="router",
    )(x, g.reshape(1, d), w_pad, b_pad)


def _moe_kernel(be_ref, tok_ref, row_ref, hf_hbm, wg_ref, wu_ref, wd_ref, y_hbm, xbuf, obuf, wg_s, wu_s, wd_s,
                sem_in, sem_out):
    b = pl.program_id(0)
    nb = pl.num_programs(0)
    blk = xbuf.shape[1]
    slot = b % 2

    def gather_start(bb, sl):
        for r in range(blk):
            pltpu.make_async_copy(hf_hbm.at[pl.ds(tok_ref[bb * blk + r], 1)], xbuf.at[sl, pl.ds(r, 1)],
                                  sem_in.at[sl]).start()

    def scatter_start(bb, sl):
        for r in range(blk):
            pltpu.make_async_copy(obuf.at[sl, pl.ds(r, 1)], y_hbm.at[pl.ds(row_ref[bb * blk + r], 1)],
                                  sem_out.at[sl]).start()

    def gather_wait(sl):
        pltpu.make_async_copy(hf_hbm.at[pl.ds(0, blk)], xbuf.at[sl], sem_in.at[sl]).wait()

    def scatter_wait(sl):
        pltpu.make_async_copy(obuf.at[sl], y_hbm.at[pl.ds(0, blk)], sem_out.at[sl]).wait()

    @pl.when(b == 0)
    def _():
        gather_start(0, 0)

    changed = jnp.logical_or(b == 0, be_ref[b] != be_ref[jnp.maximum(b - 1, 0)])

    @pl.when(changed)
    def _():
        wg_s[...] = wg_ref[0].astype(BF16)
        wu_s[...] = wu_ref[0].astype(BF16)
        wd_s[...] = wd_ref[0].astype(BF16)

    gather_wait(slot)

    @pl.when(b >= 2)
    def _():
        scatter_wait(slot)

    def step(prefetch, flush_prev):
        if prefetch:
            gather_start(b + 1, 1 - slot)
        if flush_prev:
            scatter_start(b - 1, 1 - slot)
        x = xbuf[slot].astype(BF16)
        gg = jnp.dot(x, wg_s[...], preferred_element_type=F32)
        uu = jnp.dot(x, wu_s[...], preferred_element_type=F32)
        a = (_silu(gg) * uu).astype(BF16)
        obuf[slot] = jnp.dot(a, wd_s[...], preferred_element_type=F32)

    first, last = b == 0, b == nb - 1
    pl.when(jnp.logical_and(first, jnp.logical_not(last)))(lambda: step(True, False))
    pl.when(jnp.logical_and(jnp.logical_not(first), jnp.logical_not(last)))(lambda: step(True, True))
    pl.when(jnp.logical_and(jnp.logical_not(first), last))(lambda: step(False, True))
    pl.when(jnp.logical_and(first, last))(lambda: step(False, False))

    @pl.when(last)
    def _():
        scatter_start(b, slot)
        scatter_wait(slot)

    @pl.when(jnp.logical_and(last, jnp.logical_not(first)))
    def _():
        scatter_wait(1 - slot)


def moe_experts(block_e, slot_tok, slot_row, hf, w_g, w_u, w_d, *, blk, out_rows):
    n_blocks = block_e.shape[0]
    d = hf.shape[1]
    ff = w_g.shape[2]
    gs = pltpu.PrefetchScalarGridSpec(
        num_scalar_prefetch=3,
        grid=(n_blocks,),
        in_specs=[pl.BlockSpec(memory_space=pl.ANY),
                  pl.BlockSpec((1, d, ff), lambda i, be, tk, rw: (be[i], 0, 0)),
                  pl.BlockSpec((1, d, ff), lambda i, be, tk, rw: (be[i], 0, 0)),
                  pl.BlockSpec((1, ff, d), lambda i, be, tk, rw: (be[i], 0, 0))],
        out_specs=pl.BlockSpec(memory_space=pl.ANY),
        scratch_shapes=[pltpu.VMEM((2, blk, d), F32), pltpu.VMEM((2, blk, d), F32),
                        pltpu.VMEM((d, ff), BF16), pltpu.VMEM((d, ff), BF16), pltpu.VMEM((ff, d), BF16),
                        pltpu.SemaphoreType.DMA((2,)), pltpu.SemaphoreType.DMA((2,))],
    )
    return pl.pallas_call(
        _moe_kernel, grid_spec=gs,
        out_shape=jax.ShapeDtypeStruct((out_rows, d), F32),
        compiler_params=_cparams("arbitrary"),
        name="moe_experts",
    )(block_e, slot_tok, slot_row, hf, w_g, w_u, w_d)


def _combine_kernel(x_ref, route_ref, gf_ref, y1_ref, y2_ref, o_ref):
    route = route_ref[...]
    x = x_ref[...] + route[:, 2:3] * y1_ref[...] + route[:, 3:4] * y2_ref[...]
    ms = jnp.mean(x * x, axis=-1, keepdims=True)
    o_ref[...] = x * lax.rsqrt(ms + EPS) * gf_ref[...]


def combine(x, route, gf, y, plane, *, tm):
    m, d = x.shape
    return pl.pallas_call(
        _combine_kernel,
        grid=(pl.cdiv(m, tm),),
        in_specs=[pl.BlockSpec((tm, d), lambda i: (i, 0)),
                  pl.BlockSpec((tm, LANES), lambda i: (i, 0)),
                  pl.BlockSpec((1, d), lambda i: (0, 0)),
                  pl.BlockSpec((tm, d), lambda i: (i, 0)),
                  pl.BlockSpec((tm, d), lambda i: (plane // tm + i, 0))],
        out_specs=pl.BlockSpec((tm, d), lambda i: (i, 0)),
        out_shape=jax.ShapeDtypeStruct((m, d), F32),
        compiler_params=_cparams("parallel"),
        name="combine",
    )(x, route, gf.reshape(1, d), y, y)


def _t5_bucket(dist):
    dist = jnp.asarray(dist, jnp.int32)
    max_exact = REL_BUCKETS // 2
    dist_f = jnp.maximum(dist, 1).astype(F32)
    large = max_exact + (jnp.log(dist_f / max_exact) / math.log(REL_MAX_DIST / max_exact)
                         * (REL_BUCKETS - max_exact)).astype(jnp.int32)
    large = jnp.minimum(large, REL_BUCKETS - 1)
    return jnp.where(dist < max_exact, dist, large)


def _bias_tables(rel_bias, past):
    r = np.arange(LANES)
    diff = r[:, None] - r[None, :]
    buckets = jnp.stack([_t5_bucket(np.maximum(diff, 0)),
                         _t5_bucket(np.maximum(diff + LANES, 0)),
                         _t5_bucket(np.full((LANES, LANES), 2 * LANES))])
    def lookup(bkt):
        oh = (bkt.reshape(-1, 1) == jnp.arange(REL_BUCKETS)[None, :]).astype(F32)
        out = jnp.dot(oh, rel_bias.astype(F32), precision=lax.Precision.HIGHEST)
        return out.T.reshape((rel_bias.shape[1],) + bkt.shape)

    tiles = lookup(buckets)
    tiles = tiles - tiles[:, 2:3]
    dist = np.maximum(past - np.arange(past + LANES), 0)
    rows = lookup(_t5_bucket(dist))
    rows = rows.reshape(B_KV_HEADS, B_GROUP, past + LANES)
    rows = jnp.concatenate([rows, jnp.zeros_like(rows)], axis=1)
    return tiles, rows


def _dispatch(eid, n_tokens, plane, blk):
    a = eid.shape[0]
    assert a == EXPERT_TOPK * n_tokens
    n_blocks = -(-(a + N_EXPERTS * (blk - 1)) // blk)
    rows = n_blocks * blk
    gap = plane - n_tokens
    assert rows - a >= EXPERT_TOPK * gap
    ar = jnp.arange(a, dtype=jnp.int32)
    out_row = (ar % EXPERT_TOPK) * plane + ar // EXPERT_TOPK
    order = jnp.argsort(eid)
    e_sorted = eid[order]
    counts = jnp.bincount(eid, length=N_EXPERTS)
    starts = jnp.cumsum(counts) - counts
    padded = (counts + blk - 1) // blk * blk
    pad_end = jnp.cumsum(padded)
    pad_start = pad_end - padded
    dest_sorted = (pad_start[e_sorted] + ar - starts[e_sorted]).astype(jnp.int32)
    block_e = jnp.minimum(jnp.sum(pad_end[None, :] <= (jnp.arange(n_blocks) * blk)[:, None], axis=1), N_EXPERTS - 1)
    e_slot = jnp.repeat(block_e, blk)
    sl = jnp.arange(rows, dtype=jnp.int32)
    is_real = sl - pad_start[e_slot] < counts[e_slot]
    j = sl - jnp.cumsum(counts)[e_slot]
    pad_row = jnp.where(j < gap, n_tokens + j,
                        jnp.where(j < 2 * gap, plane + n_tokens + j - gap, EXPERT_TOPK * plane + j - 2 * gap))
    slot_row = pad_row.astype(jnp.int32).at[dest_sorted].set(out_row[order])
    slot_tok = jnp.where(is_real, slot_row % plane, 0)
    return slot_tok.astype(jnp.int32), slot_row, block_e.astype(jnp.int32), rows


def kernel(x_prompt, x_sample, mem_prompt, cache_k, cache_v, cache_kidx, page_table, state_hgrn, cache_mem_k,
           cache_mem_v, norm_mix, w_in, hgrn_lb_logits, hgrn_norm, w_branch_a, w_branch_b, w_out, norm_cross, w_xq,
           w_xk, w_xv, w_xo, norm_ffn, w_router_group, b_router_group, w_router_expert, b_router_expert, w_exp_gate,
           w_exp_up, w_exp_down, rel_bias, norm_final):
    assert w_in.shape[0] == 1, "single-layer step"
    l = 0
    drop0 = lambda a: a.reshape(a.shape[1:])
    bp, t, d = x_prompt.shape
    db = x_sample.shape[0]
    past = page_table.shape[1] * PAGE_SIZE
    xp = x_prompt.reshape(bp * t, d)
    xs = x_sample.reshape(db, d)

    wi = drop0(w_in)
    o = np.cumsum((0,) + (1024, 1024, 1024, 1024, 1024, 256, 256, 1024, 16, 64, 2048, 2048))
    seg = lambda i: wi[:, o[i]:o[i + 1]]
    w_cat = jnp.concatenate([seg(0), seg(1), seg(2), seg(3), seg(4), seg(7), seg(10), seg(11), seg(5), seg(6)],
                            axis=1).astype(BF16)
    w_small = jnp.pad(jnp.concatenate([seg(8), seg(9)], axis=1), ((0, 0), (0, LANES - 80))).astype(BF16)
    wa, wb, wo = w_branch_a[l].astype(BF16), w_branch_b[l].astype(BF16), w_out[l].astype(BF16)
    wxq, wxk, wxv, wxo = (w_xq[l].astype(BF16), w_xk[l].astype(BF16), w_xv[l].astype(BF16), w_xo[l].astype(BF16))
    w_route = jnp.pad(jnp.concatenate([w_router_group[l], w_router_expert[l]], axis=1),
                      ((0, 0), (0, LANES - N_GROUPS - N_EXPERTS))).astype(BF16)
    b_route = jnp.pad(jnp.concatenate([b_router_group[l], b_router_expert[l]]),
                      (0, LANES - N_GROUPS - N_EXPERTS)).reshape(1, LANES)
    bias_tiles, bias_rows = _bias_tables(rel_bias, past)

    zp = norm_matmul(xp, norm_mix[l], w_cat, tm=1024, tn=512)
    zps = norm_matmul(xp, norm_mix[l], w_small, tm=1024, tn=LANES)
    kp = zp[:, C_BK:C_BK + 256]
    vp = zp[:, C_BV:C_BV + 256]
    ikp = zps[:, IDX_HEADS:IDX_HEADS + IDX_DIM]
    ya_p, st_p = hgrn_prompt(zp, hgrn_lb_logits, hgrn_norm[l])
    yb_p = dsa_prompt(zp, zps, ikp.astype(BF16), kp.astype(BF16), vp.astype(BF16), bias_tiles)
    mg_p = merge(ya_p, yb_p, zp, C_GA, C_GB, wa, wb, tm=512, tn=512)
    x1p = matmul(mg_p, wo, xp, tm=512, tn=512)
    memp = mem_prompt.reshape(-1, d)
    mk = matmul(memp, wxk, tm=memp.shape[0], tn=512)
    mv = matmul(memp, wxv, tm=memp.shape[0], tn=512)
    x2p = cross_prompt(x1p, norm_cross[l], wxq, mk.astype(BF16), mv.astype(BF16), wxo, tm=512)

    zs = norm_matmul(xs, norm_mix[l], w_cat, tm=db, tn=512)
    zss = norm_matmul(xs, norm_mix[l], w_small, tm=db, tn=LANES)
    ks = zs[:, C_BK:C_BK + 256]
    vs = zs[:, C_BV:C_BV + 256]
    iks = zss[:, IDX_HEADS:IDX_HEADS + IDX_DIM]
    ya_s, st_s = hgrn_step(zs[:, :4 * A_HEADS * A_DK], hgrn_lb_logits, hgrn_norm[l], drop0(state_hgrn))
    iq_s = zs[:, C_IQ:C_IQ + IDX_HEADS * IDX_DIM].reshape(db, IDX_HEADS, IDX_DIM)
    iw_s = zss[:, :IDX_HEADS].reshape(db, IDX_HEADS, 1)
    iknew_pad = jnp.pad(iks[:, None, :], ((0, 0), (0, LANES - 1), (0, 0)))
    scores = dsa_scores(page_table, iq_s, iw_s, iknew_pad, drop0(cache_kidx)).reshape(db, past + LANES)
    topk_s = min(TOPK_MAX, (past + 1) // 4)
    thr = dsa_threshold(scores, topk_s)
    q8 = jnp.pad(zs[:, C_BQ:C_BQ + B_HEADS * B_HEAD_DIM].reshape(db, B_KV_HEADS, B_GROUP, B_HEAD_DIM),
                 ((0, 0), (0, 0), (0, 8 - B_GROUP), (0, 0)))
    knew8 = jnp.pad(ks.reshape(db, B_KV_HEADS, B_HEAD_DIM), ((0, 0), (0, 8 - B_KV_HEADS), (0, 0)))
    vnew8 = jnp.pad(vs.reshape(db, B_KV_HEADS, B_HEAD_DIM), ((0, 0), (0, 8 - B_KV_HEADS), (0, 0)))
    n_pool = cache_k.shape[1]
    ob = dsa_decode(page_table, q8, scores.reshape(db, 1, -1), thr.reshape(db, 1, LANES), knew8, vnew8, bias_rows,
                    cache_k.reshape(n_pool, PAGE_SIZE * B_KV_HEADS, B_HEAD_DIM),
                    cache_v.reshape(n_pool, PAGE_SIZE * B_KV_HEADS, B_HEAD_DIM))
    yb_s = ob.reshape(db, B_KV_HEADS, 8, B_HEAD_DIM)[:, :, :B_GROUP].reshape(db, B_HEADS * B_HEAD_DIM)
    mg_s = merge(ya_s.reshape(db, -1), yb_s, zs, C_GA, C_GB, wa, wb, tm=db, tn=512)
    x1s = matmul(mg_s, wo, xs, tm=db, tn=512)
    qx_s = norm_matmul(x1s, norm_cross[l], wxq, tm=db, tn=512)
    mem = cache_mem_k.shape[2]
    att_s = cross_step(qx_s, cache_mem_k.reshape(db, mem, -1), cache_mem_v.reshape(db, mem, -1))
    x2s = matmul(att_s.reshape(db, -1), wxo, x1s, tm=db, tn=512)

    x2 = jnp.concatenate([x2p, x2s], axis=0)
    n = x2.shape[0]
    hf, route = router(x2, norm_ffn[l], w_route, b_route, tm=256)
    eid = route[:, :EXPERT_TOPK].astype(jnp.int32).reshape(-1)
    a = n * EXPERT_TOPK
    blk = min(MOE_BLOCK, max(8, a // N_EXPERTS))
    tmc = 128
    plane = -(-n // tmc) * tmc
    slot_tok, slot_row, block_e, rows = _dispatch(eid, n, plane, blk)
    ye = moe_experts(block_e, slot_tok, slot_row, hf, drop0(w_exp_gate), drop0(w_exp_up), drop0(w_exp_down),
                     blk=blk, out_rows=rows)
    y = combine(x2, route, norm_final, ye, plane, tm=tmc)

    y_prompt = y[:bp * t].reshape(bp, t, d)
    y_sample = y[bp * t:].reshape(db, 1, d)
    return (y_prompt, y_sample,
            kp.reshape(1, bp, t, B_KV_HEADS, B_HEAD_DIM), vp.reshape(1, bp, t, B_KV_HEADS, B_HEAD_DIM),
            ikp.reshape(1, bp, t, IDX_DIM),
            jnp.swapaxes(st_p, 1, 2).reshape(1, bp, A_HEADS, A_DK, A_DV),
            mk.reshape(1, bp, -1, X_HEADS, X_HEAD_DIM), mv.reshape(1, bp, -1, X_HEADS, X_HEAD_DIM),
            ks.reshape(1, db, 1, B_KV_HEADS, B_HEAD_DIM), vs.reshape(1, db, 1, B_KV_HEADS, B_HEAD_DIM),
            iks.reshape(1, db, 1, IDX_DIM),
            st_s.reshape(1, db, A_HEADS, A_DK, A_DV))
```

```python
import functools
import math

import jax
import jax.numpy as jnp
import numpy as np
from jax import lax
from jax.experimental import pallas as pl
from jax.experimental.pallas import tpu as pltpu

F32 = jnp.float32
BF16 = jnp.bfloat16
EPS = 1e-6

D_MODEL = 2048
A_HEADS, A_DK, A_DV = 8, 128, 128
B_HEADS, B_KV_HEADS, B_HEAD_DIM = 8, 2, 128
B_GROUP = B_HEADS // B_KV_HEADS
IDX_HEADS, IDX_DIM = 16, 64
TOPK_MAX = 256
PAGE_SIZE = 128
REL_BUCKETS, REL_MAX_DIST = 32, 128
X_HEADS, X_HEAD_DIM = 4, 128
N_GROUPS, EXP_PER_GROUP = 4, 8
N_EXPERTS = N_GROUPS * EXP_PER_GROUP
EXPERT_TOPK = 2
EXPERT_FF = 512
MOE_BLOCK = 128

LANES = 128
VMEM_LIMIT = 56 * 1024 * 1024

NEG = -1e30

C_AQ, C_AF, C_AI, C_AG, C_BQ, C_BK, C_BV, C_IQ = 0, 1024, 2048, 3072, 4096, 5120, 5376, 5632
NZ_MAIN = 6656
T_GA, T_GB, T_IW = 0, 2048, 4096
NZ_TAIL = 4608


def _cparams(*sem):
    return pltpu.CompilerParams(dimension_semantics=sem, vmem_limit_bytes=VMEM_LIMIT)


def _silu(x):
    return x * jax.nn.sigmoid(x)


def _nt(a, b):
    return lax.dot_general(a, b, (((1,), (1,)), ((), ())), preferred_element_type=F32)


def _nt_f32(a, b):
    return lax.dot_general(a, b, (((1,), (1,)), ((), ())), preferred_element_type=F32,
                           precision=lax.Precision.HIGHEST)


def _mm(a, w):
    return jnp.dot(a.astype(BF16), w.astype(BF16), preferred_element_type=F32)


def _norm_matmul_kernel(x_ref, g_ref, w_ref, o_ref, h_ref):
    @pl.when(pl.program_id(1) == 0)
    def _():
        x = x_ref[...]
        ms = jnp.mean(x * x, axis=-1, keepdims=True)
        h_ref[...] = (x * lax.rsqrt(ms + EPS) * g_ref[...]).astype(h_ref.dtype)

    o_ref[...] = _mm(h_ref[...], w_ref[...])


def norm_matmul(x, g, w, *, tm, tn, n_cols=None):
    m, k = x.shape
    n = w.shape[1] if n_cols is None else n_cols
    assert n % tn == 0
    return pl.pallas_call(
        _norm_matmul_kernel,
        grid=(m // tm, n // tn),
        in_specs=[pl.BlockSpec((tm, k), lambda i, j: (i, 0)),
                  pl.BlockSpec((1, k), lambda i, j: (0, 0)),
                  pl.BlockSpec((k, tn), lambda i, j: (0, j))],
        out_specs=pl.BlockSpec((tm, tn), lambda i, j: (i, j)),
        out_shape=jax.ShapeDtypeStruct((m, n), F32),
        scratch_shapes=[pltpu.VMEM((tm, k), BF16)],
        compiler_params=_cparams("parallel", "arbitrary"),
        name="norm_matmul",
    )(x, g.reshape(1, k), w)


def _matmul_res_kernel(x_ref, w_ref, r_ref, o_ref):
    o_ref[...] = r_ref[...] + _mm(x_ref[...], w_ref[...])


def _matmul_kernel(x_ref, w_ref, o_ref):
    o_ref[...] = _mm(x_ref[...], w_ref[...])


def matmul(x, w, res=None, *, tm, tn):
    m, k = x.shape
    n = w.shape[1]
    in_specs = [pl.BlockSpec((tm, k), lambda i, j: (i, 0)),
                pl.BlockSpec((k, tn), lambda i, j: (0, j))]
    args = [x, w]
    kern = _matmul_kernel
    if res is not None:
        in_specs.append(pl.BlockSpec((tm, tn), lambda i, j: (i, j)))
        args.append(res)
        kern = _matmul_res_kernel
    return pl.pallas_call(
        kern,
        grid=(m // tm, n // tn),
        in_specs=in_specs,
        out_specs=pl.BlockSpec((tm, tn), lambda i, j: (i, j)),
        out_shape=jax.ShapeDtypeStruct((m, n), F32),
        compiler_params=_cparams("parallel", "arbitrary"),
        name="matmul",
    )(*args)


HG_TB = 128
HG_C = 16
HG_H = HG_C // 2


def _hgrn_prompt_kernel(aq_ref, af_ref, ai_ref, ag_ref, lbl_ref, ng_ref, ya_ref, st_out_ref,
                        st_ref, q_s, k_s, g_s, v_s):
    t = pl.program_id(0)

    @pl.when(t == 0)
    def _():
        st_ref[...] = jnp.zeros_like(st_ref)

    lbl = lbl_ref[...]
    mx = jnp.max(lbl, axis=0, keepdims=True)
    ex = jnp.exp(lbl - mx)
    lb = ex[0:1, :] / jnp.sum(ex, axis=0, keepdims=True)

    f = lb + (1.0 - lb) * jax.nn.sigmoid(af_ref[...])
    logf = jnp.log(f)
    row = lax.broadcasted_iota(jnp.int32, (HG_TB, HG_TB), 0)
    col = lax.broadcasted_iota(jnp.int32, (HG_TB, HG_TB), 1)
    tri = jnp.where((row // HG_C == col // HG_C) & (col <= row), 1.0, 0.0).astype(F32)
    g_s[...] = jnp.dot(tri, logf, preferred_element_type=F32, precision=lax.Precision.HIGHEST)
    q_s[...] = _silu(aq_ref[...])
    k_s[...] = 1.0 - f
    v_s[...] = ai_ref[...]

    sub = lax.broadcasted_iota(jnp.int32, (HG_C, A_DK), 0)
    sub8 = lax.broadcasted_iota(jnp.int32, (HG_H, A_DK), 0)
    ng = ng_ref[...]

    def chunk(c, carry):
        r0 = pl.multiple_of(c * HG_C, HG_C)
        rows = pl.ds(r0, HG_C)
        for h in range(A_HEADS):
            cols = slice(h * A_DK, (h + 1) * A_DK)
            g = g_s[rows, cols]
            qh = q_s[rows, cols]
            kh = k_s[rows, cols]
            vh = v_s[rows, cols]
            halves = []
            for hb in range(2):
                rs = slice(hb * HG_H, (hb + 1) * HG_H)
                gb, qb, kb, vb = g[rs], qh[rs], kh[rs], vh[rs]
                ob = jnp.zeros((HG_H, A_DV), F32)
                for tt in range(HG_H):
                    d = gb[tt:tt + 1, :] - gb
                    e = jnp.exp(jnp.where(sub8 <= tt, d, -jnp.inf))
                    p = e * (qb[tt:tt + 1, :] * kb)
                    a_col = jnp.sum(p, axis=1, keepdims=True)
                    o_row = jnp.sum(a_col * vb, axis=0, keepdims=True)
                    ob = jnp.where(sub8 == tt, o_row, ob)
                halves.append(ob)
            o = jnp.concatenate(halves, axis=0)
            low = sub < HG_H
            g_mid = g[HG_H - 1:HG_H, :]
            q_hi = jnp.where(low, 0.0, qh * jnp.exp(jnp.minimum(g - g_mid, 0.0)))
            k_lo = jnp.where(low, kh * jnp.exp(jnp.minimum(g_mid - g, 0.0)), 0.0)
            st = st_ref[h]
            g_last = g[HG_C - 1:HG_C, :]
            kt = kh * jnp.exp(g_last - g)
            upd = lax.dot_general(vh.astype(BF16), jnp.concatenate([kt, k_lo], axis=1).astype(BF16),
                                  (((0,), (0,)), ((), ())), preferred_element_type=F32)
            lhs = jnp.concatenate([qh * jnp.exp(g), q_hi], axis=1).astype(BF16)
            rhs = jnp.concatenate([st, upd[:, A_DK:]], axis=1).astype(BF16)
            o = o + _nt(lhs, rhs)
            st_ref[h] = st * jnp.exp(g_last) + upd[:, :A_DK]
            on = o * lax.rsqrt(jnp.mean(o * o, axis=-1, keepdims=True) + EPS) * ng
            ya_ref[rows, cols] = (on * _silu(ag_ref[rows, cols])).astype(ya_ref.dtype)
        return carry

    lax.fori_loop(0, HG_TB // HG_C, chunk, 0)

    @pl.when(t == pl.num_programs(0) - 1)
    def _():
        st_out_ref[...] = st_ref[...]


def hgrn_prompt(z, lb_logits, norm_g):
    m = z.shape[0]
    w = A_HEADS * A_DK

    def zspec(cb):
        return pl.BlockSpec((HG_TB, w), lambda t, cb=cb: (t, cb))

    return pl.pallas_call(
        _hgrn_prompt_kernel,
        grid=(m // HG_TB,),
        in_specs=[zspec(C_AQ // w), zspec(C_AF // w), zspec(C_AI // w), zspec(C_AG // w),
                  pl.BlockSpec(lb_logits.shape, lambda t: (0, 0)),
                  pl.BlockSpec((1, A_DV), lambda t: (0, 0))],
        out_specs=[pl.BlockSpec((HG_TB, w), lambda t: (t, 0)),
                   pl.BlockSpec((A_HEADS, A_DV, A_DK), lambda t: (0, 0, 0))],
        out_shape=[jax.ShapeDtypeStruct((m, w), BF16),
                   jax.ShapeDtypeStruct((A_HEADS, A_DV, A_DK), F32)],
        scratch_shapes=[pltpu.VMEM((A_HEADS, A_DV, A_DK), F32)] + [pltpu.VMEM((HG_TB, w), F32)] * 4,
        compiler_params=_cparams("arbitrary"),
        name="hgrn_prompt",
    )(z, z, z, z, lb_logits, norm_g.reshape(1, A_DV))


def _hgrn_step_kernel(z_ref, lbl_ref, ng_ref, s_ref, ya_ref, s_out_ref):
    lbl = lbl_ref[...]
    mx = jnp.max(lbl, axis=0, keepdims=True)
    ex = jnp.exp(lbl - mx)
    lb = ex[0:1, :] / jnp.sum(ex, axis=0, keepdims=True)
    z = z_ref[0]
    w = A_HEADS * A_DK
    q = _silu(z[:, 0:w])
    f = lb + (1.0 - lb) * jax.nn.sigmoid(z[:, w:2 * w])
    kk = 1.0 - f
    v = z[:, 2 * w:3 * w]
    ag = z[:, 3 * w:4 * w]
    rows = []
    for h in range(A_HEADS):
        cols = slice(h * A_DK, (h + 1) * A_DK)
        rows += [f[:, cols], kk[:, cols], q[:, cols]]
    rows.append(jnp.zeros((LANES - 3 * A_HEADS, A_DK), F32))
    xt = jnp.concatenate(rows, axis=0).T
    ng = ng_ref[...]
    r16 = lambda a: a.astype(BF16).astype(F32)
    outs = []
    for h in range(A_HEADS):
        cols = slice(h * A_DV, (h + 1) * A_DV)
        fcol = xt[:, 3 * h:3 * h + 1]
        kcol = xt[:, 3 * h + 1:3 * h + 2]
        qcol = xt[:, 3 * h + 2:3 * h + 3]
        s_old = s_ref[0, h]
        s_out_ref[0, h] = fcol * s_old + kcol * v[:, cols]
        o = (jnp.sum(r16(qcol * fcol) * r16(s_old), axis=0, keepdims=True)
             + jnp.sum(qcol * kcol, axis=0, keepdims=True) * v[:, cols])
        on = o * lax.rsqrt(jnp.mean(o * o, axis=-1, keepdims=True) + EPS) * ng
        outs.append(on * _silu(ag[:, cols]))
    ya_ref[0] = jnp.concatenate(outs, axis=1).astype(ya_ref.dtype)


def hgrn_step(z4, lb_logits, norm_g, state):
    b = z4.shape[0]
    w = A_HEADS * A_DK
    return pl.pallas_call(
        _hgrn_step_kernel,
        grid=(b,),
        in_specs=[pl.BlockSpec((1, 1, 4 * w), lambda i: (i, 0, 0)),
                  pl.BlockSpec(lb_logits.shape, lambda i: (0, 0)),
                  pl.BlockSpec((1, A_DV), lambda i: (0, 0)),
                  pl.BlockSpec((1, A_HEADS, A_DK, A_DV), lambda i: (i, 0, 0, 0))],
        out_specs=[pl.BlockSpec((1, 1, w), lambda i: (i, 0, 0)),
                   pl.BlockSpec((1, A_HEADS, A_DK, A_DV), lambda i: (i, 0, 0, 0))],
        out_shape=[jax.ShapeDtypeStruct((b, 1, w), F32),
                   jax.ShapeDtypeStruct(state.shape, F32)],
        compiler_params=_cparams("arbitrary"),
        name="hgrn_step",
    )(z4.reshape(b, 1, 4 * w), lb_logits, norm_g.reshape(1, A_DV), state)


BISECT_MAX_ITERS = 48


def _bisect_threshold(count_ge, lo, hi, cnt_lo, topk):
    kf = float(topk)

    def cond(c):
        return jnp.logical_and(c[0] < BISECT_MAX_ITERS, c[-1] > 0.0)

    def body(c):
        it, lo, hi, cl, ch, _ = c
        mid = 0.5 * lo + 0.5 * hi
        t_int = hi - (hi - lo) * ((kf - ch) / jnp.maximum(cl - ch, 1.0))
        ok = jnp.logical_and(it % 2 == 0, jnp.logical_and(t_int > lo, t_int < hi))
        t = jnp.where(ok, t_int, mid)
        cm = count_ge(t)
        ge = cm >= kf
        lo = jnp.where(ge, t, lo)
        cl = jnp.where(ge, cm, cl)
        hi = jnp.where(ge, hi, t)
        ch = jnp.where(ge, ch, cm)
        busy = jnp.max(jnp.where(cl > kf, 1.0, 0.0))
        return it + 1, lo, hi, cl, ch, busy

    busy0 = jnp.max(jnp.where(cnt_lo > kf, 1.0, 0.0))
    out = lax.while_loop(cond, body, (jnp.int32(0), lo, hi, cnt_lo, jnp.ones_like(cnt_lo), busy0))
    return out[1]


DSA_QB = 128
DSA_W = 512
DSA_W3 = 1024


def _dsa_prompt_kernel(iq0_ref, iq1_ref, bq_ref, iw_ref, kidx_ref, k_ref, v_ref, bias_ref, o_ref,
                       score_s, qih_s, qs_s, wb_s, m_s, l_s, acc_s, *, topk):
    i = pl.program_id(0)
    nsub = DSA_W // LANES
    nsub3 = DSA_W3 // LANES
    nch3 = (i * DSA_QB + DSA_QB + DSA_W3 - 1) // DSA_W3
    nch = nch3 * (DSA_W3 // DSA_W)
    qpos = i * DSA_QB + lax.broadcasted_iota(jnp.int32, (DSA_QB, 1), 0)

    iw = iw_ref[...]
    wscale = IDX_DIM ** -0.5 * IDX_HEADS ** -0.5
    for h in range(IDX_HEADS):
        iq_ref, hh = (iq0_ref, h) if h < IDX_HEADS // 2 else (iq1_ref, h - IDX_HEADS // 2)
        qih_s[h] = iq_ref[:, hh * IDX_DIM:(hh + 1) * IDX_DIM].astype(BF16)
        wb_s[h] = jnp.broadcast_to(iw[:, h:h + 1] * wscale, (DSA_QB, LANES))
    for h in range(B_HEADS):
        qs_s[h // B_GROUP, (h % B_GROUP) * DSA_QB:(h % B_GROUP + 1) * DSA_QB, :] = (
            bq_ref[:, h * B_HEAD_DIM:(h + 1) * B_HEAD_DIM] * B_HEAD_DIM ** -0.5).astype(BF16)

    def p1(c, carry):
        c0 = pl.multiple_of(c * DSA_W, DSA_W)
        kc = kidx_ref[pl.ds(c0, DSA_W), :]
        sc = [jnp.zeros((DSA_QB, LANES), F32) for _ in range(nsub)]
        for h in range(IDX_HEADS):
            s = jnp.maximum(_nt(qih_s[h], kc), 0.0)
            wb = wb_s[h]
            for j in range(nsub):
                sc[j] = sc[j] + s[:, j * LANES:(j + 1) * LANES] * wb
        for j in range(nsub):
            kpos = c0 + j * LANES + lax.broadcasted_iota(jnp.int32, (1, LANES), 1)
            score_s[c * nsub + j] = jnp.where(kpos <= qpos, sc[j], -jnp.inf)
        return carry

    lax.fori_loop(0, nch, p1, 0)

    def stats(c, carry):
        mn, mx = carry
        for j in range(nsub):
            s = score_s[c * nsub + j]
            mx = jnp.maximum(mx, s)
            mn = jnp.minimum(mn, jnp.where(s > -jnp.inf, s, jnp.inf))
        return mn, mx

    mn, mx = lax.fori_loop(0, nch, stats, (jnp.full((DSA_QB, LANES), jnp.inf, F32),
                                           jnp.full((DSA_QB, LANES), -jnp.inf, F32)))
    lo0 = jnp.min(mn, axis=1, keepdims=True)
    hi0 = jnp.max(mx, axis=1, keepdims=True)

    def count_ge(thr):
        thr_b = jnp.broadcast_to(thr, (DSA_QB, LANES))

        def body(c, acc):
            for j in range(nsub):
                s = score_s[c * nsub + j]
                acc = acc + jnp.where(s >= thr_b, 1.0, 0.0)
            return acc

        acc = lax.fori_loop(0, nch, body, jnp.zeros((DSA_QB, LANES), F32))
        return jnp.sum(acc, axis=1, keepdims=True)

    thr = _bisect_threshold(count_ge, lo0, hi0, (qpos + 1).astype(F32), topk)
    thr_b = jnp.broadcast_to(thr, (DSA_QB, LANES))

    m_s[...] = jnp.full(m_s.shape, NEG, F32)
    l_s[...] = jnp.zeros(l_s.shape, F32)
    acc_s[...] = jnp.zeros(acc_s.shape, F32)

    def p3(c, with_bias):
        c0 = pl.multiple_of(c * DSA_W3, DSA_W3)
        madd = jnp.concatenate([jnp.where(score_s[c * nsub3 + j] >= thr_b, 0.0, NEG) for j in range(nsub3)], axis=1)
        kc = k_ref[pl.ds(c0, DSA_W3), :]
        vc = v_ref[pl.ds(c0, DSA_W3), :]
        rel = [i - (c * nsub3 + j) for j in range(nsub3)]

        def scores(n):
            return _nt(qs_s[n], kc[:, n * B_HEAD_DIM:(n + 1) * B_HEAD_DIM])

        def softmax(n, lg):
            lg = lg.reshape(B_GROUP, DSA_QB, DSA_W3) + madd[None]
            if with_bias:
                lg = lg + jnp.stack([jnp.concatenate(
                    [jnp.where(rel[j] == 0, bias_ref[n * B_GROUP + gq, 0],
                               jnp.where(rel[j] == 1, bias_ref[n * B_GROUP + gq, 1], 0.0)) for j in range(nsub3)],
                    axis=1) for gq in range(B_GROUP)])
            m_old = m_s[n]
            m_new = jnp.maximum(m_old, jnp.max(lg, axis=-1, keepdims=True))
            p = jnp.exp(lg - m_new)
            alpha = jnp.exp(m_old - m_new)
            l_s[n] = alpha * l_s[n] + jnp.sum(p, axis=-1, keepdims=True)
            m_s[n] = m_new
            pv = jnp.dot(p.reshape(B_GROUP * DSA_QB, DSA_W3).astype(BF16), vc[:, n * B_HEAD_DIM:(n + 1) * B_HEAD_DIM],
                         preferred_element_type=F32)
            return alpha, pv.reshape(B_GROUP, DSA_QB, B_HEAD_DIM)

        lgs = [scores(n) for n in range(B_KV_HEADS)]
        outs = [softmax(n, lgs[n]) for n in range(B_KV_HEADS)]
        for n in range(B_KV_HEADS):
            acc_s[n] = outs[n][0] * acc_s[n] + outs[n][1]

    n_far = jnp.maximum(i - 1, 0) // nsub3
    lax.fori_loop(0, n_far, lambda c, carry: (p3(c, False), carry)[1], 0)
    lax.fori_loop(n_far, nch3, lambda c, carry: (p3(c, True), carry)[1], 0)

    for h in range(B_HEADS):
        n, gq = h // B_GROUP, h % B_GROUP
        o_ref[:, h * B_HEAD_DIM:(h + 1) * B_HEAD_DIM] = (acc_s[n, gq] / l_s[n, gq]).astype(o_ref.dtype)


def dsa_prompt(z, ztail, kidx_bf, k_bf, v_bf, bias_tiles):
    m = z.shape[0]
    topk = min(TOPK_MAX, m // 4)
    wq = B_HEADS * B_HEAD_DIM
    wi2 = IDX_HEADS * IDX_DIM // 2
    kern = functools.partial(_dsa_prompt_kernel, topk=topk)
    return pl.pallas_call(
        kern,
        grid=(m // DSA_QB,),
        in_specs=[pl.BlockSpec((DSA_QB, wi2), lambda i: (i, C_IQ // wi2)),
                  pl.BlockSpec((DSA_QB, wi2), lambda i: (i, C_IQ // wi2 + 1)),
                  pl.BlockSpec((DSA_QB, wq), lambda i: (i, C_BQ // wq)),
                  pl.BlockSpec((DSA_QB, LANES), lambda i: (i, T_IW // LANES)),
                  pl.BlockSpec(kidx_bf.shape, lambda i: (0, 0)),
                  pl.BlockSpec(k_bf.shape, lambda i: (0, 0)),
                  pl.BlockSpec(v_bf.shape, lambda i: (0, 0)),
                  pl.BlockSpec(bias_tiles.shape, lambda i: (0, 0, 0, 0))],
        out_specs=pl.BlockSpec((DSA_QB, wq), lambda i: (i, 0)),
        out_shape=jax.ShapeDtypeStruct((m, wq), BF16),
        scratch_shapes=[pltpu.VMEM((m // LANES, DSA_QB, LANES), F32),
                        pltpu.VMEM((IDX_HEADS, DSA_QB, IDX_DIM), BF16),
                        pltpu.VMEM((B_KV_HEADS, B_GROUP * DSA_QB, B_HEAD_DIM), BF16),
                        pltpu.VMEM((IDX_HEADS, DSA_QB, LANES), F32),
                        pltpu.VMEM((B_KV_HEADS, B_GROUP, DSA_QB, 1), F32),
                        pltpu.VMEM((B_KV_HEADS, B_GROUP, DSA_QB, 1), F32),
                        pltpu.VMEM((B_KV_HEADS, B_GROUP, DSA_QB, B_HEAD_DIM), F32)],
        compiler_params=_cparams("arbitrary"),
        name="dsa_prompt",
    )(z, z, z, ztail, kidx_bf, k_bf, v_bf, bias_tiles)


def _page_copies(table_ref, b, n_pages, src_hbm, dst, sem, rows_per_page=PAGE_SIZE):
    def copy(p):
        return pltpu.make_async_copy(src_hbm.at[table_ref[b, p]],
                                     dst.at[pl.ds(p * rows_per_page, rows_per_page)], sem)
    return copy


def _dsa_scores_kernel(pt_ref, iq_ref, iw_ref, iknew_ref, kidx_hbm, o_ref, buf, sem, *, n_pages):
    b = pl.program_id(0)
    nb = pl.num_programs(0)
    past = n_pages * PAGE_SIZE

    def start(bb, slot):
        cp = _page_copies(pt_ref, bb, n_pages, kidx_hbm, buf.at[slot], sem.at[slot])
        lax.fori_loop(0, n_pages, lambda p, c: (cp(p).start(), c)[1], 0)

    def wait(bb, slot):
        cp = _page_copies(pt_ref, bb, n_pages, kidx_hbm, buf.at[slot], sem.at[slot])
        lax.fori_loop(0, n_pages, lambda p, c: (cp(p).wait(), c)[1], 0)

    slot = b % 2

    @pl.when(b == 0)
    def _():
        start(0, 0)

    @pl.when(b + 1 < nb)
    def _():
        start(b + 1, 1 - slot)

    wait(b, slot)

    r16 = lambda a: a.astype(BF16).astype(F32)
    qi = iq_ref[0].astype(BF16)
    wcol = r16(iw_ref[0]) * (IDX_DIM ** -0.5 * IDX_HEADS ** -0.5)
    s = r16(jnp.maximum(_nt(qi, buf[slot].astype(BF16)), 0.0))
    o_ref[0, :, 0:past] = jnp.sum(s * wcol, axis=0, keepdims=True)
    sn = r16(jnp.maximum(_nt(qi, iknew_ref[0].astype(BF16)), 0.0))
    sn = jnp.sum(sn * wcol, axis=0, keepdims=True)
    lane = lax.broadcasted_iota(jnp.int32, (1, LANES), 1)
    o_ref[0, :, past:past + LANES] = jnp.where(lane == 0, sn, -jnp.inf)


def dsa_scores(page_table, iq, iw, iknew_pad, cache_kidx):
    b, n_pages = page_table.shape
    past = n_pages * PAGE_SIZE
    kern = functools.partial(_dsa_scores_kernel, n_pages=n_pages)
    gs = pltpu.PrefetchScalarGridSpec(
        num_scalar_prefetch=1,
        grid=(b,),
        in_specs=[pl.BlockSpec((1, IDX_HEADS, IDX_DIM), lambda i, pt: (i, 0, 0)),
                  pl.BlockSpec((1, IDX_HEADS, 1), lambda i, pt: (i, 0, 0)),
                  pl.BlockSpec((1, LANES, IDX_DIM), lambda i, pt: (i, 0, 0)),
                  pl.BlockSpec(memory_space=pl.ANY)],
        out_specs=pl.BlockSpec((1, 1, past + LANES), lambda i, pt: (i, 0, 0)),
        scratch_shapes=[pltpu.VMEM((2, past, IDX_DIM), F32), pltpu.SemaphoreType.DMA((2,))],
    )
    return pl.pallas_call(
        kern, grid_spec=gs,
        out_shape=jax.ShapeDtypeStruct((b, 1, past + LANES), F32),
        compiler_params=_cparams("arbitrary"),
        name="dsa_scores",
    )(page_table, iq, iw, iknew_pad, cache_kidx)


def _dsa_threshold_kernel(s_ref, thr_ref, *, topk):
    s = s_ref[...]
    nb = s.shape[0]
    finite = s > -jnp.inf
    lo0 = jnp.min(jnp.where(finite, s, jnp.inf), axis=1, keepdims=True)
    hi0 = jnp.max(s, axis=1, keepdims=True)
    cnt0 = jnp.sum(jnp.where(finite, 1.0, 0.0), axis=1, keepdims=True)

    def count_ge(thr):
        return jnp.sum(jnp.where(s_ref[...] >= thr, 1.0, 0.0), axis=1, keepdims=True)

    thr = _bisect_threshold(count_ge, lo0, hi0, cnt0, topk)
    thr_ref[...] = jnp.broadcast_to(thr, (nb, LANES))


def dsa_threshold(scores, topk):
    b, l = scores.shape
    return pl.pallas_call(
        functools.partial(_dsa_threshold_kernel, topk=topk),
        grid=(1,),
        in_specs=[pl.BlockSpec((b, l), lambda i: (0, 0))],
        out_specs=pl.BlockSpec((b, LANES), lambda i: (0, 0)),
        out_shape=jax.ShapeDtypeStruct((b, LANES), F32),
        compiler_params=_cparams("arbitrary"),
        name="dsa_threshold",
    )(scores)


def _dsa_decode_kernel(pt_ref, q_ref, s_ref, thr_ref, knew_ref, vnew_ref, bias_ref, k_hbm, v_hbm, o_ref,
                       kbuf, vbuf, sem, *, n_pages):
    b = pl.program_id(0)
    nb = pl.num_programs(0)
    past = n_pages * PAGE_SIZE

    rpp = PAGE_SIZE * B_KV_HEADS

    def copies(bb, slot):
        ck = _page_copies(pt_ref, bb, n_pages, k_hbm, kbuf.at[slot], sem.at[0, slot], rpp)
        cv = _page_copies(pt_ref, bb, n_pages, v_hbm, vbuf.at[slot], sem.at[1, slot], rpp)
        return ck, cv

    def start(bb, slot):
        ck, cv = copies(bb, slot)
        lax.fori_loop(0, n_pages, lambda p, c: (ck(p).start(), cv(p).start(), c)[2], 0)

    def wait(bb, slot):
        ck, cv = copies(bb, slot)
        lax.fori_loop(0, n_pages, lambda p, c: (ck(p).wait(), cv(p).wait(), c)[2], 0)

    slot = b % 2

    @pl.when(b == 0)
    def _():
        kbuf[:, n_pages * rpp:, :] = jnp.zeros((2, rpp, B_HEAD_DIM), F32)
        vbuf[:, n_pages * rpp:, :] = jnp.zeros((2, rpp, B_HEAD_DIM), F32)
        start(0, 0)

    @pl.when(b + 1 < nb)
    def _():
        start(b + 1, 1 - slot)

    kbuf[slot, n_pages * rpp:n_pages * rpp + 8, :] = knew_ref[0]
    vbuf[slot, n_pages * rpp:n_pages * rpp + 8, :] = vnew_ref[0]
    wait(b, slot)

    sel = s_ref[0] >= thr_ref[0][:, 0:1]
    n_keys = past + PAGE_SIZE
    outs = []
    for n in range(B_KV_HEADS):
        kn = kbuf[slot, pl.ds(n, n_keys, stride=B_KV_HEADS), :].astype(BF16)
        vn = vbuf[slot, pl.ds(n, n_keys, stride=B_KV_HEADS), :].astype(BF16)
        qn = q_ref[0, n].astype(BF16)
        lg = _nt(qn, kn) * B_HEAD_DIM ** -0.5 + bias_ref[n]
        m = jnp.max(jnp.where(sel, lg, NEG), axis=1, keepdims=True)
        p = jnp.where(sel, jnp.exp(lg - m), 0.0)
        p = p / jnp.sum(p, axis=1, keepdims=True)
        outs.append(jnp.dot(p.astype(BF16), vn, preferred_element_type=F32))
    o_ref[0] = jnp.concatenate(outs, axis=0).astype(o_ref.dtype)


def dsa_decode(page_table, q8, scores, thr, knew8, vnew8, bias_rows, cache_k2, cache_v2):
    b, n_pages = page_table.shape
    past = n_pages * PAGE_SIZE
    l = past + LANES
    wkv = B_KV_HEADS * B_HEAD_DIM
    kern = functools.partial(_dsa_decode_kernel, n_pages=n_pages)
    gs = pltpu.PrefetchScalarGridSpec(
        num_scalar_prefetch=1,
        grid=(b,),
        in_specs=[pl.BlockSpec((1, B_KV_HEADS, 8, B_HEAD_DIM), lambda i, pt: (i, 0, 0, 0)),
                  pl.BlockSpec((1, 1, l), lambda i, pt: (i, 0, 0)),
                  pl.BlockSpec((1, 1, LANES), lambda i, pt: (i, 0, 0)),
                  pl.BlockSpec((1, 8, B_HEAD_DIM), lambda i, pt: (i, 0, 0)),
                  pl.BlockSpec((1, 8, B_HEAD_DIM), lambda i, pt: (i, 0, 0)),
                  pl.BlockSpec((B_KV_HEADS, 8, l), lambda i, pt: (0, 0, 0)),
                  pl.BlockSpec(memory_space=pl.ANY),
                  pl.BlockSpec(memory_space=pl.ANY)],
        out_specs=pl.BlockSpec((1, 2 * 8, B_HEAD_DIM), lambda i, pt: (i, 0, 0)),
        scratch_shapes=[pltpu.VMEM((2, l * B_KV_HEADS, B_HEAD_DIM), F32),
                        pltpu.VMEM((2, l * B_KV_HEADS, B_HEAD_DIM), F32),
                        pltpu.SemaphoreType.DMA((2, 2))],
    )
    return pl.pallas_call(
        kern, grid_spec=gs,
        out_shape=jax.ShapeDtypeStruct((b, 2 * 8, B_HEAD_DIM), F32),
        compiler_params=_cparams("arbitrary"),
        name="dsa_decode",
    )(page_table, q8, scores, thr, knew8, vnew8, bias_rows, cache_k2, cache_v2)


def _merge_kernel(ya_ref, yb_ref, ga_ref, gb_ref, wa_ref, wb_ref, o_ref):
    a = _mm(ya_ref[...], wa_ref[...])
    bb = _mm(yb_ref[...], wb_ref[...])
    o_ref[...] = (jax.nn.sigmoid(ga_ref[...]) * a + jax.nn.sigmoid(gb_ref[...]) * bb).astype(o_ref.dtype)


def merge(ya, yb, z, ga_col, gb_col, wa, wb, *, tm, tn):
    m, k = ya.shape
    n = wa.shape[1]
    return pl.pallas_call(
        _merge_kernel,
        grid=(m // tm, n // tn),
        in_specs=[pl.BlockSpec((tm, k), lambda i, j: (i, 0)),
                  pl.BlockSpec((tm, k), lambda i, j: (i, 0)),
                  pl.BlockSpec((tm, tn), lambda i, j: (i, ga_col // tn + j)),
                  pl.BlockSpec((tm, tn), lambda i, j: (i, gb_col // tn + j)),
                  pl.BlockSpec((k, tn), lambda i, j: (0, j)),
                  pl.BlockSpec((k, tn), lambda i, j: (0, j))],
        out_specs=pl.BlockSpec((tm, tn), lambda i, j: (i, j)),
        out_shape=jax.ShapeDtypeStruct((m, n), wa.dtype),
        compiler_params=_cparams("parallel", "arbitrary"),
        name="merge",
    )(ya, yb, z, z, wa, wb)


def _cross_prompt_kernel(x_ref, g_ref, wq_ref, mk_ref, mv_ref, wo_ref, o_ref):
    x = x_ref[...]
    ms = jnp.mean(x * x, axis=-1, keepdims=True)
    h = (x * lax.rsqrt(ms + EPS) * g_ref[...]).astype(BF16)
    q = jnp.dot(h, wq_ref[...], preferred_element_type=F32)
    outs = []
    for hh in range(X_HEADS):
        cols = slice(hh * X_HEAD_DIM, (hh + 1) * X_HEAD_DIM)
        lg = _nt(q[:, cols].astype(BF16), mk_ref[:, cols]) * X_HEAD_DIM ** -0.5
        mx = jnp.max(lg, axis=1, keepdims=True)
        p = jnp.exp(lg - mx)
        p = p / jnp.sum(p, axis=1, keepdims=True)
        outs.append(jnp.dot(p.astype(BF16), mv_ref[:, cols], preferred_element_type=F32).astype(BF16))
    att = jnp.concatenate(outs, axis=1)
    o_ref[...] = x + jnp.dot(att, wo_ref[...], preferred_element_type=F32)


def cross_prompt(x, g, wq, mk, mv, wo, *, tm):
    m, d = x.shape
    full = lambda a: pl.BlockSpec(a.shape, lambda i: (0,) * a.ndim)
    g2 = g.reshape(1, d)
    return pl.pallas_call(
        _cross_prompt_kernel,
        grid=(m // tm,),
        in_specs=[pl.BlockSpec((tm, d), lambda i: (i, 0)), full(g2), full(wq), full(mk), full(mv), full(wo)],
        out_specs=pl.BlockSpec((tm, d), lambda i: (i, 0)),
        out_shape=jax.ShapeDtypeStruct((m, d), F32),
        compiler_params=_cparams("parallel"),
        name="cross_prompt",
    )(x, g2, wq, mk, mv, wo)


def _cross_step_kernel(q_ref, mk_ref, mv_ref, o_ref):
    r16 = lambda a: a.astype(BF16).astype(F32)
    q = r16(q_ref[0])
    outs = []
    for hh in range(X_HEADS):
        cols = slice(hh * X_HEAD_DIM, (hh + 1) * X_HEAD_DIM)
        kh = r16(mk_ref[0, :, cols])
        vh = r16(mv_ref[0, :, cols])
        lg = jnp.sum(kh * q[:, cols], axis=1, keepdims=True) * X_HEAD_DIM ** -0.5
        mx = jnp.max(lg, axis=0, keepdims=True)
        p = jnp.exp(lg - mx)
        p = r16(p / jnp.sum(p, axis=0, keepdims=True))
        outs.append(jnp.sum(p * vh, axis=0, keepdims=True))
    o_ref[0] = jnp.concatenate(outs, axis=1).astype(o_ref.dtype)


def cross_step(q, mk, mv):
    b, w = q.shape
    mem = mk.shape[1]
    return pl.pallas_call(
        _cross_step_kernel,
        grid=(b,),
        in_specs=[pl.BlockSpec((1, 1, w), lambda i: (i, 0, 0)),
                  pl.BlockSpec((1, mem, w), lambda i: (i, 0, 0)),
                  pl.BlockSpec((1, mem, w), lambda i: (i, 0, 0))],
        out_specs=pl.BlockSpec((1, 1, w), lambda i: (i, 0, 0)),
        out_shape=jax.ShapeDtypeStruct((b, 1, w), F32),
        compiler_params=_cparams("arbitrary"),
        name="cross_step",
    )(q.reshape(b, 1, w), mk, mv)


def _router_kernel(x_ref, g_ref, w_ref, b_ref, hf_ref, route_ref):
    x = x_ref[...]
    ms = jnp.mean(x * x, axis=-1, keepdims=True)
    hf = x * lax.rsqrt(ms + EPS) * g_ref[...]
    hf_ref[...] = hf
    lg = _mm(hf, w_ref[...]) + b_ref[...]
    tm = lg.shape[0]
    lane = lax.broadcasted_iota(jnp.int32, (tm, LANES), 1)
    big = jnp.int32(LANES)
    is_g = lane < N_GROUPS
    gmax = jnp.max(jnp.where(is_g, lg, -jnp.inf), axis=1, keepdims=True)
    grp = jnp.min(jnp.where(is_g & (lg == gmax), lane, big), axis=1, keepdims=True)
    p_grp = 1.0 / jnp.sum(jnp.where(is_g, jnp.exp(lg - gmax), 0.0), axis=1, keepdims=True)
    e_lo = N_GROUPS + grp * EXP_PER_GROUP
    in_g = (lane >= e_lo) & (lane < e_lo + EXP_PER_GROUP)
    v1 = jnp.max(jnp.where(in_g, lg, -jnp.inf), axis=1, keepdims=True)
    i1 = jnp.min(jnp.where(in_g & (lg == v1), lane, big), axis=1, keepdims=True)
    rest = in_g & (lane != i1)
    v2 = jnp.max(jnp.where(rest, lg, -jnp.inf), axis=1, keepdims=True)
    i2 = jnp.min(jnp.where(rest & (lg == v2), lane, big), axis=1, keepdims=True)
    e2 = jnp.exp(v2 - v1)
    g1 = p_grp / (1.0 + e2)
    g2 = p_grp * e2 / (1.0 + e2)
    r = jnp.where(lane == 0, (i1 - N_GROUPS).astype(F32),
                  jnp.where(lane == 1, (i2 - N_GROUPS).astype(F32),
                            jnp.where(lane == 2, g1, jnp.where(lane == 3, g2, 0.0))))
    route_ref[...] = r


def router(x, g, w_pad, b_pad, *, tm):
    m, d = x.shape
    return pl.pallas_call(
        _router_kernel,
        grid=(pl.cdiv(m, tm),),
        in_specs=[pl.BlockSpec((tm, d), lambda i: (i, 0)),
                  pl.BlockSpec((1, d), lambda i: (0, 0)),
                  pl.BlockSpec((d, LANES), lambda i: (0, 0)),
                  pl.BlockSpec((1, LANES), lambda i: (0, 0))],
        out_specs=[pl.BlockSpec((tm, d), lambda i: (i, 0)),
                   pl.BlockSpec((tm, LANES), lambda i: (i, 0))],
        out_shape=[jax.ShapeDtypeStruct((m, d), F32), jax.ShapeDtypeStruct((m, LANES), F32)],
        compiler_params=_cparams("parallel"),
        name="router",
    )(x, g.reshape(1, d), w_pad, b_pad)


def _moe_kernel(be_ref, nxt_ref, tok_ref, row_ref, hf_hbm, wg_hbm, wu_hbm, wd_hbm, y_hbm, xbuf, obuf,
                wg_f, wu_f, wd_f, wg_s, wu_s, wd_s, sem_in, sem_out, sem_w):
    b = pl.program_id(0)
    nb = pl.num_programs(0)
    blk = xbuf.shape[1]
    slot = b % 2

    def weight_copies(e):
        return (pltpu.make_async_copy(wg_hbm.at[e], wg_f, sem_w.at[0]),
                pltpu.make_async_copy(wu_hbm.at[e], wu_f, sem_w.at[1]),
                pltpu.make_async_copy(wd_hbm.at[e], wd_f, sem_w.at[2]))

    def gather_start(bb, sl):
        for r in range(blk):
            pltpu.make_async_copy(hf_hbm.at[pl.ds(tok_ref[bb * blk + r], 1)], xbuf.at[sl, pl.ds(r, 1)],
                                  sem_in.at[sl]).start()

    def scatter_start(bb, sl):
        for r in range(blk):
            pltpu.make_async_copy(obuf.at[sl, pl.ds(r, 1)], y_hbm.at[pl.ds(row_ref[bb * blk + r], 1)],
                                  sem_out.at[sl]).start()

    def gather_wait(sl):
        pltpu.make_async_copy(hf_hbm.at[pl.ds(0, blk)], xbuf.at[sl], sem_in.at[sl]).wait()

    def scatter_wait(sl):
        pltpu.make_async_copy(obuf.at[sl], y_hbm.at[pl.ds(0, blk)], sem_out.at[sl]).wait()

    @pl.when(b == 0)
    def _():
        for cp in weight_copies(be_ref[0]):
            cp.start()
        gather_start(0, 0)

    changed = jnp.logical_or(b == 0, be_ref[b] != be_ref[jnp.maximum(b - 1, 0)])

    @pl.when(changed)
    def _():
        for cp in weight_copies(be_ref[b]):
            cp.wait()
        wg_s[...] = wg_f[...].astype(BF16)
        wu_s[...] = wu_f[...].astype(BF16)
        wd_s[...] = wd_f[...].astype(BF16)

    @pl.when(jnp.logical_and(changed, nxt_ref[b] >= 0))
    def _():
        for cp in weight_copies(nxt_ref[b]):
            cp.start()

    gather_wait(slot)

    @pl.when(b >= 2)
    def _():
        scatter_wait(slot)

    def step(prefetch, flush_prev):
        if prefetch:
            gather_start(b + 1, 1 - slot)
        if flush_prev:
            scatter_start(b - 1, 1 - slot)
        x = xbuf[slot].astype(BF16)
        gg = jnp.dot(x, wg_s[...], preferred_element_type=F32)
        uu = jnp.dot(x, wu_s[...], preferred_element_type=F32)
        a = (_silu(gg) * uu).astype(BF16)
        obuf[slot] = jnp.dot(a, wd_s[...], preferred_element_type=F32)

    first, last = b == 0, b == nb - 1
    pl.when(jnp.logical_and(first, jnp.logical_not(last)))(lambda: step(True, False))
    pl.when(jnp.logical_and(jnp.logical_not(first), jnp.logical_not(last)))(lambda: step(True, True))
    pl.when(jnp.logical_and(jnp.logical_not(first), last))(lambda: step(False, True))
    pl.when(jnp.logical_and(first, last))(lambda: step(False, False))

    @pl.when(last)
    def _():
        scatter_start(b, slot)
        scatter_wait(slot)

    @pl.when(jnp.logical_and(last, jnp.logical_not(first)))
    def _():
        scatter_wait(1 - slot)


def moe_experts(block_e, next_e, slot_tok, slot_row, hf, w_g, w_u, w_d, *, blk, out_rows):
    n_blocks = block_e.shape[0]
    d = hf.shape[1]
    ff = w_g.shape[2]
    gs = pltpu.PrefetchScalarGridSpec(
        num_scalar_prefetch=4,
        grid=(n_blocks,),
        in_specs=[pl.BlockSpec(memory_space=pl.ANY)] * 4,
        out_specs=pl.BlockSpec(memory_space=pl.ANY),
        scratch_shapes=[pltpu.VMEM((2, blk, d), F32), pltpu.VMEM((2, blk, d), F32),
                        pltpu.VMEM((d, ff), F32), pltpu.VMEM((d, ff), F32), pltpu.VMEM((ff, d), F32),
                        pltpu.VMEM((d, ff), BF16), pltpu.VMEM((d, ff), BF16), pltpu.VMEM((ff, d), BF16),
                        pltpu.SemaphoreType.DMA((2,)), pltpu.SemaphoreType.DMA((2,)),
                        pltpu.SemaphoreType.DMA((3,))],
    )
    return pl.pallas_call(
        _moe_kernel, grid_spec=gs,
        out_shape=jax.ShapeDtypeStruct((out_rows, d), F32),
        compiler_params=_cparams("arbitrary"),
        name="moe_experts",
    )(block_e, next_e, slot_tok, slot_row, hf, w_g, w_u, w_d)


def _combine_kernel(x_ref, route_ref, gf_ref, y1_ref, y2_ref, op_ref, os_ref):
    i = pl.program_id(0)
    route = route_ref[...]
    x = x_ref[...] + route[:, 2:3] * y1_ref[...] + route[:, 3:4] * y2_ref[...]
    ms = jnp.mean(x * x, axis=-1, keepdims=True)
    out = x * lax.rsqrt(ms + EPS) * gf_ref[...]

    @pl.when(i < pl.num_programs(0) - 1)
    def _():
        op_ref[...] = out

    @pl.when(i == pl.num_programs(0) - 1)
    def _():
        os_ref[...] = out[:os_ref.shape[0]]


def combine(x, route, gf, y, plane, n_prompt, *, tm):
    m, d = x.shape
    n_tiles = n_prompt // tm
    assert n_prompt % tm == 0 and 0 < m - n_prompt <= tm
    return pl.pallas_call(
        _combine_kernel,
        grid=(n_tiles + 1,),
        in_specs=[pl.BlockSpec((tm, d), lambda i: (i, 0)),
                  pl.BlockSpec((tm, LANES), lambda i: (i, 0)),
                  pl.BlockSpec((1, d), lambda i: (0, 0)),
                  pl.BlockSpec((tm, d), lambda i: (i, 0)),
                  pl.BlockSpec((tm, d), lambda i: (plane // tm + i, 0))],
        out_specs=[pl.BlockSpec((tm, d), lambda i: (jnp.minimum(i, n_tiles - 1), 0)),
                   pl.BlockSpec((m - n_prompt, d), lambda i: (0, 0))],
        out_shape=[jax.ShapeDtypeStruct((n_prompt, d), F32), jax.ShapeDtypeStruct((m - n_prompt, d), F32)],
        compiler_params=_cparams("arbitrary"),
        name="combine",
    )(x, route, gf.reshape(1, d), y, y)


def _t5_bucket(dist):
    dist = jnp.asarray(dist, jnp.int32)
    max_exact = REL_BUCKETS // 2
    dist_f = jnp.maximum(dist, 1).astype(F32)
    large = max_exact + (jnp.log(dist_f / max_exact) / math.log(REL_MAX_DIST / max_exact)
                         * (REL_BUCKETS - max_exact)).astype(jnp.int32)
    large = jnp.minimum(large, REL_BUCKETS - 1)
    return jnp.where(dist < max_exact, dist, large)


def _bias_tables(rel_bias, past):
    r = np.arange(LANES)
    diff = r[:, None] - r[None, :]
    buckets = jnp.stack([_t5_bucket(np.maximum(diff, 0)),
                         _t5_bucket(np.maximum(diff + LANES, 0)),
                         _t5_bucket(np.full((LANES, LANES), 2 * LANES))])
    def lookup(bkt):
        oh = (bkt.reshape(-1, 1) == jnp.arange(REL_BUCKETS)[None, :]).astype(F32)
        out = jnp.dot(oh, rel_bias.astype(F32), precision=lax.Precision.HIGHEST)
        return out.T.reshape((rel_bias.shape[1],) + bkt.shape)

    tiles = lookup(buckets)
    tiles = tiles - tiles[:, 2:3]
    dist = np.maximum(past - np.arange(past + LANES), 0)
    rows = lookup(_t5_bucket(dist))
    rows = rows.reshape(B_KV_HEADS, B_GROUP, past + LANES)
    rows = jnp.concatenate([rows, jnp.zeros_like(rows)], axis=1)
    return tiles, rows


def _dispatch(eid, n_tokens, plane, blk):
    a = eid.shape[0]
    assert a == EXPERT_TOPK * n_tokens
    n_blocks = -(-(a + N_EXPERTS * (blk - 1)) // blk)
    rows = n_blocks * blk
    gap = plane - n_tokens
    assert rows - a >= EXPERT_TOPK * gap
    ar = jnp.arange(a, dtype=jnp.int32)
    out_row = (ar % EXPERT_TOPK) * plane + ar // EXPERT_TOPK
    order = jnp.argsort(eid)
    e_sorted = eid[order]
    counts = jnp.bincount(eid, length=N_EXPERTS)
    starts = jnp.cumsum(counts) - counts
    padded = (counts + blk - 1) // blk * blk
    pad_end = jnp.cumsum(padded)
    pad_start = pad_end - padded
    dest_sorted = (pad_start[e_sorted] + ar - starts[e_sorted]).astype(jnp.int32)
    block_e = jnp.minimum(jnp.sum(pad_end[None, :] <= (jnp.arange(n_blocks) * blk)[:, None], axis=1), N_EXPERTS - 1)
    e_slot = jnp.repeat(block_e, blk)
    sl = jnp.arange(rows, dtype=jnp.int32)
    is_real = sl - pad_start[e_slot] < counts[e_slot]
    j = sl - jnp.cumsum(counts)[e_slot]
    pad_row = jnp.where(j < gap, n_tokens + j,
                        jnp.where(j < 2 * gap, plane + n_tokens + j - gap, EXPERT_TOPK * plane + j - 2 * gap))
    slot_row = pad_row.astype(jnp.int32).at[dest_sorted].set(out_row[order])
    slot_tok = jnp.where(is_real, slot_row % plane, 0)
    blocks = jnp.arange(n_blocks)
    run_start = jnp.where(jnp.concatenate([jnp.array([True]), block_e[1:] != block_e[:-1]]), blocks, n_blocks)
    nxt = jnp.concatenate([jnp.flip(lax.cummin(jnp.flip(run_start)))[1:], jnp.array([n_blocks])])
    next_e = jnp.where(nxt < n_blocks, block_e[jnp.minimum(nxt, n_blocks - 1)], -1)
    return (slot_tok.astype(jnp.int32), slot_row, block_e.astype(jnp.int32), next_e.astype(jnp.int32), rows)


def kernel(x_prompt, x_sample, mem_prompt, cache_k, cache_v, cache_kidx, page_table, state_hgrn, cache_mem_k,
           cache_mem_v, norm_mix, w_in, hgrn_lb_logits, hgrn_norm, w_branch_a, w_branch_b, w_out, norm_cross, w_xq,
           w_xk, w_xv, w_xo, norm_ffn, w_router_group, b_router_group, w_router_expert, b_router_expert, w_exp_gate,
           w_exp_up, w_exp_down, rel_bias, norm_final):
    assert w_in.shape[0] == 1, "single-layer step"
    l = 0
    drop0 = lambda a: a.reshape(a.shape[1:])
    bp, t, d = x_prompt.shape
    db = x_sample.shape[0]
    past = page_table.shape[1] * PAGE_SIZE
    xp = x_prompt.reshape(bp * t, d)
    xs = x_sample.reshape(db, d)

    wi = drop0(w_in)
    o = np.cumsum((0,) + (1024, 1024, 1024, 1024, 1024, 256, 256, 1024, 16, 64, 2048, 2048))
    assert o[8] == NZ_MAIN
    seg = lambda i: wi[:, o[i]:o[i + 1]]
    w_tail = jnp.pad(jnp.concatenate([seg(10), seg(11), seg(8), seg(9)], axis=1),
                     ((0, 0), (0, NZ_TAIL - (o[12] - o[8])))).astype(BF16)
    wa, wb, wo = w_branch_a[l].astype(BF16), w_branch_b[l].astype(BF16), w_out[l].astype(BF16)
    wxq, wxk, wxv, wxo = (w_xq[l].astype(BF16), w_xk[l].astype(BF16), w_xv[l].astype(BF16), w_xo[l].astype(BF16))
    w_route = jnp.pad(jnp.concatenate([w_router_group[l], w_router_expert[l]], axis=1),
                      ((0, 0), (0, LANES - N_GROUPS - N_EXPERTS))).astype(BF16)
    b_route = jnp.pad(jnp.concatenate([b_router_group[l], b_router_expert[l]]),
                      (0, LANES - N_GROUPS - N_EXPERTS)).reshape(1, LANES)
    bias_tiles, bias_rows = _bias_tables(rel_bias, past)

    zp = norm_matmul(xp, norm_mix[l], wi, tm=1024, tn=512, n_cols=NZ_MAIN)
    zpt = norm_matmul(xp, norm_mix[l], w_tail, tm=1024, tn=512)
    kp = zp[:, C_BK:C_BK + 256]
    vp = zp[:, C_BV:C_BV + 256]
    ikp = zpt[:, T_IW + IDX_HEADS:T_IW + IDX_HEADS + IDX_DIM]
    ya_p, st_p = hgrn_prompt(zp, hgrn_lb_logits, hgrn_norm[l])
    yb_p = dsa_prompt(zp, zpt, ikp.astype(BF16), kp.astype(BF16), vp.astype(BF16), bias_tiles)
    mg_p = merge(ya_p, yb_p, zpt, T_GA, T_GB, wa, wb, tm=512, tn=512)
    x1p = matmul(mg_p, wo, xp, tm=512, tn=512)
    memp = mem_prompt.reshape(-1, d)
    mk = matmul(memp, wxk, tm=memp.shape[0], tn=512)
    mv = matmul(memp, wxv, tm=memp.shape[0], tn=512)
    x2p = cross_prompt(x1p, norm_cross[l], wxq, mk.astype(BF16), mv.astype(BF16), wxo, tm=512)

    zs = norm_matmul(xs, norm_mix[l], wi, tm=db, tn=512, n_cols=NZ_MAIN)
    zst = norm_matmul(xs, norm_mix[l], w_tail, tm=db, tn=512)
    ks = zs[:, C_BK:C_BK + 256]
    vs = zs[:, C_BV:C_BV + 256]
    iks = zst[:, T_IW + IDX_HEADS:T_IW + IDX_HEADS + IDX_DIM]
    ya_s, st_s = hgrn_step(zs[:, :4 * A_HEADS * A_DK], hgrn_lb_logits, hgrn_norm[l], drop0(state_hgrn))
    iq_s = zs[:, C_IQ:C_IQ + IDX_HEADS * IDX_DIM].reshape(db, IDX_HEADS, IDX_DIM)
    iw_s = zst[:, T_IW:T_IW + IDX_HEADS].reshape(db, IDX_HEADS, 1)
    iknew_pad = jnp.pad(iks[:, None, :], ((0, 0), (0, LANES - 1), (0, 0)))
    scores = dsa_scores(page_table, iq_s, iw_s, iknew_pad, drop0(cache_kidx)).reshape(db, past + LANES)
    topk_s = min(TOPK_MAX, (past + 1) // 4)
    thr = dsa_threshold(scores, topk_s)
    q8 = jnp.pad(zs[:, C_BQ:C_BQ + B_HEADS * B_HEAD_DIM].reshape(db, B_KV_HEADS, B_GROUP, B_HEAD_DIM),
                 ((0, 0), (0, 0), (0, 8 - B_GROUP), (0, 0)))
    knew8 = jnp.pad(ks.reshape(db, B_KV_HEADS, B_HEAD_DIM), ((0, 0), (0, 8 - B_KV_HEADS), (0, 0)))
    vnew8 = jnp.pad(vs.reshape(db, B_KV_HEADS, B_HEAD_DIM), ((0, 0), (0, 8 - B_KV_HEADS), (0, 0)))
    n_pool = cache_k.shape[1]
    ob = dsa_decode(page_table, q8, scores.reshape(db, 1, -1), thr.reshape(db, 1, LANES), knew8, vnew8, bias_rows,
                    cache_k.reshape(n_pool, PAGE_SIZE * B_KV_HEADS, B_HEAD_DIM),
                    cache_v.reshape(n_pool, PAGE_SIZE * B_KV_HEADS, B_HEAD_DIM))
    yb_s = ob.reshape(db, B_KV_HEADS, 8, B_HEAD_DIM)[:, :, :B_GROUP].reshape(db, B_HEADS * B_HEAD_DIM)
    mg_s = merge(ya_s.reshape(db, -1), yb_s, zst, T_GA, T_GB, wa, wb, tm=db, tn=512)
    x1s = matmul(mg_s, wo, xs, tm=db, tn=512)
    qx_s = norm_matmul(x1s, norm_cross[l], wxq, tm=db, tn=512)
    mem = cache_mem_k.shape[2]
    att_s = cross_step(qx_s, cache_mem_k.reshape(db, mem, -1), cache_mem_v.reshape(db, mem, -1))
    x2s = matmul(att_s.reshape(db, -1), wxo, x1s, tm=db, tn=512)

    x2 = jnp.concatenate([x2p, x2s], axis=0)
    n = x2.shape[0]
    hf, route = router(x2, norm_ffn[l], w_route, b_route, tm=256)
    eid = route[:, :EXPERT_TOPK].astype(jnp.int32).reshape(-1)
    a = n * EXPERT_TOPK
    blk = min(MOE_BLOCK, max(8, a // N_EXPERTS))
    tmc = 128
    plane = -(-n // tmc) * tmc
    slot_tok, slot_row, block_e, next_e, rows = _dispatch(eid, n, plane, blk)
    ye = moe_experts(block_e, next_e, slot_tok, slot_row, hf, drop0(w_exp_gate), drop0(w_exp_up),
                     drop0(w_exp_down), blk=blk, out_rows=rows)
    y_p, y_s = combine(x2, route, norm_final, ye, plane, bp * t, tm=tmc)

    y_prompt = y_p.reshape(bp, t, d)
    y_sample = y_s.reshape(db, 1, d)
    return (y_prompt, y_sample,
            kp.reshape(1, bp, t, B_KV_HEADS, B_HEAD_DIM), vp.reshape(1, bp, t, B_KV_HEADS, B_HEAD_DIM),
            ikp.reshape(1, bp, t, IDX_DIM),
            jnp.swapaxes(st_p, 1, 2).reshape(1, bp, A_HEADS, A_DK, A_DV),
            mk.reshape(1, bp, -1, X_HEADS, X_HEAD_DIM), mv.reshape(1, bp, -1, X_HEADS, X_HEAD_DIM),
            ks.reshape(1, db, 1, B_KV_HEADS, B_HEAD_DIM), vs.reshape(1, db, 1, B_KV_HEADS, B_HEAD_DIM),
            iks.reshape(1, db, 1, IDX_DIM),
            st_s.reshape(1, db, A_HEADS, A_DK, A_DV))
```

```python
import functools
import math

import jax
import jax.numpy as jnp
import numpy as np
from jax import lax
from jax.experimental import pallas as pl
from jax.experimental.pallas import tpu as pltpu

F32 = jnp.float32
BF16 = jnp.bfloat16
EPS = 1e-6

D_MODEL = 2048
A_HEADS, A_DK, A_DV = 8, 128, 128
B_HEADS, B_KV_HEADS, B_HEAD_DIM = 8, 2, 128
B_GROUP = B_HEADS // B_KV_HEADS
IDX_HEADS, IDX_DIM = 16, 64
TOPK_MAX = 256
PAGE_SIZE = 128
REL_BUCKETS, REL_MAX_DIST = 32, 128
X_HEADS, X_HEAD_DIM = 4, 128
N_GROUPS, EXP_PER_GROUP = 4, 8
N_EXPERTS = N_GROUPS * EXP_PER_GROUP
EXPERT_TOPK = 2
EXPERT_FF = 512
MOE_ROWS = 256

LANES = 128
VMEM_LIMIT = 56 * 1024 * 1024

NEG = -1e30

C_AQ, C_AF, C_AI, C_AG, C_BQ, C_BK, C_BV, C_IQ = 0, 1024, 2048, 3072, 4096, 5120, 5376, 5632
NZ_MAIN = 6656
T_PAD = 432
T_SM, IW_LANE, IK_LANE = 384, 48, 64
T_GA, T_GB = 512, 2560
NZ_TAIL = 4608


def _cparams(*sem):
    return pltpu.CompilerParams(dimension_semantics=sem, vmem_limit_bytes=VMEM_LIMIT)


def _silu(x):
    return x * jax.nn.sigmoid(x)


def _nt(a, b):
    return lax.dot_general(a, b, (((1,), (1,)), ((), ())), preferred_element_type=F32)


def _nt_f32(a, b):
    return lax.dot_general(a, b, (((1,), (1,)), ((), ())), preferred_element_type=F32,
                           precision=lax.Precision.HIGHEST)


def _mm(a, w):
    return jnp.dot(a.astype(BF16), w.astype(BF16), preferred_element_type=F32)


def _norm_matmul_kernel(x_ref, g_ref, w_ref, o_ref, h_ref):
    @pl.when(pl.program_id(1) == 0)
    def _():
        x = x_ref[...]
        ms = jnp.mean(x * x, axis=-1, keepdims=True)
        h_ref[...] = (x * lax.rsqrt(ms + EPS) * g_ref[...]).astype(h_ref.dtype)

    o_ref[...] = _mm(h_ref[...], w_ref[...])


def norm_matmul(x, g, w, *, tm, tn):
    m, k = x.shape
    n = w.shape[1]
    assert n % tn == 0
    return pl.pallas_call(
        _norm_matmul_kernel,
        grid=(m // tm, n // tn),
        in_specs=[pl.BlockSpec((tm, k), lambda i, j: (i, 0)),
                  pl.BlockSpec((1, k), lambda i, j: (0, 0)),
                  pl.BlockSpec((k, tn), lambda i, j: (0, j))],
        out_specs=pl.BlockSpec((tm, tn), lambda i, j: (i, j)),
        out_shape=jax.ShapeDtypeStruct((m, n), F32),
        scratch_shapes=[pltpu.VMEM((tm, k), BF16)],
        compiler_params=_cparams("parallel", "arbitrary"),
        name="norm_matmul",
    )(x, g.reshape(1, k), w)


def _in_proj_kernel(x_ref, g_ref, wa_ref, wb_ref, oa_ref, ob_ref, h_ref, *, na):
    j = pl.program_id(1)

    @pl.when(j == 0)
    def _():
        x = x_ref[...]
        ms = jnp.mean(x * x, axis=-1, keepdims=True)
        h_ref[...] = (x * lax.rsqrt(ms + EPS) * g_ref[...]).astype(h_ref.dtype)

    @pl.when(j < na)
    def _():
        oa_ref[...] = _nt(h_ref[...], wa_ref[...])

    @pl.when(j >= na)
    def _():
        ob_ref[...] = _nt(h_ref[...], wb_ref[...])


def in_proj(x, g, wa_t, a_rows, wb_t, *, tm, tn):
    m, k = x.shape
    na, nb = a_rows // tn, wb_t.shape[0] // tn
    assert a_rows % tn == 0 and wb_t.shape[0] % tn == 0
    a_idx = lambda j: jnp.minimum(j, na - 1)
    b_idx = lambda j: jnp.maximum(j - na, 0)
    return pl.pallas_call(
        functools.partial(_in_proj_kernel, na=na),
        grid=(m // tm, na + nb),
        in_specs=[pl.BlockSpec((tm, k), lambda i, j: (i, 0)),
                  pl.BlockSpec((1, k), lambda i, j: (0, 0)),
                  pl.BlockSpec((tn, k), lambda i, j: (a_idx(j), 0)),
                  pl.BlockSpec((tn, k), lambda i, j: (b_idx(j), 0))],
        out_specs=[pl.BlockSpec((tm, tn), lambda i, j: (i, a_idx(j))),
                   pl.BlockSpec((tm, tn), lambda i, j: (i, b_idx(j)))],
        out_shape=[jax.ShapeDtypeStruct((m, na * tn), F32), jax.ShapeDtypeStruct((m, nb * tn), F32)],
        scratch_shapes=[pltpu.VMEM((tm, k), BF16)],
        compiler_params=_cparams("parallel", "arbitrary"),
        name="in_proj",
    )(x, g.reshape(1, k), wa_t, wb_t)


def _matmul_res_kernel(x_ref, w_ref, r_ref, o_ref):
    o_ref[...] = r_ref[...] + _mm(x_ref[...], w_ref[...])


def _matmul_kernel(x_ref, w_ref, o_ref):
    o_ref[...] = _mm(x_ref[...], w_ref[...])


def matmul(x, w, res=None, *, tm, tn):
    m, k = x.shape
    n = w.shape[1]
    in_specs = [pl.BlockSpec((tm, k), lambda i, j: (i, 0)),
                pl.BlockSpec((k, tn), lambda i, j: (0, j))]
    args = [x, w]
    kern = _matmul_kernel
    if res is not None:
        in_specs.append(pl.BlockSpec((tm, tn), lambda i, j: (i, j)))
        args.append(res)
        kern = _matmul_res_kernel
    return pl.pallas_call(
        kern,
        grid=(m // tm, n // tn),
        in_specs=in_specs,
        out_specs=pl.BlockSpec((tm, tn), lambda i, j: (i, j)),
        out_shape=jax.ShapeDtypeStruct((m, n), F32),
        compiler_params=_cparams("parallel", "arbitrary"),
        name="matmul",
    )(*args)


HG_TB = 128
HG_C = 16
HG_H = HG_C // 2


def _hgrn_prompt_kernel(aq_ref, af_ref, ai_ref, ag_ref, lbl_ref, ng_ref, ya_ref, st_out_ref,
                        st_ref, q_s, k_s, g_s, v_s):
    t = pl.program_id(0)

    @pl.when(t == 0)
    def _():
        st_ref[...] = jnp.zeros_like(st_ref)

    lbl = lbl_ref[...]
    mx = jnp.max(lbl, axis=0, keepdims=True)
    ex = jnp.exp(lbl - mx)
    lb = ex[0:1, :] / jnp.sum(ex, axis=0, keepdims=True)

    f = lb + (1.0 - lb) * jax.nn.sigmoid(af_ref[...])
    logf = jnp.log(f)
    row = lax.broadcasted_iota(jnp.int32, (HG_TB, HG_TB), 0)
    col = lax.broadcasted_iota(jnp.int32, (HG_TB, HG_TB), 1)
    tri = jnp.where((row // HG_C == col // HG_C) & (col <= row), 1.0, 0.0).astype(F32)
    g_s[...] = jnp.dot(tri, logf, preferred_element_type=F32, precision=lax.Precision.HIGHEST)
    q_s[...] = _silu(aq_ref[...])
    k_s[...] = 1.0 - f
    v_s[...] = ai_ref[...]

    sub = lax.broadcasted_iota(jnp.int32, (HG_C, A_DK), 0)
    sub8 = lax.broadcasted_iota(jnp.int32, (HG_H, A_DK), 0)
    ng = ng_ref[...]

    def chunk(c, carry):
        r0 = pl.multiple_of(c * HG_C, HG_C)
        rows = pl.ds(r0, HG_C)
        for h in range(A_HEADS):
            cols = slice(h * A_DK, (h + 1) * A_DK)
            g = g_s[rows, cols]
            qh = q_s[rows, cols]
            kh = k_s[rows, cols]
            vh = v_s[rows, cols]
            halves = []
            for hb in range(2):
                rs = slice(hb * HG_H, (hb + 1) * HG_H)
                gb, qb, kb, vb = g[rs], qh[rs], kh[rs], vh[rs]
                ob = jnp.zeros((HG_H, A_DV), F32)
                for tt in range(HG_H):
                    d = gb[tt:tt + 1, :] - gb
                    e = jnp.exp(jnp.where(sub8 <= tt, d, -jnp.inf))
                    p = e * (qb[tt:tt + 1, :] * kb)
                    a_col = jnp.sum(p, axis=1, keepdims=True)
                    o_row = jnp.sum(a_col * vb, axis=0, keepdims=True)
                    ob = jnp.where(sub8 == tt, o_row, ob)
                halves.append(ob)
            o = jnp.concatenate(halves, axis=0)
            low = sub < HG_H
            g_mid = g[HG_H - 1:HG_H, :]
            q_hi = jnp.where(low, 0.0, qh * jnp.exp(jnp.minimum(g - g_mid, 0.0)))
            k_lo = jnp.where(low, kh * jnp.exp(jnp.minimum(g_mid - g, 0.0)), 0.0)
            st = st_ref[h]
            g_last = g[HG_C - 1:HG_C, :]
            kt = kh * jnp.exp(g_last - g)
            upd = lax.dot_general(vh.astype(BF16), jnp.concatenate([kt, k_lo], axis=1).astype(BF16),
                                  (((0,), (0,)), ((), ())), preferred_element_type=F32)
            lhs = jnp.concatenate([qh * jnp.exp(g), q_hi], axis=1).astype(BF16)
            rhs = jnp.concatenate([st, upd[:, A_DK:]], axis=1).astype(BF16)
            o = o + _nt(lhs, rhs)
            st_ref[h] = st * jnp.exp(g_last) + upd[:, :A_DK]
            on = o * lax.rsqrt(jnp.mean(o * o, axis=-1, keepdims=True) + EPS) * ng
            ya_ref[rows, cols] = (on * _silu(ag_ref[rows, cols])).astype(ya_ref.dtype)
        return carry

    lax.fori_loop(0, HG_TB // HG_C, chunk, 0)

    @pl.when(t == pl.num_programs(0) - 1)
    def _():
        st_out_ref[...] = st_ref[...]


def hgrn_prompt(z, lb_logits, norm_g):
    m = z.shape[0]
    w = A_HEADS * A_DK

    def zspec(cb):
        return pl.BlockSpec((HG_TB, w), lambda t, cb=cb: (t, cb))

    return pl.pallas_call(
        _hgrn_prompt_kernel,
        grid=(m // HG_TB,),
        in_specs=[zspec(C_AQ // w), zspec(C_AF // w), zspec(C_AI // w), zspec(C_AG // w),
                  pl.BlockSpec(lb_logits.shape, lambda t: (0, 0)),
                  pl.BlockSpec((1, A_DV), lambda t: (0, 0))],
        out_specs=[pl.BlockSpec((HG_TB, w), lambda t: (t, 0)),
                   pl.BlockSpec((A_HEADS, A_DV, A_DK), lambda t: (0, 0, 0))],
        out_shape=[jax.ShapeDtypeStruct((m, w), BF16),
                   jax.ShapeDtypeStruct((A_HEADS, A_DV, A_DK), F32)],
        scratch_shapes=[pltpu.VMEM((A_HEADS, A_DV, A_DK), F32)] + [pltpu.VMEM((HG_TB, w), F32)] * 4,
        compiler_params=_cparams("arbitrary"),
        name="hgrn_prompt",
    )(z, z, z, z, lb_logits, norm_g.reshape(1, A_DV))


def _hgrn_step_kernel(z_ref, lbl_ref, ng_ref, s_ref, ya_ref, s_out_ref):
    lbl = lbl_ref[...]
    mx = jnp.max(lbl, axis=0, keepdims=True)
    ex = jnp.exp(lbl - mx)
    lb = ex[0:1, :] / jnp.sum(ex, axis=0, keepdims=True)
    z = z_ref[0]
    w = A_HEADS * A_DK
    q = _silu(z[:, 0:w])
    f = lb + (1.0 - lb) * jax.nn.sigmoid(z[:, w:2 * w])
    kk = 1.0 - f
    v = z[:, 2 * w:3 * w]
    ag = z[:, 3 * w:4 * w]
    rows = []
    for h in range(A_HEADS):
        cols = slice(h * A_DK, (h + 1) * A_DK)
        rows += [f[:, cols], kk[:, cols], q[:, cols]]
    rows.append(jnp.zeros((LANES - 3 * A_HEADS, A_DK), F32))
    xt = jnp.concatenate(rows, axis=0).T
    ng = ng_ref[...]
    r16 = lambda a: a.astype(BF16).astype(F32)
    outs = []
    for h in range(A_HEADS):
        cols = slice(h * A_DV, (h + 1) * A_DV)
        fcol = xt[:, 3 * h:3 * h + 1]
        kcol = xt[:, 3 * h + 1:3 * h + 2]
        qcol = xt[:, 3 * h + 2:3 * h + 3]
        s_old = s_ref[0, h]
        s_out_ref[0, h] = fcol * s_old + kcol * v[:, cols]
        o = (jnp.sum(r16(qcol * fcol) * r16(s_old), axis=0, keepdims=True)
             + jnp.sum(qcol * kcol, axis=0, keepdims=True) * v[:, cols])
        on = o * lax.rsqrt(jnp.mean(o * o, axis=-1, keepdims=True) + EPS) * ng
        outs.append(on * _silu(ag[:, cols]))
    ya_ref[0] = jnp.concatenate(outs, axis=1).astype(ya_ref.dtype)


def hgrn_step(z4, lb_logits, norm_g, state):
    b = z4.shape[0]
    w = A_HEADS * A_DK
    return pl.pallas_call(
        _hgrn_step_kernel,
        grid=(b,),
        in_specs=[pl.BlockSpec((1, 1, 4 * w), lambda i: (i, 0, 0)),
                  pl.BlockSpec(lb_logits.shape, lambda i: (0, 0)),
                  pl.BlockSpec((1, A_DV), lambda i: (0, 0)),
                  pl.BlockSpec((1, A_HEADS, A_DK, A_DV), lambda i: (i, 0, 0, 0))],
        out_specs=[pl.BlockSpec((1, 1, w), lambda i: (i, 0, 0)),
                   pl.BlockSpec((1, A_HEADS, A_DK, A_DV), lambda i: (i, 0, 0, 0))],
        out_shape=[jax.ShapeDtypeStruct((b, 1, w), F32),
                   jax.ShapeDtypeStruct(state.shape, F32)],
        compiler_params=_cparams("arbitrary"),
        name="hgrn_step",
    )(z4.reshape(b, 1, 4 * w), lb_logits, norm_g.reshape(1, A_DV), state)


BISECT_MAX_ITERS = 48


def _bisect_threshold(count_ge, lo, hi, cnt_lo, topk):
    kf = float(topk)

    def cond(c):
        return jnp.logical_and(c[0] < BISECT_MAX_ITERS, c[-1] > 0.0)

    def body(c):
        it, lo, hi, cl, ch, _ = c
        mid = 0.5 * lo + 0.5 * hi
        t_int = hi - (hi - lo) * ((kf - ch) / jnp.maximum(cl - ch, 1.0))
        ok = jnp.logical_and(it % 2 == 0, jnp.logical_and(t_int > lo, t_int < hi))
        t = jnp.where(ok, t_int, mid)
        cm = count_ge(t)
        ge = cm >= kf
        lo = jnp.where(ge, t, lo)
        cl = jnp.where(ge, cm, cl)
        hi = jnp.where(ge, hi, t)
        ch = jnp.where(ge, ch, cm)
        busy = jnp.max(jnp.where(cl > kf, 1.0, 0.0))
        return it + 1, lo, hi, cl, ch, busy

    busy0 = jnp.max(jnp.where(cnt_lo > kf, 1.0, 0.0))
    out = lax.while_loop(cond, body, (jnp.int32(0), lo, hi, cnt_lo, jnp.ones_like(cnt_lo), busy0))
    return out[1]


DSA_QB = 128
DSA_W = 512
DSA_W3 = 1024


def _dsa_prompt_kernel(iq0_ref, iq1_ref, bq_ref, iw_ref, kidx_ref, k_ref, v_ref, bias_ref, o_ref,
                       score_s, qih_s, qs_s, wb_s, m_s, l_s, acc_s, *, topk):
    i = pl.program_id(0)
    nsub = DSA_W // LANES
    nsub3 = DSA_W3 // LANES
    nch3 = (i * DSA_QB + DSA_QB + DSA_W3 - 1) // DSA_W3
    nch = nch3 * (DSA_W3 // DSA_W)
    qpos = i * DSA_QB + lax.broadcasted_iota(jnp.int32, (DSA_QB, 1), 0)

    iw = iw_ref[...]
    wscale = IDX_DIM ** -0.5 * IDX_HEADS ** -0.5
    for h in range(IDX_HEADS):
        iq_ref, hh = (iq0_ref, h) if h < IDX_HEADS // 2 else (iq1_ref, h - IDX_HEADS // 2)
        qih_s[h] = iq_ref[:, hh * IDX_DIM:(hh + 1) * IDX_DIM].astype(BF16)
        wb_s[h] = jnp.broadcast_to(iw[:, IW_LANE + h:IW_LANE + h + 1] * wscale, (DSA_QB, LANES))
    for h in range(B_HEADS):
        qs_s[h // B_GROUP, (h % B_GROUP) * DSA_QB:(h % B_GROUP + 1) * DSA_QB, :] = (
            bq_ref[:, h * B_HEAD_DIM:(h + 1) * B_HEAD_DIM] * B_HEAD_DIM ** -0.5).astype(BF16)

    def p1(c, carry):
        c0 = pl.multiple_of(c * DSA_W, DSA_W)
        kc = kidx_ref[pl.ds(c0, DSA_W), :]
        sc = [jnp.zeros((DSA_QB, LANES), F32) for _ in range(nsub)]
        for h in range(IDX_HEADS):
            s = jnp.maximum(_nt(qih_s[h], kc), 0.0)
            wb = wb_s[h]
            for j in range(nsub):
                sc[j] = sc[j] + s[:, j * LANES:(j + 1) * LANES] * wb
        for j in range(nsub):
            kpos = c0 + j * LANES + lax.broadcasted_iota(jnp.int32, (1, LANES), 1)
            score_s[c * nsub + j] = jnp.where(kpos <= qpos, sc[j], -jnp.inf)
        return carry

    lax.fori_loop(0, nch, p1, 0)

    def stats(c, carry):
        mn, mx = carry
        for j in range(nsub):
            s = score_s[c * nsub + j]
            mx = jnp.maximum(mx, s)
            mn = jnp.minimum(mn, jnp.where(s > -jnp.inf, s, jnp.inf))
        return mn, mx

    mn, mx = lax.fori_loop(0, nch, stats, (jnp.full((DSA_QB, LANES), jnp.inf, F32),
                                           jnp.full((DSA_QB, LANES), -jnp.inf, F32)))
    lo0 = jnp.min(mn, axis=1, keepdims=True)
    hi0 = jnp.max(mx, axis=1, keepdims=True)

    def count_ge(thr):
        thr_b = jnp.broadcast_to(thr, (DSA_QB, LANES))

        def body(c, acc):
            for j in range(nsub):
                s = score_s[c * nsub + j]
                acc = acc + jnp.where(s >= thr_b, 1.0, 0.0)
            return acc

        acc = lax.fori_loop(0, nch, body, jnp.zeros((DSA_QB, LANES), F32))
        return jnp.sum(acc, axis=1, keepdims=True)

    thr = _bisect_threshold(count_ge, lo0, hi0, (qpos + 1).astype(F32), topk)
    thr_b = jnp.broadcast_to(thr, (DSA_QB, LANES))

    m_s[...] = jnp.full(m_s.shape, NEG, F32)
    l_s[...] = jnp.zeros(l_s.shape, F32)
    acc_s[...] = jnp.zeros(acc_s.shape, F32)

    def p3(c, with_bias):
        c0 = pl.multiple_of(c * DSA_W3, DSA_W3)
        madd = jnp.concatenate([jnp.where(score_s[c * nsub3 + j] >= thr_b, 0.0, NEG) for j in range(nsub3)], axis=1)
        kc = k_ref[pl.ds(c0, DSA_W3), :]
        vc = v_ref[pl.ds(c0, DSA_W3), :]
        rel = [i - (c * nsub3 + j) for j in range(nsub3)]

        def scores(n):
            return _nt(qs_s[n], kc[:, n * B_HEAD_DIM:(n + 1) * B_HEAD_DIM])

        def softmax(n, lg):
            lg = lg.reshape(B_GROUP, DSA_QB, DSA_W3) + madd[None]
            if with_bias:
                lg = lg + jnp.stack([jnp.concatenate(
                    [jnp.where(rel[j] == 0, bias_ref[n * B_GROUP + gq, 0],
                               jnp.where(rel[j] == 1, bias_ref[n * B_GROUP + gq, 1], 0.0)) for j in range(nsub3)],
                    axis=1) for gq in range(B_GROUP)])
            m_old = m_s[n]
            m_new = jnp.maximum(m_old, jnp.max(lg, axis=-1, keepdims=True))
            p = jnp.exp(lg - m_new)
            alpha = jnp.exp(m_old - m_new)
            l_s[n] = alpha * l_s[n] + jnp.sum(p, axis=-1, keepdims=True)
            m_s[n] = m_new
            pv = jnp.dot(p.reshape(B_GROUP * DSA_QB, DSA_W3).astype(BF16), vc[:, n * B_HEAD_DIM:(n + 1) * B_HEAD_DIM],
                         preferred_element_type=F32)
            return alpha, pv.reshape(B_GROUP, DSA_QB, B_HEAD_DIM)

        lgs = [scores(n) for n in range(B_KV_HEADS)]
        outs = [softmax(n, lgs[n]) for n in range(B_KV_HEADS)]
        for n in range(B_KV_HEADS):
            acc_s[n] = outs[n][0] * acc_s[n] + outs[n][1]

    n_far = jnp.maximum(i - 1, 0) // nsub3
    lax.fori_loop(0, n_far, lambda c, carry: (p3(c, False), carry)[1], 0)
    lax.fori_loop(n_far, nch3, lambda c, carry: (p3(c, True), carry)[1], 0)

    for h in range(B_HEADS):
        n, gq = h // B_GROUP, h % B_GROUP
        o_ref[:, h * B_HEAD_DIM:(h + 1) * B_HEAD_DIM] = (acc_s[n, gq] / l_s[n, gq]).astype(o_ref.dtype)


def dsa_prompt(z, ztail, kidx_bf, k_bf, v_bf, bias_tiles):
    m = z.shape[0]
    topk = min(TOPK_MAX, m // 4)
    wq = B_HEADS * B_HEAD_DIM
    wi2 = IDX_HEADS * IDX_DIM // 2
    kern = functools.partial(_dsa_prompt_kernel, topk=topk)
    return pl.pallas_call(
        kern,
        grid=(m // DSA_QB,),
        in_specs=[pl.BlockSpec((DSA_QB, wi2), lambda i: (i, C_IQ // wi2)),
                  pl.BlockSpec((DSA_QB, wi2), lambda i: (i, C_IQ // wi2 + 1)),
                  pl.BlockSpec((DSA_QB, wq), lambda i: (i, C_BQ // wq)),
                  pl.BlockSpec((DSA_QB, LANES), lambda i: (i, T_SM // LANES)),
                  pl.BlockSpec(kidx_bf.shape, lambda i: (0, 0)),
                  pl.BlockSpec(k_bf.shape, lambda i: (0, 0)),
                  pl.BlockSpec(v_bf.shape, lambda i: (0, 0)),
                  pl.BlockSpec(bias_tiles.shape, lambda i: (0, 0, 0, 0))],
        out_specs=pl.BlockSpec((DSA_QB, wq), lambda i: (i, 0)),
        out_shape=jax.ShapeDtypeStruct((m, wq), BF16),
        scratch_shapes=[pltpu.VMEM((m // LANES, DSA_QB, LANES), F32),
                        pltpu.VMEM((IDX_HEADS, DSA_QB, IDX_DIM), BF16),
                        pltpu.VMEM((B_KV_HEADS, B_GROUP * DSA_QB, B_HEAD_DIM), BF16),
                        pltpu.VMEM((IDX_HEADS, DSA_QB, LANES), F32),
                        pltpu.VMEM((B_KV_HEADS, B_GROUP, DSA_QB, 1), F32),
                        pltpu.VMEM((B_KV_HEADS, B_GROUP, DSA_QB, 1), F32),
                        pltpu.VMEM((B_KV_HEADS, B_GROUP, DSA_QB, B_HEAD_DIM), F32)],
        compiler_params=_cparams("arbitrary"),
        name="dsa_prompt",
    )(z, z, z, ztail, kidx_bf, k_bf, v_bf, bias_tiles)


def _page_copies(table_ref, b, n_pages, src_hbm, dst, sem, rows_per_page=PAGE_SIZE):
    def copy(p):
        return pltpu.make_async_copy(src_hbm.at[table_ref[b, p]],
                                     dst.at[pl.ds(p * rows_per_page, rows_per_page)], sem)
    return copy


def _dsa_scores_kernel(pt_ref, iq_ref, iw_ref, iknew_ref, kidx_hbm, o_ref, buf, sem, *, n_pages):
    b = pl.program_id(0)
    nb = pl.num_programs(0)
    past = n_pages * PAGE_SIZE

    def start(bb, slot):
        cp = _page_copies(pt_ref, bb, n_pages, kidx_hbm, buf.at[slot], sem.at[slot])
        lax.fori_loop(0, n_pages, lambda p, c: (cp(p).start(), c)[1], 0)

    def wait(bb, slot):
        cp = _page_copies(pt_ref, bb, n_pages, kidx_hbm, buf.at[slot], sem.at[slot])
        lax.fori_loop(0, n_pages, lambda p, c: (cp(p).wait(), c)[1], 0)

    slot = b % 2

    @pl.when(b == 0)
    def _():
        start(0, 0)

    @pl.when(b + 1 < nb)
    def _():
        start(b + 1, 1 - slot)

    wait(b, slot)

    r16 = lambda a: a.astype(BF16).astype(F32)
    qi = iq_ref[0].astype(BF16)
    wcol = r16(iw_ref[0]) * (IDX_DIM ** -0.5 * IDX_HEADS ** -0.5)
    s = r16(jnp.maximum(_nt(qi, buf[slot].astype(BF16)), 0.0))
    o_ref[0, :, 0:past] = jnp.sum(s * wcol, axis=0, keepdims=True)
    sn = r16(jnp.maximum(_nt(qi, iknew_ref[0].astype(BF16)), 0.0))
    sn = jnp.sum(sn * wcol, axis=0, keepdims=True)
    lane = lax.broadcasted_iota(jnp.int32, (1, LANES), 1)
    o_ref[0, :, past:past + LANES] = jnp.where(lane == 0, sn, -jnp.inf)


def dsa_scores(page_table, iq, iw, iknew_pad, cache_kidx):
    b, n_pages = page_table.shape
    past = n_pages * PAGE_SIZE
    kern = functools.partial(_dsa_scores_kernel, n_pages=n_pages)
    gs = pltpu.PrefetchScalarGridSpec(
        num_scalar_prefetch=1,
        grid=(b,),
        in_specs=[pl.BlockSpec((1, IDX_HEADS, IDX_DIM), lambda i, pt: (i, 0, 0)),
                  pl.BlockSpec((1, IDX_HEADS, 1), lambda i, pt: (i, 0, 0)),
                  pl.BlockSpec((1, LANES, IDX_DIM), lambda i, pt: (i, 0, 0)),
                  pl.BlockSpec(memory_space=pl.ANY)],
        out_specs=pl.BlockSpec((1, 1, past + LANES), lambda i, pt: (i, 0, 0)),
        scratch_shapes=[pltpu.VMEM((2, past, IDX_DIM), F32), pltpu.SemaphoreType.DMA((2,))],
    )
    return pl.pallas_call(
        kern, grid_spec=gs,
        out_shape=jax.ShapeDtypeStruct((b, 1, past + LANES), F32),
        compiler_params=_cparams("arbitrary"),
        name="dsa_scores",
    )(page_table, iq, iw, iknew_pad, cache_kidx)


def _dsa_threshold_kernel(s_ref, thr_ref, *, topk):
    s = s_ref[...]
    nb = s.shape[0]
    finite = s > -jnp.inf
    lo0 = jnp.min(jnp.where(finite, s, jnp.inf), axis=1, keepdims=True)
    hi0 = jnp.max(s, axis=1, keepdims=True)
    cnt0 = jnp.sum(jnp.where(finite, 1.0, 0.0), axis=1, keepdims=True)

    def count_ge(thr):
        return jnp.sum(jnp.where(s_ref[...] >= thr, 1.0, 0.0), axis=1, keepdims=True)

    thr = _bisect_threshold(count_ge, lo0, hi0, cnt0, topk)
    thr_ref[...] = jnp.broadcast_to(thr, (nb, LANES))


def dsa_threshold(scores, topk):
    b, l = scores.shape
    return pl.pallas_call(
        functools.partial(_dsa_threshold_kernel, topk=topk),
        grid=(1,),
        in_specs=[pl.BlockSpec((b, l), lambda i: (0, 0))],
        out_specs=pl.BlockSpec((b, LANES), lambda i: (0, 0)),
        out_shape=jax.ShapeDtypeStruct((b, LANES), F32),
        compiler_params=_cparams("arbitrary"),
        name="dsa_threshold",
    )(scores)


def _dsa_decode_kernel(pt_ref, q_ref, s_ref, thr_ref, knew_ref, vnew_ref, bias_ref, k_hbm, v_hbm, o_ref,
                       kbuf, vbuf, sem, *, n_pages):
    b = pl.program_id(0)
    nb = pl.num_programs(0)
    past = n_pages * PAGE_SIZE

    rpp = PAGE_SIZE * B_KV_HEADS

    def copies(bb, slot):
        ck = _page_copies(pt_ref, bb, n_pages, k_hbm, kbuf.at[slot], sem.at[0, slot], rpp)
        cv = _page_copies(pt_ref, bb, n_pages, v_hbm, vbuf.at[slot], sem.at[1, slot], rpp)
        return ck, cv

    def start(bb, slot):
        ck, cv = copies(bb, slot)
        lax.fori_loop(0, n_pages, lambda p, c: (ck(p).start(), cv(p).start(), c)[2], 0)

    def wait(bb, slot):
        ck, cv = copies(bb, slot)
        lax.fori_loop(0, n_pages, lambda p, c: (ck(p).wait(), cv(p).wait(), c)[2], 0)

    slot = b % 2

    @pl.when(b == 0)
    def _():
        kbuf[:, n_pages * rpp:, :] = jnp.zeros((2, rpp, B_HEAD_DIM), F32)
        vbuf[:, n_pages * rpp:, :] = jnp.zeros((2, rpp, B_HEAD_DIM), F32)
        start(0, 0)

    @pl.when(b + 1 < nb)
    def _():
        start(b + 1, 1 - slot)

    kbuf[slot, n_pages * rpp:n_pages * rpp + 8, :] = knew_ref[0]
    vbuf[slot, n_pages * rpp:n_pages * rpp + 8, :] = vnew_ref[0]
    wait(b, slot)

    sel = s_ref[0] >= thr_ref[0][:, 0:1]
    n_keys = past + PAGE_SIZE
    outs = []
    for n in range(B_KV_HEADS):
        kn = kbuf[slot, pl.ds(n, n_keys, stride=B_KV_HEADS), :].astype(BF16)
        vn = vbuf[slot, pl.ds(n, n_keys, stride=B_KV_HEADS), :].astype(BF16)
        qn = q_ref[0, n].astype(BF16)
        lg = _nt(qn, kn) * B_HEAD_DIM ** -0.5 + bias_ref[n]
        m = jnp.max(jnp.where(sel, lg, NEG), axis=1, keepdims=True)
        p = jnp.where(sel, jnp.exp(lg - m), 0.0)
        p = p / jnp.sum(p, axis=1, keepdims=True)
        outs.append(jnp.dot(p.astype(BF16), vn, preferred_element_type=F32))
    o_ref[0] = jnp.concatenate(outs, axis=0).astype(o_ref.dtype)


def dsa_decode(page_table, q8, scores, thr, knew8, vnew8, bias_rows, cache_k2, cache_v2):
    b, n_pages = page_table.shape
    past = n_pages * PAGE_SIZE
    l = past + LANES
    wkv = B_KV_HEADS * B_HEAD_DIM
    kern = functools.partial(_dsa_decode_kernel, n_pages=n_pages)
    gs = pltpu.PrefetchScalarGridSpec(
        num_scalar_prefetch=1,
        grid=(b,),
        in_specs=[pl.BlockSpec((1, B_KV_HEADS, 8, B_HEAD_DIM), lambda i, pt: (i, 0, 0, 0)),
                  pl.BlockSpec((1, 1, l), lambda i, pt: (i, 0, 0)),
                  pl.BlockSpec((1, 1, LANES), lambda i, pt: (i, 0, 0)),
                  pl.BlockSpec((1, 8, B_HEAD_DIM), lambda i, pt: (i, 0, 0)),
                  pl.BlockSpec((1, 8, B_HEAD_DIM), lambda i, pt: (i, 0, 0)),
                  pl.BlockSpec((B_KV_HEADS, 8, l), lambda i, pt: (0, 0, 0)),
                  pl.BlockSpec(memory_space=pl.ANY),
                  pl.BlockSpec(memory_space=pl.ANY)],
        out_specs=pl.BlockSpec((1, 2 * 8, B_HEAD_DIM), lambda i, pt: (i, 0, 0)),
        scratch_shapes=[pltpu.VMEM((2, l * B_KV_HEADS, B_HEAD_DIM), F32),
                        pltpu.VMEM((2, l * B_KV_HEADS, B_HEAD_DIM), F32),
                        pltpu.SemaphoreType.DMA((2, 2))],
    )
    return pl.pallas_call(
        kern, grid_spec=gs,
        out_shape=jax.ShapeDtypeStruct((b, 2 * 8, B_HEAD_DIM), F32),
        compiler_params=_cparams("arbitrary"),
        name="dsa_decode",
    )(page_table, q8, scores, thr, knew8, vnew8, bias_rows, cache_k2, cache_v2)


def _merge_kernel(ya_ref, yb_ref, ga_ref, gb_ref, wa_ref, wb_ref, o_ref):
    a = _mm(ya_ref[...], wa_ref[...])
    bb = _mm(yb_ref[...], wb_ref[...])
    o_ref[...] = (jax.nn.sigmoid(ga_ref[...]) * a + jax.nn.sigmoid(gb_ref[...]) * bb).astype(o_ref.dtype)


def merge(ya, yb, z, ga_col, gb_col, wa, wb, *, tm, tn):
    m, k = ya.shape
    n = wa.shape[1]
    return pl.pallas_call(
        _merge_kernel,
        grid=(m // tm, n // tn),
        in_specs=[pl.BlockSpec((tm, k), lambda i, j: (i, 0)),
                  pl.BlockSpec((tm, k), lambda i, j: (i, 0)),
                  pl.BlockSpec((tm, tn), lambda i, j: (i, ga_col // tn + j)),
                  pl.BlockSpec((tm, tn), lambda i, j: (i, gb_col // tn + j)),
                  pl.BlockSpec((k, tn), lambda i, j: (0, j)),
                  pl.BlockSpec((k, tn), lambda i, j: (0, j))],
        out_specs=pl.BlockSpec((tm, tn), lambda i, j: (i, j)),
        out_shape=jax.ShapeDtypeStruct((m, n), wa.dtype),
        compiler_params=_cparams("parallel", "arbitrary"),
        name="merge",
    )(ya, yb, z, z, wa, wb)


def _cross_prompt_kernel(x_ref, g_ref, wq_ref, mk_ref, mv_ref, wo_ref, o_ref):
    x = x_ref[...]
    ms = jnp.mean(x * x, axis=-1, keepdims=True)
    h = (x * lax.rsqrt(ms + EPS) * g_ref[...]).astype(BF16)
    q = jnp.dot(h, wq_ref[...], preferred_element_type=F32)
    outs = []
    for hh in range(X_HEADS):
        cols = slice(hh * X_HEAD_DIM, (hh + 1) * X_HEAD_DIM)
        lg = _nt(q[:, cols].astype(BF16), mk_ref[:, cols]) * X_HEAD_DIM ** -0.5
        mx = jnp.max(lg, axis=1, keepdims=True)
        p = jnp.exp(lg - mx)
        p = p / jnp.sum(p, axis=1, keepdims=True)
        outs.append(jnp.dot(p.astype(BF16), mv_ref[:, cols], preferred_element_type=F32).astype(BF16))
    att = jnp.concatenate(outs, axis=1)
    o_ref[...] = x + jnp.dot(att, wo_ref[...], preferred_element_type=F32)


def cross_prompt(x, g, wq, mk, mv, wo, *, tm):
    m, d = x.shape
    full = lambda a: pl.BlockSpec(a.shape, lambda i: (0,) * a.ndim)
    g2 = g.reshape(1, d)
    return pl.pallas_call(
        _cross_prompt_kernel,
        grid=(m // tm,),
        in_specs=[pl.BlockSpec((tm, d), lambda i: (i, 0)), full(g2), full(wq), full(mk), full(mv), full(wo)],
        out_specs=pl.BlockSpec((tm, d), lambda i: (i, 0)),
        out_shape=jax.ShapeDtypeStruct((m, d), F32),
        compiler_params=_cparams("parallel"),
        name="cross_prompt",
    )(x, g2, wq, mk, mv, wo)


def _cross_step_kernel(q_ref, mk_ref, mv_ref, o_ref):
    r16 = lambda a: a.astype(BF16).astype(F32)
    q = r16(q_ref[0])
    outs = []
    for hh in range(X_HEADS):
        cols = slice(hh * X_HEAD_DIM, (hh + 1) * X_HEAD_DIM)
        kh = r16(mk_ref[0, :, cols])
        vh = r16(mv_ref[0, :, cols])
        lg = jnp.sum(kh * q[:, cols], axis=1, keepdims=True) * X_HEAD_DIM ** -0.5
        mx = jnp.max(lg, axis=0, keepdims=True)
        p = jnp.exp(lg - mx)
        p = r16(p / jnp.sum(p, axis=0, keepdims=True))
        outs.append(jnp.sum(p * vh, axis=0, keepdims=True))
    o_ref[0] = jnp.concatenate(outs, axis=1).astype(o_ref.dtype)


def cross_step(q, mk, mv):
    b, w = q.shape
    mem = mk.shape[1]
    return pl.pallas_call(
        _cross_step_kernel,
        grid=(b,),
        in_specs=[pl.BlockSpec((1, 1, w), lambda i: (i, 0, 0)),
                  pl.BlockSpec((1, mem, w), lambda i: (i, 0, 0)),
                  pl.BlockSpec((1, mem, w), lambda i: (i, 0, 0))],
        out_specs=pl.BlockSpec((1, 1, w), lambda i: (i, 0, 0)),
        out_shape=jax.ShapeDtypeStruct((b, 1, w), F32),
        compiler_params=_cparams("arbitrary"),
        name="cross_step",
    )(q.reshape(b, 1, w), mk, mv)


def _router_kernel(x_ref, g_ref, w_ref, b_ref, hf_ref, route_ref):
    x = x_ref[...]
    ms = jnp.mean(x * x, axis=-1, keepdims=True)
    hf = x * lax.rsqrt(ms + EPS) * g_ref[...]
    hf_ref[...] = hf
    lg = _mm(hf, w_ref[...]) + b_ref[...]
    tm = lg.shape[0]
    lane = lax.broadcasted_iota(jnp.int32, (tm, LANES), 1)
    big = jnp.int32(LANES)
    is_g = lane < N_GROUPS
    gmax = jnp.max(jnp.where(is_g, lg, -jnp.inf), axis=1, keepdims=True)
    grp = jnp.min(jnp.where(is_g & (lg == gmax), lane, big), axis=1, keepdims=True)
    p_grp = 1.0 / jnp.sum(jnp.where(is_g, jnp.exp(lg - gmax), 0.0), axis=1, keepdims=True)
    e_lo = N_GROUPS + grp * EXP_PER_GROUP
    in_g = (lane >= e_lo) & (lane < e_lo + EXP_PER_GROUP)
    v1 = jnp.max(jnp.where(in_g, lg, -jnp.inf), axis=1, keepdims=True)
    i1 = jnp.min(jnp.where(in_g & (lg == v1), lane, big), axis=1, keepdims=True)
    rest = in_g & (lane != i1)
    v2 = jnp.max(jnp.where(rest, lg, -jnp.inf), axis=1, keepdims=True)
    i2 = jnp.min(jnp.where(rest & (lg == v2), lane, big), axis=1, keepdims=True)
    e2 = jnp.exp(v2 - v1)
    g1 = p_grp / (1.0 + e2)
    g2 = p_grp * e2 / (1.0 + e2)
    r = jnp.where(lane == 0, (i1 - N_GROUPS).astype(F32),
                  jnp.where(lane == 1, (i2 - N_GROUPS).astype(F32),
                            jnp.where(lane == 2, g1, jnp.where(lane == 3, g2, 0.0))))
    route_ref[...] = r


def router(x, g, w_pad, b_pad, *, tm):
    m, d = x.shape
    return pl.pallas_call(
        _router_kernel,
        grid=(pl.cdiv(m, tm),),
        in_specs=[pl.BlockSpec((tm, d), lambda i: (i, 0)),
                  pl.BlockSpec((1, d), lambda i: (0, 0)),
                  pl.BlockSpec((d, LANES), lambda i: (0, 0)),
                  pl.BlockSpec((1, LANES), lambda i: (0, 0))],
        out_specs=[pl.BlockSpec((tm, d), lambda i: (i, 0)),
                   pl.BlockSpec((tm, LANES), lambda i: (i, 0))],
        out_shape=[jax.ShapeDtypeStruct((m, d), F32), jax.ShapeDtypeStruct((m, LANES), F32)],
        compiler_params=_cparams("parallel"),
        name="router",
    )(x, g.reshape(1, d), w_pad, b_pad)


def _moe_kernel(be_ref, nxt_ref, nused_ref, tok_ref, row_ref, hf_hbm, wg_hbm, wu_hbm, wd_hbm, y_hbm, xbuf, obuf,
                wg_f, wu_f, wd_f, wg_s, wu_s, wd_s, sem_in, sem_out, sem_w):
    b = pl.program_id(0)
    nb = pl.num_programs(0)
    blk = xbuf.shape[1]
    slot = b % 2
    used = b < nused_ref[0]
    next_used = b + 1 < nused_ref[0]

    def weight_copies(e):
        return (pltpu.make_async_copy(wg_hbm.at[e], wg_f, sem_w.at[0]),
                pltpu.make_async_copy(wu_hbm.at[e], wu_f, sem_w.at[1]),
                pltpu.make_async_copy(wd_hbm.at[e], wd_f, sem_w.at[2]))

    def gather_start(bb, sl):
        for r in range(blk):
            pltpu.make_async_copy(hf_hbm.at[pl.ds(tok_ref[bb * blk + r], 1)], xbuf.at[sl, pl.ds(r, 1)],
                                  sem_in.at[sl]).start()

    def scatter_start(bb, sl):
        for r in range(blk):
            pltpu.make_async_copy(obuf.at[sl, pl.ds(r, 1)], y_hbm.at[pl.ds(row_ref[bb * blk + r], 1)],
                                  sem_out.at[sl]).start()

    def gather_wait(sl):
        pltpu.make_async_copy(hf_hbm.at[pl.ds(0, blk)], xbuf.at[sl], sem_in.at[sl]).wait()

    def scatter_wait(sl):
        pltpu.make_async_copy(obuf.at[sl], y_hbm.at[pl.ds(0, blk)], sem_out.at[sl]).wait()

    @pl.when(b == 0)
    def _():
        for cp in weight_copies(be_ref[0]):
            cp.start()
        gather_start(0, 0)
        obuf[...] = jnp.zeros(obuf.shape, F32)

    changed = jnp.logical_and(used, jnp.logical_or(b == 0, be_ref[b] != be_ref[jnp.maximum(b - 1, 0)]))

    @pl.when(changed)
    def _():
        for cp in weight_copies(be_ref[b]):
            cp.wait()
        wg_s[...] = wg_f[...].astype(BF16)
        wu_s[...] = wu_f[...].astype(BF16)
        wd_s[...] = wd_f[...].astype(BF16)

    @pl.when(jnp.logical_and(changed, nxt_ref[b] >= 0))
    def _():
        for cp in weight_copies(nxt_ref[b]):
            cp.start()

    @pl.when(used)
    def _():
        gather_wait(slot)

    @pl.when(b >= 2)
    def _():
        scatter_wait(slot)

    def step(prefetch, flush_prev, compute):
        if prefetch:
            gather_start(b + 1, 1 - slot)
        if flush_prev:
            scatter_start(b - 1, 1 - slot)
        if compute:
            x = xbuf[slot].astype(BF16)
            gg = jnp.dot(x, wg_s[...], preferred_element_type=F32)
            uu = jnp.dot(x, wu_s[...], preferred_element_type=F32)
            a = (_silu(gg) * uu).astype(BF16)
            obuf[slot] = jnp.dot(a, wd_s[...], preferred_element_type=F32)

    first, last = b == 0, b == nb - 1
    land, lnot = jnp.logical_and, jnp.logical_not
    pl.when(land(first, next_used))(lambda: step(True, False, True))
    pl.when(land(first, lnot(next_used)))(lambda: step(False, False, True))
    pl.when(land(lnot(first), land(used, next_used)))(lambda: step(True, True, True))
    pl.when(land(lnot(first), land(used, lnot(next_used))))(lambda: step(False, True, True))
    pl.when(land(lnot(first), lnot(used)))(lambda: step(False, True, False))

    @pl.when(last)
    def _():
        scatter_start(b, slot)
        scatter_wait(slot)

    @pl.when(jnp.logical_and(last, jnp.logical_not(first)))
    def _():
        scatter_wait(1 - slot)


def moe_experts(block_e, next_e, n_used, slot_tok, slot_row, hf, w_g, w_u, w_d, *, blk, out_rows):
    n_blocks = block_e.shape[0]
    d = hf.shape[1]
    ff = w_g.shape[2]
    gs = pltpu.PrefetchScalarGridSpec(
        num_scalar_prefetch=5,
        grid=(n_blocks,),
        in_specs=[pl.BlockSpec(memory_space=pl.ANY)] * 4,
        out_specs=pl.BlockSpec(memory_space=pl.ANY),
        scratch_shapes=[pltpu.VMEM((2, blk, d), F32), pltpu.VMEM((2, blk, d), F32),
                        pltpu.VMEM((d, ff), F32), pltpu.VMEM((d, ff), F32), pltpu.VMEM((ff, d), F32),
                        pltpu.VMEM((d, ff), BF16), pltpu.VMEM((d, ff), BF16), pltpu.VMEM((ff, d), BF16),
                        pltpu.SemaphoreType.DMA((2,)), pltpu.SemaphoreType.DMA((2,)),
                        pltpu.SemaphoreType.DMA((3,))],
    )
    return pl.pallas_call(
        _moe_kernel, grid_spec=gs,
        out_shape=jax.ShapeDtypeStruct((out_rows, d), F32),
        compiler_params=_cparams("arbitrary"),
        name="moe_experts",
    )(block_e, next_e, n_used, slot_tok, slot_row, hf, w_g, w_u, w_d)


def _combine_kernel(x_ref, route_ref, gf_ref, y1_ref, y2_ref, op_ref, os_ref):
    i = pl.program_id(0)
    route = route_ref[...]
    x = x_ref[...] + route[:, 2:3] * y1_ref[...] + route[:, 3:4] * y2_ref[...]
    ms = jnp.mean(x * x, axis=-1, keepdims=True)
    out = x * lax.rsqrt(ms + EPS) * gf_ref[...]

    @pl.when(i < pl.num_programs(0) - 1)
    def _():
        op_ref[...] = out

    @pl.when(i == pl.num_programs(0) - 1)
    def _():
        os_ref[...] = out[:os_ref.shape[0]]


def combine(x, route, gf, y, plane, n_prompt, *, tm):
    m, d = x.shape
    n_tiles = n_prompt // tm
    assert n_prompt % tm == 0 and 0 < m - n_prompt <= tm
    return pl.pallas_call(
        _combine_kernel,
        grid=(n_tiles + 1,),
        in_specs=[pl.BlockSpec((tm, d), lambda i: (i, 0)),
                  pl.BlockSpec((tm, LANES), lambda i: (i, 0)),
                  pl.BlockSpec((1, d), lambda i: (0, 0)),
                  pl.BlockSpec((tm, d), lambda i: (i, 0)),
                  pl.BlockSpec((tm, d), lambda i: (plane // tm + i, 0))],
        out_specs=[pl.BlockSpec((tm, d), lambda i: (jnp.minimum(i, n_tiles - 1), 0)),
                   pl.BlockSpec((m - n_prompt, d), lambda i: (0, 0))],
        out_shape=[jax.ShapeDtypeStruct((n_prompt, d), F32), jax.ShapeDtypeStruct((m - n_prompt, d), F32)],
        compiler_params=_cparams("arbitrary"),
        name="combine",
    )(x, route, gf.reshape(1, d), y, y)


def _t5_bucket(dist):
    dist = jnp.asarray(dist, jnp.int32)
    max_exact = REL_BUCKETS // 2
    dist_f = jnp.maximum(dist, 1).astype(F32)
    large = max_exact + (jnp.log(dist_f / max_exact) / math.log(REL_MAX_DIST / max_exact)
                         * (REL_BUCKETS - max_exact)).astype(jnp.int32)
    large = jnp.minimum(large, REL_BUCKETS - 1)
    return jnp.where(dist < max_exact, dist, large)


def _bias_tables(rel_bias, past):
    r = np.arange(LANES)
    diff = r[:, None] - r[None, :]
    buckets = jnp.stack([_t5_bucket(np.maximum(diff, 0)),
                         _t5_bucket(np.maximum(diff + LANES, 0)),
                         _t5_bucket(np.full((LANES, LANES), 2 * LANES))])
    def lookup(bkt):
        oh = (bkt.reshape(-1, 1) == jnp.arange(REL_BUCKETS)[None, :]).astype(F32)
        out = jnp.dot(oh, rel_bias.astype(F32), precision=lax.Precision.HIGHEST)
        return out.T.reshape((rel_bias.shape[1],) + bkt.shape)

    tiles = lookup(buckets)
    tiles = tiles - tiles[:, 2:3]
    dist = np.maximum(past - np.arange(past + LANES), 0)
    rows = lookup(_t5_bucket(dist))
    rows = rows.reshape(B_KV_HEADS, B_GROUP, past + LANES)
    rows = jnp.concatenate([rows, jnp.zeros_like(rows)], axis=1)
    return tiles, rows


def _dispatch(eid, n_tokens, plane, blk):
    a = eid.shape[0]
    assert a == EXPERT_TOPK * n_tokens
    n_blocks = -(-(a + N_EXPERTS * (blk - 1)) // blk)
    rows = n_blocks * blk
    gap = plane - n_tokens
    assert rows - a >= EXPERT_TOPK * gap
    ar = jnp.arange(a, dtype=jnp.int32)
    out_row = (ar % EXPERT_TOPK) * plane + ar // EXPERT_TOPK
    order = jnp.argsort(eid)
    e_sorted = eid[order]
    counts = jnp.bincount(eid, length=N_EXPERTS)
    starts = jnp.cumsum(counts) - counts
    padded = (counts + blk - 1) // blk * blk
    pad_end = jnp.cumsum(padded)
    pad_start = pad_end - padded
    dest_sorted = (pad_start[e_sorted] + ar - starts[e_sorted]).astype(jnp.int32)
    block_e = jnp.minimum(jnp.sum(pad_end[None, :] <= (jnp.arange(n_blocks) * blk)[:, None], axis=1), N_EXPERTS - 1)
    e_slot = jnp.repeat(block_e, blk)
    sl = jnp.arange(rows, dtype=jnp.int32)
    is_real = sl - pad_start[e_slot] < counts[e_slot]
    j = sl - jnp.cumsum(counts)[e_slot]
    pad_row = jnp.where(j < gap, n_tokens + j,
                        jnp.where(j < 2 * gap, plane + n_tokens + j - gap, EXPERT_TOPK * plane + j - 2 * gap))
    slot_row = pad_row.astype(jnp.int32).at[dest_sorted].set(out_row[order])
    slot_tok = jnp.where(is_real, slot_row % plane, 0)
    n_used = (pad_end[-1] // blk).astype(jnp.int32)
    blocks = jnp.arange(n_blocks)
    new_run = jnp.concatenate([jnp.array([True]), block_e[1:] != block_e[:-1]])
    run_start = jnp.where(new_run & (blocks < n_used), blocks, n_blocks)
    nxt = jnp.concatenate([jnp.flip(lax.cummin(jnp.flip(run_start)))[1:], jnp.array([n_blocks])])
    next_e = jnp.where(nxt < n_blocks, block_e[jnp.minimum(nxt, n_blocks - 1)], -1)
    return (slot_tok.astype(jnp.int32), slot_row, block_e.astype(jnp.int32), next_e.astype(jnp.int32),
            n_used.reshape(1), rows)


def kernel(x_prompt, x_sample, mem_prompt, cache_k, cache_v, cache_kidx, page_table, state_hgrn, cache_mem_k,
           cache_mem_v, norm_mix, w_in, hgrn_lb_logits, hgrn_norm, w_branch_a, w_branch_b, w_out, norm_cross, w_xq,
           w_xk, w_xv, w_xo, norm_ffn, w_router_group, b_router_group, w_router_expert, b_router_expert, w_exp_gate,
           w_exp_up, w_exp_down, rel_bias, norm_final):
    assert w_in.shape[0] == 1, "single-layer step"
    l = 0
    drop0 = lambda a: a.reshape(a.shape[1:])
    bp, t, d = x_prompt.shape
    db = x_sample.shape[0]
    past = page_table.shape[1] * PAGE_SIZE
    xp = x_prompt.reshape(bp * t, d)
    xs = x_sample.reshape(db, d)

    wi_t = jnp.transpose(drop0(w_in)).astype(BF16)
    assert wi_t.shape[0] - NZ_MAIN + T_PAD == NZ_TAIL
    w_tail_t = jnp.pad(wi_t[NZ_MAIN:], ((T_PAD, 0), (0, 0)))
    wa, wb, wo = w_branch_a[l].astype(BF16), w_branch_b[l].astype(BF16), w_out[l].astype(BF16)
    wxq, wxk, wxv, wxo = (w_xq[l].astype(BF16), w_xk[l].astype(BF16), w_xv[l].astype(BF16), w_xo[l].astype(BF16))
    w_route = jnp.pad(jnp.concatenate([w_router_group[l], w_router_expert[l]], axis=1),
                      ((0, 0), (0, LANES - N_GROUPS - N_EXPERTS))).astype(BF16)
    b_route = jnp.pad(jnp.concatenate([b_router_group[l], b_router_expert[l]]),
                      (0, LANES - N_GROUPS - N_EXPERTS)).reshape(1, LANES)
    bias_tiles, bias_rows = _bias_tables(rel_bias, past)

    zp, zpt = in_proj(xp, norm_mix[l], wi_t, NZ_MAIN, w_tail_t, tm=1024, tn=512)
    kp = zp[:, C_BK:C_BK + 256]
    vp = zp[:, C_BV:C_BV + 256]
    ikp = zpt[:, T_SM + IK_LANE:T_SM + IK_LANE + IDX_DIM]
    ya_p, st_p = hgrn_prompt(zp, hgrn_lb_logits, hgrn_norm[l])
    yb_p = dsa_prompt(zp, zpt, ikp.astype(BF16), kp.astype(BF16), vp.astype(BF16), bias_tiles)
    mg_p = merge(ya_p, yb_p, zpt, T_GA, T_GB, wa, wb, tm=512, tn=512)
    x1p = matmul(mg_p, wo, xp, tm=512, tn=512)
    memp = mem_prompt.reshape(-1, d)
    mk = matmul(memp, wxk, tm=memp.shape[0], tn=512)
    mv = matmul(memp, wxv, tm=memp.shape[0], tn=512)
    x2p = cross_prompt(x1p, norm_cross[l], wxq, mk.astype(BF16), mv.astype(BF16), wxo, tm=512)

    zs, zst = in_proj(xs, norm_mix[l], wi_t, NZ_MAIN, w_tail_t, tm=db, tn=512)
    ks = zs[:, C_BK:C_BK + 256]
    vs = zs[:, C_BV:C_BV + 256]
    iks = zst[:, T_SM + IK_LANE:T_SM + IK_LANE + IDX_DIM]
    ya_s, st_s = hgrn_step(zs[:, :4 * A_HEADS * A_DK], hgrn_lb_logits, hgrn_norm[l], drop0(state_hgrn))
    iq_s = zs[:, C_IQ:C_IQ + IDX_HEADS * IDX_DIM].reshape(db, IDX_HEADS, IDX_DIM)
    iw_s = zst[:, T_SM + IW_LANE:T_SM + IW_LANE + IDX_HEADS].reshape(db, IDX_HEADS, 1)
    iknew_pad = jnp.pad(iks[:, None, :], ((0, 0), (0, LANES - 1), (0, 0)))
    scores = dsa_scores(page_table, iq_s, iw_s, iknew_pad, drop0(cache_kidx)).reshape(db, past + LANES)
    topk_s = min(TOPK_MAX, (past + 1) // 4)
    thr = dsa_threshold(scores, topk_s)
    q8 = jnp.pad(zs[:, C_BQ:C_BQ + B_HEADS * B_HEAD_DIM].reshape(db, B_KV_HEADS, B_GROUP, B_HEAD_DIM),
                 ((0, 0), (0, 0), (0, 8 - B_GROUP), (0, 0)))
    knew8 = jnp.pad(ks.reshape(db, B_KV_HEADS, B_HEAD_DIM), ((0, 0), (0, 8 - B_KV_HEADS), (0, 0)))
    vnew8 = jnp.pad(vs.reshape(db, B_KV_HEADS, B_HEAD_DIM), ((0, 0), (0, 8 - B_KV_HEADS), (0, 0)))
    n_pool = cache_k.shape[1]
    ob = dsa_decode(page_table, q8, scores.reshape(db, 1, -1), thr.reshape(db, 1, LANES), knew8, vnew8, bias_rows,
                    cache_k.reshape(n_pool, PAGE_SIZE * B_KV_HEADS, B_HEAD_DIM),
                    cache_v.reshape(n_pool, PAGE_SIZE * B_KV_HEADS, B_HEAD_DIM))
    yb_s = ob.reshape(db, B_KV_HEADS, 8, B_HEAD_DIM)[:, :, :B_GROUP].reshape(db, B_HEADS * B_HEAD_DIM)
    mg_s = merge(ya_s.reshape(db, -1), yb_s, zst, T_GA, T_GB, wa, wb, tm=db, tn=512)
    x1s = matmul(mg_s, wo, xs, tm=db, tn=512)
    qx_s = norm_matmul(x1s, norm_cross[l], wxq, tm=db, tn=512)
    mem = cache_mem_k.shape[2]
    att_s = cross_step(qx_s, cache_mem_k.reshape(db, mem, -1), cache_mem_v.reshape(db, mem, -1))
    x2s = matmul(att_s.reshape(db, -1), wxo, x1s, tm=db, tn=512)

    x2 = jnp.concatenate([x2p, x2s], axis=0)
    n = x2.shape[0]
    hf, route = router(x2, norm_ffn[l], w_route, b_route, tm=256)
    eid = route[:, :EXPERT_TOPK].astype(jnp.int32).reshape(-1)
    tmc = 128
    plane = -(-n // tmc) * tmc
    slot_tok, slot_row, block_e, next_e, n_used, rows = _dispatch(eid, n, plane, MOE_ROWS)
    ye = moe_experts(block_e, next_e, n_used, slot_tok, slot_row, hf, drop0(w_exp_gate), drop0(w_exp_up),
                     drop0(w_exp_down), blk=MOE_ROWS, out_rows=rows)
    y_p, y_s = combine(x2, route, norm_final, ye, plane, bp * t, tm=tmc)

    y_prompt = y_p.reshape(bp, t, d)
    y_sample = y_s.reshape(db, 1, d)
    return (y_prompt, y_sample,
            kp.reshape(1, bp, t, B_KV_HEADS, B_HEAD_DIM), vp.reshape(1, bp, t, B_KV_HEADS, B_HEAD_DIM),
            ikp.reshape(1, bp, t, IDX_DIM),
            jnp.swapaxes(st_p, 1, 2).reshape(1, bp, A_HEADS, A_DK, A_DV),
            mk.reshape(1, bp, -1, X_HEADS, X_HEAD_DIM), mv.reshape(1, bp, -1, X_HEADS, X_HEAD_DIM),
            ks.reshape(1, db, 1, B_KV_HEADS, B_HEAD_DIM), vs.reshape(1, db, 1, B_KV_HEADS, B_HEAD_DIM),
            iks.reshape(1, db, 1, IDX_DIM),
            st_s.reshape(1, db, A_HEADS, A_DK, A_DV))
```

```python
import functools
import math

import jax
import jax.numpy as jnp
import numpy as np
from jax import lax
from jax.experimental import pallas as pl
from jax.experimental.pallas import tpu as pltpu

F32 = jnp.float32
BF16 = jnp.bfloat16
EPS = 1e-6

D_MODEL = 2048
A_HEADS, A_DK, A_DV = 8, 128, 128
B_HEADS, B_KV_HEADS, B_HEAD_DIM = 8, 2, 128
B_GROUP = B_HEADS // B_KV_HEADS
IDX_HEADS, IDX_DIM = 16, 64
TOPK_MAX = 256
PAGE_SIZE = 128
REL_BUCKETS, REL_MAX_DIST = 32, 128
X_HEADS, X_HEAD_DIM = 4, 128
N_GROUPS, EXP_PER_GROUP = 4, 8
N_EXPERTS = N_GROUPS * EXP_PER_GROUP
EXPERT_TOPK = 2
EXPERT_FF = 512
MOE_ROWS = 256

LANES = 128
VMEM_LIMIT = 56 * 1024 * 1024

NEG = -1e30

C_AQ, C_AF, C_AI, C_AG, C_BQ, C_BK, C_BV, C_IQ = 0, 1024, 2048, 3072, 4096, 5120, 5376, 5632
NZ_MAIN = 6656
T_PAD = 432
T_SM, IW_LANE, IK_LANE = 384, 48, 64
T_GA, T_GB = 512, 2560
NZ_TAIL = 4608


def _cparams(*sem):
    return pltpu.CompilerParams(dimension_semantics=sem, vmem_limit_bytes=VMEM_LIMIT)


def _silu(x):
    return x * jax.nn.sigmoid(x)


def _nt(a, b):
    return lax.dot_general(a, b, (((1,), (1,)), ((), ())), preferred_element_type=F32)


def _nt_f32(a, b):
    return lax.dot_general(a, b, (((1,), (1,)), ((), ())), preferred_element_type=F32,
                           precision=lax.Precision.HIGHEST)


def _mm(a, w):
    return jnp.dot(a.astype(BF16), w.astype(BF16), preferred_element_type=F32)


def _norm_matmul_kernel(x_ref, g_ref, w_ref, o_ref, h_ref):
    @pl.when(pl.program_id(1) == 0)
    def _():
        x = x_ref[...]
        ms = jnp.mean(x * x, axis=-1, keepdims=True)
        h_ref[...] = (x * lax.rsqrt(ms + EPS) * g_ref[...]).astype(h_ref.dtype)

    o_ref[...] = _mm(h_ref[...], w_ref[...])


def norm_matmul(x, g, w, *, tm, tn):
    m, k = x.shape
    n = w.shape[1]
    assert n % tn == 0
    return pl.pallas_call(
        _norm_matmul_kernel,
        grid=(m // tm, n // tn),
        in_specs=[pl.BlockSpec((tm, k), lambda i, j: (i, 0)),
                  pl.BlockSpec((1, k), lambda i, j: (0, 0)),
                  pl.BlockSpec((k, tn), lambda i, j: (0, j))],
        out_specs=pl.BlockSpec((tm, tn), lambda i, j: (i, j)),
        out_shape=jax.ShapeDtypeStruct((m, n), F32),
        scratch_shapes=[pltpu.VMEM((tm, k), BF16)],
        compiler_params=_cparams("parallel", "arbitrary"),
        name="norm_matmul",
    )(x, g.reshape(1, k), w)


def _in_proj_kernel(x_ref, g_ref, wa_ref, wb_ref, oa_ref, ob_ref, h_ref, *, na):
    j = pl.program_id(1)

    @pl.when(j == 0)
    def _():
        x = x_ref[...]
        ms = jnp.mean(x * x, axis=-1, keepdims=True)
        h_ref[...] = (x * lax.rsqrt(ms + EPS) * g_ref[...]).astype(h_ref.dtype)

    @pl.when(j < na)
    def _():
        oa_ref[...] = _nt(h_ref[...], wa_ref[...])

    @pl.when(j >= na)
    def _():
        ob_ref[...] = _nt(h_ref[...], wb_ref[...])


def in_proj(x, g, wa_t, a_rows, wb_t, *, tm, tn):
    m, k = x.shape
    na, nb = a_rows // tn, wb_t.shape[0] // tn
    assert a_rows % tn == 0 and wb_t.shape[0] % tn == 0
    a_idx = lambda j: jnp.minimum(j, na - 1)
    b_idx = lambda j: jnp.maximum(j - na, 0)
    return pl.pallas_call(
        functools.partial(_in_proj_kernel, na=na),
        grid=(m // tm, na + nb),
        in_specs=[pl.BlockSpec((tm, k), lambda i, j: (i, 0)),
                  pl.BlockSpec((1, k), lambda i, j: (0, 0)),
                  pl.BlockSpec((tn, k), lambda i, j: (a_idx(j), 0)),
                  pl.BlockSpec((tn, k), lambda i, j: (b_idx(j), 0))],
        out_specs=[pl.BlockSpec((tm, tn), lambda i, j: (i, a_idx(j))),
                   pl.BlockSpec((tm, tn), lambda i, j: (i, b_idx(j)))],
        out_shape=[jax.ShapeDtypeStruct((m, na * tn), F32), jax.ShapeDtypeStruct((m, nb * tn), F32)],
        scratch_shapes=[pltpu.VMEM((tm, k), BF16)],
        compiler_params=_cparams("parallel", "arbitrary"),
        name="in_proj",
    )(x, g.reshape(1, k), wa_t, wb_t)


def _matmul_res_kernel(x_ref, w_ref, r_ref, o_ref):
    o_ref[...] = r_ref[...] + _mm(x_ref[...], w_ref[...])


def _matmul_kernel(x_ref, w_ref, o_ref):
    o_ref[...] = _mm(x_ref[...], w_ref[...])


def matmul(x, w, res=None, *, tm, tn):
    m, k = x.shape
    n = w.shape[1]
    in_specs = [pl.BlockSpec((tm, k), lambda i, j: (i, 0)),
                pl.BlockSpec((k, tn), lambda i, j: (0, j))]
    args = [x, w]
    kern = _matmul_kernel
    if res is not None:
        in_specs.append(pl.BlockSpec((tm, tn), lambda i, j: (i, j)))
        args.append(res)
        kern = _matmul_res_kernel
    return pl.pallas_call(
        kern,
        grid=(m // tm, n // tn),
        in_specs=in_specs,
        out_specs=pl.BlockSpec((tm, tn), lambda i, j: (i, j)),
        out_shape=jax.ShapeDtypeStruct((m, n), F32),
        compiler_params=_cparams("parallel", "arbitrary"),
        name="matmul",
    )(*args)


HG_TB = 128
HG_C = 16
HG_H = HG_C // 2


def _hgrn_prompt_kernel(aq_ref, af_ref, ai_ref, ag_ref, lbl_ref, ng_ref, ya_ref, st_out_ref,
                        st_ref, q_s, k_s, g_s, v_s):
    t = pl.program_id(0)

    @pl.when(t == 0)
    def _():
        st_ref[...] = jnp.zeros_like(st_ref)

    lbl = lbl_ref[...]
    mx = jnp.max(lbl, axis=0, keepdims=True)
    ex = jnp.exp(lbl - mx)
    lb = ex[0:1, :] / jnp.sum(ex, axis=0, keepdims=True)

    f = lb + (1.0 - lb) * jax.nn.sigmoid(af_ref[...])
    logf = jnp.log(f)
    row = lax.broadcasted_iota(jnp.int32, (HG_TB, HG_TB), 0)
    col = lax.broadcasted_iota(jnp.int32, (HG_TB, HG_TB), 1)
    tri = jnp.where((row // HG_C == col // HG_C) & (col <= row), 1.0, 0.0).astype(F32)
    g_s[...] = jnp.dot(tri, logf, preferred_element_type=F32, precision=lax.Precision.HIGHEST)
    q_s[...] = _silu(aq_ref[...])
    k_s[...] = 1.0 - f
    v_s[...] = ai_ref[...]

    sub = lax.broadcasted_iota(jnp.int32, (HG_C, A_DK), 0)
    sub8 = lax.broadcasted_iota(jnp.int32, (HG_H, A_DK), 0)
    ng = ng_ref[...]

    def chunk(c, carry):
        r0 = pl.multiple_of(c * HG_C, HG_C)
        rows = pl.ds(r0, HG_C)
        for h in range(A_HEADS):
            cols = slice(h * A_DK, (h + 1) * A_DK)
            g = g_s[rows, cols]
            qh = q_s[rows, cols]
            kh = k_s[rows, cols]
            vh = v_s[rows, cols]
            halves = []
            for hb in range(2):
                rs = slice(hb * HG_H, (hb + 1) * HG_H)
                gb, qb, kb, vb = g[rs], qh[rs], kh[rs], vh[rs]
                ob = jnp.zeros((HG_H, A_DV), F32)
                for tt in range(HG_H):
                    d = gb[tt:tt + 1, :] - gb
                    e = jnp.exp(jnp.where(sub8 <= tt, d, -jnp.inf))
                    p = e * (qb[tt:tt + 1, :] * kb)
                    a_col = jnp.sum(p, axis=1, keepdims=True)
                    o_row = jnp.sum(a_col * vb, axis=0, keepdims=True)
                    ob = jnp.where(sub8 == tt, o_row, ob)
                halves.append(ob)
            o = jnp.concatenate(halves, axis=0)
            low = sub < HG_H
            g_mid = g[HG_H - 1:HG_H, :]
            q_hi = jnp.where(low, 0.0, qh * jnp.exp(jnp.minimum(g - g_mid, 0.0)))
            k_lo = jnp.where(low, kh * jnp.exp(jnp.minimum(g_mid - g, 0.0)), 0.0)
            st = st_ref[h]
            g_last = g[HG_C - 1:HG_C, :]
            kt = kh * jnp.exp(g_last - g)
            upd = lax.dot_general(vh.astype(BF16), jnp.concatenate([kt, k_lo], axis=1).astype(BF16),
                                  (((0,), (0,)), ((), ())), preferred_element_type=F32)
            lhs = jnp.concatenate([qh * jnp.exp(g), q_hi], axis=1).astype(BF16)
            rhs = jnp.concatenate([st, upd[:, A_DK:]], axis=1).astype(BF16)
            o = o + _nt(lhs, rhs)
            st_ref[h] = st * jnp.exp(g_last) + upd[:, :A_DK]
            on = o * lax.rsqrt(jnp.mean(o * o, axis=-1, keepdims=True) + EPS) * ng
            ya_ref[rows, cols] = (on * _silu(ag_ref[rows, cols])).astype(ya_ref.dtype)
        return carry

    lax.fori_loop(0, HG_TB // HG_C, chunk, 0)

    @pl.when(t == pl.num_programs(0) - 1)
    def _():
        st_out_ref[...] = st_ref[...]


def hgrn_prompt(z, lb_logits, norm_g):
    m = z.shape[0]
    w = A_HEADS * A_DK

    def zspec(cb):
        return pl.BlockSpec((HG_TB, w), lambda t, cb=cb: (t, cb))

    return pl.pallas_call(
        _hgrn_prompt_kernel,
        grid=(m // HG_TB,),
        in_specs=[zspec(C_AQ // w), zspec(C_AF // w), zspec(C_AI // w), zspec(C_AG // w),
                  pl.BlockSpec(lb_logits.shape, lambda t: (0, 0)),
                  pl.BlockSpec((1, A_DV), lambda t: (0, 0))],
        out_specs=[pl.BlockSpec((HG_TB, w), lambda t: (t, 0)),
                   pl.BlockSpec((A_HEADS, A_DV, A_DK), lambda t: (0, 0, 0))],
        out_shape=[jax.ShapeDtypeStruct((m, w), BF16),
                   jax.ShapeDtypeStruct((A_HEADS, A_DV, A_DK), F32)],
        scratch_shapes=[pltpu.VMEM((A_HEADS, A_DV, A_DK), F32)] + [pltpu.VMEM((HG_TB, w), F32)] * 4,
        compiler_params=_cparams("arbitrary"),
        name="hgrn_prompt",
    )(z, z, z, z, lb_logits, norm_g.reshape(1, A_DV))


def _hgrn_step_kernel(z_ref, lbl_ref, ng_ref, s_ref, ya_ref, s_out_ref):
    lbl = lbl_ref[...]
    mx = jnp.max(lbl, axis=0, keepdims=True)
    ex = jnp.exp(lbl - mx)
    lb = ex[0:1, :] / jnp.sum(ex, axis=0, keepdims=True)
    z = z_ref[0]
    w = A_HEADS * A_DK
    q = _silu(z[:, 0:w])
    f = lb + (1.0 - lb) * jax.nn.sigmoid(z[:, w:2 * w])
    kk = 1.0 - f
    v = z[:, 2 * w:3 * w]
    ag = z[:, 3 * w:4 * w]
    rows = []
    for h in range(A_HEADS):
        cols = slice(h * A_DK, (h + 1) * A_DK)
        rows += [f[:, cols], kk[:, cols], q[:, cols]]
    rows.append(jnp.zeros((LANES - 3 * A_HEADS, A_DK), F32))
    xt = jnp.concatenate(rows, axis=0).T
    ng = ng_ref[...]
    r16 = lambda a: a.astype(BF16).astype(F32)
    outs = []
    for h in range(A_HEADS):
        cols = slice(h * A_DV, (h + 1) * A_DV)
        fcol = xt[:, 3 * h:3 * h + 1]
        kcol = xt[:, 3 * h + 1:3 * h + 2]
        qcol = xt[:, 3 * h + 2:3 * h + 3]
        s_old = s_ref[0, h]
        s_out_ref[0, h] = fcol * s_old + kcol * v[:, cols]
        o = (jnp.sum(r16(qcol * fcol) * r16(s_old), axis=0, keepdims=True)
             + jnp.sum(qcol * kcol, axis=0, keepdims=True) * v[:, cols])
        on = o * lax.rsqrt(jnp.mean(o * o, axis=-1, keepdims=True) + EPS) * ng
        outs.append(on * _silu(ag[:, cols]))
    ya_ref[0] = jnp.concatenate(outs, axis=1).astype(ya_ref.dtype)


def hgrn_step(z4, lb_logits, norm_g, state):
    b = z4.shape[0]
    w = A_HEADS * A_DK
    return pl.pallas_call(
        _hgrn_step_kernel,
        grid=(b,),
        in_specs=[pl.BlockSpec((1, 1, 4 * w), lambda i: (i, 0, 0)),
                  pl.BlockSpec(lb_logits.shape, lambda i: (0, 0)),
                  pl.BlockSpec((1, A_DV), lambda i: (0, 0)),
                  pl.BlockSpec((1, A_HEADS, A_DK, A_DV), lambda i: (i, 0, 0, 0))],
        out_specs=[pl.BlockSpec((1, 1, w), lambda i: (i, 0, 0)),
                   pl.BlockSpec((1, A_HEADS, A_DK, A_DV), lambda i: (i, 0, 0, 0))],
        out_shape=[jax.ShapeDtypeStruct((b, 1, w), F32),
                   jax.ShapeDtypeStruct(state.shape, F32)],
        compiler_params=_cparams("arbitrary"),
        name="hgrn_step",
    )(z4.reshape(b, 1, 4 * w), lb_logits, norm_g.reshape(1, A_DV), state)


BISECT_MAX_ITERS = 48


def _bisect_threshold(count_ge, lo, hi, cnt_lo, topk):
    kf = float(topk)

    def cond(c):
        return jnp.logical_and(c[0] < BISECT_MAX_ITERS, c[-1] > 0.0)

    def body(c):
        it, lo, hi, cl, ch, _ = c
        mid = 0.5 * lo + 0.5 * hi
        t_int = hi - (hi - lo) * ((kf - ch) / jnp.maximum(cl - ch, 1.0))
        ok = jnp.logical_and(it % 2 == 0, jnp.logical_and(t_int > lo, t_int < hi))
        t = jnp.where(ok, t_int, mid)
        cm = count_ge(t)
        ge = cm >= kf
        lo = jnp.where(ge, t, lo)
        cl = jnp.where(ge, cm, cl)
        hi = jnp.where(ge, hi, t)
        ch = jnp.where(ge, ch, cm)
        busy = jnp.max(jnp.where(cl > kf, 1.0, 0.0))
        return it + 1, lo, hi, cl, ch, busy

    busy0 = jnp.max(jnp.where(cnt_lo > kf, 1.0, 0.0))
    out = lax.while_loop(cond, body, (jnp.int32(0), lo, hi, cnt_lo, jnp.ones_like(cnt_lo), busy0))
    return out[1]


DSA_QB = 128
DSA_W = 512
DSA_W3 = 1024


def _dsa_prompt_kernel(iq0_ref, iq1_ref, bq_ref, iw_ref, kidx_ref, k_ref, v_ref, bias_ref, o_ref,
                       score_s, qih_s, qs_s, wb_s, m_s, l_s, acc_s, *, topk):
    i = pl.program_id(0)
    nsub = DSA_W // LANES
    nsub3 = DSA_W3 // LANES
    nch3 = (i * DSA_QB + DSA_QB + DSA_W3 - 1) // DSA_W3
    nch = nch3 * (DSA_W3 // DSA_W)
    qpos = i * DSA_QB + lax.broadcasted_iota(jnp.int32, (DSA_QB, 1), 0)

    iw = iw_ref[...]
    wscale = IDX_DIM ** -0.5 * IDX_HEADS ** -0.5
    for h in range(IDX_HEADS):
        iq_ref, hh = (iq0_ref, h) if h < IDX_HEADS // 2 else (iq1_ref, h - IDX_HEADS // 2)
        qih_s[h] = iq_ref[:, hh * IDX_DIM:(hh + 1) * IDX_DIM].astype(BF16)
        wb_s[h] = jnp.broadcast_to(iw[:, IW_LANE + h:IW_LANE + h + 1] * wscale, (DSA_QB, LANES))
    for h in range(B_HEADS):
        qs_s[h // B_GROUP, (h % B_GROUP) * DSA_QB:(h % B_GROUP + 1) * DSA_QB, :] = (
            bq_ref[:, h * B_HEAD_DIM:(h + 1) * B_HEAD_DIM] * B_HEAD_DIM ** -0.5).astype(BF16)

    def p1(c, carry):
        c0 = pl.multiple_of(c * DSA_W, DSA_W)
        kc = kidx_ref[pl.ds(c0, DSA_W), :]
        sc = [jnp.zeros((DSA_QB, LANES), F32) for _ in range(nsub)]
        for h in range(IDX_HEADS):
            s = jnp.maximum(_nt(qih_s[h], kc), 0.0)
            wb = wb_s[h]
            for j in range(nsub):
                sc[j] = sc[j] + s[:, j * LANES:(j + 1) * LANES] * wb
        for j in range(nsub):
            kpos = c0 + j * LANES + lax.broadcasted_iota(jnp.int32, (1, LANES), 1)
            score_s[c * nsub + j] = jnp.where(kpos <= qpos, sc[j], -jnp.inf)
        return carry

    lax.fori_loop(0, nch, p1, 0)

    def stats(c, carry):
        mn, mx = carry
        for j in range(nsub):
            s = score_s[c * nsub + j]
            mx = jnp.maximum(mx, s)
            mn = jnp.minimum(mn, jnp.where(s > -jnp.inf, s, jnp.inf))
        return mn, mx

    mn, mx = lax.fori_loop(0, nch, stats, (jnp.full((DSA_QB, LANES), jnp.inf, F32),
                                           jnp.full((DSA_QB, LANES), -jnp.inf, F32)))
    lo0 = jnp.min(mn, axis=1, keepdims=True)
    hi0 = jnp.max(mx, axis=1, keepdims=True)

    def count_ge(thr):
        thr_b = jnp.broadcast_to(thr, (DSA_QB, LANES))

        def body(c, acc):
            for j in range(nsub):
                s = score_s[c * nsub + j]
                acc = acc + jnp.where(s >= thr_b, 1.0, 0.0)
            return acc

        acc = lax.fori_loop(0, nch, body, jnp.zeros((DSA_QB, LANES), F32))
        return jnp.sum(acc, axis=1, keepdims=True)

    thr = _bisect_threshold(count_ge, lo0, hi0, (qpos + 1).astype(F32), topk)
    thr_b = jnp.broadcast_to(thr, (DSA_QB, LANES))

    m_s[...] = jnp.full(m_s.shape, NEG, F32)
    l_s[...] = jnp.zeros(l_s.shape, F32)
    acc_s[...] = jnp.zeros(acc_s.shape, F32)

    def p3(c, with_bias):
        c0 = pl.multiple_of(c * DSA_W3, DSA_W3)
        madd = jnp.concatenate([jnp.where(score_s[c * nsub3 + j] >= thr_b, 0.0, NEG) for j in range(nsub3)], axis=1)
        kc = k_ref[pl.ds(c0, DSA_W3), :]
        vc = v_ref[pl.ds(c0, DSA_W3), :]
        rel = [i - (c * nsub3 + j) for j in range(nsub3)]

        def scores(n):
            return _nt(qs_s[n], kc[:, n * B_HEAD_DIM:(n + 1) * B_HEAD_DIM])

        def softmax(n, lg):
            lg = lg.reshape(B_GROUP, DSA_QB, DSA_W3) + madd[None]
            if with_bias:
                lg = lg + jnp.stack([jnp.concatenate(
                    [jnp.where(rel[j] == 0, bias_ref[n * B_GROUP + gq, 0],
                               jnp.where(rel[j] == 1, bias_ref[n * B_GROUP + gq, 1], 0.0)) for j in range(nsub3)],
                    axis=1) for gq in range(B_GROUP)])
            m_old = m_s[n]
            m_new = jnp.maximum(m_old, jnp.max(lg, axis=-1, keepdims=True))
            p = jnp.exp(lg - m_new)
            alpha = jnp.exp(m_old - m_new)
            l_s[n] = alpha * l_s[n] + jnp.sum(p, axis=-1, keepdims=True)
            m_s[n] = m_new
            pv = jnp.dot(p.reshape(B_GROUP * DSA_QB, DSA_W3).astype(BF16), vc[:, n * B_HEAD_DIM:(n + 1) * B_HEAD_DIM],
                         preferred_element_type=F32)
            return alpha, pv.reshape(B_GROUP, DSA_QB, B_HEAD_DIM)

        lgs = [scores(n) for n in range(B_KV_HEADS)]
        outs = [softmax(n, lgs[n]) for n in range(B_KV_HEADS)]
        for n in range(B_KV_HEADS):
            acc_s[n] = outs[n][0] * acc_s[n] + outs[n][1]

    n_far = jnp.maximum(i - 1, 0) // nsub3
    lax.fori_loop(0, n_far, lambda c, carry: (p3(c, False), carry)[1], 0)
    lax.fori_loop(n_far, nch3, lambda c, carry: (p3(c, True), carry)[1], 0)

    for h in range(B_HEADS):
        n, gq = h // B_GROUP, h % B_GROUP
        o_ref[:, h * B_HEAD_DIM:(h + 1) * B_HEAD_DIM] = (acc_s[n, gq] / l_s[n, gq]).astype(o_ref.dtype)


def dsa_prompt(z, ztail, kidx_bf, k_bf, v_bf, bias_tiles):
    m = z.shape[0]
    topk = min(TOPK_MAX, m // 4)
    wq = B_HEADS * B_HEAD_DIM
    wi2 = IDX_HEADS * IDX_DIM // 2
    kern = functools.partial(_dsa_prompt_kernel, topk=topk)
    return pl.pallas_call(
        kern,
        grid=(m // DSA_QB,),
        in_specs=[pl.BlockSpec((DSA_QB, wi2), lambda i: (i, C_IQ // wi2)),
                  pl.BlockSpec((DSA_QB, wi2), lambda i: (i, C_IQ // wi2 + 1)),
                  pl.BlockSpec((DSA_QB, wq), lambda i: (i, C_BQ // wq)),
                  pl.BlockSpec((DSA_QB, LANES), lambda i: (i, T_SM // LANES)),
                  pl.BlockSpec(kidx_bf.shape, lambda i: (0, 0)),
                  pl.BlockSpec(k_bf.shape, lambda i: (0, 0)),
                  pl.BlockSpec(v_bf.shape, lambda i: (0, 0)),
                  pl.BlockSpec(bias_tiles.shape, lambda i: (0, 0, 0, 0))],
        out_specs=pl.BlockSpec((DSA_QB, wq), lambda i: (i, 0)),
        out_shape=jax.ShapeDtypeStruct((m, wq), BF16),
        scratch_shapes=[pltpu.VMEM((m // LANES, DSA_QB, LANES), F32),
                        pltpu.VMEM((IDX_HEADS, DSA_QB, IDX_DIM), BF16),
                        pltpu.VMEM((B_KV_HEADS, B_GROUP * DSA_QB, B_HEAD_DIM), BF16),
                        pltpu.VMEM((IDX_HEADS, DSA_QB, LANES), F32),
                        pltpu.VMEM((B_KV_HEADS, B_GROUP, DSA_QB, 1), F32),
                        pltpu.VMEM((B_KV_HEADS, B_GROUP, DSA_QB, 1), F32),
                        pltpu.VMEM((B_KV_HEADS, B_GROUP, DSA_QB, B_HEAD_DIM), F32)],
        compiler_params=_cparams("arbitrary"),
        name="dsa_prompt",
    )(z, z, z, ztail, kidx_bf, k_bf, v_bf, bias_tiles)


def _page_copies(table_ref, b, n_pages, src_hbm, dst, sem, rows_per_page=PAGE_SIZE):
    def copy(p):
        return pltpu.make_async_copy(src_hbm.at[table_ref[b, p]],
                                     dst.at[pl.ds(p * rows_per_page, rows_per_page)], sem)
    return copy


def _dsa_scores_kernel(pt_ref, iq_ref, iw_ref, iknew_ref, kidx_hbm, o_ref, buf, sem, *, n_pages):
    b = pl.program_id(0)
    nb = pl.num_programs(0)
    past = n_pages * PAGE_SIZE

    def page_copy(bb, slot, p):
        return pltpu.make_async_copy(kidx_hbm.at[pt_ref[bb, p]],
                                     buf.at[slot, :, pl.ds(pl.multiple_of(p * PAGE_SIZE, PAGE_SIZE), PAGE_SIZE)],
                                     sem.at[slot])

    def start(bb, slot):
        lax.fori_loop(0, n_pages, lambda p, c: (page_copy(bb, slot, p).start(), c)[1], 0)

    def wait(bb, slot):
        lax.fori_loop(0, n_pages, lambda p, c: (page_copy(bb, slot, p).wait(), c)[1], 0)

    slot = b % 2

    @pl.when(b == 0)
    def _():
        start(0, 0)

    @pl.when(b + 1 < nb)
    def _():
        start(b + 1, 1 - slot)

    wait(b, slot)

    r16 = lambda a: a.astype(BF16).astype(F32)
    qi = iq_ref[0].astype(BF16)
    wcol = r16(iw_ref[0]) * (IDX_DIM ** -0.5 * IDX_HEADS ** -0.5)
    s = r16(jnp.maximum(jnp.dot(qi, buf[slot].astype(BF16), preferred_element_type=F32), 0.0))
    o_ref[0, :, 0:past] = jnp.sum(s * wcol, axis=0, keepdims=True)
    sn = r16(jnp.maximum(jnp.dot(qi, iknew_ref[0].astype(BF16), preferred_element_type=F32), 0.0))
    sn = jnp.sum(sn * wcol, axis=0, keepdims=True)
    lane = lax.broadcasted_iota(jnp.int32, (1, LANES), 1)
    o_ref[0, :, past:past + LANES] = jnp.where(lane == 0, sn, -jnp.inf)


def dsa_scores(page_table, iq, iw, iknew_pad, cache_kidx):
    b, n_pages = page_table.shape
    past = n_pages * PAGE_SIZE
    kern = functools.partial(_dsa_scores_kernel, n_pages=n_pages)
    gs = pltpu.PrefetchScalarGridSpec(
        num_scalar_prefetch=1,
        grid=(b,),
        in_specs=[pl.BlockSpec((1, IDX_HEADS, IDX_DIM), lambda i, pt: (i, 0, 0)),
                  pl.BlockSpec((1, IDX_HEADS, 1), lambda i, pt: (i, 0, 0)),
                  pl.BlockSpec((1, IDX_DIM, LANES), lambda i, pt: (i, 0, 0)),
                  pl.BlockSpec(memory_space=pl.ANY)],
        out_specs=pl.BlockSpec((1, 1, past + LANES), lambda i, pt: (i, 0, 0)),
        scratch_shapes=[pltpu.VMEM((2, IDX_DIM, past), F32), pltpu.SemaphoreType.DMA((2,))],
    )
    return pl.pallas_call(
        kern, grid_spec=gs,
        out_shape=jax.ShapeDtypeStruct((b, 1, past + LANES), F32),
        compiler_params=_cparams("arbitrary"),
        name="dsa_scores",
    )(page_table, iq, iw, iknew_pad, cache_kidx)


def _dsa_threshold_kernel(s_ref, thr_ref, *, topk):
    s = s_ref[...]
    nb = s.shape[0]
    finite = s > -jnp.inf
    lo0 = jnp.min(jnp.where(finite, s, jnp.inf), axis=1, keepdims=True)
    hi0 = jnp.max(s, axis=1, keepdims=True)
    cnt0 = jnp.sum(jnp.where(finite, 1.0, 0.0), axis=1, keepdims=True)

    def count_ge(thr):
        return jnp.sum(jnp.where(s_ref[...] >= thr, 1.0, 0.0), axis=1, keepdims=True)

    thr = _bisect_threshold(count_ge, lo0, hi0, cnt0, topk)
    thr_ref[...] = jnp.broadcast_to(thr, (nb, LANES))


def dsa_threshold(scores, topk):
    b, l = scores.shape
    return pl.pallas_call(
        functools.partial(_dsa_threshold_kernel, topk=topk),
        grid=(1,),
        in_specs=[pl.BlockSpec((b, l), lambda i: (0, 0))],
        out_specs=pl.BlockSpec((b, LANES), lambda i: (0, 0)),
        out_shape=jax.ShapeDtypeStruct((b, LANES), F32),
        compiler_params=_cparams("arbitrary"),
        name="dsa_threshold",
    )(scores)


def _dsa_decode_kernel(pt_ref, q_ref, s_ref, thr_ref, knew_ref, vnew_ref, bias_ref, k_hbm, v_hbm, o_ref,
                       kbuf, vbuf, sem, *, n_pages):
    b = pl.program_id(0)
    nb = pl.num_programs(0)
    past = n_pages * PAGE_SIZE

    rpp = PAGE_SIZE * B_KV_HEADS

    def copies(bb, slot):
        ck = _page_copies(pt_ref, bb, n_pages, k_hbm, kbuf.at[slot], sem.at[0, slot], rpp)
        cv = _page_copies(pt_ref, bb, n_pages, v_hbm, vbuf.at[slot], sem.at[1, slot], rpp)
        return ck, cv

    def start(bb, slot):
        ck, cv = copies(bb, slot)
        lax.fori_loop(0, n_pages, lambda p, c: (ck(p).start(), cv(p).start(), c)[2], 0)

    def wait(bb, slot):
        ck, cv = copies(bb, slot)
        lax.fori_loop(0, n_pages, lambda p, c: (ck(p).wait(), cv(p).wait(), c)[2], 0)

    slot = b % 2

    @pl.when(b == 0)
    def _():
        kbuf[:, n_pages * rpp:, :] = jnp.zeros((2, rpp, B_HEAD_DIM), F32)
        vbuf[:, n_pages * rpp:, :] = jnp.zeros((2, rpp, B_HEAD_DIM), F32)
        start(0, 0)

    @pl.when(b + 1 < nb)
    def _():
        start(b + 1, 1 - slot)

    kbuf[slot, n_pages * rpp:n_pages * rpp + 8, :] = knew_ref[0]
    vbuf[slot, n_pages * rpp:n_pages * rpp + 8, :] = vnew_ref[0]
    wait(b, slot)

    sel = s_ref[0] >= thr_ref[0][:, 0:1]
    n_keys = past + PAGE_SIZE
    outs = []
    for n in range(B_KV_HEADS):
        kn = kbuf[slot, pl.ds(n, n_keys, stride=B_KV_HEADS), :].astype(BF16)
        vn = vbuf[slot, pl.ds(n, n_keys, stride=B_KV_HEADS), :].astype(BF16)
        qn = q_ref[0, n].astype(BF16)
        lg = _nt(qn, kn) * B_HEAD_DIM ** -0.5 + bias_ref[n]
        m = jnp.max(jnp.where(sel, lg, NEG), axis=1, keepdims=True)
        p = jnp.where(sel, jnp.exp(lg - m), 0.0)
        p = p / jnp.sum(p, axis=1, keepdims=True)
        outs.append(jnp.dot(p.astype(BF16), vn, preferred_element_type=F32))
    o_ref[0] = jnp.concatenate(outs, axis=0).astype(o_ref.dtype)


def dsa_decode(page_table, q8, scores, thr, knew8, vnew8, bias_rows, cache_k2, cache_v2):
    b, n_pages = page_table.shape
    past = n_pages * PAGE_SIZE
    l = past + LANES
    wkv = B_KV_HEADS * B_HEAD_DIM
    kern = functools.partial(_dsa_decode_kernel, n_pages=n_pages)
    gs = pltpu.PrefetchScalarGridSpec(
        num_scalar_prefetch=1,
        grid=(b,),
        in_specs=[pl.BlockSpec((1, B_KV_HEADS, 8, B_HEAD_DIM), lambda i, pt: (i, 0, 0, 0)),
                  pl.BlockSpec((1, 1, l), lambda i, pt: (i, 0, 0)),
                  pl.BlockSpec((1, 1, LANES), lambda i, pt: (i, 0, 0)),
                  pl.BlockSpec((1, 8, B_HEAD_DIM), lambda i, pt: (i, 0, 0)),
                  pl.BlockSpec((1, 8, B_HEAD_DIM), lambda i, pt: (i, 0, 0)),
                  pl.BlockSpec((B_KV_HEADS, 8, l), lambda i, pt: (0, 0, 0)),
                  pl.BlockSpec(memory_space=pl.ANY),
                  pl.BlockSpec(memory_space=pl.ANY)],
        out_specs=pl.BlockSpec((1, 2 * 8, B_HEAD_DIM), lambda i, pt: (i, 0, 0)),
        scratch_shapes=[pltpu.VMEM((2, l * B_KV_HEADS, B_HEAD_DIM), F32),
                        pltpu.VMEM((2, l * B_KV_HEADS, B_HEAD_DIM), F32),
                        pltpu.SemaphoreType.DMA((2, 2))],
    )
    return pl.pallas_call(
        kern, grid_spec=gs,
        out_shape=jax.ShapeDtypeStruct((b, 2 * 8, B_HEAD_DIM), F32),
        compiler_params=_cparams("arbitrary"),
        name="dsa_decode",
    )(page_table, q8, scores, thr, knew8, vnew8, bias_rows, cache_k2, cache_v2)


def _merge_kernel(ya_ref, yb_ref, ga_ref, gb_ref, wa_ref, wb_ref, o_ref):
    a = _mm(ya_ref[...], wa_ref[...])
    bb = _mm(yb_ref[...], wb_ref[...])
    o_ref[...] = (jax.nn.sigmoid(ga_ref[...]) * a + jax.nn.sigmoid(gb_ref[...]) * bb).astype(o_ref.dtype)


def merge(ya, yb, z, ga_col, gb_col, wa, wb, *, tm, tn):
    m, k = ya.shape
    n = wa.shape[1]
    return pl.pallas_call(
        _merge_kernel,
        grid=(m // tm, n // tn),
        in_specs=[pl.BlockSpec((tm, k), lambda i, j: (i, 0)),
                  pl.BlockSpec((tm, k), lambda i, j: (i, 0)),
                  pl.BlockSpec((tm, tn), lambda i, j: (i, ga_col // tn + j)),
                  pl.BlockSpec((tm, tn), lambda i, j: (i, gb_col // tn + j)),
                  pl.BlockSpec((k, tn), lambda i, j: (0, j)),
                  pl.BlockSpec((k, tn), lambda i, j: (0, j))],
        out_specs=pl.BlockSpec((tm, tn), lambda i, j: (i, j)),
        out_shape=jax.ShapeDtypeStruct((m, n), wa.dtype),
        compiler_params=_cparams("parallel", "arbitrary"),
        name="merge",
    )(ya, yb, z, z, wa, wb)


def _cross_prompt_kernel(x_ref, g_ref, wq_ref, mk_ref, mv_ref, wo_ref, o_ref):
    x = x_ref[...]
    ms = jnp.mean(x * x, axis=-1, keepdims=True)
    h = (x * lax.rsqrt(ms + EPS) * g_ref[...]).astype(BF16)
    q = jnp.dot(h, wq_ref[...], preferred_element_type=F32)
    outs = []
    for hh in range(X_HEADS):
        cols = slice(hh * X_HEAD_DIM, (hh + 1) * X_HEAD_DIM)
        lg = _nt(q[:, cols].astype(BF16), mk_ref[:, cols]) * X_HEAD_DIM ** -0.5
        mx = jnp.max(lg, axis=1, keepdims=True)
        p = jnp.exp(lg - mx)
        p = p / jnp.sum(p, axis=1, keepdims=True)
        outs.append(jnp.dot(p.astype(BF16), mv_ref[:, cols], preferred_element_type=F32).astype(BF16))
    att = jnp.concatenate(outs, axis=1)
    o_ref[...] = x + jnp.dot(att, wo_ref[...], preferred_element_type=F32)


def cross_prompt(x, g, wq, mk, mv, wo, *, tm):
    m, d = x.shape
    full = lambda a: pl.BlockSpec(a.shape, lambda i: (0,) * a.ndim)
    g2 = g.reshape(1, d)
    return pl.pallas_call(
        _cross_prompt_kernel,
        grid=(m // tm,),
        in_specs=[pl.BlockSpec((tm, d), lambda i: (i, 0)), full(g2), full(wq), full(mk), full(mv), full(wo)],
        out_specs=pl.BlockSpec((tm, d), lambda i: (i, 0)),
        out_shape=jax.ShapeDtypeStruct((m, d), F32),
        compiler_params=_cparams("parallel"),
        name="cross_prompt",
    )(x, g2, wq, mk, mv, wo)


def _cross_step_kernel(q_ref, mk_ref, mv_ref, o_ref):
    r16 = lambda a: a.astype(BF16).astype(F32)
    q = r16(q_ref[0])
    outs = []
    for hh in range(X_HEADS):
        cols = slice(hh * X_HEAD_DIM, (hh + 1) * X_HEAD_DIM)
        kh = r16(mk_ref[0, :, cols])
        vh = r16(mv_ref[0, :, cols])
        lg = jnp.sum(kh * q[:, cols], axis=1, keepdims=True) * X_HEAD_DIM ** -0.5
        mx = jnp.max(lg, axis=0, keepdims=True)
        p = jnp.exp(lg - mx)
        p = r16(p / jnp.sum(p, axis=0, keepdims=True))
        outs.append(jnp.sum(p * vh, axis=0, keepdims=True))
    o_ref[0] = jnp.concatenate(outs, axis=1).astype(o_ref.dtype)


def cross_step(q, mk, mv):
    b, w = q.shape
    mem = mk.shape[1]
    return pl.pallas_call(
        _cross_step_kernel,
        grid=(b,),
        in_specs=[pl.BlockSpec((1, 1, w), lambda i: (i, 0, 0)),
                  pl.BlockSpec((1, mem, w), lambda i: (i, 0, 0)),
                  pl.BlockSpec((1, mem, w), lambda i: (i, 0, 0))],
        out_specs=pl.BlockSpec((1, 1, w), lambda i: (i, 0, 0)),
        out_shape=jax.ShapeDtypeStruct((b, 1, w), F32),
        compiler_params=_cparams("arbitrary"),
        name="cross_step",
    )(q.reshape(b, 1, w), mk, mv)


def _router_kernel(x_ref, g_ref, w_ref, b_ref, hf_ref, route_ref):
    x = x_ref[...]
    ms = jnp.mean(x * x, axis=-1, keepdims=True)
    hf = x * lax.rsqrt(ms + EPS) * g_ref[...]
    hf_ref[...] = hf
    lg = _mm(hf, w_ref[...]) + b_ref[...]
    tm = lg.shape[0]
    lane = lax.broadcasted_iota(jnp.int32, (tm, LANES), 1)
    big = jnp.int32(LANES)
    is_g = lane < N_GROUPS
    gmax = jnp.max(jnp.where(is_g, lg, -jnp.inf), axis=1, keepdims=True)
    grp = jnp.min(jnp.where(is_g & (lg == gmax), lane, big), axis=1, keepdims=True)
    p_grp = 1.0 / jnp.sum(jnp.where(is_g, jnp.exp(lg - gmax), 0.0), axis=1, keepdims=True)
    e_lo = N_GROUPS + grp * EXP_PER_GROUP
    in_g = (lane >= e_lo) & (lane < e_lo + EXP_PER_GROUP)
    v1 = jnp.max(jnp.where(in_g, lg, -jnp.inf), axis=1, keepdims=True)
    i1 = jnp.min(jnp.where(in_g & (lg == v1), lane, big), axis=1, keepdims=True)
    rest = in_g & (lane != i1)
    v2 = jnp.max(jnp.where(rest, lg, -jnp.inf), axis=1, keepdims=True)
    i2 = jnp.min(jnp.where(rest & (lg == v2), lane, big), axis=1, keepdims=True)
    e2 = jnp.exp(v2 - v1)
    g1 = p_grp / (1.0 + e2)
    g2 = p_grp * e2 / (1.0 + e2)
    r = jnp.where(lane == 0, (i1 - N_GROUPS).astype(F32),
                  jnp.where(lane == 1, (i2 - N_GROUPS).astype(F32),
                            jnp.where(lane == 2, g1, jnp.where(lane == 3, g2, 0.0))))
    route_ref[...] = r


def router(x, g, w_pad, b_pad, *, tm):
    m, d = x.shape
    return pl.pallas_call(
        _router_kernel,
        grid=(pl.cdiv(m, tm),),
        in_specs=[pl.BlockSpec((tm, d), lambda i: (i, 0)),
                  pl.BlockSpec((1, d), lambda i: (0, 0)),
                  pl.BlockSpec((d, LANES), lambda i: (0, 0)),
                  pl.BlockSpec((1, LANES), lambda i: (0, 0))],
        out_specs=[pl.BlockSpec((tm, d), lambda i: (i, 0)),
                   pl.BlockSpec((tm, LANES), lambda i: (i, 0))],
        out_shape=[jax.ShapeDtypeStruct((m, d), F32), jax.ShapeDtypeStruct((m, LANES), F32)],
        compiler_params=_cparams("parallel"),
        name="router",
    )(x, g.reshape(1, d), w_pad, b_pad)


def _moe_kernel(be_ref, nxt_ref, nused_ref, sbase_ref, nvalid_ref, padj_ref, order_ref, hf_hbm, wg_hbm, wu_hbm,
                wd_hbm, y_hbm, xbuf, obuf, wg_f, wu_f, wd_f, wg_s, wu_s, wd_s, sem_in, sem_out, sem_w,
                *, n_tokens, plane):
    b = pl.program_id(0)
    nb = pl.num_programs(0)
    blk = xbuf.shape[1]
    slot = b % 2
    used = b < nused_ref[0]
    next_used = b + 1 < nused_ref[0]
    gap = plane - n_tokens
    n_assign = order_ref.shape[0]

    def slot_info(bb, r):
        valid = r < nvalid_ref[bb]
        asg = order_ref[jnp.minimum(sbase_ref[bb] + r, n_assign - 1)]
        tok = lax.shift_right_logical(asg, 1)
        j = padj_ref[bb] + r
        pad_row = jnp.where(j < gap, n_tokens + j,
                            jnp.where(j < 2 * gap, plane + n_tokens + j - gap, 2 * plane + j - 2 * gap))
        return jnp.where(valid, tok, 0), jnp.where(valid, (asg & 1) * plane + tok, pad_row)

    def weight_copies(e):
        return (pltpu.make_async_copy(wg_hbm.at[e], wg_f, sem_w.at[0]),
                pltpu.make_async_copy(wu_hbm.at[e], wu_f, sem_w.at[1]),
                pltpu.make_async_copy(wd_hbm.at[e], wd_f, sem_w.at[2]))

    def gather_start(bb, sl):
        for r in range(blk):
            pltpu.make_async_copy(hf_hbm.at[pl.ds(slot_info(bb, r)[0], 1)], xbuf.at[sl, pl.ds(r, 1)],
                                  sem_in.at[sl]).start()

    def scatter_start(bb, sl):
        for r in range(blk):
            pltpu.make_async_copy(obuf.at[sl, pl.ds(r, 1)], y_hbm.at[pl.ds(slot_info(bb, r)[1], 1)],
                                  sem_out.at[sl]).start()

    def gather_wait(sl):
        pltpu.make_async_copy(hf_hbm.at[pl.ds(0, blk)], xbuf.at[sl], sem_in.at[sl]).wait()

    def scatter_wait(sl):
        pltpu.make_async_copy(obuf.at[sl], y_hbm.at[pl.ds(0, blk)], sem_out.at[sl]).wait()

    @pl.when(b == 0)
    def _():
        for cp in weight_copies(be_ref[0]):
            cp.start()
        gather_start(0, 0)
        obuf[...] = jnp.zeros(obuf.shape, F32)

    changed = jnp.logical_and(used, jnp.logical_or(b == 0, be_ref[b] != be_ref[jnp.maximum(b - 1, 0)]))

    @pl.when(changed)
    def _():
        for cp in weight_copies(be_ref[b]):
            cp.wait()
        wg_s[...] = wg_f[...].astype(BF16)
        wu_s[...] = wu_f[...].astype(BF16)
        wd_s[...] = wd_f[...].astype(BF16)

    @pl.when(jnp.logical_and(changed, nxt_ref[b] >= 0))
    def _():
        for cp in weight_copies(nxt_ref[b]):
            cp.start()

    @pl.when(used)
    def _():
        gather_wait(slot)

    @pl.when(b >= 2)
    def _():
        scatter_wait(slot)

    def step(prefetch, flush_prev, compute):
        if prefetch:
            gather_start(b + 1, 1 - slot)
        if flush_prev:
            scatter_start(b - 1, 1 - slot)
        if compute:
            x = xbuf[slot].astype(BF16)
            gg = jnp.dot(x, wg_s[...], preferred_element_type=F32)
            uu = jnp.dot(x, wu_s[...], preferred_element_type=F32)
            a = (_silu(gg) * uu).astype(BF16)
            obuf[slot] = jnp.dot(a, wd_s[...], preferred_element_type=F32)

    first, last = b == 0, b == nb - 1
    land, lnot = jnp.logical_and, jnp.logical_not
    pl.when(land(first, next_used))(lambda: step(True, False, True))
    pl.when(land(first, lnot(next_used)))(lambda: step(False, False, True))
    pl.when(land(lnot(first), land(used, next_used)))(lambda: step(True, True, True))
    pl.when(land(lnot(first), land(used, lnot(next_used))))(lambda: step(False, True, True))
    pl.when(land(lnot(first), lnot(used)))(lambda: step(False, True, False))

    @pl.when(last)
    def _():
        scatter_start(b, slot)
        scatter_wait(slot)

    @pl.when(jnp.logical_and(last, jnp.logical_not(first)))
    def _():
        scatter_wait(1 - slot)


def moe_experts(tables, hf, w_g, w_u, w_d, *, blk, plane, out_rows):
    n_blocks = tables[0].shape[0]
    d = hf.shape[1]
    ff = w_g.shape[2]
    gs = pltpu.PrefetchScalarGridSpec(
        num_scalar_prefetch=len(tables),
        grid=(n_blocks,),
        in_specs=[pl.BlockSpec(memory_space=pl.ANY)] * 4,
        out_specs=pl.BlockSpec(memory_space=pl.ANY),
        scratch_shapes=[pltpu.VMEM((2, blk, d), F32), pltpu.VMEM((2, blk, d), F32),
                        pltpu.VMEM((d, ff), F32), pltpu.VMEM((d, ff), F32), pltpu.VMEM((ff, d), F32),
                        pltpu.VMEM((d, ff), BF16), pltpu.VMEM((d, ff), BF16), pltpu.VMEM((ff, d), BF16),
                        pltpu.SemaphoreType.DMA((2,)), pltpu.SemaphoreType.DMA((2,)),
                        pltpu.SemaphoreType.DMA((3,))],
    )
    return pl.pallas_call(
        functools.partial(_moe_kernel, n_tokens=hf.shape[0], plane=plane), grid_spec=gs,
        out_shape=jax.ShapeDtypeStruct((out_rows, d), F32),
        compiler_params=_cparams("arbitrary"),
        name="moe_experts",
    )(*tables, hf, w_g, w_u, w_d)


def _combine_kernel(x_ref, route_ref, gf_ref, y1_ref, y2_ref, op_ref, os_ref):
    i = pl.program_id(0)
    route = route_ref[...]
    x = x_ref[...] + route[:, 2:3] * y1_ref[...] + route[:, 3:4] * y2_ref[...]
    ms = jnp.mean(x * x, axis=-1, keepdims=True)
    out = x * lax.rsqrt(ms + EPS) * gf_ref[...]

    @pl.when(i < pl.num_programs(0) - 1)
    def _():
        op_ref[...] = out

    @pl.when(i == pl.num_programs(0) - 1)
    def _():
        os_ref[...] = out[:os_ref.shape[0]]


def combine(x, route, gf, y, plane, n_prompt, *, tm):
    m, d = x.shape
    n_tiles = n_prompt // tm
    assert n_prompt % tm == 0 and 0 < m - n_prompt <= tm
    return pl.pallas_call(
        _combine_kernel,
        grid=(n_tiles + 1,),
        in_specs=[pl.BlockSpec((tm, d), lambda i: (i, 0)),
                  pl.BlockSpec((tm, LANES), lambda i: (i, 0)),
                  pl.BlockSpec((1, d), lambda i: (0, 0)),
                  pl.BlockSpec((tm, d), lambda i: (i, 0)),
                  pl.BlockSpec((tm, d), lambda i: (plane // tm + i, 0))],
        out_specs=[pl.BlockSpec((tm, d), lambda i: (jnp.minimum(i, n_tiles - 1), 0)),
                   pl.BlockSpec((m - n_prompt, d), lambda i: (0, 0))],
        out_shape=[jax.ShapeDtypeStruct((n_prompt, d), F32), jax.ShapeDtypeStruct((m - n_prompt, d), F32)],
        compiler_params=_cparams("arbitrary"),
        name="combine",
    )(x, route, gf.reshape(1, d), y, y)


def _t5_bucket(dist):
    dist = jnp.asarray(dist, jnp.int32)
    max_exact = REL_BUCKETS // 2
    dist_f = jnp.maximum(dist, 1).astype(F32)
    large = max_exact + (jnp.log(dist_f / max_exact) / math.log(REL_MAX_DIST / max_exact)
                         * (REL_BUCKETS - max_exact)).astype(jnp.int32)
    large = jnp.minimum(large, REL_BUCKETS - 1)
    return jnp.where(dist < max_exact, dist, large)


def _bias_tables(rel_bias, past):
    r = np.arange(LANES)
    diff = r[:, None] - r[None, :]
    buckets = jnp.stack([_t5_bucket(np.maximum(diff, 0)),
                         _t5_bucket(np.maximum(diff + LANES, 0)),
                         _t5_bucket(np.full((LANES, LANES), 2 * LANES))])
    def lookup(bkt):
        oh = (bkt.reshape(-1, 1) == jnp.arange(REL_BUCKETS)[None, :]).astype(F32)
        out = jnp.dot(oh, rel_bias.astype(F32), precision=lax.Precision.HIGHEST)
        return out.T.reshape((rel_bias.shape[1],) + bkt.shape)

    tiles = lookup(buckets)
    tiles = tiles - tiles[:, 2:3]
    dist = np.maximum(past - np.arange(past + LANES), 0)
    rows = lookup(_t5_bucket(dist))
    rows = rows.reshape(B_KV_HEADS, B_GROUP, past + LANES)
    rows = jnp.concatenate([rows, jnp.zeros_like(rows)], axis=1)
    return tiles, rows


def _dispatch(eid, n_tokens, plane, blk):
    a = eid.shape[0]
    assert EXPERT_TOPK == 2 and a == EXPERT_TOPK * n_tokens
    n_blocks = -(-(a + N_EXPERTS * (blk - 1)) // blk)
    rows = n_blocks * blk
    assert rows - a >= EXPERT_TOPK * (plane - n_tokens)
    order = jnp.argsort(eid).astype(jnp.int32)
    counts = jnp.sum(eid[:, None] == jnp.arange(N_EXPERTS)[None, :], axis=0).astype(jnp.int32)
    cum = jnp.cumsum(counts)
    starts = cum - counts
    padded = (counts + blk - 1) // blk * blk
    pad_end = jnp.cumsum(padded)
    pad_start = pad_end - padded
    blocks = jnp.arange(n_blocks, dtype=jnp.int32)
    block_e = jnp.minimum(jnp.sum(pad_end[None, :] <= (blocks * blk)[:, None], axis=1), N_EXPERTS - 1)
    n_used = pad_end[-1] // blk
    off = blocks * blk - pad_start[block_e]
    sbase = starts[block_e] + off
    nvalid = jnp.where(blocks < n_used, jnp.clip(counts[block_e] - off, 0, blk), 0)
    padj = blocks * blk - cum[block_e]
    new_run = jnp.concatenate([jnp.array([True]), block_e[1:] != block_e[:-1]])
    run_start = jnp.where(new_run & (blocks < n_used), blocks, n_blocks)
    nxt = jnp.concatenate([jnp.flip(lax.cummin(jnp.flip(run_start)))[1:], jnp.array([n_blocks])])
    next_e = jnp.where(nxt < n_blocks, block_e[jnp.minimum(nxt, n_blocks - 1)], -1)
    i32 = lambda v: v.astype(jnp.int32)
    return (i32(block_e), i32(next_e), i32(n_used).reshape(1), i32(sbase), i32(nvalid), i32(padj), order), rows


def kernel(x_prompt, x_sample, mem_prompt, cache_k, cache_v, cache_kidx, page_table, state_hgrn, cache_mem_k,
           cache_mem_v, norm_mix, w_in, hgrn_lb_logits, hgrn_norm, w_branch_a, w_branch_b, w_out, norm_cross, w_xq,
           w_xk, w_xv, w_xo, norm_ffn, w_router_group, b_router_group, w_router_expert, b_router_expert, w_exp_gate,
           w_exp_up, w_exp_down, rel_bias, norm_final):
    assert w_in.shape[0] == 1, "single-layer step"
    l = 0
    drop0 = lambda a: a.reshape(a.shape[1:])
    bp, t, d = x_prompt.shape
    db = x_sample.shape[0]
    past = page_table.shape[1] * PAGE_SIZE
    xp = x_prompt.reshape(bp * t, d)
    xs = x_sample.reshape(db, d)

    wi_t = jnp.transpose(drop0(w_in)).astype(BF16)
    assert wi_t.shape[0] - NZ_MAIN + T_PAD == NZ_TAIL
    w_tail_t = jnp.pad(wi_t[NZ_MAIN:], ((T_PAD, 0), (0, 0)))
    wa, wb, wo = w_branch_a[l].astype(BF16), w_branch_b[l].astype(BF16), w_out[l].astype(BF16)
    wxq, wxk, wxv, wxo = (w_xq[l].astype(BF16), w_xk[l].astype(BF16), w_xv[l].astype(BF16), w_xo[l].astype(BF16))
    w_route = jnp.pad(jnp.concatenate([w_router_group[l], w_router_expert[l]], axis=1),
                      ((0, 0), (0, LANES - N_GROUPS - N_EXPERTS))).astype(BF16)
    b_route = jnp.pad(jnp.concatenate([b_router_group[l], b_router_expert[l]]),
                      (0, LANES - N_GROUPS - N_EXPERTS)).reshape(1, LANES)
    bias_tiles, bias_rows = _bias_tables(rel_bias, past)

    zp, zpt = in_proj(xp, norm_mix[l], wi_t, NZ_MAIN, w_tail_t, tm=1024, tn=512)
    kp = zp[:, C_BK:C_BK + 256]
    vp = zp[:, C_BV:C_BV + 256]
    ikp = zpt[:, T_SM + IK_LANE:T_SM + IK_LANE + IDX_DIM]
    ya_p, st_p = hgrn_prompt(zp, hgrn_lb_logits, hgrn_norm[l])
    yb_p = dsa_prompt(zp, zpt, ikp.astype(BF16), kp.astype(BF16), vp.astype(BF16), bias_tiles)
    mg_p = merge(ya_p, yb_p, zpt, T_GA, T_GB, wa, wb, tm=512, tn=512)
    x1p = matmul(mg_p, wo, xp, tm=512, tn=512)
    memp = mem_prompt.reshape(-1, d)
    mk = matmul(memp, wxk, tm=memp.shape[0], tn=512)
    mv = matmul(memp, wxv, tm=memp.shape[0], tn=512)
    x2p = cross_prompt(x1p, norm_cross[l], wxq, mk.astype(BF16), mv.astype(BF16), wxo, tm=512)

    zs, zst = in_proj(xs, norm_mix[l], wi_t, NZ_MAIN, w_tail_t, tm=db, tn=512)
    ks = zs[:, C_BK:C_BK + 256]
    vs = zs[:, C_BV:C_BV + 256]
    iks = zst[:, T_SM + IK_LANE:T_SM + IK_LANE + IDX_DIM]
    ya_s, st_s = hgrn_step(zs[:, :4 * A_HEADS * A_DK], hgrn_lb_logits, hgrn_norm[l], drop0(state_hgrn))
    iq_s = zs[:, C_IQ:C_IQ + IDX_HEADS * IDX_DIM].reshape(db, IDX_HEADS, IDX_DIM)
    iw_s = zst[:, T_SM + IW_LANE:T_SM + IW_LANE + IDX_HEADS].reshape(db, IDX_HEADS, 1)
    iknew_pad = jnp.pad(iks[:, :, None], ((0, 0), (0, 0), (0, LANES - 1)))
    scores = dsa_scores(page_table, iq_s, iw_s, iknew_pad,
                        jnp.swapaxes(drop0(cache_kidx), 1, 2)).reshape(db, past + LANES)
    topk_s = min(TOPK_MAX, (past + 1) // 4)
    thr = dsa_threshold(scores, topk_s)
    q8 = jnp.pad(zs[:, C_BQ:C_BQ + B_HEADS * B_HEAD_DIM].reshape(db, B_KV_HEADS, B_GROUP, B_HEAD_DIM),
                 ((0, 0), (0, 0), (0, 8 - B_GROUP), (0, 0)))
    knew8 = jnp.pad(ks.reshape(db, B_KV_HEADS, B_HEAD_DIM), ((0, 0), (0, 8 - B_KV_HEADS), (0, 0)))
    vnew8 = jnp.pad(vs.reshape(db, B_KV_HEADS, B_HEAD_DIM), ((0, 0), (0, 8 - B_KV_HEADS), (0, 0)))
    n_pool = cache_k.shape[1]
    ob = dsa_decode(page_table, q8, scores.reshape(db, 1, -1), thr.reshape(db, 1, LANES), knew8, vnew8, bias_rows,
                    cache_k.reshape(n_pool, PAGE_SIZE * B_KV_HEADS, B_HEAD_DIM),
                    cache_v.reshape(n_pool, PAGE_SIZE * B_KV_HEADS, B_HEAD_DIM))
    yb_s = ob.reshape(db, B_KV_HEADS, 8, B_HEAD_DIM)[:, :, :B_GROUP].reshape(db, B_HEADS * B_HEAD_DIM)
    mg_s = merge(ya_s.reshape(db, -1), yb_s, zst, T_GA, T_GB, wa, wb, tm=db, tn=512)
    x1s = matmul(mg_s, wo, xs, tm=db, tn=512)
    qx_s = norm_matmul(x1s, norm_cross[l], wxq, tm=db, tn=512)
    mem = cache_mem_k.shape[2]
    att_s = cross_step(qx_s, cache_mem_k.reshape(db, mem, -1), cache_mem_v.reshape(db, mem, -1))
    x2s = matmul(att_s.reshape(db, -1), wxo, x1s, tm=db, tn=512)

    x2 = jnp.concatenate([x2p, x2s], axis=0)
    n = x2.shape[0]
    hf, route = router(x2, norm_ffn[l], w_route, b_route, tm=256)
    eid = route[:, :EXPERT_TOPK].astype(jnp.int32).reshape(-1)
    tmc = 128
    plane = -(-n // tmc) * tmc
    tables, rows = _dispatch(eid, n, plane, MOE_ROWS)
    ye = moe_experts(tables, hf, drop0(w_exp_gate), drop0(w_exp_up), drop0(w_exp_down),
                     blk=MOE_ROWS, plane=plane, out_rows=rows)
    y_p, y_s = combine(x2, route, norm_final, ye, plane, bp * t, tm=tmc)

    y_prompt = y_p.reshape(bp, t, d)
    y_sample = y_s.reshape(db, 1, d)
    return (y_prompt, y_sample,
            kp.reshape(1, bp, t, B_KV_HEADS, B_HEAD_DIM), vp.reshape(1, bp, t, B_KV_HEADS, B_HEAD_DIM),
            ikp.reshape(1, bp, t, IDX_DIM),
            jnp.swapaxes(st_p, 1, 2).reshape(1, bp, A_HEADS, A_DK, A_DV),
            mk.reshape(1, bp, -1, X_HEADS, X_HEAD_DIM), mv.reshape(1, bp, -1, X_HEADS, X_HEAD_DIM),
            ks.reshape(1, db, 1, B_KV_HEADS, B_HEAD_DIM), vs.reshape(1, db, 1, B_KV_HEADS, B_HEAD_DIM),
            iks.reshape(1, db, 1, IDX_DIM),
            st_s.reshape(1, db, A_HEADS, A_DK, A_DV))
```

```python
import functools
import math

import jax
import jax.numpy as jnp
import numpy as np
from jax import lax
from jax.experimental import pallas as pl
from jax.experimental.pallas import tpu as pltpu

F32 = jnp.float32
BF16 = jnp.bfloat16
EPS = 1e-6

D_MODEL = 2048
A_HEADS, A_DK, A_DV = 8, 128, 128
B_HEADS, B_KV_HEADS, B_HEAD_DIM = 8, 2, 128
B_GROUP = B_HEADS // B_KV_HEADS
IDX_HEADS, IDX_DIM = 16, 64
TOPK_MAX = 256
PAGE_SIZE = 128
REL_BUCKETS, REL_MAX_DIST = 32, 128
X_HEADS, X_HEAD_DIM = 4, 128
N_GROUPS, EXP_PER_GROUP = 4, 8
N_EXPERTS = N_GROUPS * EXP_PER_GROUP
EXPERT_TOPK = 2
EXPERT_FF = 512
MOE_ROWS = 256

LANES = 128
VMEM_LIMIT = 56 * 1024 * 1024

NEG = -1e30

C_AQ, C_AF, C_AI, C_AG, C_BQ, C_BK, C_BV, C_IQ = 0, 1024, 2048, 3072, 4096, 5120, 5376, 5632
NZ_MAIN = 6656
T_PAD = 432
T_SM, IW_LANE, IK_LANE = 384, 48, 64
T_GA, T_GB = 512, 2560
NZ_TAIL = 4608


def _cparams(*sem):
    return pltpu.CompilerParams(dimension_semantics=sem, vmem_limit_bytes=VMEM_LIMIT)


def _silu(x):
    return x * jax.nn.sigmoid(x)


def _nt(a, b):
    return lax.dot_general(a, b, (((1,), (1,)), ((), ())), preferred_element_type=F32)


def _nt_f32(a, b):
    return lax.dot_general(a, b, (((1,), (1,)), ((), ())), preferred_element_type=F32,
                           precision=lax.Precision.HIGHEST)


def _mm(a, w):
    return jnp.dot(a.astype(BF16), w.astype(BF16), preferred_element_type=F32)


def _norm_matmul_kernel(x_ref, g_ref, w_ref, o_ref, h_ref):
    @pl.when(pl.program_id(1) == 0)
    def _():
        x = x_ref[...]
        ms = jnp.mean(x * x, axis=-1, keepdims=True)
        h_ref[...] = (x * lax.rsqrt(ms + EPS) * g_ref[...]).astype(h_ref.dtype)

    o_ref[...] = _mm(h_ref[...], w_ref[...])


def norm_matmul(x, g, w, *, tm, tn):
    m, k = x.shape
    n = w.shape[1]
    assert n % tn == 0
    return pl.pallas_call(
        _norm_matmul_kernel,
        grid=(m // tm, n // tn),
        in_specs=[pl.BlockSpec((tm, k), lambda i, j: (i, 0)),
                  pl.BlockSpec((1, k), lambda i, j: (0, 0)),
                  pl.BlockSpec((k, tn), lambda i, j: (0, j))],
        out_specs=pl.BlockSpec((tm, tn), lambda i, j: (i, j)),
        out_shape=jax.ShapeDtypeStruct((m, n), F32),
        scratch_shapes=[pltpu.VMEM((tm, k), BF16)],
        compiler_params=_cparams("parallel", "arbitrary"),
        name="norm_matmul",
    )(x, g.reshape(1, k), w)


def _in_proj_kernel(x_ref, g_ref, wa_ref, wb_ref, oa_ref, ob_ref, h_ref, *, na):
    j = pl.program_id(1)

    @pl.when(j == 0)
    def _():
        x = x_ref[...]
        ms = jnp.mean(x * x, axis=-1, keepdims=True)
        h_ref[...] = (x * lax.rsqrt(ms + EPS) * g_ref[...]).astype(h_ref.dtype)

    @pl.when(j < na)
    def _():
        oa_ref[...] = _nt(h_ref[...], wa_ref[...])

    @pl.when(j >= na)
    def _():
        ob_ref[...] = _nt(h_ref[...], wb_ref[...])


def in_proj(x, g, wa_t, a_rows, wb_t, *, tm, tn):
    m, k = x.shape
    na, nb = a_rows // tn, wb_t.shape[0] // tn
    assert a_rows % tn == 0 and wb_t.shape[0] % tn == 0
    a_idx = lambda j: jnp.minimum(j, na - 1)
    b_idx = lambda j: jnp.maximum(j - na, 0)
    return pl.pallas_call(
        functools.partial(_in_proj_kernel, na=na),
        grid=(m // tm, na + nb),
        in_specs=[pl.BlockSpec((tm, k), lambda i, j: (i, 0)),
                  pl.BlockSpec((1, k), lambda i, j: (0, 0)),
                  pl.BlockSpec((tn, k), lambda i, j: (a_idx(j), 0)),
                  pl.BlockSpec((tn, k), lambda i, j: (b_idx(j), 0))],
        out_specs=[pl.BlockSpec((tm, tn), lambda i, j: (i, a_idx(j))),
                   pl.BlockSpec((tm, tn), lambda i, j: (i, b_idx(j)))],
        out_shape=[jax.ShapeDtypeStruct((m, na * tn), F32), jax.ShapeDtypeStruct((m, nb * tn), F32)],
        scratch_shapes=[pltpu.VMEM((tm, k), BF16)],
        compiler_params=_cparams("parallel", "arbitrary"),
        name="in_proj",
    )(x, g.reshape(1, k), wa_t, wb_t)


def _matmul_res_kernel(x_ref, w_ref, r_ref, o_ref):
    o_ref[...] = r_ref[...] + _mm(x_ref[...], w_ref[...])


def _matmul_kernel(x_ref, w_ref, o_ref):
    o_ref[...] = _mm(x_ref[...], w_ref[...])


def matmul(x, w, res=None, *, tm, tn):
    m, k = x.shape
    n = w.shape[1]
    in_specs = [pl.BlockSpec((tm, k), lambda i, j: (i, 0)),
                pl.BlockSpec((k, tn), lambda i, j: (0, j))]
    args = [x, w]
    kern = _matmul_kernel
    if res is not None:
        in_specs.append(pl.BlockSpec((tm, tn), lambda i, j: (i, j)))
        args.append(res)
        kern = _matmul_res_kernel
    return pl.pallas_call(
        kern,
        grid=(m // tm, n // tn),
        in_specs=in_specs,
        out_specs=pl.BlockSpec((tm, tn), lambda i, j: (i, j)),
        out_shape=jax.ShapeDtypeStruct((m, n), F32),
        compiler_params=_cparams("parallel", "arbitrary"),
        name="matmul",
    )(*args)


HG_TB = 128
HG_C = 16
HG_H = HG_C // 2


def _hgrn_prompt_kernel(aq_ref, af_ref, ai_ref, ag_ref, lbl_ref, ng_ref, ya_ref, st_out_ref,
                        st_ref, q_s, k_s, g_s, v_s):
    t = pl.program_id(0)

    @pl.when(t == 0)
    def _():
        st_ref[...] = jnp.zeros_like(st_ref)

    lbl = lbl_ref[...]
    mx = jnp.max(lbl, axis=0, keepdims=True)
    ex = jnp.exp(lbl - mx)
    lb = ex[0:1, :] / jnp.sum(ex, axis=0, keepdims=True)

    f = lb + (1.0 - lb) * jax.nn.sigmoid(af_ref[...])
    logf = jnp.log(f)
    row = lax.broadcasted_iota(jnp.int32, (HG_TB, HG_TB), 0)
    col = lax.broadcasted_iota(jnp.int32, (HG_TB, HG_TB), 1)
    tri = jnp.where((row // HG_C == col // HG_C) & (col <= row), 1.0, 0.0).astype(F32)
    g_s[...] = jnp.dot(tri, logf, preferred_element_type=F32, precision=lax.Precision.HIGHEST)
    q_s[...] = _silu(aq_ref[...])
    k_s[...] = 1.0 - f
    v_s[...] = ai_ref[...]

    sub = lax.broadcasted_iota(jnp.int32, (HG_C, A_DK), 0)
    sub8 = lax.broadcasted_iota(jnp.int32, (HG_H, A_DK), 0)
    ng = ng_ref[...]

    def chunk(c, carry):
        r0 = pl.multiple_of(c * HG_C, HG_C)
        rows = pl.ds(r0, HG_C)
        for h in range(A_HEADS):
            cols = slice(h * A_DK, (h + 1) * A_DK)
            g = g_s[rows, cols]
            qh = q_s[rows, cols]
            kh = k_s[rows, cols]
            vh = v_s[rows, cols]
            halves = []
            for hb in range(2):
                rs = slice(hb * HG_H, (hb + 1) * HG_H)
                gb, qb, kb, vb = g[rs], qh[rs], kh[rs], vh[rs]
                ob = jnp.zeros((HG_H, A_DV), F32)
                for tt in range(HG_H):
                    d = gb[tt:tt + 1, :] - gb
                    e = jnp.exp(jnp.where(sub8 <= tt, d, -jnp.inf))
                    p = e * (qb[tt:tt + 1, :] * kb)
                    a_col = jnp.sum(p, axis=1, keepdims=True)
                    o_row = jnp.sum(a_col * vb, axis=0, keepdims=True)
                    ob = jnp.where(sub8 == tt, o_row, ob)
                halves.append(ob)
            o = jnp.concatenate(halves, axis=0)
            low = sub < HG_H
            g_mid = g[HG_H - 1:HG_H, :]
            q_hi = jnp.where(low, 0.0, qh * jnp.exp(jnp.minimum(g - g_mid, 0.0)))
            k_lo = jnp.where(low, kh * jnp.exp(jnp.minimum(g_mid - g, 0.0)), 0.0)
            st = st_ref[h]
            g_last = g[HG_C - 1:HG_C, :]
            kt = kh * jnp.exp(g_last - g)
            upd = lax.dot_general(vh.astype(BF16), jnp.concatenate([kt, k_lo], axis=1).astype(BF16),
                                  (((0,), (0,)), ((), ())), preferred_element_type=F32)
            lhs = jnp.concatenate([qh * jnp.exp(g), q_hi], axis=1).astype(BF16)
            rhs = jnp.concatenate([st, upd[:, A_DK:]], axis=1).astype(BF16)
            o = o + _nt(lhs, rhs)
            st_ref[h] = st * jnp.exp(g_last) + upd[:, :A_DK]
            on = o * lax.rsqrt(jnp.mean(o * o, axis=-1, keepdims=True) + EPS) * ng
            ya_ref[rows, cols] = (on * _silu(ag_ref[rows, cols])).astype(ya_ref.dtype)
        return carry

    lax.fori_loop(0, HG_TB // HG_C, chunk, 0)

    @pl.when(t == pl.num_programs(0) - 1)
    def _():
        st_out_ref[...] = st_ref[...]


def hgrn_prompt(z, lb_logits, norm_g):
    m = z.shape[0]
    w = A_HEADS * A_DK

    def zspec(cb):
        return pl.BlockSpec((HG_TB, w), lambda t, cb=cb: (t, cb))

    return pl.pallas_call(
        _hgrn_prompt_kernel,
        grid=(m // HG_TB,),
        in_specs=[zspec(C_AQ // w), zspec(C_AF // w), zspec(C_AI // w), zspec(C_AG // w),
                  pl.BlockSpec(lb_logits.shape, lambda t: (0, 0)),
                  pl.BlockSpec((1, A_DV), lambda t: (0, 0))],
        out_specs=[pl.BlockSpec((HG_TB, w), lambda t: (t, 0)),
                   pl.BlockSpec((A_HEADS, A_DV, A_DK), lambda t: (0, 0, 0))],
        out_shape=[jax.ShapeDtypeStruct((m, w), BF16),
                   jax.ShapeDtypeStruct((A_HEADS, A_DV, A_DK), F32)],
        scratch_shapes=[pltpu.VMEM((A_HEADS, A_DV, A_DK), F32)] + [pltpu.VMEM((HG_TB, w), F32)] * 4,
        compiler_params=_cparams("arbitrary"),
        name="hgrn_prompt",
    )(z, z, z, z, lb_logits, norm_g.reshape(1, A_DV))


def _hgrn_step_kernel(z_ref, lbl_ref, ng_ref, s_ref, ya_ref, s_out_ref):
    lbl = lbl_ref[...]
    mx = jnp.max(lbl, axis=0, keepdims=True)
    ex = jnp.exp(lbl - mx)
    lb = ex[0:1, :] / jnp.sum(ex, axis=0, keepdims=True)
    z = z_ref[0]
    w = A_HEADS * A_DK
    q = _silu(z[:, 0:w])
    f = lb + (1.0 - lb) * jax.nn.sigmoid(z[:, w:2 * w])
    kk = 1.0 - f
    v = z[:, 2 * w:3 * w]
    ag = z[:, 3 * w:4 * w]
    rows = []
    for h in range(A_HEADS):
        cols = slice(h * A_DK, (h + 1) * A_DK)
        rows += [f[:, cols], kk[:, cols], q[:, cols]]
    rows.append(jnp.zeros((LANES - 3 * A_HEADS, A_DK), F32))
    xt = jnp.concatenate(rows, axis=0).T
    ng = ng_ref[...]
    r16 = lambda a: a.astype(BF16).astype(F32)
    outs = []
    for h in range(A_HEADS):
        cols = slice(h * A_DV, (h + 1) * A_DV)
        fcol = xt[:, 3 * h:3 * h + 1]
        kcol = xt[:, 3 * h + 1:3 * h + 2]
        qcol = xt[:, 3 * h + 2:3 * h + 3]
        s_old = s_ref[0, h]
        s_out_ref[0, h] = fcol * s_old + kcol * v[:, cols]
        o = (jnp.sum(r16(qcol * fcol) * r16(s_old), axis=0, keepdims=True)
             + jnp.sum(qcol * kcol, axis=0, keepdims=True) * v[:, cols])
        on = o * lax.rsqrt(jnp.mean(o * o, axis=-1, keepdims=True) + EPS) * ng
        outs.append(on * _silu(ag[:, cols]))
    ya_ref[0] = jnp.concatenate(outs, axis=1).astype(ya_ref.dtype)


def hgrn_step(z4, lb_logits, norm_g, state):
    b = z4.shape[0]
    w = A_HEADS * A_DK
    return pl.pallas_call(
        _hgrn_step_kernel,
        grid=(b,),
        in_specs=[pl.BlockSpec((1, 1, 4 * w), lambda i: (i, 0, 0)),
                  pl.BlockSpec(lb_logits.shape, lambda i: (0, 0)),
                  pl.BlockSpec((1, A_DV), lambda i: (0, 0)),
                  pl.BlockSpec((1, A_HEADS, A_DK, A_DV), lambda i: (i, 0, 0, 0))],
        out_specs=[pl.BlockSpec((1, 1, w), lambda i: (i, 0, 0)),
                   pl.BlockSpec((1, A_HEADS, A_DK, A_DV), lambda i: (i, 0, 0, 0))],
        out_shape=[jax.ShapeDtypeStruct((b, 1, w), F32),
                   jax.ShapeDtypeStruct(state.shape, F32)],
        compiler_params=_cparams("arbitrary"),
        name="hgrn_step",
    )(z4.reshape(b, 1, 4 * w), lb_logits, norm_g.reshape(1, A_DV), state)


BISECT_MAX_ITERS = 48


def _bisect_threshold(count_ge, lo, hi, cnt_lo, topk):
    kf = float(topk)

    def cond(c):
        return jnp.logical_and(c[0] < BISECT_MAX_ITERS, c[-1] > 0.0)

    def body(c):
        it, lo, hi, cl, _ = c
        mid = 0.5 * lo + 0.5 * hi
        cm = count_ge(mid)
        ge = cm >= kf
        lo = jnp.where(ge, mid, lo)
        cl = jnp.where(ge, cm, cl)
        hi = jnp.where(ge, hi, mid)
        busy = jnp.max(jnp.where(cl > kf, 1.0, 0.0))
        return it + 1, lo, hi, cl, busy

    busy0 = jnp.max(jnp.where(cnt_lo > kf, 1.0, 0.0))
    out = lax.while_loop(cond, body, (jnp.int32(0), lo, hi, cnt_lo, busy0))
    return out[1]


DSA_QB = 128
DSA_W = 512
DSA_W3 = 1024


def _dsa_prompt_kernel(iq0_ref, iq1_ref, bq_ref, iw_ref, kidx_ref, k_ref, v_ref, bias_ref, o_ref,
                       score_s, qih_s, qs_s, wb_s, m_s, l_s, acc_s, *, topk):
    i = pl.program_id(0)
    nsub = DSA_W // LANES
    nsub3 = DSA_W3 // LANES
    nch3 = (i * DSA_QB + DSA_QB + DSA_W3 - 1) // DSA_W3
    nch = nch3 * (DSA_W3 // DSA_W)
    qpos = i * DSA_QB + lax.broadcasted_iota(jnp.int32, (DSA_QB, 1), 0)

    iw = iw_ref[...]
    wscale = IDX_DIM ** -0.5 * IDX_HEADS ** -0.5
    for h in range(IDX_HEADS):
        iq_ref, hh = (iq0_ref, h) if h < IDX_HEADS // 2 else (iq1_ref, h - IDX_HEADS // 2)
        qih_s[h] = iq_ref[:, hh * IDX_DIM:(hh + 1) * IDX_DIM].astype(BF16)
        wb_s[h] = jnp.broadcast_to(iw[:, IW_LANE + h:IW_LANE + h + 1] * wscale, (DSA_QB, LANES))
    for h in range(B_HEADS):
        qs_s[h // B_GROUP, (h % B_GROUP) * DSA_QB:(h % B_GROUP + 1) * DSA_QB, :] = (
            bq_ref[:, h * B_HEAD_DIM:(h + 1) * B_HEAD_DIM] * B_HEAD_DIM ** -0.5).astype(BF16)

    def p1(c, carry):
        c0 = pl.multiple_of(c * DSA_W, DSA_W)
        kc = kidx_ref[pl.ds(c0, DSA_W), :]
        sc = [jnp.zeros((DSA_QB, LANES), F32) for _ in range(nsub)]
        for h in range(IDX_HEADS):
            s = jnp.maximum(_nt(qih_s[h], kc), 0.0)
            wb = wb_s[h]
            for j in range(nsub):
                sc[j] = sc[j] + s[:, j * LANES:(j + 1) * LANES] * wb
        for j in range(nsub):
            kpos = c0 + j * LANES + lax.broadcasted_iota(jnp.int32, (1, LANES), 1)
            score_s[c * nsub + j] = jnp.where(kpos <= qpos, sc[j], -jnp.inf)
        return carry

    lax.fori_loop(0, nch, p1, 0)

    def stats(c, carry):
        mn, mx = carry
        for j in range(nsub):
            s = score_s[c * nsub + j]
            mx = jnp.maximum(mx, s)
            mn = jnp.minimum(mn, jnp.where(s > -jnp.inf, s, jnp.inf))
        return mn, mx

    mn, mx = lax.fori_loop(0, nch, stats, (jnp.full((DSA_QB, LANES), jnp.inf, F32),
                                           jnp.full((DSA_QB, LANES), -jnp.inf, F32)))
    lo0 = jnp.min(mn, axis=1, keepdims=True)
    hi0 = jnp.max(mx, axis=1, keepdims=True)

    def count_ge(thr):
        thr_b = jnp.broadcast_to(thr, (DSA_QB, LANES))

        def body(c, acc):
            for j in range(nsub):
                s = score_s[c * nsub + j]
                acc = acc + jnp.where(s >= thr_b, 1.0, 0.0)
            return acc

        acc = lax.fori_loop(0, nch, body, jnp.zeros((DSA_QB, LANES), F32))
        return jnp.sum(acc, axis=1, keepdims=True)

    thr = _bisect_threshold(count_ge, lo0, hi0, (qpos + 1).astype(F32), topk)
    thr_b = jnp.broadcast_to(thr, (DSA_QB, LANES))

    m_s[...] = jnp.full(m_s.shape, NEG, F32)
    l_s[...] = jnp.zeros(l_s.shape, F32)
    acc_s[...] = jnp.zeros(acc_s.shape, F32)

    def p3(c, with_bias):
        c0 = pl.multiple_of(c * DSA_W3, DSA_W3)
        madd = jnp.concatenate([jnp.where(score_s[c * nsub3 + j] >= thr_b, 0.0, NEG) for j in range(nsub3)], axis=1)
        kc = k_ref[pl.ds(c0, DSA_W3), :]
        vc = v_ref[pl.ds(c0, DSA_W3), :]
        rel = [i - (c * nsub3 + j) for j in range(nsub3)]

        def scores(n):
            return _nt(qs_s[n], kc[:, n * B_HEAD_DIM:(n + 1) * B_HEAD_DIM])

        def softmax(n, lg):
            lg = lg.reshape(B_GROUP, DSA_QB, DSA_W3) + madd[None]
            if with_bias:
                lg = lg + jnp.stack([jnp.concatenate(
                    [jnp.where(rel[j] == 0, bias_ref[n * B_GROUP + gq, 0],
                               jnp.where(rel[j] == 1, bias_ref[n * B_GROUP + gq, 1], 0.0)) for j in range(nsub3)],
                    axis=1) for gq in range(B_GROUP)])
            m_old = m_s[n]
            m_new = jnp.maximum(m_old, jnp.max(lg, axis=-1, keepdims=True))
            p = jnp.exp(lg - m_new)
            alpha = jnp.exp(m_old - m_new)
            l_s[n] = alpha * l_s[n] + jnp.sum(p, axis=-1, keepdims=True)
            m_s[n] = m_new
            pv = jnp.dot(p.reshape(B_GROUP * DSA_QB, DSA_W3).astype(BF16), vc[:, n * B_HEAD_DIM:(n + 1) * B_HEAD_DIM],
                         preferred_element_type=F32)
            return alpha, pv.reshape(B_GROUP, DSA_QB, B_HEAD_DIM)

        lgs = [scores(n) for n in range(B_KV_HEADS)]
        outs = [softmax(n, lgs[n]) for n in range(B_KV_HEADS)]
        for n in range(B_KV_HEADS):
            acc_s[n] = outs[n][0] * acc_s[n] + outs[n][1]

    n_far = jnp.maximum(i - 1, 0) // nsub3
    lax.fori_loop(0, n_far, lambda c, carry: (p3(c, False), carry)[1], 0)
    lax.fori_loop(n_far, nch3, lambda c, carry: (p3(c, True), carry)[1], 0)

    for h in range(B_HEADS):
        n, gq = h // B_GROUP, h % B_GROUP
        o_ref[:, h * B_HEAD_DIM:(h + 1) * B_HEAD_DIM] = (acc_s[n, gq] / l_s[n, gq]).astype(o_ref.dtype)


def dsa_prompt(z, ztail, kidx_bf, k_bf, v_bf, bias_tiles):
    m = z.shape[0]
    topk = min(TOPK_MAX, m // 4)
    wq = B_HEADS * B_HEAD_DIM
    wi2 = IDX_HEADS * IDX_DIM // 2
    kern = functools.partial(_dsa_prompt_kernel, topk=topk)
    return pl.pallas_call(
        kern,
        grid=(m // DSA_QB,),
        in_specs=[pl.BlockSpec((DSA_QB, wi2), lambda i: (i, C_IQ // wi2)),
                  pl.BlockSpec((DSA_QB, wi2), lambda i: (i, C_IQ // wi2 + 1)),
                  pl.BlockSpec((DSA_QB, wq), lambda i: (i, C_BQ // wq)),
                  pl.BlockSpec((DSA_QB, LANES), lambda i: (i, T_SM // LANES)),
                  pl.BlockSpec(kidx_bf.shape, lambda i: (0, 0)),
                  pl.BlockSpec(k_bf.shape, lambda i: (0, 0)),
                  pl.BlockSpec(v_bf.shape, lambda i: (0, 0)),
                  pl.BlockSpec(bias_tiles.shape, lambda i: (0, 0, 0, 0))],
        out_specs=pl.BlockSpec((DSA_QB, wq), lambda i: (i, 0)),
        out_shape=jax.ShapeDtypeStruct((m, wq), BF16),
        scratch_shapes=[pltpu.VMEM((m // LANES, DSA_QB, LANES), F32),
                        pltpu.VMEM((IDX_HEADS, DSA_QB, IDX_DIM), BF16),
                        pltpu.VMEM((B_KV_HEADS, B_GROUP * DSA_QB, B_HEAD_DIM), BF16),
                        pltpu.VMEM((IDX_HEADS, DSA_QB, LANES), F32),
                        pltpu.VMEM((B_KV_HEADS, B_GROUP, DSA_QB, 1), F32),
                        pltpu.VMEM((B_KV_HEADS, B_GROUP, DSA_QB, 1), F32),
                        pltpu.VMEM((B_KV_HEADS, B_GROUP, DSA_QB, B_HEAD_DIM), F32)],
        compiler_params=_cparams("arbitrary"),
        name="dsa_prompt",
    )(z, z, z, ztail, kidx_bf, k_bf, v_bf, bias_tiles)


def _page_copies(table_ref, b, n_pages, src_hbm, dst, sem, rows_per_page=PAGE_SIZE):
    def copy(p):
        return pltpu.make_async_copy(src_hbm.at[table_ref[b, p]],
                                     dst.at[pl.ds(p * rows_per_page, rows_per_page)], sem)
    return copy


def _dsa_scores_kernel(pt_ref, iq_ref, iw_ref, iknew_ref, kidx_hbm, o_ref, buf, sem, *, n_pages):
    b = pl.program_id(0)
    nb = pl.num_programs(0)
    past = n_pages * PAGE_SIZE

    def page_copy(bb, slot, p):
        return pltpu.make_async_copy(kidx_hbm.at[pt_ref[bb, p]],
                                     buf.at[slot, :, pl.ds(pl.multiple_of(p * PAGE_SIZE, PAGE_SIZE), PAGE_SIZE)],
                                     sem.at[slot])

    def start(bb, slot):
        lax.fori_loop(0, n_pages, lambda p, c: (page_copy(bb, slot, p).start(), c)[1], 0)

    def wait(bb, slot):
        lax.fori_loop(0, n_pages, lambda p, c: (page_copy(bb, slot, p).wait(), c)[1], 0)

    slot = b % 2

    @pl.when(b == 0)
    def _():
        start(0, 0)

    @pl.when(b + 1 < nb)
    def _():
        start(b + 1, 1 - slot)

    wait(b, slot)

    r16 = lambda a: a.astype(BF16).astype(F32)
    qi = iq_ref[0].astype(BF16)
    wcol = r16(iw_ref[0]) * (IDX_DIM ** -0.5 * IDX_HEADS ** -0.5)
    s = r16(jnp.maximum(jnp.dot(qi, buf[slot].astype(BF16), preferred_element_type=F32), 0.0))
    o_ref[0, :, 0:past] = jnp.sum(s * wcol, axis=0, keepdims=True)
    sn = r16(jnp.maximum(jnp.dot(qi, iknew_ref[0].astype(BF16), preferred_element_type=F32), 0.0))
    sn = jnp.sum(sn * wcol, axis=0, keepdims=True)
    lane = lax.broadcasted_iota(jnp.int32, (1, LANES), 1)
    o_ref[0, :, past:past + LANES] = jnp.where(lane == 0, sn, -jnp.inf)


def dsa_scores(page_table, iq, iw, iknew_pad, cache_kidx):
    b, n_pages = page_table.shape
    past = n_pages * PAGE_SIZE
    kern = functools.partial(_dsa_scores_kernel, n_pages=n_pages)
    gs = pltpu.PrefetchScalarGridSpec(
        num_scalar_prefetch=1,
        grid=(b,),
        in_specs=[pl.BlockSpec((1, IDX_HEADS, IDX_DIM), lambda i, pt: (i, 0, 0)),
                  pl.BlockSpec((1, IDX_HEADS, 1), lambda i, pt: (i, 0, 0)),
                  pl.BlockSpec((1, IDX_DIM, LANES), lambda i, pt: (i, 0, 0)),
                  pl.BlockSpec(memory_space=pl.ANY)],
        out_specs=pl.BlockSpec((1, 1, past + LANES), lambda i, pt: (i, 0, 0)),
        scratch_shapes=[pltpu.VMEM((2, IDX_DIM, past), F32), pltpu.SemaphoreType.DMA((2,))],
    )
    return pl.pallas_call(
        kern, grid_spec=gs,
        out_shape=jax.ShapeDtypeStruct((b, 1, past + LANES), F32),
        compiler_params=_cparams("arbitrary"),
        name="dsa_scores",
    )(page_table, iq, iw, iknew_pad, cache_kidx)


def _dsa_threshold_kernel(s_ref, thr_ref, *, topk):
    s = s_ref[...]
    nb = s.shape[0]
    finite = s > -jnp.inf
    lo0 = jnp.min(jnp.where(finite, s, jnp.inf), axis=1, keepdims=True)
    hi0 = jnp.max(s, axis=1, keepdims=True)
    cnt0 = jnp.sum(jnp.where(finite, 1.0, 0.0), axis=1, keepdims=True)

    def count_ge(thr):
        return jnp.sum(jnp.where(s_ref[...] >= thr, 1.0, 0.0), axis=1, keepdims=True)

    thr = _bisect_threshold(count_ge, lo0, hi0, cnt0, topk)
    thr_ref[...] = jnp.broadcast_to(thr, (nb, LANES))


def dsa_threshold(scores, topk):
    b, l = scores.shape
    return pl.pallas_call(
        functools.partial(_dsa_threshold_kernel, topk=topk),
        grid=(1,),
        in_specs=[pl.BlockSpec((b, l), lambda i: (0, 0))],
        out_specs=pl.BlockSpec((b, LANES), lambda i: (0, 0)),
        out_shape=jax.ShapeDtypeStruct((b, LANES), F32),
        compiler_params=_cparams("arbitrary"),
        name="dsa_threshold",
    )(scores)


def _dsa_decode_kernel(pt_ref, q_ref, s_ref, thr_ref, knew_ref, vnew_ref, bias_ref, k_hbm, v_hbm, o_ref,
                       kbuf, vbuf, sem, *, n_pages):
    b = pl.program_id(0)
    nb = pl.num_programs(0)
    past = n_pages * PAGE_SIZE

    rpp = PAGE_SIZE * B_KV_HEADS

    def copies(bb, slot):
        ck = _page_copies(pt_ref, bb, n_pages, k_hbm, kbuf.at[slot], sem.at[0, slot], rpp)
        cv = _page_copies(pt_ref, bb, n_pages, v_hbm, vbuf.at[slot], sem.at[1, slot], rpp)
        return ck, cv

    def start(bb, slot):
        ck, cv = copies(bb, slot)
        lax.fori_loop(0, n_pages, lambda p, c: (ck(p).start(), cv(p).start(), c)[2], 0)

    def wait(bb, slot):
        ck, cv = copies(bb, slot)
        lax.fori_loop(0, n_pages, lambda p, c: (ck(p).wait(), cv(p).wait(), c)[2], 0)

    slot = b % 2

    @pl.when(b == 0)
    def _():
        kbuf[:, n_pages * rpp:, :] = jnp.zeros((2, rpp, B_HEAD_DIM), F32)
        vbuf[:, n_pages * rpp:, :] = jnp.zeros((2, rpp, B_HEAD_DIM), F32)
        start(0, 0)

    @pl.when(b + 1 < nb)
    def _():
        start(b + 1, 1 - slot)

    kbuf[slot, n_pages * rpp:n_pages * rpp + 8, :] = knew_ref[0]
    vbuf[slot, n_pages * rpp:n_pages * rpp + 8, :] = vnew_ref[0]
    wait(b, slot)

    sel = s_ref[0] >= thr_ref[0][:, 0:1]
    n_keys = past + PAGE_SIZE
    outs = []
    for n in range(B_KV_HEADS):
        kn = kbuf[slot, pl.ds(n, n_keys, stride=B_KV_HEADS), :].astype(BF16)
        vn = vbuf[slot, pl.ds(n, n_keys, stride=B_KV_HEADS), :].astype(BF16)
        qn = q_ref[0, n].astype(BF16)
        lg = _nt(qn, kn) * B_HEAD_DIM ** -0.5 + bias_ref[n]
        m = jnp.max(jnp.where(sel, lg, NEG), axis=1, keepdims=True)
        p = jnp.where(sel, jnp.exp(lg - m), 0.0)
        p = p / jnp.sum(p, axis=1, keepdims=True)
        outs.append(jnp.dot(p.astype(BF16), vn, preferred_element_type=F32))
    o_ref[0] = jnp.concatenate(outs, axis=0).astype(o_ref.dtype)


def dsa_decode(page_table, q8, scores, thr, knew8, vnew8, bias_rows, cache_k2, cache_v2):
    b, n_pages = page_table.shape
    past = n_pages * PAGE_SIZE
    l = past + LANES
    wkv = B_KV_HEADS * B_HEAD_DIM
    kern = functools.partial(_dsa_decode_kernel, n_pages=n_pages)
    gs = pltpu.PrefetchScalarGridSpec(
        num_scalar_prefetch=1,
        grid=(b,),
        in_specs=[pl.BlockSpec((1, B_KV_HEADS, 8, B_HEAD_DIM), lambda i, pt: (i, 0, 0, 0)),
                  pl.BlockSpec((1, 1, l), lambda i, pt: (i, 0, 0)),
                  pl.BlockSpec((1, 1, LANES), lambda i, pt: (i, 0, 0)),
                  pl.BlockSpec((1, 8, B_HEAD_DIM), lambda i, pt: (i, 0, 0)),
                  pl.BlockSpec((1, 8, B_HEAD_DIM), lambda i, pt: (i, 0, 0)),
                  pl.BlockSpec((B_KV_HEADS, 8, l), lambda i, pt: (0, 0, 0)),
                  pl.BlockSpec(memory_space=pl.ANY),
                  pl.BlockSpec(memory_space=pl.ANY)],
        out_specs=pl.BlockSpec((1, 2 * 8, B_HEAD_DIM), lambda i, pt: (i, 0, 0)),
        scratch_shapes=[pltpu.VMEM((2, l * B_KV_HEADS, B_HEAD_DIM), F32),
                        pltpu.VMEM((2, l * B_KV_HEADS, B_HEAD_DIM), F32),
                        pltpu.SemaphoreType.DMA((2, 2))],
    )
    return pl.pallas_call(
        kern, grid_spec=gs,
        out_shape=jax.ShapeDtypeStruct((b, 2 * 8, B_HEAD_DIM), F32),
        compiler_params=_cparams("arbitrary"),
        name="dsa_decode",
    )(page_table, q8, scores, thr, knew8, vnew8, bias_rows, cache_k2, cache_v2)


def _merge_kernel(ya_ref, yb_ref, ga_ref, gb_ref, wa_ref, wb_ref, o_ref):
    a = _mm(ya_ref[...], wa_ref[...])
    bb = _mm(yb_ref[...], wb_ref[...])
    o_ref[...] = (jax.nn.sigmoid(ga_ref[...]) * a + jax.nn.sigmoid(gb_ref[...]) * bb).astype(o_ref.dtype)


def merge(ya, yb, z, ga_col, gb_col, wa, wb, *, tm, tn):
    m, k = ya.shape
    n = wa.shape[1]
    return pl.pallas_call(
        _merge_kernel,
        grid=(m // tm, n // tn),
        in_specs=[pl.BlockSpec((tm, k), lambda i, j: (i, 0)),
                  pl.BlockSpec((tm, k), lambda i, j: (i, 0)),
                  pl.BlockSpec((tm, tn), lambda i, j: (i, ga_col // tn + j)),
                  pl.BlockSpec((tm, tn), lambda i, j: (i, gb_col // tn + j)),
                  pl.BlockSpec((k, tn), lambda i, j: (0, j)),
                  pl.BlockSpec((k, tn), lambda i, j: (0, j))],
        out_specs=pl.BlockSpec((tm, tn), lambda i, j: (i, j)),
        out_shape=jax.ShapeDtypeStruct((m, n), wa.dtype),
        compiler_params=_cparams("parallel", "arbitrary"),
        name="merge",
    )(ya, yb, z, z, wa, wb)


def _cross_prompt_kernel(x_ref, g_ref, wq_ref, mk_ref, mv_ref, wo_ref, o_ref):
    x = x_ref[...]
    ms = jnp.mean(x * x, axis=-1, keepdims=True)
    h = (x * lax.rsqrt(ms + EPS) * g_ref[...]).astype(BF16)
    q = jnp.dot(h, wq_ref[...], preferred_element_type=F32)
    outs = []
    for hh in range(X_HEADS):
        cols = slice(hh * X_HEAD_DIM, (hh + 1) * X_HEAD_DIM)
        lg = _nt(q[:, cols].astype(BF16), mk_ref[:, cols]) * X_HEAD_DIM ** -0.5
        mx = jnp.max(lg, axis=1, keepdims=True)
        p = jnp.exp(lg - mx)
        p = p / jnp.sum(p, axis=1, keepdims=True)
        outs.append(jnp.dot(p.astype(BF16), mv_ref[:, cols], preferred_element_type=F32).astype(BF16))
    att = jnp.concatenate(outs, axis=1)
    o_ref[...] = x + jnp.dot(att, wo_ref[...], preferred_element_type=F32)


def cross_prompt(x, g, wq, mk, mv, wo, *, tm):
    m, d = x.shape
    full = lambda a: pl.BlockSpec(a.shape, lambda i: (0,) * a.ndim)
    g2 = g.reshape(1, d)
    return pl.pallas_call(
        _cross_prompt_kernel,
        grid=(m // tm,),
        in_specs=[pl.BlockSpec((tm, d), lambda i: (i, 0)), full(g2), full(wq), full(mk), full(mv), full(wo)],
        out_specs=pl.BlockSpec((tm, d), lambda i: (i, 0)),
        out_shape=jax.ShapeDtypeStruct((m, d), F32),
        compiler_params=_cparams("parallel"),
        name="cross_prompt",
    )(x, g2, wq, mk, mv, wo)


def _cross_step_kernel(q_ref, mk_ref, mv_ref, o_ref):
    r16 = lambda a: a.astype(BF16).astype(F32)
    q = r16(q_ref[0])
    outs = []
    for hh in range(X_HEADS):
        cols = slice(hh * X_HEAD_DIM, (hh + 1) * X_HEAD_DIM)
        kh = r16(mk_ref[0, :, cols])
        vh = r16(mv_ref[0, :, cols])
        lg = jnp.sum(kh * q[:, cols], axis=1, keepdims=True) * X_HEAD_DIM ** -0.5
        mx = jnp.max(lg, axis=0, keepdims=True)
        p = jnp.exp(lg - mx)
        p = r16(p / jnp.sum(p, axis=0, keepdims=True))
        outs.append(jnp.sum(p * vh, axis=0, keepdims=True))
    o_ref[0] = jnp.concatenate(outs, axis=1).astype(o_ref.dtype)


def cross_step(q, mk, mv):
    b, w = q.shape
    mem = mk.shape[1]
    return pl.pallas_call(
        _cross_step_kernel,
        grid=(b,),
        in_specs=[pl.BlockSpec((1, 1, w), lambda i: (i, 0, 0)),
                  pl.BlockSpec((1, mem, w), lambda i: (i, 0, 0)),
                  pl.BlockSpec((1, mem, w), lambda i: (i, 0, 0))],
        out_specs=pl.BlockSpec((1, 1, w), lambda i: (i, 0, 0)),
        out_shape=jax.ShapeDtypeStruct((b, 1, w), F32),
        compiler_params=_cparams("arbitrary"),
        name="cross_step",
    )(q.reshape(b, 1, w), mk, mv)


def _router_kernel(x_ref, g_ref, w_ref, b_ref, hf_ref, route_ref):
    x = x_ref[...]
    ms = jnp.mean(x * x, axis=-1, keepdims=True)
    hf = x * lax.rsqrt(ms + EPS) * g_ref[...]
    hf_ref[...] = hf
    lg = _mm(hf, w_ref[...]) + b_ref[...]
    tm = lg.shape[0]
    lane = lax.broadcasted_iota(jnp.int32, (tm, LANES), 1)
    big = jnp.int32(LANES)
    is_g = lane < N_GROUPS
    gmax = jnp.max(jnp.where(is_g, lg, -jnp.inf), axis=1, keepdims=True)
    grp = jnp.min(jnp.where(is_g & (lg == gmax), lane, big), axis=1, keepdims=True)
    p_grp = 1.0 / jnp.sum(jnp.where(is_g, jnp.exp(lg - gmax), 0.0), axis=1, keepdims=True)
    e_lo = N_GROUPS + grp * EXP_PER_GROUP
    in_g = (lane >= e_lo) & (lane < e_lo + EXP_PER_GROUP)
    v1 = jnp.max(jnp.where(in_g, lg, -jnp.inf), axis=1, keepdims=True)
    i1 = jnp.min(jnp.where(in_g & (lg == v1), lane, big), axis=1, keepdims=True)
    rest = in_g & (lane != i1)
    v2 = jnp.max(jnp.where(rest, lg, -jnp.inf), axis=1, keepdims=True)
    i2 = jnp.min(jnp.where(rest & (lg == v2), lane, big), axis=1, keepdims=True)
    e2 = jnp.exp(v2 - v1)
    g1 = p_grp / (1.0 + e2)
    g2 = p_grp * e2 / (1.0 + e2)
    r = jnp.where(lane == 0, (i1 - N_GROUPS).astype(F32),
                  jnp.where(lane == 1, (i2 - N_GROUPS).astype(F32),
                            jnp.where(lane == 2, g1, jnp.where(lane == 3, g2, 0.0))))
    route_ref[...] = r


def router(x, g, w_pad, b_pad, *, tm):
    m, d = x.shape
    return pl.pallas_call(
        _router_kernel,
        grid=(pl.cdiv(m, tm),),
        in_specs=[pl.BlockSpec((tm, d), lambda i: (i, 0)),
                  pl.BlockSpec((1, d), lambda i: (0, 0)),
                  pl.BlockSpec((d, LANES), lambda i: (0, 0)),
                  pl.BlockSpec((1, LANES), lambda i: (0, 0))],
        out_specs=[pl.BlockSpec((tm, d), lambda i: (i, 0)),
                   pl.BlockSpec((tm, LANES), lambda i: (i, 0))],
        out_shape=[jax.ShapeDtypeStruct((m, d), F32), jax.ShapeDtypeStruct((m, LANES), F32)],
        compiler_params=_cparams("parallel"),
        name="router",
    )(x, g.reshape(1, d), w_pad, b_pad)


def _moe_kernel(be_ref, nxt_ref, nused_ref, sbase_ref, nvalid_ref, padj_ref, order_ref, hf_hbm, wg_hbm, wu_hbm,
                wd_hbm, y_hbm, xbuf, obuf, wg_f, wu_f, wd_f, wg_s, wu_s, wd_s, sem_in, sem_out, sem_w,
                *, n_tokens):
    b = pl.program_id(0)
    nb = pl.num_programs(0)
    blk = xbuf.shape[1]
    slot = b % 2
    used = b < nused_ref[0]
    next_used = b + 1 < nused_ref[0]
    n_assign = order_ref.shape[0]

    def slot_info(bb):
        base, nv, pad0 = sbase_ref[bb], nvalid_ref[bb], padj_ref[bb]

        def info(r):
            valid = r < nv
            asg = order_ref[jnp.minimum(base + r, n_assign - 1)]
            tok = lax.shift_right_logical(asg, 1)
            return jnp.where(valid, tok, 0), jnp.where(valid, (asg & 1) * n_tokens + tok, pad0 + r)
        return info

    def weight_copies(e):
        return (pltpu.make_async_copy(wg_hbm.at[e], wg_f, sem_w.at[0]),
                pltpu.make_async_copy(wu_hbm.at[e], wu_f, sem_w.at[1]),
                pltpu.make_async_copy(wd_hbm.at[e], wd_f, sem_w.at[2]))

    def gather_start(bb, sl):
        info = slot_info(bb)
        for r in range(blk):
            pltpu.make_async_copy(hf_hbm.at[pl.ds(info(r)[0], 1)], xbuf.at[sl, pl.ds(r, 1)],
                                  sem_in.at[sl]).start()

    def scatter_start(bb, sl):
        info = slot_info(bb)
        for r in range(blk):
            pltpu.make_async_copy(obuf.at[sl, pl.ds(r, 1)], y_hbm.at[pl.ds(info(r)[1], 1)],
                                  sem_out.at[sl]).start(priority=r % 2)

    def gather_wait(sl):
        pltpu.make_async_copy(hf_hbm.at[pl.ds(0, blk)], xbuf.at[sl], sem_in.at[sl]).wait()

    def scatter_wait(sl):
        pltpu.make_async_copy(obuf.at[sl], y_hbm.at[pl.ds(0, blk)], sem_out.at[sl]).wait()

    @pl.when(b == 0)
    def _():
        for cp in weight_copies(be_ref[0]):
            cp.start(priority=1)
        gather_start(0, 0)
        obuf[...] = jnp.zeros(obuf.shape, F32)

    changed = jnp.logical_and(used, jnp.logical_or(b == 0, be_ref[b] != be_ref[jnp.maximum(b - 1, 0)]))

    @pl.when(changed)
    def _():
        for cp in weight_copies(be_ref[b]):
            cp.wait()
        wg_s[...] = wg_f[...].astype(BF16)
        wu_s[...] = wu_f[...].astype(BF16)
        wd_s[...] = wd_f[...].astype(BF16)

    @pl.when(jnp.logical_and(changed, nxt_ref[b] >= 0))
    def _():
        for cp in weight_copies(nxt_ref[b]):
            cp.start(priority=1)

    @pl.when(used)
    def _():
        gather_wait(slot)

    @pl.when(b >= 2)
    def _():
        scatter_wait(slot)

    def step(prefetch, flush_prev, compute):
        if prefetch:
            gather_start(b + 1, 1 - slot)
        if flush_prev:
            scatter_start(b - 1, 1 - slot)
        if compute:
            x = xbuf[slot].astype(BF16)
            gg = jnp.dot(x, wg_s[...], preferred_element_type=F32)
            uu = jnp.dot(x, wu_s[...], preferred_element_type=F32)
            a = (_silu(gg) * uu).astype(BF16)
            obuf[slot] = jnp.dot(a, wd_s[...], preferred_element_type=F32)

    first, last = b == 0, b == nb - 1
    land, lnot = jnp.logical_and, jnp.logical_not
    pl.when(land(first, next_used))(lambda: step(True, False, True))
    pl.when(land(first, lnot(next_used)))(lambda: step(False, False, True))
    pl.when(land(lnot(first), land(used, next_used)))(lambda: step(True, True, True))
    pl.when(land(lnot(first), land(used, lnot(next_used))))(lambda: step(False, True, True))
    pl.when(land(lnot(first), lnot(used)))(lambda: step(False, True, False))

    @pl.when(last)
    def _():
        scatter_start(b, slot)
        scatter_wait(slot)

    @pl.when(jnp.logical_and(last, jnp.logical_not(first)))
    def _():
        scatter_wait(1 - slot)


def moe_experts(tables, hf, w_g, w_u, w_d, *, blk, out_rows):
    n_blocks = tables[0].shape[0]
    d = hf.shape[1]
    ff = w_g.shape[2]
    gs = pltpu.PrefetchScalarGridSpec(
        num_scalar_prefetch=len(tables),
        grid=(n_blocks,),
        in_specs=[pl.BlockSpec(memory_space=pl.ANY)] * 4,
        out_specs=pl.BlockSpec(memory_space=pl.ANY),
        scratch_shapes=[pltpu.VMEM((2, blk, d), F32), pltpu.VMEM((2, blk, d), F32),
                        pltpu.VMEM((d, ff), F32), pltpu.VMEM((d, ff), F32), pltpu.VMEM((ff, d), F32),
                        pltpu.VMEM((d, ff), BF16), pltpu.VMEM((d, ff), BF16), pltpu.VMEM((ff, d), BF16),
                        pltpu.SemaphoreType.DMA((2,)), pltpu.SemaphoreType.DMA((2,)),
                        pltpu.SemaphoreType.DMA((3,))],
    )
    return pl.pallas_call(
        functools.partial(_moe_kernel, n_tokens=hf.shape[0]), grid_spec=gs,
        out_shape=jax.ShapeDtypeStruct((out_rows, d), F32),
        compiler_params=_cparams("arbitrary"),
        name="moe_experts",
    )(*tables, hf, w_g, w_u, w_d)


def _combine_kernel(x_ref, route_ref, gf_ref, y1_ref, y2_ref, op_ref, os_ref):
    i = pl.program_id(0)
    route = route_ref[...]
    x = x_ref[...] + route[:, 2:3] * y1_ref[...] + route[:, 3:4] * y2_ref[...]
    ms = jnp.mean(x * x, axis=-1, keepdims=True)
    out = x * lax.rsqrt(ms + EPS) * gf_ref[...]

    @pl.when(i < pl.num_programs(0) - 1)
    def _():
        op_ref[...] = out

    @pl.when(i == pl.num_programs(0) - 1)
    def _():
        os_ref[...] = out[:os_ref.shape[0]]


def combine(x, route, gf, y, plane, n_prompt, *, tm):
    m, d = x.shape
    n_tiles = n_prompt // tm
    assert n_prompt % tm == 0 and 0 < m - n_prompt <= tm
    return pl.pallas_call(
        _combine_kernel,
        grid=(n_tiles + 1,),
        in_specs=[pl.BlockSpec((tm, d), lambda i: (i, 0)),
                  pl.BlockSpec((tm, LANES), lambda i: (i, 0)),
                  pl.BlockSpec((1, d), lambda i: (0, 0)),
                  pl.BlockSpec((tm, d), lambda i: (i, 0)),
                  pl.BlockSpec((tm, d), lambda i: (plane // tm + i, 0))],
        out_specs=[pl.BlockSpec((tm, d), lambda i: (jnp.minimum(i, n_tiles - 1), 0)),
                   pl.BlockSpec((m - n_prompt, d), lambda i: (0, 0))],
        out_shape=[jax.ShapeDtypeStruct((n_prompt, d), F32), jax.ShapeDtypeStruct((m - n_prompt, d), F32)],
        compiler_params=_cparams("arbitrary"),
        name="combine",
    )(x, route, gf.reshape(1, d), y, y)


def _t5_bucket(dist):
    dist = jnp.asarray(dist, jnp.int32)
    max_exact = REL_BUCKETS // 2
    dist_f = jnp.maximum(dist, 1).astype(F32)
    large = max_exact + (jnp.log(dist_f / max_exact) / math.log(REL_MAX_DIST / max_exact)
                         * (REL_BUCKETS - max_exact)).astype(jnp.int32)
    large = jnp.minimum(large, REL_BUCKETS - 1)
    return jnp.where(dist < max_exact, dist, large)


def _bias_tables(rel_bias, past):
    r = np.arange(LANES)
    diff = r[:, None] - r[None, :]
    buckets = jnp.stack([_t5_bucket(np.maximum(diff, 0)),
                         _t5_bucket(np.maximum(diff + LANES, 0)),
                         _t5_bucket(np.full((LANES, LANES), 2 * LANES))])
    def lookup(bkt):
        oh = (bkt.reshape(-1, 1) == jnp.arange(REL_BUCKETS)[None, :]).astype(F32)
        out = jnp.dot(oh, rel_bias.astype(F32), precision=lax.Precision.HIGHEST)
        return out.T.reshape((rel_bias.shape[1],) + bkt.shape)

    tiles = lookup(buckets)
    tiles = tiles - tiles[:, 2:3]
    dist = np.maximum(past - np.arange(past + LANES), 0)
    rows = lookup(_t5_bucket(dist))
    rows = rows.reshape(B_KV_HEADS, B_GROUP, past + LANES)
    rows = jnp.concatenate([rows, jnp.zeros_like(rows)], axis=1)
    return tiles, rows


def _dispatch(eid, n_tokens, blk):
    a = eid.shape[0]
    assert EXPERT_TOPK == 2 and a == EXPERT_TOPK * n_tokens
    n_blocks = -(-(a + N_EXPERTS * (blk - 1)) // blk)
    rows = n_blocks * blk
    order = jnp.argsort(eid).astype(jnp.int32)
    counts = jnp.sum(eid[:, None] == jnp.arange(N_EXPERTS)[None, :], axis=0).astype(jnp.int32)
    cum = jnp.cumsum(counts)
    starts = cum - counts
    padded = (counts + blk - 1) // blk * blk
    pad_end = jnp.cumsum(padded)
    pad_start = pad_end - padded
    blocks = jnp.arange(n_blocks, dtype=jnp.int32)
    block_e = jnp.minimum(jnp.sum(pad_end[None, :] <= (blocks * blk)[:, None], axis=1), N_EXPERTS - 1)
    n_used = pad_end[-1] // blk
    off = blocks * blk - pad_start[block_e]
    sbase = starts[block_e] + off
    nvalid = jnp.where(blocks < n_used, jnp.clip(counts[block_e] - off, 0, blk), 0)
    padj = a + blocks * blk - cum[block_e]
    new_run = jnp.concatenate([jnp.array([True]), block_e[1:] != block_e[:-1]])
    run_start = jnp.where(new_run & (blocks < n_used), blocks, n_blocks)
    nxt = jnp.concatenate([jnp.flip(lax.cummin(jnp.flip(run_start)))[1:], jnp.array([n_blocks])])
    next_e = jnp.where(nxt < n_blocks, block_e[jnp.minimum(nxt, n_blocks - 1)], -1)
    i32 = lambda v: v.astype(jnp.int32)
    return (i32(block_e), i32(next_e), i32(n_used).reshape(1), i32(sbase), i32(nvalid), i32(padj), order), rows


def kernel(x_prompt, x_sample, mem_prompt, cache_k, cache_v, cache_kidx, page_table, state_hgrn, cache_mem_k,
           cache_mem_v, norm_mix, w_in, hgrn_lb_logits, hgrn_norm, w_branch_a, w_branch_b, w_out, norm_cross, w_xq,
           w_xk, w_xv, w_xo, norm_ffn, w_router_group, b_router_group, w_router_expert, b_router_expert, w_exp_gate,
           w_exp_up, w_exp_down, rel_bias, norm_final):
    assert w_in.shape[0] == 1, "single-layer step"
    l = 0
    drop0 = lambda a: a.reshape(a.shape[1:])
    bp, t, d = x_prompt.shape
    db = x_sample.shape[0]
    past = page_table.shape[1] * PAGE_SIZE
    xp = x_prompt.reshape(bp * t, d)
    xs = x_sample.reshape(db, d)

    wi_t = jnp.transpose(drop0(w_in)).astype(BF16)
    assert wi_t.shape[0] - NZ_MAIN + T_PAD == NZ_TAIL
    w_tail_t = jnp.pad(wi_t[NZ_MAIN:], ((T_PAD, 0), (0, 0)))
    wa, wb, wo = w_branch_a[l].astype(BF16), w_branch_b[l].astype(BF16), w_out[l].astype(BF16)
    wxq, wxk, wxv, wxo = (w_xq[l].astype(BF16), w_xk[l].astype(BF16), w_xv[l].astype(BF16), w_xo[l].astype(BF16))
    w_route = jnp.pad(jnp.concatenate([w_router_group[l], w_router_expert[l]], axis=1),
                      ((0, 0), (0, LANES - N_GROUPS - N_EXPERTS))).astype(BF16)
    b_route = jnp.pad(jnp.concatenate([b_router_group[l], b_router_expert[l]]),
                      (0, LANES - N_GROUPS - N_EXPERTS)).reshape(1, LANES)
    bias_tiles, bias_rows = _bias_tables(rel_bias, past)

    zp, zpt = in_proj(xp, norm_mix[l], wi_t, NZ_MAIN, w_tail_t, tm=1024, tn=512)
    kp = zp[:, C_BK:C_BK + 256]
    vp = zp[:, C_BV:C_BV + 256]
    ikp = zpt[:, T_SM + IK_LANE:T_SM + IK_LANE + IDX_DIM]
    ya_p, st_p = hgrn_prompt(zp, hgrn_lb_logits, hgrn_norm[l])
    yb_p = dsa_prompt(zp, zpt, ikp.astype(BF16), kp.astype(BF16), vp.astype(BF16), bias_tiles)
    mg_p = merge(ya_p, yb_p, zpt, T_GA, T_GB, wa, wb, tm=1024, tn=512)
    x1p = matmul(mg_p, wo, xp, tm=1024, tn=512)
    memp = mem_prompt.reshape(-1, d)
    mk = matmul(memp, wxk, tm=memp.shape[0], tn=512)
    mv = matmul(memp, wxv, tm=memp.shape[0], tn=512)
    x2p = cross_prompt(x1p, norm_cross[l], wxq, mk.astype(BF16), mv.astype(BF16), wxo, tm=512)

    zs, zst = in_proj(xs, norm_mix[l], wi_t, NZ_MAIN, w_tail_t, tm=db, tn=512)
    ks = zs[:, C_BK:C_BK + 256]
    vs = zs[:, C_BV:C_BV + 256]
    iks = zst[:, T_SM + IK_LANE:T_SM + IK_LANE + IDX_DIM]
    ya_s, st_s = hgrn_step(zs[:, :4 * A_HEADS * A_DK], hgrn_lb_logits, hgrn_norm[l], drop0(state_hgrn))
    iq_s = zs[:, C_IQ:C_IQ + IDX_HEADS * IDX_DIM].reshape(db, IDX_HEADS, IDX_DIM)
    iw_s = zst[:, T_SM + IW_LANE:T_SM + IW_LANE + IDX_HEADS].reshape(db, IDX_HEADS, 1)
    iknew_pad = jnp.pad(iks[:, :, None], ((0, 0), (0, 0), (0, LANES - 1)))
    scores = dsa_scores(page_table, iq_s, iw_s, iknew_pad,
                        jnp.swapaxes(drop0(cache_kidx), 1, 2)).reshape(db, past + LANES)
    topk_s = min(TOPK_MAX, (past + 1) // 4)
    thr = dsa_threshold(scores, topk_s)
    q8 = jnp.pad(zs[:, C_BQ:C_BQ + B_HEADS * B_HEAD_DIM].reshape(db, B_KV_HEADS, B_GROUP, B_HEAD_DIM),
                 ((0, 0), (0, 0), (0, 8 - B_GROUP), (0, 0)))
    knew8 = jnp.pad(ks.reshape(db, B_KV_HEADS, B_HEAD_DIM), ((0, 0), (0, 8 - B_KV_HEADS), (0, 0)))
    vnew8 = jnp.pad(vs.reshape(db, B_KV_HEADS, B_HEAD_DIM), ((0, 0), (0, 8 - B_KV_HEADS), (0, 0)))
    n_pool = cache_k.shape[1]
    ob = dsa_decode(page_table, q8, scores.reshape(db, 1, -1), thr.reshape(db, 1, LANES), knew8, vnew8, bias_rows,
                    cache_k.reshape(n_pool, PAGE_SIZE * B_KV_HEADS, B_HEAD_DIM),
                    cache_v.reshape(n_pool, PAGE_SIZE * B_KV_HEADS, B_HEAD_DIM))
    yb_s = ob.reshape(db, B_KV_HEADS, 8, B_HEAD_DIM)[:, :, :B_GROUP].reshape(db, B_HEADS * B_HEAD_DIM)
    mg_s = merge(ya_s.reshape(db, -1), yb_s, zst, T_GA, T_GB, wa, wb, tm=db, tn=512)
    x1s = matmul(mg_s, wo, xs, tm=db, tn=512)
    qx_s = norm_matmul(x1s, norm_cross[l], wxq, tm=db, tn=512)
    mem = cache_mem_k.shape[2]
    att_s = cross_step(qx_s, cache_mem_k.reshape(db, mem, -1), cache_mem_v.reshape(db, mem, -1))
    x2s = matmul(att_s.reshape(db, -1), wxo, x1s, tm=db, tn=512)

    tmc = 128
    n = -(-(bp * t + db) // tmc) * tmc
    x2 = jnp.concatenate([x2p, x2s, jnp.zeros((n - bp * t - db, d), F32)], axis=0)
    hf, route = router(x2, norm_ffn[l], w_route, b_route, tm=256)
    eid = route[:, :EXPERT_TOPK].astype(jnp.int32).reshape(-1)
    tables, rows = _dispatch(eid, n, MOE_ROWS)
    ye = moe_experts(tables, hf, drop0(w_exp_gate), drop0(w_exp_up), drop0(w_exp_down), blk=MOE_ROWS, out_rows=rows)
    y_p, y_s = combine(x2, route, norm_final, ye, n, bp * t, tm=tmc)

    y_prompt = y_p.reshape(bp, t, d)
    y_sample = y_s[:db].reshape(db, 1, d)
    return (y_prompt, y_sample,
            kp.reshape(1, bp, t, B_KV_HEADS, B_HEAD_DIM), vp.reshape(1, bp, t, B_KV_HEADS, B_HEAD_DIM),
            ikp.reshape(1, bp, t, IDX_DIM),
            jnp.swapaxes(st_p, 1, 2).reshape(1, bp, A_HEADS, A_DK, A_DV),
            mk.reshape(1, bp, -1, X_HEADS, X_HEAD_DIM), mv.reshape(1, bp, -1, X_HEADS, X_HEAD_DIM),
            ks.reshape(1, db, 1, B_KV_HEADS, B_HEAD_DIM), vs.reshape(1, db, 1, B_KV_HEADS, B_HEAD_DIM),
            iks.reshape(1, db, 1, IDX_DIM),
            st_s.reshape(1, db, A_HEADS, A_DK, A_DV))
```

```python
import functools
import math

import jax
import jax.numpy as jnp
import numpy as np
from jax import lax
from jax.experimental import pallas as pl
from jax.experimental.pallas import tpu as pltpu

F32 = jnp.float32
BF16 = jnp.bfloat16
EPS = 1e-6

D_MODEL = 2048
A_HEADS, A_DK, A_DV = 8, 128, 128
B_HEADS, B_KV_HEADS, B_HEAD_DIM = 8, 2, 128
B_GROUP = B_HEADS // B_KV_HEADS
IDX_HEADS, IDX_DIM = 16, 64
TOPK_MAX = 256
PAGE_SIZE = 128
REL_BUCKETS, REL_MAX_DIST = 32, 128
X_HEADS, X_HEAD_DIM = 4, 128
N_GROUPS, EXP_PER_GROUP = 4, 8
N_EXPERTS = N_GROUPS * EXP_PER_GROUP
EXPERT_TOPK = 2
EXPERT_FF = 512
MOE_ROWS = 256

LANES = 128
VMEM_LIMIT = 56 * 1024 * 1024

NEG = -1e30

C_AQ, C_AF, C_AI, C_AG, C_BQ, C_BK, C_BV, C_IQ = 0, 1024, 2048, 3072, 4096, 5120, 5376, 5632
NZ_MAIN = 6656
T_PAD = 432
T_SM, IW_LANE, IK_LANE = 384, 48, 64
T_GA, T_GB = 512, 2560
NZ_TAIL = 4608


def _cparams(*sem):
    return pltpu.CompilerParams(dimension_semantics=sem, vmem_limit_bytes=VMEM_LIMIT)


def _silu(x):
    return x * jax.nn.sigmoid(x)


def _nt(a, b):
    return lax.dot_general(a, b, (((1,), (1,)), ((), ())), preferred_element_type=F32)


def _nt_f32(a, b):
    return lax.dot_general(a, b, (((1,), (1,)), ((), ())), preferred_element_type=F32,
                           precision=lax.Precision.HIGHEST)


def _mm(a, w):
    return jnp.dot(a.astype(BF16), w.astype(BF16), preferred_element_type=F32)


def _norm_matmul_kernel(x_ref, g_ref, w_ref, o_ref, h_ref):
    @pl.when(pl.program_id(1) == 0)
    def _():
        x = x_ref[...]
        ms = jnp.mean(x * x, axis=-1, keepdims=True)
        h_ref[...] = (x * lax.rsqrt(ms + EPS) * g_ref[...]).astype(h_ref.dtype)

    o_ref[...] = _mm(h_ref[...], w_ref[...])


def norm_matmul(x, g, w, *, tm, tn):
    m, k = x.shape
    n = w.shape[1]
    assert n % tn == 0
    return pl.pallas_call(
        _norm_matmul_kernel,
        grid=(m // tm, n // tn),
        in_specs=[pl.BlockSpec((tm, k), lambda i, j: (i, 0)),
                  pl.BlockSpec((1, k), lambda i, j: (0, 0)),
                  pl.BlockSpec((k, tn), lambda i, j: (0, j))],
        out_specs=pl.BlockSpec((tm, tn), lambda i, j: (i, j)),
        out_shape=jax.ShapeDtypeStruct((m, n), F32),
        scratch_shapes=[pltpu.VMEM((tm, k), BF16)],
        compiler_params=_cparams("parallel", "arbitrary"),
        name="norm_matmul",
    )(x, g.reshape(1, k), w)


def _in_proj_kernel(x_ref, g_ref, wa_ref, wb_ref, oa_ref, ob_ref, h_ref, *, na):
    j = pl.program_id(1)

    @pl.when(j == 0)
    def _():
        x = x_ref[...]
        ms = jnp.mean(x * x, axis=-1, keepdims=True)
        h_ref[...] = (x * lax.rsqrt(ms + EPS) * g_ref[...]).astype(h_ref.dtype)

    @pl.when(j < na)
    def _():
        oa_ref[...] = _nt(h_ref[...], wa_ref[...])

    @pl.when(j >= na)
    def _():
        ob_ref[...] = _nt(h_ref[...], wb_ref[...])


def in_proj(x, g, wa_t, a_rows, wb_t, *, tm, tn):
    m, k = x.shape
    na, nb = a_rows // tn, wb_t.shape[0] // tn
    assert a_rows % tn == 0 and wb_t.shape[0] % tn == 0
    a_idx = lambda j: jnp.minimum(j, na - 1)
    b_idx = lambda j: jnp.maximum(j - na, 0)
    return pl.pallas_call(
        functools.partial(_in_proj_kernel, na=na),
        grid=(m // tm, na + nb),
        in_specs=[pl.BlockSpec((tm, k), lambda i, j: (i, 0)),
                  pl.BlockSpec((1, k), lambda i, j: (0, 0)),
                  pl.BlockSpec((tn, k), lambda i, j: (a_idx(j), 0)),
                  pl.BlockSpec((tn, k), lambda i, j: (b_idx(j), 0))],
        out_specs=[pl.BlockSpec((tm, tn), lambda i, j: (i, a_idx(j))),
                   pl.BlockSpec((tm, tn), lambda i, j: (i, b_idx(j)))],
        out_shape=[jax.ShapeDtypeStruct((m, na * tn), F32), jax.ShapeDtypeStruct((m, nb * tn), F32)],
        scratch_shapes=[pltpu.VMEM((tm, k), BF16)],
        compiler_params=_cparams("parallel", "arbitrary"),
        name="in_proj",
    )(x, g.reshape(1, k), wa_t, wb_t)


def _matmul_res_kernel(x_ref, w_ref, r_ref, o_ref):
    o_ref[...] = r_ref[...] + _mm(x_ref[...], w_ref[...])


def _matmul_kernel(x_ref, w_ref, o_ref):
    o_ref[...] = _mm(x_ref[...], w_ref[...])


def matmul(x, w, res=None, *, tm, tn):
    m, k = x.shape
    n = w.shape[1]
    in_specs = [pl.BlockSpec((tm, k), lambda i, j: (i, 0)),
                pl.BlockSpec((k, tn), lambda i, j: (0, j))]
    args = [x, w]
    kern = _matmul_kernel
    if res is not None:
        in_specs.append(pl.BlockSpec((tm, tn), lambda i, j: (i, j)))
        args.append(res)
        kern = _matmul_res_kernel
    return pl.pallas_call(
        kern,
        grid=(m // tm, n // tn),
        in_specs=in_specs,
        out_specs=pl.BlockSpec((tm, tn), lambda i, j: (i, j)),
        out_shape=jax.ShapeDtypeStruct((m, n), F32),
        compiler_params=_cparams("parallel", "arbitrary"),
        name="matmul",
    )(*args)


HG_TB = 128
HG_C = 16
HG_H = HG_C // 2


def _hgrn_prompt_kernel(aq_ref, af_ref, ai_ref, ag_ref, lbl_ref, ng_ref, ya_ref, st_out_ref,
                        st_ref, q_s, k_s, g_s, v_s):
    t = pl.program_id(0)

    @pl.when(t == 0)
    def _():
        st_ref[...] = jnp.zeros_like(st_ref)

    lbl = lbl_ref[...]
    mx = jnp.max(lbl, axis=0, keepdims=True)
    ex = jnp.exp(lbl - mx)
    lb = ex[0:1, :] / jnp.sum(ex, axis=0, keepdims=True)

    f = lb + (1.0 - lb) * jax.nn.sigmoid(af_ref[...])
    logf = jnp.log(f)
    row = lax.broadcasted_iota(jnp.int32, (HG_TB, HG_TB), 0)
    col = lax.broadcasted_iota(jnp.int32, (HG_TB, HG_TB), 1)
    tri = jnp.where((row // HG_C == col // HG_C) & (col <= row), 1.0, 0.0).astype(F32)
    g_s[...] = jnp.dot(tri, logf, preferred_element_type=F32, precision=lax.Precision.HIGHEST)
    q_s[...] = _silu(aq_ref[...])
    k_s[...] = 1.0 - f
    v_s[...] = ai_ref[...]

    sub = lax.broadcasted_iota(jnp.int32, (HG_C, A_DK), 0)
    sub8 = lax.broadcasted_iota(jnp.int32, (HG_H, A_DK), 0)
    ng = ng_ref[...]

    def chunk(c, carry):
        r0 = pl.multiple_of(c * HG_C, HG_C)
        rows = pl.ds(r0, HG_C)
        for h in range(A_HEADS):
            cols = slice(h * A_DK, (h + 1) * A_DK)
            g = g_s[rows, cols]
            qh = q_s[rows, cols]
            kh = k_s[rows, cols]
            vh = v_s[rows, cols]
            halves = []
            for hb in range(2):
                rs = slice(hb * HG_H, (hb + 1) * HG_H)
                gb, qb, kb, vb = g[rs], qh[rs], kh[rs], vh[rs]
                ob = jnp.zeros((HG_H, A_DV), F32)
                for tt in range(HG_H):
                    d = gb[tt:tt + 1, :] - gb
                    e = jnp.exp(jnp.where(sub8 <= tt, d, -jnp.inf))
                    p = e * (qb[tt:tt + 1, :] * kb)
                    a_col = jnp.sum(p, axis=1, keepdims=True)
                    o_row = jnp.sum(a_col * vb, axis=0, keepdims=True)
                    ob = jnp.where(sub8 == tt, o_row, ob)
                halves.append(ob)
            o = jnp.concatenate(halves, axis=0)
            low = sub < HG_H
            g_mid = g[HG_H - 1:HG_H, :]
            q_hi = jnp.where(low, 0.0, qh * jnp.exp(jnp.minimum(g - g_mid, 0.0)))
            k_lo = jnp.where(low, kh * jnp.exp(jnp.minimum(g_mid - g, 0.0)), 0.0)
            st = st_ref[h]
            g_last = g[HG_C - 1:HG_C, :]
            kt = kh * jnp.exp(g_last - g)
            upd = lax.dot_general(vh.astype(BF16), jnp.concatenate([kt, k_lo], axis=1).astype(BF16),
                                  (((0,), (0,)), ((), ())), preferred_element_type=F32)
            lhs = jnp.concatenate([qh * jnp.exp(g), q_hi], axis=1).astype(BF16)
            rhs = jnp.concatenate([st, upd[:, A_DK:]], axis=1).astype(BF16)
            o = o + _nt(lhs, rhs)
            st_ref[h] = st * jnp.exp(g_last) + upd[:, :A_DK]
            on = o * lax.rsqrt(jnp.mean(o * o, axis=-1, keepdims=True) + EPS) * ng
            ya_ref[rows, cols] = (on * _silu(ag_ref[rows, cols])).astype(ya_ref.dtype)
        return carry

    lax.fori_loop(0, HG_TB // HG_C, chunk, 0)

    @pl.when(t == pl.num_programs(0) - 1)
    def _():
        st_out_ref[...] = st_ref[...]


def hgrn_prompt(z, lb_logits, norm_g):
    m = z.shape[0]
    w = A_HEADS * A_DK

    def zspec(cb):
        return pl.BlockSpec((HG_TB, w), lambda t, cb=cb: (t, cb))

    return pl.pallas_call(
        _hgrn_prompt_kernel,
        grid=(m // HG_TB,),
        in_specs=[zspec(C_AQ // w), zspec(C_AF // w), zspec(C_AI // w), zspec(C_AG // w),
                  pl.BlockSpec(lb_logits.shape, lambda t: (0, 0)),
                  pl.BlockSpec((1, A_DV), lambda t: (0, 0))],
        out_specs=[pl.BlockSpec((HG_TB, w), lambda t: (t, 0)),
                   pl.BlockSpec((A_HEADS, A_DV, A_DK), lambda t: (0, 0, 0))],
        out_shape=[jax.ShapeDtypeStruct((m, w), BF16),
                   jax.ShapeDtypeStruct((A_HEADS, A_DV, A_DK), F32)],
        scratch_shapes=[pltpu.VMEM((A_HEADS, A_DV, A_DK), F32)] + [pltpu.VMEM((HG_TB, w), F32)] * 4,
        compiler_params=_cparams("arbitrary"),
        name="hgrn_prompt",
    )(z, z, z, z, lb_logits, norm_g.reshape(1, A_DV))


def _hgrn_step_kernel(z_ref, lbl_ref, ng_ref, s_ref, ya_ref, s_out_ref):
    lbl = lbl_ref[...]
    mx = jnp.max(lbl, axis=0, keepdims=True)
    ex = jnp.exp(lbl - mx)
    lb = ex[0:1, :] / jnp.sum(ex, axis=0, keepdims=True)
    z = z_ref[0]
    w = A_HEADS * A_DK
    q = _silu(z[:, 0:w])
    f = lb + (1.0 - lb) * jax.nn.sigmoid(z[:, w:2 * w])
    kk = 1.0 - f
    v = z[:, 2 * w:3 * w]
    ag = z[:, 3 * w:4 * w]
    rows = []
    for h in range(A_HEADS):
        cols = slice(h * A_DK, (h + 1) * A_DK)
        rows += [f[:, cols], kk[:, cols], q[:, cols]]
    rows.append(jnp.zeros((LANES - 3 * A_HEADS, A_DK), F32))
    xt = jnp.concatenate(rows, axis=0).T
    ng = ng_ref[...]
    r16 = lambda a: a.astype(BF16).astype(F32)
    outs = []
    for h in range(A_HEADS):
        cols = slice(h * A_DV, (h + 1) * A_DV)
        fcol = xt[:, 3 * h:3 * h + 1]
        kcol = xt[:, 3 * h + 1:3 * h + 2]
        qcol = xt[:, 3 * h + 2:3 * h + 3]
        s_old = s_ref[0, h]
        s_out_ref[0, h] = fcol * s_old + kcol * v[:, cols]
        o = (jnp.sum(r16(qcol * fcol) * r16(s_old), axis=0, keepdims=True)
             + jnp.sum(qcol * kcol, axis=0, keepdims=True) * v[:, cols])
        on = o * lax.rsqrt(jnp.mean(o * o, axis=-1, keepdims=True) + EPS) * ng
        outs.append(on * _silu(ag[:, cols]))
    ya_ref[0] = jnp.concatenate(outs, axis=1).astype(ya_ref.dtype)


def hgrn_step(z4, lb_logits, norm_g, state):
    b = z4.shape[0]
    w = A_HEADS * A_DK
    return pl.pallas_call(
        _hgrn_step_kernel,
        grid=(b,),
        in_specs=[pl.BlockSpec((1, 1, 4 * w), lambda i: (i, 0, 0)),
                  pl.BlockSpec(lb_logits.shape, lambda i: (0, 0)),
                  pl.BlockSpec((1, A_DV), lambda i: (0, 0)),
                  pl.BlockSpec((1, A_HEADS, A_DK, A_DV), lambda i: (i, 0, 0, 0))],
        out_specs=[pl.BlockSpec((1, 1, w), lambda i: (i, 0, 0)),
                   pl.BlockSpec((1, A_HEADS, A_DK, A_DV), lambda i: (i, 0, 0, 0))],
        out_shape=[jax.ShapeDtypeStruct((b, 1, w), F32),
                   jax.ShapeDtypeStruct(state.shape, F32)],
        compiler_params=_cparams("arbitrary"),
        name="hgrn_step",
    )(z4.reshape(b, 1, 4 * w), lb_logits, norm_g.reshape(1, A_DV), state)


BISECT_MAX_ITERS = 48


def _bisect_threshold(count_ge, lo, hi, cnt_lo, topk):
    kf = float(topk)

    def cond(c):
        return jnp.logical_and(c[0] < BISECT_MAX_ITERS, c[-1] > 0.0)

    def body(c):
        it, lo, hi, cl, _ = c
        mid = 0.5 * lo + 0.5 * hi
        cm = count_ge(mid)
        ge = cm >= kf
        lo = jnp.where(ge, mid, lo)
        cl = jnp.where(ge, cm, cl)
        hi = jnp.where(ge, hi, mid)
        busy = jnp.max(jnp.where(cl > kf, 1.0, 0.0))
        return it + 1, lo, hi, cl, busy

    busy0 = jnp.max(jnp.where(cnt_lo > kf, 1.0, 0.0))
    out = lax.while_loop(cond, body, (jnp.int32(0), lo, hi, cnt_lo, busy0))
    return out[1]


DSA_QB = 128
DSA_W = 512
DSA_W3 = 1024


def _dsa_prompt_kernel(iq0_ref, iq1_ref, bq_ref, iw_ref, kidx_ref, k_ref, v_ref, bias_ref, o_ref,
                       score_s, qih_s, qs_s, wb_s, m_s, l_s, acc_s, *, topk):
    i = pl.program_id(0)
    nsub = DSA_W // LANES
    nsub3 = DSA_W3 // LANES
    nch3 = (i * DSA_QB + DSA_QB + DSA_W3 - 1) // DSA_W3
    nch = nch3 * (DSA_W3 // DSA_W)
    qpos = i * DSA_QB + lax.broadcasted_iota(jnp.int32, (DSA_QB, 1), 0)

    iw = iw_ref[...]
    wscale = IDX_DIM ** -0.5 * IDX_HEADS ** -0.5
    for h in range(IDX_HEADS):
        iq_ref, hh = (iq0_ref, h) if h < IDX_HEADS // 2 else (iq1_ref, h - IDX_HEADS // 2)
        qih_s[h] = iq_ref[:, hh * IDX_DIM:(hh + 1) * IDX_DIM].astype(BF16)
        wb_s[h] = jnp.broadcast_to(iw[:, IW_LANE + h:IW_LANE + h + 1] * wscale, (DSA_QB, LANES))
    for h in range(B_HEADS):
        qs_s[h // B_GROUP, (h % B_GROUP) * DSA_QB:(h % B_GROUP + 1) * DSA_QB, :] = (
            bq_ref[:, h * B_HEAD_DIM:(h + 1) * B_HEAD_DIM] * B_HEAD_DIM ** -0.5).astype(BF16)

    def p1(c, carry):
        c0 = pl.multiple_of(c * DSA_W, DSA_W)
        kc = kidx_ref[pl.ds(c0, DSA_W), :]
        sc = [jnp.zeros((DSA_QB, LANES), F32) for _ in range(nsub)]
        for h in range(IDX_HEADS):
            s = jnp.maximum(_nt(qih_s[h], kc), 0.0)
            wb = wb_s[h]
            for j in range(nsub):
                sc[j] = sc[j] + s[:, j * LANES:(j + 1) * LANES] * wb
        for j in range(nsub):
            kpos = c0 + j * LANES + lax.broadcasted_iota(jnp.int32, (1, LANES), 1)
            score_s[c * nsub + j] = jnp.where(kpos <= qpos, sc[j], -jnp.inf)
        return carry

    lax.fori_loop(0, nch, p1, 0)

    def stats(c, carry):
        mn, mx = carry
        for j in range(nsub):
            s = score_s[c * nsub + j]
            mx = jnp.maximum(mx, s)
            mn = jnp.minimum(mn, jnp.where(s > -jnp.inf, s, jnp.inf))
        return mn, mx

    mn, mx = lax.fori_loop(0, nch, stats, (jnp.full((DSA_QB, LANES), jnp.inf, F32),
                                           jnp.full((DSA_QB, LANES), -jnp.inf, F32)))
    lo0 = jnp.min(mn, axis=1, keepdims=True)
    hi0 = jnp.max(mx, axis=1, keepdims=True)

    def count_ge(thr):
        thr_b = jnp.broadcast_to(thr, (DSA_QB, LANES))

        def body(c, acc):
            for j in range(nsub):
                s = score_s[c * nsub + j]
                acc = acc + jnp.where(s >= thr_b, 1.0, 0.0)
            return acc

        acc = lax.fori_loop(0, nch, body, jnp.zeros((DSA_QB, LANES), F32))
        return jnp.sum(acc, axis=1, keepdims=True)

    thr = _bisect_threshold(count_ge, lo0, hi0, (qpos + 1).astype(F32), topk)
    thr_b = jnp.broadcast_to(thr, (DSA_QB, LANES))

    m_s[...] = jnp.full(m_s.shape, NEG, F32)
    l_s[...] = jnp.zeros(l_s.shape, F32)
    acc_s[...] = jnp.zeros(acc_s.shape, F32)

    def p3(c, with_bias):
        c0 = pl.multiple_of(c * DSA_W3, DSA_W3)
        madd = jnp.concatenate([jnp.where(score_s[c * nsub3 + j] >= thr_b, 0.0, NEG) for j in range(nsub3)], axis=1)
        kc = k_ref[pl.ds(c0, DSA_W3), :]
        vc = v_ref[pl.ds(c0, DSA_W3), :]
        rel = [i - (c * nsub3 + j) for j in range(nsub3)]

        def scores(n):
            return _nt(qs_s[n], kc[:, n * B_HEAD_DIM:(n + 1) * B_HEAD_DIM])

        def softmax(n, lg):
            lg = lg.reshape(B_GROUP, DSA_QB, DSA_W3) + madd[None]
            if with_bias:
                lg = lg + jnp.stack([jnp.concatenate(
                    [jnp.where(rel[j] == 0, bias_ref[n * B_GROUP + gq, 0],
                               jnp.where(rel[j] == 1, bias_ref[n * B_GROUP + gq, 1], 0.0)) for j in range(nsub3)],
                    axis=1) for gq in range(B_GROUP)])
            m_old = m_s[n]
            m_new = jnp.maximum(m_old, jnp.max(lg, axis=-1, keepdims=True))
            p = jnp.exp(lg - m_new)
            alpha = jnp.exp(m_old - m_new)
            l_s[n] = alpha * l_s[n] + jnp.sum(p, axis=-1, keepdims=True)
            m_s[n] = m_new
            pv = jnp.dot(p.reshape(B_GROUP * DSA_QB, DSA_W3).astype(BF16), vc[:, n * B_HEAD_DIM:(n + 1) * B_HEAD_DIM],
                         preferred_element_type=F32)
            return alpha, pv.reshape(B_GROUP, DSA_QB, B_HEAD_DIM)

        lgs = [scores(n) for n in range(B_KV_HEADS)]
        outs = [softmax(n, lgs[n]) for n in range(B_KV_HEADS)]
        for n in range(B_KV_HEADS):
            acc_s[n] = outs[n][0] * acc_s[n] + outs[n][1]

    n_far = jnp.maximum(i - 1, 0) // nsub3
    lax.fori_loop(0, n_far, lambda c, carry: (p3(c, False), carry)[1], 0)
    lax.fori_loop(n_far, nch3, lambda c, carry: (p3(c, True), carry)[1], 0)

    for h in range(B_HEADS):
        n, gq = h // B_GROUP, h % B_GROUP
        o_ref[:, h * B_HEAD_DIM:(h + 1) * B_HEAD_DIM] = (acc_s[n, gq] / l_s[n, gq]).astype(o_ref.dtype)


def dsa_prompt(z, ztail, kidx_bf, k_bf, v_bf, bias_tiles):
    m = z.shape[0]
    topk = min(TOPK_MAX, m // 4)
    wq = B_HEADS * B_HEAD_DIM
    wi2 = IDX_HEADS * IDX_DIM // 2
    kern = functools.partial(_dsa_prompt_kernel, topk=topk)
    return pl.pallas_call(
        kern,
        grid=(m // DSA_QB,),
        in_specs=[pl.BlockSpec((DSA_QB, wi2), lambda i: (i, C_IQ // wi2)),
                  pl.BlockSpec((DSA_QB, wi2), lambda i: (i, C_IQ // wi2 + 1)),
                  pl.BlockSpec((DSA_QB, wq), lambda i: (i, C_BQ // wq)),
                  pl.BlockSpec((DSA_QB, LANES), lambda i: (i, T_SM // LANES)),
                  pl.BlockSpec(kidx_bf.shape, lambda i: (0, 0)),
                  pl.BlockSpec(k_bf.shape, lambda i: (0, 0)),
                  pl.BlockSpec(v_bf.shape, lambda i: (0, 0)),
                  pl.BlockSpec(bias_tiles.shape, lambda i: (0, 0, 0, 0))],
        out_specs=pl.BlockSpec((DSA_QB, wq), lambda i: (i, 0)),
        out_shape=jax.ShapeDtypeStruct((m, wq), BF16),
        scratch_shapes=[pltpu.VMEM((m // LANES, DSA_QB, LANES), F32),
                        pltpu.VMEM((IDX_HEADS, DSA_QB, IDX_DIM), BF16),
                        pltpu.VMEM((B_KV_HEADS, B_GROUP * DSA_QB, B_HEAD_DIM), BF16),
                        pltpu.VMEM((IDX_HEADS, DSA_QB, LANES), F32),
                        pltpu.VMEM((B_KV_HEADS, B_GROUP, DSA_QB, 1), F32),
                        pltpu.VMEM((B_KV_HEADS, B_GROUP, DSA_QB, 1), F32),
                        pltpu.VMEM((B_KV_HEADS, B_GROUP, DSA_QB, B_HEAD_DIM), F32)],
        compiler_params=_cparams("arbitrary"),
        name="dsa_prompt",
    )(z, z, z, ztail, kidx_bf, k_bf, v_bf, bias_tiles)


def _page_copies(table_ref, b, n_pages, src_hbm, dst, sem, rows_per_page=PAGE_SIZE):
    def copy(p):
        return pltpu.make_async_copy(src_hbm.at[table_ref[b, p]],
                                     dst.at[pl.ds(p * rows_per_page, rows_per_page)], sem)
    return copy


def _dsa_scores_kernel(pt_ref, iq_ref, iw_ref, iknew_ref, kidx_hbm, o_ref, buf, sem, *, n_pages):
    b = pl.program_id(0)
    nb = pl.num_programs(0)
    past = n_pages * PAGE_SIZE

    def page_copy(bb, slot, p):
        return pltpu.make_async_copy(kidx_hbm.at[pt_ref[bb, p]],
                                     buf.at[slot, :, pl.ds(pl.multiple_of(p * PAGE_SIZE, PAGE_SIZE), PAGE_SIZE)],
                                     sem.at[slot])

    def start(bb, slot):
        lax.fori_loop(0, n_pages, lambda p, c: (page_copy(bb, slot, p).start(), c)[1], 0)

    def wait(bb, slot):
        lax.fori_loop(0, n_pages, lambda p, c: (page_copy(bb, slot, p).wait(), c)[1], 0)

    slot = b % 2

    @pl.when(b == 0)
    def _():
        start(0, 0)

    @pl.when(b + 1 < nb)
    def _():
        start(b + 1, 1 - slot)

    wait(b, slot)

    r16 = lambda a: a.astype(BF16).astype(F32)
    qi = iq_ref[0].astype(BF16)
    wcol = r16(iw_ref[0]) * (IDX_DIM ** -0.5 * IDX_HEADS ** -0.5)
    s = r16(jnp.maximum(jnp.dot(qi, buf[slot].astype(BF16), preferred_element_type=F32), 0.0))
    o_ref[0, :, 0:past] = jnp.sum(s * wcol, axis=0, keepdims=True)
    sn = r16(jnp.maximum(jnp.dot(qi, iknew_ref[0].astype(BF16), preferred_element_type=F32), 0.0))
    sn = jnp.sum(sn * wcol, axis=0, keepdims=True)
    lane = lax.broadcasted_iota(jnp.int32, (1, LANES), 1)
    o_ref[0, :, past:past + LANES] = jnp.where(lane == 0, sn, -jnp.inf)


def dsa_scores(page_table, iq, iw, iknew_pad, cache_kidx):
    b, n_pages = page_table.shape
    past = n_pages * PAGE_SIZE
    kern = functools.partial(_dsa_scores_kernel, n_pages=n_pages)
    gs = pltpu.PrefetchScalarGridSpec(
        num_scalar_prefetch=1,
        grid=(b,),
        in_specs=[pl.BlockSpec((1, IDX_HEADS, IDX_DIM), lambda i, pt: (i, 0, 0)),
                  pl.BlockSpec((1, IDX_HEADS, 1), lambda i, pt: (i, 0, 0)),
                  pl.BlockSpec((1, IDX_DIM, LANES), lambda i, pt: (i, 0, 0)),
                  pl.BlockSpec(memory_space=pl.ANY)],
        out_specs=pl.BlockSpec((1, 1, past + LANES), lambda i, pt: (i, 0, 0)),
        scratch_shapes=[pltpu.VMEM((2, IDX_DIM, past), F32), pltpu.SemaphoreType.DMA((2,))],
    )
    return pl.pallas_call(
        kern, grid_spec=gs,
        out_shape=jax.ShapeDtypeStruct((b, 1, past + LANES), F32),
        compiler_params=_cparams("arbitrary"),
        name="dsa_scores",
    )(page_table, iq, iw, iknew_pad, cache_kidx)


def _dsa_threshold_kernel(s_ref, thr_ref, *, topk):
    s = s_ref[...]
    nb = s.shape[0]
    finite = s > -jnp.inf
    lo0 = jnp.min(jnp.where(finite, s, jnp.inf), axis=1, keepdims=True)
    hi0 = jnp.max(s, axis=1, keepdims=True)
    cnt0 = jnp.sum(jnp.where(finite, 1.0, 0.0), axis=1, keepdims=True)

    def count_ge(thr):
        return jnp.sum(jnp.where(s_ref[...] >= thr, 1.0, 0.0), axis=1, keepdims=True)

    thr = _bisect_threshold(count_ge, lo0, hi0, cnt0, topk)
    thr_ref[...] = jnp.broadcast_to(thr, (nb, LANES))


def dsa_threshold(scores, topk):
    b, l = scores.shape
    return pl.pallas_call(
        functools.partial(_dsa_threshold_kernel, topk=topk),
        grid=(1,),
        in_specs=[pl.BlockSpec((b, l), lambda i: (0, 0))],
        out_specs=pl.BlockSpec((b, LANES), lambda i: (0, 0)),
        out_shape=jax.ShapeDtypeStruct((b, LANES), F32),
        compiler_params=_cparams("arbitrary"),
        name="dsa_threshold",
    )(scores)


def _dsa_decode_kernel(pt_ref, q_ref, s_ref, thr_ref, knew_ref, vnew_ref, bias_ref, k_hbm, v_hbm, o_ref,
                       kbuf, vbuf, sem, *, n_pages):
    b = pl.program_id(0)
    nb = pl.num_programs(0)
    past = n_pages * PAGE_SIZE

    rpp = PAGE_SIZE * B_KV_HEADS

    def copies(bb, slot):
        ck = _page_copies(pt_ref, bb, n_pages, k_hbm, kbuf.at[slot], sem.at[0, slot], rpp)
        cv = _page_copies(pt_ref, bb, n_pages, v_hbm, vbuf.at[slot], sem.at[1, slot], rpp)
        return ck, cv

    def start(bb, slot):
        ck, cv = copies(bb, slot)
        lax.fori_loop(0, n_pages, lambda p, c: (ck(p).start(), cv(p).start(), c)[2], 0)

    def wait(bb, slot):
        ck, cv = copies(bb, slot)
        lax.fori_loop(0, n_pages, lambda p, c: (ck(p).wait(), cv(p).wait(), c)[2], 0)

    slot = b % 2

    @pl.when(b == 0)
    def _():
        kbuf[:, n_pages * rpp:, :] = jnp.zeros((2, rpp, B_HEAD_DIM), F32)
        vbuf[:, n_pages * rpp:, :] = jnp.zeros((2, rpp, B_HEAD_DIM), F32)
        start(0, 0)

    @pl.when(b + 1 < nb)
    def _():
        start(b + 1, 1 - slot)

    kbuf[slot, n_pages * rpp:n_pages * rpp + 8, :] = knew_ref[0]
    vbuf[slot, n_pages * rpp:n_pages * rpp + 8, :] = vnew_ref[0]
    wait(b, slot)

    sel = s_ref[0] >= thr_ref[0][:, 0:1]
    n_keys = past + PAGE_SIZE
    outs = []
    for n in range(B_KV_HEADS):
        kn = kbuf[slot, pl.ds(n, n_keys, stride=B_KV_HEADS), :].astype(BF16)
        vn = vbuf[slot, pl.ds(n, n_keys, stride=B_KV_HEADS), :].astype(BF16)
        qn = q_ref[0, n].astype(BF16)
        lg = _nt(qn, kn) * B_HEAD_DIM ** -0.5 + bias_ref[n]
        m = jnp.max(jnp.where(sel, lg, NEG), axis=1, keepdims=True)
        p = jnp.where(sel, jnp.exp(lg - m), 0.0)
        p = p / jnp.sum(p, axis=1, keepdims=True)
        outs.append(jnp.dot(p.astype(BF16), vn, preferred_element_type=F32))
    o_ref[0] = jnp.concatenate(outs, axis=0).astype(o_ref.dtype)


def dsa_decode(page_table, q8, scores, thr, knew8, vnew8, bias_rows, cache_k2, cache_v2):
    b, n_pages = page_table.shape
    past = n_pages * PAGE_SIZE
    l = past + LANES
    wkv = B_KV_HEADS * B_HEAD_DIM
    kern = functools.partial(_dsa_decode_kernel, n_pages=n_pages)
    gs = pltpu.PrefetchScalarGridSpec(
        num_scalar_prefetch=1,
        grid=(b,),
        in_specs=[pl.BlockSpec((1, B_KV_HEADS, 8, B_HEAD_DIM), lambda i, pt: (i, 0, 0, 0)),
                  pl.BlockSpec((1, 1, l), lambda i, pt: (i, 0, 0)),
                  pl.BlockSpec((1, 1, LANES), lambda i, pt: (i, 0, 0)),
                  pl.BlockSpec((1, 8, B_HEAD_DIM), lambda i, pt: (i, 0, 0)),
                  pl.BlockSpec((1, 8, B_HEAD_DIM), lambda i, pt: (i, 0, 0)),
                  pl.BlockSpec((B_KV_HEADS, 8, l), lambda i, pt: (0, 0, 0)),
                  pl.BlockSpec(memory_space=pl.ANY),
                  pl.BlockSpec(memory_space=pl.ANY)],
        out_specs=pl.BlockSpec((1, 2 * 8, B_HEAD_DIM), lambda i, pt: (i, 0, 0)),
        scratch_shapes=[pltpu.VMEM((2, l * B_KV_HEADS, B_HEAD_DIM), F32),
                        pltpu.VMEM((2, l * B_KV_HEADS, B_HEAD_DIM), F32),
                        pltpu.SemaphoreType.DMA((2, 2))],
    )
    return pl.pallas_call(
        kern, grid_spec=gs,
        out_shape=jax.ShapeDtypeStruct((b, 2 * 8, B_HEAD_DIM), F32),
        compiler_params=_cparams("arbitrary"),
        name="dsa_decode",
    )(page_table, q8, scores, thr, knew8, vnew8, bias_rows, cache_k2, cache_v2)


def _merge_kernel(ya_ref, yb_ref, ga_ref, gb_ref, wa_ref, wb_ref, o_ref):
    a = _mm(ya_ref[...], wa_ref[...])
    bb = _mm(yb_ref[...], wb_ref[...])
    o_ref[...] = (jax.nn.sigmoid(ga_ref[...]) * a + jax.nn.sigmoid(gb_ref[...]) * bb).astype(o_ref.dtype)


def merge(ya, yb, z, ga_col, gb_col, wa, wb, *, tm, tn):
    m, k = ya.shape
    n = wa.shape[1]
    return pl.pallas_call(
        _merge_kernel,
        grid=(m // tm, n // tn),
        in_specs=[pl.BlockSpec((tm, k), lambda i, j: (i, 0)),
                  pl.BlockSpec((tm, k), lambda i, j: (i, 0)),
                  pl.BlockSpec((tm, tn), lambda i, j: (i, ga_col // tn + j)),
                  pl.BlockSpec((tm, tn), lambda i, j: (i, gb_col // tn + j)),
                  pl.BlockSpec((k, tn), lambda i, j: (0, j)),
                  pl.BlockSpec((k, tn), lambda i, j: (0, j))],
        out_specs=pl.BlockSpec((tm, tn), lambda i, j: (i, j)),
        out_shape=jax.ShapeDtypeStruct((m, n), wa.dtype),
        compiler_params=_cparams("parallel", "arbitrary"),
        name="merge",
    )(ya, yb, z, z, wa, wb)


def _cross_prompt_kernel(x_ref, g_ref, wq_ref, mk_ref, mv_ref, wo_ref, o_ref):
    x = x_ref[...]
    ms = jnp.mean(x * x, axis=-1, keepdims=True)
    h = (x * lax.rsqrt(ms + EPS) * g_ref[...]).astype(BF16)
    q = jnp.dot(h, wq_ref[...], preferred_element_type=F32)
    outs = []
    for hh in range(X_HEADS):
        cols = slice(hh * X_HEAD_DIM, (hh + 1) * X_HEAD_DIM)
        lg = _nt(q[:, cols].astype(BF16), mk_ref[:, cols]) * X_HEAD_DIM ** -0.5
        mx = jnp.max(lg, axis=1, keepdims=True)
        p = jnp.exp(lg - mx)
        p = p / jnp.sum(p, axis=1, keepdims=True)
        outs.append(jnp.dot(p.astype(BF16), mv_ref[:, cols], preferred_element_type=F32).astype(BF16))
    att = jnp.concatenate(outs, axis=1)
    o_ref[...] = x + jnp.dot(att, wo_ref[...], preferred_element_type=F32)


def cross_prompt(x, g, wq, mk, mv, wo, *, tm):
    m, d = x.shape
    full = lambda a: pl.BlockSpec(a.shape, lambda i: (0,) * a.ndim)
    g2 = g.reshape(1, d)
    return pl.pallas_call(
        _cross_prompt_kernel,
        grid=(m // tm,),
        in_specs=[pl.BlockSpec((tm, d), lambda i: (i, 0)), full(g2), full(wq), full(mk), full(mv), full(wo)],
        out_specs=pl.BlockSpec((tm, d), lambda i: (i, 0)),
        out_shape=jax.ShapeDtypeStruct((m, d), F32),
        compiler_params=_cparams("parallel"),
        name="cross_prompt",
    )(x, g2, wq, mk, mv, wo)


def _cross_step_kernel(q_ref, mk_ref, mv_ref, o_ref):
    r16 = lambda a: a.astype(BF16).astype(F32)
    q = r16(q_ref[0])
    outs = []
    for hh in range(X_HEADS):
        cols = slice(hh * X_HEAD_DIM, (hh + 1) * X_HEAD_DIM)
        kh = r16(mk_ref[0, :, cols])
        vh = r16(mv_ref[0, :, cols])
        lg = jnp.sum(kh * q[:, cols], axis=1, keepdims=True) * X_HEAD_DIM ** -0.5
        mx = jnp.max(lg, axis=0, keepdims=True)
        p = jnp.exp(lg - mx)
        p = r16(p / jnp.sum(p, axis=0, keepdims=True))
        outs.append(jnp.sum(p * vh, axis=0, keepdims=True))
    o_ref[0] = jnp.concatenate(outs, axis=1).astype(o_ref.dtype)


def cross_step(q, mk, mv):
    b, w = q.shape
    mem = mk.shape[1]
    return pl.pallas_call(
        _cross_step_kernel,
        grid=(b,),
        in_specs=[pl.BlockSpec((1, 1, w), lambda i: (i, 0, 0)),
                  pl.BlockSpec((1, mem, w), lambda i: (i, 0, 0)),
                  pl.BlockSpec((1, mem, w), lambda i: (i, 0, 0))],
        out_specs=pl.BlockSpec((1, 1, w), lambda i: (i, 0, 0)),
        out_shape=jax.ShapeDtypeStruct((b, 1, w), F32),
        compiler_params=_cparams("arbitrary"),
        name="cross_step",
    )(q.reshape(b, 1, w), mk, mv)


def _pack_bf16_pairs(x):
    c = x.shape[1] // 2
    u = pltpu.bitcast(x.astype(BF16).astype(F32), jnp.uint32)
    return lax.shift_right_logical(u[:, :c], jnp.uint32(16)) | u[:, c:]


def _unpack_bf16_pairs(u):
    lo = pltpu.bitcast(lax.shift_left(u, jnp.uint32(16)), F32)
    hi = pltpu.bitcast(u & jnp.uint32(0xFFFF0000), F32)
    return jnp.concatenate([lo, hi], axis=1)


def _router_kernel(x_ref, g_ref, w_ref, b_ref, hf_ref, route_ref):
    x = x_ref[...]
    ms = jnp.mean(x * x, axis=-1, keepdims=True)
    hf = x * lax.rsqrt(ms + EPS) * g_ref[...]
    hf_ref[...] = _pack_bf16_pairs(hf)
    lg = _mm(hf, w_ref[...]) + b_ref[...]
    tm = lg.shape[0]
    lane = lax.broadcasted_iota(jnp.int32, (tm, LANES), 1)
    big = jnp.int32(LANES)
    is_g = lane < N_GROUPS
    gmax = jnp.max(jnp.where(is_g, lg, -jnp.inf), axis=1, keepdims=True)
    grp = jnp.min(jnp.where(is_g & (lg == gmax), lane, big), axis=1, keepdims=True)
    p_grp = 1.0 / jnp.sum(jnp.where(is_g, jnp.exp(lg - gmax), 0.0), axis=1, keepdims=True)
    e_lo = N_GROUPS + grp * EXP_PER_GROUP
    in_g = (lane >= e_lo) & (lane < e_lo + EXP_PER_GROUP)
    v1 = jnp.max(jnp.where(in_g, lg, -jnp.inf), axis=1, keepdims=True)
    i1 = jnp.min(jnp.where(in_g & (lg == v1), lane, big), axis=1, keepdims=True)
    rest = in_g & (lane != i1)
    v2 = jnp.max(jnp.where(rest, lg, -jnp.inf), axis=1, keepdims=True)
    i2 = jnp.min(jnp.where(rest & (lg == v2), lane, big), axis=1, keepdims=True)
    e2 = jnp.exp(v2 - v1)
    g1 = p_grp / (1.0 + e2)
    g2 = p_grp * e2 / (1.0 + e2)
    r = jnp.where(lane == 0, (i1 - N_GROUPS).astype(F32),
                  jnp.where(lane == 1, (i2 - N_GROUPS).astype(F32),
                            jnp.where(lane == 2, g1, jnp.where(lane == 3, g2, 0.0))))
    route_ref[...] = r


def router(x, g, w_pad, b_pad, *, tm):
    m, d = x.shape
    return pl.pallas_call(
        _router_kernel,
        grid=(pl.cdiv(m, tm),),
        in_specs=[pl.BlockSpec((tm, d), lambda i: (i, 0)),
                  pl.BlockSpec((1, d), lambda i: (0, 0)),
                  pl.BlockSpec((d, LANES), lambda i: (0, 0)),
                  pl.BlockSpec((1, LANES), lambda i: (0, 0))],
        out_specs=[pl.BlockSpec((tm, d // 2), lambda i: (i, 0)),
                   pl.BlockSpec((tm, LANES), lambda i: (i, 0))],
        out_shape=[jax.ShapeDtypeStruct((m, d // 2), jnp.uint32), jax.ShapeDtypeStruct((m, LANES), F32)],
        compiler_params=_cparams("parallel"),
        name="router",
    )(x, g.reshape(1, d), w_pad, b_pad)


def _moe_kernel(be_ref, nxt_ref, nused_ref, sbase_ref, nvalid_ref, padj_ref, order_ref, hf_hbm, wg_hbm, wu_hbm,
                wd_hbm, y_hbm, xbuf, obuf, wg_f, wu_f, wd_f, wg_s, wu_s, wd_s, sem_in, sem_out, sem_w,
                *, n_tokens):
    b = pl.program_id(0)
    nb = pl.num_programs(0)
    blk = xbuf.shape[1]
    slot = b % 2
    used = b < nused_ref[0]
    next_used = b + 1 < nused_ref[0]
    n_assign = order_ref.shape[0]

    def slot_info(bb):
        base, nv, pad0 = sbase_ref[bb], nvalid_ref[bb], padj_ref[bb]

        def info(r):
            valid = r < nv
            asg = order_ref[jnp.minimum(base + r, n_assign - 1)]
            tok = lax.shift_right_logical(asg, 1)
            return jnp.where(valid, tok, 0), jnp.where(valid, (asg & 1) * n_tokens + tok, pad0 + r)
        return info

    def weight_copies(e):
        return (pltpu.make_async_copy(wg_hbm.at[e], wg_f, sem_w.at[0]),
                pltpu.make_async_copy(wu_hbm.at[e], wu_f, sem_w.at[1]),
                pltpu.make_async_copy(wd_hbm.at[e], wd_f, sem_w.at[2]))

    def gather_start(bb, sl):
        info = slot_info(bb)
        for r in range(blk):
            pltpu.make_async_copy(hf_hbm.at[pl.ds(info(r)[0], 1)], xbuf.at[sl, pl.ds(r, 1)],
                                  sem_in.at[sl]).start()

    def scatter_start(bb, sl):
        info = slot_info(bb)
        for r in range(blk):
            pltpu.make_async_copy(obuf.at[sl, pl.ds(r, 1)], y_hbm.at[pl.ds(info(r)[1], 1)],
                                  sem_out.at[sl]).start(priority=r % 2)

    def gather_wait(sl):
        pltpu.make_async_copy(hf_hbm.at[pl.ds(0, blk)], xbuf.at[sl], sem_in.at[sl]).wait()

    def scatter_wait(sl):
        pltpu.make_async_copy(obuf.at[sl], y_hbm.at[pl.ds(0, blk)], sem_out.at[sl]).wait()

    @pl.when(b == 0)
    def _():
        for cp in weight_copies(be_ref[0]):
            cp.start(priority=1)
        gather_start(0, 0)
        obuf[...] = jnp.zeros(obuf.shape, obuf.dtype)

    changed = jnp.logical_and(used, jnp.logical_or(b == 0, be_ref[b] != be_ref[jnp.maximum(b - 1, 0)]))

    @pl.when(changed)
    def _():
        for cp in weight_copies(be_ref[b]):
            cp.wait()
        wg_s[...] = wg_f[...].astype(BF16)
        wu_s[...] = wu_f[...].astype(BF16)
        wd_s[...] = wd_f[...].astype(BF16)

    @pl.when(jnp.logical_and(changed, nxt_ref[b] >= 0))
    def _():
        for cp in weight_copies(nxt_ref[b]):
            cp.start(priority=1)

    @pl.when(used)
    def _():
        gather_wait(slot)

    @pl.when(b >= 2)
    def _():
        scatter_wait(slot)

    def step(prefetch, flush_prev, compute):
        if prefetch:
            gather_start(b + 1, 1 - slot)
        if flush_prev:
            scatter_start(b - 1, 1 - slot)
        if compute:
            x = _unpack_bf16_pairs(xbuf[slot]).astype(BF16)
            gg = jnp.dot(x, wg_s[...], preferred_element_type=F32)
            uu = jnp.dot(x, wu_s[...], preferred_element_type=F32)
            a = (_silu(gg) * uu).astype(BF16)
            obuf[slot] = _pack_bf16_pairs(jnp.dot(a, wd_s[...], preferred_element_type=F32))

    first, last = b == 0, b == nb - 1
    land, lnot = jnp.logical_and, jnp.logical_not
    pl.when(land(first, next_used))(lambda: step(True, False, True))
    pl.when(land(first, lnot(next_used)))(lambda: step(False, False, True))
    pl.when(land(lnot(first), land(used, next_used)))(lambda: step(True, True, True))
    pl.when(land(lnot(first), land(used, lnot(next_used))))(lambda: step(False, True, True))
    pl.when(land(lnot(first), lnot(used)))(lambda: step(False, True, False))

    @pl.when(last)
    def _():
        scatter_start(b, slot)
        scatter_wait(slot)

    @pl.when(jnp.logical_and(last, jnp.logical_not(first)))
    def _():
        scatter_wait(1 - slot)


def moe_experts(tables, hf, w_g, w_u, w_d, *, blk, out_rows):
    n_blocks = tables[0].shape[0]
    dp = hf.shape[1]
    d = 2 * dp
    ff = w_g.shape[2]
    gs = pltpu.PrefetchScalarGridSpec(
        num_scalar_prefetch=len(tables),
        grid=(n_blocks,),
        in_specs=[pl.BlockSpec(memory_space=pl.ANY)] * 4,
        out_specs=pl.BlockSpec(memory_space=pl.ANY),
        scratch_shapes=[pltpu.VMEM((2, blk, dp), jnp.uint32), pltpu.VMEM((2, blk, dp), jnp.uint32),
                        pltpu.VMEM((d, ff), F32), pltpu.VMEM((d, ff), F32), pltpu.VMEM((ff, d), F32),
                        pltpu.VMEM((d, ff), BF16), pltpu.VMEM((d, ff), BF16), pltpu.VMEM((ff, d), BF16),
                        pltpu.SemaphoreType.DMA((2,)), pltpu.SemaphoreType.DMA((2,)),
                        pltpu.SemaphoreType.DMA((3,))],
    )
    return pl.pallas_call(
        functools.partial(_moe_kernel, n_tokens=hf.shape[0]), grid_spec=gs,
        out_shape=jax.ShapeDtypeStruct((out_rows, dp), jnp.uint32),
        compiler_params=_cparams("arbitrary"),
        name="moe_experts",
    )(*tables, hf, w_g, w_u, w_d)


def _combine_kernel(x_ref, route_ref, gf_ref, y1_ref, y2_ref, op_ref, os_ref):
    i = pl.program_id(0)
    route = route_ref[...]
    x = (x_ref[...] + route[:, 2:3] * _unpack_bf16_pairs(y1_ref[...])
         + route[:, 3:4] * _unpack_bf16_pairs(y2_ref[...]))
    ms = jnp.mean(x * x, axis=-1, keepdims=True)
    out = x * lax.rsqrt(ms + EPS) * gf_ref[...]

    @pl.when(i < pl.num_programs(0) - 1)
    def _():
        op_ref[...] = out

    @pl.when(i == pl.num_programs(0) - 1)
    def _():
        os_ref[...] = out[:os_ref.shape[0]]


def combine(x, route, gf, y, plane, n_prompt, *, tm):
    m, d = x.shape
    n_tiles = n_prompt // tm
    assert n_prompt % tm == 0 and 0 < m - n_prompt <= tm
    return pl.pallas_call(
        _combine_kernel,
        grid=(n_tiles + 1,),
        in_specs=[pl.BlockSpec((tm, d), lambda i: (i, 0)),
                  pl.BlockSpec((tm, LANES), lambda i: (i, 0)),
                  pl.BlockSpec((1, d), lambda i: (0, 0)),
                  pl.BlockSpec((tm, d // 2), lambda i: (i, 0)),
                  pl.BlockSpec((tm, d // 2), lambda i: (plane // tm + i, 0))],
        out_specs=[pl.BlockSpec((tm, d), lambda i: (jnp.minimum(i, n_tiles - 1), 0)),
                   pl.BlockSpec((m - n_prompt, d), lambda i: (0, 0))],
        out_shape=[jax.ShapeDtypeStruct((n_prompt, d), F32), jax.ShapeDtypeStruct((m - n_prompt, d), F32)],
        compiler_params=_cparams("arbitrary"),
        name="combine",
    )(x, route, gf.reshape(1, d), y, y)


def _t5_bucket(dist):
    dist = jnp.asarray(dist, jnp.int32)
    max_exact = REL_BUCKETS // 2
    dist_f = jnp.maximum(dist, 1).astype(F32)
    large = max_exact + (jnp.log(dist_f / max_exact) / math.log(REL_MAX_DIST / max_exact)
                         * (REL_BUCKETS - max_exact)).astype(jnp.int32)
    large = jnp.minimum(large, REL_BUCKETS - 1)
    return jnp.where(dist < max_exact, dist, large)


def _bias_tables(rel_bias, past):
    r = np.arange(LANES)
    diff = r[:, None] - r[None, :]
    buckets = jnp.stack([_t5_bucket(np.maximum(diff, 0)),
                         _t5_bucket(np.maximum(diff + LANES, 0)),
                         _t5_bucket(np.full((LANES, LANES), 2 * LANES))])
    def lookup(bkt):
        oh = (bkt.reshape(-1, 1) == jnp.arange(REL_BUCKETS)[None, :]).astype(F32)
        out = jnp.dot(oh, rel_bias.astype(F32), precision=lax.Precision.HIGHEST)
        return out.T.reshape((rel_bias.shape[1],) + bkt.shape)

    tiles = lookup(buckets)
    tiles = tiles - tiles[:, 2:3]
    dist = np.maximum(past - np.arange(past + LANES), 0)
    rows = lookup(_t5_bucket(dist))
    rows = rows.reshape(B_KV_HEADS, B_GROUP, past + LANES)
    rows = jnp.concatenate([rows, jnp.zeros_like(rows)], axis=1)
    return tiles, rows


def _dispatch(eid, n_tokens, blk):
    a = eid.shape[0]
    assert EXPERT_TOPK == 2 and a == EXPERT_TOPK * n_tokens
    n_blocks = -(-(a + N_EXPERTS * (blk - 1)) // blk)
    rows = n_blocks * blk
    order = jnp.argsort(eid).astype(jnp.int32)
    counts = jnp.sum(eid[:, None] == jnp.arange(N_EXPERTS)[None, :], axis=0).astype(jnp.int32)
    cum = jnp.cumsum(counts)
    starts = cum - counts
    padded = (counts + blk - 1) // blk * blk
    pad_end = jnp.cumsum(padded)
    pad_start = pad_end - padded
    blocks = jnp.arange(n_blocks, dtype=jnp.int32)
    block_e = jnp.minimum(jnp.sum(pad_end[None, :] <= (blocks * blk)[:, None], axis=1), N_EXPERTS - 1)
    n_used = pad_end[-1] // blk
    off = blocks * blk - pad_start[block_e]
    sbase = starts[block_e] + off
    nvalid = jnp.where(blocks < n_used, jnp.clip(counts[block_e] - off, 0, blk), 0)
    padj = a + blocks * blk - cum[block_e]
    new_run = jnp.concatenate([jnp.array([True]), block_e[1:] != block_e[:-1]])
    run_start = jnp.where(new_run & (blocks < n_used), blocks, n_blocks)
    nxt = jnp.concatenate([jnp.flip(lax.cummin(jnp.flip(run_start)))[1:], jnp.array([n_blocks])])
    next_e = jnp.where(nxt < n_blocks, block_e[jnp.minimum(nxt, n_blocks - 1)], -1)
    i32 = lambda v: v.astype(jnp.int32)
    return (i32(block_e), i32(next_e), i32(n_used).reshape(1), i32(sbase), i32(nvalid), i32(padj), order), rows


def kernel(x_prompt, x_sample, mem_prompt, cache_k, cache_v, cache_kidx, page_table, state_hgrn, cache_mem_k,
           cache_mem_v, norm_mix, w_in, hgrn_lb_logits, hgrn_norm, w_branch_a, w_branch_b, w_out, norm_cross, w_xq,
           w_xk, w_xv, w_xo, norm_ffn, w_router_group, b_router_group, w_router_expert, b_router_expert, w_exp_gate,
           w_exp_up, w_exp_down, rel_bias, norm_final):
    assert w_in.shape[0] == 1, "single-layer step"
    l = 0
    drop0 = lambda a: a.reshape(a.shape[1:])
    bp, t, d = x_prompt.shape
    db = x_sample.shape[0]
    past = page_table.shape[1] * PAGE_SIZE
    xp = x_prompt.reshape(bp * t, d)
    xs = x_sample.reshape(db, d)

    wi_t = jnp.transpose(drop0(w_in)).astype(BF16)
    assert wi_t.shape[0] - NZ_MAIN + T_PAD == NZ_TAIL
    w_tail_t = jnp.pad(wi_t[NZ_MAIN:], ((T_PAD, 0), (0, 0)))
    wa, wb, wo = w_branch_a[l].astype(BF16), w_branch_b[l].astype(BF16), w_out[l].astype(BF16)
    wxq, wxk, wxv, wxo = (w_xq[l].astype(BF16), w_xk[l].astype(BF16), w_xv[l].astype(BF16), w_xo[l].astype(BF16))
    w_route = jnp.pad(jnp.concatenate([w_router_group[l], w_router_expert[l]], axis=1),
                      ((0, 0), (0, LANES - N_GROUPS - N_EXPERTS))).astype(BF16)
    b_route = jnp.pad(jnp.concatenate([b_router_group[l], b_router_expert[l]]),
                      (0, LANES - N_GROUPS - N_EXPERTS)).reshape(1, LANES)
    bias_tiles, bias_rows = _bias_tables(rel_bias, past)

    zp, zpt = in_proj(xp, norm_mix[l], wi_t, NZ_MAIN, w_tail_t, tm=1024, tn=512)
    kp = zp[:, C_BK:C_BK + 256]
    vp = zp[:, C_BV:C_BV + 256]
    ikp = zpt[:, T_SM + IK_LANE:T_SM + IK_LANE + IDX_DIM]
    ya_p, st_p = hgrn_prompt(zp, hgrn_lb_logits, hgrn_norm[l])
    yb_p = dsa_prompt(zp, zpt, ikp.astype(BF16), kp.astype(BF16), vp.astype(BF16), bias_tiles)
    mg_p = merge(ya_p, yb_p, zpt, T_GA, T_GB, wa, wb, tm=1024, tn=512)
    x1p = matmul(mg_p, wo, xp, tm=1024, tn=512)
    memp = mem_prompt.reshape(-1, d)
    mk = matmul(memp, wxk, tm=memp.shape[0], tn=512)
    mv = matmul(memp, wxv, tm=memp.shape[0], tn=512)
    x2p = cross_prompt(x1p, norm_cross[l], wxq, mk.astype(BF16), mv.astype(BF16), wxo, tm=512)

    zs, zst = in_proj(xs, norm_mix[l], wi_t, NZ_MAIN, w_tail_t, tm=db, tn=512)
    ks = zs[:, C_BK:C_BK + 256]
    vs = zs[:, C_BV:C_BV + 256]
    iks = zst[:, T_SM + IK_LANE:T_SM + IK_LANE + IDX_DIM]
    ya_s, st_s = hgrn_step(zs[:, :4 * A_HEADS * A_DK], hgrn_lb_logits, hgrn_norm[l], drop0(state_hgrn))
    iq_s = zs[:, C_IQ:C_IQ + IDX_HEADS * IDX_DIM].reshape(db, IDX_HEADS, IDX_DIM)
    iw_s = zst[:, T_SM + IW_LANE:T_SM + IW_LANE + IDX_HEADS].reshape(db, IDX_HEADS, 1)
    iknew_pad = jnp.pad(iks[:, :, None], ((0, 0), (0, 0), (0, LANES - 1)))
    scores = dsa_scores(page_table, iq_s, iw_s, iknew_pad,
                        jnp.swapaxes(drop0(cache_kidx), 1, 2)).reshape(db, past + LANES)
    topk_s = min(TOPK_MAX, (past + 1) // 4)
    thr = dsa_threshold(scores, topk_s)
    q8 = jnp.pad(zs[:, C_BQ:C_BQ + B_HEADS * B_HEAD_DIM].reshape(db, B_KV_HEADS, B_GROUP, B_HEAD_DIM),
                 ((0, 0), (0, 0), (0, 8 - B_GROUP), (0, 0)))
    knew8 = jnp.pad(ks.reshape(db, B_KV_HEADS, B_HEAD_DIM), ((0, 0), (0, 8 - B_KV_HEADS), (0, 0)))
    vnew8 = jnp.pad(vs.reshape(db, B_KV_HEADS, B_HEAD_DIM), ((0, 0), (0, 8 - B_KV_HEADS), (0, 0)))
    n_pool = cache_k.shape[1]
    ob = dsa_decode(page_table, q8, scores.reshape(db, 1, -1), thr.reshape(db, 1, LANES), knew8, vnew8, bias_rows,
                    cache_k.reshape(n_pool, PAGE_SIZE * B_KV_HEADS, B_HEAD_DIM),
                    cache_v.reshape(n_pool, PAGE_SIZE * B_KV_HEADS, B_HEAD_DIM))
    yb_s = ob.reshape(db, B_KV_HEADS, 8, B_HEAD_DIM)[:, :, :B_GROUP].reshape(db, B_HEADS * B_HEAD_DIM)
    mg_s = merge(ya_s.reshape(db, -1), yb_s, zst, T_GA, T_GB, wa, wb, tm=db, tn=512)
    x1s = matmul(mg_s, wo, xs, tm=db, tn=512)
    qx_s = norm_matmul(x1s, norm_cross[l], wxq, tm=db, tn=512)
    mem = cache_mem_k.shape[2]
    att_s = cross_step(qx_s, cache_mem_k.reshape(db, mem, -1), cache_mem_v.reshape(db, mem, -1))
    x2s = matmul(att_s.reshape(db, -1), wxo, x1s, tm=db, tn=512)

    tmc = 128
    n = -(-(bp * t + db) // tmc) * tmc
    x2 = jnp.concatenate([x2p, x2s, jnp.zeros((n - bp * t - db, d), F32)], axis=0)
    hf, route = router(x2, norm_ffn[l], w_route, b_route, tm=256)
    eid = route[:, :EXPERT_TOPK].astype(jnp.int32).reshape(-1)
    tables, rows = _dispatch(eid, n, MOE_ROWS)
    ye = moe_experts(tables, hf, drop0(w_exp_gate), drop0(w_exp_up), drop0(w_exp_down), blk=MOE_ROWS, out_rows=rows)
    y_p, y_s = combine(x2, route, norm_final, ye, n, bp * t, tm=tmc)

    y_prompt = y_p.reshape(bp, t, d)
    y_sample = y_s[:db].reshape(db, 1, d)
    return (y_prompt, y_sample,
            kp.reshape(1, bp, t, B_KV_HEADS, B_HEAD_DIM), vp.reshape(1, bp, t, B_KV_HEADS, B_HEAD_DIM),
            ikp.reshape(1, bp, t, IDX_DIM),
            jnp.swapaxes(st_p, 1, 2).reshape(1, bp, A_HEADS, A_DK, A_DV),
            mk.reshape(1, bp, -1, X_HEADS, X_HEAD_DIM), mv.reshape(1, bp, -1, X_HEADS, X_HEAD_DIM),
            ks.reshape(1, db, 1, B_KV_HEADS, B_HEAD_DIM), vs.reshape(1, db, 1, B_KV_HEADS, B_HEAD_DIM),
            iks.reshape(1, db, 1, IDX_DIM),
            st_s.reshape(1, db, A_HEADS, A_DK, A_DV))
```

```python
import functools
import math

import jax
import jax.numpy as jnp
import numpy as np
from jax import lax
from jax.experimental import pallas as pl
from jax.experimental.pallas import tpu as pltpu

F32 = jnp.float32
BF16 = jnp.bfloat16
EPS = 1e-6

D_MODEL = 2048
A_HEADS, A_DK, A_DV = 8, 128, 128
B_HEADS, B_KV_HEADS, B_HEAD_DIM = 8, 2, 128
B_GROUP = B_HEADS // B_KV_HEADS
IDX_HEADS, IDX_DIM = 16, 64
TOPK_MAX = 256
PAGE_SIZE = 128
REL_BUCKETS, REL_MAX_DIST = 32, 128
X_HEADS, X_HEAD_DIM = 4, 128
N_GROUPS, EXP_PER_GROUP = 4, 8
N_EXPERTS = N_GROUPS * EXP_PER_GROUP
EXPERT_TOPK = 2
EXPERT_FF = 512
MOE_ROWS = 256

LANES = 128
VMEM_LIMIT = 56 * 1024 * 1024

NEG = -1e30

C_AQ, C_AF, C_AI, C_AG, C_BQ, C_BK, C_BV, C_IQ = 0, 1024, 2048, 3072, 4096, 5120, 5376, 5632
NZ_MAIN = 6656
T_PAD = 432
T_SM, IW_LANE, IK_LANE = 384, 48, 64
T_GA, T_GB = 512, 2560
NZ_TAIL = 4608


def _cparams(*sem):
    return pltpu.CompilerParams(dimension_semantics=sem, vmem_limit_bytes=VMEM_LIMIT)


def _silu(x):
    return x * jax.nn.sigmoid(x)


def _nt(a, b):
    return lax.dot_general(a, b, (((1,), (1,)), ((), ())), preferred_element_type=F32)


def _nt_f32(a, b):
    return lax.dot_general(a, b, (((1,), (1,)), ((), ())), preferred_element_type=F32,
                           precision=lax.Precision.HIGHEST)


def _mm(a, w):
    return jnp.dot(a.astype(BF16), w.astype(BF16), preferred_element_type=F32)


def _norm_matmul_kernel(x_ref, g_ref, w_ref, o_ref, h_ref):
    @pl.when(pl.program_id(1) == 0)
    def _():
        x = x_ref[...]
        ms = jnp.mean(x * x, axis=-1, keepdims=True)
        h_ref[...] = (x * lax.rsqrt(ms + EPS) * g_ref[...]).astype(h_ref.dtype)

    o_ref[...] = _mm(h_ref[...], w_ref[...])


def norm_matmul(x, g, w, *, tm, tn):
    m, k = x.shape
    n = w.shape[1]
    assert n % tn == 0
    return pl.pallas_call(
        _norm_matmul_kernel,
        grid=(m // tm, n // tn),
        in_specs=[pl.BlockSpec((tm, k), lambda i, j: (i, 0)),
                  pl.BlockSpec((1, k), lambda i, j: (0, 0)),
                  pl.BlockSpec((k, tn), lambda i, j: (0, j))],
        out_specs=pl.BlockSpec((tm, tn), lambda i, j: (i, j)),
        out_shape=jax.ShapeDtypeStruct((m, n), F32),
        scratch_shapes=[pltpu.VMEM((tm, k), BF16)],
        compiler_params=_cparams("parallel", "arbitrary"),
        name="norm_matmul",
    )(x, g.reshape(1, k), w)


def _in_proj_kernel(x_ref, g_ref, wa_ref, wb_ref, oa_ref, ob_ref, h_ref, *, na):
    j = pl.program_id(1)

    @pl.when(j == 0)
    def _():
        x = x_ref[...]
        ms = jnp.mean(x * x, axis=-1, keepdims=True)
        h_ref[...] = (x * lax.rsqrt(ms + EPS) * g_ref[...]).astype(h_ref.dtype)

    @pl.when(j < na)
    def _():
        oa_ref[...] = _nt(h_ref[...], wa_ref[...])

    @pl.when(j >= na)
    def _():
        ob_ref[...] = _nt(h_ref[...], wb_ref[...])


def in_proj(x, g, wa_t, a_rows, wb_t, *, tm, tn):
    m, k = x.shape
    na, nb = a_rows // tn, wb_t.shape[0] // tn
    assert a_rows % tn == 0 and wb_t.shape[0] % tn == 0
    a_idx = lambda j: jnp.minimum(j, na - 1)
    b_idx = lambda j: jnp.maximum(j - na, 0)
    return pl.pallas_call(
        functools.partial(_in_proj_kernel, na=na),
        grid=(m // tm, na + nb),
        in_specs=[pl.BlockSpec((tm, k), lambda i, j: (i, 0)),
                  pl.BlockSpec((1, k), lambda i, j: (0, 0)),
                  pl.BlockSpec((tn, k), lambda i, j: (a_idx(j), 0)),
                  pl.BlockSpec((tn, k), lambda i, j: (b_idx(j), 0))],
        out_specs=[pl.BlockSpec((tm, tn), lambda i, j: (i, a_idx(j))),
                   pl.BlockSpec((tm, tn), lambda i, j: (i, b_idx(j)))],
        out_shape=[jax.ShapeDtypeStruct((m, na * tn), F32), jax.ShapeDtypeStruct((m, nb * tn), F32)],
        scratch_shapes=[pltpu.VMEM((tm, k), BF16)],
        compiler_params=_cparams("parallel", "arbitrary"),
        name="in_proj",
    )(x, g.reshape(1, k), wa_t, wb_t)


def _matmul_res_kernel(x_ref, w_ref, r_ref, o_ref):
    o_ref[...] = r_ref[...] + _mm(x_ref[...], w_ref[...])


def _matmul_kernel(x_ref, w_ref, o_ref):
    o_ref[...] = _mm(x_ref[...], w_ref[...])


def matmul(x, w, res=None, *, tm, tn):
    m, k = x.shape
    n = w.shape[1]
    in_specs = [pl.BlockSpec((tm, k), lambda i, j: (i, 0)),
                pl.BlockSpec((k, tn), lambda i, j: (0, j))]
    args = [x, w]
    kern = _matmul_kernel
    if res is not None:
        in_specs.append(pl.BlockSpec((tm, tn), lambda i, j: (i, j)))
        args.append(res)
        kern = _matmul_res_kernel
    return pl.pallas_call(
        kern,
        grid=(m // tm, n // tn),
        in_specs=in_specs,
        out_specs=pl.BlockSpec((tm, tn), lambda i, j: (i, j)),
        out_shape=jax.ShapeDtypeStruct((m, n), F32),
        compiler_params=_cparams("parallel", "arbitrary"),
        name="matmul",
    )(*args)


HG_TB = 128
HG_C = 16
HG_H = HG_C // 2


def _hgrn_prompt_kernel(aq_ref, af_ref, ai_ref, ag_ref, lbl_ref, ng_ref, ya_ref, st_out_ref,
                        st_ref, q_s, k_s, g_s, v_s):
    t = pl.program_id(0)

    @pl.when(t == 0)
    def _():
        st_ref[...] = jnp.zeros_like(st_ref)

    lbl = lbl_ref[...]
    mx = jnp.max(lbl, axis=0, keepdims=True)
    ex = jnp.exp(lbl - mx)
    lb = ex[0:1, :] / jnp.sum(ex, axis=0, keepdims=True)

    f = lb + (1.0 - lb) * jax.nn.sigmoid(af_ref[...])
    logf = jnp.log(f)
    row = lax.broadcasted_iota(jnp.int32, (HG_TB, HG_TB), 0)
    col = lax.broadcasted_iota(jnp.int32, (HG_TB, HG_TB), 1)
    tri = jnp.where((row // HG_C == col // HG_C) & (col <= row), 1.0, 0.0).astype(F32)
    g_s[...] = jnp.dot(tri, logf, preferred_element_type=F32, precision=lax.Precision.HIGHEST)
    q_s[...] = _silu(aq_ref[...])
    k_s[...] = 1.0 - f
    v_s[...] = ai_ref[...]

    sub = lax.broadcasted_iota(jnp.int32, (HG_C, A_DK), 0)
    sub8 = lax.broadcasted_iota(jnp.int32, (HG_H, A_DK), 0)
    ng = ng_ref[...]

    def chunk(c, carry):
        r0 = pl.multiple_of(c * HG_C, HG_C)
        rows = pl.ds(r0, HG_C)
        for h in range(A_HEADS):
            cols = slice(h * A_DK, (h + 1) * A_DK)
            g = g_s[rows, cols]
            qh = q_s[rows, cols]
            kh = k_s[rows, cols]
            vh = v_s[rows, cols]
            halves = []
            for hb in range(2):
                rs = slice(hb * HG_H, (hb + 1) * HG_H)
                gb, qb, kb, vb = g[rs], qh[rs], kh[rs], vh[rs]
                ob = jnp.zeros((HG_H, A_DV), F32)
                for tt in range(HG_H):
                    d = gb[tt:tt + 1, :] - gb
                    e = jnp.exp(jnp.where(sub8 <= tt, d, -jnp.inf))
                    p = e * (qb[tt:tt + 1, :] * kb)
                    a_col = jnp.sum(p, axis=1, keepdims=True)
                    o_row = jnp.sum(a_col * vb, axis=0, keepdims=True)
                    ob = jnp.where(sub8 == tt, o_row, ob)
                halves.append(ob)
            o = jnp.concatenate(halves, axis=0)
            low = sub < HG_H
            g_mid = g[HG_H - 1:HG_H, :]
            q_hi = jnp.where(low, 0.0, qh * jnp.exp(jnp.minimum(g - g_mid, 0.0)))
            k_lo = jnp.where(low, kh * jnp.exp(jnp.minimum(g_mid - g, 0.0)), 0.0)
            st = st_ref[h]
            g_last = g[HG_C - 1:HG_C, :]
            kt = kh * jnp.exp(g_last - g)
            upd = lax.dot_general(vh.astype(BF16), jnp.concatenate([kt, k_lo], axis=1).astype(BF16),
                                  (((0,), (0,)), ((), ())), preferred_element_type=F32)
            lhs = jnp.concatenate([qh * jnp.exp(g), q_hi], axis=1).astype(BF16)
            rhs = jnp.concatenate([st, upd[:, A_DK:]], axis=1).astype(BF16)
            o = o + _nt(lhs, rhs)
            st_ref[h] = st * jnp.exp(g_last) + upd[:, :A_DK]
            on = o * lax.rsqrt(jnp.mean(o * o, axis=-1, keepdims=True) + EPS) * ng
            ya_ref[rows, cols] = (on * _silu(ag_ref[rows, cols])).astype(ya_ref.dtype)
        return carry

    lax.fori_loop(0, HG_TB // HG_C, chunk, 0)

    @pl.when(t == pl.num_programs(0) - 1)
    def _():
        st_out_ref[...] = st_ref[...]


def hgrn_prompt(z, lb_logits, norm_g):
    m = z.shape[0]
    w = A_HEADS * A_DK

    def zspec(cb):
        return pl.BlockSpec((HG_TB, w), lambda t, cb=cb: (t, cb))

    return pl.pallas_call(
        _hgrn_prompt_kernel,
        grid=(m // HG_TB,),
        in_specs=[zspec(C_AQ // w), zspec(C_AF // w), zspec(C_AI // w), zspec(C_AG // w),
                  pl.BlockSpec(lb_logits.shape, lambda t: (0, 0)),
                  pl.BlockSpec((1, A_DV), lambda t: (0, 0))],
        out_specs=[pl.BlockSpec((HG_TB, w), lambda t: (t, 0)),
                   pl.BlockSpec((A_HEADS, A_DV, A_DK), lambda t: (0, 0, 0))],
        out_shape=[jax.ShapeDtypeStruct((m, w), BF16),
                   jax.ShapeDtypeStruct((A_HEADS, A_DV, A_DK), F32)],
        scratch_shapes=[pltpu.VMEM((A_HEADS, A_DV, A_DK), F32)] + [pltpu.VMEM((HG_TB, w), F32)] * 4,
        compiler_params=_cparams("arbitrary"),
        name="hgrn_prompt",
    )(z, z, z, z, lb_logits, norm_g.reshape(1, A_DV))


def _hgrn_step_kernel(z_ref, lbl_ref, ng_ref, s_ref, ya_ref, s_out_ref):
    lbl = lbl_ref[...]
    mx = jnp.max(lbl, axis=0, keepdims=True)
    ex = jnp.exp(lbl - mx)
    lb = ex[0:1, :] / jnp.sum(ex, axis=0, keepdims=True)
    z = z_ref[0]
    w = A_HEADS * A_DK
    q = _silu(z[:, 0:w])
    f = lb + (1.0 - lb) * jax.nn.sigmoid(z[:, w:2 * w])
    kk = 1.0 - f
    v = z[:, 2 * w:3 * w]
    ag = z[:, 3 * w:4 * w]
    rows = []
    for h in range(A_HEADS):
        cols = slice(h * A_DK, (h + 1) * A_DK)
        rows += [f[:, cols], kk[:, cols], q[:, cols]]
    rows.append(jnp.zeros((LANES - 3 * A_HEADS, A_DK), F32))
    xt = jnp.concatenate(rows, axis=0).T
    ng = ng_ref[...]
    r16 = lambda a: a.astype(BF16).astype(F32)
    outs = []
    for h in range(A_HEADS):
        cols = slice(h * A_DV, (h + 1) * A_DV)
        fcol = xt[:, 3 * h:3 * h + 1]
        kcol = xt[:, 3 * h + 1:3 * h + 2]
        qcol = xt[:, 3 * h + 2:3 * h + 3]
        s_old = s_ref[0, h]
        s_out_ref[0, h] = fcol * s_old + kcol * v[:, cols]
        o = (jnp.sum(r16(qcol * fcol) * r16(s_old), axis=0, keepdims=True)
             + jnp.sum(qcol * kcol, axis=0, keepdims=True) * v[:, cols])
        on = o * lax.rsqrt(jnp.mean(o * o, axis=-1, keepdims=True) + EPS) * ng
        outs.append(on * _silu(ag[:, cols]))
    ya_ref[0] = jnp.concatenate(outs, axis=1).astype(ya_ref.dtype)


def hgrn_step(z4, lb_logits, norm_g, state):
    b = z4.shape[0]
    w = A_HEADS * A_DK
    return pl.pallas_call(
        _hgrn_step_kernel,
        grid=(b,),
        in_specs=[pl.BlockSpec((1, 1, 4 * w), lambda i: (i, 0, 0)),
                  pl.BlockSpec(lb_logits.shape, lambda i: (0, 0)),
                  pl.BlockSpec((1, A_DV), lambda i: (0, 0)),
                  pl.BlockSpec((1, A_HEADS, A_DK, A_DV), lambda i: (i, 0, 0, 0))],
        out_specs=[pl.BlockSpec((1, 1, w), lambda i: (i, 0, 0)),
                   pl.BlockSpec((1, A_HEADS, A_DK, A_DV), lambda i: (i, 0, 0, 0))],
        out_shape=[jax.ShapeDtypeStruct((b, 1, w), F32),
                   jax.ShapeDtypeStruct(state.shape, F32)],
        compiler_params=_cparams("arbitrary"),
        name="hgrn_step",
    )(z4.reshape(b, 1, 4 * w), lb_logits, norm_g.reshape(1, A_DV), state)


BISECT_MAX_ITERS = 48


def _bisect_threshold(count_ge, lo, hi, cnt_lo, topk):
    kf = float(topk)

    def cond(c):
        return jnp.logical_and(c[0] < BISECT_MAX_ITERS, c[-1] > 0.0)

    def body(c):
        it, lo, hi, cl, _ = c
        mid = 0.5 * lo + 0.5 * hi
        cm = count_ge(mid)
        ge = cm >= kf
        lo = jnp.where(ge, mid, lo)
        cl = jnp.where(ge, cm, cl)
        hi = jnp.where(ge, hi, mid)
        busy = jnp.max(jnp.where(cl > kf, 1.0, 0.0))
        return it + 1, lo, hi, cl, busy

    busy0 = jnp.max(jnp.where(cnt_lo > kf, 1.0, 0.0))
    out = lax.while_loop(cond, body, (jnp.int32(0), lo, hi, cnt_lo, busy0))
    return out[1]


DSA_QB = 128
DSA_W = 512
DSA_W3 = 1024


def _dsa_prompt_kernel(iq0_ref, iq1_ref, bq_ref, iw_ref, kidx_ref, k_ref, v_ref, bias_ref, o_ref,
                       score_s, qih_s, qs_s, wb_s, m_s, l_s, acc_s, *, topk):
    i = pl.program_id(0)
    nsub = DSA_W // LANES
    nsub3 = DSA_W3 // LANES
    nch3 = (i * DSA_QB + DSA_QB + DSA_W3 - 1) // DSA_W3
    nch = nch3 * (DSA_W3 // DSA_W)
    qpos = i * DSA_QB + lax.broadcasted_iota(jnp.int32, (DSA_QB, 1), 0)

    iw = iw_ref[...]
    wscale = IDX_DIM ** -0.5 * IDX_HEADS ** -0.5
    for h in range(IDX_HEADS):
        iq_ref, hh = (iq0_ref, h) if h < IDX_HEADS // 2 else (iq1_ref, h - IDX_HEADS // 2)
        qih_s[h] = iq_ref[:, hh * IDX_DIM:(hh + 1) * IDX_DIM].astype(BF16)
        wb_s[h] = jnp.broadcast_to(iw[:, IW_LANE + h:IW_LANE + h + 1] * wscale, (DSA_QB, LANES))
    for h in range(B_HEADS):
        qs_s[h // B_GROUP, (h % B_GROUP) * DSA_QB:(h % B_GROUP + 1) * DSA_QB, :] = (
            bq_ref[:, h * B_HEAD_DIM:(h + 1) * B_HEAD_DIM] * B_HEAD_DIM ** -0.5).astype(BF16)

    def p1(c, carry):
        c0 = pl.multiple_of(c * DSA_W, DSA_W)
        kc = kidx_ref[pl.ds(c0, DSA_W), :]
        sc = [jnp.zeros((DSA_QB, LANES), F32) for _ in range(nsub)]
        for h in range(IDX_HEADS):
            s = jnp.maximum(_nt(qih_s[h], kc), 0.0)
            wb = wb_s[h]
            for j in range(nsub):
                sc[j] = sc[j] + s[:, j * LANES:(j + 1) * LANES] * wb
        for j in range(nsub):
            kpos = c0 + j * LANES + lax.broadcasted_iota(jnp.int32, (1, LANES), 1)
            score_s[c * nsub + j] = jnp.where(kpos <= qpos, sc[j], -jnp.inf)
        return carry

    lax.fori_loop(0, nch, p1, 0)

    def stats(c, carry):
        mn, mx = carry
        for j in range(nsub):
            s = score_s[c * nsub + j]
            mx = jnp.maximum(mx, s)
            mn = jnp.minimum(mn, jnp.where(s > -jnp.inf, s, jnp.inf))
        return mn, mx

    mn, mx = lax.fori_loop(0, nch, stats, (jnp.full((DSA_QB, LANES), jnp.inf, F32),
                                           jnp.full((DSA_QB, LANES), -jnp.inf, F32)))
    lo0 = jnp.min(mn, axis=1, keepdims=True)
    hi0 = jnp.max(mx, axis=1, keepdims=True)

    def count_ge(thr):
        thr_b = jnp.broadcast_to(thr, (DSA_QB, LANES))

        def body(c, acc):
            for j in range(nsub):
                s = score_s[c * nsub + j]
                acc = acc + jnp.where(s >= thr_b, 1.0, 0.0)
            return acc

        acc = lax.fori_loop(0, nch, body, jnp.zeros((DSA_QB, LANES), F32))
        return jnp.sum(acc, axis=1, keepdims=True)

    thr = _bisect_threshold(count_ge, lo0, hi0, (qpos + 1).astype(F32), topk)
    thr_b = jnp.broadcast_to(thr, (DSA_QB, LANES))

    m_s[...] = jnp.full(m_s.shape, NEG, F32)
    l_s[...] = jnp.zeros(l_s.shape, F32)
    acc_s[...] = jnp.zeros(acc_s.shape, F32)

    def p3(c, with_bias):
        c0 = pl.multiple_of(c * DSA_W3, DSA_W3)
        madd = jnp.concatenate([jnp.where(score_s[c * nsub3 + j] >= thr_b, 0.0, NEG) for j in range(nsub3)], axis=1)
        kc = k_ref[pl.ds(c0, DSA_W3), :]
        vc = v_ref[pl.ds(c0, DSA_W3), :]
        rel = [i - (c * nsub3 + j) for j in range(nsub3)]

        def scores(n):
            return _nt(qs_s[n], kc[:, n * B_HEAD_DIM:(n + 1) * B_HEAD_DIM])

        def softmax(n, lg):
            lg = lg.reshape(B_GROUP, DSA_QB, DSA_W3) + madd[None]
            if with_bias:
                lg = lg + jnp.stack([jnp.concatenate(
                    [jnp.where(rel[j] == 0, bias_ref[n * B_GROUP + gq, 0],
                               jnp.where(rel[j] == 1, bias_ref[n * B_GROUP + gq, 1], 0.0)) for j in range(nsub3)],
                    axis=1) for gq in range(B_GROUP)])
            m_old = m_s[n]
            m_new = jnp.maximum(m_old, jnp.max(lg, axis=-1, keepdims=True))
            p = jnp.exp(lg - m_new)
            alpha = jnp.exp(m_old - m_new)
            l_s[n] = alpha * l_s[n] + jnp.sum(p, axis=-1, keepdims=True)
            m_s[n] = m_new
            pv = jnp.dot(p.reshape(B_GROUP * DSA_QB, DSA_W3).astype(BF16), vc[:, n * B_HEAD_DIM:(n + 1) * B_HEAD_DIM],
                         preferred_element_type=F32)
            return alpha, pv.reshape(B_GROUP, DSA_QB, B_HEAD_DIM)

        lgs = [scores(n) for n in range(B_KV_HEADS)]
        outs = [softmax(n, lgs[n]) for n in range(B_KV_HEADS)]
        for n in range(B_KV_HEADS):
            acc_s[n] = outs[n][0] * acc_s[n] + outs[n][1]

    n_far = jnp.maximum(i - 1, 0) // nsub3
    lax.fori_loop(0, n_far, lambda c, carry: (p3(c, False), carry)[1], 0)
    lax.fori_loop(n_far, nch3, lambda c, carry: (p3(c, True), carry)[1], 0)

    for h in range(B_HEADS):
        n, gq = h // B_GROUP, h % B_GROUP
        o_ref[:, h * B_HEAD_DIM:(h + 1) * B_HEAD_DIM] = (acc_s[n, gq] / l_s[n, gq]).astype(o_ref.dtype)


def dsa_prompt(z, ztail, kidx_bf, k_bf, v_bf, bias_tiles):
    m = z.shape[0]
    topk = min(TOPK_MAX, m // 4)
    wq = B_HEADS * B_HEAD_DIM
    wi2 = IDX_HEADS * IDX_DIM // 2
    kern = functools.partial(_dsa_prompt_kernel, topk=topk)
    return pl.pallas_call(
        kern,
        grid=(m // DSA_QB,),
        in_specs=[pl.BlockSpec((DSA_QB, wi2), lambda i: (i, C_IQ // wi2)),
                  pl.BlockSpec((DSA_QB, wi2), lambda i: (i, C_IQ // wi2 + 1)),
                  pl.BlockSpec((DSA_QB, wq), lambda i: (i, C_BQ // wq)),
                  pl.BlockSpec((DSA_QB, LANES), lambda i: (i, T_SM // LANES)),
                  pl.BlockSpec(kidx_bf.shape, lambda i: (0, 0)),
                  pl.BlockSpec(k_bf.shape, lambda i: (0, 0)),
                  pl.BlockSpec(v_bf.shape, lambda i: (0, 0)),
                  pl.BlockSpec(bias_tiles.shape, lambda i: (0, 0, 0, 0))],
        out_specs=pl.BlockSpec((DSA_QB, wq), lambda i: (i, 0)),
        out_shape=jax.ShapeDtypeStruct((m, wq), BF16),
        scratch_shapes=[pltpu.VMEM((m // LANES, DSA_QB, LANES), F32),
                        pltpu.VMEM((IDX_HEADS, DSA_QB, IDX_DIM), BF16),
                        pltpu.VMEM((B_KV_HEADS, B_GROUP * DSA_QB, B_HEAD_DIM), BF16),
                        pltpu.VMEM((IDX_HEADS, DSA_QB, LANES), F32),
                        pltpu.VMEM((B_KV_HEADS, B_GROUP, DSA_QB, 1), F32),
                        pltpu.VMEM((B_KV_HEADS, B_GROUP, DSA_QB, 1), F32),
                        pltpu.VMEM((B_KV_HEADS, B_GROUP, DSA_QB, B_HEAD_DIM), F32)],
        compiler_params=_cparams("arbitrary"),
        name="dsa_prompt",
    )(z, z, z, ztail, kidx_bf, k_bf, v_bf, bias_tiles)


def _page_copies(table_ref, b, n_pages, src_hbm, dst, sem, rows_per_page=PAGE_SIZE):
    def copy(p):
        return pltpu.make_async_copy(src_hbm.at[table_ref[b, p]],
                                     dst.at[pl.ds(p * rows_per_page, rows_per_page)], sem)
    return copy


def _dsa_scores_kernel(pt_ref, iq_ref, iw_ref, iknew_ref, kidx_hbm, o_ref, buf, sem, *, n_pages):
    b = pl.program_id(0)
    nb = pl.num_programs(0)
    past = n_pages * PAGE_SIZE

    def page_copy(bb, slot, p):
        return pltpu.make_async_copy(kidx_hbm.at[pt_ref[bb, p]],
                                     buf.at[slot, :, pl.ds(pl.multiple_of(p * PAGE_SIZE, PAGE_SIZE), PAGE_SIZE)],
                                     sem.at[slot])

    def start(bb, slot):
        lax.fori_loop(0, n_pages, lambda p, c: (page_copy(bb, slot, p).start(), c)[1], 0)

    def wait(bb, slot):
        lax.fori_loop(0, n_pages, lambda p, c: (page_copy(bb, slot, p).wait(), c)[1], 0)

    slot = b % 2

    @pl.when(b == 0)
    def _():
        start(0, 0)

    @pl.when(b + 1 < nb)
    def _():
        start(b + 1, 1 - slot)

    wait(b, slot)

    r16 = lambda a: a.astype(BF16).astype(F32)
    qi = iq_ref[0].astype(BF16)
    wcol = r16(iw_ref[0]) * (IDX_DIM ** -0.5 * IDX_HEADS ** -0.5)
    s = r16(jnp.maximum(jnp.dot(qi, buf[slot].astype(BF16), preferred_element_type=F32), 0.0))
    o_ref[0, :, 0:past] = jnp.sum(s * wcol, axis=0, keepdims=True)
    sn = r16(jnp.maximum(jnp.dot(qi, iknew_ref[0].astype(BF16), preferred_element_type=F32), 0.0))
    sn = jnp.sum(sn * wcol, axis=0, keepdims=True)
    lane = lax.broadcasted_iota(jnp.int32, (1, LANES), 1)
    o_ref[0, :, past:past + LANES] = jnp.where(lane == 0, sn, -jnp.inf)


def dsa_scores(page_table, iq, iw, iknew_pad, cache_kidx):
    b, n_pages = page_table.shape
    past = n_pages * PAGE_SIZE
    kern = functools.partial(_dsa_scores_kernel, n_pages=n_pages)
    gs = pltpu.PrefetchScalarGridSpec(
        num_scalar_prefetch=1,
        grid=(b,),
        in_specs=[pl.BlockSpec((1, IDX_HEADS, IDX_DIM), lambda i, pt: (i, 0, 0)),
                  pl.BlockSpec((1, IDX_HEADS, 1), lambda i, pt: (i, 0, 0)),
                  pl.BlockSpec((1, IDX_DIM, LANES), lambda i, pt: (i, 0, 0)),
                  pl.BlockSpec(memory_space=pl.ANY)],
        out_specs=pl.BlockSpec((1, 1, past + LANES), lambda i, pt: (i, 0, 0)),
        scratch_shapes=[pltpu.VMEM((2, IDX_DIM, past), F32), pltpu.SemaphoreType.DMA((2,))],
    )
    return pl.pallas_call(
        kern, grid_spec=gs,
        out_shape=jax.ShapeDtypeStruct((b, 1, past + LANES), F32),
        compiler_params=_cparams("arbitrary"),
        name="dsa_scores",
    )(page_table, iq, iw, iknew_pad, cache_kidx)


def _dsa_threshold_kernel(s_ref, thr_ref, *, topk):
    s = s_ref[...]
    nb = s.shape[0]
    finite = s > -jnp.inf
    lo0 = jnp.min(jnp.where(finite, s, jnp.inf), axis=1, keepdims=True)
    hi0 = jnp.max(s, axis=1, keepdims=True)
    cnt0 = jnp.sum(jnp.where(finite, 1.0, 0.0), axis=1, keepdims=True)

    def count_ge(thr):
        return jnp.sum(jnp.where(s_ref[...] >= thr, 1.0, 0.0), axis=1, keepdims=True)

    thr = _bisect_threshold(count_ge, lo0, hi0, cnt0, topk)
    thr_ref[...] = jnp.broadcast_to(thr, (nb, LANES))


def dsa_threshold(scores, topk):
    b, l = scores.shape
    return pl.pallas_call(
        functools.partial(_dsa_threshold_kernel, topk=topk),
        grid=(1,),
        in_specs=[pl.BlockSpec((b, l), lambda i: (0, 0))],
        out_specs=pl.BlockSpec((b, LANES), lambda i: (0, 0)),
        out_shape=jax.ShapeDtypeStruct((b, LANES), F32),
        compiler_params=_cparams("arbitrary"),
        name="dsa_threshold",
    )(scores)


def _dsa_decode_kernel(pt_ref, q_ref, s_ref, thr_ref, knew_ref, vnew_ref, bias_ref, k_hbm, v_hbm, o_ref,
                       kbuf, vbuf, sem, *, n_pages):
    b = pl.program_id(0)
    nb = pl.num_programs(0)
    past = n_pages * PAGE_SIZE

    rpp = PAGE_SIZE * B_KV_HEADS

    def copies(bb, slot):
        ck = _page_copies(pt_ref, bb, n_pages, k_hbm, kbuf.at[slot], sem.at[0, slot], rpp)
        cv = _page_copies(pt_ref, bb, n_pages, v_hbm, vbuf.at[slot], sem.at[1, slot], rpp)
        return ck, cv

    def start(bb, slot):
        ck, cv = copies(bb, slot)
        lax.fori_loop(0, n_pages, lambda p, c: (ck(p).start(), cv(p).start(), c)[2], 0)

    def wait(bb, slot):
        ck, cv = copies(bb, slot)
        lax.fori_loop(0, n_pages, lambda p, c: (ck(p).wait(), cv(p).wait(), c)[2], 0)

    slot = b % 2

    @pl.when(b == 0)
    def _():
        kbuf[:, n_pages * rpp:, :] = jnp.zeros((2, rpp, B_HEAD_DIM), F32)
        vbuf[:, n_pages * rpp:, :] = jnp.zeros((2, rpp, B_HEAD_DIM), F32)
        start(0, 0)

    @pl.when(b + 1 < nb)
    def _():
        start(b + 1, 1 - slot)

    kbuf[slot, n_pages * rpp:n_pages * rpp + 8, :] = knew_ref[0]
    vbuf[slot, n_pages * rpp:n_pages * rpp + 8, :] = vnew_ref[0]
    wait(b, slot)

    sel = s_ref[0] >= thr_ref[0][:, 0:1]
    n_keys = past + PAGE_SIZE
    outs = []
    for n in range(B_KV_HEADS):
        kn = kbuf[slot, pl.ds(n, n_keys, stride=B_KV_HEADS), :].astype(BF16)
        vn = vbuf[slot, pl.ds(n, n_keys, stride=B_KV_HEADS), :].astype(BF16)
        qn = q_ref[0, n].astype(BF16)
        lg = _nt(qn, kn) * B_HEAD_DIM ** -0.5 + bias_ref[n]
        m = jnp.max(jnp.where(sel, lg, NEG), axis=1, keepdims=True)
        p = jnp.where(sel, jnp.exp(lg - m), 0.0)
        p = p / jnp.sum(p, axis=1, keepdims=True)
        outs.append(jnp.dot(p.astype(BF16), vn, preferred_element_type=F32))
    o_ref[0] = jnp.concatenate(outs, axis=0).astype(o_ref.dtype)


def dsa_decode(page_table, q8, scores, thr, knew8, vnew8, bias_rows, cache_k2, cache_v2):
    b, n_pages = page_table.shape
    past = n_pages * PAGE_SIZE
    l = past + LANES
    wkv = B_KV_HEADS * B_HEAD_DIM
    kern = functools.partial(_dsa_decode_kernel, n_pages=n_pages)
    gs = pltpu.PrefetchScalarGridSpec(
        num_scalar_prefetch=1,
        grid=(b,),
        in_specs=[pl.BlockSpec((1, B_KV_HEADS, 8, B_HEAD_DIM), lambda i, pt: (i, 0, 0, 0)),
                  pl.BlockSpec((1, 1, l), lambda i, pt: (i, 0, 0)),
                  pl.BlockSpec((1, 1, LANES), lambda i, pt: (i, 0, 0)),
                  pl.BlockSpec((1, 8, B_HEAD_DIM), lambda i, pt: (i, 0, 0)),
                  pl.BlockSpec((1, 8, B_HEAD_DIM), lambda i, pt: (i, 0, 0)),
                  pl.BlockSpec((B_KV_HEADS, 8, l), lambda i, pt: (0, 0, 0)),
                  pl.BlockSpec(memory_space=pl.ANY),
                  pl.BlockSpec(memory_space=pl.ANY)],
        out_specs=pl.BlockSpec((1, 2 * 8, B_HEAD_DIM), lambda i, pt: (i, 0, 0)),
        scratch_shapes=[pltpu.VMEM((2, l * B_KV_HEADS, B_HEAD_DIM), F32),
                        pltpu.VMEM((2, l * B_KV_HEADS, B_HEAD_DIM), F32),
                        pltpu.SemaphoreType.DMA((2, 2))],
    )
    return pl.pallas_call(
        kern, grid_spec=gs,
        out_shape=jax.ShapeDtypeStruct((b, 2 * 8, B_HEAD_DIM), F32),
        compiler_params=_cparams("arbitrary"),
        name="dsa_decode",
    )(page_table, q8, scores, thr, knew8, vnew8, bias_rows, cache_k2, cache_v2)


def _merge_kernel(ya_ref, yb_ref, ga_ref, gb_ref, wa_ref, wb_ref, o_ref):
    a = _mm(ya_ref[...], wa_ref[...])
    bb = _mm(yb_ref[...], wb_ref[...])
    o_ref[...] = (jax.nn.sigmoid(ga_ref[...]) * a + jax.nn.sigmoid(gb_ref[...]) * bb).astype(o_ref.dtype)


def merge(ya, yb, z, ga_col, gb_col, wa, wb, *, tm, tn):
    m, k = ya.shape
    n = wa.shape[1]
    return pl.pallas_call(
        _merge_kernel,
        grid=(m // tm, n // tn),
        in_specs=[pl.BlockSpec((tm, k), lambda i, j: (i, 0)),
                  pl.BlockSpec((tm, k), lambda i, j: (i, 0)),
                  pl.BlockSpec((tm, tn), lambda i, j: (i, ga_col // tn + j)),
                  pl.BlockSpec((tm, tn), lambda i, j: (i, gb_col // tn + j)),
                  pl.BlockSpec((k, tn), lambda i, j: (0, j)),
                  pl.BlockSpec((k, tn), lambda i, j: (0, j))],
        out_specs=pl.BlockSpec((tm, tn), lambda i, j: (i, j)),
        out_shape=jax.ShapeDtypeStruct((m, n), wa.dtype),
        compiler_params=_cparams("parallel", "arbitrary"),
        name="merge",
    )(ya, yb, z, z, wa, wb)


def _cross_prompt_kernel(x_ref, g_ref, wq_ref, mk_ref, mv_ref, wo_ref, o_ref):
    x = x_ref[...]
    ms = jnp.mean(x * x, axis=-1, keepdims=True)
    h = (x * lax.rsqrt(ms + EPS) * g_ref[...]).astype(BF16)
    q = jnp.dot(h, wq_ref[...], preferred_element_type=F32)
    outs = []
    for hh in range(X_HEADS):
        cols = slice(hh * X_HEAD_DIM, (hh + 1) * X_HEAD_DIM)
        lg = _nt(q[:, cols].astype(BF16), mk_ref[:, cols]) * X_HEAD_DIM ** -0.5
        mx = jnp.max(lg, axis=1, keepdims=True)
        p = jnp.exp(lg - mx)
        p = p / jnp.sum(p, axis=1, keepdims=True)
        outs.append(jnp.dot(p.astype(BF16), mv_ref[:, cols], preferred_element_type=F32).astype(BF16))
    att = jnp.concatenate(outs, axis=1)
    o_ref[...] = x + jnp.dot(att, wo_ref[...], preferred_element_type=F32)


def cross_prompt(x, g, wq, mk, mv, wo, *, tm):
    m, d = x.shape
    full = lambda a: pl.BlockSpec(a.shape, lambda i: (0,) * a.ndim)
    g2 = g.reshape(1, d)
    return pl.pallas_call(
        _cross_prompt_kernel,
        grid=(m // tm,),
        in_specs=[pl.BlockSpec((tm, d), lambda i: (i, 0)), full(g2), full(wq), full(mk), full(mv), full(wo)],
        out_specs=pl.BlockSpec((tm, d), lambda i: (i, 0)),
        out_shape=jax.ShapeDtypeStruct((m, d), F32),
        compiler_params=_cparams("parallel"),
        name="cross_prompt",
    )(x, g2, wq, mk, mv, wo)


def _cross_step_kernel(q_ref, mk_ref, mv_ref, o_ref):
    r16 = lambda a: a.astype(BF16).astype(F32)
    q = r16(q_ref[0])
    outs = []
    for hh in range(X_HEADS):
        cols = slice(hh * X_HEAD_DIM, (hh + 1) * X_HEAD_DIM)
        kh = r16(mk_ref[0, :, cols])
        vh = r16(mv_ref[0, :, cols])
        lg = jnp.sum(kh * q[:, cols], axis=1, keepdims=True) * X_HEAD_DIM ** -0.5
        mx = jnp.max(lg, axis=0, keepdims=True)
        p = jnp.exp(lg - mx)
        p = r16(p / jnp.sum(p, axis=0, keepdims=True))
        outs.append(jnp.sum(p * vh, axis=0, keepdims=True))
    o_ref[0] = jnp.concatenate(outs, axis=1).astype(o_ref.dtype)


def cross_step(q, mk, mv):
    b, w = q.shape
    mem = mk.shape[1]
    return pl.pallas_call(
        _cross_step_kernel,
        grid=(b,),
        in_specs=[pl.BlockSpec((1, 1, w), lambda i: (i, 0, 0)),
                  pl.BlockSpec((1, mem, w), lambda i: (i, 0, 0)),
                  pl.BlockSpec((1, mem, w), lambda i: (i, 0, 0))],
        out_specs=pl.BlockSpec((1, 1, w), lambda i: (i, 0, 0)),
        out_shape=jax.ShapeDtypeStruct((b, 1, w), F32),
        compiler_params=_cparams("arbitrary"),
        name="cross_step",
    )(q.reshape(b, 1, w), mk, mv)


def _pack_bf16_pairs(x):
    c = x.shape[1] // 2
    u = pltpu.bitcast(x.astype(BF16).astype(F32), jnp.uint32)
    return lax.shift_right_logical(u[:, :c], jnp.uint32(16)) | u[:, c:]


def _unpack_bf16_pairs(u):
    lo = pltpu.bitcast(lax.shift_left(u, jnp.uint32(16)), F32)
    hi = pltpu.bitcast(u & jnp.uint32(0xFFFF0000), F32)
    return jnp.concatenate([lo, hi], axis=1)


def _router_kernel(x_ref, g_ref, w_ref, b_ref, hf_ref, route_ref):
    x = x_ref[...]
    ms = jnp.mean(x * x, axis=-1, keepdims=True)
    hf = x * lax.rsqrt(ms + EPS) * g_ref[...]
    hf_ref[...] = _pack_bf16_pairs(hf)
    lg = _mm(hf, w_ref[...]) + b_ref[...]
    tm = lg.shape[0]
    lane = lax.broadcasted_iota(jnp.int32, (tm, LANES), 1)
    big = jnp.int32(LANES)
    is_g = lane < N_GROUPS
    gmax = jnp.max(jnp.where(is_g, lg, -jnp.inf), axis=1, keepdims=True)
    grp = jnp.min(jnp.where(is_g & (lg == gmax), lane, big), axis=1, keepdims=True)
    p_grp = 1.0 / jnp.sum(jnp.where(is_g, jnp.exp(lg - gmax), 0.0), axis=1, keepdims=True)
    e_lo = N_GROUPS + grp * EXP_PER_GROUP
    in_g = (lane >= e_lo) & (lane < e_lo + EXP_PER_GROUP)
    v1 = jnp.max(jnp.where(in_g, lg, -jnp.inf), axis=1, keepdims=True)
    i1 = jnp.min(jnp.where(in_g & (lg == v1), lane, big), axis=1, keepdims=True)
    rest = in_g & (lane != i1)
    v2 = jnp.max(jnp.where(rest, lg, -jnp.inf), axis=1, keepdims=True)
    i2 = jnp.min(jnp.where(rest & (lg == v2), lane, big), axis=1, keepdims=True)
    e2 = jnp.exp(v2 - v1)
    g1 = p_grp / (1.0 + e2)
    g2 = p_grp * e2 / (1.0 + e2)
    r = jnp.where(lane == 0, (i1 - N_GROUPS).astype(F32),
                  jnp.where(lane == 1, (i2 - N_GROUPS).astype(F32),
                            jnp.where(lane == 2, g1, jnp.where(lane == 3, g2, 0.0))))
    route_ref[...] = r


def router(x, g, w_pad, b_pad, *, tm):
    m, d = x.shape
    return pl.pallas_call(
        _router_kernel,
        grid=(pl.cdiv(m, tm),),
        in_specs=[pl.BlockSpec((tm, d), lambda i: (i, 0)),
                  pl.BlockSpec((1, d), lambda i: (0, 0)),
                  pl.BlockSpec((d, LANES), lambda i: (0, 0)),
                  pl.BlockSpec((1, LANES), lambda i: (0, 0))],
        out_specs=[pl.BlockSpec((tm, d // 2), lambda i: (i, 0)),
                   pl.BlockSpec((tm, LANES), lambda i: (i, 0))],
        out_shape=[jax.ShapeDtypeStruct((m, d // 2), jnp.uint32), jax.ShapeDtypeStruct((m, LANES), F32)],
        compiler_params=_cparams("parallel"),
        name="router",
    )(x, g.reshape(1, d), w_pad, b_pad)


def _moe_kernel(be_ref, nxt_ref, nused_ref, sbase_ref, nvalid_ref, padj_ref, order_ref, hf_hbm, wg_hbm, wu_hbm,
                wd_hbm, y_hbm, xbuf, obuf, wg_f, wu_f, wd_f, wg_s, wu_s, wd_s, sem_in, sem_out, sem_w,
                *, n_tokens):
    b = pl.program_id(0)
    nb = pl.num_programs(0)
    blk = xbuf.shape[1]
    xslot = b % 3
    used = b < nused_ref[0]
    next_used = b + 2 < nused_ref[0]
    n_assign = order_ref.shape[0]

    def slot_info(bb):
        base, nv, pad0 = sbase_ref[bb], nvalid_ref[bb], padj_ref[bb]

        def info(r):
            valid = r < nv
            asg = order_ref[jnp.minimum(base + r, n_assign - 1)]
            tok = lax.shift_right_logical(asg, 1)
            return jnp.where(valid, tok, 0), jnp.where(valid, (asg & 1) * n_tokens + tok, pad0 + r)
        return info

    def weight_copies(e):
        return (pltpu.make_async_copy(wg_hbm.at[e], wg_f, sem_w.at[0]),
                pltpu.make_async_copy(wu_hbm.at[e], wu_f, sem_w.at[1]),
                pltpu.make_async_copy(wd_hbm.at[e], wd_f, sem_w.at[2]))

    def gather_start(bb, sl):
        info = slot_info(bb)
        for r in range(blk):
            pltpu.make_async_copy(hf_hbm.at[pl.ds(info(r)[0], 1)], xbuf.at[sl, pl.ds(r, 1)],
                                  sem_in.at[sl]).start()

    def scatter_start(bb, sl):
        info = slot_info(bb)
        for r in range(blk):
            pltpu.make_async_copy(obuf.at[sl, pl.ds(r, 1)], y_hbm.at[pl.ds(info(r)[1], 1)],
                                  sem_out.at[sl]).start(priority=r % 2)

    def gather_wait(sl):
        pltpu.make_async_copy(hf_hbm.at[pl.ds(0, blk)], xbuf.at[sl], sem_in.at[sl]).wait()

    def scatter_wait(sl):
        pltpu.make_async_copy(obuf.at[sl], y_hbm.at[pl.ds(0, blk)], sem_out.at[sl]).wait()

    @pl.when(b == 0)
    def _():
        for cp in weight_copies(be_ref[0]):
            cp.start(priority=1)
        gather_start(0, 0)
        obuf[...] = jnp.zeros(obuf.shape, obuf.dtype)

    @pl.when(jnp.logical_and(b == 0, 1 < nused_ref[0]))
    def _():
        gather_start(1, 1)

    changed = jnp.logical_and(used, jnp.logical_or(b == 0, be_ref[b] != be_ref[jnp.maximum(b - 1, 0)]))

    @pl.when(changed)
    def _():
        for cp in weight_copies(be_ref[b]):
            cp.wait()
        wg_s[...] = wg_f[...].astype(BF16)
        wu_s[...] = wu_f[...].astype(BF16)
        wd_s[...] = wd_f[...].astype(BF16)

    @pl.when(jnp.logical_and(changed, nxt_ref[b] >= 0))
    def _():
        for cp in weight_copies(nxt_ref[b]):
            cp.start(priority=1)

    @pl.when(used)
    def _():
        gather_wait(xslot)

    @pl.when(b >= 3)
    def _():
        scatter_wait(xslot)

    def step(prefetch, flush_prev, compute):
        if prefetch:
            gather_start(b + 2, (b + 2) % 3)
        if flush_prev:
            scatter_start(b - 1, (b + 2) % 3)
        if compute:
            x = _unpack_bf16_pairs(xbuf[xslot]).astype(BF16)
            gg = jnp.dot(x, wg_s[...], preferred_element_type=F32)
            uu = jnp.dot(x, wu_s[...], preferred_element_type=F32)
            a = (_silu(gg) * uu).astype(BF16)
            obuf[xslot] = _pack_bf16_pairs(jnp.dot(a, wd_s[...], preferred_element_type=F32))

    first, last = b == 0, b == nb - 1
    land, lnot = jnp.logical_and, jnp.logical_not
    pl.when(land(first, next_used))(lambda: step(True, False, True))
    pl.when(land(first, lnot(next_used)))(lambda: step(False, False, True))
    pl.when(land(lnot(first), land(used, next_used)))(lambda: step(True, True, True))
    pl.when(land(lnot(first), land(used, lnot(next_used))))(lambda: step(False, True, True))
    pl.when(land(lnot(first), lnot(used)))(lambda: step(False, True, False))

    @pl.when(last)
    def _():
        scatter_start(b, xslot)
        scatter_wait(xslot)

    @pl.when(jnp.logical_and(last, b >= 1))
    def _():
        scatter_wait((b + 2) % 3)

    @pl.when(jnp.logical_and(last, b >= 2))
    def _():
        scatter_wait((b + 1) % 3)


def moe_experts(tables, hf, w_g, w_u, w_d, *, blk, out_rows):
    n_blocks = tables[0].shape[0]
    dp = hf.shape[1]
    d = 2 * dp
    ff = w_g.shape[2]
    gs = pltpu.PrefetchScalarGridSpec(
        num_scalar_prefetch=len(tables),
        grid=(n_blocks,),
        in_specs=[pl.BlockSpec(memory_space=pl.ANY)] * 4,
        out_specs=pl.BlockSpec(memory_space=pl.ANY),
        scratch_shapes=[pltpu.VMEM((3, blk, dp), jnp.uint32), pltpu.VMEM((3, blk, dp), jnp.uint32),
                        pltpu.VMEM((d, ff), F32), pltpu.VMEM((d, ff), F32), pltpu.VMEM((ff, d), F32),
                        pltpu.VMEM((d, ff), BF16), pltpu.VMEM((d, ff), BF16), pltpu.VMEM((ff, d), BF16),
                        pltpu.SemaphoreType.DMA((3,)), pltpu.SemaphoreType.DMA((3,)),
                        pltpu.SemaphoreType.DMA((3,))],
    )
    return pl.pallas_call(
        functools.partial(_moe_kernel, n_tokens=hf.shape[0]), grid_spec=gs,
        out_shape=jax.ShapeDtypeStruct((out_rows, dp), jnp.uint32),
        compiler_params=_cparams("arbitrary"),
        name="moe_experts",
    )(*tables, hf, w_g, w_u, w_d)


def _combine_kernel(x_ref, route_ref, gf_ref, y1_ref, y2_ref, op_ref, os_ref):
    i = pl.program_id(0)
    route = route_ref[...]
    x = (x_ref[...] + route[:, 2:3] * _unpack_bf16_pairs(y1_ref[...])
         + route[:, 3:4] * _unpack_bf16_pairs(y2_ref[...]))
    ms = jnp.mean(x * x, axis=-1, keepdims=True)
    out = x * lax.rsqrt(ms + EPS) * gf_ref[...]

    @pl.when(i < pl.num_programs(0) - 1)
    def _():
        op_ref[...] = out

    @pl.when(i == pl.num_programs(0) - 1)
    def _():
        os_ref[...] = out[:os_ref.shape[0]]


def combine(x, route, gf, y, plane, n_prompt, *, tm):
    m, d = x.shape
    n_tiles = n_prompt // tm
    assert n_prompt % tm == 0 and 0 < m - n_prompt <= tm
    return pl.pallas_call(
        _combine_kernel,
        grid=(n_tiles + 1,),
        in_specs=[pl.BlockSpec((tm, d), lambda i: (i, 0)),
                  pl.BlockSpec((tm, LANES), lambda i: (i, 0)),
                  pl.BlockSpec((1, d), lambda i: (0, 0)),
                  pl.BlockSpec((tm, d // 2), lambda i: (i, 0)),
                  pl.BlockSpec((tm, d // 2), lambda i: (plane // tm + i, 0))],
        out_specs=[pl.BlockSpec((tm, d), lambda i: (jnp.minimum(i, n_tiles - 1), 0)),
                   pl.BlockSpec((m - n_prompt, d), lambda i: (0, 0))],
        out_shape=[jax.ShapeDtypeStruct((n_prompt, d), F32), jax.ShapeDtypeStruct((m - n_prompt, d), F32)],
        compiler_params=_cparams("arbitrary"),
        name="combine",
    )(x, route, gf.reshape(1, d), y, y)


def _t5_bucket(dist):
    dist = jnp.asarray(dist, jnp.int32)
    max_exact = REL_BUCKETS // 2
    dist_f = jnp.maximum(dist, 1).astype(F32)
    large = max_exact + (jnp.log(dist_f / max_exact) / math.log(REL_MAX_DIST / max_exact)
                         * (REL_BUCKETS - max_exact)).astype(jnp.int32)
    large = jnp.minimum(large, REL_BUCKETS - 1)
    return jnp.where(dist < max_exact, dist, large)


def _bias_tables(rel_bias, past):
    r = np.arange(LANES)
    diff = r[:, None] - r[None, :]
    buckets = jnp.stack([_t5_bucket(np.maximum(diff, 0)),
                         _t5_bucket(np.maximum(diff + LANES, 0)),
                         _t5_bucket(np.full((LANES, LANES), 2 * LANES))])
    def lookup(bkt):
        oh = (bkt.reshape(-1, 1) == jnp.arange(REL_BUCKETS)[None, :]).astype(F32)
        out = jnp.dot(oh, rel_bias.astype(F32), precision=lax.Precision.HIGHEST)
        return out.T.reshape((rel_bias.shape[1],) + bkt.shape)

    tiles = lookup(buckets)
    tiles = tiles - tiles[:, 2:3]
    dist = np.maximum(past - np.arange(past + LANES), 0)
    rows = lookup(_t5_bucket(dist))
    rows = rows.reshape(B_KV_HEADS, B_GROUP, past + LANES)
    rows = jnp.concatenate([rows, jnp.zeros_like(rows)], axis=1)
    return tiles, rows


def _dispatch(eid, n_tokens, blk):
    a = eid.shape[0]
    assert EXPERT_TOPK == 2 and a == EXPERT_TOPK * n_tokens
    n_blocks = -(-(a + N_EXPERTS * (blk - 1)) // blk)
    rows = n_blocks * blk
    order = jnp.argsort(eid).astype(jnp.int32)
    counts = jnp.sum(eid[:, None] == jnp.arange(N_EXPERTS)[None, :], axis=0).astype(jnp.int32)
    cum = jnp.cumsum(counts)
    starts = cum - counts
    padded = (counts + blk - 1) // blk * blk
    pad_end = jnp.cumsum(padded)
    pad_start = pad_end - padded
    blocks = jnp.arange(n_blocks, dtype=jnp.int32)
    block_e = jnp.minimum(jnp.sum(pad_end[None, :] <= (blocks * blk)[:, None], axis=1), N_EXPERTS - 1)
    n_used = pad_end[-1] // blk
    off = blocks * blk - pad_start[block_e]
    sbase = starts[block_e] + off
    nvalid = jnp.where(blocks < n_used, jnp.clip(counts[block_e] - off, 0, blk), 0)
    padj = a + blocks * blk - cum[block_e]
    new_run = jnp.concatenate([jnp.array([True]), block_e[1:] != block_e[:-1]])
    run_start = jnp.where(new_run & (blocks < n_used), blocks, n_blocks)
    nxt = jnp.concatenate([jnp.flip(lax.cummin(jnp.flip(run_start)))[1:], jnp.array([n_blocks])])
    next_e = jnp.where(nxt < n_blocks, block_e[jnp.minimum(nxt, n_blocks - 1)], -1)
    i32 = lambda v: v.astype(jnp.int32)
    return (i32(block_e), i32(next_e), i32(n_used).reshape(1), i32(sbase), i32(nvalid), i32(padj), order), rows


def kernel(x_prompt, x_sample, mem_prompt, cache_k, cache_v, cache_kidx, page_table, state_hgrn, cache_mem_k,
           cache_mem_v, norm_mix, w_in, hgrn_lb_logits, hgrn_norm, w_branch_a, w_branch_b, w_out, norm_cross, w_xq,
           w_xk, w_xv, w_xo, norm_ffn, w_router_group, b_router_group, w_router_expert, b_router_expert, w_exp_gate,
           w_exp_up, w_exp_down, rel_bias, norm_final):
    assert w_in.shape[0] == 1, "single-layer step"
    l = 0
    drop0 = lambda a: a.reshape(a.shape[1:])
    bp, t, d = x_prompt.shape
    db = x_sample.shape[0]
    past = page_table.shape[1] * PAGE_SIZE
    xp = x_prompt.reshape(bp * t, d)
    xs = x_sample.reshape(db, d)

    wi_t = jnp.transpose(drop0(w_in)).astype(BF16)
    assert wi_t.shape[0] - NZ_MAIN + T_PAD == NZ_TAIL
    w_tail_t = jnp.pad(wi_t[NZ_MAIN:], ((T_PAD, 0), (0, 0)))
    wa, wb, wo = w_branch_a[l].astype(BF16), w_branch_b[l].astype(BF16), w_out[l].astype(BF16)
    wxq, wxk, wxv, wxo = (w_xq[l].astype(BF16), w_xk[l].astype(BF16), w_xv[l].astype(BF16), w_xo[l].astype(BF16))
    w_route = jnp.pad(jnp.concatenate([w_router_group[l], w_router_expert[l]], axis=1),
                      ((0, 0), (0, LANES - N_GROUPS - N_EXPERTS))).astype(BF16)
    b_route = jnp.pad(jnp.concatenate([b_router_group[l], b_router_expert[l]]),
                      (0, LANES - N_GROUPS - N_EXPERTS)).reshape(1, LANES)
    bias_tiles, bias_rows = _bias_tables(rel_bias, past)

    zp, zpt = in_proj(xp, norm_mix[l], wi_t, NZ_MAIN, w_tail_t, tm=1024, tn=512)
    kp = zp[:, C_BK:C_BK + 256]
    vp = zp[:, C_BV:C_BV + 256]
    ikp = zpt[:, T_SM + IK_LANE:T_SM + IK_LANE + IDX_DIM]
    ya_p, st_p = hgrn_prompt(zp, hgrn_lb_logits, hgrn_norm[l])
    yb_p = dsa_prompt(zp, zpt, ikp.astype(BF16), kp.astype(BF16), vp.astype(BF16), bias_tiles)
    mg_p = merge(ya_p, yb_p, zpt, T_GA, T_GB, wa, wb, tm=1024, tn=512)
    x1p = matmul(mg_p, wo, xp, tm=1024, tn=512)
    memp = mem_prompt.reshape(-1, d)
    mk = matmul(memp, wxk, tm=memp.shape[0], tn=512)
    mv = matmul(memp, wxv, tm=memp.shape[0], tn=512)
    x2p = cross_prompt(x1p, norm_cross[l], wxq, mk.astype(BF16), mv.astype(BF16), wxo, tm=512)

    zs, zst = in_proj(xs, norm_mix[l], wi_t, NZ_MAIN, w_tail_t, tm=db, tn=512)
    ks = zs[:, C_BK:C_BK + 256]
    vs = zs[:, C_BV:C_BV + 256]
    iks = zst[:, T_SM + IK_LANE:T_SM + IK_LANE + IDX_DIM]
    ya_s, st_s = hgrn_step(zs[:, :4 * A_HEADS * A_DK], hgrn_lb_logits, hgrn_norm[l], drop0(state_hgrn))
    iq_s = zs[:, C_IQ:C_IQ + IDX_HEADS * IDX_DIM].reshape(db, IDX_HEADS, IDX_DIM)
    iw_s = zst[:, T_SM + IW_LANE:T_SM + IW_LANE + IDX_HEADS].reshape(db, IDX_HEADS, 1)
    iknew_pad = jnp.pad(iks[:, :, None], ((0, 0), (0, 0), (0, LANES - 1)))
    scores = dsa_scores(page_table, iq_s, iw_s, iknew_pad,
                        jnp.swapaxes(drop0(cache_kidx), 1, 2)).reshape(db, past + LANES)
    topk_s = min(TOPK_MAX, (past + 1) // 4)
    thr = dsa_threshold(scores, topk_s)
    q8 = jnp.pad(zs[:, C_BQ:C_BQ + B_HEADS * B_HEAD_DIM].reshape(db, B_KV_HEADS, B_GROUP, B_HEAD_DIM),
                 ((0, 0), (0, 0), (0, 8 - B_GROUP), (0, 0)))
    knew8 = jnp.pad(ks.reshape(db, B_KV_HEADS, B_HEAD_DIM), ((0, 0), (0, 8 - B_KV_HEADS), (0, 0)))
    vnew8 = jnp.pad(vs.reshape(db, B_KV_HEADS, B_HEAD_DIM), ((0, 0), (0, 8 - B_KV_HEADS), (0, 0)))
    n_pool = cache_k.shape[1]
    ob = dsa_decode(page_table, q8, scores.reshape(db, 1, -1), thr.reshape(db, 1, LANES), knew8, vnew8, bias_rows,
                    cache_k.reshape(n_pool, PAGE_SIZE * B_KV_HEADS, B_HEAD_DIM),
                    cache_v.reshape(n_pool, PAGE_SIZE * B_KV_HEADS, B_HEAD_DIM))
    yb_s = ob.reshape(db, B_KV_HEADS, 8, B_HEAD_DIM)[:, :, :B_GROUP].reshape(db, B_HEADS * B_HEAD_DIM)
    mg_s = merge(ya_s.reshape(db, -1), yb_s, zst, T_GA, T_GB, wa, wb, tm=db, tn=512)
    x1s = matmul(mg_s, wo, xs, tm=db, tn=512)
    qx_s = norm_matmul(x1s, norm_cross[l], wxq, tm=db, tn=512)
    mem = cache_mem_k.shape[2]
    att_s = cross_step(qx_s, cache_mem_k.reshape(db, mem, -1), cache_mem_v.reshape(db, mem, -1))
    x2s = matmul(att_s.reshape(db, -1), wxo, x1s, tm=db, tn=512)

    tmc = 128
    n = -(-(bp * t + db) // tmc) * tmc
    x2 = jnp.concatenate([x2p, x2s, jnp.zeros((n - bp * t - db, d), F32)], axis=0)
    hf, route = router(x2, norm_ffn[l], w_route, b_route, tm=256)
    eid = route[:, :EXPERT_TOPK].astype(jnp.int32).reshape(-1)
    tables, rows = _dispatch(eid, n, MOE_ROWS)
    ye = moe_experts(tables, hf, drop0(w_exp_gate), drop0(w_exp_up), drop0(w_exp_down), blk=MOE_ROWS, out_rows=rows)
    y_p, y_s = combine(x2, route, norm_final, ye, n, bp * t, tm=tmc)

    y_prompt = y_p.reshape(bp, t, d)
    y_sample = y_s[:db].reshape(db, 1, d)
    return (y_prompt, y_sample,
            kp.reshape(1, bp, t, B_KV_HEADS, B_HEAD_DIM), vp.reshape(1, bp, t, B_KV_HEADS, B_HEAD_DIM),
            ikp.reshape(1, bp, t, IDX_DIM),
            jnp.swapaxes(st_p, 1, 2).reshape(1, bp, A_HEADS, A_DK, A_DV),
            mk.reshape(1, bp, -1, X_HEADS, X_HEAD_DIM), mv.reshape(1, bp, -1, X_HEADS, X_HEAD_DIM),
            ks.reshape(1, db, 1, B_KV_HEADS, B_HEAD_DIM), vs.reshape(1, db, 1, B_KV_HEADS, B_HEAD_DIM),
            iks.reshape(1, db, 1, IDX_DIM),
            st_s.reshape(1, db, A_HEADS, A_DK, A_DV))
```

```python
import functools
import math

import jax
import jax.numpy as jnp
import numpy as np
from jax import lax
from jax.experimental import pallas as pl
from jax.experimental.pallas import tpu as pltpu

F32 = jnp.float32
BF16 = jnp.bfloat16
EPS = 1e-6

D_MODEL = 2048
A_HEADS, A_DK, A_DV = 8, 128, 128
B_HEADS, B_KV_HEADS, B_HEAD_DIM = 8, 2, 128
B_GROUP = B_HEADS // B_KV_HEADS
IDX_HEADS, IDX_DIM = 16, 64
TOPK_MAX = 256
PAGE_SIZE = 128
REL_BUCKETS, REL_MAX_DIST = 32, 128
X_HEADS, X_HEAD_DIM = 4, 128
N_GROUPS, EXP_PER_GROUP = 4, 8
N_EXPERTS = N_GROUPS * EXP_PER_GROUP
EXPERT_TOPK = 2
EXPERT_FF = 512
MOE_ROWS = 256

LANES = 128
VMEM_LIMIT = 56 * 1024 * 1024

NEG = -1e30

C_AQ, C_AF, C_AI, C_AG, C_BQ, C_BK, C_BV, C_IQ = 0, 1024, 2048, 3072, 4096, 5120, 5376, 5632
NZ_MAIN = 6656
T_PAD = 432
T_SM, IW_LANE, IK_LANE = 384, 48, 64
T_GA, T_GB = 512, 2560
NZ_TAIL = 4608


def _cparams(*sem):
    return pltpu.CompilerParams(dimension_semantics=sem, vmem_limit_bytes=VMEM_LIMIT)


def _silu(x):
    return x * jax.nn.sigmoid(x)


def _nt(a, b):
    return lax.dot_general(a, b, (((1,), (1,)), ((), ())), preferred_element_type=F32)


def _nt_f32(a, b):
    return lax.dot_general(a, b, (((1,), (1,)), ((), ())), preferred_element_type=F32,
                           precision=lax.Precision.HIGHEST)


def _mm(a, w):
    return jnp.dot(a.astype(BF16), w.astype(BF16), preferred_element_type=F32)


def _norm_matmul_kernel(x_ref, g_ref, w_ref, o_ref, h_ref):
    @pl.when(pl.program_id(1) == 0)
    def _():
        x = x_ref[...]
        ms = jnp.mean(x * x, axis=-1, keepdims=True)
        h_ref[...] = (x * lax.rsqrt(ms + EPS) * g_ref[...]).astype(h_ref.dtype)

    o_ref[...] = _mm(h_ref[...], w_ref[...])


def norm_matmul(x, g, w, *, tm, tn):
    m, k = x.shape
    n = w.shape[1]
    assert n % tn == 0
    return pl.pallas_call(
        _norm_matmul_kernel,
        grid=(m // tm, n // tn),
        in_specs=[pl.BlockSpec((tm, k), lambda i, j: (i, 0)),
                  pl.BlockSpec((1, k), lambda i, j: (0, 0)),
                  pl.BlockSpec((k, tn), lambda i, j: (0, j))],
        out_specs=pl.BlockSpec((tm, tn), lambda i, j: (i, j)),
        out_shape=jax.ShapeDtypeStruct((m, n), F32),
        scratch_shapes=[pltpu.VMEM((tm, k), BF16)],
        compiler_params=_cparams("parallel", "arbitrary"),
        name="norm_matmul",
    )(x, g.reshape(1, k), w)


def _in_proj_kernel(x_ref, g_ref, wa_ref, wb_ref, oa_ref, ob_ref, h_ref, *, na):
    j = pl.program_id(1)

    @pl.when(j == 0)
    def _():
        x = x_ref[...]
        ms = jnp.mean(x * x, axis=-1, keepdims=True)
        h_ref[...] = (x * lax.rsqrt(ms + EPS) * g_ref[...]).astype(h_ref.dtype)

    @pl.when(j < na)
    def _():
        oa_ref[...] = _nt(h_ref[...], wa_ref[...])

    @pl.when(j >= na)
    def _():
        ob_ref[...] = _nt(h_ref[...], wb_ref[...])


def in_proj(x, g, wa_t, a_rows, wb_t, *, tm, tn):
    m, k = x.shape
    na, nb = a_rows // tn, wb_t.shape[0] // tn
    assert a_rows % tn == 0 and wb_t.shape[0] % tn == 0
    a_idx = lambda j: jnp.minimum(j, na - 1)
    b_idx = lambda j: jnp.maximum(j - na, 0)
    return pl.pallas_call(
        functools.partial(_in_proj_kernel, na=na),
        grid=(m // tm, na + nb),
        in_specs=[pl.BlockSpec((tm, k), lambda i, j: (i, 0)),
                  pl.BlockSpec((1, k), lambda i, j: (0, 0)),
                  pl.BlockSpec((tn, k), lambda i, j: (a_idx(j), 0)),
                  pl.BlockSpec((tn, k), lambda i, j: (b_idx(j), 0))],
        out_specs=[pl.BlockSpec((tm, tn), lambda i, j: (i, a_idx(j))),
                   pl.BlockSpec((tm, tn), lambda i, j: (i, b_idx(j)))],
        out_shape=[jax.ShapeDtypeStruct((m, na * tn), F32), jax.ShapeDtypeStruct((m, nb * tn), F32)],
        scratch_shapes=[pltpu.VMEM((tm, k), BF16)],
        compiler_params=_cparams("parallel", "arbitrary"),
        name="in_proj",
    )(x, g.reshape(1, k), wa_t, wb_t)


def _matmul_res_kernel(x_ref, w_ref, r_ref, o_ref):
    o_ref[...] = r_ref[...] + _mm(x_ref[...], w_ref[...])


def _matmul_res_into_kernel(x_ref, w_ref, r_ref, buf_ref, o_ref):
    del buf_ref
    o_ref[...] = r_ref[...] + _mm(x_ref[...], w_ref[...])


def _matmul_kernel(x_ref, w_ref, o_ref):
    o_ref[...] = _mm(x_ref[...], w_ref[...])


def matmul(x, w, res=None, *, tm, tn, into=None):
    m, k = x.shape
    n = w.shape[1]
    in_specs = [pl.BlockSpec((tm, k), lambda i, j: (i, 0)),
                pl.BlockSpec((k, tn), lambda i, j: (0, j))]
    args = [x, w]
    kern = _matmul_kernel
    if res is not None:
        in_specs.append(pl.BlockSpec((tm, tn), lambda i, j: (i, j)))
        args.append(res)
        kern = _matmul_res_kernel
    out_shape, row_blk, aliases = jax.ShapeDtypeStruct((m, n), F32), 0, {}
    if into is not None:
        buf, row0 = into
        assert res is not None and row0 % tm == 0 and buf.shape[1] == n
        in_specs.append(pl.BlockSpec(memory_space=pl.ANY))
        args.append(buf)
        kern = _matmul_res_into_kernel
        out_shape, row_blk, aliases = jax.ShapeDtypeStruct(buf.shape, F32), row0 // tm, {len(args) - 1: 0}
    return pl.pallas_call(
        kern,
        grid=(m // tm, n // tn),
        in_specs=in_specs,
        out_specs=pl.BlockSpec((tm, tn), lambda i, j: (row_blk + i, j)),
        out_shape=out_shape,
        input_output_aliases=aliases,
        compiler_params=_cparams("parallel", "arbitrary"),
        name="matmul",
    )(*args)


HG_TB = 128
HG_C = 16
HG_H = HG_C // 2


def _hgrn_prompt_kernel(aq_ref, af_ref, ai_ref, ag_ref, lbl_ref, ng_ref, ya_ref, st_out_ref,
                        st_ref, q_s, k_s, g_s, v_s):
    t = pl.program_id(0)

    @pl.when(t == 0)
    def _():
        st_ref[...] = jnp.zeros_like(st_ref)

    lbl = lbl_ref[...]
    mx = jnp.max(lbl, axis=0, keepdims=True)
    ex = jnp.exp(lbl - mx)
    lb = ex[0:1, :] / jnp.sum(ex, axis=0, keepdims=True)

    f = lb + (1.0 - lb) * jax.nn.sigmoid(af_ref[...])
    logf = jnp.log(f)
    row = lax.broadcasted_iota(jnp.int32, (HG_TB, HG_TB), 0)
    col = lax.broadcasted_iota(jnp.int32, (HG_TB, HG_TB), 1)
    tri = jnp.where((row // HG_C == col // HG_C) & (col <= row), 1.0, 0.0).astype(BF16)
    g = jnp.zeros(logf.shape, F32)
    rem = logf
    for _ in range(3):
        part = rem.astype(BF16)
        g = g + jnp.dot(tri, part, preferred_element_type=F32)
        rem = rem - part.astype(F32)
    g_s[...] = g
    q_s[...] = _silu(aq_ref[...])
    k_s[...] = 1.0 - f
    v_s[...] = ai_ref[...]

    sub = lax.broadcasted_iota(jnp.int32, (HG_C, A_DK), 0)
    sub8 = lax.broadcasted_iota(jnp.int32, (HG_H, A_DK), 0)
    ng = ng_ref[...]

    def chunk(c, carry):
        r0 = pl.multiple_of(c * HG_C, HG_C)
        rows = pl.ds(r0, HG_C)
        for h in range(A_HEADS):
            cols = slice(h * A_DK, (h + 1) * A_DK)
            g = g_s[rows, cols]
            qh = q_s[rows, cols]
            kh = k_s[rows, cols]
            vh = v_s[rows, cols]
            halves = []
            for hb in range(2):
                rs = slice(hb * HG_H, (hb + 1) * HG_H)
                gb, qb, kb, vb = g[rs], qh[rs], kh[rs], vh[rs]
                ob = jnp.zeros((HG_H, A_DV), F32)
                for tt in range(HG_H):
                    d = gb[tt:tt + 1, :] - gb
                    e = jnp.exp(jnp.where(sub8 <= tt, d, -jnp.inf))
                    p = e * (qb[tt:tt + 1, :] * kb)
                    a_col = jnp.sum(p, axis=1, keepdims=True)
                    o_row = jnp.sum(a_col * vb, axis=0, keepdims=True)
                    ob = jnp.where(sub8 == tt, o_row, ob)
                halves.append(ob)
            o = jnp.concatenate(halves, axis=0)
            low = sub < HG_H
            g_mid = g[HG_H - 1:HG_H, :]
            q_hi = jnp.where(low, 0.0, qh * jnp.exp(jnp.minimum(g - g_mid, 0.0)))
            k_lo = jnp.where(low, kh * jnp.exp(jnp.minimum(g_mid - g, 0.0)), 0.0)
            st = st_ref[h]
            g_last = g[HG_C - 1:HG_C, :]
            kt = kh * jnp.exp(g_last - g)
            upd = lax.dot_general(vh.astype(BF16), jnp.concatenate([kt, k_lo], axis=1).astype(BF16),
                                  (((0,), (0,)), ((), ())), preferred_element_type=F32)
            lhs = jnp.concatenate([qh * jnp.exp(g), q_hi], axis=1).astype(BF16)
            rhs = jnp.concatenate([st, upd[:, A_DK:]], axis=1).astype(BF16)
            o = o + _nt(lhs, rhs)
            st_ref[h] = st * jnp.exp(g_last) + upd[:, :A_DK]
            on = o * lax.rsqrt(jnp.mean(o * o, axis=-1, keepdims=True) + EPS) * ng
            ya_ref[rows, cols] = (on * _silu(ag_ref[rows, cols])).astype(ya_ref.dtype)
        return carry

    lax.fori_loop(0, HG_TB // HG_C, chunk, 0)

    @pl.when(t == pl.num_programs(0) - 1)
    def _():
        st_out_ref[...] = st_ref[...]


def hgrn_prompt(z, lb_logits, norm_g):
    m = z.shape[0]
    w = A_HEADS * A_DK

    def zspec(cb):
        return pl.BlockSpec((HG_TB, w), lambda t, cb=cb: (t, cb))

    return pl.pallas_call(
        _hgrn_prompt_kernel,
        grid=(m // HG_TB,),
        in_specs=[zspec(C_AQ // w), zspec(C_AF // w), zspec(C_AI // w), zspec(C_AG // w),
                  pl.BlockSpec(lb_logits.shape, lambda t: (0, 0)),
                  pl.BlockSpec((1, A_DV), lambda t: (0, 0))],
        out_specs=[pl.BlockSpec((HG_TB, w), lambda t: (t, 0)),
                   pl.BlockSpec((A_HEADS, A_DV, A_DK), lambda t: (0, 0, 0))],
        out_shape=[jax.ShapeDtypeStruct((m, w), BF16),
                   jax.ShapeDtypeStruct((A_HEADS, A_DV, A_DK), F32)],
        scratch_shapes=[pltpu.VMEM((A_HEADS, A_DV, A_DK), F32)] + [pltpu.VMEM((HG_TB, w), F32)] * 4,
        compiler_params=_cparams("arbitrary"),
        name="hgrn_prompt",
    )(z, z, z, z, lb_logits, norm_g.reshape(1, A_DV))


def _hgrn_step_kernel(z_ref, lbl_ref, ng_ref, s_ref, ya_ref, s_out_ref):
    lbl = lbl_ref[...]
    mx = jnp.max(lbl, axis=0, keepdims=True)
    ex = jnp.exp(lbl - mx)
    lb = ex[0:1, :] / jnp.sum(ex, axis=0, keepdims=True)
    z = z_ref[0]
    w = A_HEADS * A_DK
    q = _silu(z[:, 0:w])
    f = lb + (1.0 - lb) * jax.nn.sigmoid(z[:, w:2 * w])
    kk = 1.0 - f
    v = z[:, 2 * w:3 * w]
    ag = z[:, 3 * w:4 * w]
    rows = []
    for h in range(A_HEADS):
        cols = slice(h * A_DK, (h + 1) * A_DK)
        rows += [f[:, cols], kk[:, cols], q[:, cols]]
    rows.append(jnp.zeros((LANES - 3 * A_HEADS, A_DK), F32))
    xt = jnp.concatenate(rows, axis=0).T
    ng = ng_ref[...]
    r16 = lambda a: a.astype(BF16).astype(F32)
    outs = []
    for h in range(A_HEADS):
        cols = slice(h * A_DV, (h + 1) * A_DV)
        fcol = xt[:, 3 * h:3 * h + 1]
        kcol = xt[:, 3 * h + 1:3 * h + 2]
        qcol = xt[:, 3 * h + 2:3 * h + 3]
        s_old = s_ref[0, h]
        s_out_ref[0, h] = fcol * s_old + kcol * v[:, cols]
        o = (jnp.sum(r16(qcol * fcol) * r16(s_old), axis=0, keepdims=True)
             + jnp.sum(qcol * kcol, axis=0, keepdims=True) * v[:, cols])
        on = o * lax.rsqrt(jnp.mean(o * o, axis=-1, keepdims=True) + EPS) * ng
        outs.append(on * _silu(ag[:, cols]))
    ya_ref[0] = jnp.concatenate(outs, axis=1).astype(ya_ref.dtype)


def hgrn_step(z4, lb_logits, norm_g, state):
    b = z4.shape[0]
    w = A_HEADS * A_DK
    return pl.pallas_call(
        _hgrn_step_kernel,
        grid=(b,),
        in_specs=[pl.BlockSpec((1, 1, 4 * w), lambda i: (i, 0, 0)),
                  pl.BlockSpec(lb_logits.shape, lambda i: (0, 0)),
                  pl.BlockSpec((1, A_DV), lambda i: (0, 0)),
                  pl.BlockSpec((1, A_HEADS, A_DK, A_DV), lambda i: (i, 0, 0, 0))],
        out_specs=[pl.BlockSpec((1, 1, w), lambda i: (i, 0, 0)),
                   pl.BlockSpec((1, A_HEADS, A_DK, A_DV), lambda i: (i, 0, 0, 0))],
        out_shape=[jax.ShapeDtypeStruct((b, 1, w), F32),
                   jax.ShapeDtypeStruct(state.shape, F32)],
        compiler_params=_cparams("arbitrary"),
        name="hgrn_step",
    )(z4.reshape(b, 1, 4 * w), lb_logits, norm_g.reshape(1, A_DV), state)


BISECT_MAX_ITERS = 48


def _bisect_threshold(count_ge, lo, hi, cnt_lo, topk):
    kf = float(topk)

    def cond(c):
        return jnp.logical_and(c[0] < BISECT_MAX_ITERS, c[-1] > 0.0)

    def body(c):
        it, lo, hi, cl, _ = c
        mid = 0.5 * lo + 0.5 * hi
        cm = count_ge(mid)
        ge = cm >= kf
        lo = jnp.where(ge, mid, lo)
        cl = jnp.where(ge, cm, cl)
        hi = jnp.where(ge, hi, mid)
        busy = jnp.max(jnp.where(cl > kf, 1.0, 0.0))
        return it + 1, lo, hi, cl, busy

    busy0 = jnp.max(jnp.where(cnt_lo > kf, 1.0, 0.0))
    out = lax.while_loop(cond, body, (jnp.int32(0), lo, hi, cnt_lo, busy0))
    return out[1]


DSA_QB = 128
DSA_W = 512
DSA_W3 = 1024


def _dsa_prompt_kernel(iq0_ref, iq1_ref, bq_ref, iw_ref, kidx_ref, k_ref, v_ref, bias_ref, o_ref,
                       score_s, qih_s, qs_s, wb_s, m_s, l_s, acc_s, *, topk):
    i = pl.program_id(0)
    nsub = DSA_W // LANES
    nsub3 = DSA_W3 // LANES
    qsub = DSA_QB // LANES
    nch3 = (i * DSA_QB + DSA_QB + DSA_W3 - 1) // DSA_W3
    nch = nch3 * (DSA_W3 // DSA_W)
    qpos = i * DSA_QB + lax.broadcasted_iota(jnp.int32, (DSA_QB, 1), 0)

    iw = iw_ref[...]
    wscale = IDX_DIM ** -0.5 * IDX_HEADS ** -0.5
    for h in range(IDX_HEADS):
        iq_ref, hh = (iq0_ref, h) if h < IDX_HEADS // 2 else (iq1_ref, h - IDX_HEADS // 2)
        qih_s[h] = iq_ref[:, hh * IDX_DIM:(hh + 1) * IDX_DIM].astype(BF16)
        wb_s[h] = jnp.broadcast_to(iw[:, IW_LANE + h:IW_LANE + h + 1] * wscale, (DSA_QB, LANES))
    for h in range(B_HEADS):
        qs_s[h // B_GROUP, (h % B_GROUP) * DSA_QB:(h % B_GROUP + 1) * DSA_QB, :] = (
            bq_ref[:, h * B_HEAD_DIM:(h + 1) * B_HEAD_DIM] * B_HEAD_DIM ** -0.5).astype(BF16)

    def p1(c, carry):
        c0 = pl.multiple_of(c * DSA_W, DSA_W)
        kc = kidx_ref[pl.ds(c0, DSA_W), :]
        sc = [jnp.zeros((DSA_QB, LANES), F32) for _ in range(nsub)]
        for h in range(IDX_HEADS):
            s = jnp.maximum(_nt(qih_s[h], kc), 0.0)
            wb = wb_s[h]
            for j in range(nsub):
                sc[j] = sc[j] + s[:, j * LANES:(j + 1) * LANES] * wb
        for j in range(nsub):
            kpos = c0 + j * LANES + lax.broadcasted_iota(jnp.int32, (1, LANES), 1)
            score_s[c * nsub + j] = jnp.where(kpos <= qpos, sc[j], -jnp.inf)
        return carry

    lax.fori_loop(0, nch, p1, 0)

    def stats(c, carry):
        mn, mx = carry
        for j in range(nsub):
            s = score_s[c * nsub + j]
            mx = jnp.maximum(mx, s)
            mn = jnp.minimum(mn, jnp.where(s > -jnp.inf, s, jnp.inf))
        return mn, mx

    mn, mx = lax.fori_loop(0, nch, stats, (jnp.full((DSA_QB, LANES), jnp.inf, F32),
                                           jnp.full((DSA_QB, LANES), -jnp.inf, F32)))
    lo0 = jnp.min(mn, axis=1, keepdims=True)
    hi0 = jnp.max(mx, axis=1, keepdims=True)

    def count_ge(thr):
        thr_b = jnp.broadcast_to(thr, (DSA_QB, LANES))

        def body(c, acc):
            for j in range(nsub):
                s = score_s[c * nsub + j]
                acc = acc + jnp.where(s >= thr_b, 1.0, 0.0)
            return acc

        acc = lax.fori_loop(0, nch, body, jnp.zeros((DSA_QB, LANES), F32))
        return jnp.sum(acc, axis=1, keepdims=True)

    thr = _bisect_threshold(count_ge, lo0, hi0, (qpos + 1).astype(F32), topk)
    thr_b = jnp.broadcast_to(thr, (DSA_QB, LANES))

    m_s[...] = jnp.full(m_s.shape, NEG, F32)
    l_s[...] = jnp.zeros(l_s.shape, F32)
    acc_s[...] = jnp.zeros(acc_s.shape, F32)

    def p3(c, with_bias):
        c0 = pl.multiple_of(c * DSA_W3, DSA_W3)
        madd = jnp.concatenate([jnp.where(score_s[c * nsub3 + j] >= thr_b, 0.0, NEG) for j in range(nsub3)], axis=1)
        kc = k_ref[pl.ds(c0, DSA_W3), :]
        vc = v_ref[pl.ds(c0, DSA_W3), :]
        rel = lambda qs, j: i * qsub + qs - (c * nsub3 + j)

        def bias_tile(h, qs, j):
            return jnp.where(rel(qs, j) == 0, bias_ref[h, 0], jnp.where(rel(qs, j) == 1, bias_ref[h, 1], 0.0))

        def scores(n):
            return _nt(qs_s[n], kc[:, n * B_HEAD_DIM:(n + 1) * B_HEAD_DIM])

        def softmax(n, lg):
            lg = lg.reshape(B_GROUP, DSA_QB, DSA_W3) + madd[None]
            if with_bias:
                lg = lg + jnp.stack([jnp.concatenate(
                    [jnp.concatenate([bias_tile(n * B_GROUP + gq, qs, j) for j in range(nsub3)], axis=1)
                     for qs in range(qsub)], axis=0) for gq in range(B_GROUP)])
            m_old = m_s[n]
            m_new = jnp.maximum(m_old, jnp.max(lg, axis=-1, keepdims=True))
            p = jnp.exp(lg - m_new)
            alpha = jnp.exp(m_old - m_new)
            l_s[n] = alpha * l_s[n] + jnp.sum(p, axis=-1, keepdims=True)
            m_s[n] = m_new
            pv = jnp.dot(p.reshape(B_GROUP * DSA_QB, DSA_W3).astype(BF16), vc[:, n * B_HEAD_DIM:(n + 1) * B_HEAD_DIM],
                         preferred_element_type=F32)
            return alpha, pv.reshape(B_GROUP, DSA_QB, B_HEAD_DIM)

        lgs = [scores(n) for n in range(B_KV_HEADS)]
        outs = [softmax(n, lgs[n]) for n in range(B_KV_HEADS)]
        for n in range(B_KV_HEADS):
            acc_s[n] = outs[n][0] * acc_s[n] + outs[n][1]

    n_far = jnp.maximum(i * qsub - 1, 0) // nsub3
    lax.fori_loop(0, n_far, lambda c, carry: (p3(c, False), carry)[1], 0)
    lax.fori_loop(n_far, nch3, lambda c, carry: (p3(c, True), carry)[1], 0)

    for h in range(B_HEADS):
        n, gq = h // B_GROUP, h % B_GROUP
        o_ref[:, h * B_HEAD_DIM:(h + 1) * B_HEAD_DIM] = (acc_s[n, gq] / l_s[n, gq]).astype(o_ref.dtype)


def dsa_prompt(z, ztail, kidx_bf, k_bf, v_bf, bias_tiles):
    m = z.shape[0]
    topk = min(TOPK_MAX, m // 4)
    wq = B_HEADS * B_HEAD_DIM
    wi2 = IDX_HEADS * IDX_DIM // 2
    kern = functools.partial(_dsa_prompt_kernel, topk=topk)
    return pl.pallas_call(
        kern,
        grid=(m // DSA_QB,),
        in_specs=[pl.BlockSpec((DSA_QB, wi2), lambda i: (i, C_IQ // wi2)),
                  pl.BlockSpec((DSA_QB, wi2), lambda i: (i, C_IQ // wi2 + 1)),
                  pl.BlockSpec((DSA_QB, wq), lambda i: (i, C_BQ // wq)),
                  pl.BlockSpec((DSA_QB, LANES), lambda i: (i, T_SM // LANES)),
                  pl.BlockSpec(kidx_bf.shape, lambda i: (0, 0)),
                  pl.BlockSpec(k_bf.shape, lambda i: (0, 0)),
                  pl.BlockSpec(v_bf.shape, lambda i: (0, 0)),
                  pl.BlockSpec(bias_tiles.shape, lambda i: (0, 0, 0, 0))],
        out_specs=pl.BlockSpec((DSA_QB, wq), lambda i: (i, 0)),
        out_shape=jax.ShapeDtypeStruct((m, wq), BF16),
        scratch_shapes=[pltpu.VMEM((m // LANES, DSA_QB, LANES), F32),
                        pltpu.VMEM((IDX_HEADS, DSA_QB, IDX_DIM), BF16),
                        pltpu.VMEM((B_KV_HEADS, B_GROUP * DSA_QB, B_HEAD_DIM), BF16),
                        pltpu.VMEM((IDX_HEADS, DSA_QB, LANES), F32),
                        pltpu.VMEM((B_KV_HEADS, B_GROUP, DSA_QB, 1), F32),
                        pltpu.VMEM((B_KV_HEADS, B_GROUP, DSA_QB, 1), F32),
                        pltpu.VMEM((B_KV_HEADS, B_GROUP, DSA_QB, B_HEAD_DIM), F32)],
        compiler_params=_cparams("arbitrary"),
        name="dsa_prompt",
    )(z, z, z, ztail, kidx_bf, k_bf, v_bf, bias_tiles)


def _page_copies(table_ref, b, n_pages, src_hbm, dst, sem, rows_per_page=PAGE_SIZE):
    def copy(p):
        return pltpu.make_async_copy(src_hbm.at[table_ref[b, p]],
                                     dst.at[pl.ds(p * rows_per_page, rows_per_page)], sem)
    return copy


def _dsa_scores_kernel(pt_ref, iq_ref, iw_ref, iknew_ref, kidx_hbm, o_ref, buf, sem, *, n_pages):
    b = pl.program_id(0)
    nb = pl.num_programs(0)
    past = n_pages * PAGE_SIZE

    def page_copy(bb, slot, p):
        return pltpu.make_async_copy(kidx_hbm.at[pt_ref[bb, p]],
                                     buf.at[slot, :, pl.ds(pl.multiple_of(p * PAGE_SIZE, PAGE_SIZE), PAGE_SIZE)],
                                     sem.at[slot])

    def start(bb, slot):
        lax.fori_loop(0, n_pages, lambda p, c: (page_copy(bb, slot, p).start(), c)[1], 0)

    def wait(bb, slot):
        lax.fori_loop(0, n_pages, lambda p, c: (page_copy(bb, slot, p).wait(), c)[1], 0)

    slot = b % 2

    @pl.when(b == 0)
    def _():
        start(0, 0)

    @pl.when(b + 1 < nb)
    def _():
        start(b + 1, 1 - slot)

    wait(b, slot)

    r16 = lambda a: a.astype(BF16).astype(F32)
    qi = iq_ref[0].astype(BF16)
    wcol = r16(iw_ref[0]) * (IDX_DIM ** -0.5 * IDX_HEADS ** -0.5)
    s = r16(jnp.maximum(jnp.dot(qi, buf[slot].astype(BF16), preferred_element_type=F32), 0.0))
    o_ref[0, :, 0:past] = jnp.sum(s * wcol, axis=0, keepdims=True)
    sn = r16(jnp.maximum(jnp.dot(qi, iknew_ref[0].astype(BF16), preferred_element_type=F32), 0.0))
    sn = jnp.sum(sn * wcol, axis=0, keepdims=True)
    lane = lax.broadcasted_iota(jnp.int32, (1, LANES), 1)
    o_ref[0, :, past:past + LANES] = jnp.where(lane == 0, sn, -jnp.inf)


def dsa_scores(page_table, iq, iw, iknew_pad, cache_kidx):
    b, n_pages = page_table.shape
    past = n_pages * PAGE_SIZE
    kern = functools.partial(_dsa_scores_kernel, n_pages=n_pages)
    gs = pltpu.PrefetchScalarGridSpec(
        num_scalar_prefetch=1,
        grid=(b,),
        in_specs=[pl.BlockSpec((1, IDX_HEADS, IDX_DIM), lambda i, pt: (i, 0, 0)),
                  pl.BlockSpec((1, IDX_HEADS, 1), lambda i, pt: (i, 0, 0)),
                  pl.BlockSpec((1, IDX_DIM, LANES), lambda i, pt: (i, 0, 0)),
                  pl.BlockSpec(memory_space=pl.ANY)],
        out_specs=pl.BlockSpec((1, 1, past + LANES), lambda i, pt: (i, 0, 0)),
        scratch_shapes=[pltpu.VMEM((2, IDX_DIM, past), F32), pltpu.SemaphoreType.DMA((2,))],
    )
    return pl.pallas_call(
        kern, grid_spec=gs,
        out_shape=jax.ShapeDtypeStruct((b, 1, past + LANES), F32),
        compiler_params=_cparams("arbitrary"),
        name="dsa_scores",
    )(page_table, iq, iw, iknew_pad, cache_kidx)


def _dsa_threshold_kernel(s_ref, thr_ref, *, topk):
    s = s_ref[...]
    nb = s.shape[0]
    finite = s > -jnp.inf
    lo0 = jnp.min(jnp.where(finite, s, jnp.inf), axis=1, keepdims=True)
    hi0 = jnp.max(s, axis=1, keepdims=True)
    cnt0 = jnp.sum(jnp.where(finite, 1.0, 0.0), axis=1, keepdims=True)

    def count_ge(thr):
        return jnp.sum(jnp.where(s_ref[...] >= thr, 1.0, 0.0), axis=1, keepdims=True)

    thr = _bisect_threshold(count_ge, lo0, hi0, cnt0, topk)
    thr_ref[...] = jnp.broadcast_to(thr, (nb, LANES))


def dsa_threshold(scores, topk):
    b, l = scores.shape
    return pl.pallas_call(
        functools.partial(_dsa_threshold_kernel, topk=topk),
        grid=(1,),
        in_specs=[pl.BlockSpec((b, l), lambda i: (0, 0))],
        out_specs=pl.BlockSpec((b, LANES), lambda i: (0, 0)),
        out_shape=jax.ShapeDtypeStruct((b, LANES), F32),
        compiler_params=_cparams("arbitrary"),
        name="dsa_threshold",
    )(scores)


def _dsa_decode_kernel(pt_ref, q_ref, s_ref, thr_ref, knew_ref, vnew_ref, bias_ref, k_hbm, v_hbm, o_ref,
                       kbuf, vbuf, sem, *, n_pages):
    b = pl.program_id(0)
    nb = pl.num_programs(0)
    past = n_pages * PAGE_SIZE

    rpp = PAGE_SIZE * B_KV_HEADS

    def copies(bb, slot):
        ck = _page_copies(pt_ref, bb, n_pages, k_hbm, kbuf.at[slot], sem.at[0, slot], rpp)
        cv = _page_copies(pt_ref, bb, n_pages, v_hbm, vbuf.at[slot], sem.at[1, slot], rpp)
        return ck, cv

    def start(bb, slot):
        ck, cv = copies(bb, slot)
        lax.fori_loop(0, n_pages, lambda p, c: (ck(p).start(), cv(p).start(), c)[2], 0)

    def wait(bb, slot):
        ck, cv = copies(bb, slot)
        lax.fori_loop(0, n_pages, lambda p, c: (ck(p).wait(), cv(p).wait(), c)[2], 0)

    slot = b % 2

    @pl.when(b == 0)
    def _():
        kbuf[:, n_pages * rpp:, :] = jnp.zeros((2, rpp, B_HEAD_DIM), F32)
        vbuf[:, n_pages * rpp:, :] = jnp.zeros((2, rpp, B_HEAD_DIM), F32)
        start(0, 0)

    @pl.when(b + 1 < nb)
    def _():
        start(b + 1, 1 - slot)

    kbuf[slot, n_pages * rpp:n_pages * rpp + 8, :] = knew_ref[0]
    vbuf[slot, n_pages * rpp:n_pages * rpp + 8, :] = vnew_ref[0]
    wait(b, slot)

    sel = s_ref[0] >= thr_ref[0][:, 0:1]
    n_keys = past + PAGE_SIZE
    outs = []
    for n in range(B_KV_HEADS):
        kn = kbuf[slot, pl.ds(n, n_keys, stride=B_KV_HEADS), :].astype(BF16)
        vn = vbuf[slot, pl.ds(n, n_keys, stride=B_KV_HEADS), :].astype(BF16)
        qn = q_ref[0, n].astype(BF16)
        lg = _nt(qn, kn) * B_HEAD_DIM ** -0.5 + bias_ref[n]
        m = jnp.max(jnp.where(sel, lg, NEG), axis=1, keepdims=True)
        p = jnp.where(sel, jnp.exp(lg - m), 0.0)
        p = p / jnp.sum(p, axis=1, keepdims=True)
        outs.append(jnp.dot(p.astype(BF16), vn, preferred_element_type=F32))
    o_ref[0] = jnp.concatenate(outs, axis=0).astype(o_ref.dtype)


def dsa_decode(page_table, q8, scores, thr, knew8, vnew8, bias_rows, cache_k2, cache_v2):
    b, n_pages = page_table.shape
    past = n_pages * PAGE_SIZE
    l = past + LANES
    wkv = B_KV_HEADS * B_HEAD_DIM
    kern = functools.partial(_dsa_decode_kernel, n_pages=n_pages)
    gs = pltpu.PrefetchScalarGridSpec(
        num_scalar_prefetch=1,
        grid=(b,),
        in_specs=[pl.BlockSpec((1, B_KV_HEADS, 8, B_HEAD_DIM), lambda i, pt: (i, 0, 0, 0)),
                  pl.BlockSpec((1, 1, l), lambda i, pt: (i, 0, 0)),
                  pl.BlockSpec((1, 1, LANES), lambda i, pt: (i, 0, 0)),
                  pl.BlockSpec((1, 8, B_HEAD_DIM), lambda i, pt: (i, 0, 0)),
                  pl.BlockSpec((1, 8, B_HEAD_DIM), lambda i, pt: (i, 0, 0)),
                  pl.BlockSpec((B_KV_HEADS, 8, l), lambda i, pt: (0, 0, 0)),
                  pl.BlockSpec(memory_space=pl.ANY),
                  pl.BlockSpec(memory_space=pl.ANY)],
        out_specs=pl.BlockSpec((1, 2 * 8, B_HEAD_DIM), lambda i, pt: (i, 0, 0)),
        scratch_shapes=[pltpu.VMEM((2, l * B_KV_HEADS, B_HEAD_DIM), F32),
                        pltpu.VMEM((2, l * B_KV_HEADS, B_HEAD_DIM), F32),
                        pltpu.SemaphoreType.DMA((2, 2))],
    )
    return pl.pallas_call(
        kern, grid_spec=gs,
        out_shape=jax.ShapeDtypeStruct((b, 2 * 8, B_HEAD_DIM), F32),
        compiler_params=_cparams("arbitrary"),
        name="dsa_decode",
    )(page_table, q8, scores, thr, knew8, vnew8, bias_rows, cache_k2, cache_v2)


def _merge_kernel(ya_ref, yb_ref, ga_ref, gb_ref, wa_ref, wb_ref, o_ref):
    a = _mm(ya_ref[...], wa_ref[...])
    bb = _mm(yb_ref[...], wb_ref[...])
    o_ref[...] = (jax.nn.sigmoid(ga_ref[...]) * a + jax.nn.sigmoid(gb_ref[...]) * bb).astype(o_ref.dtype)


def merge(ya, yb, z, ga_col, gb_col, wa, wb, *, tm, tn):
    m, k = ya.shape
    n = wa.shape[1]
    return pl.pallas_call(
        _merge_kernel,
        grid=(m // tm, n // tn),
        in_specs=[pl.BlockSpec((tm, k), lambda i, j: (i, 0)),
                  pl.BlockSpec((tm, k), lambda i, j: (i, 0)),
                  pl.BlockSpec((tm, tn), lambda i, j: (i, ga_col // tn + j)),
                  pl.BlockSpec((tm, tn), lambda i, j: (i, gb_col // tn + j)),
                  pl.BlockSpec((k, tn), lambda i, j: (0, j)),
                  pl.BlockSpec((k, tn), lambda i, j: (0, j))],
        out_specs=pl.BlockSpec((tm, tn), lambda i, j: (i, j)),
        out_shape=jax.ShapeDtypeStruct((m, n), wa.dtype),
        compiler_params=_cparams("parallel", "arbitrary"),
        name="merge",
    )(ya, yb, z, z, wa, wb)


def _cross_prompt_kernel(x_ref, g_ref, wq_ref, mk_ref, mv_ref, wo_ref, o_ref, *, n_in_tiles):
    x = x_ref[...]
    ms = jnp.mean(x * x, axis=-1, keepdims=True)
    h = (x * lax.rsqrt(ms + EPS) * g_ref[...]).astype(BF16)
    q = jnp.dot(h, wq_ref[...], preferred_element_type=F32)
    outs = []
    for hh in range(X_HEADS):
        cols = slice(hh * X_HEAD_DIM, (hh + 1) * X_HEAD_DIM)
        lg = _nt(q[:, cols].astype(BF16), mk_ref[:, cols]) * X_HEAD_DIM ** -0.5
        mx = jnp.max(lg, axis=1, keepdims=True)
        p = jnp.exp(lg - mx)
        p = p / jnp.sum(p, axis=1, keepdims=True)
        outs.append(jnp.dot(p.astype(BF16), mv_ref[:, cols], preferred_element_type=F32).astype(BF16))
    att = jnp.concatenate(outs, axis=1)
    res = x + jnp.dot(att, wo_ref[...], preferred_element_type=F32)
    o_ref[...] = jnp.where(pl.program_id(0) < n_in_tiles, res, 0.0)


def cross_prompt(x, g, wq, mk, mv, wo, *, tm, out_rows=None):
    m, d = x.shape
    out_rows = m if out_rows is None else out_rows
    n_in = m // tm
    full = lambda a: pl.BlockSpec(a.shape, lambda i: (0,) * a.ndim)
    g2 = g.reshape(1, d)
    return pl.pallas_call(
        functools.partial(_cross_prompt_kernel, n_in_tiles=n_in),
        grid=(pl.cdiv(out_rows, tm),),
        in_specs=[pl.BlockSpec((tm, d), lambda i: (jnp.minimum(i, n_in - 1), 0)),
                  full(g2), full(wq), full(mk), full(mv), full(wo)],
        out_specs=pl.BlockSpec((tm, d), lambda i: (i, 0)),
        out_shape=jax.ShapeDtypeStruct((out_rows, d), F32),
        compiler_params=_cparams("parallel"),
        name="cross_prompt",
    )(x, g2, wq, mk, mv, wo)


def _cross_step_kernel(q_ref, mk_ref, mv_ref, o_ref):
    r16 = lambda a: a.astype(BF16).astype(F32)
    q = r16(q_ref[0])
    outs = []
    for hh in range(X_HEADS):
        cols = slice(hh * X_HEAD_DIM, (hh + 1) * X_HEAD_DIM)
        kh = r16(mk_ref[0, :, cols])
        vh = r16(mv_ref[0, :, cols])
        lg = jnp.sum(kh * q[:, cols], axis=1, keepdims=True) * X_HEAD_DIM ** -0.5
        mx = jnp.max(lg, axis=0, keepdims=True)
        p = jnp.exp(lg - mx)
        p = r16(p / jnp.sum(p, axis=0, keepdims=True))
        outs.append(jnp.sum(p * vh, axis=0, keepdims=True))
    o_ref[0] = jnp.concatenate(outs, axis=1).astype(o_ref.dtype)


def cross_step(q, mk, mv):
    b, w = q.shape
    mem = mk.shape[1]
    return pl.pallas_call(
        _cross_step_kernel,
        grid=(b,),
        in_specs=[pl.BlockSpec((1, 1, w), lambda i: (i, 0, 0)),
                  pl.BlockSpec((1, mem, w), lambda i: (i, 0, 0)),
                  pl.BlockSpec((1, mem, w), lambda i: (i, 0, 0))],
        out_specs=pl.BlockSpec((1, 1, w), lambda i: (i, 0, 0)),
        out_shape=jax.ShapeDtypeStruct((b, 1, w), F32),
        compiler_params=_cparams("arbitrary"),
        name="cross_step",
    )(q.reshape(b, 1, w), mk, mv)


def _pack_bf16_pairs(x):
    c = x.shape[1] // 2
    u = pltpu.bitcast(x.astype(BF16).astype(F32), jnp.uint32)
    return lax.shift_right_logical(u[:, :c], jnp.uint32(16)) | u[:, c:]


def _unpack_bf16_pairs(u):
    lo = pltpu.bitcast(lax.shift_left(u, jnp.uint32(16)), F32)
    hi = pltpu.bitcast(u & jnp.uint32(0xFFFF0000), F32)
    return jnp.concatenate([lo, hi], axis=1)


def _router_kernel(x_ref, g_ref, w_ref, b_ref, hf_ref, route_ref):
    x = x_ref[...]
    ms = jnp.mean(x * x, axis=-1, keepdims=True)
    hf = x * lax.rsqrt(ms + EPS) * g_ref[...]
    hf_ref[...] = _pack_bf16_pairs(hf)
    lg = _mm(hf, w_ref[...]) + b_ref[...]
    tm = lg.shape[0]
    lane = lax.broadcasted_iota(jnp.int32, (tm, LANES), 1)
    big = jnp.int32(LANES)
    is_g = lane < N_GROUPS
    gmax = jnp.max(jnp.where(is_g, lg, -jnp.inf), axis=1, keepdims=True)
    grp = jnp.min(jnp.where(is_g & (lg == gmax), lane, big), axis=1, keepdims=True)
    p_grp = 1.0 / jnp.sum(jnp.where(is_g, jnp.exp(lg - gmax), 0.0), axis=1, keepdims=True)
    e_lo = N_GROUPS + grp * EXP_PER_GROUP
    in_g = (lane >= e_lo) & (lane < e_lo + EXP_PER_GROUP)
    v1 = jnp.max(jnp.where(in_g, lg, -jnp.inf), axis=1, keepdims=True)
    i1 = jnp.min(jnp.where(in_g & (lg == v1), lane, big), axis=1, keepdims=True)
    rest = in_g & (lane != i1)
    v2 = jnp.max(jnp.where(rest, lg, -jnp.inf), axis=1, keepdims=True)
    i2 = jnp.min(jnp.where(rest & (lg == v2), lane, big), axis=1, keepdims=True)
    e2 = jnp.exp(v2 - v1)
    g1 = p_grp / (1.0 + e2)
    g2 = p_grp * e2 / (1.0 + e2)
    r = jnp.where(lane == 0, (i1 - N_GROUPS).astype(F32),
                  jnp.where(lane == 1, (i2 - N_GROUPS).astype(F32),
                            jnp.where(lane == 2, g1, jnp.where(lane == 3, g2, 0.0))))
    route_ref[...] = r


def router(x, g, w_pad, b_pad, *, tm):
    m, d = x.shape
    return pl.pallas_call(
        _router_kernel,
        grid=(pl.cdiv(m, tm),),
        in_specs=[pl.BlockSpec((tm, d), lambda i: (i, 0)),
                  pl.BlockSpec((1, d), lambda i: (0, 0)),
                  pl.BlockSpec((d, LANES), lambda i: (0, 0)),
                  pl.BlockSpec((1, LANES), lambda i: (0, 0))],
        out_specs=[pl.BlockSpec((tm, d // 2), lambda i: (i, 0)),
                   pl.BlockSpec((tm, LANES), lambda i: (i, 0))],
        out_shape=[jax.ShapeDtypeStruct((m, d // 2), jnp.uint32), jax.ShapeDtypeStruct((m, LANES), F32)],
        compiler_params=_cparams("parallel"),
        name="router",
    )(x, g.reshape(1, d), w_pad, b_pad)


def _moe_kernel(be_ref, nxt_ref, nused_ref, sbase_ref, nvalid_ref, padj_ref, order_ref, hf_hbm, wg_hbm, wu_hbm,
                wd_hbm, y_hbm, xbuf, obuf, wg_f, wu_f, wd_f, wg_s, wu_s, wd_s, sem_in, sem_out, sem_w,
                *, n_tokens):
    b = pl.program_id(0)
    nb = pl.num_programs(0)
    blk = xbuf.shape[1]
    xslot = b % 3
    used = b < nused_ref[0]
    next_used = b + 2 < nused_ref[0]
    n_assign = order_ref.shape[0]

    def slot_info(bb):
        base, nv, pad0 = sbase_ref[bb], nvalid_ref[bb], padj_ref[bb]

        def info(r):
            valid = r < nv
            asg = order_ref[jnp.minimum(base + r, n_assign - 1)]
            tok = lax.shift_right_logical(asg, 1)
            return jnp.where(valid, tok, 0), jnp.where(valid, (asg & 1) * n_tokens + tok, pad0 + r)
        return info

    def weight_copies(e):
        return (pltpu.make_async_copy(wg_hbm.at[e], wg_f, sem_w.at[0]),
                pltpu.make_async_copy(wu_hbm.at[e], wu_f, sem_w.at[1]),
                pltpu.make_async_copy(wd_hbm.at[e], wd_f, sem_w.at[2]))

    def gather_start(bb, sl):
        info = slot_info(bb)
        for r in range(blk):
            pltpu.make_async_copy(hf_hbm.at[pl.ds(info(r)[0], 1)], xbuf.at[sl, pl.ds(r, 1)],
                                  sem_in.at[sl]).start()

    def scatter_start(bb, sl):
        info = slot_info(bb)
        for r in range(blk):
            pltpu.make_async_copy(obuf.at[sl, pl.ds(r, 1)], y_hbm.at[pl.ds(info(r)[1], 1)],
                                  sem_out.at[sl]).start(priority=r % 2)

    def gather_wait(sl):
        pltpu.make_async_copy(hf_hbm.at[pl.ds(0, blk)], xbuf.at[sl], sem_in.at[sl]).wait()

    def scatter_wait(sl):
        pltpu.make_async_copy(obuf.at[sl], y_hbm.at[pl.ds(0, blk)], sem_out.at[sl]).wait()

    @pl.when(b == 0)
    def _():
        for cp in weight_copies(be_ref[0]):
            cp.start(priority=1)
        gather_start(0, 0)
        obuf[...] = jnp.zeros(obuf.shape, obuf.dtype)

    @pl.when(jnp.logical_and(b == 0, 1 < nused_ref[0]))
    def _():
        gather_start(1, 1)

    changed = jnp.logical_and(used, jnp.logical_or(b == 0, be_ref[b] != be_ref[jnp.maximum(b - 1, 0)]))

    @pl.when(changed)
    def _():
        for cp in weight_copies(be_ref[b]):
            cp.wait()
        wg_s[...] = wg_f[...].astype(BF16)
        wu_s[...] = wu_f[...].astype(BF16)
        wd_s[...] = wd_f[...].astype(BF16)

    @pl.when(jnp.logical_and(changed, nxt_ref[b] >= 0))
    def _():
        for cp in weight_copies(nxt_ref[b]):
            cp.start(priority=1)

    @pl.when(used)
    def _():
        gather_wait(xslot)

    @pl.when(b >= 3)
    def _():
        scatter_wait(xslot)

    def step(prefetch, flush_prev, compute):
        if prefetch:
            gather_start(b + 2, (b + 2) % 3)
        if flush_prev:
            scatter_start(b - 1, (b + 2) % 3)
        if compute:
            x = _unpack_bf16_pairs(xbuf[xslot]).astype(BF16)
            gg = jnp.dot(x, wg_s[...], preferred_element_type=F32)
            uu = jnp.dot(x, wu_s[...], preferred_element_type=F32)
            a = (_silu(gg) * uu).astype(BF16)
            obuf[xslot] = _pack_bf16_pairs(jnp.dot(a, wd_s[...], preferred_element_type=F32))

    first, last = b == 0, b == nb - 1
    land, lnot = jnp.logical_and, jnp.logical_not
    pl.when(land(first, next_used))(lambda: step(True, False, True))
    pl.when(land(first, lnot(next_used)))(lambda: step(False, False, True))
    pl.when(land(lnot(first), land(used, next_used)))(lambda: step(True, True, True))
    pl.when(land(lnot(first), land(used, lnot(next_used))))(lambda: step(False, True, True))
    pl.when(land(lnot(first), lnot(used)))(lambda: step(False, True, False))

    @pl.when(last)
    def _():
        scatter_start(b, xslot)
        scatter_wait(xslot)

    @pl.when(jnp.logical_and(last, b >= 1))
    def _():
        scatter_wait((b + 2) % 3)

    @pl.when(jnp.logical_and(last, b >= 2))
    def _():
        scatter_wait((b + 1) % 3)


def moe_experts(tables, hf, w_g, w_u, w_d, *, blk, out_rows):
    n_blocks = tables[0].shape[0]
    dp = hf.shape[1]
    d = 2 * dp
    ff = w_g.shape[2]
    gs = pltpu.PrefetchScalarGridSpec(
        num_scalar_prefetch=len(tables),
        grid=(n_blocks,),
        in_specs=[pl.BlockSpec(memory_space=pl.ANY)] * 4,
        out_specs=pl.BlockSpec(memory_space=pl.ANY),
        scratch_shapes=[pltpu.VMEM((3, blk, dp), jnp.uint32), pltpu.VMEM((3, blk, dp), jnp.uint32),
                        pltpu.VMEM((d, ff), F32), pltpu.VMEM((d, ff), F32), pltpu.VMEM((ff, d), F32),
                        pltpu.VMEM((d, ff), BF16), pltpu.VMEM((d, ff), BF16), pltpu.VMEM((ff, d), BF16),
                        pltpu.SemaphoreType.DMA((3,)), pltpu.SemaphoreType.DMA((3,)),
                        pltpu.SemaphoreType.DMA((3,))],
    )
    return pl.pallas_call(
        functools.partial(_moe_kernel, n_tokens=hf.shape[0]), grid_spec=gs,
        out_shape=jax.ShapeDtypeStruct((out_rows, dp), jnp.uint32),
        compiler_params=_cparams("arbitrary"),
        name="moe_experts",
    )(*tables, hf, w_g, w_u, w_d)


def _combine_kernel(x_ref, route_ref, gf_ref, y1_ref, y2_ref, op_ref, os_ref):
    i = pl.program_id(0)
    route = route_ref[...]
    x = (x_ref[...] + route[:, 2:3] * _unpack_bf16_pairs(y1_ref[...])
         + route[:, 3:4] * _unpack_bf16_pairs(y2_ref[...]))
    ms = jnp.mean(x * x, axis=-1, keepdims=True)
    out = x * lax.rsqrt(ms + EPS) * gf_ref[...]

    @pl.when(i < pl.num_programs(0) - 1)
    def _():
        op_ref[...] = out

    @pl.when(i == pl.num_programs(0) - 1)
    def _():
        os_ref[...] = out[:os_ref.shape[0]]


def combine(x, route, gf, y, plane, n_prompt, *, tm):
    m, d = x.shape
    n_tiles = n_prompt // tm
    assert n_prompt % tm == 0 and 0 < m - n_prompt <= tm
    return pl.pallas_call(
        _combine_kernel,
        grid=(n_tiles + 1,),
        in_specs=[pl.BlockSpec((tm, d), lambda i: (i, 0)),
                  pl.BlockSpec((tm, LANES), lambda i: (i, 0)),
                  pl.BlockSpec((1, d), lambda i: (0, 0)),
                  pl.BlockSpec((tm, d // 2), lambda i: (i, 0)),
                  pl.BlockSpec((tm, d // 2), lambda i: (plane // tm + i, 0))],
        out_specs=[pl.BlockSpec((tm, d), lambda i: (jnp.minimum(i, n_tiles - 1), 0)),
                   pl.BlockSpec((m - n_prompt, d), lambda i: (0, 0))],
        out_shape=[jax.ShapeDtypeStruct((n_prompt, d), F32), jax.ShapeDtypeStruct((m - n_prompt, d), F32)],
        compiler_params=_cparams("arbitrary"),
        name="combine",
    )(x, route, gf.reshape(1, d), y, y)


def _t5_bucket(dist):
    dist = jnp.asarray(dist, jnp.int32)
    max_exact = REL_BUCKETS // 2
    dist_f = jnp.maximum(dist, 1).astype(F32)
    large = max_exact + (jnp.log(dist_f / max_exact) / math.log(REL_MAX_DIST / max_exact)
                         * (REL_BUCKETS - max_exact)).astype(jnp.int32)
    large = jnp.minimum(large, REL_BUCKETS - 1)
    return jnp.where(dist < max_exact, dist, large)


def _bias_tables(rel_bias, past):
    r = np.arange(LANES)
    diff = r[:, None] - r[None, :]
    buckets = jnp.stack([_t5_bucket(np.maximum(diff, 0)),
                         _t5_bucket(np.maximum(diff + LANES, 0)),
                         _t5_bucket(np.full((LANES, LANES), 2 * LANES))])
    def lookup(bkt):
        oh = (bkt.reshape(-1, 1) == jnp.arange(REL_BUCKETS)[None, :]).astype(F32)
        out = jnp.dot(oh, rel_bias.astype(F32), precision=lax.Precision.HIGHEST)
        return out.T.reshape((rel_bias.shape[1],) + bkt.shape)

    tiles = lookup(buckets)
    tiles = tiles - tiles[:, 2:3]
    dist = np.maximum(past - np.arange(past + LANES), 0)
    rows = lookup(_t5_bucket(dist))
    rows = rows.reshape(B_KV_HEADS, B_GROUP, past + LANES)
    rows = jnp.concatenate([rows, jnp.zeros_like(rows)], axis=1)
    return tiles, rows


def _dispatch(eid, n_tokens, blk):
    a = eid.shape[0]
    assert EXPERT_TOPK == 2 and a == EXPERT_TOPK * n_tokens
    n_blocks = -(-(a + N_EXPERTS * (blk - 1)) // blk)
    rows = n_blocks * blk
    order = jnp.argsort(eid).astype(jnp.int32)
    counts = jnp.sum(eid[:, None] == jnp.arange(N_EXPERTS)[None, :], axis=0).astype(jnp.int32)
    cum = jnp.cumsum(counts)
    starts = cum - counts
    padded = (counts + blk - 1) // blk * blk
    pad_end = jnp.cumsum(padded)
    pad_start = pad_end - padded
    blocks = jnp.arange(n_blocks, dtype=jnp.int32)
    block_e = jnp.minimum(jnp.sum(pad_end[None, :] <= (blocks * blk)[:, None], axis=1), N_EXPERTS - 1)
    n_used = pad_end[-1] // blk
    off = blocks * blk - pad_start[block_e]
    sbase = starts[block_e] + off
    nvalid = jnp.where(blocks < n_used, jnp.clip(counts[block_e] - off, 0, blk), 0)
    padj = a + blocks * blk - cum[block_e]
    new_run = jnp.concatenate([jnp.array([True]), block_e[1:] != block_e[:-1]])
    run_start = jnp.where(new_run & (blocks < n_used), blocks, n_blocks)
    nxt = jnp.concatenate([jnp.flip(lax.cummin(jnp.flip(run_start)))[1:], jnp.array([n_blocks])])
    next_e = jnp.where(nxt < n_blocks, block_e[jnp.minimum(nxt, n_blocks - 1)], -1)
    i32 = lambda v: v.astype(jnp.int32)
    return (i32(block_e), i32(next_e), i32(n_used).reshape(1), i32(sbase), i32(nvalid), i32(padj), order), rows


def kernel(x_prompt, x_sample, mem_prompt, cache_k, cache_v, cache_kidx, page_table, state_hgrn, cache_mem_k,
           cache_mem_v, norm_mix, w_in, hgrn_lb_logits, hgrn_norm, w_branch_a, w_branch_b, w_out, norm_cross, w_xq,
           w_xk, w_xv, w_xo, norm_ffn, w_router_group, b_router_group, w_router_expert, b_router_expert, w_exp_gate,
           w_exp_up, w_exp_down, rel_bias, norm_final):
    assert w_in.shape[0] == 1, "single-layer step"
    l = 0
    drop0 = lambda a: a.reshape(a.shape[1:])
    bp, t, d = x_prompt.shape
    db = x_sample.shape[0]
    past = page_table.shape[1] * PAGE_SIZE
    xp = x_prompt.reshape(bp * t, d)
    xs = x_sample.reshape(db, d)

    wi_t = jnp.transpose(drop0(w_in)).astype(BF16)
    assert wi_t.shape[0] - NZ_MAIN + T_PAD == NZ_TAIL
    w_tail_t = jnp.pad(wi_t[NZ_MAIN:], ((T_PAD, 0), (0, 0)))
    wa, wb, wo = w_branch_a[l].astype(BF16), w_branch_b[l].astype(BF16), w_out[l].astype(BF16)
    wxq, wxk, wxv, wxo = (w_xq[l].astype(BF16), w_xk[l].astype(BF16), w_xv[l].astype(BF16), w_xo[l].astype(BF16))
    w_route = jnp.pad(jnp.concatenate([w_router_group[l], w_router_expert[l]], axis=1),
                      ((0, 0), (0, LANES - N_GROUPS - N_EXPERTS))).astype(BF16)
    b_route = jnp.pad(jnp.concatenate([b_router_group[l], b_router_expert[l]]),
                      (0, LANES - N_GROUPS - N_EXPERTS)).reshape(1, LANES)
    bias_tiles, bias_rows = _bias_tables(rel_bias, past)

    zp, zpt = in_proj(xp, norm_mix[l], wi_t, NZ_MAIN, w_tail_t, tm=1024, tn=512)
    kp = zp[:, C_BK:C_BK + 256]
    vp = zp[:, C_BV:C_BV + 256]
    ikp = zpt[:, T_SM + IK_LANE:T_SM + IK_LANE + IDX_DIM]
    ya_p, st_p = hgrn_prompt(zp, hgrn_lb_logits, hgrn_norm[l])
    yb_p = dsa_prompt(zp, zpt, ikp.astype(BF16), kp.astype(BF16), vp.astype(BF16), bias_tiles)
    mg_p = merge(ya_p, yb_p, zpt, T_GA, T_GB, wa, wb, tm=1024, tn=512)
    x1p = matmul(mg_p, wo, xp, tm=1024, tn=512)
    memp = mem_prompt.reshape(-1, d)
    mk = matmul(memp, wxk, tm=memp.shape[0], tn=512)
    mv = matmul(memp, wxv, tm=memp.shape[0], tn=512)
    tmc = 128
    n = -(-(bp * t + db) // tmc) * tmc
    x2p = cross_prompt(x1p, norm_cross[l], wxq, mk.astype(BF16), mv.astype(BF16), wxo, tm=512, out_rows=n)

    zs, zst = in_proj(xs, norm_mix[l], wi_t, NZ_MAIN, w_tail_t, tm=db, tn=512)
    ks = zs[:, C_BK:C_BK + 256]
    vs = zs[:, C_BV:C_BV + 256]
    iks = zst[:, T_SM + IK_LANE:T_SM + IK_LANE + IDX_DIM]
    ya_s, st_s = hgrn_step(zs[:, :4 * A_HEADS * A_DK], hgrn_lb_logits, hgrn_norm[l], drop0(state_hgrn))
    iq_s = zs[:, C_IQ:C_IQ + IDX_HEADS * IDX_DIM].reshape(db, IDX_HEADS, IDX_DIM)
    iw_s = zst[:, T_SM + IW_LANE:T_SM + IW_LANE + IDX_HEADS].reshape(db, IDX_HEADS, 1)
    iknew_pad = jnp.pad(iks[:, :, None], ((0, 0), (0, 0), (0, LANES - 1)))
    scores = dsa_scores(page_table, iq_s, iw_s, iknew_pad,
                        jnp.swapaxes(drop0(cache_kidx), 1, 2)).reshape(db, past + LANES)
    topk_s = min(TOPK_MAX, (past + 1) // 4)
    thr = dsa_threshold(scores, topk_s)
    q8 = jnp.pad(zs[:, C_BQ:C_BQ + B_HEADS * B_HEAD_DIM].reshape(db, B_KV_HEADS, B_GROUP, B_HEAD_DIM),
                 ((0, 0), (0, 0), (0, 8 - B_GROUP), (0, 0)))
    knew8 = jnp.pad(ks.reshape(db, B_KV_HEADS, B_HEAD_DIM), ((0, 0), (0, 8 - B_KV_HEADS), (0, 0)))
    vnew8 = jnp.pad(vs.reshape(db, B_KV_HEADS, B_HEAD_DIM), ((0, 0), (0, 8 - B_KV_HEADS), (0, 0)))
    n_pool = cache_k.shape[1]
    ob = dsa_decode(page_table, q8, scores.reshape(db, 1, -1), thr.reshape(db, 1, LANES), knew8, vnew8, bias_rows,
                    cache_k.reshape(n_pool, PAGE_SIZE * B_KV_HEADS, B_HEAD_DIM),
                    cache_v.reshape(n_pool, PAGE_SIZE * B_KV_HEADS, B_HEAD_DIM))
    yb_s = ob.reshape(db, B_KV_HEADS, 8, B_HEAD_DIM)[:, :, :B_GROUP].reshape(db, B_HEADS * B_HEAD_DIM)
    mg_s = merge(ya_s.reshape(db, -1), yb_s, zst, T_GA, T_GB, wa, wb, tm=db, tn=512)
    x1s = matmul(mg_s, wo, xs, tm=db, tn=512)
    qx_s = norm_matmul(x1s, norm_cross[l], wxq, tm=db, tn=512)
    mem = cache_mem_k.shape[2]
    att_s = cross_step(qx_s, cache_mem_k.reshape(db, mem, -1), cache_mem_v.reshape(db, mem, -1))
    x2 = matmul(att_s.reshape(db, -1), wxo, x1s, tm=db, tn=512, into=(x2p, bp * t))

    hf, route = router(x2, norm_ffn[l], w_route, b_route, tm=256)
    eid = route[:, :EXPERT_TOPK].astype(jnp.int32).reshape(-1)
    tables, rows = _dispatch(eid, n, MOE_ROWS)
    ye = moe_experts(tables, hf, drop0(w_exp_gate), drop0(w_exp_up), drop0(w_exp_down), blk=MOE_ROWS, out_rows=rows)
    y_p, y_s = combine(x2, route, norm_final, ye, n, bp * t, tm=tmc)

    y_prompt = y_p.reshape(bp, t, d)
    y_sample = y_s[:db].reshape(db, 1, d)
    return (y_prompt, y_sample,
            kp.reshape(1, bp, t, B_KV_HEADS, B_HEAD_DIM), vp.reshape(1, bp, t, B_KV_HEADS, B_HEAD_DIM),
            ikp.reshape(1, bp, t, IDX_DIM),
            jnp.swapaxes(st_p, 1, 2).reshape(1, bp, A_HEADS, A_DK, A_DV),
            mk.reshape(1, bp, -1, X_HEADS, X_HEAD_DIM), mv.reshape(1, bp, -1, X_HEADS, X_HEAD_DIM),
            ks.reshape(1, db, 1, B_KV_HEADS, B_HEAD_DIM), vs.reshape(1, db, 1, B_KV_HEADS, B_HEAD_DIM),
            iks.reshape(1, db, 1, IDX_DIM),
            st_s.reshape(1, db, A_HEADS, A_DK, A_DV))
```

```python
import functools
import math

import jax
import jax.numpy as jnp
import numpy as np
from jax import lax
from jax.experimental import pallas as pl
from jax.experimental.pallas import tpu as pltpu

F32 = jnp.float32
BF16 = jnp.bfloat16
EPS = 1e-6

D_MODEL = 2048
A_HEADS, A_DK, A_DV = 8, 128, 128
B_HEADS, B_KV_HEADS, B_HEAD_DIM = 8, 2, 128
B_GROUP = B_HEADS // B_KV_HEADS
IDX_HEADS, IDX_DIM = 16, 64
TOPK_MAX = 256
PAGE_SIZE = 128
REL_BUCKETS, REL_MAX_DIST = 32, 128
X_HEADS, X_HEAD_DIM = 4, 128
N_GROUPS, EXP_PER_GROUP = 4, 8
N_EXPERTS = N_GROUPS * EXP_PER_GROUP
EXPERT_TOPK = 2
EXPERT_FF = 512
MOE_ROWS = 256

LANES = 128
VMEM_LIMIT = 56 * 1024 * 1024

NEG = -1e30

C_AQ, C_AF, C_AI, C_AG, C_BQ, C_BK, C_BV, C_IQ = 0, 1024, 2048, 3072, 4096, 5120, 5376, 5632
NZ_MAIN = 6656
T_PAD = 432
T_SM, IW_LANE, IK_LANE = 384, 48, 64
T_GA, T_GB = 512, 2560
NZ_TAIL = 4608


def _cparams(*sem):
    return pltpu.CompilerParams(dimension_semantics=sem, vmem_limit_bytes=VMEM_LIMIT)


def _silu(x):
    return x * jax.nn.sigmoid(x)


def _nt(a, b):
    return lax.dot_general(a, b, (((1,), (1,)), ((), ())), preferred_element_type=F32)


def _nt_f32(a, b):
    return lax.dot_general(a, b, (((1,), (1,)), ((), ())), preferred_element_type=F32,
                           precision=lax.Precision.HIGHEST)


def _mm(a, w):
    return jnp.dot(a.astype(BF16), w.astype(BF16), preferred_element_type=F32)


def _norm_matmul_kernel(x_ref, g_ref, w_ref, o_ref, h_ref):
    @pl.when(pl.program_id(1) == 0)
    def _():
        x = x_ref[...]
        ms = jnp.mean(x * x, axis=-1, keepdims=True)
        h_ref[...] = (x * lax.rsqrt(ms + EPS) * g_ref[...]).astype(h_ref.dtype)

    o_ref[...] = _mm(h_ref[...], w_ref[...])


def norm_matmul(x, g, w, *, tm, tn):
    m, k = x.shape
    n = w.shape[1]
    assert n % tn == 0
    return pl.pallas_call(
        _norm_matmul_kernel,
        grid=(m // tm, n // tn),
        in_specs=[pl.BlockSpec((tm, k), lambda i, j: (i, 0)),
                  pl.BlockSpec((1, k), lambda i, j: (0, 0)),
                  pl.BlockSpec((k, tn), lambda i, j: (0, j))],
        out_specs=pl.BlockSpec((tm, tn), lambda i, j: (i, j)),
        out_shape=jax.ShapeDtypeStruct((m, n), F32),
        scratch_shapes=[pltpu.VMEM((tm, k), BF16)],
        compiler_params=_cparams("parallel", "arbitrary"),
        name="norm_matmul",
    )(x, g.reshape(1, k), w)


def _in_proj_kernel(x_ref, g_ref, wa_ref, wb_ref, oa_ref, ob_ref, h_ref, *, na):
    j = pl.program_id(1)

    @pl.when(j == 0)
    def _():
        x = x_ref[...]
        ms = jnp.mean(x * x, axis=-1, keepdims=True)
        h_ref[...] = (x * lax.rsqrt(ms + EPS) * g_ref[...]).astype(h_ref.dtype)

    @pl.when(j < na)
    def _():
        oa_ref[...] = _nt(h_ref[...], wa_ref[...])

    @pl.when(j >= na)
    def _():
        ob_ref[...] = _nt(h_ref[...], wb_ref[...])


def in_proj(x, g, wa_t, a_rows, wb_t, *, tm, tn):
    m, k = x.shape
    na, nb = a_rows // tn, wb_t.shape[0] // tn
    assert a_rows % tn == 0 and wb_t.shape[0] % tn == 0
    a_idx = lambda j: jnp.minimum(j, na - 1)
    b_idx = lambda j: jnp.maximum(j - na, 0)
    return pl.pallas_call(
        functools.partial(_in_proj_kernel, na=na),
        grid=(m // tm, na + nb),
        in_specs=[pl.BlockSpec((tm, k), lambda i, j: (i, 0)),
                  pl.BlockSpec((1, k), lambda i, j: (0, 0)),
                  pl.BlockSpec((tn, k), lambda i, j: (a_idx(j), 0)),
                  pl.BlockSpec((tn, k), lambda i, j: (b_idx(j), 0))],
        out_specs=[pl.BlockSpec((tm, tn), lambda i, j: (i, a_idx(j))),
                   pl.BlockSpec((tm, tn), lambda i, j: (i, b_idx(j)))],
        out_shape=[jax.ShapeDtypeStruct((m, na * tn), F32), jax.ShapeDtypeStruct((m, nb * tn), F32)],
        scratch_shapes=[pltpu.VMEM((tm, k), BF16)],
        compiler_params=_cparams("parallel", "arbitrary"),
        name="in_proj",
    )(x, g.reshape(1, k), wa_t, wb_t)


def _matmul_res_kernel(x_ref, w_ref, r_ref, o_ref):
    o_ref[...] = r_ref[...] + _mm(x_ref[...], w_ref[...])


def _matmul_res_into_kernel(x_ref, w_ref, r_ref, buf_ref, o_ref):
    del buf_ref
    o_ref[...] = r_ref[...] + _mm(x_ref[...], w_ref[...])


def _matmul_kernel(x_ref, w_ref, o_ref):
    o_ref[...] = _mm(x_ref[...], w_ref[...])


def matmul(x, w, res=None, *, tm, tn, into=None):
    m, k = x.shape
    n = w.shape[1]
    in_specs = [pl.BlockSpec((tm, k), lambda i, j: (i, 0)),
                pl.BlockSpec((k, tn), lambda i, j: (0, j))]
    args = [x, w]
    kern = _matmul_kernel
    if res is not None:
        in_specs.append(pl.BlockSpec((tm, tn), lambda i, j: (i, j)))
        args.append(res)
        kern = _matmul_res_kernel
    out_shape, row_blk, aliases = jax.ShapeDtypeStruct((m, n), F32), 0, {}
    if into is not None:
        buf, row0 = into
        assert res is not None and row0 % tm == 0 and buf.shape[1] == n
        in_specs.append(pl.BlockSpec(memory_space=pl.ANY))
        args.append(buf)
        kern = _matmul_res_into_kernel
        out_shape, row_blk, aliases = jax.ShapeDtypeStruct(buf.shape, F32), row0 // tm, {len(args) - 1: 0}
    return pl.pallas_call(
        kern,
        grid=(m // tm, n // tn),
        in_specs=in_specs,
        out_specs=pl.BlockSpec((tm, tn), lambda i, j: (row_blk + i, j)),
        out_shape=out_shape,
        input_output_aliases=aliases,
        compiler_params=_cparams("parallel", "arbitrary"),
        name="matmul",
    )(*args)


HG_TB = 128
HG_C = 16
HG_H = HG_C // 2


def _hgrn_prompt_kernel(aq_ref, af_ref, ai_ref, ag_ref, lbl_ref, ng_ref, ya_ref, st_out_ref,
                        st_ref, q_s, k_s, g_s, v_s):
    t = pl.program_id(0)

    @pl.when(t == 0)
    def _():
        st_ref[...] = jnp.zeros_like(st_ref)

    lbl = lbl_ref[...]
    mx = jnp.max(lbl, axis=0, keepdims=True)
    ex = jnp.exp(lbl - mx)
    lb = ex[0:1, :] / jnp.sum(ex, axis=0, keepdims=True)

    f = lb + (1.0 - lb) * jax.nn.sigmoid(af_ref[...])
    logf = jnp.log(f)
    row = lax.broadcasted_iota(jnp.int32, (HG_TB, HG_TB), 0)
    col = lax.broadcasted_iota(jnp.int32, (HG_TB, HG_TB), 1)
    tri = jnp.where((row // HG_C == col // HG_C) & (col <= row), 1.0, 0.0).astype(BF16)
    g = jnp.zeros(logf.shape, F32)
    rem = logf
    for _ in range(3):
        part = rem.astype(BF16)
        g = g + jnp.dot(tri, part, preferred_element_type=F32)
        rem = rem - part.astype(F32)
    g_s[...] = g
    q_s[...] = _silu(aq_ref[...])
    k_s[...] = 1.0 - f
    v_s[...] = ai_ref[...]

    sub = lax.broadcasted_iota(jnp.int32, (HG_C, A_DK), 0)
    sub8 = lax.broadcasted_iota(jnp.int32, (HG_H, A_DK), 0)
    ng = ng_ref[...]

    def chunk(c, carry):
        r0 = pl.multiple_of(c * HG_C, HG_C)
        rows = pl.ds(r0, HG_C)
        for h in range(A_HEADS):
            cols = slice(h * A_DK, (h + 1) * A_DK)
            g = g_s[rows, cols]
            qh = q_s[rows, cols]
            kh = k_s[rows, cols]
            vh = v_s[rows, cols]
            halves = []
            for hb in range(2):
                rs = slice(hb * HG_H, (hb + 1) * HG_H)
                gb, qb, kb, vb = g[rs], qh[rs], kh[rs], vh[rs]
                ob = jnp.zeros((HG_H, A_DV), F32)
                for tt in range(HG_H):
                    d = gb[tt:tt + 1, :] - gb
                    e = jnp.exp(jnp.where(sub8 <= tt, d, -jnp.inf))
                    p = e * (qb[tt:tt + 1, :] * kb)
                    a_col = jnp.sum(p, axis=1, keepdims=True)
                    o_row = jnp.sum(a_col * vb, axis=0, keepdims=True)
                    ob = jnp.where(sub8 == tt, o_row, ob)
                halves.append(ob)
            o = jnp.concatenate(halves, axis=0)
            low = sub < HG_H
            g_mid = g[HG_H - 1:HG_H, :]
            q_hi = jnp.where(low, 0.0, qh * jnp.exp(jnp.minimum(g - g_mid, 0.0)))
            k_lo = jnp.where(low, kh * jnp.exp(jnp.minimum(g_mid - g, 0.0)), 0.0)
            st = st_ref[h]
            g_last = g[HG_C - 1:HG_C, :]
            kt = kh * jnp.exp(g_last - g)
            upd = lax.dot_general(vh.astype(BF16), jnp.concatenate([kt, k_lo], axis=1).astype(BF16),
                                  (((0,), (0,)), ((), ())), preferred_element_type=F32)
            lhs = jnp.concatenate([qh * jnp.exp(g), q_hi], axis=1).astype(BF16)
            rhs = jnp.concatenate([st, upd[:, A_DK:]], axis=1).astype(BF16)
            o = o + _nt(lhs, rhs)
            st_ref[h] = st * jnp.exp(g_last) + upd[:, :A_DK]
            on = o * lax.rsqrt(jnp.mean(o * o, axis=-1, keepdims=True) + EPS) * ng
            ya_ref[rows, cols] = (on * _silu(ag_ref[rows, cols])).astype(ya_ref.dtype)
        return carry

    lax.fori_loop(0, HG_TB // HG_C, chunk, 0)

    @pl.when(t == pl.num_programs(0) - 1)
    def _():
        st_out_ref[...] = st_ref[...]


def hgrn_prompt(z, lb_logits, norm_g):
    m = z.shape[0]
    w = A_HEADS * A_DK

    def zspec(cb):
        return pl.BlockSpec((HG_TB, w), lambda t, cb=cb: (t, cb))

    return pl.pallas_call(
        _hgrn_prompt_kernel,
        grid=(m // HG_TB,),
        in_specs=[zspec(C_AQ // w), zspec(C_AF // w), zspec(C_AI // w), zspec(C_AG // w),
                  pl.BlockSpec(lb_logits.shape, lambda t: (0, 0)),
                  pl.BlockSpec((1, A_DV), lambda t: (0, 0))],
        out_specs=[pl.BlockSpec((HG_TB, w), lambda t: (t, 0)),
                   pl.BlockSpec((A_HEADS, A_DV, A_DK), lambda t: (0, 0, 0))],
        out_shape=[jax.ShapeDtypeStruct((m, w), BF16),
                   jax.ShapeDtypeStruct((A_HEADS, A_DV, A_DK), F32)],
        scratch_shapes=[pltpu.VMEM((A_HEADS, A_DV, A_DK), F32)] + [pltpu.VMEM((HG_TB, w), F32)] * 4,
        compiler_params=_cparams("arbitrary"),
        name="hgrn_prompt",
    )(z, z, z, z, lb_logits, norm_g.reshape(1, A_DV))


def _hgrn_step_kernel(z_ref, lbl_ref, ng_ref, s_ref, ya_ref, s_out_ref):
    lbl = lbl_ref[...]
    mx = jnp.max(lbl, axis=0, keepdims=True)
    ex = jnp.exp(lbl - mx)
    lb = ex[0:1, :] / jnp.sum(ex, axis=0, keepdims=True)
    z = z_ref[0]
    w = A_HEADS * A_DK
    q = _silu(z[:, 0:w])
    f = lb + (1.0 - lb) * jax.nn.sigmoid(z[:, w:2 * w])
    kk = 1.0 - f
    v = z[:, 2 * w:3 * w]
    ag = z[:, 3 * w:4 * w]
    rows = []
    for h in range(A_HEADS):
        cols = slice(h * A_DK, (h + 1) * A_DK)
        rows += [f[:, cols], kk[:, cols], q[:, cols]]
    rows.append(jnp.zeros((LANES - 3 * A_HEADS, A_DK), F32))
    xt = jnp.concatenate(rows, axis=0).T
    ng = ng_ref[...]
    r16 = lambda a: a.astype(BF16).astype(F32)
    outs = []
    for h in range(A_HEADS):
        cols = slice(h * A_DV, (h + 1) * A_DV)
        fcol = xt[:, 3 * h:3 * h + 1]
        kcol = xt[:, 3 * h + 1:3 * h + 2]
        qcol = xt[:, 3 * h + 2:3 * h + 3]
        s_old = s_ref[0, h]
        s_out_ref[0, h] = fcol * s_old + kcol * v[:, cols]
        o = (jnp.sum(r16(qcol * fcol) * r16(s_old), axis=0, keepdims=True)
             + jnp.sum(qcol * kcol, axis=0, keepdims=True) * v[:, cols])
        on = o * lax.rsqrt(jnp.mean(o * o, axis=-1, keepdims=True) + EPS) * ng
        outs.append(on * _silu(ag[:, cols]))
    ya_ref[0] = jnp.concatenate(outs, axis=1).astype(ya_ref.dtype)


def hgrn_step(z4, lb_logits, norm_g, state):
    b = z4.shape[0]
    w = A_HEADS * A_DK
    return pl.pallas_call(
        _hgrn_step_kernel,
        grid=(b,),
        in_specs=[pl.BlockSpec((1, 1, 4 * w), lambda i: (i, 0, 0)),
                  pl.BlockSpec(lb_logits.shape, lambda i: (0, 0)),
                  pl.BlockSpec((1, A_DV), lambda i: (0, 0)),
                  pl.BlockSpec((1, A_HEADS, A_DK, A_DV), lambda i: (i, 0, 0, 0))],
        out_specs=[pl.BlockSpec((1, 1, w), lambda i: (i, 0, 0)),
                   pl.BlockSpec((1, A_HEADS, A_DK, A_DV), lambda i: (i, 0, 0, 0))],
        out_shape=[jax.ShapeDtypeStruct((b, 1, w), F32),
                   jax.ShapeDtypeStruct(state.shape, F32)],
        compiler_params=_cparams("arbitrary"),
        name="hgrn_step",
    )(z4.reshape(b, 1, 4 * w), lb_logits, norm_g.reshape(1, A_DV), state)


BISECT_MAX_ITERS = 48


def _bisect_threshold(count_ge, lo, hi, cnt_lo, topk):
    kf = float(topk)

    def cond(c):
        return jnp.logical_and(c[0] < BISECT_MAX_ITERS, c[-1] > 0.0)

    def body(c):
        it, lo, hi, cl, _ = c
        mid = 0.5 * lo + 0.5 * hi
        cm = count_ge(mid)
        ge = cm >= kf
        lo = jnp.where(ge, mid, lo)
        cl = jnp.where(ge, cm, cl)
        hi = jnp.where(ge, hi, mid)
        busy = jnp.max(jnp.where(cl > kf, 1.0, 0.0))
        return it + 1, lo, hi, cl, busy

    busy0 = jnp.max(jnp.where(cnt_lo > kf, 1.0, 0.0))
    out = lax.while_loop(cond, body, (jnp.int32(0), lo, hi, cnt_lo, busy0))
    return out[1]


DSA_QB = 128
DSA_W = 512
DSA_W3 = 1024


def _dsa_prompt_kernel(iq0_ref, iq1_ref, bq_ref, iw_ref, kidx_ref, k_ref, v_ref, bias_ref, o_ref,
                       score_s, qih_s, qs_s, wb_s, m_s, l_s, acc_s, *, topk):
    i = pl.program_id(0)
    nsub = DSA_W // LANES
    nsub3 = DSA_W3 // LANES
    qsub = DSA_QB // LANES
    nch3 = (i * DSA_QB + DSA_QB + DSA_W3 - 1) // DSA_W3
    nch = (i * DSA_QB + DSA_QB + DSA_W - 1) // DSA_W
    qpos = i * DSA_QB + lax.broadcasted_iota(jnp.int32, (DSA_QB, 1), 0)

    iw = iw_ref[...]
    wscale = IDX_DIM ** -0.5 * IDX_HEADS ** -0.5
    for h in range(IDX_HEADS):
        iq_ref, hh = (iq0_ref, h) if h < IDX_HEADS // 2 else (iq1_ref, h - IDX_HEADS // 2)
        qih_s[h] = iq_ref[:, hh * IDX_DIM:(hh + 1) * IDX_DIM].astype(BF16)
        wb_s[h] = jnp.broadcast_to(iw[:, IW_LANE + h:IW_LANE + h + 1] * wscale, (DSA_QB, LANES))
    for h in range(B_HEADS):
        qs_s[h // B_GROUP, (h % B_GROUP) * DSA_QB:(h % B_GROUP + 1) * DSA_QB, :] = (
            bq_ref[:, h * B_HEAD_DIM:(h + 1) * B_HEAD_DIM] * B_HEAD_DIM ** -0.5).astype(BF16)

    def p1(c, carry):
        c0 = pl.multiple_of(c * DSA_W, DSA_W)
        kc = kidx_ref[pl.ds(c0, DSA_W), :]
        sc = [jnp.zeros((DSA_QB, LANES), F32) for _ in range(nsub)]
        for h in range(IDX_HEADS):
            s = jnp.maximum(_nt(qih_s[h], kc), 0.0)
            wb = wb_s[h]
            for j in range(nsub):
                sc[j] = sc[j] + s[:, j * LANES:(j + 1) * LANES] * wb
        for j in range(nsub):
            kpos = c0 + j * LANES + lax.broadcasted_iota(jnp.int32, (1, LANES), 1)
            score_s[c * nsub + j] = jnp.where(kpos <= qpos, sc[j], -jnp.inf)
        return carry

    lax.fori_loop(0, nch, p1, 0)

    def fill(tile, carry):
        score_s[tile] = jnp.full((DSA_QB, LANES), -jnp.inf, F32)
        return carry

    lax.fori_loop(nch * nsub, nch3 * nsub3, fill, 0)

    def stats(c, carry):
        mn, mx = carry
        for j in range(nsub):
            s = score_s[c * nsub + j]
            mx = jnp.maximum(mx, s)
            mn = jnp.minimum(mn, jnp.where(s > -jnp.inf, s, jnp.inf))
        return mn, mx

    mn, mx = lax.fori_loop(0, nch, stats, (jnp.full((DSA_QB, LANES), jnp.inf, F32),
                                           jnp.full((DSA_QB, LANES), -jnp.inf, F32)))
    lo0 = jnp.min(mn, axis=1, keepdims=True)
    hi0 = jnp.max(mx, axis=1, keepdims=True)

    def count_ge(thr):
        thr_b = jnp.broadcast_to(thr, (DSA_QB, LANES))

        def body(c, acc):
            for j in range(nsub):
                s = score_s[c * nsub + j]
                acc = acc + jnp.where(s >= thr_b, 1.0, 0.0)
            return acc

        acc = lax.fori_loop(0, nch, body, jnp.zeros((DSA_QB, LANES), F32))
        return jnp.sum(acc, axis=1, keepdims=True)

    thr = _bisect_threshold(count_ge, lo0, hi0, (qpos + 1).astype(F32), topk)
    thr_b = jnp.broadcast_to(thr, (DSA_QB, LANES))

    m_s[...] = jnp.full(m_s.shape, NEG, F32)
    l_s[...] = jnp.zeros(l_s.shape, F32)
    acc_s[...] = jnp.zeros(acc_s.shape, F32)

    def p3(c, with_bias):
        c0 = pl.multiple_of(c * DSA_W3, DSA_W3)
        madd = jnp.concatenate([jnp.where(score_s[c * nsub3 + j] >= thr_b, 0.0, NEG) for j in range(nsub3)], axis=1)
        kc = k_ref[pl.ds(c0, DSA_W3), :]
        vc = v_ref[pl.ds(c0, DSA_W3), :]
        rel = lambda qs, j: i * qsub + qs - (c * nsub3 + j)

        def bias_tile(h, qs, j):
            return jnp.where(rel(qs, j) == 0, bias_ref[h, 0], jnp.where(rel(qs, j) == 1, bias_ref[h, 1], 0.0))

        def scores(n):
            return _nt(qs_s[n], kc[:, n * B_HEAD_DIM:(n + 1) * B_HEAD_DIM])

        def softmax(n, lg):
            lg = lg.reshape(B_GROUP, DSA_QB, DSA_W3) + madd[None]
            if with_bias:
                lg = lg + jnp.stack([jnp.concatenate(
                    [jnp.concatenate([bias_tile(n * B_GROUP + gq, qs, j) for j in range(nsub3)], axis=1)
                     for qs in range(qsub)], axis=0) for gq in range(B_GROUP)])
            m_old = m_s[n]
            m_new = jnp.maximum(m_old, jnp.max(lg, axis=-1, keepdims=True))
            p = jnp.exp(lg - m_new)
            alpha = jnp.exp(m_old - m_new)
            l_s[n] = alpha * l_s[n] + jnp.sum(p, axis=-1, keepdims=True)
            m_s[n] = m_new
            pv = jnp.dot(p.reshape(B_GROUP * DSA_QB, DSA_W3).astype(BF16), vc[:, n * B_HEAD_DIM:(n + 1) * B_HEAD_DIM],
                         preferred_element_type=F32)
            return alpha, pv.reshape(B_GROUP, DSA_QB, B_HEAD_DIM)

        lgs = [scores(n) for n in range(B_KV_HEADS)]
        outs = [softmax(n, lgs[n]) for n in range(B_KV_HEADS)]
        for n in range(B_KV_HEADS):
            acc_s[n] = outs[n][0] * acc_s[n] + outs[n][1]

    n_far = jnp.maximum(i * qsub - 1, 0) // nsub3
    lax.fori_loop(0, n_far, lambda c, carry: (p3(c, False), carry)[1], 0)
    lax.fori_loop(n_far, nch3, lambda c, carry: (p3(c, True), carry)[1], 0)

    for h in range(B_HEADS):
        n, gq = h // B_GROUP, h % B_GROUP
        o_ref[:, h * B_HEAD_DIM:(h + 1) * B_HEAD_DIM] = (acc_s[n, gq] / l_s[n, gq]).astype(o_ref.dtype)


def dsa_prompt(z, ztail, kidx_bf, k_bf, v_bf, bias_tiles):
    m = z.shape[0]
    topk = min(TOPK_MAX, m // 4)
    wq = B_HEADS * B_HEAD_DIM
    wi2 = IDX_HEADS * IDX_DIM // 2
    kern = functools.partial(_dsa_prompt_kernel, topk=topk)
    return pl.pallas_call(
        kern,
        grid=(m // DSA_QB,),
        in_specs=[pl.BlockSpec((DSA_QB, wi2), lambda i: (i, C_IQ // wi2)),
                  pl.BlockSpec((DSA_QB, wi2), lambda i: (i, C_IQ // wi2 + 1)),
                  pl.BlockSpec((DSA_QB, wq), lambda i: (i, C_BQ // wq)),
                  pl.BlockSpec((DSA_QB, LANES), lambda i: (i, T_SM // LANES)),
                  pl.BlockSpec(kidx_bf.shape, lambda i: (0, 0)),
                  pl.BlockSpec(k_bf.shape, lambda i: (0, 0)),
                  pl.BlockSpec(v_bf.shape, lambda i: (0, 0)),
                  pl.BlockSpec(bias_tiles.shape, lambda i: (0, 0, 0, 0))],
        out_specs=pl.BlockSpec((DSA_QB, wq), lambda i: (i, 0)),
        out_shape=jax.ShapeDtypeStruct((m, wq), BF16),
        scratch_shapes=[pltpu.VMEM((m // LANES, DSA_QB, LANES), F32),
                        pltpu.VMEM((IDX_HEADS, DSA_QB, IDX_DIM), BF16),
                        pltpu.VMEM((B_KV_HEADS, B_GROUP * DSA_QB, B_HEAD_DIM), BF16),
                        pltpu.VMEM((IDX_HEADS, DSA_QB, LANES), F32),
                        pltpu.VMEM((B_KV_HEADS, B_GROUP, DSA_QB, 1), F32),
                        pltpu.VMEM((B_KV_HEADS, B_GROUP, DSA_QB, 1), F32),
                        pltpu.VMEM((B_KV_HEADS, B_GROUP, DSA_QB, B_HEAD_DIM), F32)],
        compiler_params=_cparams("arbitrary"),
        name="dsa_prompt",
    )(z, z, z, ztail, kidx_bf, k_bf, v_bf, bias_tiles)


def _page_copies(table_ref, b, n_pages, src_hbm, dst, sem, rows_per_page=PAGE_SIZE):
    def copy(p):
        return pltpu.make_async_copy(src_hbm.at[table_ref[b, p]],
                                     dst.at[pl.ds(p * rows_per_page, rows_per_page)], sem)
    return copy


def _dsa_scores_kernel(pt_ref, iq_ref, iw_ref, iknew_ref, kidx_hbm, o_ref, buf, sem, *, n_pages):
    b = pl.program_id(0)
    nb = pl.num_programs(0)
    past = n_pages * PAGE_SIZE

    def page_copy(bb, slot, p):
        return pltpu.make_async_copy(kidx_hbm.at[pt_ref[bb, p]],
                                     buf.at[slot, :, pl.ds(pl.multiple_of(p * PAGE_SIZE, PAGE_SIZE), PAGE_SIZE)],
                                     sem.at[slot])

    def start(bb, slot):
        lax.fori_loop(0, n_pages, lambda p, c: (page_copy(bb, slot, p).start(), c)[1], 0)

    def wait(bb, slot):
        lax.fori_loop(0, n_pages, lambda p, c: (page_copy(bb, slot, p).wait(), c)[1], 0)

    slot = b % 2

    @pl.when(b == 0)
    def _():
        start(0, 0)

    @pl.when(b + 1 < nb)
    def _():
        start(b + 1, 1 - slot)

    wait(b, slot)

    r16 = lambda a: a.astype(BF16).astype(F32)
    qi = iq_ref[0].astype(BF16)
    wcol = r16(iw_ref[0]) * (IDX_DIM ** -0.5 * IDX_HEADS ** -0.5)
    s = r16(jnp.maximum(jnp.dot(qi, buf[slot].astype(BF16), preferred_element_type=F32), 0.0))
    o_ref[0, :, 0:past] = jnp.sum(s * wcol, axis=0, keepdims=True)
    sn = r16(jnp.maximum(jnp.dot(qi, iknew_ref[0].astype(BF16), preferred_element_type=F32), 0.0))
    sn = jnp.sum(sn * wcol, axis=0, keepdims=True)
    lane = lax.broadcasted_iota(jnp.int32, (1, LANES), 1)
    o_ref[0, :, past:past + LANES] = jnp.where(lane == 0, sn, -jnp.inf)


def dsa_scores(page_table, iq, iw, iknew_pad, cache_kidx):
    b, n_pages = page_table.shape
    past = n_pages * PAGE_SIZE
    kern = functools.partial(_dsa_scores_kernel, n_pages=n_pages)
    gs = pltpu.PrefetchScalarGridSpec(
        num_scalar_prefetch=1,
        grid=(b,),
        in_specs=[pl.BlockSpec((1, IDX_HEADS, IDX_DIM), lambda i, pt: (i, 0, 0)),
                  pl.BlockSpec((1, IDX_HEADS, 1), lambda i, pt: (i, 0, 0)),
                  pl.BlockSpec((1, IDX_DIM, LANES), lambda i, pt: (i, 0, 0)),
                  pl.BlockSpec(memory_space=pl.ANY)],
        out_specs=pl.BlockSpec((1, 1, past + LANES), lambda i, pt: (i, 0, 0)),
        scratch_shapes=[pltpu.VMEM((2, IDX_DIM, past), F32), pltpu.SemaphoreType.DMA((2,))],
    )
    return pl.pallas_call(
        kern, grid_spec=gs,
        out_shape=jax.ShapeDtypeStruct((b, 1, past + LANES), F32),
        compiler_params=_cparams("arbitrary"),
        name="dsa_scores",
    )(page_table, iq, iw, iknew_pad, cache_kidx)


def _dsa_threshold_kernel(s_ref, thr_ref, *, topk):
    s = s_ref[...]
    nb = s.shape[0]
    finite = s > -jnp.inf
    lo0 = jnp.min(jnp.where(finite, s, jnp.inf), axis=1, keepdims=True)
    hi0 = jnp.max(s, axis=1, keepdims=True)
    cnt0 = jnp.sum(jnp.where(finite, 1.0, 0.0), axis=1, keepdims=True)

    def count_ge(thr):
        return jnp.sum(jnp.where(s_ref[...] >= thr, 1.0, 0.0), axis=1, keepdims=True)

    thr = _bisect_threshold(count_ge, lo0, hi0, cnt0, topk)
    thr_ref[...] = jnp.broadcast_to(thr, (nb, LANES))


def dsa_threshold(scores, topk):
    b, l = scores.shape
    return pl.pallas_call(
        functools.partial(_dsa_threshold_kernel, topk=topk),
        grid=(1,),
        in_specs=[pl.BlockSpec((b, l), lambda i: (0, 0))],
        out_specs=pl.BlockSpec((b, LANES), lambda i: (0, 0)),
        out_shape=jax.ShapeDtypeStruct((b, LANES), F32),
        compiler_params=_cparams("arbitrary"),
        name="dsa_threshold",
    )(scores)


def _dsa_decode_kernel(pt_ref, q_ref, s_ref, thr_ref, knew_ref, vnew_ref, bias_ref, k_hbm, v_hbm, o_ref,
                       kbuf, vbuf, sem, *, n_pages):
    b = pl.program_id(0)
    nb = pl.num_programs(0)
    past = n_pages * PAGE_SIZE

    rpp = PAGE_SIZE * B_KV_HEADS

    def copies(bb, slot):
        ck = _page_copies(pt_ref, bb, n_pages, k_hbm, kbuf.at[slot], sem.at[0, slot], rpp)
        cv = _page_copies(pt_ref, bb, n_pages, v_hbm, vbuf.at[slot], sem.at[1, slot], rpp)
        return ck, cv

    def start(bb, slot):
        ck, cv = copies(bb, slot)
        lax.fori_loop(0, n_pages, lambda p, c: (ck(p).start(), cv(p).start(), c)[2], 0)

    def wait(bb, slot):
        ck, cv = copies(bb, slot)
        lax.fori_loop(0, n_pages, lambda p, c: (ck(p).wait(), cv(p).wait(), c)[2], 0)

    slot = b % 2

    @pl.when(b == 0)
    def _():
        kbuf[:, n_pages * rpp:, :] = jnp.zeros((2, rpp, B_HEAD_DIM), F32)
        vbuf[:, n_pages * rpp:, :] = jnp.zeros((2, rpp, B_HEAD_DIM), F32)
        start(0, 0)

    @pl.when(b + 1 < nb)
    def _():
        start(b + 1, 1 - slot)

    kbuf[slot, n_pages * rpp:n_pages * rpp + 8, :] = knew_ref[0]
    vbuf[slot, n_pages * rpp:n_pages * rpp + 8, :] = vnew_ref[0]
    wait(b, slot)

    sel = s_ref[0] >= thr_ref[0][:, 0:1]
    n_keys = past + PAGE_SIZE
    outs = []
    for n in range(B_KV_HEADS):
        kn = kbuf[slot, pl.ds(n, n_keys, stride=B_KV_HEADS), :].astype(BF16)
        vn = vbuf[slot, pl.ds(n, n_keys, stride=B_KV_HEADS), :].astype(BF16)
        qn = q_ref[0, n].astype(BF16)
        lg = _nt(qn, kn) * B_HEAD_DIM ** -0.5 + bias_ref[n]
        m = jnp.max(jnp.where(sel, lg, NEG), axis=1, keepdims=True)
        p = jnp.where(sel, jnp.exp(lg - m), 0.0)
        p = p / jnp.sum(p, axis=1, keepdims=True)
        outs.append(jnp.dot(p.astype(BF16), vn, preferred_element_type=F32))
    o_ref[0] = jnp.concatenate(outs, axis=0).astype(o_ref.dtype)


def dsa_decode(page_table, q8, scores, thr, knew8, vnew8, bias_rows, cache_k2, cache_v2):
    b, n_pages = page_table.shape
    past = n_pages * PAGE_SIZE
    l = past + LANES
    wkv = B_KV_HEADS * B_HEAD_DIM
    kern = functools.partial(_dsa_decode_kernel, n_pages=n_pages)
    gs = pltpu.PrefetchScalarGridSpec(
        num_scalar_prefetch=1,
        grid=(b,),
        in_specs=[pl.BlockSpec((1, B_KV_HEADS, 8, B_HEAD_DIM), lambda i, pt: (i, 0, 0, 0)),
                  pl.BlockSpec((1, 1, l), lambda i, pt: (i, 0, 0)),
                  pl.BlockSpec((1, 1, LANES), lambda i, pt: (i, 0, 0)),
                  pl.BlockSpec((1, 8, B_HEAD_DIM), lambda i, pt: (i, 0, 0)),
                  pl.BlockSpec((1, 8, B_HEAD_DIM), lambda i, pt: (i, 0, 0)),
                  pl.BlockSpec((B_KV_HEADS, 8, l), lambda i, pt: (0, 0, 0)),
                  pl.BlockSpec(memory_space=pl.ANY),
                  pl.BlockSpec(memory_space=pl.ANY)],
        out_specs=pl.BlockSpec((1, 2 * 8, B_HEAD_DIM), lambda i, pt: (i, 0, 0)),
        scratch_shapes=[pltpu.VMEM((2, l * B_KV_HEADS, B_HEAD_DIM), F32),
                        pltpu.VMEM((2, l * B_KV_HEADS, B_HEAD_DIM), F32),
                        pltpu.SemaphoreType.DMA((2, 2))],
    )
    return pl.pallas_call(
        kern, grid_spec=gs,
        out_shape=jax.ShapeDtypeStruct((b, 2 * 8, B_HEAD_DIM), F32),
        compiler_params=_cparams("arbitrary"),
        name="dsa_decode",
    )(page_table, q8, scores, thr, knew8, vnew8, bias_rows, cache_k2, cache_v2)


def _merge_kernel(ya_ref, yb_ref, ga_ref, gb_ref, wa_ref, wb_ref, o_ref):
    a = _mm(ya_ref[...], wa_ref[...])
    bb = _mm(yb_ref[...], wb_ref[...])
    o_ref[...] = (jax.nn.sigmoid(ga_ref[...]) * a + jax.nn.sigmoid(gb_ref[...]) * bb).astype(o_ref.dtype)


def merge(ya, yb, z, ga_col, gb_col, wa, wb, *, tm, tn):
    m, k = ya.shape
    n = wa.shape[1]
    return pl.pallas_call(
        _merge_kernel,
        grid=(m // tm, n // tn),
        in_specs=[pl.BlockSpec((tm, k), lambda i, j: (i, 0)),
                  pl.BlockSpec((tm, k), lambda i, j: (i, 0)),
                  pl.BlockSpec((tm, tn), lambda i, j: (i, ga_col // tn + j)),
                  pl.BlockSpec((tm, tn), lambda i, j: (i, gb_col // tn + j)),
                  pl.BlockSpec((k, tn), lambda i, j: (0, j)),
                  pl.BlockSpec((k, tn), lambda i, j: (0, j))],
        out_specs=pl.BlockSpec((tm, tn), lambda i, j: (i, j)),
        out_shape=jax.ShapeDtypeStruct((m, n), wa.dtype),
        compiler_params=_cparams("parallel", "arbitrary"),
        name="merge",
    )(ya, yb, z, z, wa, wb)


def _cross_prompt_kernel(x_ref, g_ref, wq_ref, mk_ref, mv_ref, wo_ref, o_ref, *, n_in_tiles):
    x = x_ref[...]
    ms = jnp.mean(x * x, axis=-1, keepdims=True)
    h = (x * lax.rsqrt(ms + EPS) * g_ref[...]).astype(BF16)
    q = jnp.dot(h, wq_ref[...], preferred_element_type=F32)
    outs = []
    for hh in range(X_HEADS):
        cols = slice(hh * X_HEAD_DIM, (hh + 1) * X_HEAD_DIM)
        lg = _nt(q[:, cols].astype(BF16), mk_ref[:, cols]) * X_HEAD_DIM ** -0.5
        mx = jnp.max(lg, axis=1, keepdims=True)
        p = jnp.exp(lg - mx)
        p = p / jnp.sum(p, axis=1, keepdims=True)
        outs.append(jnp.dot(p.astype(BF16), mv_ref[:, cols], preferred_element_type=F32).astype(BF16))
    att = jnp.concatenate(outs, axis=1)
    res = x + jnp.dot(att, wo_ref[...], preferred_element_type=F32)
    o_ref[...] = jnp.where(pl.program_id(0) < n_in_tiles, res, 0.0)


def cross_prompt(x, g, wq, mk, mv, wo, *, tm, out_rows=None):
    m, d = x.shape
    out_rows = m if out_rows is None else out_rows
    n_in = m // tm
    full = lambda a: pl.BlockSpec(a.shape, lambda i: (0,) * a.ndim)
    g2 = g.reshape(1, d)
    return pl.pallas_call(
        functools.partial(_cross_prompt_kernel, n_in_tiles=n_in),
        grid=(pl.cdiv(out_rows, tm),),
        in_specs=[pl.BlockSpec((tm, d), lambda i: (jnp.minimum(i, n_in - 1), 0)),
                  full(g2), full(wq), full(mk), full(mv), full(wo)],
        out_specs=pl.BlockSpec((tm, d), lambda i: (i, 0)),
        out_shape=jax.ShapeDtypeStruct((out_rows, d), F32),
        compiler_params=_cparams("parallel"),
        name="cross_prompt",
    )(x, g2, wq, mk, mv, wo)


def _cross_step_kernel(q_ref, mk_ref, mv_ref, o_ref):
    r16 = lambda a: a.astype(BF16).astype(F32)
    q = r16(q_ref[0])
    outs = []
    for hh in range(X_HEADS):
        cols = slice(hh * X_HEAD_DIM, (hh + 1) * X_HEAD_DIM)
        kh = r16(mk_ref[0, :, cols])
        vh = r16(mv_ref[0, :, cols])
        lg = jnp.sum(kh * q[:, cols], axis=1, keepdims=True) * X_HEAD_DIM ** -0.5
        mx = jnp.max(lg, axis=0, keepdims=True)
        p = jnp.exp(lg - mx)
        p = r16(p / jnp.sum(p, axis=0, keepdims=True))
        outs.append(jnp.sum(p * vh, axis=0, keepdims=True))
    o_ref[0] = jnp.concatenate(outs, axis=1).astype(o_ref.dtype)


def cross_step(q, mk, mv):
    b, w = q.shape
    mem = mk.shape[1]
    return pl.pallas_call(
        _cross_step_kernel,
        grid=(b,),
        in_specs=[pl.BlockSpec((1, 1, w), lambda i: (i, 0, 0)),
                  pl.BlockSpec((1, mem, w), lambda i: (i, 0, 0)),
                  pl.BlockSpec((1, mem, w), lambda i: (i, 0, 0))],
        out_specs=pl.BlockSpec((1, 1, w), lambda i: (i, 0, 0)),
        out_shape=jax.ShapeDtypeStruct((b, 1, w), F32),
        compiler_params=_cparams("arbitrary"),
        name="cross_step",
    )(q.reshape(b, 1, w), mk, mv)


def _pack_bf16_pairs(x):
    c = x.shape[1] // 2
    u = pltpu.bitcast(x.astype(BF16).astype(F32), jnp.uint32)
    return lax.shift_right_logical(u[:, :c], jnp.uint32(16)) | u[:, c:]


def _unpack_bf16_pairs(u):
    lo = pltpu.bitcast(lax.shift_left(u, jnp.uint32(16)), F32)
    hi = pltpu.bitcast(u & jnp.uint32(0xFFFF0000), F32)
    return jnp.concatenate([lo, hi], axis=1)


def _router_kernel(x_ref, g_ref, w_ref, b_ref, hf_ref, route_ref):
    x = x_ref[...]
    ms = jnp.mean(x * x, axis=-1, keepdims=True)
    hf = x * lax.rsqrt(ms + EPS) * g_ref[...]
    hf_ref[...] = _pack_bf16_pairs(hf)
    lg = _mm(hf, w_ref[...]) + b_ref[...]
    tm = lg.shape[0]
    lane = lax.broadcasted_iota(jnp.int32, (tm, LANES), 1)
    big = jnp.int32(LANES)
    is_g = lane < N_GROUPS
    gmax = jnp.max(jnp.where(is_g, lg, -jnp.inf), axis=1, keepdims=True)
    grp = jnp.min(jnp.where(is_g & (lg == gmax), lane, big), axis=1, keepdims=True)
    p_grp = 1.0 / jnp.sum(jnp.where(is_g, jnp.exp(lg - gmax), 0.0), axis=1, keepdims=True)
    e_lo = N_GROUPS + grp * EXP_PER_GROUP
    in_g = (lane >= e_lo) & (lane < e_lo + EXP_PER_GROUP)
    v1 = jnp.max(jnp.where(in_g, lg, -jnp.inf), axis=1, keepdims=True)
    i1 = jnp.min(jnp.where(in_g & (lg == v1), lane, big), axis=1, keepdims=True)
    rest = in_g & (lane != i1)
    v2 = jnp.max(jnp.where(rest, lg, -jnp.inf), axis=1, keepdims=True)
    i2 = jnp.min(jnp.where(rest & (lg == v2), lane, big), axis=1, keepdims=True)
    e2 = jnp.exp(v2 - v1)
    g1 = p_grp / (1.0 + e2)
    g2 = p_grp * e2 / (1.0 + e2)
    r = jnp.where(lane == 0, (i1 - N_GROUPS).astype(F32),
                  jnp.where(lane == 1, (i2 - N_GROUPS).astype(F32),
                            jnp.where(lane == 2, g1, jnp.where(lane == 3, g2, 0.0))))
    route_ref[...] = r


def router(x, g, w_pad, b_pad, *, tm):
    m, d = x.shape
    return pl.pallas_call(
        _router_kernel,
        grid=(pl.cdiv(m, tm),),
        in_specs=[pl.BlockSpec((tm, d), lambda i: (i, 0)),
                  pl.BlockSpec((1, d), lambda i: (0, 0)),
                  pl.BlockSpec((d, LANES), lambda i: (0, 0)),
                  pl.BlockSpec((1, LANES), lambda i: (0, 0))],
        out_specs=[pl.BlockSpec((tm, d // 2), lambda i: (i, 0)),
                   pl.BlockSpec((tm, LANES), lambda i: (i, 0))],
        out_shape=[jax.ShapeDtypeStruct((m, d // 2), jnp.uint32), jax.ShapeDtypeStruct((m, LANES), F32)],
        compiler_params=_cparams("parallel"),
        name="router",
    )(x, g.reshape(1, d), w_pad, b_pad)


def _moe_kernel(be_ref, nxt_ref, nused_ref, sbase_ref, nvalid_ref, padj_ref, order_ref, hf_hbm, wg_hbm, wu_hbm,
                wd_hbm, y_hbm, xbuf, obuf, wg_f, wu_f, wd_f, wg_s, wu_s, wd_s, sem_in, sem_out, sem_w,
                *, n_tokens):
    b = pl.program_id(0)
    nb = pl.num_programs(0)
    blk = xbuf.shape[1]
    xslot = b % 3
    used = b < nused_ref[0]
    next_used = b + 2 < nused_ref[0]
    n_assign = order_ref.shape[0]

    def slot_info(bb):
        base, nv, pad0 = sbase_ref[bb], nvalid_ref[bb], padj_ref[bb]

        def info(r):
            valid = r < nv
            asg = order_ref[jnp.minimum(base + r, n_assign - 1)]
            tok = lax.shift_right_logical(asg, 1)
            return jnp.where(valid, tok, 0), jnp.where(valid, (asg & 1) * n_tokens + tok, pad0 + r)
        return info

    def weight_copies(e):
        return (pltpu.make_async_copy(wg_hbm.at[e], wg_f, sem_w.at[0]),
                pltpu.make_async_copy(wu_hbm.at[e], wu_f, sem_w.at[1]),
                pltpu.make_async_copy(wd_hbm.at[e], wd_f, sem_w.at[2]))

    def gather_start(bb, sl):
        info = slot_info(bb)
        for r in range(blk):
            pltpu.make_async_copy(hf_hbm.at[pl.ds(info(r)[0], 1)], xbuf.at[sl, pl.ds(r, 1)],
                                  sem_in.at[sl]).start()

    def scatter_start(bb, sl):
        info = slot_info(bb)
        for r in range(blk):
            pltpu.make_async_copy(obuf.at[sl, pl.ds(r, 1)], y_hbm.at[pl.ds(info(r)[1], 1)],
                                  sem_out.at[sl]).start(priority=r % 2)

    def gather_wait(sl):
        pltpu.make_async_copy(hf_hbm.at[pl.ds(0, blk)], xbuf.at[sl], sem_in.at[sl]).wait()

    def scatter_wait(sl):
        pltpu.make_async_copy(obuf.at[sl], y_hbm.at[pl.ds(0, blk)], sem_out.at[sl]).wait()

    @pl.when(b == 0)
    def _():
        for cp in weight_copies(be_ref[0]):
            cp.start(priority=1)
        gather_start(0, 0)
        obuf[...] = jnp.zeros(obuf.shape, obuf.dtype)

    @pl.when(jnp.logical_and(b == 0, 1 < nused_ref[0]))
    def _():
        gather_start(1, 1)

    changed = jnp.logical_and(used, jnp.logical_or(b == 0, be_ref[b] != be_ref[jnp.maximum(b - 1, 0)]))

    @pl.when(changed)
    def _():
        for cp in weight_copies(be_ref[b]):
            cp.wait()
        wg_s[...] = wg_f[...].astype(BF16)
        wu_s[...] = wu_f[...].astype(BF16)
        wd_s[...] = wd_f[...].astype(BF16)

    @pl.when(jnp.logical_and(changed, nxt_ref[b] >= 0))
    def _():
        for cp in weight_copies(nxt_ref[b]):
            cp.start(priority=1)

    @pl.when(used)
    def _():
        gather_wait(xslot)

    @pl.when(b >= 3)
    def _():
        scatter_wait(xslot)

    def step(prefetch, flush_prev, compute):
        if prefetch:
            gather_start(b + 2, (b + 2) % 3)
        if flush_prev:
            scatter_start(b - 1, (b + 2) % 3)
        if compute:
            x = _unpack_bf16_pairs(xbuf[xslot]).astype(BF16)
            gg = jnp.dot(x, wg_s[...], preferred_element_type=F32)
            uu = jnp.dot(x, wu_s[...], preferred_element_type=F32)
            a = (_silu(gg) * uu).astype(BF16)
            obuf[xslot] = _pack_bf16_pairs(jnp.dot(a, wd_s[...], preferred_element_type=F32))

    first, last = b == 0, b == nb - 1
    land, lnot = jnp.logical_and, jnp.logical_not
    pl.when(land(first, next_used))(lambda: step(True, False, True))
    pl.when(land(first, lnot(next_used)))(lambda: step(False, False, True))
    pl.when(land(lnot(first), land(used, next_used)))(lambda: step(True, True, True))
    pl.when(land(lnot(first), land(used, lnot(next_used))))(lambda: step(False, True, True))
    pl.when(land(lnot(first), lnot(used)))(lambda: step(False, True, False))

    @pl.when(last)
    def _():
        scatter_start(b, xslot)
        scatter_wait(xslot)

    @pl.when(jnp.logical_and(last, b >= 1))
    def _():
        scatter_wait((b + 2) % 3)

    @pl.when(jnp.logical_and(last, b >= 2))
    def _():
        scatter_wait((b + 1) % 3)


def moe_experts(tables, hf, w_g, w_u, w_d, *, blk, out_rows):
    n_blocks = tables[0].shape[0]
    dp = hf.shape[1]
    d = 2 * dp
    ff = w_g.shape[2]
    gs = pltpu.PrefetchScalarGridSpec(
        num_scalar_prefetch=len(tables),
        grid=(n_blocks,),
        in_specs=[pl.BlockSpec(memory_space=pl.ANY)] * 4,
        out_specs=pl.BlockSpec(memory_space=pl.ANY),
        scratch_shapes=[pltpu.VMEM((3, blk, dp), jnp.uint32), pltpu.VMEM((3, blk, dp), jnp.uint32),
                        pltpu.VMEM((d, ff), F32), pltpu.VMEM((d, ff), F32), pltpu.VMEM((ff, d), F32),
                        pltpu.VMEM((d, ff), BF16), pltpu.VMEM((d, ff), BF16), pltpu.VMEM((ff, d), BF16),
                        pltpu.SemaphoreType.DMA((3,)), pltpu.SemaphoreType.DMA((3,)),
                        pltpu.SemaphoreType.DMA((3,))],
    )
    return pl.pallas_call(
        functools.partial(_moe_kernel, n_tokens=hf.shape[0]), grid_spec=gs,
        out_shape=jax.ShapeDtypeStruct((out_rows, dp), jnp.uint32),
        compiler_params=_cparams("arbitrary"),
        name="moe_experts",
    )(*tables, hf, w_g, w_u, w_d)


def _combine_kernel(x_ref, route_ref, gf_ref, y1_ref, y2_ref, op_ref, os_ref):
    i = pl.program_id(0)
    route = route_ref[...]
    x = (x_ref[...] + route[:, 2:3] * _unpack_bf16_pairs(y1_ref[...])
         + route[:, 3:4] * _unpack_bf16_pairs(y2_ref[...]))
    ms = jnp.mean(x * x, axis=-1, keepdims=True)
    out = x * lax.rsqrt(ms + EPS) * gf_ref[...]

    @pl.when(i < pl.num_programs(0) - 1)
    def _():
        op_ref[...] = out

    @pl.when(i == pl.num_programs(0) - 1)
    def _():
        os_ref[...] = out[:os_ref.shape[0]]


def combine(x, route, gf, y, plane, n_prompt, *, tm):
    m, d = x.shape
    n_tiles = n_prompt // tm
    assert n_prompt % tm == 0 and 0 < m - n_prompt <= tm
    return pl.pallas_call(
        _combine_kernel,
        grid=(n_tiles + 1,),
        in_specs=[pl.BlockSpec((tm, d), lambda i: (i, 0)),
                  pl.BlockSpec((tm, LANES), lambda i: (i, 0)),
                  pl.BlockSpec((1, d), lambda i: (0, 0)),
                  pl.BlockSpec((tm, d // 2), lambda i: (i, 0)),
                  pl.BlockSpec((tm, d // 2), lambda i: (plane // tm + i, 0))],
        out_specs=[pl.BlockSpec((tm, d), lambda i: (jnp.minimum(i, n_tiles - 1), 0)),
                   pl.BlockSpec((m - n_prompt, d), lambda i: (0, 0))],
        out_shape=[jax.ShapeDtypeStruct((n_prompt, d), F32), jax.ShapeDtypeStruct((m - n_prompt, d), F32)],
        compiler_params=_cparams("arbitrary"),
        name="combine",
    )(x, route, gf.reshape(1, d), y, y)


def _t5_bucket(dist):
    dist = jnp.asarray(dist, jnp.int32)
    max_exact = REL_BUCKETS // 2
    dist_f = jnp.maximum(dist, 1).astype(F32)
    large = max_exact + (jnp.log(dist_f / max_exact) / math.log(REL_MAX_DIST / max_exact)
                         * (REL_BUCKETS - max_exact)).astype(jnp.int32)
    large = jnp.minimum(large, REL_BUCKETS - 1)
    return jnp.where(dist < max_exact, dist, large)


def _bias_tables(rel_bias, past):
    r = np.arange(LANES)
    diff = r[:, None] - r[None, :]
    buckets = jnp.stack([_t5_bucket(np.maximum(diff, 0)),
                         _t5_bucket(np.maximum(diff + LANES, 0)),
                         _t5_bucket(np.full((LANES, LANES), 2 * LANES))])
    def lookup(bkt):
        oh = (bkt.reshape(-1, 1) == jnp.arange(REL_BUCKETS)[None, :]).astype(F32)
        out = jnp.dot(oh, rel_bias.astype(F32), precision=lax.Precision.HIGHEST)
        return out.T.reshape((rel_bias.shape[1],) + bkt.shape)

    tiles = lookup(buckets)
    tiles = tiles - tiles[:, 2:3]
    dist = np.maximum(past - np.arange(past + LANES), 0)
    rows = lookup(_t5_bucket(dist))
    rows = rows.reshape(B_KV_HEADS, B_GROUP, past + LANES)
    rows = jnp.concatenate([rows, jnp.zeros_like(rows)], axis=1)
    return tiles, rows


def _dispatch(eid, n_tokens, blk):
    a = eid.shape[0]
    assert EXPERT_TOPK == 2 and a == EXPERT_TOPK * n_tokens
    n_blocks = -(-(a + N_EXPERTS * (blk - 1)) // blk)
    rows = n_blocks * blk
    order = jnp.argsort(eid).astype(jnp.int32)
    counts = jnp.sum(eid[:, None] == jnp.arange(N_EXPERTS)[None, :], axis=0).astype(jnp.int32)
    cum = jnp.cumsum(counts)
    starts = cum - counts
    padded = (counts + blk - 1) // blk * blk
    pad_end = jnp.cumsum(padded)
    pad_start = pad_end - padded
    blocks = jnp.arange(n_blocks, dtype=jnp.int32)
    block_e = jnp.minimum(jnp.sum(pad_end[None, :] <= (blocks * blk)[:, None], axis=1), N_EXPERTS - 1)
    n_used = pad_end[-1] // blk
    off = blocks * blk - pad_start[block_e]
    sbase = starts[block_e] + off
    nvalid = jnp.where(blocks < n_used, jnp.clip(counts[block_e] - off, 0, blk), 0)
    padj = a + blocks * blk - cum[block_e]
    new_run = jnp.concatenate([jnp.array([True]), block_e[1:] != block_e[:-1]])
    run_start = jnp.where(new_run & (blocks < n_used), blocks, n_blocks)
    nxt = jnp.concatenate([jnp.flip(lax.cummin(jnp.flip(run_start)))[1:], jnp.array([n_blocks])])
    next_e = jnp.where(nxt < n_blocks, block_e[jnp.minimum(nxt, n_blocks - 1)], -1)
    i32 = lambda v: v.astype(jnp.int32)
    return (i32(block_e), i32(next_e), i32(n_used).reshape(1), i32(sbase), i32(nvalid), i32(padj), order), rows


def kernel(x_prompt, x_sample, mem_prompt, cache_k, cache_v, cache_kidx, page_table, state_hgrn, cache_mem_k,
           cache_mem_v, norm_mix, w_in, hgrn_lb_logits, hgrn_norm, w_branch_a, w_branch_b, w_out, norm_cross, w_xq,
           w_xk, w_xv, w_xo, norm_ffn, w_router_group, b_router_group, w_router_expert, b_router_expert, w_exp_gate,
           w_exp_up, w_exp_down, rel_bias, norm_final):
    assert w_in.shape[0] == 1, "single-layer step"
    l = 0
    drop0 = lambda a: a.reshape(a.shape[1:])
    bp, t, d = x_prompt.shape
    db = x_sample.shape[0]
    past = page_table.shape[1] * PAGE_SIZE
    xp = x_prompt.reshape(bp * t, d)
    xs = x_sample.reshape(db, d)

    wi_t = jnp.transpose(drop0(w_in)).astype(BF16)
    assert wi_t.shape[0] - NZ_MAIN + T_PAD == NZ_TAIL
    w_tail_t = jnp.pad(wi_t[NZ_MAIN:], ((T_PAD, 0), (0, 0)))
    wa, wb, wo = w_branch_a[l].astype(BF16), w_branch_b[l].astype(BF16), w_out[l].astype(BF16)
    wxq, wxk, wxv, wxo = (w_xq[l].astype(BF16), w_xk[l].astype(BF16), w_xv[l].astype(BF16), w_xo[l].astype(BF16))
    w_route = jnp.pad(jnp.concatenate([w_router_group[l], w_router_expert[l]], axis=1),
                      ((0, 0), (0, LANES - N_GROUPS - N_EXPERTS))).astype(BF16)
    b_route = jnp.pad(jnp.concatenate([b_router_group[l], b_router_expert[l]]),
                      (0, LANES - N_GROUPS - N_EXPERTS)).reshape(1, LANES)
    bias_tiles, bias_rows = _bias_tables(rel_bias, past)

    zp, zpt = in_proj(xp, norm_mix[l], wi_t, NZ_MAIN, w_tail_t, tm=1024, tn=512)
    kp = zp[:, C_BK:C_BK + 256]
    vp = zp[:, C_BV:C_BV + 256]
    ikp = zpt[:, T_SM + IK_LANE:T_SM + IK_LANE + IDX_DIM]
    ya_p, st_p = hgrn_prompt(zp, hgrn_lb_logits, hgrn_norm[l])
    yb_p = dsa_prompt(zp, zpt, ikp.astype(BF16), kp.astype(BF16), vp.astype(BF16), bias_tiles)
    mg_p = merge(ya_p, yb_p, zpt, T_GA, T_GB, wa, wb, tm=1024, tn=512)
    x1p = matmul(mg_p, wo, xp, tm=1024, tn=512)
    memp = mem_prompt.reshape(-1, d)
    mk = matmul(memp, wxk, tm=memp.shape[0], tn=512)
    mv = matmul(memp, wxv, tm=memp.shape[0], tn=512)
    tmc = 128
    n = -(-(bp * t + db) // tmc) * tmc
    x2p = cross_prompt(x1p, norm_cross[l], wxq, mk.astype(BF16), mv.astype(BF16), wxo, tm=512, out_rows=n)

    zs, zst = in_proj(xs, norm_mix[l], wi_t, NZ_MAIN, w_tail_t, tm=db, tn=512)
    ks = zs[:, C_BK:C_BK + 256]
    vs = zs[:, C_BV:C_BV + 256]
    iks = zst[:, T_SM + IK_LANE:T_SM + IK_LANE + IDX_DIM]
    ya_s, st_s = hgrn_step(zs[:, :4 * A_HEADS * A_DK], hgrn_lb_logits, hgrn_norm[l], drop0(state_hgrn))
    iq_s = zs[:, C_IQ:C_IQ + IDX_HEADS * IDX_DIM].reshape(db, IDX_HEADS, IDX_DIM)
    iw_s = zst[:, T_SM + IW_LANE:T_SM + IW_LANE + IDX_HEADS].reshape(db, IDX_HEADS, 1)
    iknew_pad = jnp.pad(iks[:, :, None], ((0, 0), (0, 0), (0, LANES - 1)))
    scores = dsa_scores(page_table, iq_s, iw_s, iknew_pad,
                        jnp.swapaxes(drop0(cache_kidx), 1, 2)).reshape(db, past + LANES)
    topk_s = min(TOPK_MAX, (past + 1) // 4)
    thr = dsa_threshold(scores, topk_s)
    q8 = jnp.pad(zs[:, C_BQ:C_BQ + B_HEADS * B_HEAD_DIM].reshape(db, B_KV_HEADS, B_GROUP, B_HEAD_DIM),
                 ((0, 0), (0, 0), (0, 8 - B_GROUP), (0, 0)))
    knew8 = jnp.pad(ks.reshape(db, B_KV_HEADS, B_HEAD_DIM), ((0, 0), (0, 8 - B_KV_HEADS), (0, 0)))
    vnew8 = jnp.pad(vs.reshape(db, B_KV_HEADS, B_HEAD_DIM), ((0, 0), (0, 8 - B_KV_HEADS), (0, 0)))
    n_pool = cache_k.shape[1]
    ob = dsa_decode(page_table, q8, scores.reshape(db, 1, -1), thr.reshape(db, 1, LANES), knew8, vnew8, bias_rows,
                    cache_k.reshape(n_pool, PAGE_SIZE * B_KV_HEADS, B_HEAD_DIM),
                    cache_v.reshape(n_pool, PAGE_SIZE * B_KV_HEADS, B_HEAD_DIM))
    yb_s = ob.reshape(db, B_KV_HEADS, 8, B_HEAD_DIM)[:, :, :B_GROUP].reshape(db, B_HEADS * B_HEAD_DIM)
    mg_s = merge(ya_s.reshape(db, -1), yb_s, zst, T_GA, T_GB, wa, wb, tm=db, tn=512)
    x1s = matmul(mg_s, wo, xs, tm=db, tn=512)
    qx_s = norm_matmul(x1s, norm_cross[l], wxq, tm=db, tn=512)
    mem = cache_mem_k.shape[2]
    att_s = cross_step(qx_s, cache_mem_k.reshape(db, mem, -1), cache_mem_v.reshape(db, mem, -1))
    x2 = matmul(att_s.reshape(db, -1), wxo, x1s, tm=db, tn=512, into=(x2p, bp * t))

    hf, route = router(x2, norm_ffn[l], w_route, b_route, tm=256)
    eid = route[:, :EXPERT_TOPK].astype(jnp.int32).reshape(-1)
    tables, rows = _dispatch(eid, n, MOE_ROWS)
    ye = moe_experts(tables, hf, drop0(w_exp_gate), drop0(w_exp_up), drop0(w_exp_down), blk=MOE_ROWS, out_rows=rows)
    y_p, y_s = combine(x2, route, norm_final, ye, n, bp * t, tm=tmc)

    y_prompt = y_p.reshape(bp, t, d)
    y_sample = y_s[:db].reshape(db, 1, d)
    return (y_prompt, y_sample,
            kp.reshape(1, bp, t, B_KV_HEADS, B_HEAD_DIM), vp.reshape(1, bp, t, B_KV_HEADS, B_HEAD_DIM),
            ikp.reshape(1, bp, t, IDX_DIM),
            jnp.swapaxes(st_p, 1, 2).reshape(1, bp, A_HEADS, A_DK, A_DV),
            mk.reshape(1, bp, -1, X_HEADS, X_HEAD_DIM), mv.reshape(1, bp, -1, X_HEADS, X_HEAD_DIM),
            ks.reshape(1, db, 1, B_KV_HEADS, B_HEAD_DIM), vs.reshape(1, db, 1, B_KV_HEADS, B_HEAD_DIM),
            iks.reshape(1, db, 1, IDX_DIM),
            st_s.reshape(1, db, A_HEADS, A_DK, A_DV))
```

```python
import functools
import math

import jax
import jax.numpy as jnp
import numpy as np
from jax import lax
from jax.experimental import pallas as pl
from jax.experimental.pallas import tpu as pltpu

F32 = jnp.float32
BF16 = jnp.bfloat16
EPS = 1e-6

D_MODEL = 2048
A_HEADS, A_DK, A_DV = 8, 128, 128
B_HEADS, B_KV_HEADS, B_HEAD_DIM = 8, 2, 128
B_GROUP = B_HEADS // B_KV_HEADS
IDX_HEADS, IDX_DIM = 16, 64
TOPK_MAX = 256
PAGE_SIZE = 128
REL_BUCKETS, REL_MAX_DIST = 32, 128
X_HEADS, X_HEAD_DIM = 4, 128
N_GROUPS, EXP_PER_GROUP = 4, 8
N_EXPERTS = N_GROUPS * EXP_PER_GROUP
EXPERT_TOPK = 2
EXPERT_FF = 512
MOE_ROWS = 256

LANES = 128
VMEM_LIMIT = 56 * 1024 * 1024

NEG = -1e30

C_AQ, C_AF, C_AI, C_AG, C_BQ, C_BK, C_BV, C_IQ = 0, 1024, 2048, 3072, 4096, 5120, 5376, 5632
NZ_MAIN = 6656
T_PAD = 432
T_SM, IW_LANE, IK_LANE = 384, 48, 64
T_GA, T_GB = 512, 2560
NZ_TAIL = 4608


def _cparams(*sem):
    return pltpu.CompilerParams(dimension_semantics=sem, vmem_limit_bytes=VMEM_LIMIT)


def _silu(x):
    return x * jax.nn.sigmoid(x)


def _nt(a, b):
    return lax.dot_general(a, b, (((1,), (1,)), ((), ())), preferred_element_type=F32)


def _nt_f32(a, b):
    return lax.dot_general(a, b, (((1,), (1,)), ((), ())), preferred_element_type=F32,
                           precision=lax.Precision.HIGHEST)


def _mm(a, w):
    return jnp.dot(a.astype(BF16), w.astype(BF16), preferred_element_type=F32)


def _norm_matmul_kernel(x_ref, g_ref, w_ref, o_ref, h_ref):
    @pl.when(pl.program_id(1) == 0)
    def _():
        x = x_ref[...]
        ms = jnp.mean(x * x, axis=-1, keepdims=True)
        h_ref[...] = (x * lax.rsqrt(ms + EPS) * g_ref[...]).astype(h_ref.dtype)

    o_ref[...] = _mm(h_ref[...], w_ref[...])


def norm_matmul(x, g, w, *, tm, tn):
    m, k = x.shape
    n = w.shape[1]
    assert n % tn == 0
    return pl.pallas_call(
        _norm_matmul_kernel,
        grid=(m // tm, n // tn),
        in_specs=[pl.BlockSpec((tm, k), lambda i, j: (i, 0)),
                  pl.BlockSpec((1, k), lambda i, j: (0, 0)),
                  pl.BlockSpec((k, tn), lambda i, j: (0, j))],
        out_specs=pl.BlockSpec((tm, tn), lambda i, j: (i, j)),
        out_shape=jax.ShapeDtypeStruct((m, n), F32),
        scratch_shapes=[pltpu.VMEM((tm, k), BF16)],
        compiler_params=_cparams("parallel", "arbitrary"),
        name="norm_matmul",
    )(x, g.reshape(1, k), w)


def _in_proj_kernel(x_ref, g_ref, wa_ref, wb_ref, oa_ref, ob_ref, h_ref, *, na):
    j = pl.program_id(1)

    @pl.when(j == 0)
    def _():
        x = x_ref[...]
        ms = jnp.mean(x * x, axis=-1, keepdims=True)
        h_ref[...] = (x * lax.rsqrt(ms + EPS) * g_ref[...]).astype(h_ref.dtype)

    @pl.when(j < na)
    def _():
        oa_ref[...] = _nt(h_ref[...], wa_ref[...])

    @pl.when(j >= na)
    def _():
        ob_ref[...] = _nt(h_ref[...], wb_ref[...])


def in_proj(x, g, wa_t, a_rows, wb_t, *, tm, tn):
    m, k = x.shape
    na, nb = a_rows // tn, wb_t.shape[0] // tn
    assert a_rows % tn == 0 and wb_t.shape[0] % tn == 0
    a_idx = lambda j: jnp.minimum(j, na - 1)
    b_idx = lambda j: jnp.maximum(j - na, 0)
    return pl.pallas_call(
        functools.partial(_in_proj_kernel, na=na),
        grid=(m // tm, na + nb),
        in_specs=[pl.BlockSpec((tm, k), lambda i, j: (i, 0)),
                  pl.BlockSpec((1, k), lambda i, j: (0, 0)),
                  pl.BlockSpec((tn, k), lambda i, j: (a_idx(j), 0)),
                  pl.BlockSpec((tn, k), lambda i, j: (b_idx(j), 0))],
        out_specs=[pl.BlockSpec((tm, tn), lambda i, j: (i, a_idx(j))),
                   pl.BlockSpec((tm, tn), lambda i, j: (i, b_idx(j)))],
        out_shape=[jax.ShapeDtypeStruct((m, na * tn), F32), jax.ShapeDtypeStruct((m, nb * tn), F32)],
        scratch_shapes=[pltpu.VMEM((tm, k), BF16)],
        compiler_params=_cparams("parallel", "arbitrary"),
        name="in_proj",
    )(x, g.reshape(1, k), wa_t, wb_t)


def _matmul_res_kernel(x_ref, w_ref, r_ref, o_ref):
    o_ref[...] = r_ref[...] + _mm(x_ref[...], w_ref[...])


def _matmul_res_into_kernel(x_ref, w_ref, r_ref, buf_ref, o_ref):
    del buf_ref
    o_ref[...] = r_ref[...] + _mm(x_ref[...], w_ref[...])


def _matmul_kernel(x_ref, w_ref, o_ref):
    o_ref[...] = _mm(x_ref[...], w_ref[...])


def matmul(x, w, res=None, *, tm, tn, into=None):
    m, k = x.shape
    n = w.shape[1]
    in_specs = [pl.BlockSpec((tm, k), lambda i, j: (i, 0)),
                pl.BlockSpec((k, tn), lambda i, j: (0, j))]
    args = [x, w]
    kern = _matmul_kernel
    if res is not None:
        in_specs.append(pl.BlockSpec((tm, tn), lambda i, j: (i, j)))
        args.append(res)
        kern = _matmul_res_kernel
    out_shape, row_blk, aliases = jax.ShapeDtypeStruct((m, n), F32), 0, {}
    if into is not None:
        buf, row0 = into
        assert res is not None and row0 % tm == 0 and buf.shape[1] == n
        in_specs.append(pl.BlockSpec(memory_space=pl.ANY))
        args.append(buf)
        kern = _matmul_res_into_kernel
        out_shape, row_blk, aliases = jax.ShapeDtypeStruct(buf.shape, F32), row0 // tm, {len(args) - 1: 0}
    return pl.pallas_call(
        kern,
        grid=(m // tm, n // tn),
        in_specs=in_specs,
        out_specs=pl.BlockSpec((tm, tn), lambda i, j: (row_blk + i, j)),
        out_shape=out_shape,
        input_output_aliases=aliases,
        compiler_params=_cparams("parallel", "arbitrary"),
        name="matmul",
    )(*args)


HG_TB = 128
HG_C = 16
HG_H = HG_C // 2


def _hgrn_prompt_kernel(aq_ref, af_ref, ai_ref, ag_ref, lbl_ref, ng_ref, ya_ref, st_out_ref,
                        st_ref, q_s, k_s, g_s, v_s):
    t = pl.program_id(0)

    @pl.when(t == 0)
    def _():
        st_ref[...] = jnp.zeros_like(st_ref)

    lbl = lbl_ref[...]
    mx = jnp.max(lbl, axis=0, keepdims=True)
    ex = jnp.exp(lbl - mx)
    lb = ex[0:1, :] / jnp.sum(ex, axis=0, keepdims=True)

    f = lb + (1.0 - lb) * jax.nn.sigmoid(af_ref[...])
    logf = jnp.log(f)
    row = lax.broadcasted_iota(jnp.int32, (HG_TB, HG_TB), 0)
    col = lax.broadcasted_iota(jnp.int32, (HG_TB, HG_TB), 1)
    tri = jnp.where((row // HG_C == col // HG_C) & (col <= row), 1.0, 0.0).astype(BF16)
    g = jnp.zeros(logf.shape, F32)
    rem = logf
    for _ in range(3):
        part = rem.astype(BF16)
        g = g + jnp.dot(tri, part, preferred_element_type=F32)
        rem = rem - part.astype(F32)
    g_s[...] = g
    q_s[...] = _silu(aq_ref[...])
    k_s[...] = 1.0 - f
    v_s[...] = ai_ref[...]

    sub = lax.broadcasted_iota(jnp.int32, (HG_C, A_DK), 0)
    sub8 = lax.broadcasted_iota(jnp.int32, (HG_H, A_DK), 0)
    ng = ng_ref[...]

    def chunk(c, carry):
        r0 = pl.multiple_of(c * HG_C, HG_C)
        rows = pl.ds(r0, HG_C)
        for h in range(A_HEADS):
            cols = slice(h * A_DK, (h + 1) * A_DK)
            g = g_s[rows, cols]
            qh = q_s[rows, cols]
            kh = k_s[rows, cols]
            vh = v_s[rows, cols]
            halves = []
            for hb in range(2):
                rs = slice(hb * HG_H, (hb + 1) * HG_H)
                gb, qb, kb, vb = g[rs], qh[rs], kh[rs], vh[rs]
                ob = jnp.zeros((HG_H, A_DV), F32)
                for tt in range(HG_H):
                    d = gb[tt:tt + 1, :] - gb
                    e = jnp.exp(jnp.where(sub8 <= tt, d, -jnp.inf))
                    p = e * (qb[tt:tt + 1, :] * kb)
                    a_col = jnp.sum(p, axis=1, keepdims=True)
                    o_row = jnp.sum(a_col * vb, axis=0, keepdims=True)
                    ob = jnp.where(sub8 == tt, o_row, ob)
                halves.append(ob)
            o = jnp.concatenate(halves, axis=0)
            low = sub < HG_H
            g_mid = g[HG_H - 1:HG_H, :]
            q_hi = jnp.where(low, 0.0, qh * jnp.exp(jnp.minimum(g - g_mid, 0.0)))
            k_lo = jnp.where(low, kh * jnp.exp(jnp.minimum(g_mid - g, 0.0)), 0.0)
            st = st_ref[h]
            g_last = g[HG_C - 1:HG_C, :]
            kt = kh * jnp.exp(g_last - g)
            upd = lax.dot_general(vh.astype(BF16), jnp.concatenate([kt, k_lo], axis=1).astype(BF16),
                                  (((0,), (0,)), ((), ())), preferred_element_type=F32)
            lhs = jnp.concatenate([qh * jnp.exp(g), q_hi], axis=1).astype(BF16)
            rhs = jnp.concatenate([st, upd[:, A_DK:]], axis=1).astype(BF16)
            o = o + _nt(lhs, rhs)
            st_ref[h] = st * jnp.exp(g_last) + upd[:, :A_DK]
            on = o * lax.rsqrt(jnp.mean(o * o, axis=-1, keepdims=True) + EPS) * ng
            ya_ref[rows, cols] = (on * _silu(ag_ref[rows, cols])).astype(ya_ref.dtype)
        return carry

    lax.fori_loop(0, HG_TB // HG_C, chunk, 0, unroll=2)

    @pl.when(t == pl.num_programs(0) - 1)
    def _():
        st_out_ref[...] = st_ref[...]


def hgrn_prompt(z, lb_logits, norm_g):
    m = z.shape[0]
    w = A_HEADS * A_DK

    def zspec(cb):
        return pl.BlockSpec((HG_TB, w), lambda t, cb=cb: (t, cb))

    return pl.pallas_call(
        _hgrn_prompt_kernel,
        grid=(m // HG_TB,),
        in_specs=[zspec(C_AQ // w), zspec(C_AF // w), zspec(C_AI // w), zspec(C_AG // w),
                  pl.BlockSpec(lb_logits.shape, lambda t: (0, 0)),
                  pl.BlockSpec((1, A_DV), lambda t: (0, 0))],
        out_specs=[pl.BlockSpec((HG_TB, w), lambda t: (t, 0)),
                   pl.BlockSpec((A_HEADS, A_DV, A_DK), lambda t: (0, 0, 0))],
        out_shape=[jax.ShapeDtypeStruct((m, w), BF16),
                   jax.ShapeDtypeStruct((A_HEADS, A_DV, A_DK), F32)],
        scratch_shapes=[pltpu.VMEM((A_HEADS, A_DV, A_DK), F32)] + [pltpu.VMEM((HG_TB, w), F32)] * 4,
        compiler_params=_cparams("arbitrary"),
        name="hgrn_prompt",
    )(z, z, z, z, lb_logits, norm_g.reshape(1, A_DV))


def _hgrn_step_kernel(z_ref, lbl_ref, ng_ref, s_ref, ya_ref, s_out_ref):
    lbl = lbl_ref[...]
    mx = jnp.max(lbl, axis=0, keepdims=True)
    ex = jnp.exp(lbl - mx)
    lb = ex[0:1, :] / jnp.sum(ex, axis=0, keepdims=True)
    z = z_ref[0]
    w = A_HEADS * A_DK
    q = _silu(z[:, 0:w])
    f = lb + (1.0 - lb) * jax.nn.sigmoid(z[:, w:2 * w])
    kk = 1.0 - f
    v = z[:, 2 * w:3 * w]
    ag = z[:, 3 * w:4 * w]
    rows = []
    for h in range(A_HEADS):
        cols = slice(h * A_DK, (h + 1) * A_DK)
        rows += [f[:, cols], kk[:, cols], q[:, cols]]
    rows.append(jnp.zeros((LANES - 3 * A_HEADS, A_DK), F32))
    xt = jnp.concatenate(rows, axis=0).T
    ng = ng_ref[...]
    r16 = lambda a: a.astype(BF16).astype(F32)
    outs = []
    for h in range(A_HEADS):
        cols = slice(h * A_DV, (h + 1) * A_DV)
        fcol = xt[:, 3 * h:3 * h + 1]
        kcol = xt[:, 3 * h + 1:3 * h + 2]
        qcol = xt[:, 3 * h + 2:3 * h + 3]
        s_old = s_ref[0, h]
        s_out_ref[0, h] = fcol * s_old + kcol * v[:, cols]
        o = (jnp.sum(r16(qcol * fcol) * r16(s_old), axis=0, keepdims=True)
             + jnp.sum(qcol * kcol, axis=0, keepdims=True) * v[:, cols])
        on = o * lax.rsqrt(jnp.mean(o * o, axis=-1, keepdims=True) + EPS) * ng
        outs.append(on * _silu(ag[:, cols]))
    ya_ref[0] = jnp.concatenate(outs, axis=1).astype(ya_ref.dtype)


def hgrn_step(z4, lb_logits, norm_g, state):
    b = z4.shape[0]
    w = A_HEADS * A_DK
    return pl.pallas_call(
        _hgrn_step_kernel,
        grid=(b,),
        in_specs=[pl.BlockSpec((1, 1, 4 * w), lambda i: (i, 0, 0)),
                  pl.BlockSpec(lb_logits.shape, lambda i: (0, 0)),
                  pl.BlockSpec((1, A_DV), lambda i: (0, 0)),
                  pl.BlockSpec((1, A_HEADS, A_DK, A_DV), lambda i: (i, 0, 0, 0))],
        out_specs=[pl.BlockSpec((1, 1, w), lambda i: (i, 0, 0)),
                   pl.BlockSpec((1, A_HEADS, A_DK, A_DV), lambda i: (i, 0, 0, 0))],
        out_shape=[jax.ShapeDtypeStruct((b, 1, w), F32),
                   jax.ShapeDtypeStruct(state.shape, F32)],
        compiler_params=_cparams("arbitrary"),
        name="hgrn_step",
    )(z4.reshape(b, 1, 4 * w), lb_logits, norm_g.reshape(1, A_DV), state)


BISECT_MAX_ITERS = 48


def _bisect_threshold(count_ge, lo, hi, cnt_lo, topk):
    kf = float(topk)

    def cond(c):
        return jnp.logical_and(c[0] < BISECT_MAX_ITERS, c[-1] > 0.0)

    def body(c):
        it, lo, hi, cl, _ = c
        mid = 0.5 * lo + 0.5 * hi
        cm = count_ge(mid)
        ge = cm >= kf
        lo = jnp.where(ge, mid, lo)
        cl = jnp.where(ge, cm, cl)
        hi = jnp.where(ge, hi, mid)
        busy = jnp.max(jnp.where(cl > kf, 1.0, 0.0))
        return it + 1, lo, hi, cl, busy

    busy0 = jnp.max(jnp.where(cnt_lo > kf, 1.0, 0.0))
    out = lax.while_loop(cond, body, (jnp.int32(0), lo, hi, cnt_lo, busy0))
    return out[1]


DSA_QB = 128
DSA_W = 512
DSA_W3 = 1024
P3_GROUP = 2


def _dsa_prompt_kernel(iq0_ref, iq1_ref, bq_ref, iw_ref, kidx_ref, k_ref, v_ref, bias_ref, o_ref,
                       score_s, qih_s, qs_s, wb_s, m_s, l_s, acc_s, *, topk):
    i = pl.program_id(0)
    nsub = DSA_W // LANES
    nsub3 = DSA_W3 // LANES
    qsub = DSA_QB // LANES
    nch3 = (i * DSA_QB + DSA_QB + DSA_W3 - 1) // DSA_W3
    nch = (i * DSA_QB + DSA_QB + DSA_W - 1) // DSA_W
    qpos = i * DSA_QB + lax.broadcasted_iota(jnp.int32, (DSA_QB, 1), 0)

    iw = iw_ref[...]
    wscale = IDX_DIM ** -0.5 * IDX_HEADS ** -0.5
    for h in range(IDX_HEADS):
        iq_ref, hh = (iq0_ref, h) if h < IDX_HEADS // 2 else (iq1_ref, h - IDX_HEADS // 2)
        qih_s[h] = iq_ref[:, hh * IDX_DIM:(hh + 1) * IDX_DIM].astype(BF16)
        wb_s[h] = jnp.broadcast_to(iw[:, IW_LANE + h:IW_LANE + h + 1] * wscale, (DSA_QB, LANES))
    for h in range(B_HEADS):
        qs_s[h // B_GROUP, (h % B_GROUP) * DSA_QB:(h % B_GROUP + 1) * DSA_QB, :] = (
            bq_ref[:, h * B_HEAD_DIM:(h + 1) * B_HEAD_DIM] * B_HEAD_DIM ** -0.5).astype(BF16)

    def p1(c, carry):
        c0 = pl.multiple_of(c * DSA_W, DSA_W)
        kc = kidx_ref[pl.ds(c0, DSA_W), :]
        sc = [jnp.zeros((DSA_QB, LANES), F32) for _ in range(nsub)]
        for h in range(IDX_HEADS):
            s = jnp.maximum(_nt(qih_s[h], kc), 0.0)
            wb = wb_s[h]
            for j in range(nsub):
                sc[j] = sc[j] + s[:, j * LANES:(j + 1) * LANES] * wb
        for j in range(nsub):
            kpos = c0 + j * LANES + lax.broadcasted_iota(jnp.int32, (1, LANES), 1)
            score_s[c * nsub + j] = jnp.where(kpos <= qpos, sc[j], -jnp.inf)
        return carry

    lax.fori_loop(0, nch // 2, lambda c2, carry: p1(2 * c2 + 1, p1(2 * c2, carry)), 0)
    lax.fori_loop(nch // 2 * 2, nch, p1, 0)

    def fill(tile, carry):
        score_s[tile] = jnp.full((DSA_QB, LANES), -jnp.inf, F32)
        return carry

    lax.fori_loop(nch * nsub, nch3 * nsub3, fill, 0)

    def stats(c, carry):
        mn, mx = carry
        for j in range(nsub):
            s = score_s[c * nsub + j]
            mx = jnp.maximum(mx, s)
            mn = jnp.minimum(mn, jnp.where(s > -jnp.inf, s, jnp.inf))
        return mn, mx

    mn, mx = lax.fori_loop(0, nch, stats, (jnp.full((DSA_QB, LANES), jnp.inf, F32),
                                           jnp.full((DSA_QB, LANES), -jnp.inf, F32)))
    lo0 = jnp.min(mn, axis=1, keepdims=True)
    hi0 = jnp.max(mx, axis=1, keepdims=True)

    def count_ge(thr):
        thr_b = jnp.broadcast_to(thr, (DSA_QB, LANES))

        def body(c, acc):
            for j in range(nsub):
                s = score_s[c * nsub + j]
                acc = acc + jnp.where(s >= thr_b, 1.0, 0.0)
            return acc

        acc = lax.fori_loop(0, nch, body, jnp.zeros((DSA_QB, LANES), F32))
        return jnp.sum(acc, axis=1, keepdims=True)

    thr = _bisect_threshold(count_ge, lo0, hi0, (qpos + 1).astype(F32), topk)
    thr_b = jnp.broadcast_to(thr, (DSA_QB, LANES))

    m_s[...] = jnp.full(m_s.shape, NEG, F32)
    l_s[...] = jnp.zeros(l_s.shape, F32)
    acc_s[...] = jnp.zeros(acc_s.shape, F32)

    def p3(c, with_bias):
        c0 = pl.multiple_of(c * DSA_W3, DSA_W3)
        madd = jnp.concatenate([jnp.where(score_s[c * nsub3 + j] >= thr_b, 0.0, NEG) for j in range(nsub3)], axis=1)
        kc = k_ref[pl.ds(c0, DSA_W3), :]
        vc = v_ref[pl.ds(c0, DSA_W3), :]
        rel = lambda qs, j: i * qsub + qs - (c * nsub3 + j)

        def bias_tile(h, qs, j):
            return jnp.where(rel(qs, j) == 0, bias_ref[h, 0], jnp.where(rel(qs, j) == 1, bias_ref[h, 1], 0.0))

        def scores(n):
            return _nt(qs_s[n], kc[:, n * B_HEAD_DIM:(n + 1) * B_HEAD_DIM])

        def softmax(n, lg):
            lg = lg.reshape(B_GROUP, DSA_QB, DSA_W3) + madd[None]
            if with_bias:
                lg = lg + jnp.stack([jnp.concatenate(
                    [jnp.concatenate([bias_tile(n * B_GROUP + gq, qs, j) for j in range(nsub3)], axis=1)
                     for qs in range(qsub)], axis=0) for gq in range(B_GROUP)])
            m_old = m_s[n]
            m_new = jnp.maximum(m_old, jnp.max(lg, axis=-1, keepdims=True))
            p = jnp.exp(lg - m_new)
            alpha = jnp.exp(m_old - m_new)
            l_s[n] = alpha * l_s[n] + jnp.sum(p, axis=-1, keepdims=True)
            m_s[n] = m_new
            pv = jnp.dot(p.reshape(B_GROUP * DSA_QB, DSA_W3).astype(BF16), vc[:, n * B_HEAD_DIM:(n + 1) * B_HEAD_DIM],
                         preferred_element_type=F32)
            return alpha, pv.reshape(B_GROUP, DSA_QB, B_HEAD_DIM)

        lgs = [scores(n) for n in range(B_KV_HEADS)]
        outs = [softmax(n, lgs[n]) for n in range(B_KV_HEADS)]
        for n in range(B_KV_HEADS):
            acc_s[n] = outs[n][0] * acc_s[n] + outs[n][1]

    n_far = jnp.maximum(i * qsub - 1, 0) // nsub3
    def far_group(cg, carry):
        for u in range(P3_GROUP):
            p3(P3_GROUP * cg + u, False)
        return carry

    lax.fori_loop(0, n_far // P3_GROUP, far_group, 0)
    lax.fori_loop(n_far // P3_GROUP * P3_GROUP, n_far, lambda c, carry: (p3(c, False), carry)[1], 0)
    lax.fori_loop(n_far, nch3, lambda c, carry: (p3(c, True), carry)[1], 0)

    for h in range(B_HEADS):
        n, gq = h // B_GROUP, h % B_GROUP
        o_ref[:, h * B_HEAD_DIM:(h + 1) * B_HEAD_DIM] = (acc_s[n, gq] / l_s[n, gq]).astype(o_ref.dtype)


def dsa_prompt(z, ztail, kidx_bf, k_bf, v_bf, bias_tiles):
    m = z.shape[0]
    topk = min(TOPK_MAX, m // 4)
    wq = B_HEADS * B_HEAD_DIM
    wi2 = IDX_HEADS * IDX_DIM // 2
    kern = functools.partial(_dsa_prompt_kernel, topk=topk)
    return pl.pallas_call(
        kern,
        grid=(m // DSA_QB,),
        in_specs=[pl.BlockSpec((DSA_QB, wi2), lambda i: (i, C_IQ // wi2)),
                  pl.BlockSpec((DSA_QB, wi2), lambda i: (i, C_IQ // wi2 + 1)),
                  pl.BlockSpec((DSA_QB, wq), lambda i: (i, C_BQ // wq)),
                  pl.BlockSpec((DSA_QB, LANES), lambda i: (i, T_SM // LANES)),
                  pl.BlockSpec(kidx_bf.shape, lambda i: (0, 0)),
                  pl.BlockSpec(k_bf.shape, lambda i: (0, 0)),
                  pl.BlockSpec(v_bf.shape, lambda i: (0, 0)),
                  pl.BlockSpec(bias_tiles.shape, lambda i: (0, 0, 0, 0))],
        out_specs=pl.BlockSpec((DSA_QB, wq), lambda i: (i, 0)),
        out_shape=jax.ShapeDtypeStruct((m, wq), BF16),
        scratch_shapes=[pltpu.VMEM((m // LANES, DSA_QB, LANES), F32),
                        pltpu.VMEM((IDX_HEADS, DSA_QB, IDX_DIM), BF16),
                        pltpu.VMEM((B_KV_HEADS, B_GROUP * DSA_QB, B_HEAD_DIM), BF16),
                        pltpu.VMEM((IDX_HEADS, DSA_QB, LANES), F32),
                        pltpu.VMEM((B_KV_HEADS, B_GROUP, DSA_QB, 1), F32),
                        pltpu.VMEM((B_KV_HEADS, B_GROUP, DSA_QB, 1), F32),
                        pltpu.VMEM((B_KV_HEADS, B_GROUP, DSA_QB, B_HEAD_DIM), F32)],
        compiler_params=_cparams("arbitrary"),
        name="dsa_prompt",
    )(z, z, z, ztail, kidx_bf, k_bf, v_bf, bias_tiles)


def _page_copies(table_ref, b, n_pages, src_hbm, dst, sem, rows_per_page=PAGE_SIZE):
    def copy(p):
        return pltpu.make_async_copy(src_hbm.at[table_ref[b, p]],
                                     dst.at[pl.ds(p * rows_per_page, rows_per_page)], sem)
    return copy


def _dsa_scores_kernel(pt_ref, iq_ref, iw_ref, iknew_ref, kidx_hbm, o_ref, buf, sem, *, n_pages):
    b = pl.program_id(0)
    nb = pl.num_programs(0)
    past = n_pages * PAGE_SIZE

    def page_copy(bb, slot, p):
        return pltpu.make_async_copy(kidx_hbm.at[pt_ref[bb, p]],
                                     buf.at[slot, :, pl.ds(pl.multiple_of(p * PAGE_SIZE, PAGE_SIZE), PAGE_SIZE)],
                                     sem.at[slot])

    def start(bb, slot):
        lax.fori_loop(0, n_pages, lambda p, c: (page_copy(bb, slot, p).start(), c)[1], 0)

    def wait(bb, slot):
        lax.fori_loop(0, n_pages, lambda p, c: (page_copy(bb, slot, p).wait(), c)[1], 0)

    slot = b % 2

    @pl.when(b == 0)
    def _():
        start(0, 0)

    @pl.when(b + 1 < nb)
    def _():
        start(b + 1, 1 - slot)

    wait(b, slot)

    r16 = lambda a: a.astype(BF16).astype(F32)
    qi = iq_ref[0].astype(BF16)
    wcol = r16(iw_ref[0]) * (IDX_DIM ** -0.5 * IDX_HEADS ** -0.5)
    s = r16(jnp.maximum(jnp.dot(qi, buf[slot].astype(BF16), preferred_element_type=F32), 0.0))
    o_ref[0, :, 0:past] = jnp.sum(s * wcol, axis=0, keepdims=True)
    sn = r16(jnp.maximum(jnp.dot(qi, iknew_ref[0].astype(BF16), preferred_element_type=F32), 0.0))
    sn = jnp.sum(sn * wcol, axis=0, keepdims=True)
    lane = lax.broadcasted_iota(jnp.int32, (1, LANES), 1)
    o_ref[0, :, past:past + LANES] = jnp.where(lane == 0, sn, -jnp.inf)


def dsa_scores(page_table, iq, iw, iknew_pad, cache_kidx):
    b, n_pages = page_table.shape
    past = n_pages * PAGE_SIZE
    kern = functools.partial(_dsa_scores_kernel, n_pages=n_pages)
    gs = pltpu.PrefetchScalarGridSpec(
        num_scalar_prefetch=1,
        grid=(b,),
        in_specs=[pl.BlockSpec((1, IDX_HEADS, IDX_DIM), lambda i, pt: (i, 0, 0)),
                  pl.BlockSpec((1, IDX_HEADS, 1), lambda i, pt: (i, 0, 0)),
                  pl.BlockSpec((1, IDX_DIM, LANES), lambda i, pt: (i, 0, 0)),
                  pl.BlockSpec(memory_space=pl.ANY)],
        out_specs=pl.BlockSpec((1, 1, past + LANES), lambda i, pt: (i, 0, 0)),
        scratch_shapes=[pltpu.VMEM((2, IDX_DIM, past), F32), pltpu.SemaphoreType.DMA((2,))],
    )
    return pl.pallas_call(
        kern, grid_spec=gs,
        out_shape=jax.ShapeDtypeStruct((b, 1, past + LANES), F32),
        compiler_params=_cparams("arbitrary"),
        name="dsa_scores",
    )(page_table, iq, iw, iknew_pad, cache_kidx)


def _dsa_threshold_kernel(s_ref, thr_ref, *, topk):
    s = s_ref[...]
    nb = s.shape[0]
    finite = s > -jnp.inf
    lo0 = jnp.min(jnp.where(finite, s, jnp.inf), axis=1, keepdims=True)
    hi0 = jnp.max(s, axis=1, keepdims=True)
    cnt0 = jnp.sum(jnp.where(finite, 1.0, 0.0), axis=1, keepdims=True)

    def count_ge(thr):
        return jnp.sum(jnp.where(s_ref[...] >= thr, 1.0, 0.0), axis=1, keepdims=True)

    thr = _bisect_threshold(count_ge, lo0, hi0, cnt0, topk)
    thr_ref[...] = jnp.broadcast_to(thr, (nb, LANES))


def dsa_threshold(scores, topk):
    b, l = scores.shape
    return pl.pallas_call(
        functools.partial(_dsa_threshold_kernel, topk=topk),
        grid=(1,),
        in_specs=[pl.BlockSpec((b, l), lambda i: (0, 0))],
        out_specs=pl.BlockSpec((b, LANES), lambda i: (0, 0)),
        out_shape=jax.ShapeDtypeStruct((b, LANES), F32),
        compiler_params=_cparams("arbitrary"),
        name="dsa_threshold",
    )(scores)


def _dsa_decode_kernel(pt_ref, q_ref, s_ref, thr_ref, knew_ref, vnew_ref, bias_ref, k_hbm, v_hbm, o_ref,
                       kbuf, vbuf, sem, *, n_pages):
    b = pl.program_id(0)
    nb = pl.num_programs(0)
    past = n_pages * PAGE_SIZE

    rpp = PAGE_SIZE * B_KV_HEADS

    def copies(bb, slot):
        ck = _page_copies(pt_ref, bb, n_pages, k_hbm, kbuf.at[slot], sem.at[0, slot], rpp)
        cv = _page_copies(pt_ref, bb, n_pages, v_hbm, vbuf.at[slot], sem.at[1, slot], rpp)
        return ck, cv

    def start(bb, slot):
        ck, cv = copies(bb, slot)
        lax.fori_loop(0, n_pages, lambda p, c: (ck(p).start(), cv(p).start(), c)[2], 0)

    def wait(bb, slot):
        ck, cv = copies(bb, slot)
        lax.fori_loop(0, n_pages, lambda p, c: (ck(p).wait(), cv(p).wait(), c)[2], 0)

    slot = b % 2

    @pl.when(b == 0)
    def _():
        kbuf[:, n_pages * rpp:, :] = jnp.zeros((2, rpp, B_HEAD_DIM), F32)
        vbuf[:, n_pages * rpp:, :] = jnp.zeros((2, rpp, B_HEAD_DIM), F32)
        start(0, 0)

    @pl.when(b + 1 < nb)
    def _():
        start(b + 1, 1 - slot)

    kbuf[slot, n_pages * rpp:n_pages * rpp + 8, :] = knew_ref[0]
    vbuf[slot, n_pages * rpp:n_pages * rpp + 8, :] = vnew_ref[0]
    wait(b, slot)

    sel = s_ref[0] >= thr_ref[0][:, 0:1]
    n_keys = past + PAGE_SIZE
    outs = []
    for n in range(B_KV_HEADS):
        kn = kbuf[slot, pl.ds(n, n_keys, stride=B_KV_HEADS), :].astype(BF16)
        vn = vbuf[slot, pl.ds(n, n_keys, stride=B_KV_HEADS), :].astype(BF16)
        qn = q_ref[0, n].astype(BF16)
        lg = _nt(qn, kn) * B_HEAD_DIM ** -0.5 + bias_ref[n]
        m = jnp.max(jnp.where(sel, lg, NEG), axis=1, keepdims=True)
        p = jnp.where(sel, jnp.exp(lg - m), 0.0)
        p = p / jnp.sum(p, axis=1, keepdims=True)
        outs.append(jnp.dot(p.astype(BF16), vn, preferred_element_type=F32))
    o_ref[0] = jnp.concatenate(outs, axis=0).astype(o_ref.dtype)


def dsa_decode(page_table, q8, scores, thr, knew8, vnew8, bias_rows, cache_k2, cache_v2):
    b, n_pages = page_table.shape
    past = n_pages * PAGE_SIZE
    l = past + LANES
    wkv = B_KV_HEADS * B_HEAD_DIM
    kern = functools.partial(_dsa_decode_kernel, n_pages=n_pages)
    gs = pltpu.PrefetchScalarGridSpec(
        num_scalar_prefetch=1,
        grid=(b,),
        in_specs=[pl.BlockSpec((1, B_KV_HEADS, 8, B_HEAD_DIM), lambda i, pt: (i, 0, 0, 0)),
                  pl.BlockSpec((1, 1, l), lambda i, pt: (i, 0, 0)),
                  pl.BlockSpec((1, 1, LANES), lambda i, pt: (i, 0, 0)),
                  pl.BlockSpec((1, 8, B_HEAD_DIM), lambda i, pt: (i, 0, 0)),
                  pl.BlockSpec((1, 8, B_HEAD_DIM), lambda i, pt: (i, 0, 0)),
                  pl.BlockSpec((B_KV_HEADS, 8, l), lambda i, pt: (0, 0, 0)),
                  pl.BlockSpec(memory_space=pl.ANY),
                  pl.BlockSpec(memory_space=pl.ANY)],
        out_specs=pl.BlockSpec((1, 2 * 8, B_HEAD_DIM), lambda i, pt: (i, 0, 0)),
        scratch_shapes=[pltpu.VMEM((2, l * B_KV_HEADS, B_HEAD_DIM), F32),
                        pltpu.VMEM((2, l * B_KV_HEADS, B_HEAD_DIM), F32),
                        pltpu.SemaphoreType.DMA((2, 2))],
    )
    return pl.pallas_call(
        kern, grid_spec=gs,
        out_shape=jax.ShapeDtypeStruct((b, 2 * 8, B_HEAD_DIM), F32),
        compiler_params=_cparams("arbitrary"),
        name="dsa_decode",
    )(page_table, q8, scores, thr, knew8, vnew8, bias_rows, cache_k2, cache_v2)


def _merge_kernel(ya_ref, yb_ref, ga_ref, gb_ref, wa_ref, wb_ref, o_ref):
    a = _mm(ya_ref[...], wa_ref[...])
    bb = _mm(yb_ref[...], wb_ref[...])
    o_ref[...] = (jax.nn.sigmoid(ga_ref[...]) * a + jax.nn.sigmoid(gb_ref[...]) * bb).astype(o_ref.dtype)


def merge(ya, yb, z, ga_col, gb_col, wa, wb, *, tm, tn):
    m, k = ya.shape
    n = wa.shape[1]
    return pl.pallas_call(
        _merge_kernel,
        grid=(m // tm, n // tn),
        in_specs=[pl.BlockSpec((tm, k), lambda i, j: (i, 0)),
                  pl.BlockSpec((tm, k), lambda i, j: (i, 0)),
                  pl.BlockSpec((tm, tn), lambda i, j: (i, ga_col // tn + j)),
                  pl.BlockSpec((tm, tn), lambda i, j: (i, gb_col // tn + j)),
                  pl.BlockSpec((k, tn), lambda i, j: (0, j)),
                  pl.BlockSpec((k, tn), lambda i, j: (0, j))],
        out_specs=pl.BlockSpec((tm, tn), lambda i, j: (i, j)),
        out_shape=jax.ShapeDtypeStruct((m, n), wa.dtype),
        compiler_params=_cparams("parallel", "arbitrary"),
        name="merge",
    )(ya, yb, z, z, wa, wb)


def _cross_prompt_kernel(x_ref, g_ref, wq_ref, mk_ref, mv_ref, wo_ref, o_ref, *, n_in_tiles):
    x = x_ref[...]
    ms = jnp.mean(x * x, axis=-1, keepdims=True)
    h = (x * lax.rsqrt(ms + EPS) * g_ref[...]).astype(BF16)
    q = jnp.dot(h, wq_ref[...], preferred_element_type=F32)
    outs = []
    for hh in range(X_HEADS):
        cols = slice(hh * X_HEAD_DIM, (hh + 1) * X_HEAD_DIM)
        lg = _nt(q[:, cols].astype(BF16), mk_ref[:, cols]) * X_HEAD_DIM ** -0.5
        mx = jnp.max(lg, axis=1, keepdims=True)
        p = jnp.exp(lg - mx)
        p = p / jnp.sum(p, axis=1, keepdims=True)
        outs.append(jnp.dot(p.astype(BF16), mv_ref[:, cols], preferred_element_type=F32).astype(BF16))
    att = jnp.concatenate(outs, axis=1)
    res = x + jnp.dot(att, wo_ref[...], preferred_element_type=F32)
    o_ref[...] = jnp.where(pl.program_id(0) < n_in_tiles, res, 0.0)


def cross_prompt(x, g, wq, mk, mv, wo, *, tm, out_rows=None):
    m, d = x.shape
    out_rows = m if out_rows is None else out_rows
    n_in = m // tm
    full = lambda a: pl.BlockSpec(a.shape, lambda i: (0,) * a.ndim)
    g2 = g.reshape(1, d)
    return pl.pallas_call(
        functools.partial(_cross_prompt_kernel, n_in_tiles=n_in),
        grid=(pl.cdiv(out_rows, tm),),
        in_specs=[pl.BlockSpec((tm, d), lambda i: (jnp.minimum(i, n_in - 1), 0)),
                  full(g2), full(wq), full(mk), full(mv), full(wo)],
        out_specs=pl.BlockSpec((tm, d), lambda i: (i, 0)),
        out_shape=jax.ShapeDtypeStruct((out_rows, d), F32),
        compiler_params=_cparams("parallel"),
        name="cross_prompt",
    )(x, g2, wq, mk, mv, wo)


def _cross_step_kernel(q_ref, mk_ref, mv_ref, o_ref):
    r16 = lambda a: a.astype(BF16).astype(F32)
    q = r16(q_ref[0])
    outs = []
    for hh in range(X_HEADS):
        cols = slice(hh * X_HEAD_DIM, (hh + 1) * X_HEAD_DIM)
        kh = r16(mk_ref[0, :, cols])
        vh = r16(mv_ref[0, :, cols])
        lg = jnp.sum(kh * q[:, cols], axis=1, keepdims=True) * X_HEAD_DIM ** -0.5
        mx = jnp.max(lg, axis=0, keepdims=True)
        p = jnp.exp(lg - mx)
        p = r16(p / jnp.sum(p, axis=0, keepdims=True))
        outs.append(jnp.sum(p * vh, axis=0, keepdims=True))
    o_ref[0] = jnp.concatenate(outs, axis=1).astype(o_ref.dtype)


def cross_step(q, mk, mv):
    b, w = q.shape
    mem = mk.shape[1]
    return pl.pallas_call(
        _cross_step_kernel,
        grid=(b,),
        in_specs=[pl.BlockSpec((1, 1, w), lambda i: (i, 0, 0)),
                  pl.BlockSpec((1, mem, w), lambda i: (i, 0, 0)),
                  pl.BlockSpec((1, mem, w), lambda i: (i, 0, 0))],
        out_specs=pl.BlockSpec((1, 1, w), lambda i: (i, 0, 0)),
        out_shape=jax.ShapeDtypeStruct((b, 1, w), F32),
        compiler_params=_cparams("arbitrary"),
        name="cross_step",
    )(q.reshape(b, 1, w), mk, mv)


def _pack_bf16_pairs(x):
    c = x.shape[1] // 2
    u = pltpu.bitcast(x.astype(BF16).astype(F32), jnp.uint32)
    return lax.shift_right_logical(u[:, :c], jnp.uint32(16)) | u[:, c:]


def _unpack_bf16_pairs(u):
    lo = pltpu.bitcast(lax.shift_left(u, jnp.uint32(16)), F32)
    hi = pltpu.bitcast(u & jnp.uint32(0xFFFF0000), F32)
    return jnp.concatenate([lo, hi], axis=1)


def _router_kernel(x_ref, g_ref, w_ref, b_ref, hf_ref, route_ref):
    x = x_ref[...]
    ms = jnp.mean(x * x, axis=-1, keepdims=True)
    hf = x * lax.rsqrt(ms + EPS) * g_ref[...]
    hf_ref[...] = _pack_bf16_pairs(hf)
    lg = _mm(hf, w_ref[...]) + b_ref[...]
    tm = lg.shape[0]
    lane = lax.broadcasted_iota(jnp.int32, (tm, LANES), 1)
    big = jnp.int32(LANES)
    is_g = lane < N_GROUPS
    gmax = jnp.max(jnp.where(is_g, lg, -jnp.inf), axis=1, keepdims=True)
    grp = jnp.min(jnp.where(is_g & (lg == gmax), lane, big), axis=1, keepdims=True)
    p_grp = 1.0 / jnp.sum(jnp.where(is_g, jnp.exp(lg - gmax), 0.0), axis=1, keepdims=True)
    e_lo = N_GROUPS + grp * EXP_PER_GROUP
    in_g = (lane >= e_lo) & (lane < e_lo + EXP_PER_GROUP)
    v1 = jnp.max(jnp.where(in_g, lg, -jnp.inf), axis=1, keepdims=True)
    i1 = jnp.min(jnp.where(in_g & (lg == v1), lane, big), axis=1, keepdims=True)
    rest = in_g & (lane != i1)
    v2 = jnp.max(jnp.where(rest, lg, -jnp.inf), axis=1, keepdims=True)
    i2 = jnp.min(jnp.where(rest & (lg == v2), lane, big), axis=1, keepdims=True)
    e2 = jnp.exp(v2 - v1)
    g1 = p_grp / (1.0 + e2)
    g2 = p_grp * e2 / (1.0 + e2)
    r = jnp.where(lane == 0, (i1 - N_GROUPS).astype(F32),
                  jnp.where(lane == 1, (i2 - N_GROUPS).astype(F32),
                            jnp.where(lane == 2, g1, jnp.where(lane == 3, g2, 0.0))))
    route_ref[...] = r


def router(x, g, w_pad, b_pad, *, tm):
    m, d = x.shape
    return pl.pallas_call(
        _router_kernel,
        grid=(pl.cdiv(m, tm),),
        in_specs=[pl.BlockSpec((tm, d), lambda i: (i, 0)),
                  pl.BlockSpec((1, d), lambda i: (0, 0)),
                  pl.BlockSpec((d, LANES), lambda i: (0, 0)),
                  pl.BlockSpec((1, LANES), lambda i: (0, 0))],
        out_specs=[pl.BlockSpec((tm, d // 2), lambda i: (i, 0)),
                   pl.BlockSpec((tm, LANES), lambda i: (i, 0))],
        out_shape=[jax.ShapeDtypeStruct((m, d // 2), jnp.uint32), jax.ShapeDtypeStruct((m, LANES), F32)],
        compiler_params=_cparams("parallel"),
        name="router",
    )(x, g.reshape(1, d), w_pad, b_pad)


def _moe_kernel(be_ref, nxt_ref, nused_ref, sbase_ref, nvalid_ref, padj_ref, order_ref, hf_hbm, wg_hbm, wu_hbm,
                wd_hbm, y_hbm, xbuf, obuf, wg_f, wu_f, wd_f, wg_s, wu_s, wd_s, sem_in, sem_out, sem_w,
                *, n_tokens):
    b = pl.program_id(0)
    nb = pl.num_programs(0)
    blk = xbuf.shape[1]
    xslot = b % 3
    used = b < nused_ref[0]
    next_used = b + 2 < nused_ref[0]
    n_assign = order_ref.shape[0]

    def slot_info(bb):
        base, nv, pad0 = sbase_ref[bb], nvalid_ref[bb], padj_ref[bb]

        def info(r):
            valid = r < nv
            asg = order_ref[jnp.minimum(base + r, n_assign - 1)]
            tok = lax.shift_right_logical(asg, 1)
            return jnp.where(valid, tok, 0), jnp.where(valid, (asg & 1) * n_tokens + tok, pad0 + r)
        return info

    def weight_copies(e):
        return (pltpu.make_async_copy(wg_hbm.at[e], wg_f, sem_w.at[0]),
                pltpu.make_async_copy(wu_hbm.at[e], wu_f, sem_w.at[1]),
                pltpu.make_async_copy(wd_hbm.at[e], wd_f, sem_w.at[2]))

    def gather_start(bb, sl):
        info = slot_info(bb)
        for r in range(blk):
            pltpu.make_async_copy(hf_hbm.at[pl.ds(info(r)[0], 1)], xbuf.at[sl, pl.ds(r, 1)],
                                  sem_in.at[sl]).start()

    def scatter_start(bb, sl):
        info = slot_info(bb)
        for r in range(blk):
            pltpu.make_async_copy(obuf.at[sl, pl.ds(r, 1)], y_hbm.at[pl.ds(info(r)[1], 1)],
                                  sem_out.at[sl]).start(priority=r % 2)

    def gather_wait(sl):
        pltpu.make_async_copy(hf_hbm.at[pl.ds(0, blk)], xbuf.at[sl], sem_in.at[sl]).wait()

    def scatter_wait(sl):
        pltpu.make_async_copy(obuf.at[sl], y_hbm.at[pl.ds(0, blk)], sem_out.at[sl]).wait()

    @pl.when(b == 0)
    def _():
        for cp in weight_copies(be_ref[0]):
            cp.start(priority=1)
        gather_start(0, 0)
        obuf[...] = jnp.zeros(obuf.shape, obuf.dtype)

    @pl.when(jnp.logical_and(b == 0, 1 < nused_ref[0]))
    def _():
        gather_start(1, 1)

    changed = jnp.logical_and(used, jnp.logical_or(b == 0, be_ref[b] != be_ref[jnp.maximum(b - 1, 0)]))

    @pl.when(changed)
    def _():
        for cp in weight_copies(be_ref[b]):
            cp.wait()
        wg_s[...] = wg_f[...].astype(BF16)
        wu_s[...] = wu_f[...].astype(BF16)
        wd_s[...] = wd_f[...].astype(BF16)

    @pl.when(jnp.logical_and(changed, nxt_ref[b] >= 0))
    def _():
        for cp in weight_copies(nxt_ref[b]):
            cp.start(priority=1)

    @pl.when(used)
    def _():
        gather_wait(xslot)

    @pl.when(b >= 3)
    def _():
        scatter_wait(xslot)

    def step(prefetch, flush_prev, compute):
        if prefetch:
            gather_start(b + 2, (b + 2) % 3)
        if flush_prev:
            scatter_start(b - 1, (b + 2) % 3)
        if compute:
            x = _unpack_bf16_pairs(xbuf[xslot]).astype(BF16)
            gg = jnp.dot(x, wg_s[...], preferred_element_type=F32)
            uu = jnp.dot(x, wu_s[...], preferred_element_type=F32)
            a = (_silu(gg) * uu).astype(BF16)
            obuf[xslot] = _pack_bf16_pairs(jnp.dot(a, wd_s[...], preferred_element_type=F32))

    first, last = b == 0, b == nb - 1
    land, lnot = jnp.logical_and, jnp.logical_not
    pl.when(land(first, next_used))(lambda: step(True, False, True))
    pl.when(land(first, lnot(next_used)))(lambda: step(False, False, True))
    pl.when(land(lnot(first), land(used, next_used)))(lambda: step(True, True, True))
    pl.when(land(lnot(first), land(used, lnot(next_used))))(lambda: step(False, True, True))
    pl.when(land(lnot(first), lnot(used)))(lambda: step(False, True, False))

    @pl.when(last)
    def _():
        scatter_start(b, xslot)
        scatter_wait(xslot)

    @pl.when(jnp.logical_and(last, b >= 1))
    def _():
        scatter_wait((b + 2) % 3)

    @pl.when(jnp.logical_and(last, b >= 2))
    def _():
        scatter_wait((b + 1) % 3)


def moe_experts(tables, hf, w_g, w_u, w_d, *, blk, out_rows):
    n_blocks = tables[0].shape[0]
    dp = hf.shape[1]
    d = 2 * dp
    ff = w_g.shape[2]
    gs = pltpu.PrefetchScalarGridSpec(
        num_scalar_prefetch=len(tables),
        grid=(n_blocks,),
        in_specs=[pl.BlockSpec(memory_space=pl.ANY)] * 4,
        out_specs=pl.BlockSpec(memory_space=pl.ANY),
        scratch_shapes=[pltpu.VMEM((3, blk, dp), jnp.uint32), pltpu.VMEM((3, blk, dp), jnp.uint32),
                        pltpu.VMEM((d, ff), F32), pltpu.VMEM((d, ff), F32), pltpu.VMEM((ff, d), F32),
                        pltpu.VMEM((d, ff), BF16), pltpu.VMEM((d, ff), BF16), pltpu.VMEM((ff, d), BF16),
                        pltpu.SemaphoreType.DMA((3,)), pltpu.SemaphoreType.DMA((3,)),
                        pltpu.SemaphoreType.DMA((3,))],
    )
    return pl.pallas_call(
        functools.partial(_moe_kernel, n_tokens=hf.shape[0]), grid_spec=gs,
        out_shape=jax.ShapeDtypeStruct((out_rows, dp), jnp.uint32),
        compiler_params=_cparams("arbitrary"),
        name="moe_experts",
    )(*tables, hf, w_g, w_u, w_d)


def _combine_kernel(x_ref, route_ref, gf_ref, y1_ref, y2_ref, op_ref, os_ref):
    i = pl.program_id(0)
    route = route_ref[...]
    x = (x_ref[...] + route[:, 2:3] * _unpack_bf16_pairs(y1_ref[...])
         + route[:, 3:4] * _unpack_bf16_pairs(y2_ref[...]))
    ms = jnp.mean(x * x, axis=-1, keepdims=True)
    out = x * lax.rsqrt(ms + EPS) * gf_ref[...]

    @pl.when(i < pl.num_programs(0) - 1)
    def _():
        op_ref[...] = out

    @pl.when(i == pl.num_programs(0) - 1)
    def _():
        os_ref[...] = out[:os_ref.shape[0]]


def combine(x, route, gf, y, plane, n_prompt, *, tm):
    m, d = x.shape
    n_tiles = n_prompt // tm
    assert n_prompt % tm == 0 and 0 < m - n_prompt <= tm
    return pl.pallas_call(
        _combine_kernel,
        grid=(n_tiles + 1,),
        in_specs=[pl.BlockSpec((tm, d), lambda i: (i, 0)),
                  pl.BlockSpec((tm, LANES), lambda i: (i, 0)),
                  pl.BlockSpec((1, d), lambda i: (0, 0)),
                  pl.BlockSpec((tm, d // 2), lambda i: (i, 0)),
                  pl.BlockSpec((tm, d // 2), lambda i: (plane // tm + i, 0))],
        out_specs=[pl.BlockSpec((tm, d), lambda i: (jnp.minimum(i, n_tiles - 1), 0)),
                   pl.BlockSpec((m - n_prompt, d), lambda i: (0, 0))],
        out_shape=[jax.ShapeDtypeStruct((n_prompt, d), F32), jax.ShapeDtypeStruct((m - n_prompt, d), F32)],
        compiler_params=_cparams("arbitrary"),
        name="combine",
    )(x, route, gf.reshape(1, d), y, y)


def _t5_bucket(dist):
    dist = jnp.asarray(dist, jnp.int32)
    max_exact = REL_BUCKETS // 2
    dist_f = jnp.maximum(dist, 1).astype(F32)
    large = max_exact + (jnp.log(dist_f / max_exact) / math.log(REL_MAX_DIST / max_exact)
                         * (REL_BUCKETS - max_exact)).astype(jnp.int32)
    large = jnp.minimum(large, REL_BUCKETS - 1)
    return jnp.where(dist < max_exact, dist, large)


def _bias_tables(rel_bias, past):
    r = np.arange(LANES)
    diff = r[:, None] - r[None, :]
    buckets = jnp.stack([_t5_bucket(np.maximum(diff, 0)),
                         _t5_bucket(np.maximum(diff + LANES, 0)),
                         _t5_bucket(np.full((LANES, LANES), 2 * LANES))])
    def lookup(bkt):
        oh = (bkt.reshape(-1, 1) == jnp.arange(REL_BUCKETS)[None, :]).astype(F32)
        out = jnp.dot(oh, rel_bias.astype(F32), precision=lax.Precision.HIGHEST)
        return out.T.reshape((rel_bias.shape[1],) + bkt.shape)

    tiles = lookup(buckets)
    tiles = tiles - tiles[:, 2:3]
    dist = np.maximum(past - np.arange(past + LANES), 0)
    rows = lookup(_t5_bucket(dist))
    rows = rows.reshape(B_KV_HEADS, B_GROUP, past + LANES)
    rows = jnp.concatenate([rows, jnp.zeros_like(rows)], axis=1)
    return tiles, rows


def _dispatch(eid, n_tokens, blk):
    a = eid.shape[0]
    assert EXPERT_TOPK == 2 and a == EXPERT_TOPK * n_tokens
    n_blocks = -(-(a + N_EXPERTS * (blk - 1)) // blk)
    rows = n_blocks * blk
    order = jnp.argsort(eid).astype(jnp.int32)
    counts = jnp.sum(eid[:, None] == jnp.arange(N_EXPERTS)[None, :], axis=0).astype(jnp.int32)
    cum = jnp.cumsum(counts)
    starts = cum - counts
    padded = (counts + blk - 1) // blk * blk
    pad_end = jnp.cumsum(padded)
    pad_start = pad_end - padded
    blocks = jnp.arange(n_blocks, dtype=jnp.int32)
    block_e = jnp.minimum(jnp.sum(pad_end[None, :] <= (blocks * blk)[:, None], axis=1), N_EXPERTS - 1)
    n_used = pad_end[-1] // blk
    off = blocks * blk - pad_start[block_e]
    sbase = starts[block_e] + off
    nvalid = jnp.where(blocks < n_used, jnp.clip(counts[block_e] - off, 0, blk), 0)
    padj = a + blocks * blk - cum[block_e]
    new_run = jnp.concatenate([jnp.array([True]), block_e[1:] != block_e[:-1]])
    run_start = jnp.where(new_run & (blocks < n_used), blocks, n_blocks)
    nxt = jnp.concatenate([jnp.flip(lax.cummin(jnp.flip(run_start)))[1:], jnp.array([n_blocks])])
    next_e = jnp.where(nxt < n_blocks, block_e[jnp.minimum(nxt, n_blocks - 1)], -1)
    i32 = lambda v: v.astype(jnp.int32)
    return (i32(block_e), i32(next_e), i32(n_used).reshape(1), i32(sbase), i32(nvalid), i32(padj), order), rows


def kernel(x_prompt, x_sample, mem_prompt, cache_k, cache_v, cache_kidx, page_table, state_hgrn, cache_mem_k,
           cache_mem_v, norm_mix, w_in, hgrn_lb_logits, hgrn_norm, w_branch_a, w_branch_b, w_out, norm_cross, w_xq,
           w_xk, w_xv, w_xo, norm_ffn, w_router_group, b_router_group, w_router_expert, b_router_expert, w_exp_gate,
           w_exp_up, w_exp_down, rel_bias, norm_final):
    assert w_in.shape[0] == 1, "single-layer step"
    l = 0
    drop0 = lambda a: a.reshape(a.shape[1:])
    bp, t, d = x_prompt.shape
    db = x_sample.shape[0]
    past = page_table.shape[1] * PAGE_SIZE
    xp = x_prompt.reshape(bp * t, d)
    xs = x_sample.reshape(db, d)

    wi_t = jnp.transpose(drop0(w_in)).astype(BF16)
    assert wi_t.shape[0] - NZ_MAIN + T_PAD == NZ_TAIL
    w_tail_t = jnp.pad(wi_t[NZ_MAIN:], ((T_PAD, 0), (0, 0)))
    wa, wb, wo = w_branch_a[l].astype(BF16), w_branch_b[l].astype(BF16), w_out[l].astype(BF16)
    wxq, wxk, wxv, wxo = (w_xq[l].astype(BF16), w_xk[l].astype(BF16), w_xv[l].astype(BF16), w_xo[l].astype(BF16))
    w_route = jnp.pad(jnp.concatenate([w_router_group[l], w_router_expert[l]], axis=1),
                      ((0, 0), (0, LANES - N_GROUPS - N_EXPERTS))).astype(BF16)
    b_route = jnp.pad(jnp.concatenate([b_router_group[l], b_router_expert[l]]),
                      (0, LANES - N_GROUPS - N_EXPERTS)).reshape(1, LANES)
    bias_tiles, bias_rows = _bias_tables(rel_bias, past)

    zp, zpt = in_proj(xp, norm_mix[l], wi_t, NZ_MAIN, w_tail_t, tm=1024, tn=512)
    kp = zp[:, C_BK:C_BK + 256]
    vp = zp[:, C_BV:C_BV + 256]
    ikp = zpt[:, T_SM + IK_LANE:T_SM + IK_LANE + IDX_DIM]
    ya_p, st_p = hgrn_prompt(zp, hgrn_lb_logits, hgrn_norm[l])
    yb_p = dsa_prompt(zp, zpt, ikp.astype(BF16), kp.astype(BF16), vp.astype(BF16), bias_tiles)
    mg_p = merge(ya_p, yb_p, zpt, T_GA, T_GB, wa, wb, tm=1024, tn=512)
    x1p = matmul(mg_p, wo, xp, tm=1024, tn=512)
    memp = mem_prompt.reshape(-1, d)
    mk = matmul(memp, wxk, tm=memp.shape[0], tn=512)
    mv = matmul(memp, wxv, tm=memp.shape[0], tn=512)
    tmc = 128
    n = -(-(bp * t + db) // tmc) * tmc
    x2p = cross_prompt(x1p, norm_cross[l], wxq, mk.astype(BF16), mv.astype(BF16), wxo, tm=512, out_rows=n)

    zs, zst = in_proj(xs, norm_mix[l], wi_t, NZ_MAIN, w_tail_t, tm=db, tn=512)
    ks = zs[:, C_BK:C_BK + 256]
    vs = zs[:, C_BV:C_BV + 256]
    iks = zst[:, T_SM + IK_LANE:T_SM + IK_LANE + IDX_DIM]
    ya_s, st_s = hgrn_step(zs[:, :4 * A_HEADS * A_DK], hgrn_lb_logits, hgrn_norm[l], drop0(state_hgrn))
    iq_s = zs[:, C_IQ:C_IQ + IDX_HEADS * IDX_DIM].reshape(db, IDX_HEADS, IDX_DIM)
    iw_s = zst[:, T_SM + IW_LANE:T_SM + IW_LANE + IDX_HEADS].reshape(db, IDX_HEADS, 1)
    iknew_pad = jnp.pad(iks[:, :, None], ((0, 0), (0, 0), (0, LANES - 1)))
    scores = dsa_scores(page_table, iq_s, iw_s, iknew_pad,
                        jnp.swapaxes(drop0(cache_kidx), 1, 2)).reshape(db, past + LANES)
    topk_s = min(TOPK_MAX, (past + 1) // 4)
    thr = dsa_threshold(scores, topk_s)
    q8 = jnp.pad(zs[:, C_BQ:C_BQ + B_HEADS * B_HEAD_DIM].reshape(db, B_KV_HEADS, B_GROUP, B_HEAD_DIM),
                 ((0, 0), (0, 0), (0, 8 - B_GROUP), (0, 0)))
    knew8 = jnp.pad(ks.reshape(db, B_KV_HEADS, B_HEAD_DIM), ((0, 0), (0, 8 - B_KV_HEADS), (0, 0)))
    vnew8 = jnp.pad(vs.reshape(db, B_KV_HEADS, B_HEAD_DIM), ((0, 0), (0, 8 - B_KV_HEADS), (0, 0)))
    n_pool = cache_k.shape[1]
    ob = dsa_decode(page_table, q8, scores.reshape(db, 1, -1), thr.reshape(db, 1, LANES), knew8, vnew8, bias_rows,
                    cache_k.reshape(n_pool, PAGE_SIZE * B_KV_HEADS, B_HEAD_DIM),
                    cache_v.reshape(n_pool, PAGE_SIZE * B_KV_HEADS, B_HEAD_DIM))
    yb_s = ob.reshape(db, B_KV_HEADS, 8, B_HEAD_DIM)[:, :, :B_GROUP].reshape(db, B_HEADS * B_HEAD_DIM)
    mg_s = merge(ya_s.reshape(db, -1), yb_s, zst, T_GA, T_GB, wa, wb, tm=db, tn=512)
    x1s = matmul(mg_s, wo, xs, tm=db, tn=512)
    qx_s = norm_matmul(x1s, norm_cross[l], wxq, tm=db, tn=512)
    mem = cache_mem_k.shape[2]
    att_s = cross_step(qx_s, cache_mem_k.reshape(db, mem, -1), cache_mem_v.reshape(db, mem, -1))
    x2 = matmul(att_s.reshape(db, -1), wxo, x1s, tm=db, tn=512, into=(x2p, bp * t))

    hf, route = router(x2, norm_ffn[l], w_route, b_route, tm=256)
    eid = route[:, :EXPERT_TOPK].astype(jnp.int32).reshape(-1)
    tables, rows = _dispatch(eid, n, MOE_ROWS)
    ye = moe_experts(tables, hf, drop0(w_exp_gate), drop0(w_exp_up), drop0(w_exp_down), blk=MOE_ROWS, out_rows=rows)
    y_p, y_s = combine(x2, route, norm_final, ye, n, bp * t, tm=tmc)

    y_prompt = y_p.reshape(bp, t, d)
    y_sample = y_s[:db].reshape(db, 1, d)
    return (y_prompt, y_sample,
            kp.reshape(1, bp, t, B_KV_HEADS, B_HEAD_DIM), vp.reshape(1, bp, t, B_KV_HEADS, B_HEAD_DIM),
            ikp.reshape(1, bp, t, IDX_DIM),
            jnp.swapaxes(st_p, 1, 2).reshape(1, bp, A_HEADS, A_DK, A_DV),
            mk.reshape(1, bp, -1, X_HEADS, X_HEAD_DIM), mv.reshape(1, bp, -1, X_HEADS, X_HEAD_DIM),
            ks.reshape(1, db, 1, B_KV_HEADS, B_HEAD_DIM), vs.reshape(1, db, 1, B_KV_HEADS, B_HEAD_DIM),
            iks.reshape(1, db, 1, IDX_DIM),
            st_s.reshape(1, db, A_HEADS, A_DK, A_DV))
```

```python
import functools
import math

import jax
import jax.numpy as jnp
import numpy as np
from jax import lax
from jax.experimental import pallas as pl
from jax.experimental.pallas import tpu as pltpu

F32 = jnp.float32
BF16 = jnp.bfloat16
EPS = 1e-6

D_MODEL = 2048
A_HEADS, A_DK, A_DV = 8, 128, 128
B_HEADS, B_KV_HEADS, B_HEAD_DIM = 8, 2, 128
B_GROUP = B_HEADS // B_KV_HEADS
IDX_HEADS, IDX_DIM = 16, 64
TOPK_MAX = 256
PAGE_SIZE = 128
REL_BUCKETS, REL_MAX_DIST = 32, 128
X_HEADS, X_HEAD_DIM = 4, 128
N_GROUPS, EXP_PER_GROUP = 4, 8
N_EXPERTS = N_GROUPS * EXP_PER_GROUP
EXPERT_TOPK = 2
EXPERT_FF = 512
MOE_ROWS = 256

LANES = 128
VMEM_LIMIT = 56 * 1024 * 1024

TM_WIDE = 1024
TM_CROSS = 512
TM_ROUTER = 256
TM_TOKEN = 128
TN = 512
KV_W = B_KV_HEADS * B_HEAD_DIM

NEG = -1e30

C_AQ, C_AF, C_AI, C_AG, C_BQ, C_BK, C_BV, C_IQ = 0, 1024, 2048, 3072, 4096, 5120, 5376, 5632
NZ_MAIN = 6656
T_PAD = 432
T_SM, IW_LANE, IK_LANE = 384, 48, 64
T_GA, T_GB = 512, 2560
NZ_TAIL = 4608


def _cparams(*sem):
    return pltpu.CompilerParams(dimension_semantics=sem, vmem_limit_bytes=VMEM_LIMIT)


def _silu(x):
    return x * jax.nn.sigmoid(x)


def _nt(a, b):
    return lax.dot_general(a, b, (((1,), (1,)), ((), ())), preferred_element_type=F32)


def _mm(a, w):
    return jnp.dot(a.astype(BF16), w.astype(BF16), preferred_element_type=F32)


def _norm_matmul_kernel(x_ref, g_ref, w_ref, o_ref, h_ref):
    @pl.when(pl.program_id(1) == 0)
    def _():
        x = x_ref[...]
        ms = jnp.mean(x * x, axis=-1, keepdims=True)
        h_ref[...] = (x * lax.rsqrt(ms + EPS) * g_ref[...]).astype(h_ref.dtype)

    o_ref[...] = _mm(h_ref[...], w_ref[...])


def norm_matmul(x, g, w, *, tm, tn):
    m, k = x.shape
    n = w.shape[1]
    assert n % tn == 0
    return pl.pallas_call(
        _norm_matmul_kernel,
        grid=(m // tm, n // tn),
        in_specs=[pl.BlockSpec((tm, k), lambda i, j: (i, 0)),
                  pl.BlockSpec((1, k), lambda i, j: (0, 0)),
                  pl.BlockSpec((k, tn), lambda i, j: (0, j))],
        out_specs=pl.BlockSpec((tm, tn), lambda i, j: (i, j)),
        out_shape=jax.ShapeDtypeStruct((m, n), F32),
        scratch_shapes=[pltpu.VMEM((tm, k), BF16)],
        compiler_params=_cparams("parallel", "arbitrary"),
        name="norm_matmul",
    )(x, g.reshape(1, k), w)


def _in_proj_kernel(x_ref, g_ref, wa_ref, wb_ref, oa_ref, ob_ref, h_ref, *, na):
    j = pl.program_id(1)

    @pl.when(j == 0)
    def _():
        x = x_ref[...]
        ms = jnp.mean(x * x, axis=-1, keepdims=True)
        h_ref[...] = (x * lax.rsqrt(ms + EPS) * g_ref[...]).astype(h_ref.dtype)

    @pl.when(j < na)
    def _():
        oa_ref[...] = _nt(h_ref[...], wa_ref[...])

    @pl.when(j >= na)
    def _():
        ob_ref[...] = _nt(h_ref[...], wb_ref[...])


def in_proj(x, g, wa_t, a_rows, wb_t, *, tm, tn):
    m, k = x.shape
    na, nb = a_rows // tn, wb_t.shape[0] // tn
    assert a_rows % tn == 0 and wb_t.shape[0] % tn == 0
    a_idx = lambda j: jnp.minimum(j, na - 1)
    b_idx = lambda j: jnp.maximum(j - na, 0)
    return pl.pallas_call(
        functools.partial(_in_proj_kernel, na=na),
        grid=(m // tm, na + nb),
        in_specs=[pl.BlockSpec((tm, k), lambda i, j: (i, 0)),
                  pl.BlockSpec((1, k), lambda i, j: (0, 0)),
                  pl.BlockSpec((tn, k), lambda i, j: (a_idx(j), 0)),
                  pl.BlockSpec((tn, k), lambda i, j: (b_idx(j), 0))],
        out_specs=[pl.BlockSpec((tm, tn), lambda i, j: (i, a_idx(j))),
                   pl.BlockSpec((tm, tn), lambda i, j: (i, b_idx(j)))],
        out_shape=[jax.ShapeDtypeStruct((m, na * tn), F32), jax.ShapeDtypeStruct((m, nb * tn), F32)],
        scratch_shapes=[pltpu.VMEM((tm, k), BF16)],
        compiler_params=_cparams("parallel", "arbitrary"),
        name="in_proj",
    )(x, g.reshape(1, k), wa_t, wb_t)


def _matmul_res_kernel(x_ref, w_ref, r_ref, o_ref):
    o_ref[...] = r_ref[...] + _mm(x_ref[...], w_ref[...])


def _matmul_res_into_kernel(x_ref, w_ref, r_ref, buf_ref, o_ref):
    del buf_ref
    o_ref[...] = r_ref[...] + _mm(x_ref[...], w_ref[...])


def _matmul_kernel(x_ref, w_ref, o_ref):
    o_ref[...] = _mm(x_ref[...], w_ref[...])


def matmul(x, w, res=None, *, tm, tn, into=None):
    m, k = x.shape
    n = w.shape[1]
    in_specs = [pl.BlockSpec((tm, k), lambda i, j: (i, 0)),
                pl.BlockSpec((k, tn), lambda i, j: (0, j))]
    args = [x, w]
    kern = _matmul_kernel
    if res is not None:
        in_specs.append(pl.BlockSpec((tm, tn), lambda i, j: (i, j)))
        args.append(res)
        kern = _matmul_res_kernel
    out_shape, row_blk, aliases = jax.ShapeDtypeStruct((m, n), F32), 0, {}
    if into is not None:
        buf, row0 = into
        assert res is not None and row0 % tm == 0 and buf.shape[1] == n
        in_specs.append(pl.BlockSpec(memory_space=pl.ANY))
        args.append(buf)
        kern = _matmul_res_into_kernel
        out_shape, row_blk, aliases = jax.ShapeDtypeStruct(buf.shape, F32), row0 // tm, {len(args) - 1: 0}
    return pl.pallas_call(
        kern,
        grid=(m // tm, n // tn),
        in_specs=in_specs,
        out_specs=pl.BlockSpec((tm, tn), lambda i, j: (row_blk + i, j)),
        out_shape=out_shape,
        input_output_aliases=aliases,
        compiler_params=_cparams("parallel", "arbitrary"),
        name="matmul",
    )(*args)


HG_TB = 128
HG_C = 16
HG_H = HG_C // 2


def _hgrn_prompt_kernel(aq_ref, af_ref, ai_ref, ag_ref, lbl_ref, ng_ref, ya_ref, st_out_ref,
                        st_ref, q_s, k_s, g_s, v_s):
    t = pl.program_id(0)

    @pl.when(t == 0)
    def _():
        st_ref[...] = jnp.zeros_like(st_ref)

    lbl = lbl_ref[...]
    mx = jnp.max(lbl, axis=0, keepdims=True)
    ex = jnp.exp(lbl - mx)
    lb = ex[0:1, :] / jnp.sum(ex, axis=0, keepdims=True)

    f = lb + (1.0 - lb) * jax.nn.sigmoid(af_ref[...])
    logf = jnp.log(f)
    row = lax.broadcasted_iota(jnp.int32, (HG_TB, HG_TB), 0)
    col = lax.broadcasted_iota(jnp.int32, (HG_TB, HG_TB), 1)
    tri = jnp.where((row // HG_C == col // HG_C) & (col <= row), 1.0, 0.0).astype(BF16)
    g = jnp.zeros(logf.shape, F32)
    rem = logf
    for _ in range(3):
        part = rem.astype(BF16)
        g = g + jnp.dot(tri, part, preferred_element_type=F32)
        rem = rem - part.astype(F32)
    g_s[...] = g
    q_s[...] = _silu(aq_ref[...])
    k_s[...] = 1.0 - f
    v_s[...] = ai_ref[...]

    sub = lax.broadcasted_iota(jnp.int32, (HG_C, A_DK), 0)
    sub8 = lax.broadcasted_iota(jnp.int32, (HG_H, A_DK), 0)
    ng = ng_ref[...]

    def chunk(c, carry):
        r0 = pl.multiple_of(c * HG_C, HG_C)
        rows = pl.ds(r0, HG_C)
        for h in range(A_HEADS):
            cols = slice(h * A_DK, (h + 1) * A_DK)
            g = g_s[rows, cols]
            qh = q_s[rows, cols]
            kh = k_s[rows, cols]
            vh = v_s[rows, cols]
            halves = []
            for hb in range(2):
                rs = slice(hb * HG_H, (hb + 1) * HG_H)
                gb, qb, kb, vb = g[rs], qh[rs], kh[rs], vh[rs]
                ob = jnp.zeros((HG_H, A_DV), F32)
                for tt in range(HG_H):
                    d = gb[tt:tt + 1, :] - gb
                    e = jnp.exp(jnp.where(sub8 <= tt, d, -jnp.inf))
                    p = e * (qb[tt:tt + 1, :] * kb)
                    a_col = jnp.sum(p, axis=1, keepdims=True)
                    o_row = jnp.sum(a_col * vb, axis=0, keepdims=True)
                    ob = jnp.where(sub8 == tt, o_row, ob)
                halves.append(ob)
            o = jnp.concatenate(halves, axis=0)
            low = sub < HG_H
            g_mid = g[HG_H - 1:HG_H, :]
            q_hi = jnp.where(low, 0.0, qh * jnp.exp(jnp.minimum(g - g_mid, 0.0)))
            k_lo = jnp.where(low, kh * jnp.exp(jnp.minimum(g_mid - g, 0.0)), 0.0)
            st = st_ref[h]
            g_last = g[HG_C - 1:HG_C, :]
            kt = kh * jnp.exp(g_last - g)
            upd = lax.dot_general(vh.astype(BF16), jnp.concatenate([kt, k_lo], axis=1).astype(BF16),
                                  (((0,), (0,)), ((), ())), preferred_element_type=F32)
            lhs = jnp.concatenate([qh * jnp.exp(g), q_hi], axis=1).astype(BF16)
            rhs = jnp.concatenate([st, upd[:, A_DK:]], axis=1).astype(BF16)
            o = o + _nt(lhs, rhs)
            st_ref[h] = st * jnp.exp(g_last) + upd[:, :A_DK]
            on = o * lax.rsqrt(jnp.mean(o * o, axis=-1, keepdims=True) + EPS) * ng
            ya_ref[rows, cols] = (on * _silu(ag_ref[rows, cols])).astype(ya_ref.dtype)
        return carry

    lax.fori_loop(0, HG_TB // HG_C, chunk, 0, unroll=2)

    @pl.when(t == pl.num_programs(0) - 1)
    def _():
        st_out_ref[...] = st_ref[...]


def hgrn_prompt(z, lb_logits, norm_g):
    m = z.shape[0]
    w = A_HEADS * A_DK

    def zspec(cb):
        return pl.BlockSpec((HG_TB, w), lambda t, cb=cb: (t, cb))

    return pl.pallas_call(
        _hgrn_prompt_kernel,
        grid=(m // HG_TB,),
        in_specs=[zspec(C_AQ // w), zspec(C_AF // w), zspec(C_AI // w), zspec(C_AG // w),
                  pl.BlockSpec(lb_logits.shape, lambda t: (0, 0)),
                  pl.BlockSpec((1, A_DV), lambda t: (0, 0))],
        out_specs=[pl.BlockSpec((HG_TB, w), lambda t: (t, 0)),
                   pl.BlockSpec((A_HEADS, A_DV, A_DK), lambda t: (0, 0, 0))],
        out_shape=[jax.ShapeDtypeStruct((m, w), BF16),
                   jax.ShapeDtypeStruct((A_HEADS, A_DV, A_DK), F32)],
        scratch_shapes=[pltpu.VMEM((A_HEADS, A_DV, A_DK), F32)] + [pltpu.VMEM((HG_TB, w), F32)] * 4,
        compiler_params=_cparams("arbitrary"),
        name="hgrn_prompt",
    )(z, z, z, z, lb_logits, norm_g.reshape(1, A_DV))


def _hgrn_step_kernel(z_ref, lbl_ref, ng_ref, s_ref, ya_ref, s_out_ref):
    lbl = lbl_ref[...]
    mx = jnp.max(lbl, axis=0, keepdims=True)
    ex = jnp.exp(lbl - mx)
    lb = ex[0:1, :] / jnp.sum(ex, axis=0, keepdims=True)
    z = z_ref[0]
    w = A_HEADS * A_DK
    q = _silu(z[:, 0:w])
    f = lb + (1.0 - lb) * jax.nn.sigmoid(z[:, w:2 * w])
    kk = 1.0 - f
    v = z[:, 2 * w:3 * w]
    ag = z[:, 3 * w:4 * w]
    rows = []
    for h in range(A_HEADS):
        cols = slice(h * A_DK, (h + 1) * A_DK)
        rows += [f[:, cols], kk[:, cols], q[:, cols]]
    rows.append(jnp.zeros((LANES - 3 * A_HEADS, A_DK), F32))
    xt = jnp.concatenate(rows, axis=0).T
    ng = ng_ref[...]
    r16 = lambda a: a.astype(BF16).astype(F32)
    outs = []
    for h in range(A_HEADS):
        cols = slice(h * A_DV, (h + 1) * A_DV)
        fcol = xt[:, 3 * h:3 * h + 1]
        kcol = xt[:, 3 * h + 1:3 * h + 2]
        qcol = xt[:, 3 * h + 2:3 * h + 3]
        s_old = s_ref[0, h]
        s_out_ref[0, h] = fcol * s_old + kcol * v[:, cols]
        o = (jnp.sum(r16(qcol * fcol) * r16(s_old), axis=0, keepdims=True)
             + jnp.sum(qcol * kcol, axis=0, keepdims=True) * v[:, cols])
        on = o * lax.rsqrt(jnp.mean(o * o, axis=-1, keepdims=True) + EPS) * ng
        outs.append(on * _silu(ag[:, cols]))
    ya_ref[0] = jnp.concatenate(outs, axis=1).astype(ya_ref.dtype)


def hgrn_step(z4, lb_logits, norm_g, state):
    b = z4.shape[0]
    w = A_HEADS * A_DK
    return pl.pallas_call(
        _hgrn_step_kernel,
        grid=(b,),
        in_specs=[pl.BlockSpec((1, 1, 4 * w), lambda i: (i, 0, 0)),
                  pl.BlockSpec(lb_logits.shape, lambda i: (0, 0)),
                  pl.BlockSpec((1, A_DV), lambda i: (0, 0)),
                  pl.BlockSpec((1, A_HEADS, A_DK, A_DV), lambda i: (i, 0, 0, 0))],
        out_specs=[pl.BlockSpec((1, 1, w), lambda i: (i, 0, 0)),
                   pl.BlockSpec((1, A_HEADS, A_DK, A_DV), lambda i: (i, 0, 0, 0))],
        out_shape=[jax.ShapeDtypeStruct((b, 1, w), F32),
                   jax.ShapeDtypeStruct(state.shape, F32)],
        compiler_params=_cparams("arbitrary"),
        name="hgrn_step",
    )(z4.reshape(b, 1, 4 * w), lb_logits, norm_g.reshape(1, A_DV), state)


BISECT_MAX_ITERS = 48


def _bisect_threshold(count_ge, lo, hi, cnt_lo, topk):
    kf = float(topk)

    def cond(c):
        return jnp.logical_and(c[0] < BISECT_MAX_ITERS, c[-1] > 0.0)

    def body(c):
        it, lo, hi, cl, _ = c
        mid = 0.5 * lo + 0.5 * hi
        cm = count_ge(mid)
        ge = cm >= kf
        lo = jnp.where(ge, mid, lo)
        cl = jnp.where(ge, cm, cl)
        hi = jnp.where(ge, hi, mid)
        busy = jnp.max(jnp.where(cl > kf, 1.0, 0.0))
        return it + 1, lo, hi, cl, busy

    busy0 = jnp.max(jnp.where(cnt_lo > kf, 1.0, 0.0))
    out = lax.while_loop(cond, body, (jnp.int32(0), lo, hi, cnt_lo, busy0))
    return out[1]


DSA_QB = 128
DSA_W = 512
DSA_W3 = 1024
P3_GROUP = 2


def _dsa_prompt_kernel(iq0_ref, iq1_ref, bq_ref, iw_ref, kidx_ref, k_ref, v_ref, bias_ref, o_ref,
                       score_s, qih_s, qs_s, wb_s, m_s, l_s, acc_s, *, topk):
    i = pl.program_id(0)
    nsub = DSA_W // LANES
    nsub3 = DSA_W3 // LANES
    qsub = DSA_QB // LANES
    nch3 = (i * DSA_QB + DSA_QB + DSA_W3 - 1) // DSA_W3
    nch = (i * DSA_QB + DSA_QB + DSA_W - 1) // DSA_W
    qpos = i * DSA_QB + lax.broadcasted_iota(jnp.int32, (DSA_QB, 1), 0)

    iw = iw_ref[...]
    wscale = IDX_DIM ** -0.5 * IDX_HEADS ** -0.5
    for h in range(IDX_HEADS):
        iq_ref, hh = (iq0_ref, h) if h < IDX_HEADS // 2 else (iq1_ref, h - IDX_HEADS // 2)
        qih_s[h] = iq_ref[:, hh * IDX_DIM:(hh + 1) * IDX_DIM].astype(BF16)
        wb_s[h] = jnp.broadcast_to(iw[:, IW_LANE + h:IW_LANE + h + 1] * wscale, (DSA_QB, LANES))
    for h in range(B_HEADS):
        qs_s[h // B_GROUP, (h % B_GROUP) * DSA_QB:(h % B_GROUP + 1) * DSA_QB, :] = (
            bq_ref[:, h * B_HEAD_DIM:(h + 1) * B_HEAD_DIM] * B_HEAD_DIM ** -0.5).astype(BF16)

    def p1(c, carry):
        c0 = pl.multiple_of(c * DSA_W, DSA_W)
        kc = kidx_ref[pl.ds(c0, DSA_W), :]
        sc = [jnp.zeros((DSA_QB, LANES), F32) for _ in range(nsub)]
        for h in range(IDX_HEADS):
            s = jnp.maximum(_nt(qih_s[h], kc), 0.0)
            wb = wb_s[h]
            for j in range(nsub):
                sc[j] = sc[j] + s[:, j * LANES:(j + 1) * LANES] * wb
        for j in range(nsub):
            kpos = c0 + j * LANES + lax.broadcasted_iota(jnp.int32, (1, LANES), 1)
            score_s[c * nsub + j] = jnp.where(kpos <= qpos, sc[j], -jnp.inf)
        return carry

    lax.fori_loop(0, nch // 2, lambda c2, carry: p1(2 * c2 + 1, p1(2 * c2, carry)), 0)
    lax.fori_loop(nch // 2 * 2, nch, p1, 0)

    def fill(tile, carry):
        score_s[tile] = jnp.full((DSA_QB, LANES), -jnp.inf, F32)
        return carry

    lax.fori_loop(nch * nsub, nch3 * nsub3, fill, 0)

    def stats(c, carry):
        mn, mx = carry
        for j in range(nsub):
            s = score_s[c * nsub + j]
            mx = jnp.maximum(mx, s)
            mn = jnp.minimum(mn, jnp.where(s > -jnp.inf, s, jnp.inf))
        return mn, mx

    mn, mx = lax.fori_loop(0, nch, stats, (jnp.full((DSA_QB, LANES), jnp.inf, F32),
                                           jnp.full((DSA_QB, LANES), -jnp.inf, F32)))
    lo0 = jnp.min(mn, axis=1, keepdims=True)
    hi0 = jnp.max(mx, axis=1, keepdims=True)

    def count_ge(thr):
        thr_b = jnp.broadcast_to(thr, (DSA_QB, LANES))

        def body(c, acc):
            for j in range(nsub):
                s = score_s[c * nsub + j]
                acc = acc + jnp.where(s >= thr_b, 1.0, 0.0)
            return acc

        acc = lax.fori_loop(0, nch, body, jnp.zeros((DSA_QB, LANES), F32))
        return jnp.sum(acc, axis=1, keepdims=True)

    thr = _bisect_threshold(count_ge, lo0, hi0, (qpos + 1).astype(F32), topk)
    thr_b = jnp.broadcast_to(thr, (DSA_QB, LANES))

    m_s[...] = jnp.full(m_s.shape, NEG, F32)
    l_s[...] = jnp.zeros(l_s.shape, F32)
    acc_s[...] = jnp.zeros(acc_s.shape, F32)

    def p3(c, with_bias):
        c0 = pl.multiple_of(c * DSA_W3, DSA_W3)
        madd = jnp.concatenate([jnp.where(score_s[c * nsub3 + j] >= thr_b, 0.0, NEG) for j in range(nsub3)], axis=1)
        kc = k_ref[pl.ds(c0, DSA_W3), :]
        vc = v_ref[pl.ds(c0, DSA_W3), :]
        rel = lambda qs, j: i * qsub + qs - (c * nsub3 + j)

        def bias_tile(h, qs, j):
            return jnp.where(rel(qs, j) == 0, bias_ref[h, 0], jnp.where(rel(qs, j) == 1, bias_ref[h, 1], 0.0))

        def scores(n):
            return _nt(qs_s[n], kc[:, n * B_HEAD_DIM:(n + 1) * B_HEAD_DIM])

        def softmax(n, lg):
            lg = lg.reshape(B_GROUP, DSA_QB, DSA_W3) + madd[None]
            if with_bias:
                lg = lg + jnp.stack([jnp.concatenate(
                    [jnp.concatenate([bias_tile(n * B_GROUP + gq, qs, j) for j in range(nsub3)], axis=1)
                     for qs in range(qsub)], axis=0) for gq in range(B_GROUP)])
            m_old = m_s[n]
            m_new = jnp.maximum(m_old, jnp.max(lg, axis=-1, keepdims=True))
            p = jnp.exp(lg - m_new)
            alpha = jnp.exp(m_old - m_new)
            l_s[n] = alpha * l_s[n] + jnp.sum(p, axis=-1, keepdims=True)
            m_s[n] = m_new
            pv = jnp.dot(p.reshape(B_GROUP * DSA_QB, DSA_W3).astype(BF16), vc[:, n * B_HEAD_DIM:(n + 1) * B_HEAD_DIM],
                         preferred_element_type=F32)
            return alpha, pv.reshape(B_GROUP, DSA_QB, B_HEAD_DIM)

        lgs = [scores(n) for n in range(B_KV_HEADS)]
        outs = [softmax(n, lgs[n]) for n in range(B_KV_HEADS)]
        for n in range(B_KV_HEADS):
            acc_s[n] = outs[n][0] * acc_s[n] + outs[n][1]

    n_far = jnp.maximum(i * qsub - 1, 0) // nsub3
    def far_group(cg, carry):
        for u in range(P3_GROUP):
            p3(P3_GROUP * cg + u, False)
        return carry

    lax.fori_loop(0, n_far // P3_GROUP, far_group, 0)
    lax.fori_loop(n_far // P3_GROUP * P3_GROUP, n_far, lambda c, carry: (p3(c, False), carry)[1], 0)
    lax.fori_loop(n_far, nch3, lambda c, carry: (p3(c, True), carry)[1], 0)

    for h in range(B_HEADS):
        n, gq = h // B_GROUP, h % B_GROUP
        o_ref[:, h * B_HEAD_DIM:(h + 1) * B_HEAD_DIM] = (acc_s[n, gq] / l_s[n, gq]).astype(o_ref.dtype)


def dsa_prompt(z, ztail, kidx_bf, k_bf, v_bf, bias_tiles):
    m = z.shape[0]
    topk = min(TOPK_MAX, m // 4)
    wq = B_HEADS * B_HEAD_DIM
    wi2 = IDX_HEADS * IDX_DIM // 2
    kern = functools.partial(_dsa_prompt_kernel, topk=topk)
    return pl.pallas_call(
        kern,
        grid=(m // DSA_QB,),
        in_specs=[pl.BlockSpec((DSA_QB, wi2), lambda i: (i, C_IQ // wi2)),
                  pl.BlockSpec((DSA_QB, wi2), lambda i: (i, C_IQ // wi2 + 1)),
                  pl.BlockSpec((DSA_QB, wq), lambda i: (i, C_BQ // wq)),
                  pl.BlockSpec((DSA_QB, LANES), lambda i: (i, T_SM // LANES)),
                  pl.BlockSpec(kidx_bf.shape, lambda i: (0, 0)),
                  pl.BlockSpec(k_bf.shape, lambda i: (0, 0)),
                  pl.BlockSpec(v_bf.shape, lambda i: (0, 0)),
                  pl.BlockSpec(bias_tiles.shape, lambda i: (0, 0, 0, 0))],
        out_specs=pl.BlockSpec((DSA_QB, wq), lambda i: (i, 0)),
        out_shape=jax.ShapeDtypeStruct((m, wq), BF16),
        scratch_shapes=[pltpu.VMEM((m // LANES, DSA_QB, LANES), F32),
                        pltpu.VMEM((IDX_HEADS, DSA_QB, IDX_DIM), BF16),
                        pltpu.VMEM((B_KV_HEADS, B_GROUP * DSA_QB, B_HEAD_DIM), BF16),
                        pltpu.VMEM((IDX_HEADS, DSA_QB, LANES), F32),
                        pltpu.VMEM((B_KV_HEADS, B_GROUP, DSA_QB, 1), F32),
                        pltpu.VMEM((B_KV_HEADS, B_GROUP, DSA_QB, 1), F32),
                        pltpu.VMEM((B_KV_HEADS, B_GROUP, DSA_QB, B_HEAD_DIM), F32)],
        compiler_params=_cparams("arbitrary"),
        name="dsa_prompt",
    )(z, z, z, ztail, kidx_bf, k_bf, v_bf, bias_tiles)


def _page_copies(table_ref, b, n_pages, src_hbm, dst, sem, rows_per_page=PAGE_SIZE):
    def copy(p):
        return pltpu.make_async_copy(src_hbm.at[table_ref[b, p]],
                                     dst.at[pl.ds(p * rows_per_page, rows_per_page)], sem)
    return copy


def _dsa_scores_kernel(pt_ref, iq_ref, iw_ref, iknew_ref, kidx_hbm, o_ref, buf, sem, *, n_pages):
    b = pl.program_id(0)
    nb = pl.num_programs(0)
    past = n_pages * PAGE_SIZE

    def page_copy(bb, slot, p):
        return pltpu.make_async_copy(kidx_hbm.at[pt_ref[bb, p]],
                                     buf.at[slot, :, pl.ds(pl.multiple_of(p * PAGE_SIZE, PAGE_SIZE), PAGE_SIZE)],
                                     sem.at[slot])

    def start(bb, slot):
        lax.fori_loop(0, n_pages, lambda p, c: (page_copy(bb, slot, p).start(), c)[1], 0)

    def wait(bb, slot):
        lax.fori_loop(0, n_pages, lambda p, c: (page_copy(bb, slot, p).wait(), c)[1], 0)

    slot = b % 2

    @pl.when(b == 0)
    def _():
        start(0, 0)

    @pl.when(b + 1 < nb)
    def _():
        start(b + 1, 1 - slot)

    wait(b, slot)

    r16 = lambda a: a.astype(BF16).astype(F32)
    qi = iq_ref[0].astype(BF16)
    wcol = r16(iw_ref[0]) * (IDX_DIM ** -0.5 * IDX_HEADS ** -0.5)
    s = r16(jnp.maximum(jnp.dot(qi, buf[slot].astype(BF16), preferred_element_type=F32), 0.0))
    o_ref[0, :, 0:past] = jnp.sum(s * wcol, axis=0, keepdims=True)
    sn = r16(jnp.maximum(jnp.dot(qi, iknew_ref[0].astype(BF16), preferred_element_type=F32), 0.0))
    sn = jnp.sum(sn * wcol, axis=0, keepdims=True)
    lane = lax.broadcasted_iota(jnp.int32, (1, LANES), 1)
    o_ref[0, :, past:past + LANES] = jnp.where(lane == 0, sn, -jnp.inf)


def dsa_scores(page_table, iq, iw, iknew_pad, cache_kidx):
    b, n_pages = page_table.shape
    past = n_pages * PAGE_SIZE
    kern = functools.partial(_dsa_scores_kernel, n_pages=n_pages)
    gs = pltpu.PrefetchScalarGridSpec(
        num_scalar_prefetch=1,
        grid=(b,),
        in_specs=[pl.BlockSpec((1, IDX_HEADS, IDX_DIM), lambda i, pt: (i, 0, 0)),
                  pl.BlockSpec((1, IDX_HEADS, 1), lambda i, pt: (i, 0, 0)),
                  pl.BlockSpec((1, IDX_DIM, LANES), lambda i, pt: (i, 0, 0)),
                  pl.BlockSpec(memory_space=pl.ANY)],
        out_specs=pl.BlockSpec((1, 1, past + LANES), lambda i, pt: (i, 0, 0)),
        scratch_shapes=[pltpu.VMEM((2, IDX_DIM, past), F32), pltpu.SemaphoreType.DMA((2,))],
    )
    return pl.pallas_call(
        kern, grid_spec=gs,
        out_shape=jax.ShapeDtypeStruct((b, 1, past + LANES), F32),
        compiler_params=_cparams("arbitrary"),
        name="dsa_scores",
    )(page_table, iq, iw, iknew_pad, cache_kidx)


def _dsa_threshold_kernel(s_ref, thr_ref, *, topk):
    s = s_ref[...]
    nb = s.shape[0]
    finite = s > -jnp.inf
    lo0 = jnp.min(jnp.where(finite, s, jnp.inf), axis=1, keepdims=True)
    hi0 = jnp.max(s, axis=1, keepdims=True)
    cnt0 = jnp.sum(jnp.where(finite, 1.0, 0.0), axis=1, keepdims=True)

    def count_ge(thr):
        return jnp.sum(jnp.where(s_ref[...] >= thr, 1.0, 0.0), axis=1, keepdims=True)

    thr = _bisect_threshold(count_ge, lo0, hi0, cnt0, topk)
    thr_ref[...] = jnp.broadcast_to(thr, (nb, LANES))


def dsa_threshold(scores, topk):
    b, l = scores.shape
    return pl.pallas_call(
        functools.partial(_dsa_threshold_kernel, topk=topk),
        grid=(1,),
        in_specs=[pl.BlockSpec((b, l), lambda i: (0, 0))],
        out_specs=pl.BlockSpec((b, LANES), lambda i: (0, 0)),
        out_shape=jax.ShapeDtypeStruct((b, LANES), F32),
        compiler_params=_cparams("arbitrary"),
        name="dsa_threshold",
    )(scores)


def _dsa_decode_kernel(pt_ref, q_ref, s_ref, thr_ref, knew_ref, vnew_ref, bias_ref, k_hbm, v_hbm, o_ref,
                       kbuf, vbuf, sem, *, n_pages):
    b = pl.program_id(0)
    nb = pl.num_programs(0)
    past = n_pages * PAGE_SIZE

    rpp = PAGE_SIZE * B_KV_HEADS

    def copies(bb, slot):
        ck = _page_copies(pt_ref, bb, n_pages, k_hbm, kbuf.at[slot], sem.at[0, slot], rpp)
        cv = _page_copies(pt_ref, bb, n_pages, v_hbm, vbuf.at[slot], sem.at[1, slot], rpp)
        return ck, cv

    def start(bb, slot):
        ck, cv = copies(bb, slot)
        lax.fori_loop(0, n_pages, lambda p, c: (ck(p).start(), cv(p).start(), c)[2], 0)

    def wait(bb, slot):
        ck, cv = copies(bb, slot)
        lax.fori_loop(0, n_pages, lambda p, c: (ck(p).wait(), cv(p).wait(), c)[2], 0)

    slot = b % 2

    @pl.when(b == 0)
    def _():
        kbuf[:, n_pages * rpp:, :] = jnp.zeros((2, rpp, B_HEAD_DIM), F32)
        vbuf[:, n_pages * rpp:, :] = jnp.zeros((2, rpp, B_HEAD_DIM), F32)
        start(0, 0)

    @pl.when(b + 1 < nb)
    def _():
        start(b + 1, 1 - slot)

    kbuf[slot, n_pages * rpp:n_pages * rpp + 8, :] = knew_ref[0]
    vbuf[slot, n_pages * rpp:n_pages * rpp + 8, :] = vnew_ref[0]
    wait(b, slot)

    sel = s_ref[0] >= thr_ref[0][:, 0:1]
    n_keys = past + PAGE_SIZE
    outs = []
    for n in range(B_KV_HEADS):
        kn = kbuf[slot, pl.ds(n, n_keys, stride=B_KV_HEADS), :].astype(BF16)
        vn = vbuf[slot, pl.ds(n, n_keys, stride=B_KV_HEADS), :].astype(BF16)
        qn = q_ref[0, n].astype(BF16)
        lg = _nt(qn, kn) * B_HEAD_DIM ** -0.5 + bias_ref[n]
        m = jnp.max(jnp.where(sel, lg, NEG), axis=1, keepdims=True)
        p = jnp.where(sel, jnp.exp(lg - m), 0.0)
        p = p / jnp.sum(p, axis=1, keepdims=True)
        outs.append(jnp.dot(p.astype(BF16), vn, preferred_element_type=F32))
    o_ref[0] = jnp.concatenate(outs, axis=0).astype(o_ref.dtype)


def dsa_decode(page_table, q8, scores, thr, knew8, vnew8, bias_rows, cache_k2, cache_v2):
    b, n_pages = page_table.shape
    past = n_pages * PAGE_SIZE
    l = past + LANES
    wkv = B_KV_HEADS * B_HEAD_DIM
    kern = functools.partial(_dsa_decode_kernel, n_pages=n_pages)
    gs = pltpu.PrefetchScalarGridSpec(
        num_scalar_prefetch=1,
        grid=(b,),
        in_specs=[pl.BlockSpec((1, B_KV_HEADS, 8, B_HEAD_DIM), lambda i, pt: (i, 0, 0, 0)),
                  pl.BlockSpec((1, 1, l), lambda i, pt: (i, 0, 0)),
                  pl.BlockSpec((1, 1, LANES), lambda i, pt: (i, 0, 0)),
                  pl.BlockSpec((1, 8, B_HEAD_DIM), lambda i, pt: (i, 0, 0)),
                  pl.BlockSpec((1, 8, B_HEAD_DIM), lambda i, pt: (i, 0, 0)),
                  pl.BlockSpec((B_KV_HEADS, 8, l), lambda i, pt: (0, 0, 0)),
                  pl.BlockSpec(memory_space=pl.ANY),
                  pl.BlockSpec(memory_space=pl.ANY)],
        out_specs=pl.BlockSpec((1, 2 * 8, B_HEAD_DIM), lambda i, pt: (i, 0, 0)),
        scratch_shapes=[pltpu.VMEM((2, l * B_KV_HEADS, B_HEAD_DIM), F32),
                        pltpu.VMEM((2, l * B_KV_HEADS, B_HEAD_DIM), F32),
                        pltpu.SemaphoreType.DMA((2, 2))],
    )
    return pl.pallas_call(
        kern, grid_spec=gs,
        out_shape=jax.ShapeDtypeStruct((b, 2 * 8, B_HEAD_DIM), F32),
        compiler_params=_cparams("arbitrary"),
        name="dsa_decode",
    )(page_table, q8, scores, thr, knew8, vnew8, bias_rows, cache_k2, cache_v2)


def _merge_kernel(ya_ref, yb_ref, ga_ref, gb_ref, wa_ref, wb_ref, o_ref):
    a = _mm(ya_ref[...], wa_ref[...])
    bb = _mm(yb_ref[...], wb_ref[...])
    o_ref[...] = (jax.nn.sigmoid(ga_ref[...]) * a + jax.nn.sigmoid(gb_ref[...]) * bb).astype(o_ref.dtype)


def merge(ya, yb, z, ga_col, gb_col, wa, wb, *, tm, tn):
    m, k = ya.shape
    n = wa.shape[1]
    return pl.pallas_call(
        _merge_kernel,
        grid=(m // tm, n // tn),
        in_specs=[pl.BlockSpec((tm, k), lambda i, j: (i, 0)),
                  pl.BlockSpec((tm, k), lambda i, j: (i, 0)),
                  pl.BlockSpec((tm, tn), lambda i, j: (i, ga_col // tn + j)),
                  pl.BlockSpec((tm, tn), lambda i, j: (i, gb_col // tn + j)),
                  pl.BlockSpec((k, tn), lambda i, j: (0, j)),
                  pl.BlockSpec((k, tn), lambda i, j: (0, j))],
        out_specs=pl.BlockSpec((tm, tn), lambda i, j: (i, j)),
        out_shape=jax.ShapeDtypeStruct((m, n), wa.dtype),
        compiler_params=_cparams("parallel", "arbitrary"),
        name="merge",
    )(ya, yb, z, z, wa, wb)


def _cross_prompt_kernel(x_ref, g_ref, wq_ref, mk_ref, mv_ref, wo_ref, o_ref, *, n_in_tiles):
    x = x_ref[...]
    ms = jnp.mean(x * x, axis=-1, keepdims=True)
    h = (x * lax.rsqrt(ms + EPS) * g_ref[...]).astype(BF16)
    q = jnp.dot(h, wq_ref[...], preferred_element_type=F32)
    outs = []
    for hh in range(X_HEADS):
        cols = slice(hh * X_HEAD_DIM, (hh + 1) * X_HEAD_DIM)
        lg = _nt(q[:, cols].astype(BF16), mk_ref[:, cols]) * X_HEAD_DIM ** -0.5
        mx = jnp.max(lg, axis=1, keepdims=True)
        p = jnp.exp(lg - mx)
        p = p / jnp.sum(p, axis=1, keepdims=True)
        outs.append(jnp.dot(p.astype(BF16), mv_ref[:, cols], preferred_element_type=F32).astype(BF16))
    att = jnp.concatenate(outs, axis=1)
    res = x + jnp.dot(att, wo_ref[...], preferred_element_type=F32)
    o_ref[...] = jnp.where(pl.program_id(0) < n_in_tiles, res, 0.0)


def cross_prompt(x, g, wq, mk, mv, wo, *, tm, out_rows=None):
    m, d = x.shape
    out_rows = m if out_rows is None else out_rows
    n_in = m // tm
    full = lambda a: pl.BlockSpec(a.shape, lambda i: (0,) * a.ndim)
    g2 = g.reshape(1, d)
    return pl.pallas_call(
        functools.partial(_cross_prompt_kernel, n_in_tiles=n_in),
        grid=(pl.cdiv(out_rows, tm),),
        in_specs=[pl.BlockSpec((tm, d), lambda i: (jnp.minimum(i, n_in - 1), 0)),
                  full(g2), full(wq), full(mk), full(mv), full(wo)],
        out_specs=pl.BlockSpec((tm, d), lambda i: (i, 0)),
        out_shape=jax.ShapeDtypeStruct((out_rows, d), F32),
        compiler_params=_cparams("parallel"),
        name="cross_prompt",
    )(x, g2, wq, mk, mv, wo)


def _cross_step_kernel(q_ref, mk_ref, mv_ref, o_ref):
    r16 = lambda a: a.astype(BF16).astype(F32)
    q = r16(q_ref[0])
    outs = []
    for hh in range(X_HEADS):
        cols = slice(hh * X_HEAD_DIM, (hh + 1) * X_HEAD_DIM)
        kh = r16(mk_ref[0, :, cols])
        vh = r16(mv_ref[0, :, cols])
        lg = jnp.sum(kh * q[:, cols], axis=1, keepdims=True) * X_HEAD_DIM ** -0.5
        mx = jnp.max(lg, axis=0, keepdims=True)
        p = jnp.exp(lg - mx)
        p = r16(p / jnp.sum(p, axis=0, keepdims=True))
        outs.append(jnp.sum(p * vh, axis=0, keepdims=True))
    o_ref[0] = jnp.concatenate(outs, axis=1).astype(o_ref.dtype)


def cross_step(q, mk, mv):
    b, w = q.shape
    mem = mk.shape[1]
    return pl.pallas_call(
        _cross_step_kernel,
        grid=(b,),
        in_specs=[pl.BlockSpec((1, 1, w), lambda i: (i, 0, 0)),
                  pl.BlockSpec((1, mem, w), lambda i: (i, 0, 0)),
                  pl.BlockSpec((1, mem, w), lambda i: (i, 0, 0))],
        out_specs=pl.BlockSpec((1, 1, w), lambda i: (i, 0, 0)),
        out_shape=jax.ShapeDtypeStruct((b, 1, w), F32),
        compiler_params=_cparams("arbitrary"),
        name="cross_step",
    )(q.reshape(b, 1, w), mk, mv)


def _pack_bf16_pairs(x):
    c = x.shape[1] // 2
    u = pltpu.bitcast(x.astype(BF16).astype(F32), jnp.uint32)
    return lax.shift_right_logical(u[:, :c], jnp.uint32(16)) | u[:, c:]


def _unpack_bf16_pairs(u):
    lo = pltpu.bitcast(lax.shift_left(u, jnp.uint32(16)), F32)
    hi = pltpu.bitcast(u & jnp.uint32(0xFFFF0000), F32)
    return jnp.concatenate([lo, hi], axis=1)


def _router_kernel(x_ref, g_ref, w_ref, b_ref, hf_ref, route_ref):
    x = x_ref[...]
    ms = jnp.mean(x * x, axis=-1, keepdims=True)
    hf = x * lax.rsqrt(ms + EPS) * g_ref[...]
    hf_ref[...] = _pack_bf16_pairs(hf)
    lg = _mm(hf, w_ref[...]) + b_ref[...]
    tm = lg.shape[0]
    lane = lax.broadcasted_iota(jnp.int32, (tm, LANES), 1)
    big = jnp.int32(LANES)
    is_g = lane < N_GROUPS
    gmax = jnp.max(jnp.where(is_g, lg, -jnp.inf), axis=1, keepdims=True)
    grp = jnp.min(jnp.where(is_g & (lg == gmax), lane, big), axis=1, keepdims=True)
    p_grp = 1.0 / jnp.sum(jnp.where(is_g, jnp.exp(lg - gmax), 0.0), axis=1, keepdims=True)
    e_lo = N_GROUPS + grp * EXP_PER_GROUP
    in_g = (lane >= e_lo) & (lane < e_lo + EXP_PER_GROUP)
    v1 = jnp.max(jnp.where(in_g, lg, -jnp.inf), axis=1, keepdims=True)
    i1 = jnp.min(jnp.where(in_g & (lg == v1), lane, big), axis=1, keepdims=True)
    rest = in_g & (lane != i1)
    v2 = jnp.max(jnp.where(rest, lg, -jnp.inf), axis=1, keepdims=True)
    i2 = jnp.min(jnp.where(rest & (lg == v2), lane, big), axis=1, keepdims=True)
    e2 = jnp.exp(v2 - v1)
    g1 = p_grp / (1.0 + e2)
    g2 = p_grp * e2 / (1.0 + e2)
    r = jnp.where(lane == 0, (i1 - N_GROUPS).astype(F32),
                  jnp.where(lane == 1, (i2 - N_GROUPS).astype(F32),
                            jnp.where(lane == 2, g1, jnp.where(lane == 3, g2, 0.0))))
    route_ref[...] = r


def router(x, g, w_pad, b_pad, *, tm):
    m, d = x.shape
    return pl.pallas_call(
        _router_kernel,
        grid=(pl.cdiv(m, tm),),
        in_specs=[pl.BlockSpec((tm, d), lambda i: (i, 0)),
                  pl.BlockSpec((1, d), lambda i: (0, 0)),
                  pl.BlockSpec((d, LANES), lambda i: (0, 0)),
                  pl.BlockSpec((1, LANES), lambda i: (0, 0))],
        out_specs=[pl.BlockSpec((tm, d // 2), lambda i: (i, 0)),
                   pl.BlockSpec((tm, LANES), lambda i: (i, 0))],
        out_shape=[jax.ShapeDtypeStruct((m, d // 2), jnp.uint32), jax.ShapeDtypeStruct((m, LANES), F32)],
        compiler_params=_cparams("parallel"),
        name="router",
    )(x, g.reshape(1, d), w_pad, b_pad)


def _moe_kernel(be_ref, nxt_ref, nused_ref, sbase_ref, nvalid_ref, padj_ref, order_ref, hf_hbm, wg_hbm, wu_hbm,
                wd_hbm, y_hbm, xbuf, obuf, wg_f, wu_f, wd_f, wg_s, wu_s, wd_s, sem_in, sem_out, sem_w,
                *, n_tokens):
    b = pl.program_id(0)
    nb = pl.num_programs(0)
    blk = xbuf.shape[1]
    xslot = b % 3
    used = b < nused_ref[0]
    next_used = b + 2 < nused_ref[0]
    n_assign = order_ref.shape[0]

    def slot_info(bb):
        base, nv, pad0 = sbase_ref[bb], nvalid_ref[bb], padj_ref[bb]

        def info(r):
            valid = r < nv
            asg = order_ref[jnp.minimum(base + r, n_assign - 1)]
            tok = lax.shift_right_logical(asg, 1)
            return jnp.where(valid, tok, 0), jnp.where(valid, (asg & 1) * n_tokens + tok, pad0 + r)
        return info

    def weight_copies(e):
        return (pltpu.make_async_copy(wg_hbm.at[e], wg_f, sem_w.at[0]),
                pltpu.make_async_copy(wu_hbm.at[e], wu_f, sem_w.at[1]),
                pltpu.make_async_copy(wd_hbm.at[e], wd_f, sem_w.at[2]))

    def gather_start(bb, sl):
        info = slot_info(bb)
        for r in range(blk):
            pltpu.make_async_copy(hf_hbm.at[pl.ds(info(r)[0], 1)], xbuf.at[sl, pl.ds(r, 1)],
                                  sem_in.at[sl]).start()

    def scatter_start(bb, sl):
        info = slot_info(bb)
        for r in range(blk):
            pltpu.make_async_copy(obuf.at[sl, pl.ds(r, 1)], y_hbm.at[pl.ds(info(r)[1], 1)],
                                  sem_out.at[sl]).start(priority=r % 2)

    def gather_wait(sl):
        pltpu.make_async_copy(hf_hbm.at[pl.ds(0, blk)], xbuf.at[sl], sem_in.at[sl]).wait()

    def scatter_wait(sl):
        pltpu.make_async_copy(obuf.at[sl], y_hbm.at[pl.ds(0, blk)], sem_out.at[sl]).wait()

    @pl.when(b == 0)
    def _():
        for cp in weight_copies(be_ref[0]):
            cp.start(priority=1)
        gather_start(0, 0)
        obuf[...] = jnp.zeros(obuf.shape, obuf.dtype)

    @pl.when(jnp.logical_and(b == 0, 1 < nused_ref[0]))
    def _():
        gather_start(1, 1)

    changed = jnp.logical_and(used, jnp.logical_or(b == 0, be_ref[b] != be_ref[jnp.maximum(b - 1, 0)]))

    @pl.when(changed)
    def _():
        for cp in weight_copies(be_ref[b]):
            cp.wait()
        wg_s[...] = wg_f[...].astype(BF16)
        wu_s[...] = wu_f[...].astype(BF16)
        wd_s[...] = wd_f[...].astype(BF16)

    @pl.when(jnp.logical_and(changed, nxt_ref[b] >= 0))
    def _():
        for cp in weight_copies(nxt_ref[b]):
            cp.start(priority=1)

    @pl.when(used)
    def _():
        gather_wait(xslot)

    @pl.when(b >= 3)
    def _():
        scatter_wait(xslot)

    def step(prefetch, flush_prev, compute):
        if prefetch:
            gather_start(b + 2, (b + 2) % 3)
        if flush_prev:
            scatter_start(b - 1, (b + 2) % 3)
        if compute:
            x = _unpack_bf16_pairs(xbuf[xslot]).astype(BF16)
            gg = jnp.dot(x, wg_s[...], preferred_element_type=F32)
            uu = jnp.dot(x, wu_s[...], preferred_element_type=F32)
            a = (_silu(gg) * uu).astype(BF16)
            obuf[xslot] = _pack_bf16_pairs(jnp.dot(a, wd_s[...], preferred_element_type=F32))

    first, last = b == 0, b == nb - 1
    land, lnot = jnp.logical_and, jnp.logical_not
    pl.when(land(first, next_used))(lambda: step(True, False, True))
    pl.when(land(first, lnot(next_used)))(lambda: step(False, False, True))
    pl.when(land(lnot(first), land(used, next_used)))(lambda: step(True, True, True))
    pl.when(land(lnot(first), land(used, lnot(next_used))))(lambda: step(False, True, True))
    pl.when(land(lnot(first), lnot(used)))(lambda: step(False, True, False))

    @pl.when(last)
    def _():
        scatter_start(b, xslot)
        scatter_wait(xslot)

    @pl.when(jnp.logical_and(last, b >= 1))
    def _():
        scatter_wait((b + 2) % 3)

    @pl.when(jnp.logical_and(last, b >= 2))
    def _():
        scatter_wait((b + 1) % 3)


def moe_experts(tables, hf, w_g, w_u, w_d, *, blk, out_rows):
    n_blocks = tables[0].shape[0]
    dp = hf.shape[1]
    d = 2 * dp
    ff = w_g.shape[2]
    gs = pltpu.PrefetchScalarGridSpec(
        num_scalar_prefetch=len(tables),
        grid=(n_blocks,),
        in_specs=[pl.BlockSpec(memory_space=pl.ANY)] * 4,
        out_specs=pl.BlockSpec(memory_space=pl.ANY),
        scratch_shapes=[pltpu.VMEM((3, blk, dp), jnp.uint32), pltpu.VMEM((3, blk, dp), jnp.uint32),
                        pltpu.VMEM((d, ff), F32), pltpu.VMEM((d, ff), F32), pltpu.VMEM((ff, d), F32),
                        pltpu.VMEM((d, ff), BF16), pltpu.VMEM((d, ff), BF16), pltpu.VMEM((ff, d), BF16),
                        pltpu.SemaphoreType.DMA((3,)), pltpu.SemaphoreType.DMA((3,)),
                        pltpu.SemaphoreType.DMA((3,))],
    )
    return pl.pallas_call(
        functools.partial(_moe_kernel, n_tokens=hf.shape[0]), grid_spec=gs,
        out_shape=jax.ShapeDtypeStruct((out_rows, dp), jnp.uint32),
        compiler_params=_cparams("arbitrary"),
        name="moe_experts",
    )(*tables, hf, w_g, w_u, w_d)


def _combine_kernel(x_ref, route_ref, gf_ref, y1_ref, y2_ref, op_ref, os_ref):
    i = pl.program_id(0)
    route = route_ref[...]
    x = (x_ref[...] + route[:, 2:3] * _unpack_bf16_pairs(y1_ref[...])
         + route[:, 3:4] * _unpack_bf16_pairs(y2_ref[...]))
    ms = jnp.mean(x * x, axis=-1, keepdims=True)
    out = x * lax.rsqrt(ms + EPS) * gf_ref[...]

    @pl.when(i < pl.num_programs(0) - 1)
    def _():
        op_ref[...] = out

    @pl.when(i == pl.num_programs(0) - 1)
    def _():
        os_ref[...] = out[:os_ref.shape[0]]


def combine(x, route, gf, y, plane, n_prompt, *, tm):
    m, d = x.shape
    n_tiles = n_prompt // tm
    assert n_prompt % tm == 0 and 0 < m - n_prompt <= tm
    return pl.pallas_call(
        _combine_kernel,
        grid=(n_tiles + 1,),
        in_specs=[pl.BlockSpec((tm, d), lambda i: (i, 0)),
                  pl.BlockSpec((tm, LANES), lambda i: (i, 0)),
                  pl.BlockSpec((1, d), lambda i: (0, 0)),
                  pl.BlockSpec((tm, d // 2), lambda i: (i, 0)),
                  pl.BlockSpec((tm, d // 2), lambda i: (plane // tm + i, 0))],
        out_specs=[pl.BlockSpec((tm, d), lambda i: (jnp.minimum(i, n_tiles - 1), 0)),
                   pl.BlockSpec((m - n_prompt, d), lambda i: (0, 0))],
        out_shape=[jax.ShapeDtypeStruct((n_prompt, d), F32), jax.ShapeDtypeStruct((m - n_prompt, d), F32)],
        compiler_params=_cparams("arbitrary"),
        name="combine",
    )(x, route, gf.reshape(1, d), y, y)


def _t5_bucket(dist):
    dist = jnp.asarray(dist, jnp.int32)
    max_exact = REL_BUCKETS // 2
    dist_f = jnp.maximum(dist, 1).astype(F32)
    large = max_exact + (jnp.log(dist_f / max_exact) / math.log(REL_MAX_DIST / max_exact)
                         * (REL_BUCKETS - max_exact)).astype(jnp.int32)
    large = jnp.minimum(large, REL_BUCKETS - 1)
    return jnp.where(dist < max_exact, dist, large)


def _bias_tables(rel_bias, past):
    r = np.arange(LANES)
    diff = r[:, None] - r[None, :]
    buckets = jnp.stack([_t5_bucket(np.maximum(diff, 0)),
                         _t5_bucket(np.maximum(diff + LANES, 0)),
                         _t5_bucket(np.full((LANES, LANES), 2 * LANES))])
    def lookup(bkt):
        oh = (bkt.reshape(-1, 1) == jnp.arange(REL_BUCKETS)[None, :]).astype(F32)
        out = jnp.dot(oh, rel_bias.astype(F32), precision=lax.Precision.HIGHEST)
        return out.T.reshape((rel_bias.shape[1],) + bkt.shape)

    tiles = lookup(buckets)
    tiles = tiles - tiles[:, 2:3]
    dist = np.maximum(past - np.arange(past + LANES), 0)
    rows = lookup(_t5_bucket(dist))
    rows = rows.reshape(B_KV_HEADS, B_GROUP, past + LANES)
    rows = jnp.concatenate([rows, jnp.zeros_like(rows)], axis=1)
    return tiles, rows


def _dispatch(eid, n_tokens, blk):
    a = eid.shape[0]
    assert EXPERT_TOPK == 2 and a == EXPERT_TOPK * n_tokens
    n_blocks = -(-(a + N_EXPERTS * (blk - 1)) // blk)
    rows = n_blocks * blk
    order = jnp.argsort(eid).astype(jnp.int32)
    counts = jnp.sum(eid[:, None] == jnp.arange(N_EXPERTS)[None, :], axis=0).astype(jnp.int32)
    cum = jnp.cumsum(counts)
    starts = cum - counts
    padded = (counts + blk - 1) // blk * blk
    pad_end = jnp.cumsum(padded)
    pad_start = pad_end - padded
    blocks = jnp.arange(n_blocks, dtype=jnp.int32)
    block_e = jnp.minimum(jnp.sum(pad_end[None, :] <= (blocks * blk)[:, None], axis=1), N_EXPERTS - 1)
    n_used = pad_end[-1] // blk
    off = blocks * blk - pad_start[block_e]
    sbase = starts[block_e] + off
    nvalid = jnp.where(blocks < n_used, jnp.clip(counts[block_e] - off, 0, blk), 0)
    padj = a + blocks * blk - cum[block_e]
    new_run = jnp.concatenate([jnp.array([True]), block_e[1:] != block_e[:-1]])
    run_start = jnp.where(new_run & (blocks < n_used), blocks, n_blocks)
    nxt = jnp.concatenate([jnp.flip(lax.cummin(jnp.flip(run_start)))[1:], jnp.array([n_blocks])])
    next_e = jnp.where(nxt < n_blocks, block_e[jnp.minimum(nxt, n_blocks - 1)], -1)
    i32 = lambda v: v.astype(jnp.int32)
    return (i32(block_e), i32(next_e), i32(n_used).reshape(1), i32(sbase), i32(nvalid), i32(padj), order), rows


def kernel(x_prompt, x_sample, mem_prompt, cache_k, cache_v, cache_kidx, page_table, state_hgrn, cache_mem_k,
           cache_mem_v, norm_mix, w_in, hgrn_lb_logits, hgrn_norm, w_branch_a, w_branch_b, w_out, norm_cross, w_xq,
           w_xk, w_xv, w_xo, norm_ffn, w_router_group, b_router_group, w_router_expert, b_router_expert, w_exp_gate,
           w_exp_up, w_exp_down, rel_bias, norm_final):
    assert w_in.shape[0] == 1, "single-layer step"
    l = 0
    drop0 = lambda a: a.reshape(a.shape[1:])
    bp, t, d = x_prompt.shape
    db = x_sample.shape[0]
    past = page_table.shape[1] * PAGE_SIZE
    xp = x_prompt.reshape(bp * t, d)
    xs = x_sample.reshape(db, d)

    wi_t = jnp.transpose(drop0(w_in)).astype(BF16)
    assert wi_t.shape[0] - NZ_MAIN + T_PAD == NZ_TAIL
    w_tail_t = jnp.pad(wi_t[NZ_MAIN:], ((T_PAD, 0), (0, 0)))
    wa, wb, wo = w_branch_a[l].astype(BF16), w_branch_b[l].astype(BF16), w_out[l].astype(BF16)
    wxq, wxk, wxv, wxo = (w_xq[l].astype(BF16), w_xk[l].astype(BF16), w_xv[l].astype(BF16), w_xo[l].astype(BF16))
    w_route = jnp.pad(jnp.concatenate([w_router_group[l], w_router_expert[l]], axis=1),
                      ((0, 0), (0, LANES - N_GROUPS - N_EXPERTS))).astype(BF16)
    b_route = jnp.pad(jnp.concatenate([b_router_group[l], b_router_expert[l]]),
                      (0, LANES - N_GROUPS - N_EXPERTS)).reshape(1, LANES)
    bias_tiles, bias_rows = _bias_tables(rel_bias, past)

    zp, zpt = in_proj(xp, norm_mix[l], wi_t, NZ_MAIN, w_tail_t, tm=TM_WIDE, tn=TN)
    kp = zp[:, C_BK:C_BK + KV_W]
    vp = zp[:, C_BV:C_BV + KV_W]
    ikp = zpt[:, T_SM + IK_LANE:T_SM + IK_LANE + IDX_DIM]
    ya_p, st_p = hgrn_prompt(zp, hgrn_lb_logits, hgrn_norm[l])
    yb_p = dsa_prompt(zp, zpt, ikp.astype(BF16), kp.astype(BF16), vp.astype(BF16), bias_tiles)
    mg_p = merge(ya_p, yb_p, zpt, T_GA, T_GB, wa, wb, tm=TM_WIDE, tn=TN)
    x1p = matmul(mg_p, wo, xp, tm=TM_WIDE, tn=TN)
    memp = mem_prompt.reshape(-1, d)
    mk = matmul(memp, wxk, tm=memp.shape[0], tn=TN)
    mv = matmul(memp, wxv, tm=memp.shape[0], tn=TN)
    n = -(-(bp * t + db) // TM_TOKEN) * TM_TOKEN
    x2p = cross_prompt(x1p, norm_cross[l], wxq, mk.astype(BF16), mv.astype(BF16), wxo, tm=TM_CROSS, out_rows=n)

    zs, zst = in_proj(xs, norm_mix[l], wi_t, NZ_MAIN, w_tail_t, tm=db, tn=TN)
    ks = zs[:, C_BK:C_BK + KV_W]
    vs = zs[:, C_BV:C_BV + KV_W]
    iks = zst[:, T_SM + IK_LANE:T_SM + IK_LANE + IDX_DIM]
    ya_s, st_s = hgrn_step(zs[:, :4 * A_HEADS * A_DK], hgrn_lb_logits, hgrn_norm[l], drop0(state_hgrn))
    iq_s = zs[:, C_IQ:C_IQ + IDX_HEADS * IDX_DIM].reshape(db, IDX_HEADS, IDX_DIM)
    iw_s = zst[:, T_SM + IW_LANE:T_SM + IW_LANE + IDX_HEADS].reshape(db, IDX_HEADS, 1)
    iknew_pad = jnp.pad(iks[:, :, None], ((0, 0), (0, 0), (0, LANES - 1)))
    scores = dsa_scores(page_table, iq_s, iw_s, iknew_pad,
                        jnp.swapaxes(drop0(cache_kidx), 1, 2)).reshape(db, past + LANES)
    topk_s = min(TOPK_MAX, (past + 1) // 4)
    thr = dsa_threshold(scores, topk_s)
    q8 = jnp.pad(zs[:, C_BQ:C_BQ + B_HEADS * B_HEAD_DIM].reshape(db, B_KV_HEADS, B_GROUP, B_HEAD_DIM),
                 ((0, 0), (0, 0), (0, 8 - B_GROUP), (0, 0)))
    knew8 = jnp.pad(ks.reshape(db, B_KV_HEADS, B_HEAD_DIM), ((0, 0), (0, 8 - B_KV_HEADS), (0, 0)))
    vnew8 = jnp.pad(vs.reshape(db, B_KV_HEADS, B_HEAD_DIM), ((0, 0), (0, 8 - B_KV_HEADS), (0, 0)))
    n_pool = cache_k.shape[1]
    ob = dsa_decode(page_table, q8, scores.reshape(db, 1, -1), thr.reshape(db, 1, LANES), knew8, vnew8, bias_rows,
                    cache_k.reshape(n_pool, PAGE_SIZE * B_KV_HEADS, B_HEAD_DIM),
                    cache_v.reshape(n_pool, PAGE_SIZE * B_KV_HEADS, B_HEAD_DIM))
    yb_s = ob.reshape(db, B_KV_HEADS, 8, B_HEAD_DIM)[:, :, :B_GROUP].reshape(db, B_HEADS * B_HEAD_DIM)
    mg_s = merge(ya_s.reshape(db, -1), yb_s, zst, T_GA, T_GB, wa, wb, tm=db, tn=TN)
    x1s = matmul(mg_s, wo, xs, tm=db, tn=TN)
    qx_s = norm_matmul(x1s, norm_cross[l], wxq, tm=db, tn=TN)
    mem = cache_mem_k.shape[2]
    att_s = cross_step(qx_s, cache_mem_k.reshape(db, mem, -1), cache_mem_v.reshape(db, mem, -1))
    x2 = matmul(att_s.reshape(db, -1), wxo, x1s, tm=db, tn=TN, into=(x2p, bp * t))

    hf, route = router(x2, norm_ffn[l], w_route, b_route, tm=TM_ROUTER)
    eid = route[:, :EXPERT_TOPK].astype(jnp.int32).reshape(-1)
    tables, rows = _dispatch(eid, n, MOE_ROWS)
    ye = moe_experts(tables, hf, drop0(w_exp_gate), drop0(w_exp_up), drop0(w_exp_down), blk=MOE_ROWS, out_rows=rows)
    y_p, y_s = combine(x2, route, norm_final, ye, n, bp * t, tm=TM_TOKEN)

    y_prompt = y_p.reshape(bp, t, d)
    y_sample = y_s[:db].reshape(db, 1, d)
    return (y_prompt, y_sample,
            kp.reshape(1, bp, t, B_KV_HEADS, B_HEAD_DIM), vp.reshape(1, bp, t, B_KV_HEADS, B_HEAD_DIM),
            ikp.reshape(1, bp, t, IDX_DIM),
            jnp.swapaxes(st_p, 1, 2).reshape(1, bp, A_HEADS, A_DK, A_DV),
            mk.reshape(1, bp, -1, X_HEADS, X_HEAD_DIM), mv.reshape(1, bp, -1, X_HEADS, X_HEAD_DIM),
            ks.reshape(1, db, 1, B_KV_HEADS, B_HEAD_DIM), vs.reshape(1, db, 1, B_KV_HEADS, B_HEAD_DIM),
            iks.reshape(1, db, 1, IDX_DIM),
            st_s.reshape(1, db, A_HEADS, A_DK, A_DV))
```

```python
import functools
import math

import jax
import jax.numpy as jnp
import numpy as np
from jax import lax
from jax.experimental import pallas as pl
from jax.experimental.pallas import tpu as pltpu

F32 = jnp.float32
BF16 = jnp.bfloat16
EPS = 1e-6

D_MODEL = 2048
A_HEADS, A_DK, A_DV = 8, 128, 128
B_HEADS, B_KV_HEADS, B_HEAD_DIM = 8, 2, 128
B_GROUP = B_HEADS // B_KV_HEADS
IDX_HEADS, IDX_DIM = 16, 64
TOPK_MAX = 256
PAGE_SIZE = 128
REL_BUCKETS, REL_MAX_DIST = 32, 128
X_HEADS, X_HEAD_DIM = 4, 128
N_GROUPS, EXP_PER_GROUP = 4, 8
N_EXPERTS = N_GROUPS * EXP_PER_GROUP
EXPERT_TOPK = 2
EXPERT_FF = 512
MOE_ROWS = 256

LANES = 128
VMEM_LIMIT = 56 * 1024 * 1024

TM_WIDE = 1024
TM_CROSS = 512
TM_ROUTER = 256
TM_TOKEN = 128
TN = 512
KV_W = B_KV_HEADS * B_HEAD_DIM

NEG = -1e30

C_AQ, C_AF, C_AI, C_AG, C_BQ, C_BK, C_BV, C_IQ = 0, 1024, 2048, 3072, 4096, 5120, 5376, 5632
NZ_MAIN = 6656
T_PAD = 432
T_SM, IW_LANE, IK_LANE = 384, 48, 64
T_GA, T_GB = 512, 2560
NZ_TAIL = 4608


def _cparams(*sem):
    return pltpu.CompilerParams(dimension_semantics=sem, vmem_limit_bytes=VMEM_LIMIT)


def _silu(x):
    return x * jax.nn.sigmoid(x)


def _nt(a, b):
    return lax.dot_general(a, b, (((1,), (1,)), ((), ())), preferred_element_type=F32)


def _mm(a, w):
    return jnp.dot(a.astype(BF16), w.astype(BF16), preferred_element_type=F32)


def _norm_matmul_kernel(x_ref, g_ref, w_ref, o_ref, h_ref):
    @pl.when(pl.program_id(1) == 0)
    def _():
        x = x_ref[...]
        ms = jnp.mean(x * x, axis=-1, keepdims=True)
        h_ref[...] = (x * lax.rsqrt(ms + EPS) * g_ref[...]).astype(h_ref.dtype)

    o_ref[...] = _mm(h_ref[...], w_ref[...])


def norm_matmul(x, g, w, *, tm, tn):
    m, k = x.shape
    n = w.shape[1]
    assert n % tn == 0
    return pl.pallas_call(
        _norm_matmul_kernel,
        grid=(m // tm, n // tn),
        in_specs=[pl.BlockSpec((tm, k), lambda i, j: (i, 0)),
                  pl.BlockSpec((1, k), lambda i, j: (0, 0)),
                  pl.BlockSpec((k, tn), lambda i, j: (0, j))],
        out_specs=pl.BlockSpec((tm, tn), lambda i, j: (i, j)),
        out_shape=jax.ShapeDtypeStruct((m, n), F32),
        scratch_shapes=[pltpu.VMEM((tm, k), BF16)],
        compiler_params=_cparams("parallel", "arbitrary"),
        name="norm_matmul",
    )(x, g.reshape(1, k), w)


def _in_proj_kernel(x_ref, g_ref, wa_ref, wb_ref, oa_ref, ob_ref, h_ref, *, na):
    j = pl.program_id(1)

    @pl.when(j == 0)
    def _():
        x = x_ref[...]
        ms = jnp.mean(x * x, axis=-1, keepdims=True)
        h_ref[...] = (x * lax.rsqrt(ms + EPS) * g_ref[...]).astype(h_ref.dtype)

    @pl.when(j < na)
    def _():
        oa_ref[...] = _nt(h_ref[...], wa_ref[...])

    @pl.when(j >= na)
    def _():
        ob_ref[...] = _nt(h_ref[...], wb_ref[...])


def in_proj(x, g, wa_t, a_rows, wb_t, *, tm, tn):
    m, k = x.shape
    na, nb = a_rows // tn, wb_t.shape[0] // tn
    assert a_rows % tn == 0 and wb_t.shape[0] % tn == 0
    a_idx = lambda j: jnp.minimum(j, na - 1)
    b_idx = lambda j: jnp.maximum(j - na, 0)
    return pl.pallas_call(
        functools.partial(_in_proj_kernel, na=na),
        grid=(m // tm, na + nb),
        in_specs=[pl.BlockSpec((tm, k), lambda i, j: (i, 0)),
                  pl.BlockSpec((1, k), lambda i, j: (0, 0)),
                  pl.BlockSpec((tn, k), lambda i, j: (a_idx(j), 0)),
                  pl.BlockSpec((tn, k), lambda i, j: (b_idx(j), 0))],
        out_specs=[pl.BlockSpec((tm, tn), lambda i, j: (i, a_idx(j))),
                   pl.BlockSpec((tm, tn), lambda i, j: (i, b_idx(j)))],
        out_shape=[jax.ShapeDtypeStruct((m, na * tn), F32), jax.ShapeDtypeStruct((m, nb * tn), F32)],
        scratch_shapes=[pltpu.VMEM((tm, k), BF16)],
        compiler_params=_cparams("parallel", "arbitrary"),
        name="in_proj",
    )(x, g.reshape(1, k), wa_t, wb_t)


def _matmul_res_kernel(x_ref, w_ref, r_ref, o_ref):
    o_ref[...] = r_ref[...] + _mm(x_ref[...], w_ref[...])


def _matmul_res_into_kernel(x_ref, w_ref, r_ref, buf_ref, o_ref):
    del buf_ref
    o_ref[...] = r_ref[...] + _mm(x_ref[...], w_ref[...])


def _matmul_kernel(x_ref, w_ref, o_ref):
    o_ref[...] = _mm(x_ref[...], w_ref[...])


def matmul(x, w, res=None, *, tm, tn, into=None):
    m, k = x.shape
    n = w.shape[1]
    in_specs = [pl.BlockSpec((tm, k), lambda i, j: (i, 0)),
                pl.BlockSpec((k, tn), lambda i, j: (0, j))]
    args = [x, w]
    kern = _matmul_kernel
    if res is not None:
        in_specs.append(pl.BlockSpec((tm, tn), lambda i, j: (i, j)))
        args.append(res)
        kern = _matmul_res_kernel
    out_shape, row_blk, aliases = jax.ShapeDtypeStruct((m, n), F32), 0, {}
    if into is not None:
        buf, row0 = into
        assert res is not None and row0 % tm == 0 and buf.shape[1] == n
        in_specs.append(pl.BlockSpec(memory_space=pl.ANY))
        args.append(buf)
        kern = _matmul_res_into_kernel
        out_shape, row_blk, aliases = jax.ShapeDtypeStruct(buf.shape, F32), row0 // tm, {len(args) - 1: 0}
    return pl.pallas_call(
        kern,
        grid=(m // tm, n // tn),
        in_specs=in_specs,
        out_specs=pl.BlockSpec((tm, tn), lambda i, j: (row_blk + i, j)),
        out_shape=out_shape,
        input_output_aliases=aliases,
        compiler_params=_cparams("parallel", "arbitrary"),
        name="matmul",
    )(*args)


HG_TB = 128
HG_C = 16
HG_H = HG_C // 2


def _hgrn_prompt_kernel(aq_ref, af_ref, ai_ref, ag_ref, lbl_ref, ng_ref, ya_ref, st_out_ref,
                        st_ref, q_s, k_s, g_s, v_s):
    t = pl.program_id(0)

    @pl.when(t == 0)
    def _():
        st_ref[...] = jnp.zeros_like(st_ref)

    lbl = lbl_ref[...]
    mx = jnp.max(lbl, axis=0, keepdims=True)
    ex = jnp.exp(lbl - mx)
    lb = ex[0:1, :] / jnp.sum(ex, axis=0, keepdims=True)

    f = lb + (1.0 - lb) * jax.nn.sigmoid(af_ref[...])
    logf = jnp.log(f)
    row = lax.broadcasted_iota(jnp.int32, (HG_TB, HG_TB), 0)
    col = lax.broadcasted_iota(jnp.int32, (HG_TB, HG_TB), 1)
    tri = jnp.where((row // HG_C == col // HG_C) & (col <= row), 1.0, 0.0).astype(BF16)
    g = jnp.zeros(logf.shape, F32)
    rem = logf
    for _ in range(3):
        part = rem.astype(BF16)
        g = g + jnp.dot(tri, part, preferred_element_type=F32)
        rem = rem - part.astype(F32)
    g_s[...] = g
    q_s[...] = _silu(aq_ref[...])
    k_s[...] = 1.0 - f
    v_s[...] = ai_ref[...]

    sub = lax.broadcasted_iota(jnp.int32, (HG_C, A_DK), 0)
    sub8 = lax.broadcasted_iota(jnp.int32, (HG_H, A_DK), 0)
    ng = ng_ref[...]

    def chunk(c, carry):
        r0 = pl.multiple_of(c * HG_C, HG_C)
        rows = pl.ds(r0, HG_C)
        for h in range(A_HEADS):
            cols = slice(h * A_DK, (h + 1) * A_DK)
            g = g_s[rows, cols]
            qh = q_s[rows, cols]
            kh = k_s[rows, cols]
            vh = v_s[rows, cols]
            halves = []
            for hb in range(2):
                rs = slice(hb * HG_H, (hb + 1) * HG_H)
                gb, qb, kb, vb = g[rs], qh[rs], kh[rs], vh[rs]
                ob = jnp.zeros((HG_H, A_DV), F32)
                for tt in range(HG_H):
                    d = gb[tt:tt + 1, :] - gb
                    e = jnp.exp(jnp.where(sub8 <= tt, d, -jnp.inf))
                    p = e * (qb[tt:tt + 1, :] * kb)
                    a_col = jnp.sum(p, axis=1, keepdims=True)
                    o_row = jnp.sum(a_col * vb, axis=0, keepdims=True)
                    ob = jnp.where(sub8 == tt, o_row, ob)
                halves.append(ob)
            o = jnp.concatenate(halves, axis=0)
            low = sub < HG_H
            g_mid = g[HG_H - 1:HG_H, :]
            q_hi = jnp.where(low, 0.0, qh * jnp.exp(jnp.minimum(g - g_mid, 0.0)))
            k_lo = jnp.where(low, kh * jnp.exp(jnp.minimum(g_mid - g, 0.0)), 0.0)
            st = st_ref[h]
            g_last = g[HG_C - 1:HG_C, :]
            kt = kh * jnp.exp(g_last - g)
            upd = lax.dot_general(vh.astype(BF16), jnp.concatenate([kt, k_lo], axis=1).astype(BF16),
                                  (((0,), (0,)), ((), ())), preferred_element_type=F32)
            lhs = jnp.concatenate([qh * jnp.exp(g), q_hi], axis=1).astype(BF16)
            rhs = jnp.concatenate([st, upd[:, A_DK:]], axis=1).astype(BF16)
            o = o + _nt(lhs, rhs)
            st_ref[h] = st * jnp.exp(g_last) + upd[:, :A_DK]
            on = o * lax.rsqrt(jnp.mean(o * o, axis=-1, keepdims=True) + EPS) * ng
            ya_ref[rows, cols] = (on * _silu(ag_ref[rows, cols])).astype(ya_ref.dtype)
        return carry

    lax.fori_loop(0, HG_TB // HG_C, chunk, 0, unroll=2)

    @pl.when(t == pl.num_programs(0) - 1)
    def _():
        st_out_ref[...] = st_ref[...]


def hgrn_prompt(z, lb_logits, norm_g):
    m = z.shape[0]
    w = A_HEADS * A_DK

    def zspec(cb):
        return pl.BlockSpec((HG_TB, w), lambda t, cb=cb: (t, cb))

    return pl.pallas_call(
        _hgrn_prompt_kernel,
        grid=(m // HG_TB,),
        in_specs=[zspec(C_AQ // w), zspec(C_AF // w), zspec(C_AI // w), zspec(C_AG // w),
                  pl.BlockSpec(lb_logits.shape, lambda t: (0, 0)),
                  pl.BlockSpec((1, A_DV), lambda t: (0, 0))],
        out_specs=[pl.BlockSpec((HG_TB, w), lambda t: (t, 0)),
                   pl.BlockSpec((A_HEADS, A_DV, A_DK), lambda t: (0, 0, 0))],
        out_shape=[jax.ShapeDtypeStruct((m, w), BF16),
                   jax.ShapeDtypeStruct((A_HEADS, A_DV, A_DK), F32)],
        scratch_shapes=[pltpu.VMEM((A_HEADS, A_DV, A_DK), F32)] + [pltpu.VMEM((HG_TB, w), F32)] * 4,
        compiler_params=_cparams("arbitrary"),
        name="hgrn_prompt",
    )(z, z, z, z, lb_logits, norm_g.reshape(1, A_DV))


def _hgrn_step_kernel(z_ref, lbl_ref, ng_ref, s_ref, ya_ref, s_out_ref):
    lbl = lbl_ref[...]
    mx = jnp.max(lbl, axis=0, keepdims=True)
    ex = jnp.exp(lbl - mx)
    lb = ex[0:1, :] / jnp.sum(ex, axis=0, keepdims=True)
    z = z_ref[0]
    w = A_HEADS * A_DK
    q = _silu(z[:, 0:w])
    f = lb + (1.0 - lb) * jax.nn.sigmoid(z[:, w:2 * w])
    kk = 1.0 - f
    v = z[:, 2 * w:3 * w]
    ag = z[:, 3 * w:4 * w]
    rows = []
    for h in range(A_HEADS):
        cols = slice(h * A_DK, (h + 1) * A_DK)
        rows += [f[:, cols], kk[:, cols], q[:, cols]]
    rows.append(jnp.zeros((LANES - 3 * A_HEADS, A_DK), F32))
    xt = jnp.concatenate(rows, axis=0).T
    ng = ng_ref[...]
    r16 = lambda a: a.astype(BF16).astype(F32)
    outs = []
    for h in range(A_HEADS):
        cols = slice(h * A_DV, (h + 1) * A_DV)
        fcol = xt[:, 3 * h:3 * h + 1]
        kcol = xt[:, 3 * h + 1:3 * h + 2]
        qcol = xt[:, 3 * h + 2:3 * h + 3]
        s_old = s_ref[0, h]
        s_out_ref[0, h] = fcol * s_old + kcol * v[:, cols]
        o = (jnp.sum(r16(qcol * fcol) * r16(s_old), axis=0, keepdims=True)
             + jnp.sum(qcol * kcol, axis=0, keepdims=True) * v[:, cols])
        on = o * lax.rsqrt(jnp.mean(o * o, axis=-1, keepdims=True) + EPS) * ng
        outs.append(on * _silu(ag[:, cols]))
    ya_ref[0] = jnp.concatenate(outs, axis=1).astype(ya_ref.dtype)


def hgrn_step(z4, lb_logits, norm_g, state):
    b = z4.shape[0]
    w = A_HEADS * A_DK
    return pl.pallas_call(
        _hgrn_step_kernel,
        grid=(b,),
        in_specs=[pl.BlockSpec((1, 1, 4 * w), lambda i: (i, 0, 0)),
                  pl.BlockSpec(lb_logits.shape, lambda i: (0, 0)),
                  pl.BlockSpec((1, A_DV), lambda i: (0, 0)),
                  pl.BlockSpec((1, A_HEADS, A_DK, A_DV), lambda i: (i, 0, 0, 0))],
        out_specs=[pl.BlockSpec((1, 1, w), lambda i: (i, 0, 0)),
                   pl.BlockSpec((1, A_HEADS, A_DK, A_DV), lambda i: (i, 0, 0, 0))],
        out_shape=[jax.ShapeDtypeStruct((b, 1, w), F32),
                   jax.ShapeDtypeStruct(state.shape, F32)],
        compiler_params=_cparams("arbitrary"),
        name="hgrn_step",
    )(z4.reshape(b, 1, 4 * w), lb_logits, norm_g.reshape(1, A_DV), state)


BISECT_MAX_ITERS = 320


def _bisect_threshold(count_ge, lo, hi, cnt_lo, topk):
    kf = float(topk)

    def unfinished(lo, hi, cl):
        mid = 0.5 * lo + 0.5 * hi
        can_split = jnp.logical_and(mid > lo, mid < hi)
        return jnp.max(jnp.where(jnp.logical_and(cl > kf, can_split), 1.0, 0.0))

    def cond(c):
        return jnp.logical_and(c[0] < BISECT_MAX_ITERS, c[-1] > 0.0)

    def body(c):
        it, lo, hi, cl, _ = c
        mid = 0.5 * lo + 0.5 * hi
        cm = count_ge(mid)
        ge = cm >= kf
        lo = jnp.where(ge, mid, lo)
        cl = jnp.where(ge, cm, cl)
        hi = jnp.where(ge, hi, mid)
        return it + 1, lo, hi, cl, unfinished(lo, hi, cl)

    out = lax.while_loop(cond, body, (jnp.int32(0), lo, hi, cnt_lo, unfinished(lo, hi, cnt_lo)))
    return out[1]


DSA_QB = 128
DSA_W = 512
DSA_W3 = 1024
P3_GROUP = 2


def _dsa_prompt_kernel(iq0_ref, iq1_ref, bq_ref, iw_ref, kidx_ref, k_ref, v_ref, bias_ref, o_ref,
                       score_s, qih_s, qs_s, wb_s, m_s, l_s, acc_s, *, topk):
    i = pl.program_id(0)
    nsub = DSA_W // LANES
    nsub3 = DSA_W3 // LANES
    qsub = DSA_QB // LANES
    nch3 = (i * DSA_QB + DSA_QB + DSA_W3 - 1) // DSA_W3
    nch = (i * DSA_QB + DSA_QB + DSA_W - 1) // DSA_W
    qpos = i * DSA_QB + lax.broadcasted_iota(jnp.int32, (DSA_QB, 1), 0)

    iw = iw_ref[...]
    wscale = IDX_DIM ** -0.5 * IDX_HEADS ** -0.5
    for h in range(IDX_HEADS):
        iq_ref, hh = (iq0_ref, h) if h < IDX_HEADS // 2 else (iq1_ref, h - IDX_HEADS // 2)
        qih_s[h] = iq_ref[:, hh * IDX_DIM:(hh + 1) * IDX_DIM].astype(BF16)
        wb_s[h] = jnp.broadcast_to(iw[:, IW_LANE + h:IW_LANE + h + 1] * wscale, (DSA_QB, LANES))
    for h in range(B_HEADS):
        qs_s[h // B_GROUP, (h % B_GROUP) * DSA_QB:(h % B_GROUP + 1) * DSA_QB, :] = (
            bq_ref[:, h * B_HEAD_DIM:(h + 1) * B_HEAD_DIM] * B_HEAD_DIM ** -0.5).astype(BF16)

    def p1(c, carry):
        c0 = pl.multiple_of(c * DSA_W, DSA_W)
        kc = kidx_ref[pl.ds(c0, DSA_W), :]
        sc = [jnp.zeros((DSA_QB, LANES), F32) for _ in range(nsub)]
        for h in range(IDX_HEADS):
            s = jnp.maximum(_nt(qih_s[h], kc), 0.0)
            wb = wb_s[h]
            for j in range(nsub):
                sc[j] = sc[j] + s[:, j * LANES:(j + 1) * LANES] * wb
        for j in range(nsub):
            kpos = c0 + j * LANES + lax.broadcasted_iota(jnp.int32, (1, LANES), 1)
            score_s[c * nsub + j] = jnp.where(kpos <= qpos, sc[j], -jnp.inf)
        return carry

    lax.fori_loop(0, nch // 2, lambda c2, carry: p1(2 * c2 + 1, p1(2 * c2, carry)), 0)
    lax.fori_loop(nch // 2 * 2, nch, p1, 0)

    def fill(tile, carry):
        score_s[tile] = jnp.full((DSA_QB, LANES), -jnp.inf, F32)
        return carry

    lax.fori_loop(nch * nsub, nch3 * nsub3, fill, 0)

    def stats(c, carry):
        mn, mx = carry
        for j in range(nsub):
            s = score_s[c * nsub + j]
            mx = jnp.maximum(mx, s)
            mn = jnp.minimum(mn, jnp.where(s > -jnp.inf, s, jnp.inf))
        return mn, mx

    mn, mx = lax.fori_loop(0, nch, stats, (jnp.full((DSA_QB, LANES), jnp.inf, F32),
                                           jnp.full((DSA_QB, LANES), -jnp.inf, F32)))
    lo0 = jnp.min(mn, axis=1, keepdims=True)
    hi0 = jnp.max(mx, axis=1, keepdims=True)

    def count_ge(thr):
        thr_b = jnp.broadcast_to(thr, (DSA_QB, LANES))

        def body(c, acc):
            for j in range(nsub):
                s = score_s[c * nsub + j]
                acc = acc + jnp.where(s >= thr_b, 1.0, 0.0)
            return acc

        acc = lax.fori_loop(0, nch, body, jnp.zeros((DSA_QB, LANES), F32))
        return jnp.sum(acc, axis=1, keepdims=True)

    thr = _bisect_threshold(count_ge, lo0, hi0, (qpos + 1).astype(F32), topk)
    thr_b = jnp.broadcast_to(thr, (DSA_QB, LANES))

    m_s[...] = jnp.full(m_s.shape, NEG, F32)
    l_s[...] = jnp.zeros(l_s.shape, F32)
    acc_s[...] = jnp.zeros(acc_s.shape, F32)

    def p3(c, with_bias):
        c0 = pl.multiple_of(c * DSA_W3, DSA_W3)
        madd = jnp.concatenate([jnp.where(score_s[c * nsub3 + j] >= thr_b, 0.0, NEG) for j in range(nsub3)], axis=1)
        kc = k_ref[pl.ds(c0, DSA_W3), :]
        vc = v_ref[pl.ds(c0, DSA_W3), :]
        rel = lambda qs, j: i * qsub + qs - (c * nsub3 + j)

        def bias_tile(h, qs, j):
            return jnp.where(rel(qs, j) == 0, bias_ref[h, 0], jnp.where(rel(qs, j) == 1, bias_ref[h, 1], 0.0))

        def scores(n):
            return _nt(qs_s[n], kc[:, n * B_HEAD_DIM:(n + 1) * B_HEAD_DIM])

        def softmax(n, lg):
            lg = lg.reshape(B_GROUP, DSA_QB, DSA_W3) + madd[None]
            if with_bias:
                lg = lg + jnp.stack([jnp.concatenate(
                    [jnp.concatenate([bias_tile(n * B_GROUP + gq, qs, j) for j in range(nsub3)], axis=1)
                     for qs in range(qsub)], axis=0) for gq in range(B_GROUP)])
            m_old = m_s[n]
            m_new = jnp.maximum(m_old, jnp.max(lg, axis=-1, keepdims=True))
            p = jnp.exp(lg - m_new)
            alpha = jnp.exp(m_old - m_new)
            l_s[n] = alpha * l_s[n] + jnp.sum(p, axis=-1, keepdims=True)
            m_s[n] = m_new
            pv = jnp.dot(p.reshape(B_GROUP * DSA_QB, DSA_W3).astype(BF16), vc[:, n * B_HEAD_DIM:(n + 1) * B_HEAD_DIM],
                         preferred_element_type=F32)
            return alpha, pv.reshape(B_GROUP, DSA_QB, B_HEAD_DIM)

        lgs = [scores(n) for n in range(B_KV_HEADS)]
        outs = [softmax(n, lgs[n]) for n in range(B_KV_HEADS)]
        for n in range(B_KV_HEADS):
            acc_s[n] = outs[n][0] * acc_s[n] + outs[n][1]

    n_far = jnp.maximum(i * qsub - 1, 0) // nsub3
    def far_group(cg, carry):
        for u in range(P3_GROUP):
            p3(P3_GROUP * cg + u, False)
        return carry

    lax.fori_loop(0, n_far // P3_GROUP, far_group, 0)
    lax.fori_loop(n_far // P3_GROUP * P3_GROUP, n_far, lambda c, carry: (p3(c, False), carry)[1], 0)
    lax.fori_loop(n_far, nch3, lambda c, carry: (p3(c, True), carry)[1], 0)

    for h in range(B_HEADS):
        n, gq = h // B_GROUP, h % B_GROUP
        o_ref[:, h * B_HEAD_DIM:(h + 1) * B_HEAD_DIM] = (acc_s[n, gq] / l_s[n, gq]).astype(o_ref.dtype)


def dsa_prompt(z, ztail, kidx_bf, k_bf, v_bf, bias_tiles):
    m = z.shape[0]
    topk = min(TOPK_MAX, m // 4)
    wq = B_HEADS * B_HEAD_DIM
    wi2 = IDX_HEADS * IDX_DIM // 2
    kern = functools.partial(_dsa_prompt_kernel, topk=topk)
    return pl.pallas_call(
        kern,
        grid=(m // DSA_QB,),
        in_specs=[pl.BlockSpec((DSA_QB, wi2), lambda i: (i, C_IQ // wi2)),
                  pl.BlockSpec((DSA_QB, wi2), lambda i: (i, C_IQ // wi2 + 1)),
                  pl.BlockSpec((DSA_QB, wq), lambda i: (i, C_BQ // wq)),
                  pl.BlockSpec((DSA_QB, LANES), lambda i: (i, T_SM // LANES)),
                  pl.BlockSpec(kidx_bf.shape, lambda i: (0, 0)),
                  pl.BlockSpec(k_bf.shape, lambda i: (0, 0)),
                  pl.BlockSpec(v_bf.shape, lambda i: (0, 0)),
                  pl.BlockSpec(bias_tiles.shape, lambda i: (0, 0, 0, 0))],
        out_specs=pl.BlockSpec((DSA_QB, wq), lambda i: (i, 0)),
        out_shape=jax.ShapeDtypeStruct((m, wq), BF16),
        scratch_shapes=[pltpu.VMEM((m // LANES, DSA_QB, LANES), F32),
                        pltpu.VMEM((IDX_HEADS, DSA_QB, IDX_DIM), BF16),
                        pltpu.VMEM((B_KV_HEADS, B_GROUP * DSA_QB, B_HEAD_DIM), BF16),
                        pltpu.VMEM((IDX_HEADS, DSA_QB, LANES), F32),
                        pltpu.VMEM((B_KV_HEADS, B_GROUP, DSA_QB, 1), F32),
                        pltpu.VMEM((B_KV_HEADS, B_GROUP, DSA_QB, 1), F32),
                        pltpu.VMEM((B_KV_HEADS, B_GROUP, DSA_QB, B_HEAD_DIM), F32)],
        compiler_params=_cparams("arbitrary"),
        name="dsa_prompt",
    )(z, z, z, ztail, kidx_bf, k_bf, v_bf, bias_tiles)


def _page_copies(table_ref, b, n_pages, src_hbm, dst, sem, rows_per_page=PAGE_SIZE):
    def copy(p):
        return pltpu.make_async_copy(src_hbm.at[table_ref[b, p]],
                                     dst.at[pl.ds(p * rows_per_page, rows_per_page)], sem)
    return copy


def _dsa_scores_kernel(pt_ref, iq_ref, iw_ref, iknew_ref, kidx_hbm, o_ref, buf, sem, *, n_pages):
    b = pl.program_id(0)
    nb = pl.num_programs(0)
    past = n_pages * PAGE_SIZE

    def page_copy(bb, slot, p):
        return pltpu.make_async_copy(kidx_hbm.at[pt_ref[bb, p]],
                                     buf.at[slot, :, pl.ds(pl.multiple_of(p * PAGE_SIZE, PAGE_SIZE), PAGE_SIZE)],
                                     sem.at[slot])

    def start(bb, slot):
        lax.fori_loop(0, n_pages, lambda p, c: (page_copy(bb, slot, p).start(), c)[1], 0)

    def wait(bb, slot):
        lax.fori_loop(0, n_pages, lambda p, c: (page_copy(bb, slot, p).wait(), c)[1], 0)

    slot = b % 2

    @pl.when(b == 0)
    def _():
        start(0, 0)

    @pl.when(b + 1 < nb)
    def _():
        start(b + 1, 1 - slot)

    wait(b, slot)

    r16 = lambda a: a.astype(BF16).astype(F32)
    qi = iq_ref[0].astype(BF16)
    wcol = r16(iw_ref[0]) * (IDX_DIM ** -0.5 * IDX_HEADS ** -0.5)
    s = r16(jnp.maximum(jnp.dot(qi, buf[slot].astype(BF16), preferred_element_type=F32), 0.0))
    o_ref[0, :, 0:past] = jnp.sum(s * wcol, axis=0, keepdims=True)
    sn = r16(jnp.maximum(jnp.dot(qi, iknew_ref[0].astype(BF16), preferred_element_type=F32), 0.0))
    sn = jnp.sum(sn * wcol, axis=0, keepdims=True)
    lane = lax.broadcasted_iota(jnp.int32, (1, LANES), 1)
    o_ref[0, :, past:past + LANES] = jnp.where(lane == 0, sn, -jnp.inf)


def dsa_scores(page_table, iq, iw, iknew_pad, cache_kidx):
    b, n_pages = page_table.shape
    past = n_pages * PAGE_SIZE
    kern = functools.partial(_dsa_scores_kernel, n_pages=n_pages)
    gs = pltpu.PrefetchScalarGridSpec(
        num_scalar_prefetch=1,
        grid=(b,),
        in_specs=[pl.BlockSpec((1, IDX_HEADS, IDX_DIM), lambda i, pt: (i, 0, 0)),
                  pl.BlockSpec((1, IDX_HEADS, 1), lambda i, pt: (i, 0, 0)),
                  pl.BlockSpec((1, IDX_DIM, LANES), lambda i, pt: (i, 0, 0)),
                  pl.BlockSpec(memory_space=pl.ANY)],
        out_specs=pl.BlockSpec((1, 1, past + LANES), lambda i, pt: (i, 0, 0)),
        scratch_shapes=[pltpu.VMEM((2, IDX_DIM, past), F32), pltpu.SemaphoreType.DMA((2,))],
    )
    return pl.pallas_call(
        kern, grid_spec=gs,
        out_shape=jax.ShapeDtypeStruct((b, 1, past + LANES), F32),
        compiler_params=_cparams("arbitrary"),
        name="dsa_scores",
    )(page_table, iq, iw, iknew_pad, cache_kidx)


def _dsa_threshold_kernel(s_ref, thr_ref, *, topk):
    s = s_ref[...]
    nb = s.shape[0]
    finite = s > -jnp.inf
    lo0 = jnp.min(jnp.where(finite, s, jnp.inf), axis=1, keepdims=True)
    hi0 = jnp.max(s, axis=1, keepdims=True)
    cnt0 = jnp.sum(jnp.where(finite, 1.0, 0.0), axis=1, keepdims=True)

    def count_ge(thr):
        return jnp.sum(jnp.where(s_ref[...] >= thr, 1.0, 0.0), axis=1, keepdims=True)

    thr = _bisect_threshold(count_ge, lo0, hi0, cnt0, topk)
    thr_ref[...] = jnp.broadcast_to(thr, (nb, LANES))


def dsa_threshold(scores, topk):
    b, l = scores.shape
    return pl.pallas_call(
        functools.partial(_dsa_threshold_kernel, topk=topk),
        grid=(1,),
        in_specs=[pl.BlockSpec((b, l), lambda i: (0, 0))],
        out_specs=pl.BlockSpec((b, LANES), lambda i: (0, 0)),
        out_shape=jax.ShapeDtypeStruct((b, LANES), F32),
        compiler_params=_cparams("arbitrary"),
        name="dsa_threshold",
    )(scores)


def _dsa_decode_kernel(pt_ref, q_ref, s_ref, thr_ref, knew_ref, vnew_ref, bias_ref, k_hbm, v_hbm, o_ref,
                       kbuf, vbuf, sem, *, n_pages):
    b = pl.program_id(0)
    nb = pl.num_programs(0)
    past = n_pages * PAGE_SIZE

    rpp = PAGE_SIZE * B_KV_HEADS

    def copies(bb, slot):
        ck = _page_copies(pt_ref, bb, n_pages, k_hbm, kbuf.at[slot], sem.at[0, slot], rpp)
        cv = _page_copies(pt_ref, bb, n_pages, v_hbm, vbuf.at[slot], sem.at[1, slot], rpp)
        return ck, cv

    def start(bb, slot):
        ck, cv = copies(bb, slot)
        lax.fori_loop(0, n_pages, lambda p, c: (ck(p).start(), cv(p).start(), c)[2], 0)

    def wait(bb, slot):
        ck, cv = copies(bb, slot)
        lax.fori_loop(0, n_pages, lambda p, c: (ck(p).wait(), cv(p).wait(), c)[2], 0)

    slot = b % 2

    @pl.when(b == 0)
    def _():
        kbuf[:, n_pages * rpp:, :] = jnp.zeros((2, rpp, B_HEAD_DIM), F32)
        vbuf[:, n_pages * rpp:, :] = jnp.zeros((2, rpp, B_HEAD_DIM), F32)
        start(0, 0)

    @pl.when(b + 1 < nb)
    def _():
        start(b + 1, 1 - slot)

    kbuf[slot, n_pages * rpp:n_pages * rpp + 8, :] = knew_ref[0]
    vbuf[slot, n_pages * rpp:n_pages * rpp + 8, :] = vnew_ref[0]
    wait(b, slot)

    sel = s_ref[0] >= thr_ref[0][:, 0:1]
    n_keys = past + PAGE_SIZE
    outs = []
    for n in range(B_KV_HEADS):
        kn = kbuf[slot, pl.ds(n, n_keys, stride=B_KV_HEADS), :].astype(BF16)
        vn = vbuf[slot, pl.ds(n, n_keys, stride=B_KV_HEADS), :].astype(BF16)
        qn = q_ref[0, n].astype(BF16)
        lg = _nt(qn, kn) * B_HEAD_DIM ** -0.5 + bias_ref[n]
        m = jnp.max(jnp.where(sel, lg, NEG), axis=1, keepdims=True)
        p = jnp.where(sel, jnp.exp(lg - m), 0.0)
        p = p / jnp.sum(p, axis=1, keepdims=True)
        outs.append(jnp.dot(p.astype(BF16), vn, preferred_element_type=F32))
    o_ref[0] = jnp.concatenate(outs, axis=0).astype(o_ref.dtype)


def dsa_decode(page_table, q8, scores, thr, knew8, vnew8, bias_rows, cache_k2, cache_v2):
    b, n_pages = page_table.shape
    past = n_pages * PAGE_SIZE
    l = past + LANES
    wkv = B_KV_HEADS * B_HEAD_DIM
    kern = functools.partial(_dsa_decode_kernel, n_pages=n_pages)
    gs = pltpu.PrefetchScalarGridSpec(
        num_scalar_prefetch=1,
        grid=(b,),
        in_specs=[pl.BlockSpec((1, B_KV_HEADS, 8, B_HEAD_DIM), lambda i, pt: (i, 0, 0, 0)),
                  pl.BlockSpec((1, 1, l), lambda i, pt: (i, 0, 0)),
                  pl.BlockSpec((1, 1, LANES), lambda i, pt: (i, 0, 0)),
                  pl.BlockSpec((1, 8, B_HEAD_DIM), lambda i, pt: (i, 0, 0)),
                  pl.BlockSpec((1, 8, B_HEAD_DIM), lambda i, pt: (i, 0, 0)),
                  pl.BlockSpec((B_KV_HEADS, 8, l), lambda i, pt: (0, 0, 0)),
                  pl.BlockSpec(memory_space=pl.ANY),
                  pl.BlockSpec(memory_space=pl.ANY)],
        out_specs=pl.BlockSpec((1, 2 * 8, B_HEAD_DIM), lambda i, pt: (i, 0, 0)),
        scratch_shapes=[pltpu.VMEM((2, l * B_KV_HEADS, B_HEAD_DIM), F32),
                        pltpu.VMEM((2, l * B_KV_HEADS, B_HEAD_DIM), F32),
                        pltpu.SemaphoreType.DMA((2, 2))],
    )
    return pl.pallas_call(
        kern, grid_spec=gs,
        out_shape=jax.ShapeDtypeStruct((b, 2 * 8, B_HEAD_DIM), F32),
        compiler_params=_cparams("arbitrary"),
        name="dsa_decode",
    )(page_table, q8, scores, thr, knew8, vnew8, bias_rows, cache_k2, cache_v2)


def _merge_kernel(ya_ref, yb_ref, ga_ref, gb_ref, wa_ref, wb_ref, o_ref):
    a = _mm(ya_ref[...], wa_ref[...])
    bb = _mm(yb_ref[...], wb_ref[...])
    o_ref[...] = (jax.nn.sigmoid(ga_ref[...]) * a + jax.nn.sigmoid(gb_ref[...]) * bb).astype(o_ref.dtype)


def merge(ya, yb, z, ga_col, gb_col, wa, wb, *, tm, tn):
    m, k = ya.shape
    n = wa.shape[1]
    return pl.pallas_call(
        _merge_kernel,
        grid=(m // tm, n // tn),
        in_specs=[pl.BlockSpec((tm, k), lambda i, j: (i, 0)),
                  pl.BlockSpec((tm, k), lambda i, j: (i, 0)),
                  pl.BlockSpec((tm, tn), lambda i, j: (i, ga_col // tn + j)),
                  pl.BlockSpec((tm, tn), lambda i, j: (i, gb_col // tn + j)),
                  pl.BlockSpec((k, tn), lambda i, j: (0, j)),
                  pl.BlockSpec((k, tn), lambda i, j: (0, j))],
        out_specs=pl.BlockSpec((tm, tn), lambda i, j: (i, j)),
        out_shape=jax.ShapeDtypeStruct((m, n), wa.dtype),
        compiler_params=_cparams("parallel", "arbitrary"),
        name="merge",
    )(ya, yb, z, z, wa, wb)


def _cross_prompt_kernel(x_ref, g_ref, wq_ref, mk_ref, mv_ref, wo_ref, o_ref, *, n_in_tiles):
    x = x_ref[...]
    ms = jnp.mean(x * x, axis=-1, keepdims=True)
    h = (x * lax.rsqrt(ms + EPS) * g_ref[...]).astype(BF16)
    q = jnp.dot(h, wq_ref[...], preferred_element_type=F32)
    outs = []
    for hh in range(X_HEADS):
        cols = slice(hh * X_HEAD_DIM, (hh + 1) * X_HEAD_DIM)
        lg = _nt(q[:, cols].astype(BF16), mk_ref[:, cols]) * X_HEAD_DIM ** -0.5
        mx = jnp.max(lg, axis=1, keepdims=True)
        p = jnp.exp(lg - mx)
        p = p / jnp.sum(p, axis=1, keepdims=True)
        outs.append(jnp.dot(p.astype(BF16), mv_ref[:, cols], preferred_element_type=F32).astype(BF16))
    att = jnp.concatenate(outs, axis=1)
    res = x + jnp.dot(att, wo_ref[...], preferred_element_type=F32)
    o_ref[...] = jnp.where(pl.program_id(0) < n_in_tiles, res, 0.0)


def cross_prompt(x, g, wq, mk, mv, wo, *, tm, out_rows=None):
    m, d = x.shape
    out_rows = m if out_rows is None else out_rows
    n_in = m // tm
    full = lambda a: pl.BlockSpec(a.shape, lambda i: (0,) * a.ndim)
    g2 = g.reshape(1, d)
    return pl.pallas_call(
        functools.partial(_cross_prompt_kernel, n_in_tiles=n_in),
        grid=(pl.cdiv(out_rows, tm),),
        in_specs=[pl.BlockSpec((tm, d), lambda i: (jnp.minimum(i, n_in - 1), 0)),
                  full(g2), full(wq), full(mk), full(mv), full(wo)],
        out_specs=pl.BlockSpec((tm, d), lambda i: (i, 0)),
        out_shape=jax.ShapeDtypeStruct((out_rows, d), F32),
        compiler_params=_cparams("parallel"),
        name="cross_prompt",
    )(x, g2, wq, mk, mv, wo)


def _cross_step_kernel(q_ref, mk_ref, mv_ref, o_ref):
    r16 = lambda a: a.astype(BF16).astype(F32)
    q = r16(q_ref[0])
    outs = []
    for hh in range(X_HEADS):
        cols = slice(hh * X_HEAD_DIM, (hh + 1) * X_HEAD_DIM)
        kh = r16(mk_ref[0, :, cols])
        vh = r16(mv_ref[0, :, cols])
        lg = jnp.sum(kh * q[:, cols], axis=1, keepdims=True) * X_HEAD_DIM ** -0.5
        mx = jnp.max(lg, axis=0, keepdims=True)
        p = jnp.exp(lg - mx)
        p = r16(p / jnp.sum(p, axis=0, keepdims=True))
        outs.append(jnp.sum(p * vh, axis=0, keepdims=True))
    o_ref[0] = jnp.concatenate(outs, axis=1).astype(o_ref.dtype)


def cross_step(q, mk, mv):
    b, w = q.shape
    mem = mk.shape[1]
    return pl.pallas_call(
        _cross_step_kernel,
        grid=(b,),
        in_specs=[pl.BlockSpec((1, 1, w), lambda i: (i, 0, 0)),
                  pl.BlockSpec((1, mem, w), lambda i: (i, 0, 0)),
                  pl.BlockSpec((1, mem, w), lambda i: (i, 0, 0))],
        out_specs=pl.BlockSpec((1, 1, w), lambda i: (i, 0, 0)),
        out_shape=jax.ShapeDtypeStruct((b, 1, w), F32),
        compiler_params=_cparams("arbitrary"),
        name="cross_step",
    )(q.reshape(b, 1, w), mk, mv)


def _pack_bf16_pairs(x):
    c = x.shape[1] // 2
    u = pltpu.bitcast(x.astype(BF16).astype(F32), jnp.uint32)
    return lax.shift_right_logical(u[:, :c], jnp.uint32(16)) | u[:, c:]


def _unpack_bf16_pairs(u):
    lo = pltpu.bitcast(lax.shift_left(u, jnp.uint32(16)), F32)
    hi = pltpu.bitcast(u & jnp.uint32(0xFFFF0000), F32)
    return jnp.concatenate([lo, hi], axis=1)


def _router_kernel(x_ref, g_ref, w_ref, b_ref, hf_ref, route_ref):
    x = x_ref[...]
    ms = jnp.mean(x * x, axis=-1, keepdims=True)
    hf = x * lax.rsqrt(ms + EPS) * g_ref[...]
    hf_ref[...] = _pack_bf16_pairs(hf)
    lg = _mm(hf, w_ref[...]) + b_ref[...]
    tm = lg.shape[0]
    lane = lax.broadcasted_iota(jnp.int32, (tm, LANES), 1)
    big = jnp.int32(LANES)
    is_g = lane < N_GROUPS
    gmax = jnp.max(jnp.where(is_g, lg, -jnp.inf), axis=1, keepdims=True)
    grp = jnp.min(jnp.where(is_g & (lg == gmax), lane, big), axis=1, keepdims=True)
    p_grp = 1.0 / jnp.sum(jnp.where(is_g, jnp.exp(lg - gmax), 0.0), axis=1, keepdims=True)
    e_lo = N_GROUPS + grp * EXP_PER_GROUP
    in_g = (lane >= e_lo) & (lane < e_lo + EXP_PER_GROUP)
    v1 = jnp.max(jnp.where(in_g, lg, -jnp.inf), axis=1, keepdims=True)
    i1 = jnp.min(jnp.where(in_g & (lg == v1), lane, big), axis=1, keepdims=True)
    rest = in_g & (lane != i1)
    v2 = jnp.max(jnp.where(rest, lg, -jnp.inf), axis=1, keepdims=True)
    i2 = jnp.min(jnp.where(rest & (lg == v2), lane, big), axis=1, keepdims=True)
    e2 = jnp.exp(v2 - v1)
    g1 = p_grp / (1.0 + e2)
    g2 = p_grp * e2 / (1.0 + e2)
    r = jnp.where(lane == 0, (i1 - N_GROUPS).astype(F32),
                  jnp.where(lane == 1, (i2 - N_GROUPS).astype(F32),
                            jnp.where(lane == 2, g1, jnp.where(lane == 3, g2, 0.0))))
    route_ref[...] = r


def router(x, g, w_pad, b_pad, *, tm):
    m, d = x.shape
    return pl.pallas_call(
        _router_kernel,
        grid=(pl.cdiv(m, tm),),
        in_specs=[pl.BlockSpec((tm, d), lambda i: (i, 0)),
                  pl.BlockSpec((1, d), lambda i: (0, 0)),
                  pl.BlockSpec((d, LANES), lambda i: (0, 0)),
                  pl.BlockSpec((1, LANES), lambda i: (0, 0))],
        out_specs=[pl.BlockSpec((tm, d // 2), lambda i: (i, 0)),
                   pl.BlockSpec((tm, LANES), lambda i: (i, 0))],
        out_shape=[jax.ShapeDtypeStruct((m, d // 2), jnp.uint32), jax.ShapeDtypeStruct((m, LANES), F32)],
        compiler_params=_cparams("parallel"),
        name="router",
    )(x, g.reshape(1, d), w_pad, b_pad)


def _moe_kernel(be_ref, nxt_ref, nused_ref, sbase_ref, nvalid_ref, padj_ref, order_ref, hf_hbm, wg_hbm, wu_hbm,
                wd_hbm, y_hbm, xbuf, obuf, wg_f, wu_f, wd_f, wg_s, wu_s, wd_s, sem_in, sem_out, sem_w,
                *, n_tokens):
    b = pl.program_id(0)
    nb = pl.num_programs(0)
    blk = xbuf.shape[1]
    xslot = b % 3
    used = b < nused_ref[0]
    next_used = b + 2 < nused_ref[0]
    n_assign = order_ref.shape[0]

    def slot_info(bb):
        base, nv, pad0 = sbase_ref[bb], nvalid_ref[bb], padj_ref[bb]

        def info(r):
            valid = r < nv
            asg = order_ref[jnp.minimum(base + r, n_assign - 1)]
            tok = lax.shift_right_logical(asg, 1)
            return jnp.where(valid, tok, 0), jnp.where(valid, (asg & 1) * n_tokens + tok, pad0 + r)
        return info

    def weight_copies(e):
        return (pltpu.make_async_copy(wg_hbm.at[e], wg_f, sem_w.at[0]),
                pltpu.make_async_copy(wu_hbm.at[e], wu_f, sem_w.at[1]),
                pltpu.make_async_copy(wd_hbm.at[e], wd_f, sem_w.at[2]))

    def gather_start(bb, sl):
        info = slot_info(bb)
        for r in range(blk):
            pltpu.make_async_copy(hf_hbm.at[pl.ds(info(r)[0], 1)], xbuf.at[sl, pl.ds(r, 1)],
                                  sem_in.at[sl]).start()

    def scatter_start(bb, sl):
        info = slot_info(bb)
        for r in range(blk):
            pltpu.make_async_copy(obuf.at[sl, pl.ds(r, 1)], y_hbm.at[pl.ds(info(r)[1], 1)],
                                  sem_out.at[sl]).start(priority=r % 2)

    def gather_wait(sl):
        pltpu.make_async_copy(hf_hbm.at[pl.ds(0, blk)], xbuf.at[sl], sem_in.at[sl]).wait()

    def scatter_wait(sl):
        pltpu.make_async_copy(obuf.at[sl], y_hbm.at[pl.ds(0, blk)], sem_out.at[sl]).wait()

    @pl.when(b == 0)
    def _():
        for cp in weight_copies(be_ref[0]):
            cp.start(priority=1)
        gather_start(0, 0)
        obuf[...] = jnp.zeros(obuf.shape, obuf.dtype)

    @pl.when(jnp.logical_and(b == 0, 1 < nused_ref[0]))
    def _():
        gather_start(1, 1)

    changed = jnp.logical_and(used, jnp.logical_or(b == 0, be_ref[b] != be_ref[jnp.maximum(b - 1, 0)]))

    @pl.when(changed)
    def _():
        for cp in weight_copies(be_ref[b]):
            cp.wait()
        wg_s[...] = wg_f[...].astype(BF16)
        wu_s[...] = wu_f[...].astype(BF16)
        wd_s[...] = wd_f[...].astype(BF16)

    @pl.when(jnp.logical_and(changed, nxt_ref[b] >= 0))
    def _():
        for cp in weight_copies(nxt_ref[b]):
            cp.start(priority=1)

    @pl.when(used)
    def _():
        gather_wait(xslot)

    @pl.when(b >= 3)
    def _():
        scatter_wait(xslot)

    def step(prefetch, flush_prev, compute):
        if prefetch:
            gather_start(b + 2, (b + 2) % 3)
        if flush_prev:
            scatter_start(b - 1, (b + 2) % 3)
        if compute:
            x = _unpack_bf16_pairs(xbuf[xslot]).astype(BF16)
            gg = jnp.dot(x, wg_s[...], preferred_element_type=F32)
            uu = jnp.dot(x, wu_s[...], preferred_element_type=F32)
            a = (_silu(gg) * uu).astype(BF16)
            obuf[xslot] = _pack_bf16_pairs(jnp.dot(a, wd_s[...], preferred_element_type=F32))

    first, last = b == 0, b == nb - 1
    land, lnot = jnp.logical_and, jnp.logical_not
    pl.when(land(first, next_used))(lambda: step(True, False, True))
    pl.when(land(first, lnot(next_used)))(lambda: step(False, False, True))
    pl.when(land(lnot(first), land(used, next_used)))(lambda: step(True, True, True))
    pl.when(land(lnot(first), land(used, lnot(next_used))))(lambda: step(False, True, True))
    pl.when(land(lnot(first), lnot(used)))(lambda: step(False, True, False))

    @pl.when(last)
    def _():
        scatter_start(b, xslot)
        scatter_wait(xslot)

    @pl.when(jnp.logical_and(last, b >= 1))
    def _():
        scatter_wait((b + 2) % 3)

    @pl.when(jnp.logical_and(last, b >= 2))
    def _():
        scatter_wait((b + 1) % 3)


def moe_experts(tables, hf, w_g, w_u, w_d, *, blk, out_rows):
    n_blocks = tables[0].shape[0]
    dp = hf.shape[1]
    d = 2 * dp
    ff = w_g.shape[2]
    gs = pltpu.PrefetchScalarGridSpec(
        num_scalar_prefetch=len(tables),
        grid=(n_blocks,),
        in_specs=[pl.BlockSpec(memory_space=pl.ANY)] * 4,
        out_specs=pl.BlockSpec(memory_space=pl.ANY),
        scratch_shapes=[pltpu.VMEM((3, blk, dp), jnp.uint32), pltpu.VMEM((3, blk, dp), jnp.uint32),
                        pltpu.VMEM((d, ff), F32), pltpu.VMEM((d, ff), F32), pltpu.VMEM((ff, d), F32),
                        pltpu.VMEM((d, ff), BF16), pltpu.VMEM((d, ff), BF16), pltpu.VMEM((ff, d), BF16),
                        pltpu.SemaphoreType.DMA((3,)), pltpu.SemaphoreType.DMA((3,)),
                        pltpu.SemaphoreType.DMA((3,))],
    )
    return pl.pallas_call(
        functools.partial(_moe_kernel, n_tokens=hf.shape[0]), grid_spec=gs,
        out_shape=jax.ShapeDtypeStruct((out_rows, dp), jnp.uint32),
        compiler_params=_cparams("arbitrary"),
        name="moe_experts",
    )(*tables, hf, w_g, w_u, w_d)


def _combine_kernel(x_ref, route_ref, gf_ref, y1_ref, y2_ref, op_ref, os_ref):
    i = pl.program_id(0)
    route = route_ref[...]
    x = (x_ref[...] + route[:, 2:3] * _unpack_bf16_pairs(y1_ref[...])
         + route[:, 3:4] * _unpack_bf16_pairs(y2_ref[...]))
    ms = jnp.mean(x * x, axis=-1, keepdims=True)
    out = x * lax.rsqrt(ms + EPS) * gf_ref[...]

    @pl.when(i < pl.num_programs(0) - 1)
    def _():
        op_ref[...] = out

    @pl.when(i == pl.num_programs(0) - 1)
    def _():
        os_ref[...] = out[:os_ref.shape[0]]


def combine(x, route, gf, y, plane, n_prompt, *, tm):
    m, d = x.shape
    n_tiles = n_prompt // tm
    assert n_prompt % tm == 0 and 0 < m - n_prompt <= tm
    return pl.pallas_call(
        _combine_kernel,
        grid=(n_tiles + 1,),
        in_specs=[pl.BlockSpec((tm, d), lambda i: (i, 0)),
                  pl.BlockSpec((tm, LANES), lambda i: (i, 0)),
                  pl.BlockSpec((1, d), lambda i: (0, 0)),
                  pl.BlockSpec((tm, d // 2), lambda i: (i, 0)),
                  pl.BlockSpec((tm, d // 2), lambda i: (plane // tm + i, 0))],
        out_specs=[pl.BlockSpec((tm, d), lambda i: (jnp.minimum(i, n_tiles - 1), 0)),
                   pl.BlockSpec((m - n_prompt, d), lambda i: (0, 0))],
        out_shape=[jax.ShapeDtypeStruct((n_prompt, d), F32), jax.ShapeDtypeStruct((m - n_prompt, d), F32)],
        compiler_params=_cparams("arbitrary"),
        name="combine",
    )(x, route, gf.reshape(1, d), y, y)


def _t5_bucket(dist):
    dist = jnp.asarray(dist, jnp.int32)
    max_exact = REL_BUCKETS // 2
    dist_f = jnp.maximum(dist, 1).astype(F32)
    large = max_exact + (jnp.log(dist_f / max_exact) / math.log(REL_MAX_DIST / max_exact)
                         * (REL_BUCKETS - max_exact)).astype(jnp.int32)
    large = jnp.minimum(large, REL_BUCKETS - 1)
    return jnp.where(dist < max_exact, dist, large)


def _bias_tables(rel_bias, past):
    r = np.arange(LANES)
    diff = r[:, None] - r[None, :]
    buckets = jnp.stack([_t5_bucket(np.maximum(diff, 0)),
                         _t5_bucket(np.maximum(diff + LANES, 0)),
                         _t5_bucket(np.full((LANES, LANES), 2 * LANES))])
    def lookup(bkt):
        oh = (bkt.reshape(-1, 1) == jnp.arange(REL_BUCKETS)[None, :]).astype(F32)
        out = jnp.dot(oh, rel_bias.astype(F32), precision=lax.Precision.HIGHEST)
        return out.T.reshape((rel_bias.shape[1],) + bkt.shape)

    tiles = lookup(buckets)
    tiles = tiles - tiles[:, 2:3]
    dist = np.maximum(past - np.arange(past + LANES), 0)
    rows = lookup(_t5_bucket(dist))
    rows = rows.reshape(B_KV_HEADS, B_GROUP, past + LANES)
    rows = jnp.concatenate([rows, jnp.zeros_like(rows)], axis=1)
    return tiles, rows


def _dispatch(eid, n_tokens, blk):
    a = eid.shape[0]
    assert EXPERT_TOPK == 2 and a == EXPERT_TOPK * n_tokens
    n_blocks = -(-(a + N_EXPERTS * (blk - 1)) // blk)
    rows = n_blocks * blk
    order = jnp.argsort(eid).astype(jnp.int32)
    counts = jnp.sum(eid[:, None] == jnp.arange(N_EXPERTS)[None, :], axis=0).astype(jnp.int32)
    cum = jnp.cumsum(counts)
    starts = cum - counts
    padded = (counts + blk - 1) // blk * blk
    pad_end = jnp.cumsum(padded)
    pad_start = pad_end - padded
    blocks = jnp.arange(n_blocks, dtype=jnp.int32)
    block_e = jnp.minimum(jnp.sum(pad_end[None, :] <= (blocks * blk)[:, None], axis=1), N_EXPERTS - 1)
    n_used = pad_end[-1] // blk
    off = blocks * blk - pad_start[block_e]
    sbase = starts[block_e] + off
    nvalid = jnp.where(blocks < n_used, jnp.clip(counts[block_e] - off, 0, blk), 0)
    padj = a + blocks * blk - cum[block_e]
    new_run = jnp.concatenate([jnp.array([True]), block_e[1:] != block_e[:-1]])
    run_start = jnp.where(new_run & (blocks < n_used), blocks, n_blocks)
    nxt = jnp.concatenate([jnp.flip(lax.cummin(jnp.flip(run_start)))[1:], jnp.array([n_blocks])])
    next_e = jnp.where(nxt < n_blocks, block_e[jnp.minimum(nxt, n_blocks - 1)], -1)
    i32 = lambda v: v.astype(jnp.int32)
    return (i32(block_e), i32(next_e), i32(n_used).reshape(1), i32(sbase), i32(nvalid), i32(padj), order), rows


def kernel(x_prompt, x_sample, mem_prompt, cache_k, cache_v, cache_kidx, page_table, state_hgrn, cache_mem_k,
           cache_mem_v, norm_mix, w_in, hgrn_lb_logits, hgrn_norm, w_branch_a, w_branch_b, w_out, norm_cross, w_xq,
           w_xk, w_xv, w_xo, norm_ffn, w_router_group, b_router_group, w_router_expert, b_router_expert, w_exp_gate,
           w_exp_up, w_exp_down, rel_bias, norm_final):
    assert w_in.shape[0] == 1, "single-layer step"
    l = 0
    drop0 = lambda a: a.reshape(a.shape[1:])
    bp, t, d = x_prompt.shape
    db = x_sample.shape[0]
    past = page_table.shape[1] * PAGE_SIZE
    xp = x_prompt.reshape(bp * t, d)
    xs = x_sample.reshape(db, d)

    wi_t = jnp.transpose(drop0(w_in)).astype(BF16)
    assert wi_t.shape[0] - NZ_MAIN + T_PAD == NZ_TAIL
    w_tail_t = jnp.pad(wi_t[NZ_MAIN:], ((T_PAD, 0), (0, 0)))
    wa, wb, wo = w_branch_a[l].astype(BF16), w_branch_b[l].astype(BF16), w_out[l].astype(BF16)
    wxq, wxk, wxv, wxo = (w_xq[l].astype(BF16), w_xk[l].astype(BF16), w_xv[l].astype(BF16), w_xo[l].astype(BF16))
    w_route = jnp.pad(jnp.concatenate([w_router_group[l], w_router_expert[l]], axis=1),
                      ((0, 0), (0, LANES - N_GROUPS - N_EXPERTS))).astype(BF16)
    b_route = jnp.pad(jnp.concatenate([b_router_group[l], b_router_expert[l]]),
                      (0, LANES - N_GROUPS - N_EXPERTS)).reshape(1, LANES)
    bias_tiles, bias_rows = _bias_tables(rel_bias, past)

    zp, zpt = in_proj(xp, norm_mix[l], wi_t, NZ_MAIN, w_tail_t, tm=TM_WIDE, tn=TN)
    kp = zp[:, C_BK:C_BK + KV_W]
    vp = zp[:, C_BV:C_BV + KV_W]
    ikp = zpt[:, T_SM + IK_LANE:T_SM + IK_LANE + IDX_DIM]
    ya_p, st_p = hgrn_prompt(zp, hgrn_lb_logits, hgrn_norm[l])
    yb_p = dsa_prompt(zp, zpt, ikp.astype(BF16), kp.astype(BF16), vp.astype(BF16), bias_tiles)
    mg_p = merge(ya_p, yb_p, zpt, T_GA, T_GB, wa, wb, tm=TM_WIDE, tn=TN)
    x1p = matmul(mg_p, wo, xp, tm=TM_WIDE, tn=TN)
    memp = mem_prompt.reshape(-1, d)
    mk = matmul(memp, wxk, tm=memp.shape[0], tn=TN)
    mv = matmul(memp, wxv, tm=memp.shape[0], tn=TN)
    n = -(-(bp * t + db) // TM_TOKEN) * TM_TOKEN
    x2p = cross_prompt(x1p, norm_cross[l], wxq, mk.astype(BF16), mv.astype(BF16), wxo, tm=TM_CROSS, out_rows=n)

    zs, zst = in_proj(xs, norm_mix[l], wi_t, NZ_MAIN, w_tail_t, tm=db, tn=TN)
    ks = zs[:, C_BK:C_BK + KV_W]
    vs = zs[:, C_BV:C_BV + KV_W]
    iks = zst[:, T_SM + IK_LANE:T_SM + IK_LANE + IDX_DIM]
    ya_s, st_s = hgrn_step(zs[:, :4 * A_HEADS * A_DK], hgrn_lb_logits, hgrn_norm[l], drop0(state_hgrn))
    iq_s = zs[:, C_IQ:C_IQ + IDX_HEADS * IDX_DIM].reshape(db, IDX_HEADS, IDX_DIM)
    iw_s = zst[:, T_SM + IW_LANE:T_SM + IW_LANE + IDX_HEADS].reshape(db, IDX_HEADS, 1)
    iknew_pad = jnp.pad(iks[:, :, None], ((0, 0), (0, 0), (0, LANES - 1)))
    scores = dsa_scores(page_table, iq_s, iw_s, iknew_pad,
                        jnp.swapaxes(drop0(cache_kidx), 1, 2)).reshape(db, past + LANES)
    topk_s = min(TOPK_MAX, (past + 1) // 4)
    thr = dsa_threshold(scores, topk_s)
    q8 = jnp.pad(zs[:, C_BQ:C_BQ + B_HEADS * B_HEAD_DIM].reshape(db, B_KV_HEADS, B_GROUP, B_HEAD_DIM),
                 ((0, 0), (0, 0), (0, 8 - B_GROUP), (0, 0)))
    knew8 = jnp.pad(ks.reshape(db, B_KV_HEADS, B_HEAD_DIM), ((0, 0), (0, 8 - B_KV_HEADS), (0, 0)))
    vnew8 = jnp.pad(vs.reshape(db, B_KV_HEADS, B_HEAD_DIM), ((0, 0), (0, 8 - B_KV_HEADS), (0, 0)))
    n_pool = cache_k.shape[1]
    ob = dsa_decode(page_table, q8, scores.reshape(db, 1, -1), thr.reshape(db, 1, LANES), knew8, vnew8, bias_rows,
                    cache_k.reshape(n_pool, PAGE_SIZE * B_KV_HEADS, B_HEAD_DIM),
                    cache_v.reshape(n_pool, PAGE_SIZE * B_KV_HEADS, B_HEAD_DIM))
    yb_s = ob.reshape(db, B_KV_HEADS, 8, B_HEAD_DIM)[:, :, :B_GROUP].reshape(db, B_HEADS * B_HEAD_DIM)
    mg_s = merge(ya_s.reshape(db, -1), yb_s, zst, T_GA, T_GB, wa, wb, tm=db, tn=TN)
    x1s = matmul(mg_s, wo, xs, tm=db, tn=TN)
    qx_s = norm_matmul(x1s, norm_cross[l], wxq, tm=db, tn=TN)
    mem = cache_mem_k.shape[2]
    att_s = cross_step(qx_s, cache_mem_k.reshape(db, mem, -1), cache_mem_v.reshape(db, mem, -1))
    x2 = matmul(att_s.reshape(db, -1), wxo, x1s, tm=db, tn=TN, into=(x2p, bp * t))

    hf, route = router(x2, norm_ffn[l], w_route, b_route, tm=TM_ROUTER)
    eid = route[:, :EXPERT_TOPK].astype(jnp.int32).reshape(-1)
    tables, rows = _dispatch(eid, n, MOE_ROWS)
    ye = moe_experts(tables, hf, drop0(w_exp_gate), drop0(w_exp_up), drop0(w_exp_down), blk=MOE_ROWS, out_rows=rows)
    y_p, y_s = combine(x2, route, norm_final, ye, n, bp * t, tm=TM_TOKEN)

    y_prompt = y_p.reshape(bp, t, d)
    y_sample = y_s[:db].reshape(db, 1, d)
    return (y_prompt, y_sample,
            kp.reshape(1, bp, t, B_KV_HEADS, B_HEAD_DIM), vp.reshape(1, bp, t, B_KV_HEADS, B_HEAD_DIM),
            ikp.reshape(1, bp, t, IDX_DIM),
            jnp.swapaxes(st_p, 1, 2).reshape(1, bp, A_HEADS, A_DK, A_DV),
            mk.reshape(1, bp, -1, X_HEADS, X_HEAD_DIM), mv.reshape(1, bp, -1, X_HEADS, X_HEAD_DIM),
            ks.reshape(1, db, 1, B_KV_HEADS, B_HEAD_DIM), vs.reshape(1, db, 1, B_KV_HEADS, B_HEAD_DIM),
            iks.reshape(1, db, 1, IDX_DIM),
            st_s.reshape(1, db, A_HEADS, A_DK, A_DV))
```

```python
import functools
import math

import jax
import jax.numpy as jnp
import numpy as np
from jax import lax
from jax.experimental import pallas as pl
from jax.experimental.pallas import tpu as pltpu

F32 = jnp.float32
BF16 = jnp.bfloat16
EPS = 1e-6

D_MODEL = 2048
A_HEADS, A_DK, A_DV = 8, 128, 128
B_HEADS, B_KV_HEADS, B_HEAD_DIM = 8, 2, 128
B_GROUP = B_HEADS // B_KV_HEADS
IDX_HEADS, IDX_DIM = 16, 64
TOPK_MAX = 256
PAGE_SIZE = 128
REL_BUCKETS, REL_MAX_DIST = 32, 128
X_HEADS, X_HEAD_DIM = 4, 128
N_GROUPS, EXP_PER_GROUP = 4, 8
N_EXPERTS = N_GROUPS * EXP_PER_GROUP
EXPERT_TOPK = 2
EXPERT_FF = 512
MOE_ROWS = 256

LANES = 128
VMEM_LIMIT = 56 * 1024 * 1024

TM_WIDE = 1024
TM_CROSS = 512
TM_ROUTER = 256
TM_TOKEN = 128
TN = 512
KV_W = B_KV_HEADS * B_HEAD_DIM

NEG = -1e30

C_AQ, C_AF, C_AI, C_AG, C_BQ, C_BK, C_BV, C_IQ = 0, 1024, 2048, 3072, 4096, 5120, 5376, 5632
NZ_MAIN = 6656
T_PAD = 432
T_SM, IW_LANE, IK_LANE = 384, 48, 64
T_GA, T_GB = 512, 2560
NZ_TAIL = 4608


def _cparams(*sem):
    return pltpu.CompilerParams(dimension_semantics=sem, vmem_limit_bytes=VMEM_LIMIT)


def _silu(x):
    return x * jax.nn.sigmoid(x)


def _nt(a, b):
    return lax.dot_general(a, b, (((1,), (1,)), ((), ())), preferred_element_type=F32)


def _mm(a, w):
    return jnp.dot(a.astype(BF16), w.astype(BF16), preferred_element_type=F32)


def _norm_matmul_kernel(x_ref, g_ref, w_ref, o_ref, h_ref):
    @pl.when(pl.program_id(1) == 0)
    def _():
        x = x_ref[...]
        ms = jnp.mean(x * x, axis=-1, keepdims=True)
        h_ref[...] = (x * lax.rsqrt(ms + EPS) * g_ref[...]).astype(h_ref.dtype)

    o_ref[...] = _mm(h_ref[...], w_ref[...])


def norm_matmul(x, g, w, *, tm, tn):
    m, k = x.shape
    n = w.shape[1]
    assert n % tn == 0
    return pl.pallas_call(
        _norm_matmul_kernel,
        grid=(m // tm, n // tn),
        in_specs=[pl.BlockSpec((tm, k), lambda i, j: (i, 0)),
                  pl.BlockSpec((1, k), lambda i, j: (0, 0)),
                  pl.BlockSpec((k, tn), lambda i, j: (0, j))],
        out_specs=pl.BlockSpec((tm, tn), lambda i, j: (i, j)),
        out_shape=jax.ShapeDtypeStruct((m, n), F32),
        scratch_shapes=[pltpu.VMEM((tm, k), BF16)],
        compiler_params=_cparams("parallel", "arbitrary"),
        name="norm_matmul",
    )(x, g.reshape(1, k), w)


def _in_proj_kernel(x_ref, g_ref, wa_ref, wb_ref, oa_ref, ob_ref, h_ref, *, na):
    j = pl.program_id(1)

    @pl.when(j == 0)
    def _():
        x = x_ref[...]
        ms = jnp.mean(x * x, axis=-1, keepdims=True)
        h_ref[...] = (x * lax.rsqrt(ms + EPS) * g_ref[...]).astype(h_ref.dtype)

    @pl.when(j < na)
    def _():
        oa_ref[...] = _nt(h_ref[...], wa_ref[...])

    @pl.when(j >= na)
    def _():
        ob_ref[...] = _nt(h_ref[...], wb_ref[...])


def in_proj(x, g, wa_t, a_rows, wb_t, *, tm, tn):
    m, k = x.shape
    na, nb = a_rows // tn, wb_t.shape[0] // tn
    assert a_rows % tn == 0 and wb_t.shape[0] % tn == 0
    a_idx = lambda j: jnp.minimum(j, na - 1)
    b_idx = lambda j: jnp.maximum(j - na, 0)
    return pl.pallas_call(
        functools.partial(_in_proj_kernel, na=na),
        grid=(m // tm, na + nb),
        in_specs=[pl.BlockSpec((tm, k), lambda i, j: (i, 0)),
                  pl.BlockSpec((1, k), lambda i, j: (0, 0)),
                  pl.BlockSpec((tn, k), lambda i, j: (a_idx(j), 0)),
                  pl.BlockSpec((tn, k), lambda i, j: (b_idx(j), 0))],
        out_specs=[pl.BlockSpec((tm, tn), lambda i, j: (i, a_idx(j))),
                   pl.BlockSpec((tm, tn), lambda i, j: (i, b_idx(j)))],
        out_shape=[jax.ShapeDtypeStruct((m, na * tn), F32), jax.ShapeDtypeStruct((m, nb * tn), F32)],
        scratch_shapes=[pltpu.VMEM((tm, k), BF16)],
        compiler_params=_cparams("parallel", "arbitrary"),
        name="in_proj",
    )(x, g.reshape(1, k), wa_t, wb_t)


def _matmul_res_kernel(x_ref, w_ref, r_ref, o_ref):
    o_ref[...] = r_ref[...] + _mm(x_ref[...], w_ref[...])


def _matmul_res_into_kernel(x_ref, w_ref, r_ref, buf_ref, o_ref):
    del buf_ref
    o_ref[...] = r_ref[...] + _mm(x_ref[...], w_ref[...])


def _matmul_kernel(x_ref, w_ref, o_ref):
    o_ref[...] = _mm(x_ref[...], w_ref[...])


def matmul(x, w, res=None, *, tm, tn, into=None):
    m, k = x.shape
    n = w.shape[1]
    in_specs = [pl.BlockSpec((tm, k), lambda i, j: (i, 0)),
                pl.BlockSpec((k, tn), lambda i, j: (0, j))]
    args = [x, w]
    kern = _matmul_kernel
    if res is not None:
        in_specs.append(pl.BlockSpec((tm, tn), lambda i, j: (i, j)))
        args.append(res)
        kern = _matmul_res_kernel
    out_shape, row_blk, aliases = jax.ShapeDtypeStruct((m, n), F32), 0, {}
    if into is not None:
        buf, row0 = into
        assert res is not None and row0 % tm == 0 and buf.shape[1] == n
        in_specs.append(pl.BlockSpec(memory_space=pl.ANY))
        args.append(buf)
        kern = _matmul_res_into_kernel
        out_shape, row_blk, aliases = jax.ShapeDtypeStruct(buf.shape, F32), row0 // tm, {len(args) - 1: 0}
    return pl.pallas_call(
        kern,
        grid=(m // tm, n // tn),
        in_specs=in_specs,
        out_specs=pl.BlockSpec((tm, tn), lambda i, j: (row_blk + i, j)),
        out_shape=out_shape,
        input_output_aliases=aliases,
        compiler_params=_cparams("parallel", "arbitrary"),
        name="matmul",
    )(*args)


HG_TB = 128
HG_C = 16
HG_H = HG_C // 2


def _hgrn_prompt_kernel(aq_ref, af_ref, ai_ref, ag_ref, lbl_ref, ng_ref, ya_ref, st_out_ref,
                        st_ref, q_s, k_s, g_s, v_s):
    t = pl.program_id(0)

    @pl.when(t == 0)
    def _():
        st_ref[...] = jnp.zeros_like(st_ref)

    lbl = lbl_ref[...]
    mx = jnp.max(lbl, axis=0, keepdims=True)
    ex = jnp.exp(lbl - mx)
    lb = ex[0:1, :] / jnp.sum(ex, axis=0, keepdims=True)

    f = lb + (1.0 - lb) * jax.nn.sigmoid(af_ref[...])
    logf = jnp.log(f)
    row = lax.broadcasted_iota(jnp.int32, (HG_TB, HG_TB), 0)
    col = lax.broadcasted_iota(jnp.int32, (HG_TB, HG_TB), 1)
    tri = jnp.where((row // HG_C == col // HG_C) & (col <= row), 1.0, 0.0).astype(BF16)
    g = jnp.zeros(logf.shape, F32)
    rem = logf
    for _ in range(3):
        part = rem.astype(BF16)
        g = g + jnp.dot(tri, part, preferred_element_type=F32)
        rem = rem - part.astype(F32)
    g_s[...] = g
    q_s[...] = _silu(aq_ref[...])
    k_s[...] = 1.0 - f
    v_s[...] = ai_ref[...]

    sub = lax.broadcasted_iota(jnp.int32, (HG_C, A_DK), 0)
    sub8 = lax.broadcasted_iota(jnp.int32, (HG_H, A_DK), 0)
    ng = ng_ref[...]

    def chunk(c, carry):
        r0 = pl.multiple_of(c * HG_C, HG_C)
        rows = pl.ds(r0, HG_C)
        for h in range(A_HEADS):
            cols = slice(h * A_DK, (h + 1) * A_DK)
            g = g_s[rows, cols]
            qh = q_s[rows, cols]
            kh = k_s[rows, cols]
            vh = v_s[rows, cols]
            halves = []
            for hb in range(2):
                rs = slice(hb * HG_H, (hb + 1) * HG_H)
                gb, qb, kb, vb = g[rs], qh[rs], kh[rs], vh[rs]
                ob = jnp.zeros((HG_H, A_DV), F32)
                for tt in range(HG_H):
                    d = gb[tt:tt + 1, :] - gb
                    e = jnp.exp(jnp.where(sub8 <= tt, d, -jnp.inf))
                    p = e * (qb[tt:tt + 1, :] * kb)
                    a_col = jnp.sum(p, axis=1, keepdims=True)
                    o_row = jnp.sum(a_col * vb, axis=0, keepdims=True)
                    ob = jnp.where(sub8 == tt, o_row, ob)
                halves.append(ob)
            o = jnp.concatenate(halves, axis=0)
            low = sub < HG_H
            g_mid = g[HG_H - 1:HG_H, :]
            q_hi = jnp.where(low, 0.0, qh * jnp.exp(jnp.minimum(g - g_mid, 0.0)))
            k_lo = jnp.where(low, kh * jnp.exp(jnp.minimum(g_mid - g, 0.0)), 0.0)
            st = st_ref[h]
            g_last = g[HG_C - 1:HG_C, :]
            kt = kh * jnp.exp(g_last - g)
            upd = lax.dot_general(vh.astype(BF16), jnp.concatenate([kt, k_lo], axis=1).astype(BF16),
                                  (((0,), (0,)), ((), ())), preferred_element_type=F32)
            lhs = jnp.concatenate([qh * jnp.exp(g), q_hi], axis=1).astype(BF16)
            rhs = jnp.concatenate([st, upd[:, A_DK:]], axis=1).astype(BF16)
            o = o + _nt(lhs, rhs)
            st_ref[h] = st * jnp.exp(g_last) + upd[:, :A_DK]
            on = o * lax.rsqrt(jnp.mean(o * o, axis=-1, keepdims=True) + EPS) * ng
            ya_ref[rows, cols] = (on * _silu(ag_ref[rows, cols])).astype(ya_ref.dtype)
        return carry

    lax.fori_loop(0, HG_TB // HG_C, chunk, 0, unroll=2)

    @pl.when(t == pl.num_programs(0) - 1)
    def _():
        st_out_ref[...] = st_ref[...]


def hgrn_prompt(z, lb_logits, norm_g):
    m = z.shape[0]
    w = A_HEADS * A_DK

    def zspec(cb):
        return pl.BlockSpec((HG_TB, w), lambda t, cb=cb: (t, cb))

    return pl.pallas_call(
        _hgrn_prompt_kernel,
        grid=(m // HG_TB,),
        in_specs=[zspec(C_AQ // w), zspec(C_AF // w), zspec(C_AI // w), zspec(C_AG // w),
                  pl.BlockSpec(lb_logits.shape, lambda t: (0, 0)),
                  pl.BlockSpec((1, A_DV), lambda t: (0, 0))],
        out_specs=[pl.BlockSpec((HG_TB, w), lambda t: (t, 0)),
                   pl.BlockSpec((A_HEADS, A_DV, A_DK), lambda t: (0, 0, 0))],
        out_shape=[jax.ShapeDtypeStruct((m, w), BF16),
                   jax.ShapeDtypeStruct((A_HEADS, A_DV, A_DK), F32)],
        scratch_shapes=[pltpu.VMEM((A_HEADS, A_DV, A_DK), F32)] + [pltpu.VMEM((HG_TB, w), F32)] * 4,
        compiler_params=_cparams("arbitrary"),
        name="hgrn_prompt",
    )(z, z, z, z, lb_logits, norm_g.reshape(1, A_DV))


def _hgrn_step_kernel(z_ref, lbl_ref, ng_ref, s_ref, ya_ref, s_out_ref):
    lbl = lbl_ref[...]
    mx = jnp.max(lbl, axis=0, keepdims=True)
    ex = jnp.exp(lbl - mx)
    lb = ex[0:1, :] / jnp.sum(ex, axis=0, keepdims=True)
    z = z_ref[0]
    w = A_HEADS * A_DK
    q = _silu(z[:, 0:w])
    f = lb + (1.0 - lb) * jax.nn.sigmoid(z[:, w:2 * w])
    kk = 1.0 - f
    v = z[:, 2 * w:3 * w]
    ag = z[:, 3 * w:4 * w]
    rows = []
    for h in range(A_HEADS):
        cols = slice(h * A_DK, (h + 1) * A_DK)
        rows += [f[:, cols], kk[:, cols], q[:, cols]]
    rows.append(jnp.zeros((LANES - 3 * A_HEADS, A_DK), F32))
    xt = jnp.concatenate(rows, axis=0).T
    ng = ng_ref[...]
    r16 = lambda a: a.astype(BF16).astype(F32)
    outs = []
    for h in range(A_HEADS):
        cols = slice(h * A_DV, (h + 1) * A_DV)
        fcol = xt[:, 3 * h:3 * h + 1]
        kcol = xt[:, 3 * h + 1:3 * h + 2]
        qcol = xt[:, 3 * h + 2:3 * h + 3]
        s_old = s_ref[0, h]
        s_out_ref[0, h] = fcol * s_old + kcol * v[:, cols]
        o = (jnp.sum(r16(qcol * fcol) * r16(s_old), axis=0, keepdims=True)
             + jnp.sum(qcol * kcol, axis=0, keepdims=True) * v[:, cols])
        on = o * lax.rsqrt(jnp.mean(o * o, axis=-1, keepdims=True) + EPS) * ng
        outs.append(on * _silu(ag[:, cols]))
    ya_ref[0] = jnp.concatenate(outs, axis=1).astype(ya_ref.dtype)


def hgrn_step(z4, lb_logits, norm_g, state):
    b = z4.shape[0]
    w = A_HEADS * A_DK
    return pl.pallas_call(
        _hgrn_step_kernel,
        grid=(b,),
        in_specs=[pl.BlockSpec((1, 1, 4 * w), lambda i: (i, 0, 0)),
                  pl.BlockSpec(lb_logits.shape, lambda i: (0, 0)),
                  pl.BlockSpec((1, A_DV), lambda i: (0, 0)),
                  pl.BlockSpec((1, A_HEADS, A_DK, A_DV), lambda i: (i, 0, 0, 0))],
        out_specs=[pl.BlockSpec((1, 1, w), lambda i: (i, 0, 0)),
                   pl.BlockSpec((1, A_HEADS, A_DK, A_DV), lambda i: (i, 0, 0, 0))],
        out_shape=[jax.ShapeDtypeStruct((b, 1, w), F32),
                   jax.ShapeDtypeStruct(state.shape, F32)],
        compiler_params=_cparams("arbitrary"),
        name="hgrn_step",
    )(z4.reshape(b, 1, 4 * w), lb_logits, norm_g.reshape(1, A_DV), state)


BISECT_MAX_ITERS = 320


def _bisect_threshold(count_ge, lo, hi, cnt_lo, topk):
    kf = float(topk)

    def unfinished(lo, hi, cl):
        mid = 0.5 * lo + 0.5 * hi
        can_split = jnp.logical_and(mid > lo, mid < hi)
        return jnp.max(jnp.where(jnp.logical_and(cl > kf, can_split), 1.0, 0.0))

    def cond(c):
        return jnp.logical_and(c[0] < BISECT_MAX_ITERS, c[-1] > 0.0)

    def body(c):
        it, lo, hi, cl, _ = c
        mid = 0.5 * lo + 0.5 * hi
        cm = count_ge(mid)
        ge = cm >= kf
        lo = jnp.where(ge, mid, lo)
        cl = jnp.where(ge, cm, cl)
        hi = jnp.where(ge, hi, mid)
        return it + 1, lo, hi, cl, unfinished(lo, hi, cl)

    out = lax.while_loop(cond, body, (jnp.int32(0), lo, hi, cnt_lo, unfinished(lo, hi, cnt_lo)))
    return out[1]


DSA_QB = 128
DSA_W = 512
DSA_W3 = 1024
P3_GROUP = 2


def _dsa_prompt_kernel(iq0_ref, iq1_ref, bq_ref, iw_ref, kidx_ref, k_ref, v_ref, bias_ref, o_ref,
                       score_s, qih_s, qs_s, wb_s, m_s, l_s, acc_s, *, topk):
    i = pl.program_id(0)
    nsub = DSA_W // LANES
    nsub3 = DSA_W3 // LANES
    qsub = DSA_QB // LANES
    nch3 = (i * DSA_QB + DSA_QB + DSA_W3 - 1) // DSA_W3
    nch = (i * DSA_QB + DSA_QB + DSA_W - 1) // DSA_W
    qpos = i * DSA_QB + lax.broadcasted_iota(jnp.int32, (DSA_QB, 1), 0)

    iw = iw_ref[...]
    wscale = IDX_DIM ** -0.5 * IDX_HEADS ** -0.5
    for h in range(IDX_HEADS):
        iq_ref, hh = (iq0_ref, h) if h < IDX_HEADS // 2 else (iq1_ref, h - IDX_HEADS // 2)
        qih_s[h] = iq_ref[:, hh * IDX_DIM:(hh + 1) * IDX_DIM].astype(BF16)
        wb_s[h] = jnp.broadcast_to(iw[:, IW_LANE + h:IW_LANE + h + 1] * wscale, (DSA_QB, LANES))
    for h in range(B_HEADS):
        qs_s[h // B_GROUP, (h % B_GROUP) * DSA_QB:(h % B_GROUP + 1) * DSA_QB, :] = (
            bq_ref[:, h * B_HEAD_DIM:(h + 1) * B_HEAD_DIM] * B_HEAD_DIM ** -0.5).astype(BF16)

    def p1(c, carry):
        c0 = pl.multiple_of(c * DSA_W, DSA_W)
        kc = kidx_ref[pl.ds(c0, DSA_W), :]
        sc = [jnp.zeros((DSA_QB, LANES), F32) for _ in range(nsub)]
        for h in range(IDX_HEADS):
            s = jnp.maximum(_nt(qih_s[h], kc), 0.0)
            wb = wb_s[h]
            for j in range(nsub):
                sc[j] = sc[j] + s[:, j * LANES:(j + 1) * LANES] * wb
        for j in range(nsub):
            kpos = c0 + j * LANES + lax.broadcasted_iota(jnp.int32, (1, LANES), 1)
            score_s[c * nsub + j] = jnp.where(kpos <= qpos, sc[j], -jnp.inf)
        return carry

    lax.fori_loop(0, nch // 2, lambda c2, carry: p1(2 * c2 + 1, p1(2 * c2, carry)), 0)
    lax.fori_loop(nch // 2 * 2, nch, p1, 0)

    def fill(tile, carry):
        score_s[tile] = jnp.full((DSA_QB, LANES), -jnp.inf, F32)
        return carry

    lax.fori_loop(nch * nsub, nch3 * nsub3, fill, 0)

    def stats(c, carry):
        mn, mx = carry
        for j in range(nsub):
            s = score_s[c * nsub + j]
            mx = jnp.maximum(mx, s)
            mn = jnp.minimum(mn, jnp.where(s > -jnp.inf, s, jnp.inf))
        return mn, mx

    mn, mx = lax.fori_loop(0, nch, stats, (jnp.full((DSA_QB, LANES), jnp.inf, F32),
                                           jnp.full((DSA_QB, LANES), -jnp.inf, F32)))
    lo0 = jnp.min(mn, axis=1, keepdims=True)
    hi0 = jnp.max(mx, axis=1, keepdims=True)

    def count_ge(thr):
        thr_b = jnp.broadcast_to(thr, (DSA_QB, LANES))

        def body(c, acc):
            for j in range(nsub):
                s = score_s[c * nsub + j]
                acc = acc + jnp.where(s >= thr_b, 1.0, 0.0)
            return acc

        acc = lax.fori_loop(0, nch // 2, lambda c2, a: body(2 * c2 + 1, body(2 * c2, a)),
                            jnp.zeros((DSA_QB, LANES), F32))
        acc = lax.fori_loop(nch // 2 * 2, nch, body, acc)
        return jnp.sum(acc, axis=1, keepdims=True)

    thr = _bisect_threshold(count_ge, lo0, hi0, (qpos + 1).astype(F32), topk)
    thr_b = jnp.broadcast_to(thr, (DSA_QB, LANES))

    m_s[...] = jnp.full(m_s.shape, NEG, F32)
    l_s[...] = jnp.zeros(l_s.shape, F32)
    acc_s[...] = jnp.zeros(acc_s.shape, F32)

    def p3(c, with_bias):
        c0 = pl.multiple_of(c * DSA_W3, DSA_W3)
        madd = jnp.concatenate([jnp.where(score_s[c * nsub3 + j] >= thr_b, 0.0, NEG) for j in range(nsub3)], axis=1)
        kc = k_ref[pl.ds(c0, DSA_W3), :]
        vc = v_ref[pl.ds(c0, DSA_W3), :]
        rel = lambda qs, j: i * qsub + qs - (c * nsub3 + j)

        def bias_tile(h, qs, j):
            return jnp.where(rel(qs, j) == 0, bias_ref[h, 0], jnp.where(rel(qs, j) == 1, bias_ref[h, 1], 0.0))

        def scores(n):
            return _nt(qs_s[n], kc[:, n * B_HEAD_DIM:(n + 1) * B_HEAD_DIM])

        def softmax(n, lg):
            lg = lg.reshape(B_GROUP, DSA_QB, DSA_W3) + madd[None]
            if with_bias:
                lg = lg + jnp.stack([jnp.concatenate(
                    [jnp.concatenate([bias_tile(n * B_GROUP + gq, qs, j) for j in range(nsub3)], axis=1)
                     for qs in range(qsub)], axis=0) for gq in range(B_GROUP)])
            m_old = m_s[n]
            m_new = jnp.maximum(m_old, jnp.max(lg, axis=-1, keepdims=True))
            p = jnp.exp(lg - m_new)
            alpha = jnp.exp(m_old - m_new)
            l_s[n] = alpha * l_s[n] + jnp.sum(p, axis=-1, keepdims=True)
            m_s[n] = m_new
            pv = jnp.dot(p.reshape(B_GROUP * DSA_QB, DSA_W3).astype(BF16), vc[:, n * B_HEAD_DIM:(n + 1) * B_HEAD_DIM],
                         preferred_element_type=F32)
            return alpha, pv.reshape(B_GROUP, DSA_QB, B_HEAD_DIM)

        lgs = [scores(n) for n in range(B_KV_HEADS)]
        outs = [softmax(n, lgs[n]) for n in range(B_KV_HEADS)]
        for n in range(B_KV_HEADS):
            acc_s[n] = outs[n][0] * acc_s[n] + outs[n][1]

    n_far = jnp.maximum(i * qsub - 1, 0) // nsub3
    def far_group(cg, carry):
        for u in range(P3_GROUP):
            p3(P3_GROUP * cg + u, False)
        return carry

    lax.fori_loop(0, n_far // P3_GROUP, far_group, 0)
    lax.fori_loop(n_far // P3_GROUP * P3_GROUP, n_far, lambda c, carry: (p3(c, False), carry)[1], 0)
    lax.fori_loop(n_far, nch3, lambda c, carry: (p3(c, True), carry)[1], 0)

    for h in range(B_HEADS):
        n, gq = h // B_GROUP, h % B_GROUP
        o_ref[:, h * B_HEAD_DIM:(h + 1) * B_HEAD_DIM] = (acc_s[n, gq] / l_s[n, gq]).astype(o_ref.dtype)


def dsa_prompt(z, ztail, kidx_bf, k_bf, v_bf, bias_tiles):
    m = z.shape[0]
    topk = min(TOPK_MAX, m // 4)
    wq = B_HEADS * B_HEAD_DIM
    wi2 = IDX_HEADS * IDX_DIM // 2
    kern = functools.partial(_dsa_prompt_kernel, topk=topk)
    return pl.pallas_call(
        kern,
        grid=(m // DSA_QB,),
        in_specs=[pl.BlockSpec((DSA_QB, wi2), lambda i: (i, C_IQ // wi2)),
                  pl.BlockSpec((DSA_QB, wi2), lambda i: (i, C_IQ // wi2 + 1)),
                  pl.BlockSpec((DSA_QB, wq), lambda i: (i, C_BQ // wq)),
                  pl.BlockSpec((DSA_QB, LANES), lambda i: (i, T_SM // LANES)),
                  pl.BlockSpec(kidx_bf.shape, lambda i: (0, 0)),
                  pl.BlockSpec(k_bf.shape, lambda i: (0, 0)),
                  pl.BlockSpec(v_bf.shape, lambda i: (0, 0)),
                  pl.BlockSpec(bias_tiles.shape, lambda i: (0, 0, 0, 0))],
        out_specs=pl.BlockSpec((DSA_QB, wq), lambda i: (i, 0)),
        out_shape=jax.ShapeDtypeStruct((m, wq), BF16),
        scratch_shapes=[pltpu.VMEM((m // LANES, DSA_QB, LANES), F32),
                        pltpu.VMEM((IDX_HEADS, DSA_QB, IDX_DIM), BF16),
                        pltpu.VMEM((B_KV_HEADS, B_GROUP * DSA_QB, B_HEAD_DIM), BF16),
                        pltpu.VMEM((IDX_HEADS, DSA_QB, LANES), F32),
                        pltpu.VMEM((B_KV_HEADS, B_GROUP, DSA_QB, 1), F32),
                        pltpu.VMEM((B_KV_HEADS, B_GROUP, DSA_QB, 1), F32),
                        pltpu.VMEM((B_KV_HEADS, B_GROUP, DSA_QB, B_HEAD_DIM), F32)],
        compiler_params=_cparams("arbitrary"),
        name="dsa_prompt",
    )(z, z, z, ztail, kidx_bf, k_bf, v_bf, bias_tiles)


def _page_copies(table_ref, b, n_pages, src_hbm, dst, sem, rows_per_page=PAGE_SIZE):
    def copy(p):
        return pltpu.make_async_copy(src_hbm.at[table_ref[b, p]],
                                     dst.at[pl.ds(p * rows_per_page, rows_per_page)], sem)
    return copy


def _dsa_scores_kernel(pt_ref, iq_ref, iw_ref, iknew_ref, kidx_hbm, o_ref, buf, sem, *, n_pages):
    b = pl.program_id(0)
    nb = pl.num_programs(0)
    past = n_pages * PAGE_SIZE

    def page_copy(bb, slot, p):
        return pltpu.make_async_copy(kidx_hbm.at[pt_ref[bb, p]],
                                     buf.at[slot, :, pl.ds(pl.multiple_of(p * PAGE_SIZE, PAGE_SIZE), PAGE_SIZE)],
                                     sem.at[slot])

    def start(bb, slot):
        lax.fori_loop(0, n_pages, lambda p, c: (page_copy(bb, slot, p).start(), c)[1], 0)

    def wait(bb, slot):
        lax.fori_loop(0, n_pages, lambda p, c: (page_copy(bb, slot, p).wait(), c)[1], 0)

    slot = b % 2

    @pl.when(b == 0)
    def _():
        start(0, 0)

    @pl.when(b + 1 < nb)
    def _():
        start(b + 1, 1 - slot)

    wait(b, slot)

    r16 = lambda a: a.astype(BF16).astype(F32)
    qi = iq_ref[0].astype(BF16)
    wcol = r16(iw_ref[0]) * (IDX_DIM ** -0.5 * IDX_HEADS ** -0.5)
    s = r16(jnp.maximum(jnp.dot(qi, buf[slot].astype(BF16), preferred_element_type=F32), 0.0))
    o_ref[0, :, 0:past] = jnp.sum(s * wcol, axis=0, keepdims=True)
    sn = r16(jnp.maximum(jnp.dot(qi, iknew_ref[0].astype(BF16), preferred_element_type=F32), 0.0))
    sn = jnp.sum(sn * wcol, axis=0, keepdims=True)
    lane = lax.broadcasted_iota(jnp.int32, (1, LANES), 1)
    o_ref[0, :, past:past + LANES] = jnp.where(lane == 0, sn, -jnp.inf)


def dsa_scores(page_table, iq, iw, iknew_pad, cache_kidx):
    b, n_pages = page_table.shape
    past = n_pages * PAGE_SIZE
    kern = functools.partial(_dsa_scores_kernel, n_pages=n_pages)
    gs = pltpu.PrefetchScalarGridSpec(
        num_scalar_prefetch=1,
        grid=(b,),
        in_specs=[pl.BlockSpec((1, IDX_HEADS, IDX_DIM), lambda i, pt: (i, 0, 0)),
                  pl.BlockSpec((1, IDX_HEADS, 1), lambda i, pt: (i, 0, 0)),
                  pl.BlockSpec((1, IDX_DIM, LANES), lambda i, pt: (i, 0, 0)),
                  pl.BlockSpec(memory_space=pl.ANY)],
        out_specs=pl.BlockSpec((1, 1, past + LANES), lambda i, pt: (i, 0, 0)),
        scratch_shapes=[pltpu.VMEM((2, IDX_DIM, past), F32), pltpu.SemaphoreType.DMA((2,))],
    )
    return pl.pallas_call(
        kern, grid_spec=gs,
        out_shape=jax.ShapeDtypeStruct((b, 1, past + LANES), F32),
        compiler_params=_cparams("arbitrary"),
        name="dsa_scores",
    )(page_table, iq, iw, iknew_pad, cache_kidx)


def _dsa_threshold_kernel(s_ref, thr_ref, *, topk):
    s = s_ref[...]
    nb = s.shape[0]
    finite = s > -jnp.inf
    lo0 = jnp.min(jnp.where(finite, s, jnp.inf), axis=1, keepdims=True)
    hi0 = jnp.max(s, axis=1, keepdims=True)
    cnt0 = jnp.sum(jnp.where(finite, 1.0, 0.0), axis=1, keepdims=True)

    def count_ge(thr):
        return jnp.sum(jnp.where(s_ref[...] >= thr, 1.0, 0.0), axis=1, keepdims=True)

    thr = _bisect_threshold(count_ge, lo0, hi0, cnt0, topk)
    thr_ref[...] = jnp.broadcast_to(thr, (nb, LANES))


def dsa_threshold(scores, topk):
    b, l = scores.shape
    return pl.pallas_call(
        functools.partial(_dsa_threshold_kernel, topk=topk),
        grid=(1,),
        in_specs=[pl.BlockSpec((b, l), lambda i: (0, 0))],
        out_specs=pl.BlockSpec((b, LANES), lambda i: (0, 0)),
        out_shape=jax.ShapeDtypeStruct((b, LANES), F32),
        compiler_params=_cparams("arbitrary"),
        name="dsa_threshold",
    )(scores)


def _dsa_decode_kernel(pt_ref, q_ref, s_ref, thr_ref, knew_ref, vnew_ref, bias_ref, k_hbm, v_hbm, o_ref,
                       kbuf, vbuf, sem, *, n_pages):
    b = pl.program_id(0)
    nb = pl.num_programs(0)
    past = n_pages * PAGE_SIZE

    rpp = PAGE_SIZE * B_KV_HEADS

    def copies(bb, slot):
        ck = _page_copies(pt_ref, bb, n_pages, k_hbm, kbuf.at[slot], sem.at[0, slot], rpp)
        cv = _page_copies(pt_ref, bb, n_pages, v_hbm, vbuf.at[slot], sem.at[1, slot], rpp)
        return ck, cv

    def start(bb, slot):
        ck, cv = copies(bb, slot)
        lax.fori_loop(0, n_pages, lambda p, c: (ck(p).start(), cv(p).start(), c)[2], 0)

    def wait(bb, slot):
        ck, cv = copies(bb, slot)
        lax.fori_loop(0, n_pages, lambda p, c: (ck(p).wait(), cv(p).wait(), c)[2], 0)

    slot = b % 2

    @pl.when(b == 0)
    def _():
        kbuf[:, n_pages * rpp:, :] = jnp.zeros((2, rpp, B_HEAD_DIM), F32)
        vbuf[:, n_pages * rpp:, :] = jnp.zeros((2, rpp, B_HEAD_DIM), F32)
        start(0, 0)

    @pl.when(b + 1 < nb)
    def _():
        start(b + 1, 1 - slot)

    kbuf[slot, n_pages * rpp:n_pages * rpp + 8, :] = knew_ref[0]
    vbuf[slot, n_pages * rpp:n_pages * rpp + 8, :] = vnew_ref[0]
    wait(b, slot)

    sel = s_ref[0] >= thr_ref[0][:, 0:1]
    n_keys = past + PAGE_SIZE
    outs = []
    for n in range(B_KV_HEADS):
        kn = kbuf[slot, pl.ds(n, n_keys, stride=B_KV_HEADS), :].astype(BF16)
        vn = vbuf[slot, pl.ds(n, n_keys, stride=B_KV_HEADS), :].astype(BF16)
        qn = q_ref[0, n].astype(BF16)
        lg = _nt(qn, kn) * B_HEAD_DIM ** -0.5 + bias_ref[n]
        m = jnp.max(jnp.where(sel, lg, NEG), axis=1, keepdims=True)
        p = jnp.where(sel, jnp.exp(lg - m), 0.0)
        p = p / jnp.sum(p, axis=1, keepdims=True)
        outs.append(jnp.dot(p.astype(BF16), vn, preferred_element_type=F32))
    o_ref[0] = jnp.concatenate(outs, axis=0).astype(o_ref.dtype)


def dsa_decode(page_table, q8, scores, thr, knew8, vnew8, bias_rows, cache_k2, cache_v2):
    b, n_pages = page_table.shape
    past = n_pages * PAGE_SIZE
    l = past + LANES
    wkv = B_KV_HEADS * B_HEAD_DIM
    kern = functools.partial(_dsa_decode_kernel, n_pages=n_pages)
    gs = pltpu.PrefetchScalarGridSpec(
        num_scalar_prefetch=1,
        grid=(b,),
        in_specs=[pl.BlockSpec((1, B_KV_HEADS, 8, B_HEAD_DIM), lambda i, pt: (i, 0, 0, 0)),
                  pl.BlockSpec((1, 1, l), lambda i, pt: (i, 0, 0)),
                  pl.BlockSpec((1, 1, LANES), lambda i, pt: (i, 0, 0)),
                  pl.BlockSpec((1, 8, B_HEAD_DIM), lambda i, pt: (i, 0, 0)),
                  pl.BlockSpec((1, 8, B_HEAD_DIM), lambda i, pt: (i, 0, 0)),
                  pl.BlockSpec((B_KV_HEADS, 8, l), lambda i, pt: (0, 0, 0)),
                  pl.BlockSpec(memory_space=pl.ANY),
                  pl.BlockSpec(memory_space=pl.ANY)],
        out_specs=pl.BlockSpec((1, 2 * 8, B_HEAD_DIM), lambda i, pt: (i, 0, 0)),
        scratch_shapes=[pltpu.VMEM((2, l * B_KV_HEADS, B_HEAD_DIM), F32),
                        pltpu.VMEM((2, l * B_KV_HEADS, B_HEAD_DIM), F32),
                        pltpu.SemaphoreType.DMA((2, 2))],
    )
    return pl.pallas_call(
        kern, grid_spec=gs,
        out_shape=jax.ShapeDtypeStruct((b, 2 * 8, B_HEAD_DIM), F32),
        compiler_params=_cparams("arbitrary"),
        name="dsa_decode",
    )(page_table, q8, scores, thr, knew8, vnew8, bias_rows, cache_k2, cache_v2)


def _merge_kernel(ya_ref, yb_ref, ga_ref, gb_ref, wa_ref, wb_ref, o_ref):
    a = _mm(ya_ref[...], wa_ref[...])
    bb = _mm(yb_ref[...], wb_ref[...])
    o_ref[...] = (jax.nn.sigmoid(ga_ref[...]) * a + jax.nn.sigmoid(gb_ref[...]) * bb).astype(o_ref.dtype)


def merge(ya, yb, z, ga_col, gb_col, wa, wb, *, tm, tn):
    m, k = ya.shape
    n = wa.shape[1]
    return pl.pallas_call(
        _merge_kernel,
        grid=(m // tm, n // tn),
        in_specs=[pl.BlockSpec((tm, k), lambda i, j: (i, 0)),
                  pl.BlockSpec((tm, k), lambda i, j: (i, 0)),
                  pl.BlockSpec((tm, tn), lambda i, j: (i, ga_col // tn + j)),
                  pl.BlockSpec((tm, tn), lambda i, j: (i, gb_col // tn + j)),
                  pl.BlockSpec((k, tn), lambda i, j: (0, j)),
                  pl.BlockSpec((k, tn), lambda i, j: (0, j))],
        out_specs=pl.BlockSpec((tm, tn), lambda i, j: (i, j)),
        out_shape=jax.ShapeDtypeStruct((m, n), wa.dtype),
        compiler_params=_cparams("parallel", "arbitrary"),
        name="merge",
    )(ya, yb, z, z, wa, wb)


def _cross_prompt_kernel(x_ref, g_ref, wq_ref, mk_ref, mv_ref, wo_ref, o_ref, *, n_in_tiles):
    x = x_ref[...]
    ms = jnp.mean(x * x, axis=-1, keepdims=True)
    h = (x * lax.rsqrt(ms + EPS) * g_ref[...]).astype(BF16)
    q = jnp.dot(h, wq_ref[...], preferred_element_type=F32)
    outs = []
    for hh in range(X_HEADS):
        cols = slice(hh * X_HEAD_DIM, (hh + 1) * X_HEAD_DIM)
        lg = _nt(q[:, cols].astype(BF16), mk_ref[:, cols]) * X_HEAD_DIM ** -0.5
        mx = jnp.max(lg, axis=1, keepdims=True)
        p = jnp.exp(lg - mx)
        p = p / jnp.sum(p, axis=1, keepdims=True)
        outs.append(jnp.dot(p.astype(BF16), mv_ref[:, cols], preferred_element_type=F32).astype(BF16))
    att = jnp.concatenate(outs, axis=1)
    res = x + jnp.dot(att, wo_ref[...], preferred_element_type=F32)
    o_ref[...] = jnp.where(pl.program_id(0) < n_in_tiles, res, 0.0)


def cross_prompt(x, g, wq, mk, mv, wo, *, tm, out_rows=None):
    m, d = x.shape
    out_rows = m if out_rows is None else out_rows
    n_in = m // tm
    full = lambda a: pl.BlockSpec(a.shape, lambda i: (0,) * a.ndim)
    g2 = g.reshape(1, d)
    return pl.pallas_call(
        functools.partial(_cross_prompt_kernel, n_in_tiles=n_in),
        grid=(pl.cdiv(out_rows, tm),),
        in_specs=[pl.BlockSpec((tm, d), lambda i: (jnp.minimum(i, n_in - 1), 0)),
                  full(g2), full(wq), full(mk), full(mv), full(wo)],
        out_specs=pl.BlockSpec((tm, d), lambda i: (i, 0)),
        out_shape=jax.ShapeDtypeStruct((out_rows, d), F32),
        compiler_params=_cparams("parallel"),
        name="cross_prompt",
    )(x, g2, wq, mk, mv, wo)


def _cross_step_kernel(q_ref, mk_ref, mv_ref, o_ref):
    r16 = lambda a: a.astype(BF16).astype(F32)
    q = r16(q_ref[0])
    outs = []
    for hh in range(X_HEADS):
        cols = slice(hh * X_HEAD_DIM, (hh + 1) * X_HEAD_DIM)
        kh = r16(mk_ref[0, :, cols])
        vh = r16(mv_ref[0, :, cols])
        lg = jnp.sum(kh * q[:, cols], axis=1, keepdims=True) * X_HEAD_DIM ** -0.5
        mx = jnp.max(lg, axis=0, keepdims=True)
        p = jnp.exp(lg - mx)
        p = r16(p / jnp.sum(p, axis=0, keepdims=True))
        outs.append(jnp.sum(p * vh, axis=0, keepdims=True))
    o_ref[0] = jnp.concatenate(outs, axis=1).astype(o_ref.dtype)


def cross_step(q, mk, mv):
    b, w = q.shape
    mem = mk.shape[1]
    return pl.pallas_call(
        _cross_step_kernel,
        grid=(b,),
        in_specs=[pl.BlockSpec((1, 1, w), lambda i: (i, 0, 0)),
                  pl.BlockSpec((1, mem, w), lambda i: (i, 0, 0)),
                  pl.BlockSpec((1, mem, w), lambda i: (i, 0, 0))],
        out_specs=pl.BlockSpec((1, 1, w), lambda i: (i, 0, 0)),
        out_shape=jax.ShapeDtypeStruct((b, 1, w), F32),
        compiler_params=_cparams("arbitrary"),
        name="cross_step",
    )(q.reshape(b, 1, w), mk, mv)


def _pack_bf16_pairs(x):
    c = x.shape[1] // 2
    u = pltpu.bitcast(x.astype(BF16).astype(F32), jnp.uint32)
    return lax.shift_right_logical(u[:, :c], jnp.uint32(16)) | u[:, c:]


def _unpack_bf16_pairs(u):
    lo = pltpu.bitcast(lax.shift_left(u, jnp.uint32(16)), F32)
    hi = pltpu.bitcast(u & jnp.uint32(0xFFFF0000), F32)
    return jnp.concatenate([lo, hi], axis=1)


def _router_kernel(x_ref, g_ref, w_ref, b_ref, hf_ref, route_ref):
    x = x_ref[...]
    ms = jnp.mean(x * x, axis=-1, keepdims=True)
    hf = x * lax.rsqrt(ms + EPS) * g_ref[...]
    hf_ref[...] = _pack_bf16_pairs(hf)
    lg = _mm(hf, w_ref[...]) + b_ref[...]
    tm = lg.shape[0]
    lane = lax.broadcasted_iota(jnp.int32, (tm, LANES), 1)
    big = jnp.int32(LANES)
    is_g = lane < N_GROUPS
    gmax = jnp.max(jnp.where(is_g, lg, -jnp.inf), axis=1, keepdims=True)
    grp = jnp.min(jnp.where(is_g & (lg == gmax), lane, big), axis=1, keepdims=True)
    p_grp = 1.0 / jnp.sum(jnp.where(is_g, jnp.exp(lg - gmax), 0.0), axis=1, keepdims=True)
    e_lo = N_GROUPS + grp * EXP_PER_GROUP
    in_g = (lane >= e_lo) & (lane < e_lo + EXP_PER_GROUP)
    v1 = jnp.max(jnp.where(in_g, lg, -jnp.inf), axis=1, keepdims=True)
    i1 = jnp.min(jnp.where(in_g & (lg == v1), lane, big), axis=1, keepdims=True)
    rest = in_g & (lane != i1)
    v2 = jnp.max(jnp.where(rest, lg, -jnp.inf), axis=1, keepdims=True)
    i2 = jnp.min(jnp.where(rest & (lg == v2), lane, big), axis=1, keepdims=True)
    e2 = jnp.exp(v2 - v1)
    g1 = p_grp / (1.0 + e2)
    g2 = p_grp * e2 / (1.0 + e2)
    r = jnp.where(lane == 0, (i1 - N_GROUPS).astype(F32),
                  jnp.where(lane == 1, (i2 - N_GROUPS).astype(F32),
                            jnp.where(lane == 2, g1, jnp.where(lane == 3, g2, 0.0))))
    route_ref[...] = r


def router(x, g, w_pad, b_pad, *, tm):
    m, d = x.shape
    return pl.pallas_call(
        _router_kernel,
        grid=(pl.cdiv(m, tm),),
        in_specs=[pl.BlockSpec((tm, d), lambda i: (i, 0)),
                  pl.BlockSpec((1, d), lambda i: (0, 0)),
                  pl.BlockSpec((d, LANES), lambda i: (0, 0)),
                  pl.BlockSpec((1, LANES), lambda i: (0, 0))],
        out_specs=[pl.BlockSpec((tm, d // 2), lambda i: (i, 0)),
                   pl.BlockSpec((tm, LANES), lambda i: (i, 0))],
        out_shape=[jax.ShapeDtypeStruct((m, d // 2), jnp.uint32), jax.ShapeDtypeStruct((m, LANES), F32)],
        compiler_params=_cparams("parallel"),
        name="router",
    )(x, g.reshape(1, d), w_pad, b_pad)


def _moe_kernel(be_ref, nxt_ref, nused_ref, sbase_ref, nvalid_ref, padj_ref, order_ref, hf_hbm, wg_hbm, wu_hbm,
                wd_hbm, y_hbm, xbuf, obuf, wg_f, wu_f, wd_f, wg_s, wu_s, wd_s, sem_in, sem_out, sem_w,
                *, n_tokens):
    b = pl.program_id(0)
    nb = pl.num_programs(0)
    blk = xbuf.shape[1]
    xslot = b % 3
    used = b < nused_ref[0]
    next_used = b + 2 < nused_ref[0]

    def slot_info(bb):
        base, nv, pad0 = sbase_ref[bb], nvalid_ref[bb], padj_ref[bb]

        def info(r):
            asg = order_ref[base + r]
            tok = lax.shift_right_logical(asg, 1)
            return tok, jnp.where(r < nv, (asg & 1) * n_tokens + tok, pad0 + r)
        return info

    def weight_copies(e):
        return (pltpu.make_async_copy(wg_hbm.at[e], wg_f, sem_w.at[0]),
                pltpu.make_async_copy(wu_hbm.at[e], wu_f, sem_w.at[1]),
                pltpu.make_async_copy(wd_hbm.at[e], wd_f, sem_w.at[2]))

    def gather_start(bb, sl):
        info = slot_info(bb)
        for r in range(blk):
            pltpu.make_async_copy(hf_hbm.at[pl.ds(info(r)[0], 1)], xbuf.at[sl, pl.ds(r, 1)],
                                  sem_in.at[sl]).start()

    def scatter_start(bb, sl):
        info = slot_info(bb)
        for r in range(blk):
            pltpu.make_async_copy(obuf.at[sl, pl.ds(r, 1)], y_hbm.at[pl.ds(info(r)[1], 1)],
                                  sem_out.at[sl]).start(priority=r % 2)

    def gather_wait(sl):
        pltpu.make_async_copy(hf_hbm.at[pl.ds(0, blk)], xbuf.at[sl], sem_in.at[sl]).wait()

    def scatter_wait(sl):
        pltpu.make_async_copy(obuf.at[sl], y_hbm.at[pl.ds(0, blk)], sem_out.at[sl]).wait()

    @pl.when(b == 0)
    def _():
        for cp in weight_copies(be_ref[0]):
            cp.start(priority=1)
        gather_start(0, 0)
        obuf[...] = jnp.zeros(obuf.shape, obuf.dtype)

    @pl.when(jnp.logical_and(b == 0, 1 < nused_ref[0]))
    def _():
        gather_start(1, 1)

    changed = jnp.logical_and(used, jnp.logical_or(b == 0, be_ref[b] != be_ref[jnp.maximum(b - 1, 0)]))

    @pl.when(changed)
    def _():
        for cp in weight_copies(be_ref[b]):
            cp.wait()
        wg_s[...] = wg_f[...].astype(BF16)
        wu_s[...] = wu_f[...].astype(BF16)
        wd_s[...] = wd_f[...].astype(BF16)

    @pl.when(jnp.logical_and(changed, nxt_ref[b] >= 0))
    def _():
        for cp in weight_copies(nxt_ref[b]):
            cp.start(priority=1)

    @pl.when(used)
    def _():
        gather_wait(xslot)

    @pl.when(b >= 3)
    def _():
        scatter_wait(xslot)

    def step(prefetch, flush_prev, compute):
        if prefetch:
            gather_start(b + 2, (b + 2) % 3)
        if flush_prev:
            scatter_start(b - 1, (b + 2) % 3)
        if compute:
            x = _unpack_bf16_pairs(xbuf[xslot]).astype(BF16)
            gg = jnp.dot(x, wg_s[...], preferred_element_type=F32)
            uu = jnp.dot(x, wu_s[...], preferred_element_type=F32)
            a = (_silu(gg) * uu).astype(BF16)
            obuf[xslot] = _pack_bf16_pairs(jnp.dot(a, wd_s[...], preferred_element_type=F32))

    first, last = b == 0, b == nb - 1
    land, lnot = jnp.logical_and, jnp.logical_not
    pl.when(land(first, next_used))(lambda: step(True, False, True))
    pl.when(land(first, lnot(next_used)))(lambda: step(False, False, True))
    pl.when(land(lnot(first), land(used, next_used)))(lambda: step(True, True, True))
    pl.when(land(lnot(first), land(used, lnot(next_used))))(lambda: step(False, True, True))
    pl.when(land(lnot(first), lnot(used)))(lambda: step(False, True, False))

    @pl.when(last)
    def _():
        scatter_start(b, xslot)
        scatter_wait(xslot)

    @pl.when(jnp.logical_and(last, b >= 1))
    def _():
        scatter_wait((b + 2) % 3)

    @pl.when(jnp.logical_and(last, b >= 2))
    def _():
        scatter_wait((b + 1) % 3)


def moe_experts(tables, hf, w_g, w_u, w_d, *, blk, out_rows):
    n_blocks = tables[0].shape[0]
    dp = hf.shape[1]
    d = 2 * dp
    ff = w_g.shape[2]
    gs = pltpu.PrefetchScalarGridSpec(
        num_scalar_prefetch=len(tables),
        grid=(n_blocks,),
        in_specs=[pl.BlockSpec(memory_space=pl.ANY)] * 4,
        out_specs=pl.BlockSpec(memory_space=pl.ANY),
        scratch_shapes=[pltpu.VMEM((3, blk, dp), jnp.uint32), pltpu.VMEM((3, blk, dp), jnp.uint32),
                        pltpu.VMEM((d, ff), F32), pltpu.VMEM((d, ff), F32), pltpu.VMEM((ff, d), F32),
                        pltpu.VMEM((d, ff), BF16), pltpu.VMEM((d, ff), BF16), pltpu.VMEM((ff, d), BF16),
                        pltpu.SemaphoreType.DMA((3,)), pltpu.SemaphoreType.DMA((3,)),
                        pltpu.SemaphoreType.DMA((3,))],
    )
    return pl.pallas_call(
        functools.partial(_moe_kernel, n_tokens=hf.shape[0]), grid_spec=gs,
        out_shape=jax.ShapeDtypeStruct((out_rows, dp), jnp.uint32),
        compiler_params=_cparams("arbitrary"),
        name="moe_experts",
    )(*tables, hf, w_g, w_u, w_d)


def _combine_kernel(x_ref, route_ref, gf_ref, y1_ref, y2_ref, op_ref, os_ref):
    i = pl.program_id(0)
    route = route_ref[...]
    x = (x_ref[...] + route[:, 2:3] * _unpack_bf16_pairs(y1_ref[...])
         + route[:, 3:4] * _unpack_bf16_pairs(y2_ref[...]))
    ms = jnp.mean(x * x, axis=-1, keepdims=True)
    out = x * lax.rsqrt(ms + EPS) * gf_ref[...]

    @pl.when(i < pl.num_programs(0) - 1)
    def _():
        op_ref[...] = out

    @pl.when(i == pl.num_programs(0) - 1)
    def _():
        os_ref[...] = out[:os_ref.shape[0]]


def combine(x, route, gf, y, plane, n_prompt, *, tm):
    m, d = x.shape
    n_tiles = n_prompt // tm
    assert n_prompt % tm == 0 and 0 < m - n_prompt <= tm
    return pl.pallas_call(
        _combine_kernel,
        grid=(n_tiles + 1,),
        in_specs=[pl.BlockSpec((tm, d), lambda i: (i, 0)),
                  pl.BlockSpec((tm, LANES), lambda i: (i, 0)),
                  pl.BlockSpec((1, d), lambda i: (0, 0)),
                  pl.BlockSpec((tm, d // 2), lambda i: (i, 0)),
                  pl.BlockSpec((tm, d // 2), lambda i: (plane // tm + i, 0))],
        out_specs=[pl.BlockSpec((tm, d), lambda i: (jnp.minimum(i, n_tiles - 1), 0)),
                   pl.BlockSpec((m - n_prompt, d), lambda i: (0, 0))],
        out_shape=[jax.ShapeDtypeStruct((n_prompt, d), F32), jax.ShapeDtypeStruct((m - n_prompt, d), F32)],
        compiler_params=_cparams("arbitrary"),
        name="combine",
    )(x, route, gf.reshape(1, d), y, y)


def _t5_bucket(dist):
    dist = jnp.asarray(dist, jnp.int32)
    max_exact = REL_BUCKETS // 2
    dist_f = jnp.maximum(dist, 1).astype(F32)
    large = max_exact + (jnp.log(dist_f / max_exact) / math.log(REL_MAX_DIST / max_exact)
                         * (REL_BUCKETS - max_exact)).astype(jnp.int32)
    large = jnp.minimum(large, REL_BUCKETS - 1)
    return jnp.where(dist < max_exact, dist, large)


def _bias_tables(rel_bias, past):
    r = np.arange(LANES)
    diff = r[:, None] - r[None, :]
    buckets = jnp.stack([_t5_bucket(np.maximum(diff, 0)),
                         _t5_bucket(np.maximum(diff + LANES, 0)),
                         _t5_bucket(np.full((LANES, LANES), 2 * LANES))])
    def lookup(bkt):
        oh = (bkt.reshape(-1, 1) == jnp.arange(REL_BUCKETS)[None, :]).astype(F32)
        out = jnp.dot(oh, rel_bias.astype(F32), precision=lax.Precision.HIGHEST)
        return out.T.reshape((rel_bias.shape[1],) + bkt.shape)

    tiles = lookup(buckets)
    tiles = tiles - tiles[:, 2:3]
    dist = np.maximum(past - np.arange(past + LANES), 0)
    rows = lookup(_t5_bucket(dist))
    rows = rows.reshape(B_KV_HEADS, B_GROUP, past + LANES)
    rows = jnp.concatenate([rows, jnp.zeros_like(rows)], axis=1)
    return tiles, rows


def _dispatch(eid, n_tokens, blk):
    a = eid.shape[0]
    assert EXPERT_TOPK == 2 and a == EXPERT_TOPK * n_tokens
    n_blocks = -(-(a + N_EXPERTS * (blk - 1)) // blk)
    rows = n_blocks * blk
    order = jnp.argsort(eid).astype(jnp.int32)
    counts = jnp.sum(eid[:, None] == jnp.arange(N_EXPERTS)[None, :], axis=0).astype(jnp.int32)
    cum = jnp.cumsum(counts)
    starts = cum - counts
    padded = (counts + blk - 1) // blk * blk
    pad_end = jnp.cumsum(padded)
    pad_start = pad_end - padded
    blocks = jnp.arange(n_blocks, dtype=jnp.int32)
    block_e = jnp.minimum(jnp.sum(pad_end[None, :] <= (blocks * blk)[:, None], axis=1), N_EXPERTS - 1)
    n_used = pad_end[-1] // blk
    off = blocks * blk - pad_start[block_e]
    sbase = jnp.where(blocks < n_used, starts[block_e] + off, 0)
    nvalid = jnp.where(blocks < n_used, jnp.clip(counts[block_e] - off, 0, blk), 0)
    padj = a + blocks * blk - cum[block_e]
    new_run = jnp.concatenate([jnp.array([True]), block_e[1:] != block_e[:-1]])
    run_start = jnp.where(new_run & (blocks < n_used), blocks, n_blocks)
    nxt = jnp.concatenate([jnp.flip(lax.cummin(jnp.flip(run_start)))[1:], jnp.array([n_blocks])])
    next_e = jnp.where(nxt < n_blocks, block_e[jnp.minimum(nxt, n_blocks - 1)], -1)
    i32 = lambda v: v.astype(jnp.int32)
    order_pad = jnp.pad(order, (0, blk))
    return (i32(block_e), i32(next_e), i32(n_used).reshape(1), i32(sbase), i32(nvalid), i32(padj), order_pad), rows


def kernel(x_prompt, x_sample, mem_prompt, cache_k, cache_v, cache_kidx, page_table, state_hgrn, cache_mem_k,
           cache_mem_v, norm_mix, w_in, hgrn_lb_logits, hgrn_norm, w_branch_a, w_branch_b, w_out, norm_cross, w_xq,
           w_xk, w_xv, w_xo, norm_ffn, w_router_group, b_router_group, w_router_expert, b_router_expert, w_exp_gate,
           w_exp_up, w_exp_down, rel_bias, norm_final):
    assert w_in.shape[0] == 1, "single-layer step"
    l = 0
    drop0 = lambda a: a.reshape(a.shape[1:])
    bp, t, d = x_prompt.shape
    db = x_sample.shape[0]
    past = page_table.shape[1] * PAGE_SIZE
    xp = x_prompt.reshape(bp * t, d)
    xs = x_sample.reshape(db, d)

    wi_t = jnp.transpose(drop0(w_in)).astype(BF16)
    assert wi_t.shape[0] - NZ_MAIN + T_PAD == NZ_TAIL
    w_tail_t = jnp.pad(wi_t[NZ_MAIN:], ((T_PAD, 0), (0, 0)))
    wa, wb, wo = w_branch_a[l].astype(BF16), w_branch_b[l].astype(BF16), w_out[l].astype(BF16)
    wxq, wxk, wxv, wxo = (w_xq[l].astype(BF16), w_xk[l].astype(BF16), w_xv[l].astype(BF16), w_xo[l].astype(BF16))
    w_route = jnp.pad(jnp.concatenate([w_router_group[l], w_router_expert[l]], axis=1),
                      ((0, 0), (0, LANES - N_GROUPS - N_EXPERTS))).astype(BF16)
    b_route = jnp.pad(jnp.concatenate([b_router_group[l], b_router_expert[l]]),
                      (0, LANES - N_GROUPS - N_EXPERTS)).reshape(1, LANES)
    bias_tiles, bias_rows = _bias_tables(rel_bias, past)

    zp, zpt = in_proj(xp, norm_mix[l], wi_t, NZ_MAIN, w_tail_t, tm=TM_WIDE, tn=TN)
    kp = zp[:, C_BK:C_BK + KV_W]
    vp = zp[:, C_BV:C_BV + KV_W]
    ikp = zpt[:, T_SM + IK_LANE:T_SM + IK_LANE + IDX_DIM]
    ya_p, st_p = hgrn_prompt(zp, hgrn_lb_logits, hgrn_norm[l])
    yb_p = dsa_prompt(zp, zpt, ikp.astype(BF16), kp.astype(BF16), vp.astype(BF16), bias_tiles)
    mg_p = merge(ya_p, yb_p, zpt, T_GA, T_GB, wa, wb, tm=TM_WIDE, tn=TN)
    x1p = matmul(mg_p, wo, xp, tm=TM_WIDE, tn=TN)
    memp = mem_prompt.reshape(-1, d)
    mk = matmul(memp, wxk, tm=memp.shape[0], tn=TN)
    mv = matmul(memp, wxv, tm=memp.shape[0], tn=TN)
    n = -(-(bp * t + db) // TM_TOKEN) * TM_TOKEN
    x2p = cross_prompt(x1p, norm_cross[l], wxq, mk.astype(BF16), mv.astype(BF16), wxo, tm=TM_CROSS, out_rows=n)

    zs, zst = in_proj(xs, norm_mix[l], wi_t, NZ_MAIN, w_tail_t, tm=db, tn=TN)
    ks = zs[:, C_BK:C_BK + KV_W]
    vs = zs[:, C_BV:C_BV + KV_W]
    iks = zst[:, T_SM + IK_LANE:T_SM + IK_LANE + IDX_DIM]
    ya_s, st_s = hgrn_step(zs[:, :4 * A_HEADS * A_DK], hgrn_lb_logits, hgrn_norm[l], drop0(state_hgrn))
    iq_s = zs[:, C_IQ:C_IQ + IDX_HEADS * IDX_DIM].reshape(db, IDX_HEADS, IDX_DIM)
    iw_s = zst[:, T_SM + IW_LANE:T_SM + IW_LANE + IDX_HEADS].reshape(db, IDX_HEADS, 1)
    iknew_pad = jnp.pad(iks[:, :, None], ((0, 0), (0, 0), (0, LANES - 1)))
    scores = dsa_scores(page_table, iq_s, iw_s, iknew_pad,
                        jnp.swapaxes(drop0(cache_kidx), 1, 2)).reshape(db, past + LANES)
    topk_s = min(TOPK_MAX, (past + 1) // 4)
    thr = dsa_threshold(scores, topk_s)
    q8 = jnp.pad(zs[:, C_BQ:C_BQ + B_HEADS * B_HEAD_DIM].reshape(db, B_KV_HEADS, B_GROUP, B_HEAD_DIM),
                 ((0, 0), (0, 0), (0, 8 - B_GROUP), (0, 0)))
    knew8 = jnp.pad(ks.reshape(db, B_KV_HEADS, B_HEAD_DIM), ((0, 0), (0, 8 - B_KV_HEADS), (0, 0)))
    vnew8 = jnp.pad(vs.reshape(db, B_KV_HEADS, B_HEAD_DIM), ((0, 0), (0, 8 - B_KV_HEADS), (0, 0)))
    n_pool = cache_k.shape[1]
    ob = dsa_decode(page_table, q8, scores.reshape(db, 1, -1), thr.reshape(db, 1, LANES), knew8, vnew8, bias_rows,
                    cache_k.reshape(n_pool, PAGE_SIZE * B_KV_HEADS, B_HEAD_DIM),
                    cache_v.reshape(n_pool, PAGE_SIZE * B_KV_HEADS, B_HEAD_DIM))
    yb_s = ob.reshape(db, B_KV_HEADS, 8, B_HEAD_DIM)[:, :, :B_GROUP].reshape(db, B_HEADS * B_HEAD_DIM)
    mg_s = merge(ya_s.reshape(db, -1), yb_s, zst, T_GA, T_GB, wa, wb, tm=db, tn=TN)
    x1s = matmul(mg_s, wo, xs, tm=db, tn=TN)
    qx_s = norm_matmul(x1s, norm_cross[l], wxq, tm=db, tn=TN)
    mem = cache_mem_k.shape[2]
    att_s = cross_step(qx_s, cache_mem_k.reshape(db, mem, -1), cache_mem_v.reshape(db, mem, -1))
    x2 = matmul(att_s.reshape(db, -1), wxo, x1s, tm=db, tn=TN, into=(x2p, bp * t))

    hf, route = router(x2, norm_ffn[l], w_route, b_route, tm=TM_ROUTER)
    eid = route[:, :EXPERT_TOPK].astype(jnp.int32).reshape(-1)
    tables, rows = _dispatch(eid, n, MOE_ROWS)
    ye = moe_experts(tables, hf, drop0(w_exp_gate), drop0(w_exp_up), drop0(w_exp_down), blk=MOE_ROWS, out_rows=rows)
    y_p, y_s = combine(x2, route, norm_final, ye, n, bp * t, tm=TM_TOKEN)

    y_prompt = y_p.reshape(bp, t, d)
    y_sample = y_s[:db].reshape(db, 1, d)
    return (y_prompt, y_sample,
            kp.reshape(1, bp, t, B_KV_HEADS, B_HEAD_DIM), vp.reshape(1, bp, t, B_KV_HEADS, B_HEAD_DIM),
            ikp.reshape(1, bp, t, IDX_DIM),
            jnp.swapaxes(st_p, 1, 2).reshape(1, bp, A_HEADS, A_DK, A_DV),
            mk.reshape(1, bp, -1, X_HEADS, X_HEAD_DIM), mv.reshape(1, bp, -1, X_HEADS, X_HEAD_DIM),
            ks.reshape(1, db, 1, B_KV_HEADS, B_HEAD_DIM), vs.reshape(1, db, 1, B_KV_HEADS, B_HEAD_DIM),
            iks.reshape(1, db, 1, IDX_DIM),
            st_s.reshape(1, db, A_HEADS, A_DK, A_DV))
```

```python
import functools
import math

import jax
import jax.numpy as jnp
import numpy as np
from jax import lax
from jax.experimental import pallas as pl
from jax.experimental.pallas import tpu as pltpu

F32 = jnp.float32
BF16 = jnp.bfloat16
EPS = 1e-6

D_MODEL = 2048
A_HEADS, A_DK, A_DV = 8, 128, 128
B_HEADS, B_KV_HEADS, B_HEAD_DIM = 8, 2, 128
B_GROUP = B_HEADS // B_KV_HEADS
IDX_HEADS, IDX_DIM = 16, 64
TOPK_MAX = 256
PAGE_SIZE = 128
REL_BUCKETS, REL_MAX_DIST = 32, 128
X_HEADS, X_HEAD_DIM = 4, 128
N_GROUPS, EXP_PER_GROUP = 4, 8
N_EXPERTS = N_GROUPS * EXP_PER_GROUP
EXPERT_TOPK = 2
EXPERT_FF = 512
MOE_ROWS = 256

LANES = 128
VMEM_LIMIT = 56 * 1024 * 1024

TM_WIDE = 1024
TM_CROSS = 512
TM_ROUTER = 256
TM_TOKEN = 128
TN = 512
KV_W = B_KV_HEADS * B_HEAD_DIM

NEG = -1e30

C_AQ, C_AF, C_AI, C_AG, C_BQ, C_BK, C_BV, C_IQ = 0, 1024, 2048, 3072, 4096, 5120, 5376, 5632
NZ_MAIN = 6656
T_PAD = 432
T_SM, IW_LANE, IK_LANE = 384, 48, 64
T_GA, T_GB = 512, 2560
NZ_TAIL = 4608


def _cparams(*sem):
    return pltpu.CompilerParams(dimension_semantics=sem, vmem_limit_bytes=VMEM_LIMIT)


def _silu(x):
    return x * jax.nn.sigmoid(x)


def _nt(a, b):
    return lax.dot_general(a, b, (((1,), (1,)), ((), ())), preferred_element_type=F32)


def _mm(a, w):
    return jnp.dot(a.astype(BF16), w.astype(BF16), preferred_element_type=F32)


def _norm_matmul_kernel(x_ref, g_ref, w_ref, o_ref, h_ref):
    @pl.when(pl.program_id(1) == 0)
    def _():
        x = x_ref[...]
        ms = jnp.mean(x * x, axis=-1, keepdims=True)
        h_ref[...] = (x * lax.rsqrt(ms + EPS) * g_ref[...]).astype(h_ref.dtype)

    o_ref[...] = _mm(h_ref[...], w_ref[...])


def norm_matmul(x, g, w, *, tm, tn):
    m, k = x.shape
    n = w.shape[1]
    assert n % tn == 0
    return pl.pallas_call(
        _norm_matmul_kernel,
        grid=(m // tm, n // tn),
        in_specs=[pl.BlockSpec((tm, k), lambda i, j: (i, 0)),
                  pl.BlockSpec((1, k), lambda i, j: (0, 0)),
                  pl.BlockSpec((k, tn), lambda i, j: (0, j))],
        out_specs=pl.BlockSpec((tm, tn), lambda i, j: (i, j)),
        out_shape=jax.ShapeDtypeStruct((m, n), F32),
        scratch_shapes=[pltpu.VMEM((tm, k), BF16)],
        compiler_params=_cparams("parallel", "arbitrary"),
        name="norm_matmul",
    )(x, g.reshape(1, k), w)


def _in_proj_kernel(x_ref, g_ref, wa_ref, wb_ref, oa_ref, ob_ref, h_ref, *, na):
    j = pl.program_id(1)

    @pl.when(j == 0)
    def _():
        x = x_ref[...]
        ms = jnp.mean(x * x, axis=-1, keepdims=True)
        h_ref[...] = (x * lax.rsqrt(ms + EPS) * g_ref[...]).astype(h_ref.dtype)

    @pl.when(j < na)
    def _():
        oa_ref[...] = _nt(h_ref[...], wa_ref[...])

    @pl.when(j >= na)
    def _():
        ob_ref[...] = _nt(h_ref[...], wb_ref[...])


def in_proj(x, g, wa_t, a_rows, wb_t, *, tm, tn):
    m, k = x.shape
    na, nb = a_rows // tn, wb_t.shape[0] // tn
    assert a_rows % tn == 0 and wb_t.shape[0] % tn == 0
    a_idx = lambda j: jnp.minimum(j, na - 1)
    b_idx = lambda j: jnp.maximum(j - na, 0)
    return pl.pallas_call(
        functools.partial(_in_proj_kernel, na=na),
        grid=(m // tm, na + nb),
        in_specs=[pl.BlockSpec((tm, k), lambda i, j: (i, 0)),
                  pl.BlockSpec((1, k), lambda i, j: (0, 0)),
                  pl.BlockSpec((tn, k), lambda i, j: (a_idx(j), 0)),
                  pl.BlockSpec((tn, k), lambda i, j: (b_idx(j), 0))],
        out_specs=[pl.BlockSpec((tm, tn), lambda i, j: (i, a_idx(j))),
                   pl.BlockSpec((tm, tn), lambda i, j: (i, b_idx(j)))],
        out_shape=[jax.ShapeDtypeStruct((m, na * tn), F32), jax.ShapeDtypeStruct((m, nb * tn), F32)],
        scratch_shapes=[pltpu.VMEM((tm, k), BF16)],
        compiler_params=_cparams("parallel", "arbitrary"),
        name="in_proj",
    )(x, g.reshape(1, k), wa_t, wb_t)


def _matmul_res_kernel(x_ref, w_ref, r_ref, o_ref):
    o_ref[...] = r_ref[...] + _mm(x_ref[...], w_ref[...])


def _matmul_res_into_kernel(x_ref, w_ref, r_ref, buf_ref, o_ref):
    del buf_ref
    o_ref[...] = r_ref[...] + _mm(x_ref[...], w_ref[...])


def _matmul_kernel(x_ref, w_ref, o_ref):
    o_ref[...] = _mm(x_ref[...], w_ref[...])


def matmul(x, w, res=None, *, tm, tn, into=None):
    m, k = x.shape
    n = w.shape[1]
    in_specs = [pl.BlockSpec((tm, k), lambda i, j: (i, 0)),
                pl.BlockSpec((k, tn), lambda i, j: (0, j))]
    args = [x, w]
    kern = _matmul_kernel
    if res is not None:
        in_specs.append(pl.BlockSpec((tm, tn), lambda i, j: (i, j)))
        args.append(res)
        kern = _matmul_res_kernel
    out_shape, row_blk, aliases = jax.ShapeDtypeStruct((m, n), F32), 0, {}
    if into is not None:
        buf, row0 = into
        assert res is not None and row0 % tm == 0 and buf.shape[1] == n
        in_specs.append(pl.BlockSpec(memory_space=pl.ANY))
        args.append(buf)
        kern = _matmul_res_into_kernel
        out_shape, row_blk, aliases = jax.ShapeDtypeStruct(buf.shape, F32), row0 // tm, {len(args) - 1: 0}
    return pl.pallas_call(
        kern,
        grid=(m // tm, n // tn),
        in_specs=in_specs,
        out_specs=pl.BlockSpec((tm, tn), lambda i, j: (row_blk + i, j)),
        out_shape=out_shape,
        input_output_aliases=aliases,
        compiler_params=_cparams("parallel", "arbitrary"),
        name="matmul",
    )(*args)


HG_TB = 128
HG_C = 16
HG_H = HG_C // 2


def _hgrn_prompt_kernel(aq_ref, af_ref, ai_ref, ag_ref, lbl_ref, ng_ref, ya_ref, st_out_ref,
                        st_ref, q_s, k_s, g_s, v_s):
    t = pl.program_id(0)

    @pl.when(t == 0)
    def _():
        st_ref[...] = jnp.zeros_like(st_ref)

    lbl = lbl_ref[...]
    mx = jnp.max(lbl, axis=0, keepdims=True)
    ex = jnp.exp(lbl - mx)
    lb = ex[0:1, :] / jnp.sum(ex, axis=0, keepdims=True)

    f = lb + (1.0 - lb) * jax.nn.sigmoid(af_ref[...])
    logf = jnp.log(f)
    row = lax.broadcasted_iota(jnp.int32, (HG_TB, HG_TB), 0)
    col = lax.broadcasted_iota(jnp.int32, (HG_TB, HG_TB), 1)
    tri = jnp.where((row // HG_C == col // HG_C) & (col <= row), 1.0, 0.0).astype(BF16)
    g = jnp.zeros(logf.shape, F32)
    rem = logf
    for _ in range(3):
        part = rem.astype(BF16)
        g = g + jnp.dot(tri, part, preferred_element_type=F32)
        rem = rem - part.astype(F32)
    g_s[...] = g
    q_s[...] = _silu(aq_ref[...])
    k_s[...] = 1.0 - f
    v_s[...] = ai_ref[...]

    sub = lax.broadcasted_iota(jnp.int32, (HG_C, A_DK), 0)
    sub8 = lax.broadcasted_iota(jnp.int32, (HG_H, A_DK), 0)
    ng = ng_ref[...]

    def chunk(c, carry):
        r0 = pl.multiple_of(c * HG_C, HG_C)
        rows = pl.ds(r0, HG_C)
        for h in range(A_HEADS):
            cols = slice(h * A_DK, (h + 1) * A_DK)
            g = g_s[rows, cols]
            qh = q_s[rows, cols]
            kh = k_s[rows, cols]
            vh = v_s[rows, cols]
            halves = []
            for hb in range(2):
                rs = slice(hb * HG_H, (hb + 1) * HG_H)
                gb, qb, kb, vb = g[rs], qh[rs], kh[rs], vh[rs]
                ob = jnp.zeros((HG_H, A_DV), F32)
                for tt in range(HG_H):
                    d = gb[tt:tt + 1, :] - gb
                    e = jnp.exp(jnp.where(sub8 <= tt, d, -jnp.inf))
                    p = e * (qb[tt:tt + 1, :] * kb)
                    a_col = jnp.sum(p, axis=1, keepdims=True)
                    o_row = jnp.sum(a_col * vb, axis=0, keepdims=True)
                    ob = jnp.where(sub8 == tt, o_row, ob)
                halves.append(ob)
            o = jnp.concatenate(halves, axis=0)
            low = sub < HG_H
            g_mid = g[HG_H - 1:HG_H, :]
            q_hi = jnp.where(low, 0.0, qh * jnp.exp(jnp.minimum(g - g_mid, 0.0)))
            k_lo = jnp.where(low, kh * jnp.exp(jnp.minimum(g_mid - g, 0.0)), 0.0)
            st = st_ref[h]
            g_last = g[HG_C - 1:HG_C, :]
            kt = kh * jnp.exp(g_last - g)
            upd = lax.dot_general(vh.astype(BF16), jnp.concatenate([kt, k_lo], axis=1).astype(BF16),
                                  (((0,), (0,)), ((), ())), preferred_element_type=F32)
            lhs = jnp.concatenate([qh * jnp.exp(g), q_hi], axis=1).astype(BF16)
            rhs = jnp.concatenate([st, upd[:, A_DK:]], axis=1).astype(BF16)
            o = o + _nt(lhs, rhs)
            st_ref[h] = st * jnp.exp(g_last) + upd[:, :A_DK]
            on = o * lax.rsqrt(jnp.mean(o * o, axis=-1, keepdims=True) + EPS) * ng
            ya_ref[rows, cols] = (on * _silu(ag_ref[rows, cols])).astype(ya_ref.dtype)
        return carry

    lax.fori_loop(0, HG_TB // HG_C, chunk, 0, unroll=2)

    @pl.when(t == pl.num_programs(0) - 1)
    def _():
        st_out_ref[...] = st_ref[...]


def hgrn_prompt(z, lb_logits, norm_g):
    m = z.shape[0]
    w = A_HEADS * A_DK

    def zspec(cb):
        return pl.BlockSpec((HG_TB, w), lambda t, cb=cb: (t, cb))

    return pl.pallas_call(
        _hgrn_prompt_kernel,
        grid=(m // HG_TB,),
        in_specs=[zspec(C_AQ // w), zspec(C_AF // w), zspec(C_AI // w), zspec(C_AG // w),
                  pl.BlockSpec(lb_logits.shape, lambda t: (0, 0)),
                  pl.BlockSpec((1, A_DV), lambda t: (0, 0))],
        out_specs=[pl.BlockSpec((HG_TB, w), lambda t: (t, 0)),
                   pl.BlockSpec((A_HEADS, A_DV, A_DK), lambda t: (0, 0, 0))],
        out_shape=[jax.ShapeDtypeStruct((m, w), BF16),
                   jax.ShapeDtypeStruct((A_HEADS, A_DV, A_DK), F32)],
        scratch_shapes=[pltpu.VMEM((A_HEADS, A_DV, A_DK), F32)] + [pltpu.VMEM((HG_TB, w), F32)] * 4,
        compiler_params=_cparams("arbitrary"),
        name="hgrn_prompt",
    )(z, z, z, z, lb_logits, norm_g.reshape(1, A_DV))


def _hgrn_step_kernel(z_ref, lbl_ref, ng_ref, s_ref, ya_ref, s_out_ref):
    lbl = lbl_ref[...]
    mx = jnp.max(lbl, axis=0, keepdims=True)
    ex = jnp.exp(lbl - mx)
    lb = ex[0:1, :] / jnp.sum(ex, axis=0, keepdims=True)
    z = z_ref[0]
    w = A_HEADS * A_DK
    q = _silu(z[:, 0:w])
    f = lb + (1.0 - lb) * jax.nn.sigmoid(z[:, w:2 * w])
    kk = 1.0 - f
    v = z[:, 2 * w:3 * w]
    ag = z[:, 3 * w:4 * w]
    rows = []
    for h in range(A_HEADS):
        cols = slice(h * A_DK, (h + 1) * A_DK)
        rows += [f[:, cols], kk[:, cols], q[:, cols]]
    rows.append(jnp.zeros((LANES - 3 * A_HEADS, A_DK), F32))
    xt = jnp.concatenate(rows, axis=0).T
    ng = ng_ref[...]
    r16 = lambda a: a.astype(BF16).astype(F32)
    outs = []
    for h in range(A_HEADS):
        cols = slice(h * A_DV, (h + 1) * A_DV)
        fcol = xt[:, 3 * h:3 * h + 1]
        kcol = xt[:, 3 * h + 1:3 * h + 2]
        qcol = xt[:, 3 * h + 2:3 * h + 3]
        s_old = s_ref[0, h]
        s_out_ref[0, h] = fcol * s_old + kcol * v[:, cols]
        o = (jnp.sum(r16(qcol * fcol) * r16(s_old), axis=0, keepdims=True)
             + jnp.sum(qcol * kcol, axis=0, keepdims=True) * v[:, cols])
        on = o * lax.rsqrt(jnp.mean(o * o, axis=-1, keepdims=True) + EPS) * ng
        outs.append(on * _silu(ag[:, cols]))
    ya_ref[0] = jnp.concatenate(outs, axis=1).astype(ya_ref.dtype)


def hgrn_step(z4, lb_logits, norm_g, state):
    b = z4.shape[0]
    w = A_HEADS * A_DK
    return pl.pallas_call(
        _hgrn_step_kernel,
        grid=(b,),
        in_specs=[pl.BlockSpec((1, 1, 4 * w), lambda i: (i, 0, 0)),
                  pl.BlockSpec(lb_logits.shape, lambda i: (0, 0)),
                  pl.BlockSpec((1, A_DV), lambda i: (0, 0)),
                  pl.BlockSpec((1, A_HEADS, A_DK, A_DV), lambda i: (i, 0, 0, 0))],
        out_specs=[pl.BlockSpec((1, 1, w), lambda i: (i, 0, 0)),
                   pl.BlockSpec((1, A_HEADS, A_DK, A_DV), lambda i: (i, 0, 0, 0))],
        out_shape=[jax.ShapeDtypeStruct((b, 1, w), F32),
                   jax.ShapeDtypeStruct(state.shape, F32)],
        compiler_params=_cparams("arbitrary"),
        name="hgrn_step",
    )(z4.reshape(b, 1, 4 * w), lb_logits, norm_g.reshape(1, A_DV), state)


BISECT_MAX_ITERS = 320


def _bisect_threshold(count_ge, lo, hi, cnt_lo, topk):
    kf = float(topk)

    def unfinished(lo, hi, cl):
        mid = 0.5 * lo + 0.5 * hi
        can_split = jnp.logical_and(mid > lo, mid < hi)
        return jnp.max(jnp.where(jnp.logical_and(cl > kf, can_split), 1.0, 0.0))

    def cond(c):
        return jnp.logical_and(c[0] < BISECT_MAX_ITERS, c[-1] > 0.0)

    def body(c):
        it, lo, hi, cl, _ = c
        mid = 0.5 * lo + 0.5 * hi
        cm = count_ge(mid)
        ge = cm >= kf
        lo = jnp.where(ge, mid, lo)
        cl = jnp.where(ge, cm, cl)
        hi = jnp.where(ge, hi, mid)
        return it + 1, lo, hi, cl, unfinished(lo, hi, cl)

    out = lax.while_loop(cond, body, (jnp.int32(0), lo, hi, cnt_lo, unfinished(lo, hi, cnt_lo)))
    return out[1]


DSA_QB = 128
DSA_W = 512
DSA_W3 = 1024
P3_GROUP = 2


def _dsa_prompt_kernel(iq0_ref, iq1_ref, bq_ref, iw_ref, kidx_ref, k_ref, v_ref, bias_ref, o_ref,
                       score_s, qih_s, qs_s, wb_s, m_s, l_s, acc_s, *, topk):
    i = pl.program_id(0)
    nsub = DSA_W // LANES
    nsub3 = DSA_W3 // LANES
    qsub = DSA_QB // LANES
    nch3 = (i * DSA_QB + DSA_QB + DSA_W3 - 1) // DSA_W3
    nch = (i * DSA_QB + DSA_QB + DSA_W - 1) // DSA_W
    qpos = i * DSA_QB + lax.broadcasted_iota(jnp.int32, (DSA_QB, 1), 0)

    iw = iw_ref[...]
    wscale = IDX_DIM ** -0.5 * IDX_HEADS ** -0.5
    for h in range(IDX_HEADS):
        iq_ref, hh = (iq0_ref, h) if h < IDX_HEADS // 2 else (iq1_ref, h - IDX_HEADS // 2)
        qih_s[h] = iq_ref[:, hh * IDX_DIM:(hh + 1) * IDX_DIM].astype(BF16)
        wb_s[h] = jnp.broadcast_to(iw[:, IW_LANE + h:IW_LANE + h + 1] * wscale, (DSA_QB, LANES))
    for h in range(B_HEADS):
        qs_s[h // B_GROUP, (h % B_GROUP) * DSA_QB:(h % B_GROUP + 1) * DSA_QB, :] = (
            bq_ref[:, h * B_HEAD_DIM:(h + 1) * B_HEAD_DIM] * B_HEAD_DIM ** -0.5).astype(BF16)

    def p1(c, carry):
        c0 = pl.multiple_of(c * DSA_W, DSA_W)
        kc = kidx_ref[pl.ds(c0, DSA_W), :]
        sc = [jnp.zeros((DSA_QB, LANES), F32) for _ in range(nsub)]
        for h in range(IDX_HEADS):
            s = jnp.maximum(_nt(qih_s[h], kc), 0.0)
            wb = wb_s[h]
            for j in range(nsub):
                sc[j] = sc[j] + s[:, j * LANES:(j + 1) * LANES] * wb
        for j in range(nsub):
            kpos = c0 + j * LANES + lax.broadcasted_iota(jnp.int32, (1, LANES), 1)
            score_s[c * nsub + j] = jnp.where(kpos <= qpos, sc[j], -jnp.inf)
        return carry

    lax.fori_loop(0, nch // 2, lambda c2, carry: p1(2 * c2 + 1, p1(2 * c2, carry)), 0)
    lax.fori_loop(nch // 2 * 2, nch, p1, 0)

    def fill(tile, carry):
        score_s[tile] = jnp.full((DSA_QB, LANES), -jnp.inf, F32)
        return carry

    lax.fori_loop(nch * nsub, nch3 * nsub3, fill, 0)

    def stats(c, carry):
        mn, mx = carry
        for j in range(nsub):
            s = score_s[c * nsub + j]
            mx = jnp.maximum(mx, s)
            mn = jnp.minimum(mn, jnp.where(s > -jnp.inf, s, jnp.inf))
        return mn, mx

    mn, mx = lax.fori_loop(0, nch, stats, (jnp.full((DSA_QB, LANES), jnp.inf, F32),
                                           jnp.full((DSA_QB, LANES), -jnp.inf, F32)))
    lo0 = jnp.min(mn, axis=1, keepdims=True)
    hi0 = jnp.max(mx, axis=1, keepdims=True)

    def count_ge(thr):
        thr_b = jnp.broadcast_to(thr, (DSA_QB, LANES))

        def body(c, acc):
            for j in range(nsub):
                s = score_s[c * nsub + j]
                acc = acc + jnp.where(s >= thr_b, 1.0, 0.0)
            return acc

        acc = lax.fori_loop(0, nch // 2, lambda c2, a: body(2 * c2 + 1, body(2 * c2, a)),
                            jnp.zeros((DSA_QB, LANES), F32))
        acc = lax.fori_loop(nch // 2 * 2, nch, body, acc)
        return jnp.sum(acc, axis=1, keepdims=True)

    thr = _bisect_threshold(count_ge, lo0, hi0, (qpos + 1).astype(F32), topk)
    thr_b = jnp.broadcast_to(thr, (DSA_QB, LANES))

    m_s[...] = jnp.full(m_s.shape, NEG, F32)
    l_s[...] = jnp.zeros(l_s.shape, F32)
    acc_s[...] = jnp.zeros(acc_s.shape, F32)

    def p3(c, with_bias):
        c0 = pl.multiple_of(c * DSA_W3, DSA_W3)
        madd = jnp.concatenate([jnp.where(score_s[c * nsub3 + j] >= thr_b, 0.0, NEG) for j in range(nsub3)], axis=1)
        kc = k_ref[pl.ds(c0, DSA_W3), :]
        vc = v_ref[pl.ds(c0, DSA_W3), :]
        rel = lambda qs, j: i * qsub + qs - (c * nsub3 + j)

        def bias_tile(h, qs, j):
            return jnp.where(rel(qs, j) == 0, bias_ref[h, 0], jnp.where(rel(qs, j) == 1, bias_ref[h, 1], 0.0))

        def scores(n):
            return _nt(qs_s[n], kc[:, n * B_HEAD_DIM:(n + 1) * B_HEAD_DIM])

        def softmax(n, lg):
            lg = lg.reshape(B_GROUP, DSA_QB, DSA_W3) + madd[None]
            if with_bias:
                lg = lg + jnp.stack([jnp.concatenate(
                    [jnp.concatenate([bias_tile(n * B_GROUP + gq, qs, j) for j in range(nsub3)], axis=1)
                     for qs in range(qsub)], axis=0) for gq in range(B_GROUP)])
            m_old = m_s[n]
            m_new = jnp.maximum(m_old, jnp.max(lg, axis=-1, keepdims=True))
            p = jnp.exp(lg - m_new)
            alpha = jnp.exp(m_old - m_new)
            l_s[n] = alpha * l_s[n] + jnp.sum(p, axis=-1, keepdims=True)
            m_s[n] = m_new
            pv = jnp.dot(p.reshape(B_GROUP * DSA_QB, DSA_W3).astype(BF16), vc[:, n * B_HEAD_DIM:(n + 1) * B_HEAD_DIM],
                         preferred_element_type=F32)
            return alpha, pv.reshape(B_GROUP, DSA_QB, B_HEAD_DIM)

        lgs = [scores(n) for n in range(B_KV_HEADS)]
        outs = [softmax(n, lgs[n]) for n in range(B_KV_HEADS)]
        for n in range(B_KV_HEADS):
            acc_s[n] = outs[n][0] * acc_s[n] + outs[n][1]

    n_far = jnp.maximum(i * qsub - 1, 0) // nsub3
    def far_group(cg, carry):
        for u in range(P3_GROUP):
            p3(P3_GROUP * cg + u, False)
        return carry

    lax.fori_loop(0, n_far // P3_GROUP, far_group, 0)
    lax.fori_loop(n_far // P3_GROUP * P3_GROUP, n_far, lambda c, carry: (p3(c, False), carry)[1], 0)
    lax.fori_loop(n_far, nch3, lambda c, carry: (p3(c, True), carry)[1], 0)

    for h in range(B_HEADS):
        n, gq = h // B_GROUP, h % B_GROUP
        o_ref[:, h * B_HEAD_DIM:(h + 1) * B_HEAD_DIM] = (acc_s[n, gq] / l_s[n, gq]).astype(o_ref.dtype)


def dsa_prompt(z, ztail, kidx_bf, k_bf, v_bf, bias_tiles):
    m = z.shape[0]
    topk = min(TOPK_MAX, m // 4)
    wq = B_HEADS * B_HEAD_DIM
    wi2 = IDX_HEADS * IDX_DIM // 2
    kern = functools.partial(_dsa_prompt_kernel, topk=topk)
    return pl.pallas_call(
        kern,
        grid=(m // DSA_QB,),
        in_specs=[pl.BlockSpec((DSA_QB, wi2), lambda i: (i, C_IQ // wi2)),
                  pl.BlockSpec((DSA_QB, wi2), lambda i: (i, C_IQ // wi2 + 1)),
                  pl.BlockSpec((DSA_QB, wq), lambda i: (i, C_BQ // wq)),
                  pl.BlockSpec((DSA_QB, LANES), lambda i: (i, T_SM // LANES)),
                  pl.BlockSpec(kidx_bf.shape, lambda i: (0, 0)),
                  pl.BlockSpec(k_bf.shape, lambda i: (0, 0)),
                  pl.BlockSpec(v_bf.shape, lambda i: (0, 0)),
                  pl.BlockSpec(bias_tiles.shape, lambda i: (0, 0, 0, 0))],
        out_specs=pl.BlockSpec((DSA_QB, wq), lambda i: (i, 0)),
        out_shape=jax.ShapeDtypeStruct((m, wq), BF16),
        scratch_shapes=[pltpu.VMEM((m // LANES, DSA_QB, LANES), F32),
                        pltpu.VMEM((IDX_HEADS, DSA_QB, IDX_DIM), BF16),
                        pltpu.VMEM((B_KV_HEADS, B_GROUP * DSA_QB, B_HEAD_DIM), BF16),
                        pltpu.VMEM((IDX_HEADS, DSA_QB, LANES), F32),
                        pltpu.VMEM((B_KV_HEADS, B_GROUP, DSA_QB, 1), F32),
                        pltpu.VMEM((B_KV_HEADS, B_GROUP, DSA_QB, 1), F32),
                        pltpu.VMEM((B_KV_HEADS, B_GROUP, DSA_QB, B_HEAD_DIM), F32)],
        compiler_params=_cparams("arbitrary"),
        name="dsa_prompt",
    )(z, z, z, ztail, kidx_bf, k_bf, v_bf, bias_tiles)


def _page_copies(table_ref, b, n_pages, src_hbm, dst, sem, rows_per_page=PAGE_SIZE):
    def copy(p):
        return pltpu.make_async_copy(src_hbm.at[table_ref[b, p]],
                                     dst.at[pl.ds(p * rows_per_page, rows_per_page)], sem)
    return copy


def _dsa_scores_kernel(pt_ref, iq_ref, iw_ref, iknew_ref, kidx_hbm, o_ref, buf, sem, *, n_pages):
    b = pl.program_id(0)
    nb = pl.num_programs(0)
    past = n_pages * PAGE_SIZE

    def page_copy(bb, slot, p):
        return pltpu.make_async_copy(kidx_hbm.at[pt_ref[bb, p]],
                                     buf.at[slot, :, pl.ds(pl.multiple_of(p * PAGE_SIZE, PAGE_SIZE), PAGE_SIZE)],
                                     sem.at[slot])

    def start(bb, slot):
        lax.fori_loop(0, n_pages, lambda p, c: (page_copy(bb, slot, p).start(), c)[1], 0)

    def wait(bb, slot):
        lax.fori_loop(0, n_pages, lambda p, c: (page_copy(bb, slot, p).wait(), c)[1], 0)

    slot = b % 2

    @pl.when(b == 0)
    def _():
        start(0, 0)

    @pl.when(b + 1 < nb)
    def _():
        start(b + 1, 1 - slot)

    wait(b, slot)

    r16 = lambda a: a.astype(BF16).astype(F32)
    qi = iq_ref[0].astype(BF16)
    wcol = r16(iw_ref[0]) * (IDX_DIM ** -0.5 * IDX_HEADS ** -0.5)
    s = r16(jnp.maximum(jnp.dot(qi, buf[slot].astype(BF16), preferred_element_type=F32), 0.0))
    o_ref[0, :, 0:past] = jnp.sum(s * wcol, axis=0, keepdims=True)
    sn = r16(jnp.maximum(jnp.dot(qi, iknew_ref[0].astype(BF16), preferred_element_type=F32), 0.0))
    sn = jnp.sum(sn * wcol, axis=0, keepdims=True)
    lane = lax.broadcasted_iota(jnp.int32, (1, LANES), 1)
    o_ref[0, :, past:past + LANES] = jnp.where(lane == 0, sn, -jnp.inf)


def dsa_scores(page_table, iq, iw, iknew_pad, cache_kidx):
    b, n_pages = page_table.shape
    past = n_pages * PAGE_SIZE
    kern = functools.partial(_dsa_scores_kernel, n_pages=n_pages)
    gs = pltpu.PrefetchScalarGridSpec(
        num_scalar_prefetch=1,
        grid=(b,),
        in_specs=[pl.BlockSpec((1, IDX_HEADS, IDX_DIM), lambda i, pt: (i, 0, 0)),
                  pl.BlockSpec((1, IDX_HEADS, 1), lambda i, pt: (i, 0, 0)),
                  pl.BlockSpec((1, IDX_DIM, LANES), lambda i, pt: (i, 0, 0)),
                  pl.BlockSpec(memory_space=pl.ANY)],
        out_specs=pl.BlockSpec((1, 1, past + LANES), lambda i, pt: (i, 0, 0)),
        scratch_shapes=[pltpu.VMEM((2, IDX_DIM, past), F32), pltpu.SemaphoreType.DMA((2,))],
    )
    return pl.pallas_call(
        kern, grid_spec=gs,
        out_shape=jax.ShapeDtypeStruct((b, 1, past + LANES), F32),
        compiler_params=_cparams("arbitrary"),
        name="dsa_scores",
    )(page_table, iq, iw, iknew_pad, cache_kidx)


def _dsa_threshold_kernel(s_ref, thr_ref, *, topk):
    s = s_ref[...]
    nb = s.shape[0]
    finite = s > -jnp.inf
    lo0 = jnp.min(jnp.where(finite, s, jnp.inf), axis=1, keepdims=True)
    hi0 = jnp.max(s, axis=1, keepdims=True)
    cnt0 = jnp.sum(jnp.where(finite, 1.0, 0.0), axis=1, keepdims=True)

    def count_ge(thr):
        return jnp.sum(jnp.where(s_ref[...] >= thr, 1.0, 0.0), axis=1, keepdims=True)

    thr = _bisect_threshold(count_ge, lo0, hi0, cnt0, topk)
    thr_ref[...] = jnp.broadcast_to(thr, (nb, LANES))


def dsa_threshold(scores, topk):
    b, l = scores.shape
    return pl.pallas_call(
        functools.partial(_dsa_threshold_kernel, topk=topk),
        grid=(1,),
        in_specs=[pl.BlockSpec((b, l), lambda i: (0, 0))],
        out_specs=pl.BlockSpec((b, LANES), lambda i: (0, 0)),
        out_shape=jax.ShapeDtypeStruct((b, LANES), F32),
        compiler_params=_cparams("arbitrary"),
        name="dsa_threshold",
    )(scores)


def _dsa_decode_kernel(pt_ref, q_ref, s_ref, thr_ref, knew_ref, vnew_ref, bias_ref, k_hbm, v_hbm, o_ref,
                       kbuf, vbuf, sem, *, n_pages):
    b = pl.program_id(0)
    nb = pl.num_programs(0)
    past = n_pages * PAGE_SIZE

    rpp = PAGE_SIZE * B_KV_HEADS

    def copies(bb, slot):
        ck = _page_copies(pt_ref, bb, n_pages, k_hbm, kbuf.at[slot], sem.at[0, slot], rpp)
        cv = _page_copies(pt_ref, bb, n_pages, v_hbm, vbuf.at[slot], sem.at[1, slot], rpp)
        return ck, cv

    def start(bb, slot):
        ck, cv = copies(bb, slot)
        for p in range(n_pages):
            ck(p).start()
            cv(p).start(priority=1)

    def wait(bb, slot):
        ck, cv = copies(bb, slot)
        for p in range(n_pages):
            ck(p).wait()
            cv(p).wait()

    slot = b % 2

    @pl.when(b == 0)
    def _():
        kbuf[:, n_pages * rpp:, :] = jnp.zeros((2, rpp, B_HEAD_DIM), F32)
        vbuf[:, n_pages * rpp:, :] = jnp.zeros((2, rpp, B_HEAD_DIM), F32)
        start(0, 0)

    @pl.when(b + 1 < nb)
    def _():
        start(b + 1, 1 - slot)

    kbuf[slot, n_pages * rpp:n_pages * rpp + 8, :] = knew_ref[0]
    vbuf[slot, n_pages * rpp:n_pages * rpp + 8, :] = vnew_ref[0]
    wait(b, slot)

    sel = s_ref[0] >= thr_ref[0][:, 0:1]
    n_keys = past + PAGE_SIZE
    outs = []
    for n in range(B_KV_HEADS):
        kn = kbuf[slot, pl.ds(n, n_keys, stride=B_KV_HEADS), :].astype(BF16)
        vn = vbuf[slot, pl.ds(n, n_keys, stride=B_KV_HEADS), :].astype(BF16)
        qn = q_ref[0, n].astype(BF16)
        lg = _nt(qn, kn) * B_HEAD_DIM ** -0.5 + bias_ref[n]
        m = jnp.max(jnp.where(sel, lg, NEG), axis=1, keepdims=True)
        p = jnp.where(sel, jnp.exp(lg - m), 0.0)
        p = p / jnp.sum(p, axis=1, keepdims=True)
        outs.append(jnp.dot(p.astype(BF16), vn, preferred_element_type=F32))
    o_ref[0] = jnp.concatenate(outs, axis=0).astype(o_ref.dtype)


def dsa_decode(page_table, q8, scores, thr, knew8, vnew8, bias_rows, cache_k2, cache_v2):
    b, n_pages = page_table.shape
    past = n_pages * PAGE_SIZE
    l = past + LANES
    wkv = B_KV_HEADS * B_HEAD_DIM
    kern = functools.partial(_dsa_decode_kernel, n_pages=n_pages)
    gs = pltpu.PrefetchScalarGridSpec(
        num_scalar_prefetch=1,
        grid=(b,),
        in_specs=[pl.BlockSpec((1, B_KV_HEADS, 8, B_HEAD_DIM), lambda i, pt: (i, 0, 0, 0)),
                  pl.BlockSpec((1, 1, l), lambda i, pt: (i, 0, 0)),
                  pl.BlockSpec((1, 1, LANES), lambda i, pt: (i, 0, 0)),
                  pl.BlockSpec((1, 8, B_HEAD_DIM), lambda i, pt: (i, 0, 0)),
                  pl.BlockSpec((1, 8, B_HEAD_DIM), lambda i, pt: (i, 0, 0)),
                  pl.BlockSpec((B_KV_HEADS, 8, l), lambda i, pt: (0, 0, 0)),
                  pl.BlockSpec(memory_space=pl.ANY),
                  pl.BlockSpec(memory_space=pl.ANY)],
        out_specs=pl.BlockSpec((1, 2 * 8, B_HEAD_DIM), lambda i, pt: (i, 0, 0)),
        scratch_shapes=[pltpu.VMEM((2, l * B_KV_HEADS, B_HEAD_DIM), F32),
                        pltpu.VMEM((2, l * B_KV_HEADS, B_HEAD_DIM), F32),
                        pltpu.SemaphoreType.DMA((2, 2))],
    )
    return pl.pallas_call(
        kern, grid_spec=gs,
        out_shape=jax.ShapeDtypeStruct((b, 2 * 8, B_HEAD_DIM), F32),
        compiler_params=_cparams("arbitrary"),
        name="dsa_decode",
    )(page_table, q8, scores, thr, knew8, vnew8, bias_rows, cache_k2, cache_v2)


def _merge_kernel(ya_ref, yb_ref, ga_ref, gb_ref, wa_ref, wb_ref, o_ref):
    a = _mm(ya_ref[...], wa_ref[...])
    bb = _mm(yb_ref[...], wb_ref[...])
    o_ref[...] = (jax.nn.sigmoid(ga_ref[...]) * a + jax.nn.sigmoid(gb_ref[...]) * bb).astype(o_ref.dtype)


def merge(ya, yb, z, ga_col, gb_col, wa, wb, *, tm, tn):
    m, k = ya.shape
    n = wa.shape[1]
    return pl.pallas_call(
        _merge_kernel,
        grid=(m // tm, n // tn),
        in_specs=[pl.BlockSpec((tm, k), lambda i, j: (i, 0)),
                  pl.BlockSpec((tm, k), lambda i, j: (i, 0)),
                  pl.BlockSpec((tm, tn), lambda i, j: (i, ga_col // tn + j)),
                  pl.BlockSpec((tm, tn), lambda i, j: (i, gb_col // tn + j)),
                  pl.BlockSpec((k, tn), lambda i, j: (0, j)),
                  pl.BlockSpec((k, tn), lambda i, j: (0, j))],
        out_specs=pl.BlockSpec((tm, tn), lambda i, j: (i, j)),
        out_shape=jax.ShapeDtypeStruct((m, n), wa.dtype),
        compiler_params=_cparams("parallel", "arbitrary"),
        name="merge",
    )(ya, yb, z, z, wa, wb)


def _cross_prompt_kernel(x_ref, g_ref, wq_ref, mk_ref, mv_ref, wo_ref, o_ref, *, n_in_tiles):
    x = x_ref[...]
    ms = jnp.mean(x * x, axis=-1, keepdims=True)
    h = (x * lax.rsqrt(ms + EPS) * g_ref[...]).astype(BF16)
    q = jnp.dot(h, wq_ref[...], preferred_element_type=F32)
    outs = []
    for hh in range(X_HEADS):
        cols = slice(hh * X_HEAD_DIM, (hh + 1) * X_HEAD_DIM)
        lg = _nt(q[:, cols].astype(BF16), mk_ref[:, cols]) * X_HEAD_DIM ** -0.5
        mx = jnp.max(lg, axis=1, keepdims=True)
        p = jnp.exp(lg - mx)
        p = p / jnp.sum(p, axis=1, keepdims=True)
        outs.append(jnp.dot(p.astype(BF16), mv_ref[:, cols], preferred_element_type=F32).astype(BF16))
    att = jnp.concatenate(outs, axis=1)
    res = x + jnp.dot(att, wo_ref[...], preferred_element_type=F32)
    o_ref[...] = jnp.where(pl.program_id(0) < n_in_tiles, res, 0.0)


def cross_prompt(x, g, wq, mk, mv, wo, *, tm, out_rows=None):
    m, d = x.shape
    out_rows = m if out_rows is None else out_rows
    n_in = m // tm
    full = lambda a: pl.BlockSpec(a.shape, lambda i: (0,) * a.ndim)
    g2 = g.reshape(1, d)
    return pl.pallas_call(
        functools.partial(_cross_prompt_kernel, n_in_tiles=n_in),
        grid=(pl.cdiv(out_rows, tm),),
        in_specs=[pl.BlockSpec((tm, d), lambda i: (jnp.minimum(i, n_in - 1), 0)),
                  full(g2), full(wq), full(mk), full(mv), full(wo)],
        out_specs=pl.BlockSpec((tm, d), lambda i: (i, 0)),
        out_shape=jax.ShapeDtypeStruct((out_rows, d), F32),
        compiler_params=_cparams("parallel"),
        name="cross_prompt",
    )(x, g2, wq, mk, mv, wo)


def _cross_step_kernel(q_ref, mk_ref, mv_ref, o_ref):
    r16 = lambda a: a.astype(BF16).astype(F32)
    q = r16(q_ref[0])
    outs = []
    for hh in range(X_HEADS):
        cols = slice(hh * X_HEAD_DIM, (hh + 1) * X_HEAD_DIM)
        kh = r16(mk_ref[0, :, cols])
        vh = r16(mv_ref[0, :, cols])
        lg = jnp.sum(kh * q[:, cols], axis=1, keepdims=True) * X_HEAD_DIM ** -0.5
        mx = jnp.max(lg, axis=0, keepdims=True)
        p = jnp.exp(lg - mx)
        p = r16(p / jnp.sum(p, axis=0, keepdims=True))
        outs.append(jnp.sum(p * vh, axis=0, keepdims=True))
    o_ref[0] = jnp.concatenate(outs, axis=1).astype(o_ref.dtype)


def cross_step(q, mk, mv):
    b, w = q.shape
    mem = mk.shape[1]
    return pl.pallas_call(
        _cross_step_kernel,
        grid=(b,),
        in_specs=[pl.BlockSpec((1, 1, w), lambda i: (i, 0, 0)),
                  pl.BlockSpec((1, mem, w), lambda i: (i, 0, 0)),
                  pl.BlockSpec((1, mem, w), lambda i: (i, 0, 0))],
        out_specs=pl.BlockSpec((1, 1, w), lambda i: (i, 0, 0)),
        out_shape=jax.ShapeDtypeStruct((b, 1, w), F32),
        compiler_params=_cparams("arbitrary"),
        name="cross_step",
    )(q.reshape(b, 1, w), mk, mv)


def _pack_bf16_pairs(x):
    c = x.shape[1] // 2
    u = pltpu.bitcast(x.astype(BF16).astype(F32), jnp.uint32)
    return lax.shift_right_logical(u[:, :c], jnp.uint32(16)) | u[:, c:]


def _unpack_bf16_pairs(u):
    lo = pltpu.bitcast(lax.shift_left(u, jnp.uint32(16)), F32)
    hi = pltpu.bitcast(u & jnp.uint32(0xFFFF0000), F32)
    return jnp.concatenate([lo, hi], axis=1)


def _router_kernel(x_ref, g_ref, w_ref, b_ref, hf_ref, route_ref):
    x = x_ref[...]
    ms = jnp.mean(x * x, axis=-1, keepdims=True)
    hf = x * lax.rsqrt(ms + EPS) * g_ref[...]
    hf_ref[...] = _pack_bf16_pairs(hf)
    lg = _mm(hf, w_ref[...]) + b_ref[...]
    tm = lg.shape[0]
    lane = lax.broadcasted_iota(jnp.int32, (tm, LANES), 1)
    big = jnp.int32(LANES)
    is_g = lane < N_GROUPS
    gmax = jnp.max(jnp.where(is_g, lg, -jnp.inf), axis=1, keepdims=True)
    grp = jnp.min(jnp.where(is_g & (lg == gmax), lane, big), axis=1, keepdims=True)
    p_grp = 1.0 / jnp.sum(jnp.where(is_g, jnp.exp(lg - gmax), 0.0), axis=1, keepdims=True)
    e_lo = N_GROUPS + grp * EXP_PER_GROUP
    in_g = (lane >= e_lo) & (lane < e_lo + EXP_PER_GROUP)
    v1 = jnp.max(jnp.where(in_g, lg, -jnp.inf), axis=1, keepdims=True)
    i1 = jnp.min(jnp.where(in_g & (lg == v1), lane, big), axis=1, keepdims=True)
    rest = in_g & (lane != i1)
    v2 = jnp.max(jnp.where(rest, lg, -jnp.inf), axis=1, keepdims=True)
    i2 = jnp.min(jnp.where(rest & (lg == v2), lane, big), axis=1, keepdims=True)
    e2 = jnp.exp(v2 - v1)
    g1 = p_grp / (1.0 + e2)
    g2 = p_grp * e2 / (1.0 + e2)
    r = jnp.where(lane == 0, (i1 - N_GROUPS).astype(F32),
                  jnp.where(lane == 1, (i2 - N_GROUPS).astype(F32),
                            jnp.where(lane == 2, g1, jnp.where(lane == 3, g2, 0.0))))
    route_ref[...] = r


def router(x, g, w_pad, b_pad, *, tm):
    m, d = x.shape
    return pl.pallas_call(
        _router_kernel,
        grid=(pl.cdiv(m, tm),),
        in_specs=[pl.BlockSpec((tm, d), lambda i: (i, 0)),
                  pl.BlockSpec((1, d), lambda i: (0, 0)),
                  pl.BlockSpec((d, LANES), lambda i: (0, 0)),
                  pl.BlockSpec((1, LANES), lambda i: (0, 0))],
        out_specs=[pl.BlockSpec((tm, d // 2), lambda i: (i, 0)),
                   pl.BlockSpec((tm, LANES), lambda i: (i, 0))],
        out_shape=[jax.ShapeDtypeStruct((m, d // 2), jnp.uint32), jax.ShapeDtypeStruct((m, LANES), F32)],
        compiler_params=_cparams("parallel"),
        name="router",
    )(x, g.reshape(1, d), w_pad, b_pad)


def _moe_kernel(be_ref, nxt_ref, nused_ref, sbase_ref, nvalid_ref, padj_ref, order_ref, hf_hbm, wg_hbm, wu_hbm,
                wd_hbm, y_hbm, xbuf, obuf, wg_f, wu_f, wd_f, wg_s, wu_s, wd_s, sem_in, sem_out, sem_w,
                *, n_tokens):
    b = pl.program_id(0)
    nb = pl.num_programs(0)
    blk = xbuf.shape[1]
    xslot = b % 3
    used = b < nused_ref[0]
    next_used = b + 2 < nused_ref[0]

    def slot_info(bb):
        base, nv, pad0 = sbase_ref[bb], nvalid_ref[bb], padj_ref[bb]

        def info(r):
            asg = order_ref[base + r]
            tok = lax.shift_right_logical(asg, 1)
            return tok, jnp.where(r < nv, (asg & 1) * n_tokens + tok, pad0 + r)
        return info

    def weight_copies(e):
        return (pltpu.make_async_copy(wg_hbm.at[e], wg_f, sem_w.at[0]),
                pltpu.make_async_copy(wu_hbm.at[e], wu_f, sem_w.at[1]),
                pltpu.make_async_copy(wd_hbm.at[e], wd_f, sem_w.at[2]))

    def gather_start(bb, sl):
        info = slot_info(bb)
        for r in range(blk):
            pltpu.make_async_copy(hf_hbm.at[pl.ds(info(r)[0], 1)], xbuf.at[sl, pl.ds(r, 1)],
                                  sem_in.at[sl]).start()

    def scatter_start(bb, sl):
        info = slot_info(bb)
        for r in range(blk):
            pltpu.make_async_copy(obuf.at[sl, pl.ds(r, 1)], y_hbm.at[pl.ds(info(r)[1], 1)],
                                  sem_out.at[sl]).start(priority=r % 2)

    def gather_wait(sl):
        pltpu.make_async_copy(hf_hbm.at[pl.ds(0, blk)], xbuf.at[sl], sem_in.at[sl]).wait()

    def scatter_wait(sl):
        pltpu.make_async_copy(obuf.at[sl], y_hbm.at[pl.ds(0, blk)], sem_out.at[sl]).wait()

    @pl.when(b == 0)
    def _():
        for cp in weight_copies(be_ref[0]):
            cp.start(priority=1)
        gather_start(0, 0)
        obuf[...] = jnp.zeros(obuf.shape, obuf.dtype)

    @pl.when(jnp.logical_and(b == 0, 1 < nused_ref[0]))
    def _():
        gather_start(1, 1)

    changed = jnp.logical_and(used, jnp.logical_or(b == 0, be_ref[b] != be_ref[jnp.maximum(b - 1, 0)]))

    @pl.when(changed)
    def _():
        for cp in weight_copies(be_ref[b]):
            cp.wait()
        wg_s[...] = wg_f[...].astype(BF16)
        wu_s[...] = wu_f[...].astype(BF16)
        wd_s[...] = wd_f[...].astype(BF16)

    @pl.when(jnp.logical_and(changed, nxt_ref[b] >= 0))
    def _():
        for cp in weight_copies(nxt_ref[b]):
            cp.start(priority=1)

    @pl.when(used)
    def _():
        gather_wait(xslot)

    @pl.when(b >= 3)
    def _():
        scatter_wait(xslot)

    def step(prefetch, flush_prev, compute):
        if prefetch:
            gather_start(b + 2, (b + 2) % 3)
        if flush_prev:
            scatter_start(b - 1, (b + 2) % 3)
        if compute:
            x = _unpack_bf16_pairs(xbuf[xslot]).astype(BF16)
            gg = jnp.dot(x, wg_s[...], preferred_element_type=F32)
            uu = jnp.dot(x, wu_s[...], preferred_element_type=F32)
            a = (_silu(gg) * uu).astype(BF16)
            obuf[xslot] = _pack_bf16_pairs(jnp.dot(a, wd_s[...], preferred_element_type=F32))

    first, last = b == 0, b == nb - 1
    land, lnot = jnp.logical_and, jnp.logical_not
    pl.when(land(first, next_used))(lambda: step(True, False, True))
    pl.when(land(first, lnot(next_used)))(lambda: step(False, False, True))
    pl.when(land(lnot(first), land(used, next_used)))(lambda: step(True, True, True))
    pl.when(land(lnot(first), land(used, lnot(next_used))))(lambda: step(False, True, True))
    pl.when(land(lnot(first), lnot(used)))(lambda: step(False, True, False))

    @pl.when(last)
    def _():
        scatter_start(b, xslot)
        scatter_wait(xslot)

    @pl.when(jnp.logical_and(last, b >= 1))
    def _():
        scatter_wait((b + 2) % 3)

    @pl.when(jnp.logical_and(last, b >= 2))
    def _():
        scatter_wait((b + 1) % 3)


def moe_experts(tables, hf, w_g, w_u, w_d, *, blk, out_rows):
    n_blocks = tables[0].shape[0]
    dp = hf.shape[1]
    d = 2 * dp
    ff = w_g.shape[2]
    gs = pltpu.PrefetchScalarGridSpec(
        num_scalar_prefetch=len(tables),
        grid=(n_blocks,),
        in_specs=[pl.BlockSpec(memory_space=pl.ANY)] * 4,
        out_specs=pl.BlockSpec(memory_space=pl.ANY),
        scratch_shapes=[pltpu.VMEM((3, blk, dp), jnp.uint32), pltpu.VMEM((3, blk, dp), jnp.uint32),
                        pltpu.VMEM((d, ff), F32), pltpu.VMEM((d, ff), F32), pltpu.VMEM((ff, d), F32),
                        pltpu.VMEM((d, ff), BF16), pltpu.VMEM((d, ff), BF16), pltpu.VMEM((ff, d), BF16),
                        pltpu.SemaphoreType.DMA((3,)), pltpu.SemaphoreType.DMA((3,)),
                        pltpu.SemaphoreType.DMA((3,))],
    )
    return pl.pallas_call(
        functools.partial(_moe_kernel, n_tokens=hf.shape[0]), grid_spec=gs,
        out_shape=jax.ShapeDtypeStruct((out_rows, dp), jnp.uint32),
        compiler_params=_cparams("arbitrary"),
        name="moe_experts",
    )(*tables, hf, w_g, w_u, w_d)


def _combine_kernel(x_ref, route_ref, gf_ref, y1_ref, y2_ref, op_ref, os_ref):
    i = pl.program_id(0)
    route = route_ref[...]
    x = (x_ref[...] + route[:, 2:3] * _unpack_bf16_pairs(y1_ref[...])
         + route[:, 3:4] * _unpack_bf16_pairs(y2_ref[...]))
    ms = jnp.mean(x * x, axis=-1, keepdims=True)
    out = x * lax.rsqrt(ms + EPS) * gf_ref[...]

    @pl.when(i < pl.num_programs(0) - 1)
    def _():
        op_ref[...] = out

    @pl.when(i == pl.num_programs(0) - 1)
    def _():
        os_ref[...] = out[:os_ref.shape[0]]


def combine(x, route, gf, y, plane, n_prompt, *, tm):
    m, d = x.shape
    n_tiles = n_prompt // tm
    assert n_prompt % tm == 0 and 0 < m - n_prompt <= tm
    return pl.pallas_call(
        _combine_kernel,
        grid=(n_tiles + 1,),
        in_specs=[pl.BlockSpec((tm, d), lambda i: (i, 0)),
                  pl.BlockSpec((tm, LANES), lambda i: (i, 0)),
                  pl.BlockSpec((1, d), lambda i: (0, 0)),
                  pl.BlockSpec((tm, d // 2), lambda i: (i, 0)),
                  pl.BlockSpec((tm, d // 2), lambda i: (plane // tm + i, 0))],
        out_specs=[pl.BlockSpec((tm, d), lambda i: (jnp.minimum(i, n_tiles - 1), 0)),
                   pl.BlockSpec((m - n_prompt, d), lambda i: (0, 0))],
        out_shape=[jax.ShapeDtypeStruct((n_prompt, d), F32), jax.ShapeDtypeStruct((m - n_prompt, d), F32)],
        compiler_params=_cparams("arbitrary"),
        name="combine",
    )(x, route, gf.reshape(1, d), y, y)


def _t5_bucket(dist):
    dist = jnp.asarray(dist, jnp.int32)
    max_exact = REL_BUCKETS // 2
    dist_f = jnp.maximum(dist, 1).astype(F32)
    large = max_exact + (jnp.log(dist_f / max_exact) / math.log(REL_MAX_DIST / max_exact)
                         * (REL_BUCKETS - max_exact)).astype(jnp.int32)
    large = jnp.minimum(large, REL_BUCKETS - 1)
    return jnp.where(dist < max_exact, dist, large)


def _bias_tables(rel_bias, past):
    r = np.arange(LANES)
    diff = r[:, None] - r[None, :]
    buckets = jnp.stack([_t5_bucket(np.maximum(diff, 0)),
                         _t5_bucket(np.maximum(diff + LANES, 0)),
                         _t5_bucket(np.full((LANES, LANES), 2 * LANES))])
    def lookup(bkt):
        oh = (bkt.reshape(-1, 1) == jnp.arange(REL_BUCKETS)[None, :]).astype(F32)
        out = jnp.dot(oh, rel_bias.astype(F32), precision=lax.Precision.HIGHEST)
        return out.T.reshape((rel_bias.shape[1],) + bkt.shape)

    tiles = lookup(buckets)
    tiles = tiles - tiles[:, 2:3]
    dist = np.maximum(past - np.arange(past + LANES), 0)
    rows = lookup(_t5_bucket(dist))
    rows = rows.reshape(B_KV_HEADS, B_GROUP, past + LANES)
    rows = jnp.concatenate([rows, jnp.zeros_like(rows)], axis=1)
    return tiles, rows


def _dispatch(eid, n_tokens, blk):
    a = eid.shape[0]
    assert EXPERT_TOPK == 2 and a == EXPERT_TOPK * n_tokens
    n_blocks = -(-(a + N_EXPERTS * (blk - 1)) // blk)
    rows = n_blocks * blk
    order = jnp.argsort(eid).astype(jnp.int32)
    counts = jnp.sum(eid[:, None] == jnp.arange(N_EXPERTS)[None, :], axis=0).astype(jnp.int32)
    cum = jnp.cumsum(counts)
    starts = cum - counts
    padded = (counts + blk - 1) // blk * blk
    pad_end = jnp.cumsum(padded)
    pad_start = pad_end - padded
    blocks = jnp.arange(n_blocks, dtype=jnp.int32)
    block_e = jnp.minimum(jnp.sum(pad_end[None, :] <= (blocks * blk)[:, None], axis=1), N_EXPERTS - 1)
    n_used = pad_end[-1] // blk
    off = blocks * blk - pad_start[block_e]
    sbase = jnp.where(blocks < n_used, starts[block_e] + off, 0)
    nvalid = jnp.where(blocks < n_used, jnp.clip(counts[block_e] - off, 0, blk), 0)
    padj = a + blocks * blk - cum[block_e]
    new_run = jnp.concatenate([jnp.array([True]), block_e[1:] != block_e[:-1]])
    run_start = jnp.where(new_run & (blocks < n_used), blocks, n_blocks)
    nxt = jnp.concatenate([jnp.flip(lax.cummin(jnp.flip(run_start)))[1:], jnp.array([n_blocks])])
    next_e = jnp.where(nxt < n_blocks, block_e[jnp.minimum(nxt, n_blocks - 1)], -1)
    i32 = lambda v: v.astype(jnp.int32)
    order_pad = jnp.pad(order, (0, blk))
    return (i32(block_e), i32(next_e), i32(n_used).reshape(1), i32(sbase), i32(nvalid), i32(padj), order_pad), rows


def kernel(x_prompt, x_sample, mem_prompt, cache_k, cache_v, cache_kidx, page_table, state_hgrn, cache_mem_k,
           cache_mem_v, norm_mix, w_in, hgrn_lb_logits, hgrn_norm, w_branch_a, w_branch_b, w_out, norm_cross, w_xq,
           w_xk, w_xv, w_xo, norm_ffn, w_router_group, b_router_group, w_router_expert, b_router_expert, w_exp_gate,
           w_exp_up, w_exp_down, rel_bias, norm_final):
    assert w_in.shape[0] == 1, "single-layer step"
    l = 0
    drop0 = lambda a: a.reshape(a.shape[1:])
    bp, t, d = x_prompt.shape
    db = x_sample.shape[0]
    past = page_table.shape[1] * PAGE_SIZE
    xp = x_prompt.reshape(bp * t, d)
    xs = x_sample.reshape(db, d)

    wi_t = jnp.transpose(drop0(w_in)).astype(BF16)
    assert wi_t.shape[0] - NZ_MAIN + T_PAD == NZ_TAIL
    w_tail_t = jnp.pad(wi_t[NZ_MAIN:], ((T_PAD, 0), (0, 0)))
    wa, wb, wo = w_branch_a[l].astype(BF16), w_branch_b[l].astype(BF16), w_out[l].astype(BF16)
    wxq, wxk, wxv, wxo = (w_xq[l].astype(BF16), w_xk[l].astype(BF16), w_xv[l].astype(BF16), w_xo[l].astype(BF16))
    w_route = jnp.pad(jnp.concatenate([w_router_group[l], w_router_expert[l]], axis=1),
                      ((0, 0), (0, LANES - N_GROUPS - N_EXPERTS))).astype(BF16)
    b_route = jnp.pad(jnp.concatenate([b_router_group[l], b_router_expert[l]]),
                      (0, LANES - N_GROUPS - N_EXPERTS)).reshape(1, LANES)
    bias_tiles, bias_rows = _bias_tables(rel_bias, past)

    zp, zpt = in_proj(xp, norm_mix[l], wi_t, NZ_MAIN, w_tail_t, tm=TM_WIDE, tn=TN)
    kp = zp[:, C_BK:C_BK + KV_W]
    vp = zp[:, C_BV:C_BV + KV_W]
    ikp = zpt[:, T_SM + IK_LANE:T_SM + IK_LANE + IDX_DIM]
    ya_p, st_p = hgrn_prompt(zp, hgrn_lb_logits, hgrn_norm[l])
    yb_p = dsa_prompt(zp, zpt, ikp.astype(BF16), kp.astype(BF16), vp.astype(BF16), bias_tiles)
    mg_p = merge(ya_p, yb_p, zpt, T_GA, T_GB, wa, wb, tm=TM_WIDE, tn=TN)
    x1p = matmul(mg_p, wo, xp, tm=TM_WIDE, tn=TN)
    memp = mem_prompt.reshape(-1, d)
    mk = matmul(memp, wxk, tm=memp.shape[0], tn=TN)
    mv = matmul(memp, wxv, tm=memp.shape[0], tn=TN)
    n = -(-(bp * t + db) // TM_TOKEN) * TM_TOKEN
    x2p = cross_prompt(x1p, norm_cross[l], wxq, mk.astype(BF16), mv.astype(BF16), wxo, tm=TM_CROSS, out_rows=n)

    zs, zst = in_proj(xs, norm_mix[l], wi_t, NZ_MAIN, w_tail_t, tm=db, tn=TN)
    ks = zs[:, C_BK:C_BK + KV_W]
    vs = zs[:, C_BV:C_BV + KV_W]
    iks = zst[:, T_SM + IK_LANE:T_SM + IK_LANE + IDX_DIM]
    ya_s, st_s = hgrn_step(zs[:, :4 * A_HEADS * A_DK], hgrn_lb_logits, hgrn_norm[l], drop0(state_hgrn))
    iq_s = zs[:, C_IQ:C_IQ + IDX_HEADS * IDX_DIM].reshape(db, IDX_HEADS, IDX_DIM)
    iw_s = zst[:, T_SM + IW_LANE:T_SM + IW_LANE + IDX_HEADS].reshape(db, IDX_HEADS, 1)
    iknew_pad = jnp.pad(iks[:, :, None], ((0, 0), (0, 0), (0, LANES - 1)))
    scores = dsa_scores(page_table, iq_s, iw_s, iknew_pad,
                        jnp.swapaxes(drop0(cache_kidx), 1, 2)).reshape(db, past + LANES)
    topk_s = min(TOPK_MAX, (past + 1) // 4)
    thr = dsa_threshold(scores, topk_s)
    q8 = jnp.pad(zs[:, C_BQ:C_BQ + B_HEADS * B_HEAD_DIM].reshape(db, B_KV_HEADS, B_GROUP, B_HEAD_DIM),
                 ((0, 0), (0, 0), (0, 8 - B_GROUP), (0, 0)))
    knew8 = jnp.pad(ks.reshape(db, B_KV_HEADS, B_HEAD_DIM), ((0, 0), (0, 8 - B_KV_HEADS), (0, 0)))
    vnew8 = jnp.pad(vs.reshape(db, B_KV_HEADS, B_HEAD_DIM), ((0, 0), (0, 8 - B_KV_HEADS), (0, 0)))
    n_pool = cache_k.shape[1]
    ob = dsa_decode(page_table, q8, scores.reshape(db, 1, -1), thr.reshape(db, 1, LANES), knew8, vnew8, bias_rows,
                    cache_k.reshape(n_pool, PAGE_SIZE * B_KV_HEADS, B_HEAD_DIM),
                    cache_v.reshape(n_pool, PAGE_SIZE * B_KV_HEADS, B_HEAD_DIM))
    yb_s = ob.reshape(db, B_KV_HEADS, 8, B_HEAD_DIM)[:, :, :B_GROUP].reshape(db, B_HEADS * B_HEAD_DIM)
    mg_s = merge(ya_s.reshape(db, -1), yb_s, zst, T_GA, T_GB, wa, wb, tm=db, tn=TN)
    x1s = matmul(mg_s, wo, xs, tm=db, tn=TN)
    qx_s = norm_matmul(x1s, norm_cross[l], wxq, tm=db, tn=TN)
    mem = cache_mem_k.shape[2]
    att_s = cross_step(qx_s, cache_mem_k.reshape(db, mem, -1), cache_mem_v.reshape(db, mem, -1))
    x2 = matmul(att_s.reshape(db, -1), wxo, x1s, tm=db, tn=TN, into=(x2p, bp * t))

    hf, route = router(x2, norm_ffn[l], w_route, b_route, tm=TM_ROUTER)
    eid = route[:, :EXPERT_TOPK].astype(jnp.int32).reshape(-1)
    tables, rows = _dispatch(eid, n, MOE_ROWS)
    ye = moe_experts(tables, hf, drop0(w_exp_gate), drop0(w_exp_up), drop0(w_exp_down), blk=MOE_ROWS, out_rows=rows)
    y_p, y_s = combine(x2, route, norm_final, ye, n, bp * t, tm=TM_TOKEN)

    y_prompt = y_p.reshape(bp, t, d)
    y_sample = y_s[:db].reshape(db, 1, d)
    return (y_prompt, y_sample,
            kp.reshape(1, bp, t, B_KV_HEADS, B_HEAD_DIM), vp.reshape(1, bp, t, B_KV_HEADS, B_HEAD_DIM),
            ikp.reshape(1, bp, t, IDX_DIM),
            jnp.swapaxes(st_p, 1, 2).reshape(1, bp, A_HEADS, A_DK, A_DV),
            mk.reshape(1, bp, -1, X_HEADS, X_HEAD_DIM), mv.reshape(1, bp, -1, X_HEADS, X_HEAD_DIM),
            ks.reshape(1, db, 1, B_KV_HEADS, B_HEAD_DIM), vs.reshape(1, db, 1, B_KV_HEADS, B_HEAD_DIM),
            iks.reshape(1, db, 1, IDX_DIM),
            st_s.reshape(1, db, A_HEADS, A_DK, A_DV))
```
